```python
import math
import jax, jax.numpy as jnp
from jax import lax
import numpy as np

D_MODEL = 1024
BATCH = 8
SEQ = 4096
DEPTH = 2

EPS = 1e-6
NEG = -1e30
MIX_W = 512
HA = 4
DA = 64
A_OUT = HA * 2 * DA
QBLK = 128
WINDOWS = (128, 512, 2048)
DILATIONS = (1, 4, 16)
NG_B = 3
HB = 8
DB = 64
B_OUT = HB * DB
BLK_B = 64
CHUNK = 128
C_GROUPS = 4
C_WIDTH = 512
C_GDIM = C_WIDTH // C_GROUPS
N_BRANCH = 3
N_BUCKETS = 32
MAX_DIST = 128
N_BIAS_HEADS = HA + NG_B * HB
N_GROUPS = 4
E_PER_GROUP = 4
N_EXPERTS = N_GROUPS * E_PER_GROUP
TOP_K = 2
D_FF_EXPERT = 512
A_COLS = 3 * A_OUT
B_COLS = 3 * NG_B * HB * DB
C_COLS = 2 * C_WIDTH
G_COLS = N_BRANCH * D_MODEL
D_IN = A_COLS + B_COLS + C_COLS + G_COLS
SPLITS = (A_OUT, 2 * A_OUT, A_COLS, A_COLS + B_COLS, A_COLS + B_COLS + C_COLS)

kernel_name = "hybrid_diffattn_dilated_gmlp_hiermoe"


def rms_norm(x, g):
    xf = x.astype(jnp.float32)
    y = xf * lax.rsqrt(jnp.mean(xf * xf, axis=-1, keepdims=True) + EPS)
    return (y * g.astype(jnp.float32)).astype(x.dtype)


def layer_norm(x, g, b):
    xf = x.astype(jnp.float32)
    mu = jnp.mean(xf, axis=-1, keepdims=True)
    var = jnp.mean(jnp.square(xf - mu), axis=-1, keepdims=True)
    y = (xf - mu) * lax.rsqrt(var + EPS)
    return (y * g.astype(jnp.float32) + b.astype(jnp.float32)).astype(x.dtype)


def t5_bucket(rel):
    nb = N_BUCKETS // 2
    max_exact = nb // 2
    ret = (rel > 0).astype(jnp.int32) * nb
    n = jnp.abs(rel)
    nf = jnp.maximum(n, 1).astype(jnp.float32)
    large = max_exact + (jnp.log(nf / max_exact) / math.log(MAX_DIST / max_exact)
                         * (nb - max_exact)).astype(jnp.int32)
    large = jnp.minimum(large, nb - 1)
    return ret + jnp.where(n < max_exact, n, large)


def diff_attention(q, k, v, lam, lam_init, subln_g, bias_tab):
    B, S = q.shape[0], q.shape[1]
    nblk = S // QBLK
    scale = DA ** -0.5
    qb = q.reshape(B, nblk, QBLK, HA, 2, DA).transpose(1, 0, 2, 3, 4, 5)
    kpos = jnp.arange(S, dtype=jnp.int32)

    def one_block(args):
        qblk, start = args
        qpos = start + jnp.arange(QBLK, dtype=jnp.int32)
        bias = bias_tab.astype(jnp.float32)[t5_bucket(kpos[None, :] - qpos[:, None])]
        logits = jnp.einsum('bqhcd,bkhcd->bhcqk', qblk, k).astype(jnp.float32) * scale
        logits = logits + bias.transpose(2, 0, 1)[None, :, None]
        p = jax.nn.softmax(logits, axis=-1)
        attn = p[:, :, 0] - lam * p[:, :, 1]
        return jnp.einsum('bhqk,bkhe->bqhe', attn.astype(v.dtype), v)

    out = lax.map(one_block, (qb, jnp.arange(nblk, dtype=jnp.int32) * QBLK))
    out = out.transpose(1, 0, 2, 3, 4).reshape(B, S, HA, 2 * DA)
    out = rms_norm(out, subln_g) * (1.0 - lam_init)
    return out.reshape(B, S, A_OUT)


def dilated_group_attention(q, k, v, bias_tab, window, r):
    B, S, H, d = q.shape
    L = S // r
    nb = -(-L // BLK_B)
    Lp = nb * BLK_B
    half = window // (2 * r)

    def to_sub(t):
        t = t.reshape(B, L, r, H, d).transpose(0, 2, 1, 3, 4)
        return jnp.pad(t, ((0, 0), (0, 0), (0, Lp - L), (0, 0), (0, 0)))

    def band(t):
        t = jnp.pad(t, ((0, 0), (0, 0), (BLK_B, BLK_B), (0, 0), (0, 0)))
        t = t.reshape(B, r, nb + 2, BLK_B, H, d)
        return jnp.concatenate([t[:, :, :-2], t[:, :, 1:-1], t[:, :, 2:]], axis=3)

    qs = to_sub(q).reshape(B, r, nb, BLK_B, H, d)
    kb = band(to_sub(k))
    vb = band(to_sub(v))
    logits = jnp.einsum('brnqhd,brnkhd->brnhqk', qs, kb).astype(jnp.float32) * (d ** -0.5)
    qi = jnp.arange(BLK_B, dtype=jnp.int32)[:, None]
    ki = jnp.arange(3 * BLK_B, dtype=jnp.int32)[None, :] - BLK_B
    rel = ki - qi
    bias = bias_tab.astype(jnp.float32)[t5_bucket(rel * r)].transpose(2, 0, 1)
    kidx = jnp.arange(nb, dtype=jnp.int32)[:, None, None] * BLK_B + ki[None]
    valid = (jnp.abs(rel) <= half)[None] & (kidx >= 0) & (kidx < L)
    logits = jnp.where(valid[None, None, :, None], logits + bias, NEG)
    lse = jax.nn.logsumexp(logits, axis=-1)
    p = jnp.exp(logits - lse[..., None])
    out = jnp.einsum('brnhqk,brnkhd->brnqhd', p.astype(v.dtype), vb)
    out = out.reshape(B, r, Lp, H, d)[:, :, :L].transpose(0, 2, 1, 3, 4).reshape(B, S, H, d)
    lse = lse.transpose(0, 1, 2, 4, 3).reshape(B, r, Lp, H)[:, :, :L]
    lse = lse.transpose(0, 2, 1, 3).reshape(B, S, H)
    return out, lse


def dilated_attention(qkv, rel_bias):
    B, S = qkv.shape[0], qkv.shape[1]
    qkv = qkv.reshape(B, S, 3, NG_B, HB, DB)
    outs, lses = [], []
    for g in range(NG_B):
        tab = rel_bias[:, HA + g * HB: HA + (g + 1) * HB]
        o, l = dilated_group_attention(qkv[:, :, 0, g], qkv[:, :, 1, g], qkv[:, :, 2, g],
                                       tab, WINDOWS[g], DILATIONS[g])
        outs.append(o)
        lses.append(l)
    w = jax.nn.softmax(jnp.stack(lses, axis=0), axis=0)
    y = jnp.sum(w[..., None].astype(qkv.dtype) * jnp.stack(outs, axis=0), axis=0)
    return y.reshape(B, S, B_OUT)


def chunked_sgu(z, ln_g, ln_b, w_s, b_s):
    B, S = z.shape[0], z.shape[1]
    z = jax.nn.gelu(z)
    u, v = z[..., :C_WIDTH], z[..., C_WIDTH:]
    v = layer_norm(v, ln_g, ln_b)
    vc = v.reshape(B, S // CHUNK, CHUNK, C_GROUPS, C_GDIM)
    mixed = jnp.einsum('gpq,bnqgc->bnpgc', w_s, vc) + b_s.T[None, None, :, :, None]
    return u * mixed.reshape(B, S, C_WIDTH)


def hier_moe(h, w_rg, b_rg, w_re, b_re, w_gate, w_up, w_down):
    B, S, D = h.shape
    t = h.reshape(B * S, D)
    T = t.shape[0]
    grp_logits = (t @ w_rg).astype(jnp.float32) + b_rg.astype(jnp.float32)
    grp_prob = jax.nn.softmax(grp_logits, axis=-1)
    g_idx = jnp.argmax(grp_logits, axis=-1)
    g_w = jnp.max(grp_prob, axis=-1, keepdims=True)
    exp_logits = jnp.einsum('td,gde->tge', t, w_re).astype(jnp.float32) + b_re.astype(jnp.float32)
    sel = exp_logits[jnp.arange(T), g_idx]
    top_v, top_i = lax.top_k(sel, TOP_K)
    top_w = jax.nn.softmax(top_v, axis=-1) * g_w
    ex_idx = g_idx[:, None] * E_PER_GROUP + top_i
    combine = jnp.sum(jax.nn.one_hot(ex_idx, N_EXPERTS, dtype=jnp.float32) * top_w[..., None], axis=1)
    combine = combine.astype(t.dtype)
    out = jnp.zeros_like(t)
    for e in range(N_EXPERTS):
        hid = jax.nn.silu(t @ w_gate[e]) * (t @ w_up[e])
        out = out + combine[:, e:e + 1] * (hid @ w_down[e])
    return out.reshape(B, S, D)


def setup_inputs(seed: int = 0) -> dict:
    key = jax.random.key(seed)
    ks = jax.random.split(key, 24)

    def nrm(k, shape, scale):
        return jax.random.normal(k, shape, jnp.float32) * scale

    return {
        "x": nrm(ks[0], (BATCH, SEQ, D_MODEL), 1.0),
        "rel_bias": nrm(ks[1], (N_BUCKETS, N_BIAS_HEADS), 0.2),
        "norm_mix": 1.0 + nrm(ks[2], (DEPTH, D_MODEL), 0.02),
        "w_in": nrm(ks[3], (DEPTH, D_MODEL, D_IN), D_MODEL ** -0.5),
        "diff_lambda": nrm(ks[4], (DEPTH, 4, DA), 0.1),
        "diff_subln": 1.0 + nrm(ks[5], (DEPTH, 2 * DA), 0.02),
        "sgu_ln_g": 1.0 + nrm(ks[6], (DEPTH, C_WIDTH), 0.02),
        "sgu_ln_b": nrm(ks[7], (DEPTH, C_WIDTH), 0.02),
        "sgu_w": nrm(ks[8], (DEPTH, C_GROUPS, CHUNK, CHUNK), CHUNK ** -0.5),
        "sgu_b": 1.0 + nrm(ks[9], (DEPTH, C_GROUPS, CHUNK), 0.02),
        "w_branch": nrm(ks[10], (DEPTH, N_BRANCH, MIX_W, D_MODEL), MIX_W ** -0.5),
        "w_out": nrm(ks[11], (DEPTH, D_MODEL, D_MODEL), D_MODEL ** -0.5),
        "norm_ffn": 1.0 + nrm(ks[12], (DEPTH, D_MODEL), 0.02),
        "w_router_grp": nrm(ks[13], (DEPTH, D_MODEL, N_GROUPS), D_MODEL ** -0.5),
        "b_router_grp": nrm(ks[14], (DEPTH, N_GROUPS), 0.01),
        "w_router_exp": nrm(ks[15], (DEPTH, N_GROUPS, D_MODEL, E_PER_GROUP), D_MODEL ** -0.5),
        "b_router_exp": nrm(ks[16], (DEPTH, N_GROUPS, E_PER_GROUP), 0.01),
        "w_gate": nrm(ks[17], (DEPTH, N_EXPERTS, D_MODEL, D_FF_EXPERT), D_MODEL ** -0.5),
        "w_up": nrm(ks[18], (DEPTH, N_EXPERTS, D_MODEL, D_FF_EXPERT), D_MODEL ** -0.5),
        "w_down": nrm(ks[19], (DEPTH, N_EXPERTS, D_FF_EXPERT, D_MODEL), D_FF_EXPERT ** -0.5),
        "norm_final": 1.0 + nrm(ks[20], (D_MODEL,), 0.02),
    }


def reference(x, rel_bias, norm_mix, w_in, diff_lambda, diff_subln, sgu_ln_g, sgu_ln_b,
              sgu_w, sgu_b, w_branch, w_out, norm_ffn, w_router_grp, b_router_grp,
              w_router_exp, b_router_exp, w_gate, w_up, w_down, norm_final):
    B, S, D = x.shape
    for i in range(DEPTH):
        h = rms_norm(x, norm_mix[i])
        proj = h @ w_in[i]
        qa, ka, va, qkv_b, zc, gates = jnp.split(proj, SPLITS, axis=-1)
        lam_init = 0.8 - 0.6 * math.exp(-0.3 * i)
        lp = diff_lambda[i].astype(jnp.float32)
        lam = jnp.exp(jnp.sum(lp[0] * lp[1])) - jnp.exp(jnp.sum(lp[2] * lp[3])) + lam_init
        ya = diff_attention(qa.reshape(B, S, HA, 2, DA), ka.reshape(B, S, HA, 2, DA),
                            va.reshape(B, S, HA, 2 * DA), lam, lam_init, diff_subln[i],
                            rel_bias[:, :HA])
        yb = dilated_attention(qkv_b, rel_bias)
        yc = chunked_sgu(zc, sgu_ln_g[i], sgu_ln_b[i], sgu_w[i], sgu_b[i])
        branches = jnp.stack([ya, yb, yc], axis=2)
        g = jax.nn.sigmoid(gates.reshape(B, S, N_BRANCH, D))
        merged = jnp.sum(g * jnp.einsum('bsgi,gid->bsgd', branches, w_branch[i]), axis=2)
        x = x + merged @ w_out[i]
        h = rms_norm(x, norm_ffn[i])
        x = x + hier_moe(h, w_router_grp[i], b_router_grp[i], w_router_exp[i], b_router_exp[i],
                         w_gate[i], w_up[i], w_down[i])
    return rms_norm(x, norm_final)
```

```python
import functools
import math

import jax
import jax.numpy as jnp
from jax import lax
from jax.experimental import pallas as pl
from jax.experimental.pallas import tpu as pltpu

F32 = jnp.float32
BF16 = jnp.bfloat16

EPS = 1e-6
NEG = -1e30
LANES = 128
HALF_LANES = LANES // 2
VMEM_LIMIT = 48 * 1024 * 1024

HA = 4
DA = 64
MIX_W = 512
WINDOWS = (128, 512, 2048)
DILATIONS = (1, 4, 16)
NG_B = 3
HB = 8
DB = 64
HALF_WIN = 64
CHUNK = 128
C_GROUPS = 4
N_BRANCH = 3
N_BUCKETS = 32
MAX_DIST = 128
N_GROUPS = 4
E_PER_GROUP = 4
N_EXPERTS = N_GROUPS * E_PER_GROUP

TM_PROJ = 1024
TN_PROJ = 1024
T_ATT = 512
QB_DIL = 128
KW_DIL = QB_DIL + 2 * HALF_WIN
TM_SGU = 512
TM_MIX = 512
TM_MOE = 1024


def _cparams(sem):
    return pltpu.CompilerParams(dimension_semantics=sem, vmem_limit_bytes=VMEM_LIMIT)


def _t5_bucket(rel):
    nb = N_BUCKETS // 2
    max_exact = nb // 2
    ret = (rel > 0).astype(jnp.int32) * nb
    n = jnp.abs(rel)
    nf = jnp.maximum(n, 1).astype(F32)
    large = max_exact + (jnp.log(nf / max_exact) / math.log(MAX_DIST / max_exact)
                         * (nb - max_exact)).astype(jnp.int32)
    large = jnp.minimum(large, nb - 1)
    return ret + jnp.where(n < max_exact, n, large)


def _inproj_kernel(x_ref, g_ref, w_ref, o_ref, h_scr):
    @pl.when(pl.program_id(1) == 0)
    def _():
        x = x_ref[...]
        ms = jnp.mean(x * x, axis=-1, keepdims=True)
        h_scr[...] = (x * lax.rsqrt(ms + EPS) * g_ref[...]).astype(BF16)

    o_ref[...] = jnp.dot(h_scr[...], w_ref[...], preferred_element_type=F32).astype(o_ref.dtype)


def _inproj(x2, g, w):
    T, D = x2.shape
    N = w.shape[1]
    tm = min(TM_PROJ, T)
    return pl.pallas_call(
        _inproj_kernel,
        out_shape=jax.ShapeDtypeStruct((T, N), BF16),
        grid=(T // tm, N // TN_PROJ),
        in_specs=[pl.BlockSpec((tm, D), lambda i, j: (i, 0)),
                  pl.BlockSpec((1, D), lambda i, j: (0, 0)),
                  pl.BlockSpec((D, TN_PROJ), lambda i, j: (0, j))],
        out_specs=pl.BlockSpec((tm, TN_PROJ), lambda i, j: (i, j)),
        scratch_shapes=[pltpu.VMEM((tm, D), BF16)],
        compiler_params=_cparams(("parallel", "arbitrary")),
        name="inproj",
    )(x2, g, w)


def _attn_a_kernel(lam_ref, q_ref, k_ref, v_ref, bias_ref, g_ref, o_ref, m_scr, l_scr, acc_scr,
                   *, out_scale):
    ki = pl.program_id(3)

    @pl.when(ki == 0)
    def _():
        m_scr[...] = jnp.full(m_scr.shape, NEG, F32)
        l_scr[...] = jnp.zeros(l_scr.shape, F32)
        acc_scr[...] = jnp.zeros(acc_scr.shape, F32)

    q = q_ref[0]
    k = k_ref[0]
    v = v_ref[0]
    bias = bias_ref[0, 0]
    low_half = lax.broadcasted_iota(jnp.int32, (1, LANES), 1) < HALF_LANES
    for c in range(2):
        qc = jnp.where(low_half if c == 0 else jnp.logical_not(low_half), q, jnp.zeros_like(q))
        s = lax.dot_general(qc, k, (((1,), (1,)), ((), ())), preferred_element_type=F32) + bias
        m_prev = m_scr[c]
        m_new = jnp.maximum(m_prev, jnp.max(s, axis=-1, keepdims=True))
        alpha = jnp.exp(m_prev - m_new)
        p = jnp.exp(s - m_new)
        l_scr[c] = alpha * l_scr[c] + jnp.sum(p, axis=-1, keepdims=True)
        acc_scr[c] = alpha * acc_scr[c] + jnp.dot(p.astype(BF16), v, preferred_element_type=F32)
        m_scr[c] = m_new

    @pl.when(ki == pl.num_programs(3) - 1)
    def _():
        o = acc_scr[0] / l_scr[0] - lam_ref[0] * (acc_scr[1] / l_scr[1])
        ms = jnp.mean(o * o, axis=-1, keepdims=True)
        o_ref[0] = (o * lax.rsqrt(ms + EPS) * g_ref[...] * out_scale).astype(o_ref.dtype)


def _attn_a(proj3, lam, bias5, subln_g, lam_init):
    B, S, _ = proj3.shape
    t = T_ATT
    nq = S // t
    kern = functools.partial(_attn_a_kernel, out_scale=1.0 - lam_init)
    return pl.pallas_call(
        kern,
        out_shape=jax.ShapeDtypeStruct((B, S, HA * 2 * DA), BF16),
        grid=(B, HA, nq, nq),
        in_specs=[
            pl.BlockSpec(memory_space=pltpu.SMEM),
            pl.BlockSpec((1, t, LANES), lambda b, h, qi, ki: (b, qi, h)),
            pl.BlockSpec((1, t, LANES), lambda b, h, qi, ki: (b, ki, HA + h)),
            pl.BlockSpec((1, t, LANES), lambda b, h, qi, ki: (b, ki, 2 * HA + h)),
            pl.BlockSpec((1, 1, t, t), lambda b, h, qi, ki: (h, jnp.clip(ki - qi, -2, 2) + 2, 0, 0)),
            pl.BlockSpec((1, LANES), lambda b, h, qi, ki: (0, 0)),
        ],
        out_specs=pl.BlockSpec((1, t, LANES), lambda b, h, qi, ki: (b, qi, h)),
        scratch_shapes=[pltpu.VMEM((2, t, 1), F32), pltpu.VMEM((2, t, 1), F32),
                        pltpu.VMEM((2, t, LANES), F32)],
        compiler_params=_cparams(("parallel", "parallel", "parallel", "arbitrary")),
        name="diff_attn",
    )(lam, proj3, proj3, proj3, bias5, subln_g)


def _attn_a_bias(rel_bias):
    t = T_ATT
    d = jnp.arange(-2, 3, dtype=jnp.int32)[:, None, None] * t
    rel = d + jnp.arange(t, dtype=jnp.int32)[None, None, :] - jnp.arange(t, dtype=jnp.int32)[None, :, None]
    tiles = rel_bias[:, :HA].astype(F32)[_t5_bucket(rel)]
    return tiles.transpose(3, 0, 1, 2)


def _attn_b_kernel(q_ref, k_ref, v_ref, bias_ref, o_ref, lse_ref, *, sub_len):
    i = pl.program_id(2)
    start = jnp.clip(i * QB_DIL - HALF_WIN, 0, sub_len - KW_DIL)
    start = pl.multiple_of(start, HALF_WIN)
    q = q_ref[0]
    kw = k_ref[0, pl.ds(start, KW_DIL), :]
    vw = v_ref[0, pl.ds(start, KW_DIL), :]
    low_half = lax.broadcasted_iota(jnp.int32, (1, LANES), 1) < HALF_LANES
    for j in range(HB // 2):
        cols = slice(j * LANES, (j + 1) * LANES)
        qp, kp, vp = q[:, cols], kw[:, cols], vw[:, cols]
        outs, lses = [], []
        for c in range(2):
            qc = jnp.where(low_half if c == 0 else jnp.logical_not(low_half), qp, jnp.zeros_like(qp))
            s = lax.dot_general(qc, kp, (((1,), (1,)), ((), ())), preferred_element_type=F32)
            s = s + bias_ref[0, 2 * j + c]
            m = jnp.max(s, axis=-1, keepdims=True)
            p = jnp.exp(s - m)
            l = jnp.sum(p, axis=-1, keepdims=True)
            outs.append(jnp.dot(p.astype(BF16), vp, preferred_element_type=F32) / l)
            lses.append(m + jnp.log(l))
        o_ref[0, :, cols] = jnp.where(low_half, outs[0], outs[1]).astype(o_ref.dtype)
        lse_ref[0, :, cols] = jnp.where(low_half, lses[0], lses[1])


def _attn_b(proj3, bias3, g, n_col_blocks):
    B, S, d_in = proj3.shape
    r = DILATIONS[g]
    L = S // r
    width = HB * DB
    pv = proj3.reshape(B, L, r * d_in)
    qcol, kcol, vcol = 3 + g, 3 + NG_B + g, 3 + 2 * NG_B + g
    kern = functools.partial(_attn_b_kernel, sub_len=L)
    out, lse = pl.pallas_call(
        kern,
        out_shape=[jax.ShapeDtypeStruct((B, L, r * width), BF16),
                   jax.ShapeDtypeStruct((B, L, r * width), F32)],
        grid=(B, r, L // QB_DIL),
        in_specs=[
            pl.BlockSpec((1, QB_DIL, width), lambda b, s, i: (b, i, s * n_col_blocks + qcol)),
            pl.BlockSpec((1, L, width), lambda b, s, i: (b, 0, s * n_col_blocks + kcol)),
            pl.BlockSpec((1, L, width), lambda b, s, i: (b, 0, s * n_col_blocks + vcol)),
            pl.BlockSpec((1, HB, QB_DIL, KW_DIL),
                         lambda b, s, i: (jnp.where(i == 0, 0, jnp.where(i == L // QB_DIL - 1, 2, 1)), 0, 0, 0)),
        ],
        out_specs=[pl.BlockSpec((1, QB_DIL, width), lambda b, s, i: (b, i, s)),
                   pl.BlockSpec((1, QB_DIL, width), lambda b, s, i: (b, i, s))],
        compiler_params=_cparams(("parallel", "parallel", "arbitrary")),
        name=f"dilated_attn_{g}",
    )(pv, pv, pv, bias3)
    return out.reshape(B, S, width), lse.reshape(B, S, width)


def _attn_b_bias(rel_bias, g):
    r = DILATIONS[g]
    tab = rel_bias[:, HA + g * HB: HA + (g + 1) * HB].astype(F32)
    off = jnp.arange(3, dtype=jnp.int32)[:, None, None] * HALF_WIN
    rel = (jnp.arange(KW_DIL, dtype=jnp.int32)[None, None, :] - off
           - jnp.arange(QB_DIL, dtype=jnp.int32)[None, :, None])
    bias = tab[_t5_bucket(rel * r)]
    bias = jnp.where((jnp.abs(rel) <= HALF_WIN)[..., None], bias, NEG)
    return bias.transpose(0, 3, 1, 2)


def _sgu_kernel(zu_ref, zv_ref, lng_ref, lnb_ref, ws_ref, bs_ref, o_ref):
    u = jax.nn.gelu(zu_ref[...].astype(F32))
    v = jax.nn.gelu(zv_ref[...].astype(F32))
    mu = jnp.mean(v, axis=-1, keepdims=True)
    var = jnp.mean(jnp.square(v - mu), axis=-1, keepdims=True)
    v = ((v - mu) * lax.rsqrt(var + EPS) * lng_ref[...] + lnb_ref[...]).astype(BF16)
    gd = v.shape[1] // C_GROUPS
    for n in range(v.shape[0] // CHUNK):
        rows = slice(n * CHUNK, (n + 1) * CHUNK)
        for g in range(C_GROUPS):
            cols = slice(g * gd, (g + 1) * gd)
            mixed = jnp.dot(ws_ref[g], v[rows, cols], preferred_element_type=F32) + bs_ref[:, cols]
            o_ref[rows, cols] = (u[rows, cols] * mixed).astype(o_ref.dtype)


def _sgu(proj2, ln_g, ln_b, w_s, b_exp, ucol):
    T = proj2.shape[0]
    tm = min(TM_SGU, T)
    w = MIX_W
    return pl.pallas_call(
        _sgu_kernel,
        out_shape=jax.ShapeDtypeStruct((T, w), BF16),
        grid=(T // tm,),
        in_specs=[pl.BlockSpec((tm, w), lambda i: (i, ucol)),
                  pl.BlockSpec((tm, w), lambda i: (i, ucol + 1)),
                  pl.BlockSpec((1, w), lambda i: (0, 0)),
                  pl.BlockSpec((1, w), lambda i: (0, 0)),
                  pl.BlockSpec((C_GROUPS, CHUNK, CHUNK), lambda i: (0, 0, 0)),
                  pl.BlockSpec((CHUNK, w), lambda i: (0, 0))],
        out_specs=pl.BlockSpec((tm, w), lambda i: (i, 0)),
        compiler_params=_cparams(("parallel",)),
        name="sgu",
    )(proj2, proj2, ln_g, ln_b, w_s, b_exp)


def _route(logits):
    lane = lax.broadcasted_iota(jnp.int32, logits.shape, 1)
    big = jnp.int32(LANES)
    is_grp = (lane >= N_EXPERTS) & (lane < N_EXPERTS + N_GROUPS)
    gl = jnp.where(is_grp, logits, NEG)
    gmax = jnp.max(gl, axis=-1, keepdims=True)
    g_idx = jnp.min(jnp.where(is_grp & (gl == gmax), lane, big), axis=-1, keepdims=True) - N_EXPERTS
    g_w = 1.0 / jnp.sum(jnp.where(is_grp, jnp.exp(gl - gmax), 0.0), axis=-1, keepdims=True)
    in_grp = (lane >= g_idx * E_PER_GROUP) & (lane < (g_idx + 1) * E_PER_GROUP)
    sel = jnp.where(in_grp, logits, NEG)
    v1 = jnp.max(sel, axis=-1, keepdims=True)
    i1 = jnp.min(jnp.where(in_grp & (sel == v1), lane, big), axis=-1, keepdims=True)
    rest = in_grp & (lane != i1)
    sel2 = jnp.where(rest, logits, NEG)
    v2 = jnp.max(sel2, axis=-1, keepdims=True)
    i2 = jnp.min(jnp.where(rest & (sel2 == v2), lane, big), axis=-1, keepdims=True)
    e2 = jnp.exp(v2 - v1)
    w1 = g_w / (1.0 + e2)
    w2 = g_w * e2 / (1.0 + e2)
    return jnp.where(lane == i1, w1, jnp.where(lane == i2, w2, 0.0))


def _mix_kernel(x_ref, ya_ref, ob0_ref, ob1_ref, ob2_ref, ls0_ref, ls1_ref, ls2_ref, yc_ref,
                g0_ref, g1_ref, g2_ref, wb_ref, wo_ref, nf_ref, wr_ref, br_ref,
                xo_ref, h_ref, comb_ref):
    ls0, ls1, ls2 = ls0_ref[...], ls1_ref[...], ls2_ref[...]
    mx = jnp.maximum(jnp.maximum(ls0, ls1), ls2)
    e0, e1, e2 = jnp.exp(ls0 - mx), jnp.exp(ls1 - mx), jnp.exp(ls2 - mx)
    yb = (e0 * ob0_ref[...].astype(F32) + e1 * ob1_ref[...].astype(F32)
          + e2 * ob2_ref[...].astype(F32)) / (e0 + e1 + e2)
    merged = jax.nn.sigmoid(g0_ref[...].astype(F32)) * jnp.dot(ya_ref[...], wb_ref[0],
                                                               preferred_element_type=F32)
    merged += jax.nn.sigmoid(g1_ref[...].astype(F32)) * jnp.dot(yb.astype(BF16), wb_ref[1],
                                                                preferred_element_type=F32)
    merged += jax.nn.sigmoid(g2_ref[...].astype(F32)) * jnp.dot(yc_ref[...], wb_ref[2],
                                                                preferred_element_type=F32)
    xn = x_ref[...] + jnp.dot(merged.astype(BF16), wo_ref[...], preferred_element_type=F32)
    xo_ref[...] = xn
    ms = jnp.mean(xn * xn, axis=-1, keepdims=True)
    h = (xn * lax.rsqrt(ms + EPS) * nf_ref[...]).astype(BF16)
    h_ref[...] = h
    logits = jnp.dot(h, wr_ref[...], preferred_element_type=F32) + br_ref[...]
    comb_ref[...] = _route(logits)


def _mix(x2, ya, obs, lses, yc, proj2, gcol, wb, wo, nf, wr, br):
    T, D = x2.shape
    tm = min(TM_MIX, T)
    w = MIX_W
    row = lambda width: pl.BlockSpec((tm, width), lambda i: (i, 0))
    full = lambda a: pl.BlockSpec(a.shape, lambda i: (0,) * a.ndim)
    gate = lambda n: pl.BlockSpec((tm, D), lambda i: (i, gcol + n))
    return pl.pallas_call(
        _mix_kernel,
        out_shape=[jax.ShapeDtypeStruct((T, D), F32), jax.ShapeDtypeStruct((T, D), BF16),
                   jax.ShapeDtypeStruct((T, LANES), F32)],
        grid=(T // tm,),
        in_specs=[row(D), row(w), row(w), row(w), row(w), row(w), row(w), row(w), row(w),
                  gate(0), gate(1), gate(2), full(wb), full(wo), full(nf), full(wr), full(br)],
        out_specs=[row(D), row(D), row(LANES)],
        compiler_params=_cparams(("parallel",)),
        name="mix",
    )(x2, ya, obs[0], obs[1], obs[2], lses[0], lses[1], lses[2], yc, proj2, proj2, proj2,
      wb, wo, nf, wr, br)


def _moe_kernel(h_ref, comb_ref, x_ref, wg_ref, wu_ref, wd_ref, nfin_ref, o_ref, acc_scr,
                *, final_norm):
    e = pl.program_id(1)

    @pl.when(e == 0)
    def _():
        acc_scr[...] = jnp.zeros(acc_scr.shape, F32)

    h = h_ref[...]
    lane = lax.broadcasted_iota(jnp.int32, comb_ref.shape, 1)
    c = jnp.sum(jnp.where(lane == e, comb_ref[...], 0.0), axis=-1, keepdims=True)
    hid = (jax.nn.silu(jnp.dot(h, wg_ref[0], preferred_element_type=F32))
           * jnp.dot(h, wu_ref[0], preferred_element_type=F32))
    acc_scr[...] += c * jnp.dot(hid.astype(BF16), wd_ref[0], preferred_element_type=F32)

    @pl.when(e == pl.num_programs(1) - 1)
    def _():
        xn = x_ref[...] + acc_scr[...]
        if final_norm:
            ms = jnp.mean(xn * xn, axis=-1, keepdims=True)
            xn = xn * lax.rsqrt(ms + EPS) * nfin_ref[...]
        o_ref[...] = xn


def _moe(h, comb, x2, wg, wu, wd, nfin, final_norm):
    T, D = x2.shape
    tm = min(TM_MOE, T)
    F = wg.shape[2]
    kern = functools.partial(_moe_kernel, final_norm=final_norm)
    return pl.pallas_call(
        kern,
        out_shape=jax.ShapeDtypeStruct((T, D), F32),
        grid=(T // tm, N_EXPERTS),
        in_specs=[pl.BlockSpec((tm, D), lambda i, e: (i, 0)),
                  pl.BlockSpec((tm, LANES), lambda i, e: (i, 0)),
                  pl.BlockSpec((tm, D), lambda i, e: (i, 0)),
                  pl.BlockSpec((1, D, F), lambda i, e: (e, 0, 0)),
                  pl.BlockSpec((1, D, F), lambda i, e: (e, 0, 0)),
                  pl.BlockSpec((1, F, D), lambda i, e: (e, 0, 0)),
                  pl.BlockSpec((1, D), lambda i, e: (0, 0))],
        out_specs=pl.BlockSpec((tm, D), lambda i, e: (i, 0)),
        scratch_shapes=[pltpu.VMEM((tm, D), F32)],
        compiler_params=_cparams(("parallel", "arbitrary")),
        name="moe",
    )(h, comb, x2, wg, wu, wd, nfin)


def kernel(x, rel_bias, norm_mix, w_in, diff_lambda, diff_subln, sgu_ln_g, sgu_ln_b, sgu_w, sgu_b,
           w_branch, w_out, norm_ffn, w_router_grp, b_router_grp, w_router_exp, b_router_exp,
           w_gate, w_up, w_down, norm_final):
    B, S, D = x.shape
    T = B * S
    depth = w_in.shape[0]
    d_in = w_in.shape[2]
    a_out = HA * 2 * DA
    b_cols = 3 * NG_B * HB * DB
    ucol = (3 * a_out + b_cols) // MIX_W
    gcol = (3 * a_out + b_cols + 2 * MIX_W) // D
    n_col_blocks = d_in // MIX_W

    col = jnp.arange(d_in)
    is_q = (col < a_out) | ((col >= 3 * a_out) & (col < 3 * a_out + NG_B * HB * DB))
    col_scale = jnp.where(is_q, DA ** -0.5, 1.0).astype(F32)

    bias_a = _attn_a_bias(rel_bias)
    bias_b = [_attn_b_bias(rel_bias, g) for g in range(NG_B)]

    x2 = x.reshape(T, D)
    for i in range(depth):
        w = (w_in[i] * col_scale).astype(BF16)
        proj2 = _inproj(x2, norm_mix[i][None, :], w)
        proj3 = proj2.reshape(B, S, d_in)

        lam_init = 0.8 - 0.6 * math.exp(-0.3 * i)
        lp = diff_lambda[i].astype(F32)
        lam = jnp.exp(jnp.sum(lp[0] * lp[1])) - jnp.exp(jnp.sum(lp[2] * lp[3])) + lam_init
        ya = _attn_a(proj3, lam.reshape(1), bias_a, diff_subln[i][None, :], lam_init)

        obs, lses = [], []
        for g in range(NG_B):
            o, l = _attn_b(proj3, bias_b[g], g, n_col_blocks)
            obs.append(o.reshape(T, MIX_W))
            lses.append(l.reshape(T, MIX_W))

        b_exp = jnp.repeat(sgu_b[i].T, MIX_W // C_GROUPS, axis=1)
        yc = _sgu(proj2, sgu_ln_g[i][None, :], sgu_ln_b[i][None, :], sgu_w[i].astype(BF16),
                  b_exp, ucol)

        wr = jnp.concatenate([w_router_exp[i].transpose(1, 0, 2).reshape(D, N_EXPERTS),
                              w_router_grp[i]], axis=1)
        wr = jnp.pad(wr, ((0, 0), (0, LANES - wr.shape[1]))).astype(BF16)
        br = jnp.concatenate([b_router_exp[i].reshape(N_EXPERTS), b_router_grp[i]])
        br = jnp.pad(br, (0, LANES - br.shape[0]))[None, :].astype(F32)

        x2, h, comb = _mix(x2, ya.reshape(T, a_out), obs, lses, yc, proj2, gcol,
                           w_branch[i].astype(BF16), w_out[i].astype(BF16), norm_ffn[i][None, :],
                           wr, br)
        x2 = _moe(h, comb, x2, w_gate[i].astype(BF16), w_up[i].astype(BF16),
                  w_down[i].astype(BF16), norm_final[None, :], i == depth - 1)
    return x2.reshape(B, S, D)
```

```python
import functools
import math

import jax
import jax.numpy as jnp
from jax import lax
from jax.experimental import pallas as pl
from jax.experimental.pallas import tpu as pltpu

F32 = jnp.float32
BF16 = jnp.bfloat16

EPS = 1e-6
NEG = -1e30
LOG2E = 1.4426950408889634
LANES = 128
HALF_LANES = LANES // 2
VMEM_LIMIT = 48 * 1024 * 1024

HA = 4
DA = 64
MIX_W = 512
WINDOWS = (128, 512, 2048)
DILATIONS = (1, 4, 16)
NG_B = 3
HB = 8
DB = 64
HALF_WIN = 64
CHUNK = 128
C_GROUPS = 4
N_BRANCH = 3
N_BUCKETS = 32
MAX_DIST = 128
N_GROUPS = 4
E_PER_GROUP = 4
N_EXPERTS = N_GROUPS * E_PER_GROUP
N_SLABS = MIX_W // LANES

TM_PROJ = 1024
TN_PROJ = 512
TM_PERM = 512
T_ATT = 512
QB_DIL = 128
KW_DIL = QB_DIL + 2 * HALF_WIN
TM_SGU = 512
TM_MIX = 512
TM_MOE = 1024

COL_ZU = 2
COL_GATE = 2
COL_QKV0 = 10


def _cparams(sem):
    return pltpu.CompilerParams(dimension_semantics=sem, vmem_limit_bytes=VMEM_LIMIT)


def _t5_bucket(rel):
    nb = N_BUCKETS // 2
    max_exact = nb // 2
    ret = (rel > 0).astype(jnp.int32) * nb
    n = jnp.abs(rel)
    nf = jnp.maximum(n, 1).astype(F32)
    large = max_exact + (jnp.log(nf / max_exact) / math.log(MAX_DIST / max_exact)
                         * (nb - max_exact)).astype(jnp.int32)
    large = jnp.minimum(large, nb - 1)
    return ret + jnp.where(n < max_exact, n, large)


def _bias_lookup(bucket, tab):
    out = jnp.zeros((tab.shape[1],) + bucket.shape, F32)
    expand = (slice(None),) + (None,) * bucket.ndim
    for b in range(N_BUCKETS):
        out = jnp.where(bucket[None] == b, tab[b][expand], out)
    return out


def _rms_bf16(x, g):
    ms = jnp.mean(x * x, axis=-1, keepdims=True)
    return (x * lax.rsqrt(ms + EPS) * g).astype(BF16)


def _inproj_kernel(x_ref, g_ref, w_ref, o_ref, h_scr):
    @pl.when(pl.program_id(1) == 0)
    def _():
        h_scr[...] = _rms_bf16(x_ref[...], g_ref[...])

    o_ref[...] = jnp.dot(h_scr[...], w_ref[...], preferred_element_type=F32).astype(o_ref.dtype)


def _inproj(x2, g, w):
    T, D = x2.shape
    N = w.shape[1]
    tm = min(TM_PROJ, T)
    return pl.pallas_call(
        _inproj_kernel,
        out_shape=jax.ShapeDtypeStruct((T, N), BF16),
        grid=(T // tm, N // TN_PROJ),
        in_specs=[pl.BlockSpec((tm, D), lambda i, j: (i, 0)),
                  pl.BlockSpec((1, D), lambda i, j: (0, 0)),
                  pl.BlockSpec((D, TN_PROJ), lambda i, j: (0, j))],
        out_specs=pl.BlockSpec((tm, TN_PROJ), lambda i, j: (i, j)),
        scratch_shapes=[pltpu.VMEM((tm, D), BF16)],
        compiler_params=_cparams(("parallel", "arbitrary")),
        name="inproj",
    )(x2, g, w)


def _inproj_t_kernel(x_ref, g_ref, wt_ref, o_ref):
    h = _rms_bf16(x_ref[0], g_ref[...])
    o_ref[0] = lax.dot_general(wt_ref[...], h, (((1,), (1,)), ((), ())),
                               preferred_element_type=F32).astype(o_ref.dtype)


def _inproj_t(x3, g, wt):
    B, S, D = x3.shape
    N = wt.shape[0]
    tm = min(TM_PROJ, S)
    return pl.pallas_call(
        _inproj_t_kernel,
        out_shape=jax.ShapeDtypeStruct((B, N, S), BF16),
        grid=(B, S // tm),
        in_specs=[pl.BlockSpec((1, tm, D), lambda b, i: (b, i, 0)),
                  pl.BlockSpec((1, D), lambda b, i: (0, 0)),
                  pl.BlockSpec((N, D), lambda b, i: (0, 0))],
        out_specs=pl.BlockSpec((1, N, tm), lambda b, i: (b, 0, i)),
        compiler_params=_cparams(("parallel", "parallel")),
        name="inproj_t",
    )(x3, g, wt)


def _inproj_perm_kernel(x_ref, g_ref, p_ref, w_ref, o_ref, *, r):
    h = _rms_bf16(x_ref[0], g_ref[...])
    hp = jnp.dot(p_ref[...], h, preferred_element_type=F32).astype(BF16)
    res = jnp.dot(hp, w_ref[...], preferred_element_type=F32).astype(o_ref.dtype)
    n = res.shape[0] // r
    for s in range(r):
        o_ref[0, s] = res[s * n:(s + 1) * n, :]


def _inproj_perm(x3, g, w, r):
    B, S, D = x3.shape
    N = w.shape[1]
    tm = min(TM_PERM, S)
    n = tm // r
    o = jnp.arange(tm, dtype=jnp.int32)
    src = (o % n) * r + o // n
    perm = (src[:, None] == jnp.arange(tm, dtype=jnp.int32)[None, :]).astype(BF16)
    kern = functools.partial(_inproj_perm_kernel, r=r)
    return pl.pallas_call(
        kern,
        out_shape=jax.ShapeDtypeStruct((B, r, S // r, N), BF16),
        grid=(B, S // tm),
        in_specs=[pl.BlockSpec((1, tm, D), lambda b, i: (b, i, 0)),
                  pl.BlockSpec((1, D), lambda b, i: (0, 0)),
                  pl.BlockSpec((tm, tm), lambda b, i: (0, 0)),
                  pl.BlockSpec((D, N), lambda b, i: (0, 0))],
        out_specs=pl.BlockSpec((1, r, n, N), lambda b, i: (b, 0, i, 0)),
        compiler_params=_cparams(("parallel", "parallel")),
        name=f"inproj_perm_{r}",
    )(x3, g, perm, w)


def _attn_a_kernel(lam_ref, cfar_ref, q_ref, k_ref, vt_ref, bias_ref, g_ref, o_ref,
                   m_scr, l_scr, acc_scr, *, out_scale):
    h, qi, ki = pl.program_id(1), pl.program_id(2), pl.program_id(3)

    @pl.when(ki == 0)
    def _():
        m_scr[...] = jnp.full(m_scr.shape, NEG, F32)
        l_scr[...] = jnp.zeros(l_scr.shape, F32)
        acc_scr[...] = jnp.zeros(acc_scr.shape, F32)

    def update(tile_bias, const):
        q = q_ref[0]
        k = k_ref[0]
        vt = vt_ref[0]
        low_half = lax.broadcasted_iota(jnp.int32, (1, LANES), 1) < HALF_LANES
        for c in range(2):
            kc = jnp.where(low_half if c == 0 else jnp.logical_not(low_half), k, jnp.zeros_like(k))
            st = lax.dot_general(kc, q, (((1,), (1,)), ((), ())), preferred_element_type=F32)
            if tile_bias:
                st = st + bias_ref[0, 0]
            m_prev = m_scr[c]
            m_new = jnp.maximum(m_prev, jnp.max(st, axis=0, keepdims=True) + const)
            alpha = jnp.exp2(m_prev - m_new)
            p = jnp.exp2(st - (m_new - const))
            l_scr[c] = alpha * l_scr[c] + jnp.sum(p, axis=0, keepdims=True)
            acc_scr[c] = alpha * acc_scr[c] + jnp.dot(vt, p.astype(BF16),
                                                      preferred_element_type=F32)
            m_scr[c] = m_new

    near = jnp.abs(ki - qi) <= 1

    @pl.when(near)
    def _():
        update(True, 0.0)

    @pl.when(jnp.logical_not(near))
    def _():
        update(False, cfar_ref[2 * h + (ki > qi).astype(jnp.int32)])

    @pl.when(ki == pl.num_programs(3) - 1)
    def _():
        ot = acc_scr[0] / l_scr[0] - lam_ref[0] * (acc_scr[1] / l_scr[1])
        o = ot.T
        ms = jnp.mean(o * o, axis=-1, keepdims=True)
        o_ref[0] = (o * lax.rsqrt(ms + EPS) * g_ref[...] * out_scale).astype(o_ref.dtype)


def _attn_a(proj3, vt, lam, cfar, bias3, subln_g, lam_init):
    B, S, _ = proj3.shape
    t = T_ATT
    nq = S // t
    kern = functools.partial(_attn_a_kernel, out_scale=1.0 - lam_init)
    return pl.pallas_call(
        kern,
        out_shape=jax.ShapeDtypeStruct((B, S, HA * 2 * DA), BF16),
        grid=(B, HA, nq, nq),
        in_specs=[
            pl.BlockSpec(memory_space=pltpu.SMEM),
            pl.BlockSpec(memory_space=pltpu.SMEM),
            pl.BlockSpec((1, t, LANES), lambda b, h, qi, ki: (b, qi, h)),
            pl.BlockSpec((1, t, LANES), lambda b, h, qi, ki: (b, ki, HA + h)),
            pl.BlockSpec((1, LANES, t), lambda b, h, qi, ki: (b, h, ki)),
            pl.BlockSpec((1, 1, t, t), lambda b, h, qi, ki: (h, jnp.clip(ki - qi, -1, 1) + 1, 0, 0)),
            pl.BlockSpec((1, LANES), lambda b, h, qi, ki: (0, 0)),
        ],
        out_specs=pl.BlockSpec((1, t, LANES), lambda b, h, qi, ki: (b, qi, h)),
        scratch_shapes=[pltpu.VMEM((2, 1, t), F32), pltpu.VMEM((2, 1, t), F32),
                        pltpu.VMEM((2, LANES, t), F32)],
        compiler_params=_cparams(("parallel", "parallel", "parallel", "arbitrary")),
        name="diff_attn",
    )(lam, cfar, proj3, proj3, vt, bias3, subln_g)


def _attn_a_bias(rel_bias):
    t = T_ATT
    tab = rel_bias[:, :HA].astype(F32) * LOG2E
    d = jnp.arange(-1, 2, dtype=jnp.int32)[:, None, None] * t
    rel = d + jnp.arange(t, dtype=jnp.int32)[None, :, None] - jnp.arange(t, dtype=jnp.int32)[None, None, :]
    tiles = _bias_lookup(_t5_bucket(rel), tab)
    far = _t5_bucket(jnp.array([-(t + 1), t + 1], dtype=jnp.int32))
    cfar = tab[far].T.reshape(2 * HA)
    return tiles, cfar


def _attn_b_kernel(q_ref, k_ref, v_ref, bias_ref, o_ref, lse_ref, *, sub_len, r):
    i, s = pl.program_id(1), pl.program_id(2)
    start = jnp.clip(i * QB_DIL - HALF_WIN, 0, sub_len - KW_DIL)
    start = pl.multiple_of(start, HALF_WIN)
    q = q_ref[0, 0]
    kw = k_ref[0, s, pl.ds(start, KW_DIL), :]
    vw = v_ref[0, s, pl.ds(start, KW_DIL), :]
    low_half = lax.broadcasted_iota(jnp.int32, (1, LANES), 1) < HALF_LANES
    rows = slice(None) if r == 1 else pl.ds(s, QB_DIL, stride=r)
    for j in range(HB // 2):
        cols = slice(j * LANES, (j + 1) * LANES)
        qp, kp, vp = q[:, cols], kw[:, cols], vw[:, cols]
        outs, lses = [], []
        for c in range(2):
            qc = jnp.where(low_half if c == 0 else jnp.logical_not(low_half), qp, jnp.zeros_like(qp))
            sc = lax.dot_general(qc, kp, (((1,), (1,)), ((), ())), preferred_element_type=F32)
            sc = sc + bias_ref[2 * j + c, 0]
            m = jnp.max(sc, axis=-1, keepdims=True)
            p = jnp.exp(sc - m)
            l = jnp.sum(p, axis=-1, keepdims=True)
            outs.append(jnp.dot(p.astype(BF16), vp, preferred_element_type=F32) / l)
            lses.append(m + jnp.log(l))
        o_ref[0, j, rows, :] = jnp.where(low_half, outs[0], outs[1])
        lse_ref[0, j, rows, :] = jnp.where(low_half, lses[0], lses[1])


def _attn_b(qkv4, bias3, g, cols):
    B, r, L, _ = qkv4.shape
    S = r * L
    width = HB * DB
    nblk = L // QB_DIL
    qcol, kcol, vcol = cols
    kern = functools.partial(_attn_b_kernel, sub_len=L, r=r)
    slab = jax.ShapeDtypeStruct((B, N_SLABS, S, LANES), F32)
    slab_spec = pl.BlockSpec((1, N_SLABS, QB_DIL * r, LANES), lambda b, i, s: (b, 0, i, 0))
    return pl.pallas_call(
        kern,
        out_shape=[slab, slab],
        grid=(B, nblk, r),
        in_specs=[
            pl.BlockSpec((1, 1, QB_DIL, width), lambda b, i, s: (b, s, i, qcol)),
            pl.BlockSpec((1, r, L, width), lambda b, i, s: (b, 0, 0, kcol)),
            pl.BlockSpec((1, r, L, width), lambda b, i, s: (b, 0, 0, vcol)),
            pl.BlockSpec((HB, 1, QB_DIL, KW_DIL),
                         lambda b, i, s: (0, jnp.where(i == 0, 0, jnp.where(i == nblk - 1, 2, 1)), 0, 0)),
        ],
        out_specs=[slab_spec, slab_spec],
        compiler_params=_cparams(("parallel", "arbitrary", "arbitrary")),
        name=f"dilated_attn_{g}",
    )(qkv4, qkv4, qkv4, bias3)


def _attn_b_bias(rel_bias, g):
    r = DILATIONS[g]
    tab = rel_bias[:, HA + g * HB: HA + (g + 1) * HB].astype(F32)
    off = jnp.arange(3, dtype=jnp.int32)[:, None, None] * HALF_WIN
    rel = (jnp.arange(KW_DIL, dtype=jnp.int32)[None, None, :] - off
           - jnp.arange(QB_DIL, dtype=jnp.int32)[None, :, None])
    bias = _bias_lookup(_t5_bucket(rel * r), tab)
    return jnp.where((jnp.abs(rel) <= HALF_WIN)[None], bias, NEG)


def _sgu_kernel(zu_ref, zv_ref, lng_ref, lnb_ref, ws_ref, bs_ref, o_ref):
    u = jax.nn.gelu(zu_ref[...].astype(F32))
    v = jax.nn.gelu(zv_ref[...].astype(F32))
    mu = jnp.mean(v, axis=-1, keepdims=True)
    var = jnp.mean(jnp.square(v - mu), axis=-1, keepdims=True)
    v = ((v - mu) * lax.rsqrt(var + EPS) * lng_ref[...] + lnb_ref[...]).astype(BF16)
    gd = v.shape[1] // C_GROUPS
    for n in range(v.shape[0] // CHUNK):
        rows = slice(n * CHUNK, (n + 1) * CHUNK)
        for g in range(C_GROUPS):
            cols = slice(g * gd, (g + 1) * gd)
            mixed = jnp.dot(ws_ref[g], v[rows, cols], preferred_element_type=F32) + bs_ref[:, cols]
            o_ref[rows, cols] = (u[rows, cols] * mixed).astype(o_ref.dtype)


def _sgu(proj2, ln_g, ln_b, w_s, b_exp):
    T = proj2.shape[0]
    tm = min(TM_SGU, T)
    w = MIX_W
    return pl.pallas_call(
        _sgu_kernel,
        out_shape=jax.ShapeDtypeStruct((T, w), BF16),
        grid=(T // tm,),
        in_specs=[pl.BlockSpec((tm, w), lambda i: (i, COL_ZU)),
                  pl.BlockSpec((tm, w), lambda i: (i, COL_ZU + 1)),
                  pl.BlockSpec((1, w), lambda i: (0, 0)),
                  pl.BlockSpec((1, w), lambda i: (0, 0)),
                  pl.BlockSpec((C_GROUPS, CHUNK, CHUNK), lambda i: (0, 0, 0)),
                  pl.BlockSpec((CHUNK, w), lambda i: (0, 0))],
        out_specs=pl.BlockSpec((tm, w), lambda i: (i, 0)),
        compiler_params=_cparams(("parallel",)),
        name="sgu",
    )(proj2, proj2, ln_g, ln_b, w_s, b_exp)


def _route(logits):
    lane = lax.broadcasted_iota(jnp.int32, logits.shape, 1)
    big = jnp.int32(LANES)
    is_grp = (lane >= N_EXPERTS) & (lane < N_EXPERTS + N_GROUPS)
    gl = jnp.where(is_grp, logits, NEG)
    gmax = jnp.max(gl, axis=-1, keepdims=True)
    g_idx = jnp.min(jnp.where(is_grp & (gl == gmax), lane, big), axis=-1, keepdims=True) - N_EXPERTS
    g_w = 1.0 / jnp.sum(jnp.where(is_grp, jnp.exp(gl - gmax), 0.0), axis=-1, keepdims=True)
    in_grp = (lane >= g_idx * E_PER_GROUP) & (lane < (g_idx + 1) * E_PER_GROUP)
    sel = jnp.where(in_grp, logits, NEG)
    v1 = jnp.max(sel, axis=-1, keepdims=True)
    i1 = jnp.min(jnp.where(in_grp & (sel == v1), lane, big), axis=-1, keepdims=True)
    rest = in_grp & (lane != i1)
    sel2 = jnp.where(rest, logits, NEG)
    v2 = jnp.max(sel2, axis=-1, keepdims=True)
    i2 = jnp.min(jnp.where(rest & (sel2 == v2), lane, big), axis=-1, keepdims=True)
    e2 = jnp.exp(v2 - v1)
    w1 = g_w / (1.0 + e2)
    w2 = g_w * e2 / (1.0 + e2)
    return jnp.where(lane == i1, w1, jnp.where(lane == i2, w2, 0.0))


def _mix_kernel(x_ref, ya_ref, ob0_ref, ob1_ref, ob2_ref, ls0_ref, ls1_ref, ls2_ref, yc_ref,
                g0_ref, g1_ref, g2_ref, wb_ref, wo_ref, nf_ref, wr_ref, br_ref,
                xo_ref, h_ref, comb_ref):
    slabs = []
    for j in range(N_SLABS):
        ls0, ls1, ls2 = ls0_ref[0, j], ls1_ref[0, j], ls2_ref[0, j]
        mx = jnp.maximum(jnp.maximum(ls0, ls1), ls2)
        e0, e1, e2 = jnp.exp(ls0 - mx), jnp.exp(ls1 - mx), jnp.exp(ls2 - mx)
        yb = (e0 * ob0_ref[0, j] + e1 * ob1_ref[0, j] + e2 * ob2_ref[0, j]) / (e0 + e1 + e2)
        slabs.append(yb.astype(BF16))
    yb = jnp.concatenate(slabs, axis=-1)
    merged = jax.nn.sigmoid(g0_ref[...].astype(F32)) * jnp.dot(ya_ref[...], wb_ref[0],
                                                               preferred_element_type=F32)
    merged += jax.nn.sigmoid(g1_ref[...].astype(F32)) * jnp.dot(yb, wb_ref[1],
                                                                preferred_element_type=F32)
    merged += jax.nn.sigmoid(g2_ref[...].astype(F32)) * jnp.dot(yc_ref[...], wb_ref[2],
                                                                preferred_element_type=F32)
    xn = x_ref[...] + jnp.dot(merged.astype(BF16), wo_ref[...], preferred_element_type=F32)
    xo_ref[...] = xn
    h = _rms_bf16(xn, nf_ref[...])
    h_ref[...] = h
    logits = jnp.dot(h, wr_ref[...], preferred_element_type=F32) + br_ref[...]
    comb_ref[...] = _route(logits)


def _mix(x2, ya, obs, lses, yc, proj2, wb, wo, nf, wr, br):
    T, D = x2.shape
    S = obs[0].shape[2]
    tm = min(TM_MIX, S)
    per_b = S // tm
    w = MIX_W
    row = lambda width: pl.BlockSpec((tm, width), lambda i: (i, 0))
    full = lambda a: pl.BlockSpec(a.shape, lambda i: (0,) * a.ndim)
    gate = lambda n: pl.BlockSpec((tm, D), lambda i: (i, COL_GATE + n))
    slab = pl.BlockSpec((1, N_SLABS, tm, LANES), lambda i: (i // per_b, 0, i % per_b, 0))
    return pl.pallas_call(
        _mix_kernel,
        out_shape=[jax.ShapeDtypeStruct((T, D), F32), jax.ShapeDtypeStruct((T, D), BF16),
                   jax.ShapeDtypeStruct((T, LANES), F32)],
        grid=(T // tm,),
        in_specs=[row(D), row(w), slab, slab, slab, slab, slab, slab, row(w),
                  gate(0), gate(1), gate(2), full(wb), full(wo), full(nf), full(wr), full(br)],
        out_specs=[row(D), row(D), row(LANES)],
        compiler_params=_cparams(("parallel",)),
        name="mix",
    )(x2, ya, obs[0], obs[1], obs[2], lses[0], lses[1], lses[2], yc, proj2, proj2, proj2,
      wb, wo, nf, wr, br)


def _moe_kernel(h_ref, comb_ref, x_ref, wg_ref, wu_ref, wd_ref, nfin_ref, o_ref, acc_scr,
                *, final_norm):
    e = pl.program_id(1)

    @pl.when(e == 0)
    def _():
        acc_scr[...] = jnp.zeros(acc_scr.shape, F32)

    h = h_ref[...]
    lane = lax.broadcasted_iota(jnp.int32, comb_ref.shape, 1)
    c = jnp.sum(jnp.where(lane == e, comb_ref[...], 0.0), axis=-1, keepdims=True)
    hid = (jax.nn.silu(jnp.dot(h, wg_ref[0], preferred_element_type=F32))
           * jnp.dot(h, wu_ref[0], preferred_element_type=F32))
    acc_scr[...] += c * jnp.dot(hid.astype(BF16), wd_ref[0], preferred_element_type=F32)

    @pl.when(e == pl.num_programs(1) - 1)
    def _():
        xn = x_ref[...] + acc_scr[...]
        if final_norm:
            ms = jnp.mean(xn * xn, axis=-1, keepdims=True)
            xn = xn * lax.rsqrt(ms + EPS) * nfin_ref[...]
        o_ref[...] = xn


def _moe(h, comb, x2, wg, wu, wd, nfin, final_norm):
    T, D = x2.shape
    tm = min(TM_MOE, T)
    F = wg.shape[2]
    kern = functools.partial(_moe_kernel, final_norm=final_norm)
    return pl.pallas_call(
        kern,
        out_shape=jax.ShapeDtypeStruct((T, D), F32),
        grid=(T // tm, N_EXPERTS),
        in_specs=[pl.BlockSpec((tm, D), lambda i, e: (i, 0)),
                  pl.BlockSpec((tm, LANES), lambda i, e: (i, 0)),
                  pl.BlockSpec((tm, D), lambda i, e: (i, 0)),
                  pl.BlockSpec((1, D, F), lambda i, e: (e, 0, 0)),
                  pl.BlockSpec((1, D, F), lambda i, e: (e, 0, 0)),
                  pl.BlockSpec((1, F, D), lambda i, e: (e, 0, 0)),
                  pl.BlockSpec((1, D), lambda i, e: (0, 0))],
        out_specs=pl.BlockSpec((tm, D), lambda i, e: (i, 0)),
        scratch_shapes=[pltpu.VMEM((tm, D), F32)],
        compiler_params=_cparams(("parallel", "arbitrary")),
        name="moe",
    )(h, comb, x2, wg, wu, wd, nfin)


def kernel(x, rel_bias, norm_mix, w_in, diff_lambda, diff_subln, sgu_ln_g, sgu_ln_b, sgu_w, sgu_b,
           w_branch, w_out, norm_ffn, w_router_grp, b_router_grp, w_router_exp, b_router_exp,
           w_gate, w_up, w_down, norm_final):
    B, S, D = x.shape
    T = B * S
    depth = w_in.shape[0]
    a_out = HA * 2 * DA
    grp_w = HB * DB
    b_cols = 3 * NG_B * grp_w
    qkv_b0 = 3 * a_out
    zc0 = qkv_b0 + b_cols
    gate0 = zc0 + 2 * MIX_W
    qk_scale = DA ** -0.5

    bias_a, cfar = _attn_a_bias(rel_bias)
    bias_b = [_attn_b_bias(rel_bias, g) for g in range(NG_B)]

    def group_cols(w, g):
        q = w[:, qkv_b0 + g * grp_w: qkv_b0 + (g + 1) * grp_w] * qk_scale
        k = w[:, qkv_b0 + (NG_B + g) * grp_w: qkv_b0 + (NG_B + g + 1) * grp_w]
        v = w[:, qkv_b0 + (2 * NG_B + g) * grp_w: qkv_b0 + (2 * NG_B + g + 1) * grp_w]
        return [q, k, v]

    x2 = x.reshape(T, D)
    for i in range(depth):
        w = w_in[i]
        nm = norm_mix[i][None, :]
        w_main = jnp.concatenate([w[:, :a_out] * (qk_scale * LOG2E), w[:, a_out:2 * a_out],
                                  w[:, zc0:]] + group_cols(w, 0), axis=1).astype(BF16)
        proj2 = _inproj(x2, nm, w_main)
        x3 = x2.reshape(B, S, D)
        vt = _inproj_t(x3, nm, w[:, 2 * a_out:3 * a_out].T.astype(BF16))
        proj3 = proj2.reshape(B, S, proj2.shape[1])

        lam_init = 0.8 - 0.6 * math.exp(-0.3 * i)
        lp = diff_lambda[i].astype(F32)
        lam = jnp.exp(jnp.sum(lp[0] * lp[1])) - jnp.exp(jnp.sum(lp[2] * lp[3])) + lam_init
        ya = _attn_a(proj3, vt, lam.reshape(1), cfar, bias_a, diff_subln[i][None, :], lam_init)

        obs, lses = [], []
        for g in range(NG_B):
            r = DILATIONS[g]
            if r == 1:
                qkv4, cols = proj3[:, None], (COL_QKV0, COL_QKV0 + 1, COL_QKV0 + 2)
            else:
                w_g = jnp.concatenate(group_cols(w, g), axis=1).astype(BF16)
                qkv4, cols = _inproj_perm(x3, nm, w_g, r), (0, 1, 2)
            o, l = _attn_b(qkv4, bias_b[g], g, cols)
            obs.append(o)
            lses.append(l)

        b_exp = jnp.repeat(sgu_b[i].T, MIX_W // C_GROUPS, axis=1)
        yc = _sgu(proj2, sgu_ln_g[i][None, :], sgu_ln_b[i][None, :], sgu_w[i].astype(BF16), b_exp)

        wr = jnp.concatenate([w_router_exp[i].transpose(1, 0, 2).reshape(D, N_EXPERTS),
                              w_router_grp[i]], axis=1)
        wr = jnp.pad(wr, ((0, 0), (0, LANES - wr.shape[1]))).astype(BF16)
        br = jnp.concatenate([b_router_exp[i].reshape(N_EXPERTS), b_router_grp[i]])
        br = jnp.pad(br, (0, LANES - br.shape[0]))[None, :].astype(F32)

        x2, h, comb = _mix(x2, ya.reshape(T, a_out), obs, lses, yc, proj2,
                           w_branch[i].astype(BF16), w_out[i].astype(BF16), norm_ffn[i][None, :],
                           wr, br)
        x2 = _moe(h, comb, x2, w_gate[i].astype(BF16), w_up[i].astype(BF16),
                  w_down[i].astype(BF16), norm_final[None, :], i == depth - 1)
    return x2.reshape(B, S, D)
```

```python
import functools
import math

import jax
import jax.numpy as jnp
from jax import lax
from jax.experimental import pallas as pl
from jax.experimental.pallas import tpu as pltpu

F32 = jnp.float32
BF16 = jnp.bfloat16

EPS = 1e-6
NEG = -1e30
LOG2E = 1.4426950408889634
LANES = 128
HALF_LANES = LANES // 2
VMEM_LIMIT = 48 * 1024 * 1024

HA = 4
DA = 64
MIX_W = 512
WINDOWS = (128, 512, 2048)
DILATIONS = (1, 4, 16)
NG_B = 3
HB = 8
DB = 64
HALF_WIN = 64
CHUNK = 128
C_GROUPS = 4
N_BRANCH = 3
N_BUCKETS = 32
MAX_DIST = 128
N_GROUPS = 4
E_PER_GROUP = 4
N_EXPERTS = N_GROUPS * E_PER_GROUP
N_SLABS = MIX_W // LANES

TM_PROJ = 1024
TN_PROJ = 512
TM_PERM = 512
T_ATT = 512
QB_DIL = 128
KW_DIL = QB_DIL + 2 * HALF_WIN
TM_SGU = 512
TM_MIX = 512
TM_MOE = 1024

COL_ZU = 2
COL_GATE = 2
COL_QKV0 = 10


def _cparams(sem):
    return pltpu.CompilerParams(dimension_semantics=sem, vmem_limit_bytes=VMEM_LIMIT)


def _t5_bucket(rel):
    nb = N_BUCKETS // 2
    max_exact = nb // 2
    ret = (rel > 0).astype(jnp.int32) * nb
    n = jnp.abs(rel)
    nf = jnp.maximum(n, 1).astype(F32)
    large = max_exact + (jnp.log(nf / max_exact) / math.log(MAX_DIST / max_exact)
                         * (nb - max_exact)).astype(jnp.int32)
    large = jnp.minimum(large, nb - 1)
    return ret + jnp.where(n < max_exact, n, large)


def _bias_lookup(bucket, tab):
    out = jnp.zeros((tab.shape[1],) + bucket.shape, F32)
    expand = (slice(None),) + (None,) * bucket.ndim
    for b in range(N_BUCKETS):
        out = jnp.where(bucket[None] == b, tab[b][expand], out)
    return out


def _rms_bf16(x, g):
    ms = jnp.mean(x * x, axis=-1, keepdims=True)
    return (x * lax.rsqrt(ms + EPS) * g).astype(BF16)


def _inproj_kernel(x_ref, g_ref, w_ref, o_ref, h_scr):
    @pl.when(pl.program_id(1) == 0)
    def _():
        h_scr[...] = _rms_bf16(x_ref[...], g_ref[...])

    o_ref[...] = jnp.dot(h_scr[...], w_ref[...], preferred_element_type=F32).astype(o_ref.dtype)


def _inproj(x2, g, w):
    T, D = x2.shape
    N = w.shape[1]
    tm = min(TM_PROJ, T)
    return pl.pallas_call(
        _inproj_kernel,
        out_shape=jax.ShapeDtypeStruct((T, N), BF16),
        grid=(T // tm, N // TN_PROJ),
        in_specs=[pl.BlockSpec((tm, D), lambda i, j: (i, 0)),
                  pl.BlockSpec((1, D), lambda i, j: (0, 0)),
                  pl.BlockSpec((D, TN_PROJ), lambda i, j: (0, j))],
        out_specs=pl.BlockSpec((tm, TN_PROJ), lambda i, j: (i, j)),
        scratch_shapes=[pltpu.VMEM((tm, D), BF16)],
        compiler_params=_cparams(("parallel", "arbitrary")),
        name="inproj",
    )(x2, g, w)


def _inproj_t_kernel(x_ref, g_ref, wt_ref, o_ref):
    h = _rms_bf16(x_ref[0], g_ref[...])
    res = lax.dot_general(wt_ref[...], h, (((1,), (1,)), ((), ())),
                          preferred_element_type=F32).astype(o_ref.dtype)
    for hd in range(o_ref.shape[1]):
        for n in range(o_ref.shape[2]):
            o_ref[0, hd, n] = res[hd * LANES:(hd + 1) * LANES, n * T_ATT:(n + 1) * T_ATT]


def _inproj_t(x3, g, wt):
    B, S, D = x3.shape
    N = wt.shape[0]
    tm = min(TM_PROJ, S)
    nh, nb = N // LANES, tm // T_ATT
    return pl.pallas_call(
        _inproj_t_kernel,
        out_shape=jax.ShapeDtypeStruct((B, nh, S // T_ATT, LANES, T_ATT), BF16),
        grid=(B, S // tm),
        in_specs=[pl.BlockSpec((1, tm, D), lambda b, i: (b, i, 0)),
                  pl.BlockSpec((1, D), lambda b, i: (0, 0)),
                  pl.BlockSpec((N, D), lambda b, i: (0, 0))],
        out_specs=pl.BlockSpec((1, nh, nb, LANES, T_ATT), lambda b, i: (b, 0, i, 0, 0)),
        compiler_params=_cparams(("parallel", "parallel")),
        name="inproj_t",
    )(x3, g, wt)


def _inproj_perm_kernel(x_ref, g_ref, p_ref, w_ref, o_ref, *, r):
    h = _rms_bf16(x_ref[0], g_ref[...])
    hp = jnp.dot(p_ref[...], h, preferred_element_type=F32).astype(BF16)
    res = jnp.dot(hp, w_ref[...], preferred_element_type=F32).astype(o_ref.dtype)
    n = res.shape[0] // r
    for s in range(r):
        o_ref[0, s] = res[s * n:(s + 1) * n, :]


def _inproj_perm(x3, g, w, r):
    B, S, D = x3.shape
    N = w.shape[1]
    tm = min(TM_PERM, S)
    n = tm // r
    o = jnp.arange(tm, dtype=jnp.int32)
    src = (o % n) * r + o // n
    perm = (src[:, None] == jnp.arange(tm, dtype=jnp.int32)[None, :]).astype(BF16)
    kern = functools.partial(_inproj_perm_kernel, r=r)
    return pl.pallas_call(
        kern,
        out_shape=jax.ShapeDtypeStruct((B, r, S // r, N), BF16),
        grid=(B, S // tm),
        in_specs=[pl.BlockSpec((1, tm, D), lambda b, i: (b, i, 0)),
                  pl.BlockSpec((1, D), lambda b, i: (0, 0)),
                  pl.BlockSpec((tm, tm), lambda b, i: (0, 0)),
                  pl.BlockSpec((D, N), lambda b, i: (0, 0))],
        out_specs=pl.BlockSpec((1, r, n, N), lambda b, i: (b, 0, i, 0)),
        compiler_params=_cparams(("parallel", "parallel")),
        name=f"inproj_perm_{r}",
    )(x3, g, perm, w)


def _attn_a_kernel(lam_ref, cfar_ref, q_ref, k_ref, vt_ref, bias_ref, g_ref, o_ref, *, out_scale):
    h, qi = pl.program_id(1), pl.program_id(2)
    t = T_ATT
    nk = k_ref.shape[1] // t
    q = q_ref[0]
    low_half = lax.broadcasted_iota(jnp.int32, (1, LANES), 1) < HALF_LANES
    zero = jnp.zeros_like(q)
    qs = jnp.concatenate([jnp.where(low_half, q, zero), jnp.where(low_half, zero, q)], axis=0)

    blocks, sts, shifts = [], [], []
    m = None
    for d in range(-1, nk - 1):
        a = lax.rem(qi + (d + nk), nk)
        delta = a - qi
        kb = k_ref[0, pl.ds(pl.multiple_of(a * t, t), t), :]
        st = lax.dot_general(kb, qs, (((1,), (1,)), ((), ())), preferred_element_type=F32)
        if d <= 1:
            tile = bias_ref[0, jnp.clip(delta, -2, 2) + 2]
            st = st + jnp.concatenate([tile, tile], axis=1)
            shift = None
            cm = jnp.max(st, axis=0, keepdims=True)
        else:
            shift = cfar_ref[2 * h + (delta > 0).astype(jnp.int32)]
            cm = jnp.max(st, axis=0, keepdims=True) + shift
        m = cm if m is None else jnp.maximum(m, cm)
        blocks.append(a)
        sts.append(st)
        shifts.append(shift)

    l = jnp.zeros_like(m)
    accs = [jnp.zeros((LANES, t), F32), jnp.zeros((LANES, t), F32)]
    for a, st, shift in zip(blocks, sts, shifts):
        p = jnp.exp2(st - (m if shift is None else m - shift))
        l = l + jnp.sum(p, axis=0, keepdims=True)
        pb = p.astype(BF16)
        vt = vt_ref[0, 0, a]
        for c in range(2):
            accs[c] = accs[c] + jnp.dot(vt, pb[:, c * t:(c + 1) * t], preferred_element_type=F32)

    ot = accs[0] / l[:, :t] - lam_ref[0] * (accs[1] / l[:, t:])
    o = ot.T
    ms = jnp.mean(o * o, axis=-1, keepdims=True)
    o_ref[0] = (o * lax.rsqrt(ms + EPS) * g_ref[...] * out_scale).astype(o_ref.dtype)


def _attn_a(proj3, vt, lam, cfar, bias5, subln_g, lam_init):
    B, S, _ = proj3.shape
    t = T_ATT
    nk = S // t
    kern = functools.partial(_attn_a_kernel, out_scale=1.0 - lam_init)
    return pl.pallas_call(
        kern,
        out_shape=jax.ShapeDtypeStruct((B, S, HA * 2 * DA), BF16),
        grid=(B, HA, nk),
        in_specs=[
            pl.BlockSpec(memory_space=pltpu.SMEM),
            pl.BlockSpec(memory_space=pltpu.SMEM),
            pl.BlockSpec((1, t, LANES), lambda b, h, qi: (b, qi, h)),
            pl.BlockSpec((1, S, LANES), lambda b, h, qi: (b, 0, HA + h)),
            pl.BlockSpec((1, 1, nk, LANES, t), lambda b, h, qi: (b, h, 0, 0, 0)),
            pl.BlockSpec((1, 5, t, t), lambda b, h, qi: (h, 0, 0, 0)),
            pl.BlockSpec((1, LANES), lambda b, h, qi: (0, 0)),
        ],
        out_specs=pl.BlockSpec((1, t, LANES), lambda b, h, qi: (b, qi, h)),
        compiler_params=_cparams(("parallel", "parallel", "arbitrary")),
        name="diff_attn",
    )(lam, cfar, proj3, proj3, vt, bias5, subln_g)


def _attn_a_bias(rel_bias):
    t = T_ATT
    tab = rel_bias[:, :HA].astype(F32) * LOG2E
    d = jnp.arange(-2, 3, dtype=jnp.int32)[:, None, None] * t
    rel = d + jnp.arange(t, dtype=jnp.int32)[None, :, None] - jnp.arange(t, dtype=jnp.int32)[None, None, :]
    tiles = _bias_lookup(_t5_bucket(rel), tab)
    far = _t5_bucket(jnp.array([-(t + 1), t + 1], dtype=jnp.int32))
    cfar = tab[far].T.reshape(2 * HA)
    return tiles, cfar


def _attn_b_kernel(q_ref, k_ref, v_ref, bias_ref, o_ref, lse_ref, *, sub_len, r):
    i, s = pl.program_id(1), pl.program_id(2)
    start = jnp.clip(i * QB_DIL - HALF_WIN, 0, sub_len - KW_DIL)
    start = pl.multiple_of(start, HALF_WIN)
    q = q_ref[0, 0]
    kw = k_ref[0, s, pl.ds(start, KW_DIL), :]
    vw = v_ref[0, s, pl.ds(start, KW_DIL), :]
    low_half = lax.broadcasted_iota(jnp.int32, (1, LANES), 1) < HALF_LANES
    rows = slice(None) if r == 1 else pl.ds(s, QB_DIL, stride=r)
    for j in range(HB // 2):
        cols = slice(j * LANES, (j + 1) * LANES)
        qp, kp, vp = q[:, cols], kw[:, cols], vw[:, cols]
        outs, lses = [], []
        for c in range(2):
            qc = jnp.where(low_half if c == 0 else jnp.logical_not(low_half), qp, jnp.zeros_like(qp))
            sc = lax.dot_general(qc, kp, (((1,), (1,)), ((), ())), preferred_element_type=F32)
            sc = sc + bias_ref[2 * j + c, 0]
            m = jnp.max(sc, axis=-1, keepdims=True)
            p = jnp.exp(sc - m)
            l = jnp.sum(p, axis=-1, keepdims=True)
            outs.append(jnp.dot(p.astype(BF16), vp, preferred_element_type=F32) / l)
            lses.append(m + jnp.log(l))
        o_ref[0, j, rows, :] = jnp.where(low_half, outs[0], outs[1])
        lse_ref[0, j, rows, :] = jnp.where(low_half, lses[0], lses[1])


def _attn_b(qkv4, bias3, g, cols):
    B, r, L, _ = qkv4.shape
    S = r * L
    width = HB * DB
    nblk = L // QB_DIL
    qcol, kcol, vcol = cols
    kern = functools.partial(_attn_b_kernel, sub_len=L, r=r)
    slab = jax.ShapeDtypeStruct((B, N_SLABS, S, LANES), F32)
    slab_spec = pl.BlockSpec((1, N_SLABS, QB_DIL * r, LANES), lambda b, i, s: (b, 0, i, 0))
    return pl.pallas_call(
        kern,
        out_shape=[slab, slab],
        grid=(B, nblk, r),
        in_specs=[
            pl.BlockSpec((1, 1, QB_DIL, width), lambda b, i, s: (b, s, i, qcol)),
            pl.BlockSpec((1, r, L, width), lambda b, i, s: (b, 0, 0, kcol)),
            pl.BlockSpec((1, r, L, width), lambda b, i, s: (b, 0, 0, vcol)),
            pl.BlockSpec((HB, 1, QB_DIL, KW_DIL),
                         lambda b, i, s: (0, jnp.where(i == 0, 0, jnp.where(i == nblk - 1, 2, 1)), 0, 0)),
        ],
        out_specs=[slab_spec, slab_spec],
        compiler_params=_cparams(("parallel", "arbitrary", "arbitrary")),
        name=f"dilated_attn_{g}",
    )(qkv4, qkv4, qkv4, bias3)


def _attn_b_bias(rel_bias, g):
    r = DILATIONS[g]
    tab = rel_bias[:, HA + g * HB: HA + (g + 1) * HB].astype(F32)
    off = jnp.arange(3, dtype=jnp.int32)[:, None, None] * HALF_WIN
    rel = (jnp.arange(KW_DIL, dtype=jnp.int32)[None, None, :] - off
           - jnp.arange(QB_DIL, dtype=jnp.int32)[None, :, None])
    bias = _bias_lookup(_t5_bucket(rel * r), tab)
    return jnp.where((jnp.abs(rel) <= HALF_WIN)[None], bias, NEG)


def _sgu_kernel(zu_ref, zv_ref, lng_ref, lnb_ref, ws_ref, bs_ref, o_ref):
    u = jax.nn.gelu(zu_ref[...].astype(F32))
    v = jax.nn.gelu(zv_ref[...].astype(F32))
    mu = jnp.mean(v, axis=-1, keepdims=True)
    var = jnp.mean(jnp.square(v - mu), axis=-1, keepdims=True)
    v = ((v - mu) * lax.rsqrt(var + EPS) * lng_ref[...] + lnb_ref[...]).astype(BF16)
    gd = v.shape[1] // C_GROUPS
    for n in range(v.shape[0] // CHUNK):
        rows = slice(n * CHUNK, (n + 1) * CHUNK)
        for g in range(C_GROUPS):
            cols = slice(g * gd, (g + 1) * gd)
            mixed = jnp.dot(ws_ref[g], v[rows, cols], preferred_element_type=F32) + bs_ref[:, cols]
            o_ref[rows, cols] = (u[rows, cols] * mixed).astype(o_ref.dtype)


def _sgu(proj2, ln_g, ln_b, w_s, b_exp):
    T = proj2.shape[0]
    tm = min(TM_SGU, T)
    w = MIX_W
    return pl.pallas_call(
        _sgu_kernel,
        out_shape=jax.ShapeDtypeStruct((T, w), BF16),
        grid=(T // tm,),
        in_specs=[pl.BlockSpec((tm, w), lambda i: (i, COL_ZU)),
                  pl.BlockSpec((tm, w), lambda i: (i, COL_ZU + 1)),
                  pl.BlockSpec((1, w), lambda i: (0, 0)),
                  pl.BlockSpec((1, w), lambda i: (0, 0)),
                  pl.BlockSpec((C_GROUPS, CHUNK, CHUNK), lambda i: (0, 0, 0)),
                  pl.BlockSpec((CHUNK, w), lambda i: (0, 0))],
        out_specs=pl.BlockSpec((tm, w), lambda i: (i, 0)),
        compiler_params=_cparams(("parallel",)),
        name="sgu",
    )(proj2, proj2, ln_g, ln_b, w_s, b_exp)


def _route(logits):
    lane = lax.broadcasted_iota(jnp.int32, logits.shape, 1)
    big = jnp.int32(LANES)
    is_grp = (lane >= N_EXPERTS) & (lane < N_EXPERTS + N_GROUPS)
    gl = jnp.where(is_grp, logits, NEG)
    gmax = jnp.max(gl, axis=-1, keepdims=True)
    g_idx = jnp.min(jnp.where(is_grp & (gl == gmax), lane, big), axis=-1, keepdims=True) - N_EXPERTS
    g_w = 1.0 / jnp.sum(jnp.where(is_grp, jnp.exp(gl - gmax), 0.0), axis=-1, keepdims=True)
    in_grp = (lane >= g_idx * E_PER_GROUP) & (lane < (g_idx + 1) * E_PER_GROUP)
    sel = jnp.where(in_grp, logits, NEG)
    v1 = jnp.max(sel, axis=-1, keepdims=True)
    i1 = jnp.min(jnp.where(in_grp & (sel == v1), lane, big), axis=-1, keepdims=True)
    rest = in_grp & (lane != i1)
    sel2 = jnp.where(rest, logits, NEG)
    v2 = jnp.max(sel2, axis=-1, keepdims=True)
    i2 = jnp.min(jnp.where(rest & (sel2 == v2), lane, big), axis=-1, keepdims=True)
    e2 = jnp.exp(v2 - v1)
    w1 = g_w / (1.0 + e2)
    w2 = g_w * e2 / (1.0 + e2)
    return jnp.where(lane == i1, w1, jnp.where(lane == i2, w2, 0.0))


def _mix_kernel(x_ref, ya_ref, ob0_ref, ob1_ref, ob2_ref, ls0_ref, ls1_ref, ls2_ref, yc_ref,
                g0_ref, g1_ref, g2_ref, wb_ref, wo_ref, nf_ref, wr_ref, br_ref,
                xo_ref, h_ref, comb_ref):
    slabs = []
    for j in range(N_SLABS):
        ls0, ls1, ls2 = ls0_ref[0, j], ls1_ref[0, j], ls2_ref[0, j]
        mx = jnp.maximum(jnp.maximum(ls0, ls1), ls2)
        e0, e1, e2 = jnp.exp(ls0 - mx), jnp.exp(ls1 - mx), jnp.exp(ls2 - mx)
        yb = (e0 * ob0_ref[0, j] + e1 * ob1_ref[0, j] + e2 * ob2_ref[0, j]) / (e0 + e1 + e2)
        slabs.append(yb.astype(BF16))
    yb = jnp.concatenate(slabs, axis=-1)
    merged = jax.nn.sigmoid(g0_ref[...].astype(F32)) * jnp.dot(ya_ref[...], wb_ref[0],
                                                               preferred_element_type=F32)
    merged += jax.nn.sigmoid(g1_ref[...].astype(F32)) * jnp.dot(yb, wb_ref[1],
                                                                preferred_element_type=F32)
    merged += jax.nn.sigmoid(g2_ref[...].astype(F32)) * jnp.dot(yc_ref[...], wb_ref[2],
                                                                preferred_element_type=F32)
    xn = x_ref[...] + jnp.dot(merged.astype(BF16), wo_ref[...], preferred_element_type=F32)
    xo_ref[...] = xn
    h = _rms_bf16(xn, nf_ref[...])
    h_ref[...] = h
    logits = jnp.dot(h, wr_ref[...], preferred_element_type=F32) + br_ref[...]
    comb_ref[...] = _route(logits)


def _mix(x2, ya, obs, lses, yc, proj2, wb, wo, nf, wr, br):
    T, D = x2.shape
    S = obs[0].shape[2]
    tm = min(TM_MIX, S)
    per_b = S // tm
    w = MIX_W
    row = lambda width: pl.BlockSpec((tm, width), lambda i: (i, 0))
    full = lambda a: pl.BlockSpec(a.shape, lambda i: (0,) * a.ndim)
    gate = lambda n: pl.BlockSpec((tm, D), lambda i: (i, COL_GATE + n))
    slab = pl.BlockSpec((1, N_SLABS, tm, LANES), lambda i: (i // per_b, 0, i % per_b, 0))
    return pl.pallas_call(
        _mix_kernel,
        out_shape=[jax.ShapeDtypeStruct((T, D), F32), jax.ShapeDtypeStruct((T, D), BF16),
                   jax.ShapeDtypeStruct((T, LANES), F32)],
        grid=(T // tm,),
        in_specs=[row(D), row(w), slab, slab, slab, slab, slab, slab, row(w),
                  gate(0), gate(1), gate(2), full(wb), full(wo), full(nf), full(wr), full(br)],
        out_specs=[row(D), row(D), row(LANES)],
        compiler_params=_cparams(("parallel",)),
        name="mix",
    )(x2, ya, obs[0], obs[1], obs[2], lses[0], lses[1], lses[2], yc, proj2, proj2, proj2,
      wb, wo, nf, wr, br)


def _moe_kernel(h_ref, comb_ref, x_ref, wg_ref, wu_ref, wd_ref, nfin_ref, o_ref, acc_scr,
                *, final_norm):
    e = pl.program_id(1)

    @pl.when(e == 0)
    def _():
        acc_scr[...] = jnp.zeros(acc_scr.shape, F32)

    h = h_ref[...]
    lane = lax.broadcasted_iota(jnp.int32, comb_ref.shape, 1)
    c = jnp.sum(jnp.where(lane == e, comb_ref[...], 0.0), axis=-1, keepdims=True)
    hid = (jax.nn.silu(jnp.dot(h, wg_ref[0], preferred_element_type=F32))
           * jnp.dot(h, wu_ref[0], preferred_element_type=F32))
    acc_scr[...] += c * jnp.dot(hid.astype(BF16), wd_ref[0], preferred_element_type=F32)

    @pl.when(e == pl.num_programs(1) - 1)
    def _():
        xn = x_ref[...] + acc_scr[...]
        if final_norm:
            ms = jnp.mean(xn * xn, axis=-1, keepdims=True)
            xn = xn * lax.rsqrt(ms + EPS) * nfin_ref[...]
        o_ref[...] = xn


def _moe(h, comb, x2, wg, wu, wd, nfin, final_norm):
    T, D = x2.shape
    tm = min(TM_MOE, T)
    F = wg.shape[2]
    kern = functools.partial(_moe_kernel, final_norm=final_norm)
    return pl.pallas_call(
        kern,
        out_shape=jax.ShapeDtypeStruct((T, D), F32),
        grid=(T // tm, N_EXPERTS),
        in_specs=[pl.BlockSpec((tm, D), lambda i, e: (i, 0)),
                  pl.BlockSpec((tm, LANES), lambda i, e: (i, 0)),
                  pl.BlockSpec((tm, D), lambda i, e: (i, 0)),
                  pl.BlockSpec((1, D, F), lambda i, e: (e, 0, 0)),
                  pl.BlockSpec((1, D, F), lambda i, e: (e, 0, 0)),
                  pl.BlockSpec((1, F, D), lambda i, e: (e, 0, 0)),
                  pl.BlockSpec((1, D), lambda i, e: (0, 0))],
        out_specs=pl.BlockSpec((tm, D), lambda i, e: (i, 0)),
        scratch_shapes=[pltpu.VMEM((tm, D), F32)],
        compiler_params=_cparams(("parallel", "arbitrary")),
        name="moe",
    )(h, comb, x2, wg, wu, wd, nfin)


def kernel(x, rel_bias, norm_mix, w_in, diff_lambda, diff_subln, sgu_ln_g, sgu_ln_b, sgu_w, sgu_b,
           w_branch, w_out, norm_ffn, w_router_grp, b_router_grp, w_router_exp, b_router_exp,
           w_gate, w_up, w_down, norm_final):
    B, S, D = x.shape
    T = B * S
    depth = w_in.shape[0]
    a_out = HA * 2 * DA
    grp_w = HB * DB
    b_cols = 3 * NG_B * grp_w
    qkv_b0 = 3 * a_out
    zc0 = qkv_b0 + b_cols
    gate0 = zc0 + 2 * MIX_W
    qk_scale = DA ** -0.5

    bias_a, cfar = _attn_a_bias(rel_bias)
    bias_b = [_attn_b_bias(rel_bias, g) for g in range(NG_B)]

    def group_cols(w, g):
        q = w[:, qkv_b0 + g * grp_w: qkv_b0 + (g + 1) * grp_w] * qk_scale
        k = w[:, qkv_b0 + (NG_B + g) * grp_w: qkv_b0 + (NG_B + g + 1) * grp_w]
        v = w[:, qkv_b0 + (2 * NG_B + g) * grp_w: qkv_b0 + (2 * NG_B + g + 1) * grp_w]
        return [q, k, v]

    x2 = x.reshape(T, D)
    for i in range(depth):
        w = w_in[i]
        nm = norm_mix[i][None, :]
        w_main = jnp.concatenate([w[:, :a_out] * (qk_scale * LOG2E), w[:, a_out:2 * a_out],
                                  w[:, zc0:]] + group_cols(w, 0), axis=1).astype(BF16)
        proj2 = _inproj(x2, nm, w_main)
        x3 = x2.reshape(B, S, D)
        vt = _inproj_t(x3, nm, w[:, 2 * a_out:3 * a_out].T.astype(BF16))
        proj3 = proj2.reshape(B, S, proj2.shape[1])

        lam_init = 0.8 - 0.6 * math.exp(-0.3 * i)
        lp = diff_lambda[i].astype(F32)
        lam = jnp.exp(jnp.sum(lp[0] * lp[1])) - jnp.exp(jnp.sum(lp[2] * lp[3])) + lam_init
        ya = _attn_a(proj3, vt, lam.reshape(1), cfar, bias_a, diff_subln[i][None, :], lam_init)

        obs, lses = [], []
        for g in range(NG_B):
            r = DILATIONS[g]
            if r == 1:
                qkv4, cols = proj3[:, None], (COL_QKV0, COL_QKV0 + 1, COL_QKV0 + 2)
            else:
                w_g = jnp.concatenate(group_cols(w, g), axis=1).astype(BF16)
                qkv4, cols = _inproj_perm(x3, nm, w_g, r), (0, 1, 2)
            o, l = _attn_b(qkv4, bias_b[g], g, cols)
            obs.append(o)
            lses.append(l)

        b_exp = jnp.repeat(sgu_b[i].T, MIX_W // C_GROUPS, axis=1)
        yc = _sgu(proj2, sgu_ln_g[i][None, :], sgu_ln_b[i][None, :], sgu_w[i].astype(BF16), b_exp)

        wr = jnp.concatenate([w_router_exp[i].transpose(1, 0, 2).reshape(D, N_EXPERTS),
                              w_router_grp[i]], axis=1)
        wr = jnp.pad(wr, ((0, 0), (0, LANES - wr.shape[1]))).astype(BF16)
        br = jnp.concatenate([b_router_exp[i].reshape(N_EXPERTS), b_router_grp[i]])
        br = jnp.pad(br, (0, LANES - br.shape[0]))[None, :].astype(F32)

        x2, h, comb = _mix(x2, ya.reshape(T, a_out), obs, lses, yc, proj2,
                           w_branch[i].astype(BF16), w_out[i].astype(BF16), norm_ffn[i][None, :],
                           wr, br)
        x2 = _moe(h, comb, x2, w_gate[i].astype(BF16), w_up[i].astype(BF16),
                  w_down[i].astype(BF16), norm_final[None, :], i == depth - 1)
    return x2.reshape(B, S, D)
```

```python
import functools
import math

import jax
import jax.numpy as jnp
from jax import lax
from jax.experimental import pallas as pl
from jax.experimental.pallas import tpu as pltpu

F32 = jnp.float32
BF16 = jnp.bfloat16

EPS = 1e-6
NEG = -1e30
LOG2E = 1.4426950408889634
LANES = 128
HALF_LANES = LANES // 2
VMEM_LIMIT = 48 * 1024 * 1024

HA = 4
DA = 64
MIX_W = 512
WINDOWS = (128, 512, 2048)
DILATIONS = (1, 4, 16)
NG_B = 3
HB = 8
DB = 64
HALF_WIN = 64
CHUNK = 128
C_GROUPS = 4
N_BRANCH = 3
N_BUCKETS = 32
MAX_DIST = 128
N_GROUPS = 4
E_PER_GROUP = 4
N_EXPERTS = N_GROUPS * E_PER_GROUP
N_SLABS = MIX_W // LANES

TM_PROJ = 1024
TN_PROJ = 512
TM_PERM = 512
T_ATT = 512
QB_DIL = 128
KW_DIL = QB_DIL + 2 * HALF_WIN
TM_SGU = 512
TM_MIX = 512
TM_MOE = 1024
TM_DISP = 256
MOE_CAP = HALF_LANES
G_FFN = 8
MAX_OVF = 16

COL_ZU = 2
COL_GATE = 2
COL_QKV0 = 10


def _cparams(sem):
    return pltpu.CompilerParams(dimension_semantics=sem, vmem_limit_bytes=VMEM_LIMIT)


def _t5_bucket(rel):
    nb = N_BUCKETS // 2
    max_exact = nb // 2
    ret = (rel > 0).astype(jnp.int32) * nb
    n = jnp.abs(rel)
    nf = jnp.maximum(n, 1).astype(F32)
    large = max_exact + (jnp.log(nf / max_exact) / math.log(MAX_DIST / max_exact)
                         * (nb - max_exact)).astype(jnp.int32)
    large = jnp.minimum(large, nb - 1)
    return ret + jnp.where(n < max_exact, n, large)


def _bias_lookup(bucket, tab):
    out = jnp.zeros((tab.shape[1],) + bucket.shape, F32)
    expand = (slice(None),) + (None,) * bucket.ndim
    for b in range(N_BUCKETS):
        out = jnp.where(bucket[None] == b, tab[b][expand], out)
    return out


def _rms_bf16(x, g):
    ms = jnp.mean(x * x, axis=-1, keepdims=True)
    return (x * lax.rsqrt(ms + EPS) * g).astype(BF16)


def _inproj_kernel(x_ref, g_ref, w_ref, o_ref, h_scr):
    @pl.when(pl.program_id(1) == 0)
    def _():
        h_scr[...] = _rms_bf16(x_ref[...], g_ref[...])

    o_ref[...] = jnp.dot(h_scr[...], w_ref[...], preferred_element_type=F32).astype(o_ref.dtype)


def _inproj(x2, g, w):
    T, D = x2.shape
    N = w.shape[1]
    tm = min(TM_PROJ, T)
    return pl.pallas_call(
        _inproj_kernel,
        out_shape=jax.ShapeDtypeStruct((T, N), BF16),
        grid=(T // tm, N // TN_PROJ),
        in_specs=[pl.BlockSpec((tm, D), lambda i, j: (i, 0)),
                  pl.BlockSpec((1, D), lambda i, j: (0, 0)),
                  pl.BlockSpec((D, TN_PROJ), lambda i, j: (0, j))],
        out_specs=pl.BlockSpec((tm, TN_PROJ), lambda i, j: (i, j)),
        scratch_shapes=[pltpu.VMEM((tm, D), BF16)],
        compiler_params=_cparams(("parallel", "arbitrary")),
        name="inproj",
    )(x2, g, w)


def _inproj_t_kernel(x_ref, g_ref, wt_ref, o_ref):
    h = _rms_bf16(x_ref[0], g_ref[...])
    res = lax.dot_general(wt_ref[...], h, (((1,), (1,)), ((), ())),
                          preferred_element_type=F32).astype(o_ref.dtype)
    for hd in range(o_ref.shape[1]):
        for n in range(o_ref.shape[2]):
            o_ref[0, hd, n] = res[hd * LANES:(hd + 1) * LANES, n * T_ATT:(n + 1) * T_ATT]


def _inproj_t(x3, g, wt):
    B, S, D = x3.shape
    N = wt.shape[0]
    tm = min(TM_PROJ, S)
    nh, nb = N // LANES, tm // T_ATT
    return pl.pallas_call(
        _inproj_t_kernel,
        out_shape=jax.ShapeDtypeStruct((B, nh, S // T_ATT, LANES, T_ATT), BF16),
        grid=(B, S // tm),
        in_specs=[pl.BlockSpec((1, tm, D), lambda b, i: (b, i, 0)),
                  pl.BlockSpec((1, D), lambda b, i: (0, 0)),
                  pl.BlockSpec((N, D), lambda b, i: (0, 0))],
        out_specs=pl.BlockSpec((1, nh, nb, LANES, T_ATT), lambda b, i: (b, 0, i, 0, 0)),
        compiler_params=_cparams(("parallel", "parallel")),
        name="inproj_t",
    )(x3, g, wt)


def _inproj_perm_kernel(x_ref, g_ref, p_ref, w_ref, o_ref, *, r):
    h = _rms_bf16(x_ref[0], g_ref[...])
    hp = jnp.dot(p_ref[...], h, preferred_element_type=F32).astype(BF16)
    res = jnp.dot(hp, w_ref[...], preferred_element_type=F32).astype(o_ref.dtype)
    n = res.shape[0] // r
    for s in range(r):
        o_ref[0, s] = res[s * n:(s + 1) * n, :]


def _inproj_perm(x3, g, w, r):
    B, S, D = x3.shape
    N = w.shape[1]
    tm = min(TM_PERM, S)
    n = tm // r
    o = jnp.arange(tm, dtype=jnp.int32)
    src = (o % n) * r + o // n
    perm = (src[:, None] == jnp.arange(tm, dtype=jnp.int32)[None, :]).astype(BF16)
    kern = functools.partial(_inproj_perm_kernel, r=r)
    return pl.pallas_call(
        kern,
        out_shape=jax.ShapeDtypeStruct((B, r, S // r, N), BF16),
        grid=(B, S // tm),
        in_specs=[pl.BlockSpec((1, tm, D), lambda b, i: (b, i, 0)),
                  pl.BlockSpec((1, D), lambda b, i: (0, 0)),
                  pl.BlockSpec((tm, tm), lambda b, i: (0, 0)),
                  pl.BlockSpec((D, N), lambda b, i: (0, 0))],
        out_specs=pl.BlockSpec((1, r, n, N), lambda b, i: (b, 0, i, 0)),
        compiler_params=_cparams(("parallel", "parallel")),
        name=f"inproj_perm_{r}",
    )(x3, g, perm, w)


def _attn_a_kernel(lam_ref, cfar_ref, q_ref, k_ref, vt_ref, bias_ref, g_ref, o_ref, *, out_scale):
    h, qi = pl.program_id(1), pl.program_id(2)
    t = T_ATT
    nk = k_ref.shape[1] // t
    q = q_ref[0]
    low_half = lax.broadcasted_iota(jnp.int32, (1, LANES), 1) < HALF_LANES
    zero = jnp.zeros_like(q)
    qs = jnp.concatenate([jnp.where(low_half, q, zero), jnp.where(low_half, zero, q)], axis=0)

    blocks, sts, shifts = [], [], []
    m = None
    for d in range(-1, nk - 1):
        a = lax.rem(qi + (d + nk), nk)
        delta = a - qi
        kb = k_ref[0, pl.ds(pl.multiple_of(a * t, t), t), :]
        st = lax.dot_general(kb, qs, (((1,), (1,)), ((), ())), preferred_element_type=F32)
        if d <= 1:
            tile = bias_ref[0, jnp.clip(delta, -2, 2) + 2]
            st = st + jnp.concatenate([tile, tile], axis=1)
            shift = None
            cm = jnp.max(st, axis=0, keepdims=True)
        else:
            shift = cfar_ref[2 * h + (delta > 0).astype(jnp.int32)]
            cm = jnp.max(st, axis=0, keepdims=True) + shift
        m = cm if m is None else jnp.maximum(m, cm)
        blocks.append(a)
        sts.append(st)
        shifts.append(shift)

    l = jnp.zeros_like(m)
    accs = [jnp.zeros((LANES, t), F32), jnp.zeros((LANES, t), F32)]
    for a, st, shift in zip(blocks, sts, shifts):
        p = jnp.exp2(st - (m if shift is None else m - shift))
        l = l + jnp.sum(p, axis=0, keepdims=True)
        pb = p.astype(BF16)
        vt = vt_ref[0, 0, a]
        for c in range(2):
            accs[c] = accs[c] + jnp.dot(vt, pb[:, c * t:(c + 1) * t], preferred_element_type=F32)

    ot = accs[0] / l[:, :t] - lam_ref[0] * (accs[1] / l[:, t:])
    o = ot.T
    ms = jnp.mean(o * o, axis=-1, keepdims=True)
    o_ref[0] = (o * lax.rsqrt(ms + EPS) * g_ref[...] * out_scale).astype(o_ref.dtype)


def _attn_a(proj3, vt, lam, cfar, bias5, subln_g, lam_init):
    B, S, _ = proj3.shape
    t = T_ATT
    nk = S // t
    kern = functools.partial(_attn_a_kernel, out_scale=1.0 - lam_init)
    return pl.pallas_call(
        kern,
        out_shape=jax.ShapeDtypeStruct((B, S, HA * 2 * DA), BF16),
        grid=(B, HA, nk),
        in_specs=[
            pl.BlockSpec(memory_space=pltpu.SMEM),
            pl.BlockSpec(memory_space=pltpu.SMEM),
            pl.BlockSpec((1, t, LANES), lambda b, h, qi: (b, qi, h)),
            pl.BlockSpec((1, S, LANES), lambda b, h, qi: (b, 0, HA + h)),
            pl.BlockSpec((1, 1, nk, LANES, t), lambda b, h, qi: (b, h, 0, 0, 0)),
            pl.BlockSpec((1, 5, t, t), lambda b, h, qi: (h, 0, 0, 0)),
            pl.BlockSpec((1, LANES), lambda b, h, qi: (0, 0)),
        ],
        out_specs=pl.BlockSpec((1, t, LANES), lambda b, h, qi: (b, qi, h)),
        compiler_params=_cparams(("parallel", "parallel", "arbitrary")),
        name="diff_attn",
    )(lam, cfar, proj3, proj3, vt, bias5, subln_g)


def _attn_a_bias(rel_bias):
    t = T_ATT
    tab = rel_bias[:, :HA].astype(F32) * LOG2E
    d = jnp.arange(-2, 3, dtype=jnp.int32)[:, None, None] * t
    rel = d + jnp.arange(t, dtype=jnp.int32)[None, :, None] - jnp.arange(t, dtype=jnp.int32)[None, None, :]
    tiles = _bias_lookup(_t5_bucket(rel), tab)
    far = _t5_bucket(jnp.array([-(t + 1), t + 1], dtype=jnp.int32))
    cfar = tab[far].T.reshape(2 * HA)
    return tiles, cfar


def _attn_b_kernel(q_ref, k_ref, v_ref, bias_ref, o_ref, lse_ref, *, sub_len, r):
    i, s = pl.program_id(1), pl.program_id(2)
    start = jnp.clip(i * QB_DIL - HALF_WIN, 0, sub_len - KW_DIL)
    start = pl.multiple_of(start, HALF_WIN)
    q = q_ref[0, 0]
    kw = k_ref[0, s, pl.ds(start, KW_DIL), :]
    vw = v_ref[0, s, pl.ds(start, KW_DIL), :]
    low_half = lax.broadcasted_iota(jnp.int32, (1, LANES), 1) < HALF_LANES
    rows = slice(None) if r == 1 else pl.ds(s, QB_DIL, stride=r)
    for j in range(HB // 2):
        cols = slice(j * LANES, (j + 1) * LANES)
        qp, kp, vp = q[:, cols], kw[:, cols], vw[:, cols]
        outs, lses = [], []
        for c in range(2):
            qc = jnp.where(low_half if c == 0 else jnp.logical_not(low_half), qp, jnp.zeros_like(qp))
            sc = lax.dot_general(qc, kp, (((1,), (1,)), ((), ())), preferred_element_type=F32)
            sc = sc + bias_ref[2 * j + c, 0]
            m = jnp.max(sc, axis=-1, keepdims=True)
            p = jnp.exp(sc - m)
            l = jnp.sum(p, axis=-1, keepdims=True)
            outs.append(jnp.dot(p.astype(BF16), vp, preferred_element_type=F32) / l)
            lses.append(m + jnp.log(l))
        o_ref[0, j, rows, :] = jnp.where(low_half, outs[0], outs[1])
        lse_ref[0, j, rows, :] = jnp.where(low_half, lses[0], lses[1])


def _attn_b(qkv4, bias3, g, cols):
    B, r, L, _ = qkv4.shape
    S = r * L
    width = HB * DB
    nblk = L // QB_DIL
    qcol, kcol, vcol = cols
    kern = functools.partial(_attn_b_kernel, sub_len=L, r=r)
    slab = jax.ShapeDtypeStruct((B, N_SLABS, S, LANES), F32)
    slab_spec = pl.BlockSpec((1, N_SLABS, QB_DIL * r, LANES), lambda b, i, s: (b, 0, i, 0))
    return pl.pallas_call(
        kern,
        out_shape=[slab, slab],
        grid=(B, nblk, r),
        in_specs=[
            pl.BlockSpec((1, 1, QB_DIL, width), lambda b, i, s: (b, s, i, qcol)),
            pl.BlockSpec((1, r, L, width), lambda b, i, s: (b, 0, 0, kcol)),
            pl.BlockSpec((1, r, L, width), lambda b, i, s: (b, 0, 0, vcol)),
            pl.BlockSpec((HB, 1, QB_DIL, KW_DIL),
                         lambda b, i, s: (0, jnp.where(i == 0, 0, jnp.where(i == nblk - 1, 2, 1)), 0, 0)),
        ],
        out_specs=[slab_spec, slab_spec],
        compiler_params=_cparams(("parallel", "arbitrary", "arbitrary")),
        name=f"dilated_attn_{g}",
    )(qkv4, qkv4, qkv4, bias3)


def _attn_b_bias(rel_bias, g):
    r = DILATIONS[g]
    tab = rel_bias[:, HA + g * HB: HA + (g + 1) * HB].astype(F32)
    off = jnp.arange(3, dtype=jnp.int32)[:, None, None] * HALF_WIN
    rel = (jnp.arange(KW_DIL, dtype=jnp.int32)[None, None, :] - off
           - jnp.arange(QB_DIL, dtype=jnp.int32)[None, :, None])
    bias = _bias_lookup(_t5_bucket(rel * r), tab)
    return jnp.where((jnp.abs(rel) <= HALF_WIN)[None], bias, NEG)


def _sgu_kernel(zu_ref, zv_ref, lng_ref, lnb_ref, ws_ref, bs_ref, o_ref):
    u = jax.nn.gelu(zu_ref[...].astype(F32))
    v = jax.nn.gelu(zv_ref[...].astype(F32))
    mu = jnp.mean(v, axis=-1, keepdims=True)
    var = jnp.mean(jnp.square(v - mu), axis=-1, keepdims=True)
    v = ((v - mu) * lax.rsqrt(var + EPS) * lng_ref[...] + lnb_ref[...]).astype(BF16)
    gd = v.shape[1] // C_GROUPS
    for n in range(v.shape[0] // CHUNK):
        rows = slice(n * CHUNK, (n + 1) * CHUNK)
        for g in range(C_GROUPS):
            cols = slice(g * gd, (g + 1) * gd)
            mixed = jnp.dot(ws_ref[g], v[rows, cols], preferred_element_type=F32) + bs_ref[:, cols]
            o_ref[rows, cols] = (u[rows, cols] * mixed).astype(o_ref.dtype)


def _sgu(proj2, ln_g, ln_b, w_s, b_exp):
    T = proj2.shape[0]
    tm = min(TM_SGU, T)
    w = MIX_W
    return pl.pallas_call(
        _sgu_kernel,
        out_shape=jax.ShapeDtypeStruct((T, w), BF16),
        grid=(T // tm,),
        in_specs=[pl.BlockSpec((tm, w), lambda i: (i, COL_ZU)),
                  pl.BlockSpec((tm, w), lambda i: (i, COL_ZU + 1)),
                  pl.BlockSpec((1, w), lambda i: (0, 0)),
                  pl.BlockSpec((1, w), lambda i: (0, 0)),
                  pl.BlockSpec((C_GROUPS, CHUNK, CHUNK), lambda i: (0, 0, 0)),
                  pl.BlockSpec((CHUNK, w), lambda i: (0, 0))],
        out_specs=pl.BlockSpec((tm, w), lambda i: (i, 0)),
        compiler_params=_cparams(("parallel",)),
        name="sgu",
    )(proj2, proj2, ln_g, ln_b, w_s, b_exp)


def _route(logits):
    lane = lax.broadcasted_iota(jnp.int32, logits.shape, 1)
    big = jnp.int32(LANES)
    is_grp = (lane >= N_EXPERTS) & (lane < N_EXPERTS + N_GROUPS)
    gl = jnp.where(is_grp, logits, NEG)
    gmax = jnp.max(gl, axis=-1, keepdims=True)
    g_idx = jnp.min(jnp.where(is_grp & (gl == gmax), lane, big), axis=-1, keepdims=True) - N_EXPERTS
    g_w = 1.0 / jnp.sum(jnp.where(is_grp, jnp.exp(gl - gmax), 0.0), axis=-1, keepdims=True)
    in_grp = (lane >= g_idx * E_PER_GROUP) & (lane < (g_idx + 1) * E_PER_GROUP)
    sel = jnp.where(in_grp, logits, NEG)
    v1 = jnp.max(sel, axis=-1, keepdims=True)
    i1 = jnp.min(jnp.where(in_grp & (sel == v1), lane, big), axis=-1, keepdims=True)
    rest = in_grp & (lane != i1)
    sel2 = jnp.where(rest, logits, NEG)
    v2 = jnp.max(sel2, axis=-1, keepdims=True)
    i2 = jnp.min(jnp.where(rest & (sel2 == v2), lane, big), axis=-1, keepdims=True)
    e2 = jnp.exp(v2 - v1)
    w1 = g_w / (1.0 + e2)
    w2 = g_w * e2 / (1.0 + e2)
    return jnp.where(lane == i1, w1, jnp.where(lane == i2, w2, 0.0))


def _mix_kernel(x_ref, ya_ref, ob0_ref, ob1_ref, ob2_ref, ls0_ref, ls1_ref, ls2_ref, yc_ref,
                g0_ref, g1_ref, g2_ref, wb_ref, wo_ref, nf_ref, wr_ref, br_ref,
                xo_ref, h_ref, comb_ref):
    slabs = []
    for j in range(N_SLABS):
        ls0, ls1, ls2 = ls0_ref[0, j], ls1_ref[0, j], ls2_ref[0, j]
        mx = jnp.maximum(jnp.maximum(ls0, ls1), ls2)
        e0, e1, e2 = jnp.exp(ls0 - mx), jnp.exp(ls1 - mx), jnp.exp(ls2 - mx)
        yb = (e0 * ob0_ref[0, j] + e1 * ob1_ref[0, j] + e2 * ob2_ref[0, j]) / (e0 + e1 + e2)
        slabs.append(yb.astype(BF16))
    yb = jnp.concatenate(slabs, axis=-1)
    merged = jax.nn.sigmoid(g0_ref[...].astype(F32)) * jnp.dot(ya_ref[...], wb_ref[0],
                                                               preferred_element_type=F32)
    merged += jax.nn.sigmoid(g1_ref[...].astype(F32)) * jnp.dot(yb, wb_ref[1],
                                                                preferred_element_type=F32)
    merged += jax.nn.sigmoid(g2_ref[...].astype(F32)) * jnp.dot(yc_ref[...], wb_ref[2],
                                                                preferred_element_type=F32)
    xn = x_ref[...] + jnp.dot(merged.astype(BF16), wo_ref[...], preferred_element_type=F32)
    xo_ref[...] = xn
    h = _rms_bf16(xn, nf_ref[...])
    h_ref[...] = h
    logits = jnp.dot(h, wr_ref[...], preferred_element_type=F32) + br_ref[...]
    comb_ref[...] = _route(logits)


def _mix(x2, ya, obs, lses, yc, proj2, wb, wo, nf, wr, br):
    T, D = x2.shape
    S = obs[0].shape[2]
    tm = min(TM_MIX, S)
    per_b = S // tm
    w = MIX_W
    row = lambda width: pl.BlockSpec((tm, width), lambda i: (i, 0))
    full = lambda a: pl.BlockSpec(a.shape, lambda i: (0,) * a.ndim)
    gate = lambda n: pl.BlockSpec((tm, D), lambda i: (i, COL_GATE + n))
    slab = pl.BlockSpec((1, N_SLABS, tm, LANES), lambda i: (i // per_b, 0, i % per_b, 0))
    return pl.pallas_call(
        _mix_kernel,
        out_shape=[jax.ShapeDtypeStruct((T, D), F32), jax.ShapeDtypeStruct((T, D), BF16),
                   jax.ShapeDtypeStruct((T, LANES), F32)],
        grid=(T // tm,),
        in_specs=[row(D), row(w), slab, slab, slab, slab, slab, slab, row(w),
                  gate(0), gate(1), gate(2), full(wb), full(wo), full(nf), full(wr), full(br)],
        out_specs=[row(D), row(D), row(LANES)],
        compiler_params=_cparams(("parallel",)),
        name="mix",
    )(x2, ya, obs[0], obs[1], obs[2], lses[0], lses[1], lses[2], yc, proj2, proj2, proj2,
      wb, wo, nf, wr, br)


def _moe_kernel(h_ref, comb_ref, x_ref, wg_ref, wu_ref, wd_ref, nfin_ref, o_ref, acc_scr,
                *, final_norm):
    e = pl.program_id(1)

    @pl.when(e == 0)
    def _():
        acc_scr[...] = jnp.zeros(acc_scr.shape, F32)

    h = h_ref[...]
    lane = lax.broadcasted_iota(jnp.int32, comb_ref.shape, 1)
    c = jnp.sum(jnp.where(lane == e, comb_ref[...], 0.0), axis=-1, keepdims=True)
    hid = (jax.nn.silu(jnp.dot(h, wg_ref[0], preferred_element_type=F32))
           * jnp.dot(h, wu_ref[0], preferred_element_type=F32))
    acc_scr[...] += c * jnp.dot(hid.astype(BF16), wd_ref[0], preferred_element_type=F32)

    @pl.when(e == pl.num_programs(1) - 1)
    def _():
        xn = x_ref[...] + acc_scr[...]
        if final_norm:
            ms = jnp.mean(xn * xn, axis=-1, keepdims=True)
            xn = xn * lax.rsqrt(ms + EPS) * nfin_ref[...]
        o_ref[...] = xn


def _moe_dense(h, comb, x2, wg, wu, wd, nfin, final_norm):
    T, D = x2.shape
    tm = min(TM_MOE, T)
    F = wg.shape[2]
    kern = functools.partial(_moe_kernel, final_norm=final_norm)
    return pl.pallas_call(
        kern,
        out_shape=jax.ShapeDtypeStruct((T, D), F32),
        grid=(T // tm, N_EXPERTS),
        in_specs=[pl.BlockSpec((tm, D), lambda i, e: (i, 0)),
                  pl.BlockSpec((tm, LANES), lambda i, e: (i, 0)),
                  pl.BlockSpec((tm, D), lambda i, e: (i, 0)),
                  pl.BlockSpec((1, D, F), lambda i, e: (e, 0, 0)),
                  pl.BlockSpec((1, D, F), lambda i, e: (e, 0, 0)),
                  pl.BlockSpec((1, F, D), lambda i, e: (e, 0, 0)),
                  pl.BlockSpec((1, D), lambda i, e: (0, 0))],
        out_specs=pl.BlockSpec((tm, D), lambda i, e: (i, 0)),
        scratch_shapes=[pltpu.VMEM((tm, D), F32)],
        compiler_params=_cparams(("parallel", "arbitrary")),
        name="moe_dense",
    )(h, comb, x2, wg, wu, wd, nfin)


def _moe_dispatch_kernel(h_ref, comb_ref, o_ref, cnt_ref):
    tm = h_ref.shape[0]
    comb = comb_ref[...]
    hi = comb.astype(BF16)
    lo = (comb - hi.astype(F32)).astype(BF16)
    haug = jnp.concatenate([h_ref[...], hi, lo], axis=1)
    a_t = comb.T[:N_EXPERTS] > 0.0
    a_f = jnp.where(a_t, 1.0, 0.0)
    before = (lax.broadcasted_iota(jnp.int32, (tm, tm), 0)
              < lax.broadcasted_iota(jnp.int32, (tm, tm), 1))
    rank_t = jnp.dot(a_f.astype(BF16), jnp.where(before, 1.0, 0.0).astype(BF16),
                     preferred_element_type=F32)
    slot = lax.broadcasted_iota(jnp.int32, (MOE_CAP, tm), 0).astype(F32)
    blocks = [jnp.where((slot == rank_t[e:e + 1]) & a_t[e:e + 1], 1.0, 0.0).astype(BF16)
              for e in range(N_EXPERTS)]
    res = jnp.dot(jnp.concatenate(blocks, axis=0), haug, preferred_element_type=F32)
    res = res.astype(o_ref.dtype)
    for e in range(N_EXPERTS):
        o_ref[0, e] = res[e * MOE_CAP:(e + 1) * MOE_CAP]
    cnt_ref[0] = jnp.broadcast_to(jnp.max(jnp.sum(a_f, axis=1, keepdims=True)), cnt_ref.shape[1:])


def _moe_dispatch(h, comb):
    T, D = h.shape
    tm = min(TM_DISP, T)
    n = T // tm
    return pl.pallas_call(
        _moe_dispatch_kernel,
        out_shape=[jax.ShapeDtypeStruct((n, N_EXPERTS, MOE_CAP, D + 2 * LANES), BF16),
                   jax.ShapeDtypeStruct((n, 8, LANES), F32)],
        grid=(n,),
        in_specs=[pl.BlockSpec((tm, D), lambda i: (i, 0)),
                  pl.BlockSpec((tm, LANES), lambda i: (i, 0))],
        out_specs=[pl.BlockSpec((1, N_EXPERTS, MOE_CAP, D + 2 * LANES), lambda i: (i, 0, 0, 0)),
                   pl.BlockSpec((1, 8, LANES), lambda i: (i, 0, 0))],
        compiler_params=_cparams(("parallel",)),
        name="moe_dispatch",
    )(h, comb)


def _moe_ffn_kernel(s_ref, wg_ref, wu_ref, wd_ref, o_ref):
    e = pl.program_id(0)
    g = s_ref.shape[0]
    D = o_ref.shape[-1]
    rows = jnp.concatenate([s_ref[t, 0] for t in range(g)], axis=0)
    h = rows[:, :D]
    wparts = rows[:, D:].astype(F32)
    lane = lax.broadcasted_iota(jnp.int32, wparts.shape, 1)
    w = jnp.sum(jnp.where(lane % LANES == e, wparts, 0.0), axis=-1, keepdims=True)
    hid = (jax.nn.silu(jnp.dot(h, wg_ref[0], preferred_element_type=F32))
           * jnp.dot(h, wu_ref[0], preferred_element_type=F32))
    y = (w * jnp.dot(hid.astype(BF16), wd_ref[0], preferred_element_type=F32)).astype(o_ref.dtype)
    cap = s_ref.shape[2]
    for t in range(g):
        o_ref[t, 0] = y[t * cap:(t + 1) * cap]


def _moe_ffn(srt, wg, wu, wd):
    n, ne, cap, wdt = srt.shape
    D, F = wg.shape[1], wg.shape[2]
    g = min(G_FFN, n)
    return pl.pallas_call(
        _moe_ffn_kernel,
        out_shape=jax.ShapeDtypeStruct((n, ne, cap, D), BF16),
        grid=(ne, n // g),
        in_specs=[pl.BlockSpec((g, 1, cap, wdt), lambda e, c: (c, e, 0, 0)),
                  pl.BlockSpec((1, D, F), lambda e, c: (e, 0, 0)),
                  pl.BlockSpec((1, D, F), lambda e, c: (e, 0, 0)),
                  pl.BlockSpec((1, F, D), lambda e, c: (e, 0, 0))],
        out_specs=pl.BlockSpec((g, 1, cap, D), lambda e, c: (c, e, 0, 0)),
        compiler_params=_cparams(("parallel", "parallel")),
        name="moe_ffn",
    )(srt, wg, wu, wd)


def _moe_combine_kernel(y_ref, comb_ref, x_ref, nfin_ref, o_ref, *, final_norm):
    tm = x_ref.shape[0]
    comb = comb_ref[...]
    a = jnp.where(comb > 0.0, 1.0, 0.0)
    before = (lax.broadcasted_iota(jnp.int32, (tm, tm), 1)
              < lax.broadcasted_iota(jnp.int32, (tm, tm), 0))
    rank = jnp.dot(jnp.where(before, 1.0, 0.0).astype(BF16), a.astype(BF16),
                   preferred_element_type=F32)
    lane = lax.broadcasted_iota(jnp.int32, (tm, LANES), 1)
    first = lane < MOE_CAP
    lane_f = lane.astype(F32)
    pieces = []
    for k in range(N_EXPERTS // 2):
        e0, e1 = 2 * k, 2 * k + 1
        target = jnp.where(first, rank[:, e0:e0 + 1], rank[:, e1:e1 + 1] + MOE_CAP)
        active = jnp.where(first, a[:, e0:e0 + 1], a[:, e1:e1 + 1])
        pieces.append(jnp.where((lane_f == target) & (active > 0.0), 1.0, 0.0).astype(BF16))
    pc = jnp.concatenate(pieces, axis=1)
    y = jnp.concatenate([y_ref[0, e] for e in range(N_EXPERTS)], axis=0)
    xn = x_ref[...] + jnp.dot(pc, y, preferred_element_type=F32)
    if final_norm:
        ms = jnp.mean(xn * xn, axis=-1, keepdims=True)
        xn = xn * lax.rsqrt(ms + EPS) * nfin_ref[...]
    o_ref[...] = xn


def _moe_combine(y, comb, x2, nfin, final_norm):
    T, D = x2.shape
    n, ne, cap, _ = y.shape
    tm = T // n
    kern = functools.partial(_moe_combine_kernel, final_norm=final_norm)
    return pl.pallas_call(
        kern,
        out_shape=jax.ShapeDtypeStruct((T, D), F32),
        grid=(n,),
        in_specs=[pl.BlockSpec((1, ne, cap, D), lambda i: (i, 0, 0, 0)),
                  pl.BlockSpec((tm, LANES), lambda i: (i, 0)),
                  pl.BlockSpec((tm, D), lambda i: (i, 0)),
                  pl.BlockSpec((1, D), lambda i: (0, 0))],
        out_specs=pl.BlockSpec((tm, D), lambda i: (i, 0)),
        compiler_params=_cparams(("parallel",)),
        name="moe_combine",
    )(y, comb, x2, nfin)


def _moe_fix_kernel(ids_ref, n_ref, h_ref, comb_ref, x_ref, wg_ref, wu_ref, wd_ref, nfin_ref,
                    prev_ref, o_ref, acc_scr, *, final_norm):
    del ids_ref, prev_ref

    @pl.when(pl.program_id(0) < n_ref[0])
    def _():
        _moe_kernel(h_ref, comb_ref, x_ref, wg_ref, wu_ref, wd_ref, nfin_ref, o_ref, acc_scr,
                    final_norm=final_norm)


def _moe_fix(ids, n_ovf, out, h, comb, x2, wg, wu, wd, nfin, final_norm):
    T, D = x2.shape
    tm = min(TM_DISP, T)
    F = wg.shape[2]
    last_e = N_EXPERTS - 1
    tile = lambda width: pl.BlockSpec((tm, width), lambda s, e, ids, n: (ids[s], 0))
    wspec = lambda shape: pl.BlockSpec(
        shape, lambda s, e, ids, n: (jnp.where(s < n[0], e, last_e), 0, 0))
    kern = functools.partial(_moe_fix_kernel, final_norm=final_norm)
    return pl.pallas_call(
        kern,
        out_shape=jax.ShapeDtypeStruct((T, D), F32),
        grid_spec=pltpu.PrefetchScalarGridSpec(
            num_scalar_prefetch=2,
            grid=(MAX_OVF, N_EXPERTS),
            in_specs=[tile(D), tile(LANES), tile(D), wspec((1, D, F)), wspec((1, D, F)),
                      wspec((1, F, D)), pl.BlockSpec((1, D), lambda s, e, ids, n: (0, 0)),
                      pl.BlockSpec(memory_space=pl.ANY)],
            out_specs=tile(D),
            scratch_shapes=[pltpu.VMEM((tm, D), F32)]),
        input_output_aliases={9: 0},
        compiler_params=_cparams(("arbitrary", "arbitrary")),
        name="moe_fix",
    )(ids, n_ovf, h, comb, x2, wg, wu, wd, nfin, out)


def _moe(h, comb, x2, wg, wu, wd, nfin, final_norm):
    srt, cnt = _moe_dispatch(h, comb)
    over = cnt[:, 0, 0] > MOE_CAP
    n_ovf = jnp.sum(over.astype(jnp.int32))
    ids = jnp.nonzero(over, size=MAX_OVF, fill_value=0)[0].astype(jnp.int32)
    ids = jnp.where(jnp.arange(MAX_OVF) < n_ovf, ids, ids[jnp.clip(n_ovf - 1, 0, MAX_OVF - 1)])

    def routed():
        out = _moe_combine(_moe_ffn(srt, wg, wu, wd), comb, x2, nfin, final_norm)
        return lax.cond(
            n_ovf > 0,
            lambda: _moe_fix(ids, n_ovf.reshape(1), out, h, comb, x2, wg, wu, wd, nfin, final_norm),
            lambda: out)

    return lax.cond(n_ovf > MAX_OVF,
                    lambda: _moe_dense(h, comb, x2, wg, wu, wd, nfin, final_norm), routed)


def kernel(x, rel_bias, norm_mix, w_in, diff_lambda, diff_subln, sgu_ln_g, sgu_ln_b, sgu_w, sgu_b,
           w_branch, w_out, norm_ffn, w_router_grp, b_router_grp, w_router_exp, b_router_exp,
           w_gate, w_up, w_down, norm_final):
    B, S, D = x.shape
    T = B * S
    depth = w_in.shape[0]
    a_out = HA * 2 * DA
    grp_w = HB * DB
    b_cols = 3 * NG_B * grp_w
    qkv_b0 = 3 * a_out
    zc0 = qkv_b0 + b_cols
    gate0 = zc0 + 2 * MIX_W
    qk_scale = DA ** -0.5

    bias_a, cfar = _attn_a_bias(rel_bias)
    bias_b = [_attn_b_bias(rel_bias, g) for g in range(NG_B)]

    def group_cols(w, g):
        q = w[:, qkv_b0 + g * grp_w: qkv_b0 + (g + 1) * grp_w] * qk_scale
        k = w[:, qkv_b0 + (NG_B + g) * grp_w: qkv_b0 + (NG_B + g + 1) * grp_w]
        v = w[:, qkv_b0 + (2 * NG_B + g) * grp_w: qkv_b0 + (2 * NG_B + g + 1) * grp_w]
        return [q, k, v]

    x2 = x.reshape(T, D)
    for i in range(depth):
        w = w_in[i]
        nm = norm_mix[i][None, :]
        w_main = jnp.concatenate([w[:, :a_out] * (qk_scale * LOG2E), w[:, a_out:2 * a_out],
                                  w[:, zc0:]] + group_cols(w, 0), axis=1).astype(BF16)
        proj2 = _inproj(x2, nm, w_main)
        x3 = x2.reshape(B, S, D)
        vt = _inproj_t(x3, nm, w[:, 2 * a_out:3 * a_out].T.astype(BF16))
        proj3 = proj2.reshape(B, S, proj2.shape[1])

        lam_init = 0.8 - 0.6 * math.exp(-0.3 * i)
        lp = diff_lambda[i].astype(F32)
        lam = jnp.exp(jnp.sum(lp[0] * lp[1])) - jnp.exp(jnp.sum(lp[2] * lp[3])) + lam_init
        ya = _attn_a(proj3, vt, lam.reshape(1), cfar, bias_a, diff_subln[i][None, :], lam_init)

        obs, lses = [], []
        for g in range(NG_B):
            r = DILATIONS[g]
            if r == 1:
                qkv4, cols = proj3[:, None], (COL_QKV0, COL_QKV0 + 1, COL_QKV0 + 2)
            else:
                w_g = jnp.concatenate(group_cols(w, g), axis=1).astype(BF16)
                qkv4, cols = _inproj_perm(x3, nm, w_g, r), (0, 1, 2)
            o, l = _attn_b(qkv4, bias_b[g], g, cols)
            obs.append(o)
            lses.append(l)

        b_exp = jnp.repeat(sgu_b[i].T, MIX_W // C_GROUPS, axis=1)
        yc = _sgu(proj2, sgu_ln_g[i][None, :], sgu_ln_b[i][None, :], sgu_w[i].astype(BF16), b_exp)

        wr = jnp.concatenate([w_router_exp[i].transpose(1, 0, 2).reshape(D, N_EXPERTS),
                              w_router_grp[i]], axis=1)
        wr = jnp.pad(wr, ((0, 0), (0, LANES - wr.shape[1]))).astype(BF16)
        br = jnp.concatenate([b_router_exp[i].reshape(N_EXPERTS), b_router_grp[i]])
        br = jnp.pad(br, (0, LANES - br.shape[0]))[None, :].astype(F32)

        x2, h, comb = _mix(x2, ya.reshape(T, a_out), obs, lses, yc, proj2,
                           w_branch[i].astype(BF16), w_out[i].astype(BF16), norm_ffn[i][None, :],
                           wr, br)
        x2 = _moe(h, comb, x2, w_gate[i].astype(BF16), w_up[i].astype(BF16),
                  w_down[i].astype(BF16), norm_final[None, :], i == depth - 1)
    return x2.reshape(B, S, D)
```

```python
import functools
import math

import jax
import jax.numpy as jnp
from jax import lax
from jax.experimental import pallas as pl
from jax.experimental.pallas import tpu as pltpu

F32 = jnp.float32
BF16 = jnp.bfloat16

EPS = 1e-6
NEG = -1e30
LOG2E = 1.4426950408889634
LANES = 128
HALF_LANES = LANES // 2
VMEM_LIMIT = 48 * 1024 * 1024

HA = 4
DA = 64
MIX_W = 512
WINDOWS = (128, 512, 2048)
DILATIONS = (1, 4, 16)
NG_B = 3
HB = 8
DB = 64
HALF_WIN = 64
CHUNK = 128
C_GROUPS = 4
N_BRANCH = 3
N_BUCKETS = 32
MAX_DIST = 128
N_GROUPS = 4
E_PER_GROUP = 4
N_EXPERTS = N_GROUPS * E_PER_GROUP
N_SLABS = MIX_W // LANES

TM_PROJ = 1024
TN_PROJ = 3328
TM_PERM = 512
T_ATT = 512
QB_DIL = 128
KW_DIL = QB_DIL + 2 * HALF_WIN
ITEMS_DIL = 4
TM_SGU = 512
TM_MIX = 512
TM_MOE = 1024
TM_DISP = 256
MOE_CAP = HALF_LANES
G_FFN = 8
MAX_OVF = 16

COL_ZU = 2
COL_GATE = 2
COL_QKV0 = 10


def _cparams(sem):
    return pltpu.CompilerParams(dimension_semantics=sem, vmem_limit_bytes=VMEM_LIMIT)


def _t5_bucket(rel):
    nb = N_BUCKETS // 2
    max_exact = nb // 2
    ret = (rel > 0).astype(jnp.int32) * nb
    n = jnp.abs(rel)
    nf = jnp.maximum(n, 1).astype(F32)
    large = max_exact + (jnp.log(nf / max_exact) / math.log(MAX_DIST / max_exact)
                         * (nb - max_exact)).astype(jnp.int32)
    large = jnp.minimum(large, nb - 1)
    return ret + jnp.where(n < max_exact, n, large)


def _bias_lookup(bucket, tab):
    out = jnp.zeros((tab.shape[1],) + bucket.shape, F32)
    expand = (slice(None),) + (None,) * bucket.ndim
    for b in range(N_BUCKETS):
        out = jnp.where(bucket[None] == b, tab[b][expand], out)
    return out


def _rms_bf16(x, g):
    ms = jnp.mean(x * x, axis=-1, keepdims=True)
    return (x * lax.rsqrt(ms + EPS) * g).astype(BF16)


def _inproj_kernel(x_ref, g_ref, w_ref, o_ref, h_scr):
    @pl.when(pl.program_id(1) == 0)
    def _():
        h_scr[...] = _rms_bf16(x_ref[...], g_ref[...])

    o_ref[...] = jnp.dot(h_scr[...], w_ref[...], preferred_element_type=F32).astype(o_ref.dtype)


def _inproj(x2, g, w):
    T, D = x2.shape
    N = w.shape[1]
    tm = min(TM_PROJ, T)
    return pl.pallas_call(
        _inproj_kernel,
        out_shape=jax.ShapeDtypeStruct((T, N), BF16),
        grid=(T // tm, N // TN_PROJ),
        in_specs=[pl.BlockSpec((tm, D), lambda i, j: (i, 0)),
                  pl.BlockSpec((1, D), lambda i, j: (0, 0)),
                  pl.BlockSpec((D, TN_PROJ), lambda i, j: (0, j))],
        out_specs=pl.BlockSpec((tm, TN_PROJ), lambda i, j: (i, j)),
        scratch_shapes=[pltpu.VMEM((tm, D), BF16)],
        compiler_params=_cparams(("parallel", "arbitrary")),
        name="inproj",
    )(x2, g, w)


def _inproj_t_kernel(x_ref, g_ref, wt_ref, o_ref):
    h = _rms_bf16(x_ref[0], g_ref[...])
    res = lax.dot_general(wt_ref[...], h, (((1,), (1,)), ((), ())),
                          preferred_element_type=F32).astype(o_ref.dtype)
    for hd in range(o_ref.shape[1]):
        for n in range(o_ref.shape[2]):
            o_ref[0, hd, n] = res[hd * LANES:(hd + 1) * LANES, n * T_ATT:(n + 1) * T_ATT]


def _inproj_t(x3, g, wt):
    B, S, D = x3.shape
    N = wt.shape[0]
    tm = min(TM_PROJ, S)
    nh, nb = N // LANES, tm // T_ATT
    return pl.pallas_call(
        _inproj_t_kernel,
        out_shape=jax.ShapeDtypeStruct((B, nh, S // T_ATT, LANES, T_ATT), BF16),
        grid=(B, S // tm),
        in_specs=[pl.BlockSpec((1, tm, D), lambda b, i: (b, i, 0)),
                  pl.BlockSpec((1, D), lambda b, i: (0, 0)),
                  pl.BlockSpec((N, D), lambda b, i: (0, 0))],
        out_specs=pl.BlockSpec((1, nh, nb, LANES, T_ATT), lambda b, i: (b, 0, i, 0, 0)),
        compiler_params=_cparams(("parallel", "parallel")),
        name="inproj_t",
    )(x3, g, wt)


def _inproj_perm_kernel(x_ref, g_ref, p_ref, w_ref, o_ref, *, r):
    h = _rms_bf16(x_ref[0], g_ref[...])
    hp = jnp.dot(p_ref[...], h, preferred_element_type=F32).astype(BF16)
    res = jnp.dot(hp, w_ref[...], preferred_element_type=F32).astype(o_ref.dtype)
    n = res.shape[0] // r
    for s in range(r):
        o_ref[0, s] = res[s * n:(s + 1) * n, :]


def _inproj_perm(x3, g, w, r):
    B, S, D = x3.shape
    N = w.shape[1]
    tm = min(TM_PERM, S)
    n = tm // r
    o = jnp.arange(tm, dtype=jnp.int32)
    src = (o % n) * r + o // n
    perm = (src[:, None] == jnp.arange(tm, dtype=jnp.int32)[None, :]).astype(BF16)
    kern = functools.partial(_inproj_perm_kernel, r=r)
    return pl.pallas_call(
        kern,
        out_shape=jax.ShapeDtypeStruct((B, r, S // r, N), BF16),
        grid=(B, S // tm),
        in_specs=[pl.BlockSpec((1, tm, D), lambda b, i: (b, i, 0)),
                  pl.BlockSpec((1, D), lambda b, i: (0, 0)),
                  pl.BlockSpec((tm, tm), lambda b, i: (0, 0)),
                  pl.BlockSpec((D, N), lambda b, i: (0, 0))],
        out_specs=pl.BlockSpec((1, r, n, N), lambda b, i: (b, 0, i, 0)),
        compiler_params=_cparams(("parallel", "parallel")),
        name=f"inproj_perm_{r}",
    )(x3, g, perm, w)


def _attn_a_kernel(lam_ref, cfar_ref, q_ref, k_ref, vt_ref, bias_ref, g_ref, o_ref, *, out_scale):
    h, qi = pl.program_id(1), pl.program_id(2)
    t = T_ATT
    nk = k_ref.shape[1] // t
    q = q_ref[0]
    low_half = lax.broadcasted_iota(jnp.int32, (1, LANES), 1) < HALF_LANES
    zero = jnp.zeros_like(q)
    qs = jnp.concatenate([jnp.where(low_half, q, zero), jnp.where(low_half, zero, q)], axis=0)

    blocks, sts, shifts = [], [], []
    m = None
    for d in range(-1, nk - 1):
        a = lax.rem(qi + (d + nk), nk)
        delta = a - qi
        kb = k_ref[0, pl.ds(pl.multiple_of(a * t, t), t), :]
        st = lax.dot_general(kb, qs, (((1,), (1,)), ((), ())), preferred_element_type=F32)
        if d <= 1:
            tile = bias_ref[0, jnp.clip(delta, -2, 2) + 2]
            st = st + jnp.concatenate([tile, tile], axis=1)
            shift = None
            cm = jnp.max(st, axis=0, keepdims=True)
        else:
            shift = cfar_ref[2 * h + (delta > 0).astype(jnp.int32)]
            cm = jnp.max(st, axis=0, keepdims=True) + shift
        m = cm if m is None else jnp.maximum(m, cm)
        blocks.append(a)
        sts.append(st)
        shifts.append(shift)

    l = jnp.zeros_like(m)
    accs = [jnp.zeros((LANES, t), F32), jnp.zeros((LANES, t), F32)]
    for a, st, shift in zip(blocks, sts, shifts):
        p = jnp.exp2(st - (m if shift is None else m - shift))
        l = l + jnp.sum(p, axis=0, keepdims=True)
        pb = p.astype(BF16)
        vt = vt_ref[0, 0, a]
        for c in range(2):
            accs[c] = accs[c] + jnp.dot(vt, pb[:, c * t:(c + 1) * t], preferred_element_type=F32)

    ot = accs[0] / l[:, :t] - lam_ref[0] * (accs[1] / l[:, t:])
    o = ot.T
    ms = jnp.mean(o * o, axis=-1, keepdims=True)
    o_ref[0] = (o * lax.rsqrt(ms + EPS) * g_ref[...] * out_scale).astype(o_ref.dtype)


def _attn_a(proj3, vt, lam, cfar, bias5, subln_g, lam_init):
    B, S, _ = proj3.shape
    t = T_ATT
    nk = S // t
    kern = functools.partial(_attn_a_kernel, out_scale=1.0 - lam_init)
    return pl.pallas_call(
        kern,
        out_shape=jax.ShapeDtypeStruct((B, S, HA * 2 * DA), BF16),
        grid=(B, HA, nk),
        in_specs=[
            pl.BlockSpec(memory_space=pltpu.SMEM),
            pl.BlockSpec(memory_space=pltpu.SMEM),
            pl.BlockSpec((1, t, LANES), lambda b, h, qi: (b, qi, h)),
            pl.BlockSpec((1, S, LANES), lambda b, h, qi: (b, 0, HA + h)),
            pl.BlockSpec((1, 1, nk, LANES, t), lambda b, h, qi: (b, h, 0, 0, 0)),
            pl.BlockSpec((1, 5, t, t), lambda b, h, qi: (h, 0, 0, 0)),
            pl.BlockSpec((1, LANES), lambda b, h, qi: (0, 0)),
        ],
        out_specs=pl.BlockSpec((1, t, LANES), lambda b, h, qi: (b, qi, h)),
        compiler_params=_cparams(("parallel", "parallel", "arbitrary")),
        name="diff_attn",
    )(lam, cfar, proj3, proj3, vt, bias5, subln_g)


def _attn_a_bias(rel_bias):
    t = T_ATT
    tab = rel_bias[:, :HA].astype(F32) * LOG2E
    d = jnp.arange(-2, 3, dtype=jnp.int32)[:, None, None] * t
    rel = d + jnp.arange(t, dtype=jnp.int32)[None, :, None] - jnp.arange(t, dtype=jnp.int32)[None, None, :]
    tiles = _bias_lookup(_t5_bucket(rel), tab)
    far = _t5_bucket(jnp.array([-(t + 1), t + 1], dtype=jnp.int32))
    cfar = tab[far].T.reshape(2 * HA)
    return tiles, cfar


def _attn_b_kernel(q_ref, k_ref, v_ref, bias_ref, o_ref, lse_ref, *, sub_len, r, sp, qp):
    nblk = sub_len // QB_DIL
    low_half = lax.broadcasted_iota(jnp.int32, (1, LANES), 1) < HALF_LANES
    for si in range(sp):
        s = si if sp == r else pl.program_id(2) * sp + si
        for qb in range(qp):
            i = pl.program_id(1) * qp + qb
            start = jnp.clip(i * QB_DIL - HALF_WIN, 0, sub_len - KW_DIL)
            start = pl.multiple_of(start, HALF_WIN)
            variant = jnp.where(i == 0, 0, jnp.where(i == nblk - 1, 2, 1))
            q = q_ref[0, si, qb * QB_DIL:(qb + 1) * QB_DIL, :]
            kw = k_ref[0, s, pl.ds(start, KW_DIL), :]
            vw = v_ref[0, s, pl.ds(start, KW_DIL), :]
            rows = (slice(qb * QB_DIL, (qb + 1) * QB_DIL) if r == 1
                    else pl.ds(s, QB_DIL, stride=r))
            for j in range(HB // 2):
                cols = slice(j * LANES, (j + 1) * LANES)
                qpair, kp, vp = q[:, cols], kw[:, cols], vw[:, cols]
                outs, lses = [], []
                for c in range(2):
                    qc = jnp.where(low_half if c == 0 else jnp.logical_not(low_half), qpair,
                                   jnp.zeros_like(qpair))
                    sc = lax.dot_general(qc, kp, (((1,), (1,)), ((), ())),
                                         preferred_element_type=F32)
                    sc = sc + bias_ref[2 * j + c, variant]
                    m = jnp.max(sc, axis=-1, keepdims=True)
                    p = jnp.exp(sc - m)
                    l = jnp.sum(p, axis=-1, keepdims=True)
                    outs.append(jnp.dot(p.astype(BF16), vp, preferred_element_type=F32) / l)
                    lses.append(m + jnp.log(l))
                o_ref[0, j, rows, :] = jnp.where(low_half, outs[0], outs[1])
                lse_ref[0, j, rows, :] = jnp.where(low_half, lses[0], lses[1])


def _attn_b(qkv4, bias3, g, cols):
    B, r, L, _ = qkv4.shape
    S = r * L
    width = HB * DB
    nblk = L // QB_DIL
    sp = min(r, ITEMS_DIL)
    qp = ITEMS_DIL // sp
    qcol, kcol, vcol = cols
    kern = functools.partial(_attn_b_kernel, sub_len=L, r=r, sp=sp, qp=qp)
    slab = jax.ShapeDtypeStruct((B, N_SLABS, S, LANES), F32)
    slab_spec = pl.BlockSpec((1, N_SLABS, QB_DIL * r * qp, LANES), lambda b, i, s: (b, 0, i, 0))
    return pl.pallas_call(
        kern,
        out_shape=[slab, slab],
        grid=(B, nblk // qp, r // sp),
        in_specs=[
            pl.BlockSpec((1, sp, QB_DIL * qp, width), lambda b, i, s: (b, s, i, qcol)),
            pl.BlockSpec((1, r, L, width), lambda b, i, s: (b, 0, 0, kcol)),
            pl.BlockSpec((1, r, L, width), lambda b, i, s: (b, 0, 0, vcol)),
            pl.BlockSpec((HB, 3, QB_DIL, KW_DIL), lambda b, i, s: (0, 0, 0, 0)),
        ],
        out_specs=[slab_spec, slab_spec],
        compiler_params=_cparams(("parallel", "arbitrary", "arbitrary")),
        name=f"dilated_attn_{g}",
    )(qkv4, qkv4, qkv4, bias3)


def _attn_b_bias(rel_bias, g):
    r = DILATIONS[g]
    tab = rel_bias[:, HA + g * HB: HA + (g + 1) * HB].astype(F32)
    off = jnp.arange(3, dtype=jnp.int32)[:, None, None] * HALF_WIN
    rel = (jnp.arange(KW_DIL, dtype=jnp.int32)[None, None, :] - off
           - jnp.arange(QB_DIL, dtype=jnp.int32)[None, :, None])
    bias = _bias_lookup(_t5_bucket(rel * r), tab)
    return jnp.where((jnp.abs(rel) <= HALF_WIN)[None], bias, NEG)


def _sgu_kernel(zu_ref, zv_ref, lng_ref, lnb_ref, ws_ref, bs_ref, o_ref):
    u = jax.nn.gelu(zu_ref[...].astype(F32))
    v = jax.nn.gelu(zv_ref[...].astype(F32))
    mu = jnp.mean(v, axis=-1, keepdims=True)
    var = jnp.mean(jnp.square(v - mu), axis=-1, keepdims=True)
    v = ((v - mu) * lax.rsqrt(var + EPS) * lng_ref[...] + lnb_ref[...]).astype(BF16)
    gd = v.shape[1] // C_GROUPS
    for n in range(v.shape[0] // CHUNK):
        rows = slice(n * CHUNK, (n + 1) * CHUNK)
        for g in range(C_GROUPS):
            cols = slice(g * gd, (g + 1) * gd)
            mixed = jnp.dot(ws_ref[g], v[rows, cols], preferred_element_type=F32) + bs_ref[:, cols]
            o_ref[rows, cols] = (u[rows, cols] * mixed).astype(o_ref.dtype)


def _sgu(proj2, ln_g, ln_b, w_s, b_exp):
    T = proj2.shape[0]
    tm = min(TM_SGU, T)
    w = MIX_W
    return pl.pallas_call(
        _sgu_kernel,
        out_shape=jax.ShapeDtypeStruct((T, w), BF16),
        grid=(T // tm,),
        in_specs=[pl.BlockSpec((tm, w), lambda i: (i, COL_ZU)),
                  pl.BlockSpec((tm, w), lambda i: (i, COL_ZU + 1)),
                  pl.BlockSpec((1, w), lambda i: (0, 0)),
                  pl.BlockSpec((1, w), lambda i: (0, 0)),
                  pl.BlockSpec((C_GROUPS, CHUNK, CHUNK), lambda i: (0, 0, 0)),
                  pl.BlockSpec((CHUNK, w), lambda i: (0, 0))],
        out_specs=pl.BlockSpec((tm, w), lambda i: (i, 0)),
        compiler_params=_cparams(("parallel",)),
        name="sgu",
    )(proj2, proj2, ln_g, ln_b, w_s, b_exp)


def _route(logits):
    lane = lax.broadcasted_iota(jnp.int32, logits.shape, 1)
    big = jnp.int32(LANES)
    is_grp = (lane >= N_EXPERTS) & (lane < N_EXPERTS + N_GROUPS)
    gl = jnp.where(is_grp, logits, NEG)
    gmax = jnp.max(gl, axis=-1, keepdims=True)
    g_idx = jnp.min(jnp.where(is_grp & (gl == gmax), lane, big), axis=-1, keepdims=True) - N_EXPERTS
    g_w = 1.0 / jnp.sum(jnp.where(is_grp, jnp.exp(gl - gmax), 0.0), axis=-1, keepdims=True)
    in_grp = (lane >= g_idx * E_PER_GROUP) & (lane < (g_idx + 1) * E_PER_GROUP)
    sel = jnp.where(in_grp, logits, NEG)
    v1 = jnp.max(sel, axis=-1, keepdims=True)
    i1 = jnp.min(jnp.where(in_grp & (sel == v1), lane, big), axis=-1, keepdims=True)
    rest = in_grp & (lane != i1)
    sel2 = jnp.where(rest, logits, NEG)
    v2 = jnp.max(sel2, axis=-1, keepdims=True)
    i2 = jnp.min(jnp.where(rest & (sel2 == v2), lane, big), axis=-1, keepdims=True)
    e2 = jnp.exp(v2 - v1)
    w1 = g_w / (1.0 + e2)
    w2 = g_w * e2 / (1.0 + e2)
    return jnp.where(lane == i1, w1, jnp.where(lane == i2, w2, 0.0))


def _mix_kernel(x_ref, ya_ref, ob0_ref, ob1_ref, ob2_ref, ls0_ref, ls1_ref, ls2_ref, yc_ref,
                g0_ref, g1_ref, g2_ref, wb_ref, wo_ref, nf_ref, wr_ref, br_ref,
                xo_ref, h_ref, comb_ref):
    slabs = []
    for j in range(N_SLABS):
        ls0, ls1, ls2 = ls0_ref[0, j], ls1_ref[0, j], ls2_ref[0, j]
        mx = jnp.maximum(jnp.maximum(ls0, ls1), ls2)
        e0, e1, e2 = jnp.exp(ls0 - mx), jnp.exp(ls1 - mx), jnp.exp(ls2 - mx)
        yb = (e0 * ob0_ref[0, j] + e1 * ob1_ref[0, j] + e2 * ob2_ref[0, j]) / (e0 + e1 + e2)
        slabs.append(yb.astype(BF16))
    yb = jnp.concatenate(slabs, axis=-1)
    merged = jax.nn.sigmoid(g0_ref[...].astype(F32)) * jnp.dot(ya_ref[...], wb_ref[0],
                                                               preferred_element_type=F32)
    merged += jax.nn.sigmoid(g1_ref[...].astype(F32)) * jnp.dot(yb, wb_ref[1],
                                                                preferred_element_type=F32)
    merged += jax.nn.sigmoid(g2_ref[...].astype(F32)) * jnp.dot(yc_ref[...], wb_ref[2],
                                                                preferred_element_type=F32)
    xn = x_ref[...] + jnp.dot(merged.astype(BF16), wo_ref[...], preferred_element_type=F32)
    xo_ref[...] = xn
    h = _rms_bf16(xn, nf_ref[...])
    h_ref[...] = h
    logits = jnp.dot(h, wr_ref[...], preferred_element_type=F32) + br_ref[...]
    comb_ref[...] = _route(logits)


def _mix(x2, ya, obs, lses, yc, proj2, wb, wo, nf, wr, br):
    T, D = x2.shape
    S = obs[0].shape[2]
    tm = min(TM_MIX, S)
    per_b = S // tm
    w = MIX_W
    row = lambda width: pl.BlockSpec((tm, width), lambda i: (i, 0))
    full = lambda a: pl.BlockSpec(a.shape, lambda i: (0,) * a.ndim)
    gate = lambda n: pl.BlockSpec((tm, D), lambda i: (i, COL_GATE + n))
    slab = pl.BlockSpec((1, N_SLABS, tm, LANES), lambda i: (i // per_b, 0, i % per_b, 0))
    return pl.pallas_call(
        _mix_kernel,
        out_shape=[jax.ShapeDtypeStruct((T, D), F32), jax.ShapeDtypeStruct((T, D), BF16),
                   jax.ShapeDtypeStruct((T, LANES), F32)],
        grid=(T // tm,),
        in_specs=[row(D), row(w), slab, slab, slab, slab, slab, slab, row(w),
                  gate(0), gate(1), gate(2), full(wb), full(wo), full(nf), full(wr), full(br)],
        out_specs=[row(D), row(D), row(LANES)],
        compiler_params=_cparams(("parallel",)),
        name="mix",
    )(x2, ya, obs[0], obs[1], obs[2], lses[0], lses[1], lses[2], yc, proj2, proj2, proj2,
      wb, wo, nf, wr, br)


def _moe_kernel(h_ref, comb_ref, x_ref, wg_ref, wu_ref, wd_ref, nfin_ref, o_ref, acc_scr,
                *, final_norm):
    e = pl.program_id(1)

    @pl.when(e == 0)
    def _():
        acc_scr[...] = jnp.zeros(acc_scr.shape, F32)

    h = h_ref[...]
    lane = lax.broadcasted_iota(jnp.int32, comb_ref.shape, 1)
    c = jnp.sum(jnp.where(lane == e, comb_ref[...], 0.0), axis=-1, keepdims=True)
    hid = (jax.nn.silu(jnp.dot(h, wg_ref[0], preferred_element_type=F32))
           * jnp.dot(h, wu_ref[0], preferred_element_type=F32))
    acc_scr[...] += c * jnp.dot(hid.astype(BF16), wd_ref[0], preferred_element_type=F32)

    @pl.when(e == pl.num_programs(1) - 1)
    def _():
        xn = x_ref[...] + acc_scr[...]
        if final_norm:
            ms = jnp.mean(xn * xn, axis=-1, keepdims=True)
            xn = xn * lax.rsqrt(ms + EPS) * nfin_ref[...]
        o_ref[...] = xn


def _moe_dense(h, comb, x2, wg, wu, wd, nfin, final_norm):
    T, D = x2.shape
    tm = min(TM_MOE, T)
    F = wg.shape[2]
    kern = functools.partial(_moe_kernel, final_norm=final_norm)
    return pl.pallas_call(
        kern,
        out_shape=jax.ShapeDtypeStruct((T, D), F32),
        grid=(T // tm, N_EXPERTS),
        in_specs=[pl.BlockSpec((tm, D), lambda i, e: (i, 0)),
                  pl.BlockSpec((tm, LANES), lambda i, e: (i, 0)),
                  pl.BlockSpec((tm, D), lambda i, e: (i, 0)),
                  pl.BlockSpec((1, D, F), lambda i, e: (e, 0, 0)),
                  pl.BlockSpec((1, D, F), lambda i, e: (e, 0, 0)),
                  pl.BlockSpec((1, F, D), lambda i, e: (e, 0, 0)),
                  pl.BlockSpec((1, D), lambda i, e: (0, 0))],
        out_specs=pl.BlockSpec((tm, D), lambda i, e: (i, 0)),
        scratch_shapes=[pltpu.VMEM((tm, D), F32)],
        compiler_params=_cparams(("parallel", "arbitrary")),
        name="moe_dense",
    )(h, comb, x2, wg, wu, wd, nfin)


def _moe_dispatch_kernel(h_ref, comb_ref, o_ref, cnt_ref):
    tm = h_ref.shape[0]
    comb = comb_ref[...]
    hi = comb.astype(BF16)
    lo = (comb - hi.astype(F32)).astype(BF16)
    haug = jnp.concatenate([h_ref[...], hi, lo], axis=1)
    a_t = comb.T[:N_EXPERTS] > 0.0
    a_f = jnp.where(a_t, 1.0, 0.0)
    before = (lax.broadcasted_iota(jnp.int32, (tm, tm), 0)
              < lax.broadcasted_iota(jnp.int32, (tm, tm), 1))
    rank_t = jnp.dot(a_f.astype(BF16), jnp.where(before, 1.0, 0.0).astype(BF16),
                     preferred_element_type=F32)
    slot = lax.broadcasted_iota(jnp.int32, (MOE_CAP, tm), 0).astype(F32)
    blocks = [jnp.where((slot == rank_t[e:e + 1]) & a_t[e:e + 1], 1.0, 0.0).astype(BF16)
              for e in range(N_EXPERTS)]
    res = jnp.dot(jnp.concatenate(blocks, axis=0), haug, preferred_element_type=F32)
    res = res.astype(o_ref.dtype)
    for e in range(N_EXPERTS):
        o_ref[0, e] = res[e * MOE_CAP:(e + 1) * MOE_CAP]
    cnt_ref[0] = jnp.broadcast_to(jnp.max(jnp.sum(a_f, axis=1, keepdims=True)), cnt_ref.shape[1:])


def _moe_dispatch(h, comb):
    T, D = h.shape
    tm = min(TM_DISP, T)
    n = T // tm
    return pl.pallas_call(
        _moe_dispatch_kernel,
        out_shape=[jax.ShapeDtypeStruct((n, N_EXPERTS, MOE_CAP, D + 2 * LANES), BF16),
                   jax.ShapeDtypeStruct((n, 8, LANES), F32)],
        grid=(n,),
        in_specs=[pl.BlockSpec((tm, D), lambda i: (i, 0)),
                  pl.BlockSpec((tm, LANES), lambda i: (i, 0))],
        out_specs=[pl.BlockSpec((1, N_EXPERTS, MOE_CAP, D + 2 * LANES), lambda i: (i, 0, 0, 0)),
                   pl.BlockSpec((1, 8, LANES), lambda i: (i, 0, 0))],
        compiler_params=_cparams(("parallel",)),
        name="moe_dispatch",
    )(h, comb)


def _moe_ffn_kernel(s_ref, wg_ref, wu_ref, wd_ref, o_ref):
    e = pl.program_id(0)
    g = s_ref.shape[0]
    D = o_ref.shape[-1]
    rows = jnp.concatenate([s_ref[t, 0] for t in range(g)], axis=0)
    h = rows[:, :D]
    wparts = rows[:, D:].astype(F32)
    lane = lax.broadcasted_iota(jnp.int32, wparts.shape, 1)
    w = jnp.sum(jnp.where(lane % LANES == e, wparts, 0.0), axis=-1, keepdims=True)
    hid = (jax.nn.silu(jnp.dot(h, wg_ref[0], preferred_element_type=F32))
           * jnp.dot(h, wu_ref[0], preferred_element_type=F32))
    y = (w * jnp.dot(hid.astype(BF16), wd_ref[0], preferred_element_type=F32)).astype(o_ref.dtype)
    cap = s_ref.shape[2]
    for t in range(g):
        o_ref[t, 0] = y[t * cap:(t + 1) * cap]


def _moe_ffn(srt, wg, wu, wd):
    n, ne, cap, wdt = srt.shape
    D, F = wg.shape[1], wg.shape[2]
    g = min(G_FFN, n)
    return pl.pallas_call(
        _moe_ffn_kernel,
        out_shape=jax.ShapeDtypeStruct((n, ne, cap, D), BF16),
        grid=(ne, n // g),
        in_specs=[pl.BlockSpec((g, 1, cap, wdt), lambda e, c: (c, e, 0, 0)),
                  pl.BlockSpec((1, D, F), lambda e, c: (e, 0, 0)),
                  pl.BlockSpec((1, D, F), lambda e, c: (e, 0, 0)),
                  pl.BlockSpec((1, F, D), lambda e, c: (e, 0, 0))],
        out_specs=pl.BlockSpec((g, 1, cap, D), lambda e, c: (c, e, 0, 0)),
        compiler_params=_cparams(("parallel", "parallel")),
        name="moe_ffn",
    )(srt, wg, wu, wd)


def _moe_combine_kernel(y_ref, comb_ref, x_ref, nfin_ref, o_ref, *, final_norm):
    tm = x_ref.shape[0]
    comb = comb_ref[...]
    a = jnp.where(comb > 0.0, 1.0, 0.0)
    before = (lax.broadcasted_iota(jnp.int32, (tm, tm), 1)
              < lax.broadcasted_iota(jnp.int32, (tm, tm), 0))
    rank = jnp.dot(jnp.where(before, 1.0, 0.0).astype(BF16), a.astype(BF16),
                   preferred_element_type=F32)
    lane = lax.broadcasted_iota(jnp.int32, (tm, LANES), 1)
    first = lane < MOE_CAP
    lane_f = lane.astype(F32)
    pieces = []
    for k in range(N_EXPERTS // 2):
        e0, e1 = 2 * k, 2 * k + 1
        target = jnp.where(first, rank[:, e0:e0 + 1], rank[:, e1:e1 + 1] + MOE_CAP)
        active = jnp.where(first, a[:, e0:e0 + 1], a[:, e1:e1 + 1])
        pieces.append(jnp.where((lane_f == target) & (active > 0.0), 1.0, 0.0).astype(BF16))
    pc = jnp.concatenate(pieces, axis=1)
    y = jnp.concatenate([y_ref[0, e] for e in range(N_EXPERTS)], axis=0)
    xn = x_ref[...] + jnp.dot(pc, y, preferred_element_type=F32)
    if final_norm:
        ms = jnp.mean(xn * xn, axis=-1, keepdims=True)
        xn = xn * lax.rsqrt(ms + EPS) * nfin_ref[...]
    o_ref[...] = xn


def _moe_combine(y, comb, x2, nfin, final_norm):
    T, D = x2.shape
    n, ne, cap, _ = y.shape
    tm = T // n
    kern = functools.partial(_moe_combine_kernel, final_norm=final_norm)
    return pl.pallas_call(
        kern,
        out_shape=jax.ShapeDtypeStruct((T, D), F32),
        grid=(n,),
        in_specs=[pl.BlockSpec((1, ne, cap, D), lambda i: (i, 0, 0, 0)),
                  pl.BlockSpec((tm, LANES), lambda i: (i, 0)),
                  pl.BlockSpec((tm, D), lambda i: (i, 0)),
                  pl.BlockSpec((1, D), lambda i: (0, 0))],
        out_specs=pl.BlockSpec((tm, D), lambda i: (i, 0)),
        compiler_params=_cparams(("parallel",)),
        name="moe_combine",
    )(y, comb, x2, nfin)


def _moe_fix_kernel(ids_ref, n_ref, h_ref, comb_ref, x_ref, wg_ref, wu_ref, wd_ref, nfin_ref,
                    prev_ref, o_ref, acc_scr, *, final_norm):
    del ids_ref, prev_ref

    @pl.when(pl.program_id(0) < n_ref[0])
    def _():
        _moe_kernel(h_ref, comb_ref, x_ref, wg_ref, wu_ref, wd_ref, nfin_ref, o_ref, acc_scr,
                    final_norm=final_norm)


def _moe_fix(ids, n_ovf, out, h, comb, x2, wg, wu, wd, nfin, final_norm):
    T, D = x2.shape
    tm = min(TM_DISP, T)
    F = wg.shape[2]
    last_e = N_EXPERTS - 1
    tile = lambda width: pl.BlockSpec((tm, width), lambda s, e, ids, n: (ids[s], 0))
    wspec = lambda shape: pl.BlockSpec(
        shape, lambda s, e, ids, n: (jnp.where(s < n[0], e, last_e), 0, 0))
    kern = functools.partial(_moe_fix_kernel, final_norm=final_norm)
    return pl.pallas_call(
        kern,
        out_shape=jax.ShapeDtypeStruct((T, D), F32),
        grid_spec=pltpu.PrefetchScalarGridSpec(
            num_scalar_prefetch=2,
            grid=(MAX_OVF, N_EXPERTS),
            in_specs=[tile(D), tile(LANES), tile(D), wspec((1, D, F)), wspec((1, D, F)),
                      wspec((1, F, D)), pl.BlockSpec((1, D), lambda s, e, ids, n: (0, 0)),
                      pl.BlockSpec(memory_space=pl.ANY)],
            out_specs=tile(D),
            scratch_shapes=[pltpu.VMEM((tm, D), F32)]),
        input_output_aliases={9: 0},
        compiler_params=_cparams(("arbitrary", "arbitrary")),
        name="moe_fix",
    )(ids, n_ovf, h, comb, x2, wg, wu, wd, nfin, out)


def _moe(h, comb, x2, wg, wu, wd, nfin, final_norm):
    srt, cnt = _moe_dispatch(h, comb)
    over = cnt[:, 0, 0] > MOE_CAP
    n_ovf = jnp.sum(over.astype(jnp.int32))
    ids = jnp.nonzero(over, size=MAX_OVF, fill_value=0)[0].astype(jnp.int32)
    ids = jnp.where(jnp.arange(MAX_OVF) < n_ovf, ids, ids[jnp.clip(n_ovf - 1, 0, MAX_OVF - 1)])

    def routed():
        out = _moe_combine(_moe_ffn(srt, wg, wu, wd), comb, x2, nfin, final_norm)
        return lax.cond(
            n_ovf > 0,
            lambda: _moe_fix(ids, n_ovf.reshape(1), out, h, comb, x2, wg, wu, wd, nfin, final_norm),
            lambda: out)

    return lax.cond(n_ovf > MAX_OVF,
                    lambda: _moe_dense(h, comb, x2, wg, wu, wd, nfin, final_norm), routed)


def kernel(x, rel_bias, norm_mix, w_in, diff_lambda, diff_subln, sgu_ln_g, sgu_ln_b, sgu_w, sgu_b,
           w_branch, w_out, norm_ffn, w_router_grp, b_router_grp, w_router_exp, b_router_exp,
           w_gate, w_up, w_down, norm_final):
    B, S, D = x.shape
    T = B * S
    depth = w_in.shape[0]
    a_out = HA * 2 * DA
    grp_w = HB * DB
    b_cols = 3 * NG_B * grp_w
    qkv_b0 = 3 * a_out
    zc0 = qkv_b0 + b_cols
    gate0 = zc0 + 2 * MIX_W
    qk_scale = DA ** -0.5

    bias_a, cfar = _attn_a_bias(rel_bias)
    bias_b = [_attn_b_bias(rel_bias, g) for g in range(NG_B)]

    def group_cols(w, g):
        q = w[:, qkv_b0 + g * grp_w: qkv_b0 + (g + 1) * grp_w] * qk_scale
        k = w[:, qkv_b0 + (NG_B + g) * grp_w: qkv_b0 + (NG_B + g + 1) * grp_w]
        v = w[:, qkv_b0 + (2 * NG_B + g) * grp_w: qkv_b0 + (2 * NG_B + g + 1) * grp_w]
        return [q, k, v]

    x2 = x.reshape(T, D)
    for i in range(depth):
        w = w_in[i]
        nm = norm_mix[i][None, :]
        w_main = jnp.concatenate([w[:, :a_out] * (qk_scale * LOG2E), w[:, a_out:2 * a_out],
                                  w[:, zc0:]] + group_cols(w, 0), axis=1).astype(BF16)
        proj2 = _inproj(x2, nm, w_main)
        x3 = x2.reshape(B, S, D)
        vt = _inproj_t(x3, nm, w[:, 2 * a_out:3 * a_out].T.astype(BF16))
        proj3 = proj2.reshape(B, S, proj2.shape[1])

        lam_init = 0.8 - 0.6 * math.exp(-0.3 * i)
        lp = diff_lambda[i].astype(F32)
        lam = jnp.exp(jnp.sum(lp[0] * lp[1])) - jnp.exp(jnp.sum(lp[2] * lp[3])) + lam_init
        ya = _attn_a(proj3, vt, lam.reshape(1), cfar, bias_a, diff_subln[i][None, :], lam_init)

        obs, lses = [], []
        for g in range(NG_B):
            r = DILATIONS[g]
            if r == 1:
                qkv4, cols = proj3[:, None], (COL_QKV0, COL_QKV0 + 1, COL_QKV0 + 2)
            else:
                w_g = jnp.concatenate(group_cols(w, g), axis=1).astype(BF16)
                qkv4, cols = _inproj_perm(x3, nm, w_g, r), (0, 1, 2)
            o, l = _attn_b(qkv4, bias_b[g], g, cols)
            obs.append(o)
            lses.append(l)

        b_exp = jnp.repeat(sgu_b[i].T, MIX_W // C_GROUPS, axis=1)
        yc = _sgu(proj2, sgu_ln_g[i][None, :], sgu_ln_b[i][None, :], sgu_w[i].astype(BF16), b_exp)

        wr = jnp.concatenate([w_router_exp[i].transpose(1, 0, 2).reshape(D, N_EXPERTS),
                              w_router_grp[i]], axis=1)
        wr = jnp.pad(wr, ((0, 0), (0, LANES - wr.shape[1]))).astype(BF16)
        br = jnp.concatenate([b_router_exp[i].reshape(N_EXPERTS), b_router_grp[i]])
        br = jnp.pad(br, (0, LANES - br.shape[0]))[None, :].astype(F32)

        x2, h, comb = _mix(x2, ya.reshape(T, a_out), obs, lses, yc, proj2,
                           w_branch[i].astype(BF16), w_out[i].astype(BF16), norm_ffn[i][None, :],
                           wr, br)
        x2 = _moe(h, comb, x2, w_gate[i].astype(BF16), w_up[i].astype(BF16),
                  w_down[i].astype(BF16), norm_final[None, :], i == depth - 1)
    return x2.reshape(B, S, D)
```

```python
import functools
import math

import jax
import jax.numpy as jnp
from jax import lax
from jax.experimental import pallas as pl
from jax.experimental.pallas import tpu as pltpu

F32 = jnp.float32
BF16 = jnp.bfloat16

EPS = 1e-6
NEG = -1e30
LOG2E = 1.4426950408889634
LANES = 128
HALF_LANES = LANES // 2
VMEM_LIMIT = 48 * 1024 * 1024

HA = 4
DA = 64
MIX_W = 512
WINDOWS = (128, 512, 2048)
DILATIONS = (1, 4, 16)
NG_B = 3
HB = 8
DB = 64
HALF_WIN = 64
CHUNK = 128
C_GROUPS = 4
N_BRANCH = 3
N_BUCKETS = 32
MAX_DIST = 128
N_GROUPS = 4
E_PER_GROUP = 4
N_EXPERTS = N_GROUPS * E_PER_GROUP
N_SLABS = MIX_W // LANES

TM_PROJ = 1024
TN_PROJ = 3328
TM_PERM = 512
T_ATT = 512
QB_DIL = 128
KW_DIL = QB_DIL + 2 * HALF_WIN
ITEMS_DIL = 4
TM_SGU = 512
TM_MIX = 512
TM_MOE = 1024
TM_DISP = 256
MOE_CAP = HALF_LANES
G_FFN = 8
MAX_OVF = 64

COL_ZU = 2
COL_GATE = 2
COL_QKV0 = 10


def _cparams(sem):
    return pltpu.CompilerParams(dimension_semantics=sem, vmem_limit_bytes=VMEM_LIMIT)


def _t5_bucket(rel):
    nb = N_BUCKETS // 2
    max_exact = nb // 2
    ret = (rel > 0).astype(jnp.int32) * nb
    n = jnp.abs(rel)
    nf = jnp.maximum(n, 1).astype(F32)
    large = max_exact + (jnp.log(nf / max_exact) / math.log(MAX_DIST / max_exact)
                         * (nb - max_exact)).astype(jnp.int32)
    large = jnp.minimum(large, nb - 1)
    return ret + jnp.where(n < max_exact, n, large)


def _bias_lookup(bucket, tab):
    out = jnp.zeros((tab.shape[1],) + bucket.shape, F32)
    expand = (slice(None),) + (None,) * bucket.ndim
    for b in range(N_BUCKETS):
        out = jnp.where(bucket[None] == b, tab[b][expand], out)
    return out


def _rms_bf16(x, g):
    ms = jnp.mean(x * x, axis=-1, keepdims=True)
    return (x * lax.rsqrt(ms + EPS) * g).astype(BF16)


def _inproj_kernel(x_ref, g_ref, w_ref, o_ref, h_scr):
    @pl.when(pl.program_id(1) == 0)
    def _():
        h_scr[...] = _rms_bf16(x_ref[...], g_ref[...])

    o_ref[...] = jnp.dot(h_scr[...], w_ref[...], preferred_element_type=F32).astype(o_ref.dtype)


def _inproj(x2, g, w):
    T, D = x2.shape
    N = w.shape[1]
    tm = min(TM_PROJ, T)
    return pl.pallas_call(
        _inproj_kernel,
        out_shape=jax.ShapeDtypeStruct((T, N), BF16),
        grid=(T // tm, N // TN_PROJ),
        in_specs=[pl.BlockSpec((tm, D), lambda i, j: (i, 0)),
                  pl.BlockSpec((1, D), lambda i, j: (0, 0)),
                  pl.BlockSpec((D, TN_PROJ), lambda i, j: (0, j))],
        out_specs=pl.BlockSpec((tm, TN_PROJ), lambda i, j: (i, j)),
        scratch_shapes=[pltpu.VMEM((tm, D), BF16)],
        compiler_params=_cparams(("parallel", "arbitrary")),
        name="inproj",
    )(x2, g, w)


def _inproj_t_kernel(x_ref, g_ref, wt_ref, o_ref):
    h = _rms_bf16(x_ref[0], g_ref[...])
    res = lax.dot_general(wt_ref[...], h, (((1,), (1,)), ((), ())),
                          preferred_element_type=F32).astype(o_ref.dtype)
    for hd in range(o_ref.shape[1]):
        for n in range(o_ref.shape[2]):
            o_ref[0, hd, n] = res[hd * LANES:(hd + 1) * LANES, n * T_ATT:(n + 1) * T_ATT]


def _inproj_t(x3, g, wt):
    B, S, D = x3.shape
    N = wt.shape[0]
    tm = min(TM_PROJ, S)
    nh, nb = N // LANES, tm // T_ATT
    return pl.pallas_call(
        _inproj_t_kernel,
        out_shape=jax.ShapeDtypeStruct((B, nh, S // T_ATT, LANES, T_ATT), BF16),
        grid=(B, S // tm),
        in_specs=[pl.BlockSpec((1, tm, D), lambda b, i: (b, i, 0)),
                  pl.BlockSpec((1, D), lambda b, i: (0, 0)),
                  pl.BlockSpec((N, D), lambda b, i: (0, 0))],
        out_specs=pl.BlockSpec((1, nh, nb, LANES, T_ATT), lambda b, i: (b, 0, i, 0, 0)),
        compiler_params=_cparams(("parallel", "parallel")),
        name="inproj_t",
    )(x3, g, wt)


def _inproj_perm_kernel(x_ref, g_ref, p_ref, w_ref, o_ref, *, r):
    h = _rms_bf16(x_ref[0], g_ref[...])
    hp = jnp.dot(p_ref[...], h, preferred_element_type=F32).astype(BF16)
    res = jnp.dot(hp, w_ref[...], preferred_element_type=F32).astype(o_ref.dtype)
    n = res.shape[0] // r
    for s in range(r):
        o_ref[0, s] = res[s * n:(s + 1) * n, :]


def _inproj_perm(x3, g, w, r):
    B, S, D = x3.shape
    N = w.shape[1]
    tm = min(TM_PERM, S)
    n = tm // r
    o = jnp.arange(tm, dtype=jnp.int32)
    src = (o % n) * r + o // n
    perm = (src[:, None] == jnp.arange(tm, dtype=jnp.int32)[None, :]).astype(BF16)
    kern = functools.partial(_inproj_perm_kernel, r=r)
    return pl.pallas_call(
        kern,
        out_shape=jax.ShapeDtypeStruct((B, r, S // r, N), BF16),
        grid=(B, S // tm),
        in_specs=[pl.BlockSpec((1, tm, D), lambda b, i: (b, i, 0)),
                  pl.BlockSpec((1, D), lambda b, i: (0, 0)),
                  pl.BlockSpec((tm, tm), lambda b, i: (0, 0)),
                  pl.BlockSpec((D, N), lambda b, i: (0, 0))],
        out_specs=pl.BlockSpec((1, r, n, N), lambda b, i: (b, 0, i, 0)),
        compiler_params=_cparams(("parallel", "parallel")),
        name=f"inproj_perm_{r}",
    )(x3, g, perm, w)


def _attn_a_kernel(lam_ref, cfar_ref, q_ref, k_ref, vt_ref, bias_ref, g_ref, o_ref, *, out_scale):
    h, qi = pl.program_id(1), pl.program_id(2)
    t = T_ATT
    nk = k_ref.shape[1] // t
    q = q_ref[0]
    low_half = lax.broadcasted_iota(jnp.int32, (1, LANES), 1) < HALF_LANES
    zero = jnp.zeros_like(q)
    qs = jnp.concatenate([jnp.where(low_half, q, zero), jnp.where(low_half, zero, q)], axis=0)

    blocks, sts, shifts = [], [], []
    m = None
    for d in range(-1, nk - 1):
        a = lax.rem(qi + (d + nk), nk)
        delta = a - qi
        kb = k_ref[0, pl.ds(pl.multiple_of(a * t, t), t), :]
        st = lax.dot_general(kb, qs, (((1,), (1,)), ((), ())), preferred_element_type=F32)
        if d <= 1:
            tile = bias_ref[0, jnp.clip(delta, -2, 2) + 2]
            st = st + jnp.concatenate([tile, tile], axis=1)
            shift = None
            cm = jnp.max(st, axis=0, keepdims=True)
        else:
            shift = cfar_ref[2 * h + (delta > 0).astype(jnp.int32)]
            cm = jnp.max(st, axis=0, keepdims=True) + shift
        m = cm if m is None else jnp.maximum(m, cm)
        blocks.append(a)
        sts.append(st)
        shifts.append(shift)

    l = jnp.zeros_like(m)
    accs = [jnp.zeros((LANES, t), F32), jnp.zeros((LANES, t), F32)]
    for a, st, shift in zip(blocks, sts, shifts):
        p = jnp.exp2(st - (m if shift is None else m - shift))
        l = l + jnp.sum(p, axis=0, keepdims=True)
        pb = p.astype(BF16)
        vt = vt_ref[0, 0, a]
        for c in range(2):
            accs[c] = accs[c] + jnp.dot(vt, pb[:, c * t:(c + 1) * t], preferred_element_type=F32)

    ot = accs[0] / l[:, :t] - lam_ref[0] * (accs[1] / l[:, t:])
    o = ot.T
    ms = jnp.mean(o * o, axis=-1, keepdims=True)
    o_ref[0] = (o * lax.rsqrt(ms + EPS) * g_ref[...] * out_scale).astype(o_ref.dtype)


def _attn_a(proj3, vt, lam, cfar, bias5, subln_g, lam_init):
    B, S, _ = proj3.shape
    t = T_ATT
    nk = S // t
    kern = functools.partial(_attn_a_kernel, out_scale=1.0 - lam_init)
    return pl.pallas_call(
        kern,
        out_shape=jax.ShapeDtypeStruct((B, S, HA * 2 * DA), BF16),
        grid=(B, HA, nk),
        in_specs=[
            pl.BlockSpec(memory_space=pltpu.SMEM),
            pl.BlockSpec(memory_space=pltpu.SMEM),
            pl.BlockSpec((1, t, LANES), lambda b, h, qi: (b, qi, h)),
            pl.BlockSpec((1, S, LANES), lambda b, h, qi: (b, 0, HA + h)),
            pl.BlockSpec((1, 1, nk, LANES, t), lambda b, h, qi: (b, h, 0, 0, 0)),
            pl.BlockSpec((1, 5, t, t), lambda b, h, qi: (h, 0, 0, 0)),
            pl.BlockSpec((1, LANES), lambda b, h, qi: (0, 0)),
        ],
        out_specs=pl.BlockSpec((1, t, LANES), lambda b, h, qi: (b, qi, h)),
        compiler_params=_cparams(("parallel", "parallel", "arbitrary")),
        name="diff_attn",
    )(lam, cfar, proj3, proj3, vt, bias5, subln_g)


def _attn_a_bias(rel_bias):
    t = T_ATT
    tab = rel_bias[:, :HA].astype(F32) * LOG2E
    d = jnp.arange(-2, 3, dtype=jnp.int32)[:, None, None] * t
    rel = d + jnp.arange(t, dtype=jnp.int32)[None, :, None] - jnp.arange(t, dtype=jnp.int32)[None, None, :]
    tiles = _bias_lookup(_t5_bucket(rel), tab)
    far = _t5_bucket(jnp.array([-(t + 1), t + 1], dtype=jnp.int32))
    cfar = tab[far].T.reshape(2 * HA)
    return tiles, cfar


def _attn_b_kernel(q_ref, k_ref, v_ref, bias_ref, o_ref, lse_ref, *, sub_len, r, sp, qp):
    nblk = sub_len // QB_DIL
    low_half = lax.broadcasted_iota(jnp.int32, (1, LANES), 1) < HALF_LANES
    for si in range(sp):
        s = si if sp == r else pl.program_id(2) * sp + si
        for qb in range(qp):
            i = pl.program_id(1) * qp + qb
            start = jnp.clip(i * QB_DIL - HALF_WIN, 0, sub_len - KW_DIL)
            start = pl.multiple_of(start, HALF_WIN)
            variant = jnp.where(i == 0, 0, jnp.where(i == nblk - 1, 2, 1))
            q = q_ref[0, si, qb * QB_DIL:(qb + 1) * QB_DIL, :]
            kw = k_ref[0, s, pl.ds(start, KW_DIL), :]
            vw = v_ref[0, s, pl.ds(start, KW_DIL), :]
            rows = (slice(qb * QB_DIL, (qb + 1) * QB_DIL) if r == 1
                    else pl.ds(s, QB_DIL, stride=r))
            for j in range(HB // 2):
                cols = slice(j * LANES, (j + 1) * LANES)
                qpair, kp, vp = q[:, cols], kw[:, cols], vw[:, cols]
                outs, lses = [], []
                for c in range(2):
                    qc = jnp.where(low_half if c == 0 else jnp.logical_not(low_half), qpair,
                                   jnp.zeros_like(qpair))
                    sc = lax.dot_general(qc, kp, (((1,), (1,)), ((), ())),
                                         preferred_element_type=F32)
                    sc = sc + bias_ref[2 * j + c, variant]
                    m = jnp.max(sc, axis=-1, keepdims=True)
                    p = jnp.exp(sc - m)
                    l = jnp.sum(p, axis=-1, keepdims=True)
                    outs.append(jnp.dot(p.astype(BF16), vp, preferred_element_type=F32) / l)
                    lses.append(m + jnp.log(l))
                o_ref[0, j, rows, :] = jnp.where(low_half, outs[0], outs[1])
                lse_ref[0, j, rows, :] = jnp.where(low_half, lses[0], lses[1])


def _attn_b(qkv4, bias3, g, cols):
    B, r, L, _ = qkv4.shape
    S = r * L
    width = HB * DB
    nblk = L // QB_DIL
    sp = min(r, ITEMS_DIL)
    qp = ITEMS_DIL // sp
    qcol, kcol, vcol = cols
    kern = functools.partial(_attn_b_kernel, sub_len=L, r=r, sp=sp, qp=qp)
    slab = jax.ShapeDtypeStruct((B, N_SLABS, S, LANES), F32)
    slab_spec = pl.BlockSpec((1, N_SLABS, QB_DIL * r * qp, LANES), lambda b, i, s: (b, 0, i, 0))
    return pl.pallas_call(
        kern,
        out_shape=[slab, slab],
        grid=(B, nblk // qp, r // sp),
        in_specs=[
            pl.BlockSpec((1, sp, QB_DIL * qp, width), lambda b, i, s: (b, s, i, qcol)),
            pl.BlockSpec((1, r, L, width), lambda b, i, s: (b, 0, 0, kcol)),
            pl.BlockSpec((1, r, L, width), lambda b, i, s: (b, 0, 0, vcol)),
            pl.BlockSpec((HB, 3, QB_DIL, KW_DIL), lambda b, i, s: (0, 0, 0, 0)),
        ],
        out_specs=[slab_spec, slab_spec],
        compiler_params=_cparams(("parallel", "arbitrary", "arbitrary")),
        name=f"dilated_attn_{g}",
    )(qkv4, qkv4, qkv4, bias3)


def _attn_b_bias(rel_bias, g):
    r = DILATIONS[g]
    tab = rel_bias[:, HA + g * HB: HA + (g + 1) * HB].astype(F32)
    off = jnp.arange(3, dtype=jnp.int32)[:, None, None] * HALF_WIN
    rel = (jnp.arange(KW_DIL, dtype=jnp.int32)[None, None, :] - off
           - jnp.arange(QB_DIL, dtype=jnp.int32)[None, :, None])
    bias = _bias_lookup(_t5_bucket(rel * r), tab)
    return jnp.where((jnp.abs(rel) <= HALF_WIN)[None], bias, NEG)


def _sgu_kernel(zu_ref, zv_ref, lng_ref, lnb_ref, ws_ref, bs_ref, o_ref):
    u = jax.nn.gelu(zu_ref[...].astype(F32))
    v = jax.nn.gelu(zv_ref[...].astype(F32))
    mu = jnp.mean(v, axis=-1, keepdims=True)
    var = jnp.mean(jnp.square(v - mu), axis=-1, keepdims=True)
    v = ((v - mu) * lax.rsqrt(var + EPS) * lng_ref[...] + lnb_ref[...]).astype(BF16)
    gd = v.shape[1] // C_GROUPS
    for n in range(v.shape[0] // CHUNK):
        rows = slice(n * CHUNK, (n + 1) * CHUNK)
        for g in range(C_GROUPS):
            cols = slice(g * gd, (g + 1) * gd)
            mixed = jnp.dot(ws_ref[g], v[rows, cols], preferred_element_type=F32) + bs_ref[:, cols]
            o_ref[rows, cols] = (u[rows, cols] * mixed).astype(o_ref.dtype)


def _sgu(proj2, ln_g, ln_b, w_s, b_exp):
    T = proj2.shape[0]
    tm = min(TM_SGU, T)
    w = MIX_W
    return pl.pallas_call(
        _sgu_kernel,
        out_shape=jax.ShapeDtypeStruct((T, w), BF16),
        grid=(T // tm,),
        in_specs=[pl.BlockSpec((tm, w), lambda i: (i, COL_ZU)),
                  pl.BlockSpec((tm, w), lambda i: (i, COL_ZU + 1)),
                  pl.BlockSpec((1, w), lambda i: (0, 0)),
                  pl.BlockSpec((1, w), lambda i: (0, 0)),
                  pl.BlockSpec((C_GROUPS, CHUNK, CHUNK), lambda i: (0, 0, 0)),
                  pl.BlockSpec((CHUNK, w), lambda i: (0, 0))],
        out_specs=pl.BlockSpec((tm, w), lambda i: (i, 0)),
        compiler_params=_cparams(("parallel",)),
        name="sgu",
    )(proj2, proj2, ln_g, ln_b, w_s, b_exp)


def _route(logits):
    lane = lax.broadcasted_iota(jnp.int32, logits.shape, 1)
    big = jnp.int32(LANES)
    is_grp = (lane >= N_EXPERTS) & (lane < N_EXPERTS + N_GROUPS)
    gl = jnp.where(is_grp, logits, NEG)
    gmax = jnp.max(gl, axis=-1, keepdims=True)
    g_idx = jnp.min(jnp.where(is_grp & (gl == gmax), lane, big), axis=-1, keepdims=True) - N_EXPERTS
    g_w = 1.0 / jnp.sum(jnp.where(is_grp, jnp.exp(gl - gmax), 0.0), axis=-1, keepdims=True)
    in_grp = (lane >= g_idx * E_PER_GROUP) & (lane < (g_idx + 1) * E_PER_GROUP)
    sel = jnp.where(in_grp, logits, NEG)
    v1 = jnp.max(sel, axis=-1, keepdims=True)
    i1 = jnp.min(jnp.where(in_grp & (sel == v1), lane, big), axis=-1, keepdims=True)
    rest = in_grp & (lane != i1)
    sel2 = jnp.where(rest, logits, NEG)
    v2 = jnp.max(sel2, axis=-1, keepdims=True)
    i2 = jnp.min(jnp.where(rest & (sel2 == v2), lane, big), axis=-1, keepdims=True)
    e2 = jnp.exp(v2 - v1)
    w1 = g_w / (1.0 + e2)
    w2 = g_w * e2 / (1.0 + e2)
    return jnp.where(lane == i1, w1, jnp.where(lane == i2, w2, 0.0))


def _mix_kernel(x_ref, ya_ref, ob0_ref, ob1_ref, ob2_ref, ls0_ref, ls1_ref, ls2_ref, yc_ref,
                g0_ref, g1_ref, g2_ref, wb_ref, wo_ref, nf_ref, wr_ref, br_ref,
                xo_ref, h_ref, comb_ref):
    slabs = []
    for j in range(N_SLABS):
        ls0, ls1, ls2 = ls0_ref[0, j], ls1_ref[0, j], ls2_ref[0, j]
        mx = jnp.maximum(jnp.maximum(ls0, ls1), ls2)
        e0, e1, e2 = jnp.exp(ls0 - mx), jnp.exp(ls1 - mx), jnp.exp(ls2 - mx)
        yb = (e0 * ob0_ref[0, j] + e1 * ob1_ref[0, j] + e2 * ob2_ref[0, j]) / (e0 + e1 + e2)
        slabs.append(yb.astype(BF16))
    yb = jnp.concatenate(slabs, axis=-1)
    merged = jax.nn.sigmoid(g0_ref[...].astype(F32)) * jnp.dot(ya_ref[...], wb_ref[0],
                                                               preferred_element_type=F32)
    merged += jax.nn.sigmoid(g1_ref[...].astype(F32)) * jnp.dot(yb, wb_ref[1],
                                                                preferred_element_type=F32)
    merged += jax.nn.sigmoid(g2_ref[...].astype(F32)) * jnp.dot(yc_ref[...], wb_ref[2],
                                                                preferred_element_type=F32)
    xn = x_ref[...] + jnp.dot(merged.astype(BF16), wo_ref[...], preferred_element_type=F32)
    xo_ref[...] = xn
    h = _rms_bf16(xn, nf_ref[...])
    h_ref[...] = h
    logits = jnp.dot(h, wr_ref[...], preferred_element_type=F32) + br_ref[...]
    comb_ref[...] = _route(logits)


def _mix(x2, ya, obs, lses, yc, proj2, wb, wo, nf, wr, br):
    T, D = x2.shape
    S = obs[0].shape[2]
    tm = min(TM_MIX, S)
    per_b = S // tm
    w = MIX_W
    row = lambda width: pl.BlockSpec((tm, width), lambda i: (i, 0))
    full = lambda a: pl.BlockSpec(a.shape, lambda i: (0,) * a.ndim)
    gate = lambda n: pl.BlockSpec((tm, D), lambda i: (i, COL_GATE + n))
    slab = pl.BlockSpec((1, N_SLABS, tm, LANES), lambda i: (i // per_b, 0, i % per_b, 0))
    return pl.pallas_call(
        _mix_kernel,
        out_shape=[jax.ShapeDtypeStruct((T, D), F32), jax.ShapeDtypeStruct((T, D), BF16),
                   jax.ShapeDtypeStruct((T, LANES), F32)],
        grid=(T // tm,),
        in_specs=[row(D), row(w), slab, slab, slab, slab, slab, slab, row(w),
                  gate(0), gate(1), gate(2), full(wb), full(wo), full(nf), full(wr), full(br)],
        out_specs=[row(D), row(D), row(LANES)],
        compiler_params=_cparams(("parallel",)),
        name="mix",
    )(x2, ya, obs[0], obs[1], obs[2], lses[0], lses[1], lses[2], yc, proj2, proj2, proj2,
      wb, wo, nf, wr, br)


def _moe_kernel(h_ref, comb_ref, x_ref, wg_ref, wu_ref, wd_ref, nfin_ref, o_ref, acc_scr,
                *, final_norm):
    e = pl.program_id(1)

    @pl.when(e == 0)
    def _():
        acc_scr[...] = jnp.zeros(acc_scr.shape, F32)

    h = h_ref[...]
    lane = lax.broadcasted_iota(jnp.int32, comb_ref.shape, 1)
    c = jnp.sum(jnp.where(lane == e, comb_ref[...], 0.0), axis=-1, keepdims=True)
    hid = (jax.nn.silu(jnp.dot(h, wg_ref[0], preferred_element_type=F32))
           * jnp.dot(h, wu_ref[0], preferred_element_type=F32))
    acc_scr[...] += c * jnp.dot(hid.astype(BF16), wd_ref[0], preferred_element_type=F32)

    @pl.when(e == pl.num_programs(1) - 1)
    def _():
        xn = x_ref[...] + acc_scr[...]
        if final_norm:
            ms = jnp.mean(xn * xn, axis=-1, keepdims=True)
            xn = xn * lax.rsqrt(ms + EPS) * nfin_ref[...]
        o_ref[...] = xn


def _moe_dense(h, comb, x2, wg, wu, wd, nfin, final_norm):
    T, D = x2.shape
    tm = min(TM_MOE, T)
    F = wg.shape[2]
    kern = functools.partial(_moe_kernel, final_norm=final_norm)
    return pl.pallas_call(
        kern,
        out_shape=jax.ShapeDtypeStruct((T, D), F32),
        grid=(T // tm, N_EXPERTS),
        in_specs=[pl.BlockSpec((tm, D), lambda i, e: (i, 0)),
                  pl.BlockSpec((tm, LANES), lambda i, e: (i, 0)),
                  pl.BlockSpec((tm, D), lambda i, e: (i, 0)),
                  pl.BlockSpec((1, D, F), lambda i, e: (e, 0, 0)),
                  pl.BlockSpec((1, D, F), lambda i, e: (e, 0, 0)),
                  pl.BlockSpec((1, F, D), lambda i, e: (e, 0, 0)),
                  pl.BlockSpec((1, D), lambda i, e: (0, 0))],
        out_specs=pl.BlockSpec((tm, D), lambda i, e: (i, 0)),
        scratch_shapes=[pltpu.VMEM((tm, D), F32)],
        compiler_params=_cparams(("parallel", "arbitrary")),
        name="moe_dense",
    )(h, comb, x2, wg, wu, wd, nfin)


def _moe_dispatch_kernel(h_ref, comb_ref, o_ref, cnt_ref):
    tm = h_ref.shape[0]
    comb = comb_ref[...]
    hi = comb.astype(BF16)
    lo = (comb - hi.astype(F32)).astype(BF16)
    haug = jnp.concatenate([h_ref[...], hi, lo], axis=1)
    a_t = comb.T[:N_EXPERTS] > 0.0
    a_f = jnp.where(a_t, 1.0, 0.0)
    before = (lax.broadcasted_iota(jnp.int32, (tm, tm), 0)
              < lax.broadcasted_iota(jnp.int32, (tm, tm), 1))
    rank_t = jnp.dot(a_f.astype(BF16), jnp.where(before, 1.0, 0.0).astype(BF16),
                     preferred_element_type=F32)
    slot = lax.broadcasted_iota(jnp.int32, (MOE_CAP, tm), 0).astype(F32)
    blocks = [jnp.where((slot == rank_t[e:e + 1]) & a_t[e:e + 1], 1.0, 0.0).astype(BF16)
              for e in range(N_EXPERTS)]
    res = jnp.dot(jnp.concatenate(blocks, axis=0), haug, preferred_element_type=F32)
    res = res.astype(o_ref.dtype)
    for e in range(N_EXPERTS):
        o_ref[0, e] = res[e * MOE_CAP:(e + 1) * MOE_CAP]
    cnt_ref[0] = jnp.broadcast_to(jnp.sum(a_f, axis=1, keepdims=True), cnt_ref.shape[1:])


def _moe_dispatch(h, comb):
    T, D = h.shape
    tm = min(TM_DISP, T)
    n = T // tm
    return pl.pallas_call(
        _moe_dispatch_kernel,
        out_shape=[jax.ShapeDtypeStruct((n, N_EXPERTS, MOE_CAP, D + 2 * LANES), BF16),
                   jax.ShapeDtypeStruct((n, N_EXPERTS, LANES), F32)],
        grid=(n,),
        in_specs=[pl.BlockSpec((tm, D), lambda i: (i, 0)),
                  pl.BlockSpec((tm, LANES), lambda i: (i, 0))],
        out_specs=[pl.BlockSpec((1, N_EXPERTS, MOE_CAP, D + 2 * LANES), lambda i: (i, 0, 0, 0)),
                   pl.BlockSpec((1, N_EXPERTS, LANES), lambda i: (i, 0, 0))],
        compiler_params=_cparams(("parallel",)),
        name="moe_dispatch",
    )(h, comb)


def _moe_ffn_kernel(s_ref, wg_ref, wu_ref, wd_ref, o_ref):
    e = pl.program_id(0)
    g = s_ref.shape[0]
    D = o_ref.shape[-1]
    rows = jnp.concatenate([s_ref[t, 0] for t in range(g)], axis=0)
    h = rows[:, :D]
    wparts = rows[:, D:].astype(F32)
    lane = lax.broadcasted_iota(jnp.int32, wparts.shape, 1)
    w = jnp.sum(jnp.where(lane % LANES == e, wparts, 0.0), axis=-1, keepdims=True)
    hid = (jax.nn.silu(jnp.dot(h, wg_ref[0], preferred_element_type=F32))
           * jnp.dot(h, wu_ref[0], preferred_element_type=F32))
    y = (w * jnp.dot(hid.astype(BF16), wd_ref[0], preferred_element_type=F32)).astype(o_ref.dtype)
    cap = s_ref.shape[2]
    for t in range(g):
        o_ref[t, 0] = y[t * cap:(t + 1) * cap]


def _moe_ffn(srt, wg, wu, wd):
    n, ne, cap, wdt = srt.shape
    D, F = wg.shape[1], wg.shape[2]
    g = min(G_FFN, n)
    return pl.pallas_call(
        _moe_ffn_kernel,
        out_shape=jax.ShapeDtypeStruct((n, ne, cap, D), BF16),
        grid=(ne, n // g),
        in_specs=[pl.BlockSpec((g, 1, cap, wdt), lambda e, c: (c, e, 0, 0)),
                  pl.BlockSpec((1, D, F), lambda e, c: (e, 0, 0)),
                  pl.BlockSpec((1, D, F), lambda e, c: (e, 0, 0)),
                  pl.BlockSpec((1, F, D), lambda e, c: (e, 0, 0))],
        out_specs=pl.BlockSpec((g, 1, cap, D), lambda e, c: (c, e, 0, 0)),
        compiler_params=_cparams(("parallel", "parallel")),
        name="moe_ffn",
    )(srt, wg, wu, wd)


def _moe_combine_kernel(y_ref, comb_ref, x_ref, nfin_ref, o_ref, *, final_norm):
    tm = x_ref.shape[0]
    comb = comb_ref[...]
    a = jnp.where(comb > 0.0, 1.0, 0.0)
    before = (lax.broadcasted_iota(jnp.int32, (tm, tm), 1)
              < lax.broadcasted_iota(jnp.int32, (tm, tm), 0))
    rank = jnp.dot(jnp.where(before, 1.0, 0.0).astype(BF16), a.astype(BF16),
                   preferred_element_type=F32)
    lane = lax.broadcasted_iota(jnp.int32, (tm, LANES), 1)
    first = lane < MOE_CAP
    lane_f = lane.astype(F32)
    pieces = []
    for k in range(N_EXPERTS // 2):
        e0, e1 = 2 * k, 2 * k + 1
        target = jnp.where(first, rank[:, e0:e0 + 1], rank[:, e1:e1 + 1] + MOE_CAP)
        active = jnp.where(first, a[:, e0:e0 + 1], a[:, e1:e1 + 1])
        pieces.append(jnp.where((lane_f == target) & (active > 0.0), 1.0, 0.0).astype(BF16))
    pc = jnp.concatenate(pieces, axis=1)
    y = jnp.concatenate([y_ref[0, e] for e in range(N_EXPERTS)], axis=0)
    xn = x_ref[...] + jnp.dot(pc, y, preferred_element_type=F32)
    if final_norm:
        ms = jnp.mean(xn * xn, axis=-1, keepdims=True)
        xn = xn * lax.rsqrt(ms + EPS) * nfin_ref[...]
    o_ref[...] = xn


def _moe_combine(y, comb, x2, nfin, final_norm):
    T, D = x2.shape
    n, ne, cap, _ = y.shape
    tm = T // n
    kern = functools.partial(_moe_combine_kernel, final_norm=final_norm)
    return pl.pallas_call(
        kern,
        out_shape=jax.ShapeDtypeStruct((T, D), F32),
        grid=(n,),
        in_specs=[pl.BlockSpec((1, ne, cap, D), lambda i: (i, 0, 0, 0)),
                  pl.BlockSpec((tm, LANES), lambda i: (i, 0)),
                  pl.BlockSpec((tm, D), lambda i: (i, 0)),
                  pl.BlockSpec((1, D), lambda i: (0, 0))],
        out_specs=pl.BlockSpec((tm, D), lambda i: (i, 0)),
        compiler_params=_cparams(("parallel",)),
        name="moe_combine",
    )(y, comb, x2, nfin)


def _moe_fix_kernel(tiles_ref, experts_ref, first_ref, n_ref, h_ref, comb_ref, prev_ref,
                    wg_ref, wu_ref, wd_ref, o_ref):
    del tiles_ref
    s = pl.program_id(0)

    @pl.when(s < n_ref[0])
    def _():
        e = experts_ref[s]
        tm = h_ref.shape[0]
        comb = comb_ref[...]
        a = jnp.where(comb > 0.0, 1.0, 0.0)
        before = (lax.broadcasted_iota(jnp.int32, (tm, tm), 1)
                  < lax.broadcasted_iota(jnp.int32, (tm, tm), 0))
        rank = jnp.dot(jnp.where(before, 1.0, 0.0).astype(BF16), a.astype(BF16),
                       preferred_element_type=F32)
        lane = lax.broadcasted_iota(jnp.int32, comb.shape, 1)
        dropped = (lane == e) & (rank >= MOE_CAP)
        c = jnp.sum(jnp.where(dropped, comb, 0.0), axis=-1, keepdims=True)
        h = h_ref[...]
        hid = (jax.nn.silu(jnp.dot(h, wg_ref[0], preferred_element_type=F32))
               * jnp.dot(h, wu_ref[0], preferred_element_type=F32))
        add = c * jnp.dot(hid.astype(BF16), wd_ref[0], preferred_element_type=F32)
        fresh = first_ref[s] == 1

        @pl.when(fresh)
        def _():
            o_ref[...] = prev_ref[...] + add

        @pl.when(jnp.logical_not(fresh))
        def _():
            o_ref[...] += add


def _moe_fix(tiles, experts, first, n, out, h, comb, wg, wu, wd):
    T, D = out.shape
    tm = min(TM_DISP, T)
    F = wg.shape[2]
    tile = lambda width: pl.BlockSpec((tm, width), lambda s, tl, ex, fi, n: (tl[s], 0))
    wspec = lambda shape: pl.BlockSpec(shape, lambda s, tl, ex, fi, n: (ex[s], 0, 0))
    return pl.pallas_call(
        _moe_fix_kernel,
        out_shape=jax.ShapeDtypeStruct((T, D), F32),
        grid_spec=pltpu.PrefetchScalarGridSpec(
            num_scalar_prefetch=4,
            grid=(MAX_OVF,),
            in_specs=[tile(D), tile(LANES), tile(D), wspec((1, D, F)), wspec((1, D, F)),
                      wspec((1, F, D))],
            out_specs=tile(D)),
        input_output_aliases={6: 0},
        compiler_params=_cparams(("arbitrary",)),
        name="moe_fix",
    )(tiles, experts, first, n, h, comb, out, wg, wu, wd)


def _final_norm_kernel(x_ref, g_ref, o_ref):
    x = x_ref[...]
    ms = jnp.mean(x * x, axis=-1, keepdims=True)
    o_ref[...] = x * lax.rsqrt(ms + EPS) * g_ref[...]


def _final_norm(x2, g):
    T, D = x2.shape
    tm = min(TM_PROJ, T)
    return pl.pallas_call(
        _final_norm_kernel,
        out_shape=jax.ShapeDtypeStruct((T, D), F32),
        grid=(T // tm,),
        in_specs=[pl.BlockSpec((tm, D), lambda i: (i, 0)), pl.BlockSpec((1, D), lambda i: (0, 0))],
        out_specs=pl.BlockSpec((tm, D), lambda i: (i, 0)),
        compiler_params=_cparams(("parallel",)),
        name="final_norm",
    )(x2, g)


def _moe(h, comb, x2, wg, wu, wd, nfin, final_norm):
    srt, cnt = _moe_dispatch(h, comb)
    over = (cnt[:, :, 0] > MOE_CAP).reshape(-1)
    n_ovf = jnp.sum(over.astype(jnp.int32))
    pairs = jnp.nonzero(over, size=MAX_OVF, fill_value=0)[0].astype(jnp.int32)
    pairs = jnp.where(jnp.arange(MAX_OVF) < n_ovf, pairs, pairs[jnp.clip(n_ovf - 1, 0, MAX_OVF - 1)])
    tiles, experts = pairs // N_EXPERTS, pairs % N_EXPERTS
    first = jnp.concatenate([jnp.ones((1,), jnp.int32),
                             (tiles[1:] != tiles[:-1]).astype(jnp.int32)])

    def routed():
        y = _moe_ffn(srt, wg, wu, wd)

        def fixed():
            out = _moe_combine(y, comb, x2, nfin, False)
            out = _moe_fix(tiles, experts, first, n_ovf.reshape(1), out, h, comb, wg, wu, wd)
            return _final_norm(out, nfin) if final_norm else out

        return lax.cond(n_ovf > 0, fixed, lambda: _moe_combine(y, comb, x2, nfin, final_norm))

    return lax.cond(n_ovf > MAX_OVF,
                    lambda: _moe_dense(h, comb, x2, wg, wu, wd, nfin, final_norm), routed)


def kernel(x, rel_bias, norm_mix, w_in, diff_lambda, diff_subln, sgu_ln_g, sgu_ln_b, sgu_w, sgu_b,
           w_branch, w_out, norm_ffn, w_router_grp, b_router_grp, w_router_exp, b_router_exp,
           w_gate, w_up, w_down, norm_final):
    B, S, D = x.shape
    T = B * S
    depth = w_in.shape[0]
    a_out = HA * 2 * DA
    grp_w = HB * DB
    b_cols = 3 * NG_B * grp_w
    qkv_b0 = 3 * a_out
    zc0 = qkv_b0 + b_cols
    gate0 = zc0 + 2 * MIX_W
    qk_scale = DA ** -0.5

    bias_a, cfar = _attn_a_bias(rel_bias)
    bias_b = [_attn_b_bias(rel_bias, g) for g in range(NG_B)]

    def group_cols(w, g):
        q = w[:, qkv_b0 + g * grp_w: qkv_b0 + (g + 1) * grp_w] * qk_scale
        k = w[:, qkv_b0 + (NG_B + g) * grp_w: qkv_b0 + (NG_B + g + 1) * grp_w]
        v = w[:, qkv_b0 + (2 * NG_B + g) * grp_w: qkv_b0 + (2 * NG_B + g + 1) * grp_w]
        return [q, k, v]

    x2 = x.reshape(T, D)
    for i in range(depth):
        w = w_in[i]
        nm = norm_mix[i][None, :]
        w_main = jnp.concatenate([w[:, :a_out] * (qk_scale * LOG2E), w[:, a_out:2 * a_out],
                                  w[:, zc0:]] + group_cols(w, 0), axis=1).astype(BF16)
        proj2 = _inproj(x2, nm, w_main)
        x3 = x2.reshape(B, S, D)
        vt = _inproj_t(x3, nm, w[:, 2 * a_out:3 * a_out].T.astype(BF16))
        proj3 = proj2.reshape(B, S, proj2.shape[1])

        lam_init = 0.8 - 0.6 * math.exp(-0.3 * i)
        lp = diff_lambda[i].astype(F32)
        lam = jnp.exp(jnp.sum(lp[0] * lp[1])) - jnp.exp(jnp.sum(lp[2] * lp[3])) + lam_init
        ya = _attn_a(proj3, vt, lam.reshape(1), cfar, bias_a, diff_subln[i][None, :], lam_init)

        obs, lses = [], []
        for g in range(NG_B):
            r = DILATIONS[g]
            if r == 1:
                qkv4, cols = proj3[:, None], (COL_QKV0, COL_QKV0 + 1, COL_QKV0 + 2)
            else:
                w_g = jnp.concatenate(group_cols(w, g), axis=1).astype(BF16)
                qkv4, cols = _inproj_perm(x3, nm, w_g, r), (0, 1, 2)
            o, l = _attn_b(qkv4, bias_b[g], g, cols)
            obs.append(o)
            lses.append(l)

        b_exp = jnp.repeat(sgu_b[i].T, MIX_W // C_GROUPS, axis=1)
        yc = _sgu(proj2, sgu_ln_g[i][None, :], sgu_ln_b[i][None, :], sgu_w[i].astype(BF16), b_exp)

        wr = jnp.concatenate([w_router_exp[i].transpose(1, 0, 2).reshape(D, N_EXPERTS),
                              w_router_grp[i]], axis=1)
        wr = jnp.pad(wr, ((0, 0), (0, LANES - wr.shape[1]))).astype(BF16)
        br = jnp.concatenate([b_router_exp[i].reshape(N_EXPERTS), b_router_grp[i]])
        br = jnp.pad(br, (0, LANES - br.shape[0]))[None, :].astype(F32)

        x2, h, comb = _mix(x2, ya.reshape(T, a_out), obs, lses, yc, proj2,
                           w_branch[i].astype(BF16), w_out[i].astype(BF16), norm_ffn[i][None, :],
                           wr, br)
        x2 = _moe(h, comb, x2, w_gate[i].astype(BF16), w_up[i].astype(BF16),
                  w_down[i].astype(BF16), norm_final[None, :], i == depth - 1)
    return x2.reshape(B, S, D)
```

```python
import functools
import math

import jax
import jax.numpy as jnp
from jax import lax
from jax.experimental import pallas as pl
from jax.experimental.pallas import tpu as pltpu

F32 = jnp.float32
BF16 = jnp.bfloat16

EPS = 1e-6
NEG = -1e30
LOG2E = 1.4426950408889634
LANES = 128
HALF_LANES = LANES // 2
VMEM_LIMIT = 48 * 1024 * 1024

HA = 4
DA = 64
MIX_W = 512
WINDOWS = (128, 512, 2048)
DILATIONS = (1, 4, 16)
NG_B = 3
HB = 8
DB = 64
HALF_WIN = 64
CHUNK = 128
C_GROUPS = 4
N_BRANCH = 3
N_BUCKETS = 32
MAX_DIST = 128
N_GROUPS = 4
E_PER_GROUP = 4
N_EXPERTS = N_GROUPS * E_PER_GROUP
N_SLABS = MIX_W // LANES

TM_PROJ = 1024
TN_PROJ = 3328
TM_PERM = 512
T_ATT = 512
QB_DIL = 128
KW_DIL = QB_DIL + 2 * HALF_WIN
ITEMS_DIL = 4
TM_SGU = 512
TM_MIX = 512
TM_MOE = 1024
TM_DISP = 256
MOE_CAP = HALF_LANES
G_FFN = 8
MAX_OVF = 64

COL_ZU = 2
COL_GATE = 2
COL_QKV0 = 10


def _cparams(sem):
    return pltpu.CompilerParams(dimension_semantics=sem, vmem_limit_bytes=VMEM_LIMIT)


def _t5_bucket(rel):
    nb = N_BUCKETS // 2
    max_exact = nb // 2
    ret = (rel > 0).astype(jnp.int32) * nb
    n = jnp.abs(rel)
    nf = jnp.maximum(n, 1).astype(F32)
    large = max_exact + (jnp.log(nf / max_exact) / math.log(MAX_DIST / max_exact)
                         * (nb - max_exact)).astype(jnp.int32)
    large = jnp.minimum(large, nb - 1)
    return ret + jnp.where(n < max_exact, n, large)


def _bias_lookup(bucket, tab):
    out = jnp.zeros((tab.shape[1],) + bucket.shape, F32)
    expand = (slice(None),) + (None,) * bucket.ndim
    for b in range(N_BUCKETS):
        out = jnp.where(bucket[None] == b, tab[b][expand], out)
    return out


def _rms_bf16(x, g):
    ms = jnp.mean(x * x, axis=-1, keepdims=True)
    return (x * lax.rsqrt(ms + EPS) * g).astype(BF16)


def _inproj_kernel(x_ref, g_ref, w_ref, o_ref, h_scr):
    @pl.when(pl.program_id(1) == 0)
    def _():
        h_scr[...] = _rms_bf16(x_ref[...], g_ref[...])

    o_ref[...] = jnp.dot(h_scr[...], w_ref[...], preferred_element_type=F32).astype(o_ref.dtype)


def _inproj(x2, g, w):
    T, D = x2.shape
    N = w.shape[1]
    tm = min(TM_PROJ, T)
    return pl.pallas_call(
        _inproj_kernel,
        out_shape=jax.ShapeDtypeStruct((T, N), BF16),
        grid=(T // tm, N // TN_PROJ),
        in_specs=[pl.BlockSpec((tm, D), lambda i, j: (i, 0)),
                  pl.BlockSpec((1, D), lambda i, j: (0, 0)),
                  pl.BlockSpec((D, TN_PROJ), lambda i, j: (0, j))],
        out_specs=pl.BlockSpec((tm, TN_PROJ), lambda i, j: (i, j)),
        scratch_shapes=[pltpu.VMEM((tm, D), BF16)],
        compiler_params=_cparams(("parallel", "arbitrary")),
        name="inproj",
    )(x2, g, w)


def _inproj_t_kernel(x_ref, g_ref, wt_ref, o_ref):
    h = _rms_bf16(x_ref[0], g_ref[...])
    res = lax.dot_general(wt_ref[...], h, (((1,), (1,)), ((), ())),
                          preferred_element_type=F32).astype(o_ref.dtype)
    for hd in range(o_ref.shape[1]):
        for n in range(o_ref.shape[2]):
            o_ref[0, hd, n] = res[hd * LANES:(hd + 1) * LANES, n * T_ATT:(n + 1) * T_ATT]


def _inproj_t(x3, g, wt):
    B, S, D = x3.shape
    N = wt.shape[0]
    tm = min(TM_PROJ, S)
    nh, nb = N // LANES, tm // T_ATT
    return pl.pallas_call(
        _inproj_t_kernel,
        out_shape=jax.ShapeDtypeStruct((B, nh, S // T_ATT, LANES, T_ATT), BF16),
        grid=(B, S // tm),
        in_specs=[pl.BlockSpec((1, tm, D), lambda b, i: (b, i, 0)),
                  pl.BlockSpec((1, D), lambda b, i: (0, 0)),
                  pl.BlockSpec((N, D), lambda b, i: (0, 0))],
        out_specs=pl.BlockSpec((1, nh, nb, LANES, T_ATT), lambda b, i: (b, 0, i, 0, 0)),
        compiler_params=_cparams(("parallel", "parallel")),
        name="inproj_t",
    )(x3, g, wt)


def _inproj_perm_kernel(x_ref, g_ref, p_ref, w_ref, o_ref, *, r):
    h = _rms_bf16(x_ref[0], g_ref[...])
    hp = jnp.dot(p_ref[...], h, preferred_element_type=F32).astype(BF16)
    res = jnp.dot(hp, w_ref[...], preferred_element_type=F32).astype(o_ref.dtype)
    n = res.shape[0] // r
    for s in range(r):
        o_ref[0, s] = res[s * n:(s + 1) * n, :]


def _inproj_perm(x3, g, w, r):
    B, S, D = x3.shape
    N = w.shape[1]
    tm = min(TM_PERM, S)
    n = tm // r
    o = jnp.arange(tm, dtype=jnp.int32)
    src = (o % n) * r + o // n
    perm = (src[:, None] == jnp.arange(tm, dtype=jnp.int32)[None, :]).astype(BF16)
    kern = functools.partial(_inproj_perm_kernel, r=r)
    return pl.pallas_call(
        kern,
        out_shape=jax.ShapeDtypeStruct((B, r, S // r, N), BF16),
        grid=(B, S // tm),
        in_specs=[pl.BlockSpec((1, tm, D), lambda b, i: (b, i, 0)),
                  pl.BlockSpec((1, D), lambda b, i: (0, 0)),
                  pl.BlockSpec((tm, tm), lambda b, i: (0, 0)),
                  pl.BlockSpec((D, N), lambda b, i: (0, 0))],
        out_specs=pl.BlockSpec((1, r, n, N), lambda b, i: (b, 0, i, 0)),
        compiler_params=_cparams(("parallel", "parallel")),
        name=f"inproj_perm_{r}",
    )(x3, g, perm, w)


def _attn_a_kernel(lam_ref, cfar_ref, q_ref, k_ref, vt_ref, bias_ref, g_ref, o_ref, *, out_scale):
    h, qi = pl.program_id(1), pl.program_id(2)
    t = T_ATT
    nk = k_ref.shape[1] // t
    q = q_ref[0]
    low_half = lax.broadcasted_iota(jnp.int32, (1, LANES), 1) < HALF_LANES
    zero = jnp.zeros_like(q)
    qs = jnp.concatenate([jnp.where(low_half, q, zero), jnp.where(low_half, zero, q)], axis=0)

    blocks, sts, shifts = [], [], []
    m = None
    for d in range(-1, nk - 1):
        a = lax.rem(qi + (d + nk), nk)
        delta = a - qi
        kb = k_ref[0, pl.ds(pl.multiple_of(a * t, t), t), :]
        st = lax.dot_general(kb, qs, (((1,), (1,)), ((), ())), preferred_element_type=F32)
        if d <= 1:
            tile = bias_ref[0, jnp.clip(delta, -2, 2) + 2]
            st = st + jnp.concatenate([tile, tile], axis=1)
            shift = None
            cm = jnp.max(st, axis=0, keepdims=True)
        else:
            shift = cfar_ref[2 * h + (delta > 0).astype(jnp.int32)]
            cm = jnp.max(st, axis=0, keepdims=True) + shift
        m = cm if m is None else jnp.maximum(m, cm)
        blocks.append(a)
        sts.append(st)
        shifts.append(shift)

    l = jnp.zeros_like(m)
    accs = [jnp.zeros((LANES, t), F32), jnp.zeros((LANES, t), F32)]
    for a, st, shift in zip(blocks, sts, shifts):
        p = jnp.exp2(st - (m if shift is None else m - shift))
        l = l + jnp.sum(p, axis=0, keepdims=True)
        pb = p.astype(BF16)
        vt = vt_ref[0, 0, a]
        for c in range(2):
            accs[c] = accs[c] + jnp.dot(vt, pb[:, c * t:(c + 1) * t], preferred_element_type=F32)

    ot = accs[0] / l[:, :t] - lam_ref[0] * (accs[1] / l[:, t:])
    o = ot.T
    ms = jnp.mean(o * o, axis=-1, keepdims=True)
    o_ref[0] = (o * lax.rsqrt(ms + EPS) * g_ref[...] * out_scale).astype(o_ref.dtype)


def _attn_a(proj3, vt, lam, cfar, bias5, subln_g, lam_init):
    B, S, _ = proj3.shape
    t = T_ATT
    nk = S // t
    kern = functools.partial(_attn_a_kernel, out_scale=1.0 - lam_init)
    return pl.pallas_call(
        kern,
        out_shape=jax.ShapeDtypeStruct((B, S, HA * 2 * DA), BF16),
        grid=(B, HA, nk),
        in_specs=[
            pl.BlockSpec(memory_space=pltpu.SMEM),
            pl.BlockSpec(memory_space=pltpu.SMEM),
            pl.BlockSpec((1, t, LANES), lambda b, h, qi: (b, qi, h)),
            pl.BlockSpec((1, S, LANES), lambda b, h, qi: (b, 0, HA + h)),
            pl.BlockSpec((1, 1, nk, LANES, t), lambda b, h, qi: (b, h, 0, 0, 0)),
            pl.BlockSpec((1, 5, t, t), lambda b, h, qi: (h, 0, 0, 0)),
            pl.BlockSpec((1, LANES), lambda b, h, qi: (0, 0)),
        ],
        out_specs=pl.BlockSpec((1, t, LANES), lambda b, h, qi: (b, qi, h)),
        compiler_params=_cparams(("parallel", "parallel", "arbitrary")),
        name="diff_attn",
    )(lam, cfar, proj3, proj3, vt, bias5, subln_g)


def _attn_a_bias(rel_bias):
    t = T_ATT
    tab = rel_bias[:, :HA].astype(F32) * LOG2E
    d = jnp.arange(-2, 3, dtype=jnp.int32)[:, None, None] * t
    rel = d + jnp.arange(t, dtype=jnp.int32)[None, :, None] - jnp.arange(t, dtype=jnp.int32)[None, None, :]
    tiles = _bias_lookup(_t5_bucket(rel), tab)
    far = _t5_bucket(jnp.array([-(t + 1), t + 1], dtype=jnp.int32))
    cfar = tab[far].T.reshape(2 * HA)
    return tiles, cfar


def _attn_b_kernel(q_ref, k_ref, v_ref, bias_ref, o_ref, lse_ref, *, sub_len, r, sp, qp):
    nblk = sub_len // QB_DIL
    low_half = lax.broadcasted_iota(jnp.int32, (1, LANES), 1) < HALF_LANES
    for si in range(sp):
        s = si if sp == r else pl.program_id(2) * sp + si
        for qb in range(qp):
            i = pl.program_id(1) * qp + qb
            start = jnp.clip(i * QB_DIL - HALF_WIN, 0, sub_len - KW_DIL)
            start = pl.multiple_of(start, HALF_WIN)
            variant = jnp.where(i == 0, 0, jnp.where(i == nblk - 1, 2, 1))
            q = q_ref[0, si, qb * QB_DIL:(qb + 1) * QB_DIL, :]
            kw = k_ref[0, s, pl.ds(start, KW_DIL), :]
            vw = v_ref[0, s, pl.ds(start, KW_DIL), :]
            rows = (slice(qb * QB_DIL, (qb + 1) * QB_DIL) if r == 1
                    else pl.ds(s, QB_DIL, stride=r))
            for j in range(HB // 2):
                cols = slice(j * LANES, (j + 1) * LANES)
                qpair, kp, vp = q[:, cols], kw[:, cols], vw[:, cols]
                outs, lses = [], []
                for c in range(2):
                    qc = jnp.where(low_half if c == 0 else jnp.logical_not(low_half), qpair,
                                   jnp.zeros_like(qpair))
                    sc = lax.dot_general(qc, kp, (((1,), (1,)), ((), ())),
                                         preferred_element_type=F32)
                    sc = sc + bias_ref[2 * j + c, variant]
                    m = jnp.max(sc, axis=-1, keepdims=True)
                    p = jnp.exp(sc - m)
                    l = jnp.sum(p, axis=-1, keepdims=True)
                    outs.append(jnp.dot(p.astype(BF16), vp, preferred_element_type=F32) / l)
                    lses.append(m + jnp.log(l))
                o_ref[0, j, rows, :] = jnp.where(low_half, outs[0], outs[1])
                lse_ref[0, j, rows, :] = jnp.where(low_half, lses[0], lses[1])


def _attn_b(qkv4, bias3, g, cols):
    B, r, L, _ = qkv4.shape
    S = r * L
    width = HB * DB
    nblk = L // QB_DIL
    sp = min(r, ITEMS_DIL)
    qp = ITEMS_DIL // sp
    qcol, kcol, vcol = cols
    kern = functools.partial(_attn_b_kernel, sub_len=L, r=r, sp=sp, qp=qp)
    slab = jax.ShapeDtypeStruct((B, N_SLABS, S, LANES), F32)
    slab_spec = pl.BlockSpec((1, N_SLABS, QB_DIL * r * qp, LANES), lambda b, i, s: (b, 0, i, 0))
    return pl.pallas_call(
        kern,
        out_shape=[slab, slab],
        grid=(B, nblk // qp, r // sp),
        in_specs=[
            pl.BlockSpec((1, sp, QB_DIL * qp, width), lambda b, i, s: (b, s, i, qcol)),
            pl.BlockSpec((1, r, L, width), lambda b, i, s: (b, 0, 0, kcol)),
            pl.BlockSpec((1, r, L, width), lambda b, i, s: (b, 0, 0, vcol)),
            pl.BlockSpec((HB, 3, QB_DIL, KW_DIL), lambda b, i, s: (0, 0, 0, 0)),
        ],
        out_specs=[slab_spec, slab_spec],
        compiler_params=_cparams(("parallel", "arbitrary", "arbitrary")),
        name=f"dilated_attn_{g}",
    )(qkv4, qkv4, qkv4, bias3)


def _attn_b_bias(rel_bias, g):
    r = DILATIONS[g]
    tab = rel_bias[:, HA + g * HB: HA + (g + 1) * HB].astype(F32)
    off = jnp.arange(3, dtype=jnp.int32)[:, None, None] * HALF_WIN
    rel = (jnp.arange(KW_DIL, dtype=jnp.int32)[None, None, :] - off
           - jnp.arange(QB_DIL, dtype=jnp.int32)[None, :, None])
    bias = _bias_lookup(_t5_bucket(rel * r), tab)
    return jnp.where((jnp.abs(rel) <= HALF_WIN)[None], bias, NEG)


def _sgu_kernel(zu_ref, zv_ref, lng_ref, lnb_ref, ws_ref, bs_ref, o_ref):
    u = jax.nn.gelu(zu_ref[...].astype(F32))
    v = jax.nn.gelu(zv_ref[...].astype(F32))
    mu = jnp.mean(v, axis=-1, keepdims=True)
    var = jnp.mean(jnp.square(v - mu), axis=-1, keepdims=True)
    v = ((v - mu) * lax.rsqrt(var + EPS) * lng_ref[...] + lnb_ref[...]).astype(BF16)
    gd = v.shape[1] // C_GROUPS
    for n in range(v.shape[0] // CHUNK):
        rows = slice(n * CHUNK, (n + 1) * CHUNK)
        for g in range(C_GROUPS):
            cols = slice(g * gd, (g + 1) * gd)
            mixed = jnp.dot(ws_ref[g], v[rows, cols], preferred_element_type=F32) + bs_ref[:, cols]
            o_ref[rows, cols] = (u[rows, cols] * mixed).astype(o_ref.dtype)


def _sgu(proj2, ln_g, ln_b, w_s, b_exp):
    T = proj2.shape[0]
    tm = min(TM_SGU, T)
    w = MIX_W
    return pl.pallas_call(
        _sgu_kernel,
        out_shape=jax.ShapeDtypeStruct((T, w), BF16),
        grid=(T // tm,),
        in_specs=[pl.BlockSpec((tm, w), lambda i: (i, COL_ZU)),
                  pl.BlockSpec((tm, w), lambda i: (i, COL_ZU + 1)),
                  pl.BlockSpec((1, w), lambda i: (0, 0)),
                  pl.BlockSpec((1, w), lambda i: (0, 0)),
                  pl.BlockSpec((C_GROUPS, CHUNK, CHUNK), lambda i: (0, 0, 0)),
                  pl.BlockSpec((CHUNK, w), lambda i: (0, 0))],
        out_specs=pl.BlockSpec((tm, w), lambda i: (i, 0)),
        compiler_params=_cparams(("parallel",)),
        name="sgu",
    )(proj2, proj2, ln_g, ln_b, w_s, b_exp)


def _route(logits):
    lane = lax.broadcasted_iota(jnp.int32, logits.shape, 1)
    big = jnp.int32(LANES)
    is_grp = (lane >= N_EXPERTS) & (lane < N_EXPERTS + N_GROUPS)
    gl = jnp.where(is_grp, logits, NEG)
    gmax = jnp.max(gl, axis=-1, keepdims=True)
    g_idx = jnp.min(jnp.where(is_grp & (gl == gmax), lane, big), axis=-1, keepdims=True) - N_EXPERTS
    g_w = 1.0 / jnp.sum(jnp.where(is_grp, jnp.exp(gl - gmax), 0.0), axis=-1, keepdims=True)
    in_grp = (lane >= g_idx * E_PER_GROUP) & (lane < (g_idx + 1) * E_PER_GROUP)
    sel = jnp.where(in_grp, logits, NEG)
    v1 = jnp.max(sel, axis=-1, keepdims=True)
    i1 = jnp.min(jnp.where(in_grp & (sel == v1), lane, big), axis=-1, keepdims=True)
    rest = in_grp & (lane != i1)
    sel2 = jnp.where(rest, logits, NEG)
    v2 = jnp.max(sel2, axis=-1, keepdims=True)
    i2 = jnp.min(jnp.where(rest & (sel2 == v2), lane, big), axis=-1, keepdims=True)
    e2 = jnp.exp(v2 - v1)
    w1 = g_w / (1.0 + e2)
    w2 = g_w * e2 / (1.0 + e2)
    return jnp.where(lane == i1, w1, jnp.where(lane == i2, w2, 0.0))


def _mix_kernel(x_ref, ya_ref, ob0_ref, ob1_ref, ob2_ref, ls0_ref, ls1_ref, ls2_ref, yc_ref,
                g0_ref, g1_ref, g2_ref, wb_ref, wo_ref, nf_ref, wr_ref, br_ref,
                xo_ref, h_ref, comb_ref):
    slabs = []
    for j in range(N_SLABS):
        ls0, ls1, ls2 = ls0_ref[0, j], ls1_ref[0, j], ls2_ref[0, j]
        mx = jnp.maximum(jnp.maximum(ls0, ls1), ls2)
        e0, e1, e2 = jnp.exp(ls0 - mx), jnp.exp(ls1 - mx), jnp.exp(ls2 - mx)
        yb = (e0 * ob0_ref[0, j] + e1 * ob1_ref[0, j] + e2 * ob2_ref[0, j]) / (e0 + e1 + e2)
        slabs.append(yb.astype(BF16))
    yb = jnp.concatenate(slabs, axis=-1)
    merged = jax.nn.sigmoid(g0_ref[...].astype(F32)) * jnp.dot(ya_ref[...], wb_ref[0],
                                                               preferred_element_type=F32)
    merged += jax.nn.sigmoid(g1_ref[...].astype(F32)) * jnp.dot(yb, wb_ref[1],
                                                                preferred_element_type=F32)
    merged += jax.nn.sigmoid(g2_ref[...].astype(F32)) * jnp.dot(yc_ref[...], wb_ref[2],
                                                                preferred_element_type=F32)
    xn = x_ref[...] + jnp.dot(merged.astype(BF16), wo_ref[...], preferred_element_type=F32)
    xo_ref[...] = xn
    h = _rms_bf16(xn, nf_ref[...])
    h_ref[...] = h
    logits = jnp.dot(h, wr_ref[...], preferred_element_type=F32) + br_ref[...]
    comb_ref[...] = _route(logits)


def _mix(x2, ya, obs, lses, yc, proj2, wb, wo, nf, wr, br):
    T, D = x2.shape
    S = obs[0].shape[2]
    tm = min(TM_MIX, S)
    per_b = S // tm
    w = MIX_W
    row = lambda width: pl.BlockSpec((tm, width), lambda i: (i, 0))
    full = lambda a: pl.BlockSpec(a.shape, lambda i: (0,) * a.ndim)
    gate = lambda n: pl.BlockSpec((tm, D), lambda i: (i, COL_GATE + n))
    slab = pl.BlockSpec((1, N_SLABS, tm, LANES), lambda i: (i // per_b, 0, i % per_b, 0))
    return pl.pallas_call(
        _mix_kernel,
        out_shape=[jax.ShapeDtypeStruct((T, D), F32), jax.ShapeDtypeStruct((T, D), BF16),
                   jax.ShapeDtypeStruct((T, LANES), F32)],
        grid=(T // tm,),
        in_specs=[row(D), row(w), slab, slab, slab, slab, slab, slab, row(w),
                  gate(0), gate(1), gate(2), full(wb), full(wo), full(nf), full(wr), full(br)],
        out_specs=[row(D), row(D), row(LANES)],
        compiler_params=_cparams(("parallel",)),
        name="mix",
    )(x2, ya, obs[0], obs[1], obs[2], lses[0], lses[1], lses[2], yc, proj2, proj2, proj2,
      wb, wo, nf, wr, br)


def _moe_kernel(h_ref, comb_ref, x_ref, wg_ref, wu_ref, wd_ref, nfin_ref, o_ref, acc_scr,
                *, final_norm):
    e = pl.program_id(1)

    @pl.when(e == 0)
    def _():
        acc_scr[...] = jnp.zeros(acc_scr.shape, F32)

    h = h_ref[...]
    lane = lax.broadcasted_iota(jnp.int32, comb_ref.shape, 1)
    c = jnp.sum(jnp.where(lane == e, comb_ref[...], 0.0), axis=-1, keepdims=True)
    hid = (jax.nn.silu(jnp.dot(h, wg_ref[0].astype(BF16), preferred_element_type=F32))
           * jnp.dot(h, wu_ref[0].astype(BF16), preferred_element_type=F32))
    acc_scr[...] += c * jnp.dot(hid.astype(BF16), wd_ref[0].astype(BF16),
                                preferred_element_type=F32)

    @pl.when(e == pl.num_programs(1) - 1)
    def _():
        xn = x_ref[...] + acc_scr[...]
        if final_norm:
            ms = jnp.mean(xn * xn, axis=-1, keepdims=True)
            xn = xn * lax.rsqrt(ms + EPS) * nfin_ref[...]
        o_ref[...] = xn


def _moe_dense(h, comb, x2, wg, wu, wd, nfin, final_norm):
    T, D = x2.shape
    tm = min(TM_MOE, T)
    F = wg.shape[2]
    kern = functools.partial(_moe_kernel, final_norm=final_norm)
    return pl.pallas_call(
        kern,
        out_shape=jax.ShapeDtypeStruct((T, D), F32),
        grid=(T // tm, N_EXPERTS),
        in_specs=[pl.BlockSpec((tm, D), lambda i, e: (i, 0)),
                  pl.BlockSpec((tm, LANES), lambda i, e: (i, 0)),
                  pl.BlockSpec((tm, D), lambda i, e: (i, 0)),
                  pl.BlockSpec((1, D, F), lambda i, e: (e, 0, 0)),
                  pl.BlockSpec((1, D, F), lambda i, e: (e, 0, 0)),
                  pl.BlockSpec((1, F, D), lambda i, e: (e, 0, 0)),
                  pl.BlockSpec((1, D), lambda i, e: (0, 0))],
        out_specs=pl.BlockSpec((tm, D), lambda i, e: (i, 0)),
        scratch_shapes=[pltpu.VMEM((tm, D), F32)],
        compiler_params=_cparams(("parallel", "arbitrary")),
        name="moe_dense",
    )(h, comb, x2, wg, wu, wd, nfin)


def _moe_dispatch_kernel(h_ref, comb_ref, o_ref, cnt_ref):
    tm = h_ref.shape[0]
    comb = comb_ref[...]
    hi = comb.astype(BF16)
    lo = (comb - hi.astype(F32)).astype(BF16)
    haug = jnp.concatenate([h_ref[...], hi, lo], axis=1)
    a_t = comb.T[:N_EXPERTS] > 0.0
    a_f = jnp.where(a_t, 1.0, 0.0)
    before = (lax.broadcasted_iota(jnp.int32, (tm, tm), 0)
              < lax.broadcasted_iota(jnp.int32, (tm, tm), 1))
    rank_t = jnp.dot(a_f.astype(BF16), jnp.where(before, 1.0, 0.0).astype(BF16),
                     preferred_element_type=F32)
    slot = lax.broadcasted_iota(jnp.int32, (MOE_CAP, tm), 0).astype(F32)
    blocks = [jnp.where((slot == rank_t[e:e + 1]) & a_t[e:e + 1], 1.0, 0.0).astype(BF16)
              for e in range(N_EXPERTS)]
    res = jnp.dot(jnp.concatenate(blocks, axis=0), haug, preferred_element_type=F32)
    res = res.astype(o_ref.dtype)
    for e in range(N_EXPERTS):
        o_ref[0, e] = res[e * MOE_CAP:(e + 1) * MOE_CAP]
    cnt_ref[0] = jnp.broadcast_to(jnp.sum(a_f, axis=1, keepdims=True), cnt_ref.shape[1:])


def _moe_dispatch(h, comb):
    T, D = h.shape
    tm = min(TM_DISP, T)
    n = T // tm
    return pl.pallas_call(
        _moe_dispatch_kernel,
        out_shape=[jax.ShapeDtypeStruct((n, N_EXPERTS, MOE_CAP, D + 2 * LANES), BF16),
                   jax.ShapeDtypeStruct((n, N_EXPERTS, LANES), F32)],
        grid=(n,),
        in_specs=[pl.BlockSpec((tm, D), lambda i: (i, 0)),
                  pl.BlockSpec((tm, LANES), lambda i: (i, 0))],
        out_specs=[pl.BlockSpec((1, N_EXPERTS, MOE_CAP, D + 2 * LANES), lambda i: (i, 0, 0, 0)),
                   pl.BlockSpec((1, N_EXPERTS, LANES), lambda i: (i, 0, 0))],
        compiler_params=_cparams(("parallel",)),
        name="moe_dispatch",
    )(h, comb)


def _moe_ffn_kernel(s_ref, wg_ref, wu_ref, wd_ref, o_ref, wg_scr, wu_scr, wd_scr):
    e = pl.program_id(0)

    @pl.when(pl.program_id(1) == 0)
    def _():
        wg_scr[...] = wg_ref[0].astype(BF16)
        wu_scr[...] = wu_ref[0].astype(BF16)
        wd_scr[...] = wd_ref[0].astype(BF16)

    g = s_ref.shape[0]
    D = o_ref.shape[-1]
    rows = jnp.concatenate([s_ref[t, 0] for t in range(g)], axis=0)
    h = rows[:, :D]
    wparts = rows[:, D:].astype(F32)
    lane = lax.broadcasted_iota(jnp.int32, wparts.shape, 1)
    w = jnp.sum(jnp.where(lane % LANES == e, wparts, 0.0), axis=-1, keepdims=True)
    hid = (jax.nn.silu(jnp.dot(h, wg_scr[...], preferred_element_type=F32))
           * jnp.dot(h, wu_scr[...], preferred_element_type=F32))
    y = (w * jnp.dot(hid.astype(BF16), wd_scr[...], preferred_element_type=F32)).astype(o_ref.dtype)
    cap = s_ref.shape[2]
    for t in range(g):
        o_ref[t, 0] = y[t * cap:(t + 1) * cap]


def _moe_ffn(srt, wg, wu, wd):
    n, ne, cap, wdt = srt.shape
    D, F = wg.shape[1], wg.shape[2]
    g = min(G_FFN, n)
    return pl.pallas_call(
        _moe_ffn_kernel,
        out_shape=jax.ShapeDtypeStruct((n, ne, cap, D), BF16),
        grid=(ne, n // g),
        in_specs=[pl.BlockSpec((g, 1, cap, wdt), lambda e, c: (c, e, 0, 0)),
                  pl.BlockSpec((1, D, F), lambda e, c: (e, 0, 0)),
                  pl.BlockSpec((1, D, F), lambda e, c: (e, 0, 0)),
                  pl.BlockSpec((1, F, D), lambda e, c: (e, 0, 0))],
        out_specs=pl.BlockSpec((g, 1, cap, D), lambda e, c: (c, e, 0, 0)),
        scratch_shapes=[pltpu.VMEM((D, F), BF16), pltpu.VMEM((D, F), BF16), pltpu.VMEM((F, D), BF16)],
        compiler_params=_cparams(("parallel", "arbitrary")),
        name="moe_ffn",
    )(srt, wg, wu, wd)


def _moe_combine_kernel(y_ref, comb_ref, x_ref, nfin_ref, o_ref, *, final_norm):
    tm = x_ref.shape[0]
    comb = comb_ref[...]
    a = comb > 0.0
    before = (lax.broadcasted_iota(jnp.int32, (tm, tm), 1)
              < lax.broadcasted_iota(jnp.int32, (tm, tm), 0))
    rank = jnp.dot(jnp.where(before, 1.0, 0.0).astype(BF16), jnp.where(a, 1.0, 0.0).astype(BF16),
                   preferred_element_type=F32)
    key = jnp.where(a, rank, -1.0).astype(BF16)
    ncol = N_EXPERTS * MOE_CAP
    spread = (lax.broadcasted_iota(jnp.int32, (LANES, ncol), 1) // MOE_CAP
              == lax.broadcasted_iota(jnp.int32, (LANES, ncol), 0))
    key_all = jnp.dot(key, jnp.where(spread, 1.0, 0.0).astype(BF16), preferred_element_type=F32)
    slot = (lax.broadcasted_iota(jnp.int32, (tm, ncol), 1) % MOE_CAP).astype(F32)
    pc = jnp.where(slot == key_all, 1.0, 0.0).astype(BF16)
    y = jnp.concatenate([y_ref[0, e] for e in range(N_EXPERTS)], axis=0)
    xn = x_ref[...] + jnp.dot(pc, y, preferred_element_type=F32)
    if final_norm:
        ms = jnp.mean(xn * xn, axis=-1, keepdims=True)
        xn = xn * lax.rsqrt(ms + EPS) * nfin_ref[...]
    o_ref[...] = xn


def _moe_combine(y, comb, x2, nfin, final_norm):
    T, D = x2.shape
    n, ne, cap, _ = y.shape
    tm = T // n
    kern = functools.partial(_moe_combine_kernel, final_norm=final_norm)
    return pl.pallas_call(
        kern,
        out_shape=jax.ShapeDtypeStruct((T, D), F32),
        grid=(n,),
        in_specs=[pl.BlockSpec((1, ne, cap, D), lambda i: (i, 0, 0, 0)),
                  pl.BlockSpec((tm, LANES), lambda i: (i, 0)),
                  pl.BlockSpec((tm, D), lambda i: (i, 0)),
                  pl.BlockSpec((1, D), lambda i: (0, 0))],
        out_specs=pl.BlockSpec((tm, D), lambda i: (i, 0)),
        compiler_params=_cparams(("parallel",)),
        name="moe_combine",
    )(y, comb, x2, nfin)


def _moe_fix_kernel(tiles_ref, experts_ref, first_ref, n_ref, h_ref, comb_ref, prev_ref,
                    wg_ref, wu_ref, wd_ref, o_ref):
    del tiles_ref
    s = pl.program_id(0)

    @pl.when(s < n_ref[0])
    def _():
        e = experts_ref[s]
        tm = h_ref.shape[0]
        comb = comb_ref[...]
        a = jnp.where(comb > 0.0, 1.0, 0.0)
        before = (lax.broadcasted_iota(jnp.int32, (tm, tm), 1)
                  < lax.broadcasted_iota(jnp.int32, (tm, tm), 0))
        rank = jnp.dot(jnp.where(before, 1.0, 0.0).astype(BF16), a.astype(BF16),
                       preferred_element_type=F32)
        lane = lax.broadcasted_iota(jnp.int32, comb.shape, 1)
        dropped = (lane == e) & (rank >= MOE_CAP)
        c = jnp.sum(jnp.where(dropped, comb, 0.0), axis=-1, keepdims=True)
        h = h_ref[...]
        hid = (jax.nn.silu(jnp.dot(h, wg_ref[0].astype(BF16), preferred_element_type=F32))
               * jnp.dot(h, wu_ref[0].astype(BF16), preferred_element_type=F32))
        add = c * jnp.dot(hid.astype(BF16), wd_ref[0].astype(BF16), preferred_element_type=F32)
        fresh = first_ref[s] == 1

        @pl.when(fresh)
        def _():
            o_ref[...] = prev_ref[...] + add

        @pl.when(jnp.logical_not(fresh))
        def _():
            o_ref[...] += add


def _moe_fix(tiles, experts, first, n, out, h, comb, wg, wu, wd):
    T, D = out.shape
    tm = min(TM_DISP, T)
    F = wg.shape[2]
    tile = lambda width: pl.BlockSpec((tm, width), lambda s, tl, ex, fi, n: (tl[s], 0))
    wspec = lambda shape: pl.BlockSpec(shape, lambda s, tl, ex, fi, n: (ex[s], 0, 0))
    return pl.pallas_call(
        _moe_fix_kernel,
        out_shape=jax.ShapeDtypeStruct((T, D), F32),
        grid_spec=pltpu.PrefetchScalarGridSpec(
            num_scalar_prefetch=4,
            grid=(MAX_OVF,),
            in_specs=[tile(D), tile(LANES), tile(D), wspec((1, D, F)), wspec((1, D, F)),
                      wspec((1, F, D))],
            out_specs=tile(D)),
        input_output_aliases={6: 0},
        compiler_params=_cparams(("arbitrary",)),
        name="moe_fix",
    )(tiles, experts, first, n, h, comb, out, wg, wu, wd)


def _final_norm_kernel(x_ref, g_ref, o_ref):
    x = x_ref[...]
    ms = jnp.mean(x * x, axis=-1, keepdims=True)
    o_ref[...] = x * lax.rsqrt(ms + EPS) * g_ref[...]


def _final_norm(x2, g):
    T, D = x2.shape
    tm = min(TM_PROJ, T)
    return pl.pallas_call(
        _final_norm_kernel,
        out_shape=jax.ShapeDtypeStruct((T, D), F32),
        grid=(T // tm,),
        in_specs=[pl.BlockSpec((tm, D), lambda i: (i, 0)), pl.BlockSpec((1, D), lambda i: (0, 0))],
        out_specs=pl.BlockSpec((tm, D), lambda i: (i, 0)),
        compiler_params=_cparams(("parallel",)),
        name="final_norm",
    )(x2, g)


def _moe(h, comb, x2, wg, wu, wd, nfin, final_norm):
    srt, cnt = _moe_dispatch(h, comb)
    over = (cnt[:, :, 0] > MOE_CAP).reshape(-1)
    n_ovf = jnp.sum(over.astype(jnp.int32))
    pairs = jnp.nonzero(over, size=MAX_OVF, fill_value=0)[0].astype(jnp.int32)
    pairs = jnp.where(jnp.arange(MAX_OVF) < n_ovf, pairs, pairs[jnp.clip(n_ovf - 1, 0, MAX_OVF - 1)])
    tiles, experts = pairs // N_EXPERTS, pairs % N_EXPERTS
    first = jnp.concatenate([jnp.ones((1,), jnp.int32),
                             (tiles[1:] != tiles[:-1]).astype(jnp.int32)])

    def routed():
        y = _moe_ffn(srt, wg, wu, wd)

        def fixed():
            out = _moe_combine(y, comb, x2, nfin, False)
            out = _moe_fix(tiles, experts, first, n_ovf.reshape(1), out, h, comb, wg, wu, wd)
            return _final_norm(out, nfin) if final_norm else out

        return lax.cond(n_ovf > 0, fixed, lambda: _moe_combine(y, comb, x2, nfin, final_norm))

    return lax.cond(n_ovf > MAX_OVF,
                    lambda: _moe_dense(h, comb, x2, wg, wu, wd, nfin, final_norm), routed)


def kernel(x, rel_bias, norm_mix, w_in, diff_lambda, diff_subln, sgu_ln_g, sgu_ln_b, sgu_w, sgu_b,
           w_branch, w_out, norm_ffn, w_router_grp, b_router_grp, w_router_exp, b_router_exp,
           w_gate, w_up, w_down, norm_final):
    B, S, D = x.shape
    T = B * S
    depth = w_in.shape[0]
    a_out = HA * 2 * DA
    grp_w = HB * DB
    b_cols = 3 * NG_B * grp_w
    qkv_b0 = 3 * a_out
    zc0 = qkv_b0 + b_cols
    gate0 = zc0 + 2 * MIX_W
    qk_scale = DA ** -0.5

    bias_a, cfar = _attn_a_bias(rel_bias)
    bias_b = [_attn_b_bias(rel_bias, g) for g in range(NG_B)]

    def group_cols(w, g):
        q = w[:, qkv_b0 + g * grp_w: qkv_b0 + (g + 1) * grp_w] * qk_scale
        k = w[:, qkv_b0 + (NG_B + g) * grp_w: qkv_b0 + (NG_B + g + 1) * grp_w]
        v = w[:, qkv_b0 + (2 * NG_B + g) * grp_w: qkv_b0 + (2 * NG_B + g + 1) * grp_w]
        return [q, k, v]

    x2 = x.reshape(T, D)
    for i in range(depth):
        w = w_in[i]
        nm = norm_mix[i][None, :]
        w_main = jnp.concatenate([w[:, :a_out] * (qk_scale * LOG2E), w[:, a_out:2 * a_out],
                                  w[:, zc0:]] + group_cols(w, 0), axis=1).astype(BF16)
        proj2 = _inproj(x2, nm, w_main)
        x3 = x2.reshape(B, S, D)
        vt = _inproj_t(x3, nm, w[:, 2 * a_out:3 * a_out].T.astype(BF16))
        proj3 = proj2.reshape(B, S, proj2.shape[1])

        lam_init = 0.8 - 0.6 * math.exp(-0.3 * i)
        lp = diff_lambda[i].astype(F32)
        lam = jnp.exp(jnp.sum(lp[0] * lp[1])) - jnp.exp(jnp.sum(lp[2] * lp[3])) + lam_init
        ya = _attn_a(proj3, vt, lam.reshape(1), cfar, bias_a, diff_subln[i][None, :], lam_init)

        obs, lses = [], []
        for g in range(NG_B):
            r = DILATIONS[g]
            if r == 1:
                qkv4, cols = proj3[:, None], (COL_QKV0, COL_QKV0 + 1, COL_QKV0 + 2)
            else:
                w_g = jnp.concatenate(group_cols(w, g), axis=1).astype(BF16)
                qkv4, cols = _inproj_perm(x3, nm, w_g, r), (0, 1, 2)
            o, l = _attn_b(qkv4, bias_b[g], g, cols)
            obs.append(o)
            lses.append(l)

        b_exp = jnp.repeat(sgu_b[i].T, MIX_W // C_GROUPS, axis=1)
        yc = _sgu(proj2, sgu_ln_g[i][None, :], sgu_ln_b[i][None, :], sgu_w[i].astype(BF16), b_exp)

        wr = jnp.concatenate([w_router_exp[i].transpose(1, 0, 2).reshape(D, N_EXPERTS),
                              w_router_grp[i]], axis=1)
        wr = jnp.pad(wr, ((0, 0), (0, LANES - wr.shape[1]))).astype(BF16)
        br = jnp.concatenate([b_router_exp[i].reshape(N_EXPERTS), b_router_grp[i]])
        br = jnp.pad(br, (0, LANES - br.shape[0]))[None, :].astype(F32)

        x2, h, comb = _mix(x2, ya.reshape(T, a_out), obs, lses, yc, proj2,
                           w_branch[i].astype(BF16), w_out[i].astype(BF16), norm_ffn[i][None, :],
                           wr, br)
        x2 = _moe(h, comb, x2, w_gate[i], w_up[i], w_down[i], norm_final[None, :], i == depth - 1)
    return x2.reshape(B, S, D)
```

```python
import functools
import math

import jax
import jax.numpy as jnp
from jax import lax
from jax.experimental import pallas as pl
from jax.experimental.pallas import tpu as pltpu

F32 = jnp.float32
BF16 = jnp.bfloat16

EPS = 1e-6
NEG = -1e30
LOG2E = 1.4426950408889634
LANES = 128
HALF_LANES = LANES // 2
VMEM_LIMIT = 48 * 1024 * 1024

HA = 4
DA = 64
MIX_W = 512
WINDOWS = (128, 512, 2048)
DILATIONS = (1, 4, 16)
NG_B = 3
HB = 8
DB = 64
HALF_WIN = 64
CHUNK = 128
C_GROUPS = 4
N_BRANCH = 3
N_BUCKETS = 32
MAX_DIST = 128
N_GROUPS = 4
E_PER_GROUP = 4
N_EXPERTS = N_GROUPS * E_PER_GROUP
N_SLABS = MIX_W // LANES

TM_PROJ = 1024
TN_PROJ = 3328
TM_PERM = 512
PERM_BLK = 256
T_ATT = 512
QB_DIL = 128
KW_DIL = QB_DIL + 2 * HALF_WIN
ITEMS_DIL = 4
TM_SGU = 512
TM_MIX = 512
TM_MOE = 1024
TM_DISP = 256
MOE_CAP = HALF_LANES
G_FFN = 8
MAX_OVF = 64

COL_ZU = 2
COL_GATE = 2
COL_QKV0 = 10


def _cparams(sem):
    return pltpu.CompilerParams(dimension_semantics=sem, vmem_limit_bytes=VMEM_LIMIT)


def _t5_bucket(rel):
    nb = N_BUCKETS // 2
    max_exact = nb // 2
    ret = (rel > 0).astype(jnp.int32) * nb
    n = jnp.abs(rel)
    nf = jnp.maximum(n, 1).astype(F32)
    large = max_exact + (jnp.log(nf / max_exact) / math.log(MAX_DIST / max_exact)
                         * (nb - max_exact)).astype(jnp.int32)
    large = jnp.minimum(large, nb - 1)
    return ret + jnp.where(n < max_exact, n, large)


def _bias_lookup(bucket, tab):
    out = jnp.zeros((tab.shape[1],) + bucket.shape, F32)
    expand = (slice(None),) + (None,) * bucket.ndim
    for b in range(N_BUCKETS):
        out = jnp.where(bucket[None] == b, tab[b][expand], out)
    return out


def _rms_bf16(x, g):
    ms = jnp.mean(x * x, axis=-1, keepdims=True)
    return (x * lax.rsqrt(ms + EPS) * g).astype(BF16)


def _inproj_kernel(x_ref, g_ref, w_ref, o_ref, h_scr):
    @pl.when(pl.program_id(1) == 0)
    def _():
        h_scr[...] = _rms_bf16(x_ref[...], g_ref[...])

    o_ref[...] = jnp.dot(h_scr[...], w_ref[...], preferred_element_type=F32).astype(o_ref.dtype)


def _inproj(x2, g, w):
    T, D = x2.shape
    N = w.shape[1]
    tm = min(TM_PROJ, T)
    return pl.pallas_call(
        _inproj_kernel,
        out_shape=jax.ShapeDtypeStruct((T, N), BF16),
        grid=(T // tm, N // TN_PROJ),
        in_specs=[pl.BlockSpec((tm, D), lambda i, j: (i, 0)),
                  pl.BlockSpec((1, D), lambda i, j: (0, 0)),
                  pl.BlockSpec((D, TN_PROJ), lambda i, j: (0, j))],
        out_specs=pl.BlockSpec((tm, TN_PROJ), lambda i, j: (i, j)),
        scratch_shapes=[pltpu.VMEM((tm, D), BF16)],
        compiler_params=_cparams(("parallel", "arbitrary")),
        name="inproj",
    )(x2, g, w)


def _inproj_t_kernel(x_ref, g_ref, wt_ref, o_ref):
    h = _rms_bf16(x_ref[0], g_ref[...])
    res = lax.dot_general(wt_ref[...], h, (((1,), (1,)), ((), ())),
                          preferred_element_type=F32).astype(o_ref.dtype)
    for hd in range(o_ref.shape[1]):
        for n in range(o_ref.shape[2]):
            o_ref[0, hd, n] = res[hd * LANES:(hd + 1) * LANES, n * T_ATT:(n + 1) * T_ATT]


def _inproj_t(x3, g, wt):
    B, S, D = x3.shape
    N = wt.shape[0]
    tm = min(TM_PROJ, S)
    nh, nb = N // LANES, tm // T_ATT
    return pl.pallas_call(
        _inproj_t_kernel,
        out_shape=jax.ShapeDtypeStruct((B, nh, S // T_ATT, LANES, T_ATT), BF16),
        grid=(B, S // tm),
        in_specs=[pl.BlockSpec((1, tm, D), lambda b, i: (b, i, 0)),
                  pl.BlockSpec((1, D), lambda b, i: (0, 0)),
                  pl.BlockSpec((N, D), lambda b, i: (0, 0))],
        out_specs=pl.BlockSpec((1, nh, nb, LANES, T_ATT), lambda b, i: (b, 0, i, 0, 0)),
        compiler_params=_cparams(("parallel", "parallel")),
        name="inproj_t",
    )(x3, g, wt)


def _inproj_perm_kernel(x_ref, g_ref, p_ref, w_ref, o_ref, *, r):
    h = _rms_bf16(x_ref[0], g_ref[...])
    nblk = h.shape[0] // PERM_BLK
    hp = jnp.concatenate(
        [jnp.dot(p_ref[...], h[k * PERM_BLK:(k + 1) * PERM_BLK], preferred_element_type=F32)
         for k in range(nblk)], axis=0).astype(BF16)
    res = jnp.dot(hp, w_ref[...], preferred_element_type=F32).astype(o_ref.dtype)
    n = PERM_BLK // r
    for k in range(nblk):
        for s in range(r):
            o_ref[0, s, k * n:(k + 1) * n, :] = res[k * PERM_BLK + s * n:k * PERM_BLK + (s + 1) * n, :]


def _inproj_perm(x3, g, w, r):
    B, S, D = x3.shape
    N = w.shape[1]
    tm = min(TM_PERM, S)
    n = PERM_BLK // r
    o = jnp.arange(PERM_BLK, dtype=jnp.int32)
    src = (o % n) * r + o // n
    perm = (src[:, None] == jnp.arange(PERM_BLK, dtype=jnp.int32)[None, :]).astype(BF16)
    kern = functools.partial(_inproj_perm_kernel, r=r)
    return pl.pallas_call(
        kern,
        out_shape=jax.ShapeDtypeStruct((B, r, S // r, N), BF16),
        grid=(B, S // tm),
        in_specs=[pl.BlockSpec((1, tm, D), lambda b, i: (b, i, 0)),
                  pl.BlockSpec((1, D), lambda b, i: (0, 0)),
                  pl.BlockSpec((PERM_BLK, PERM_BLK), lambda b, i: (0, 0)),
                  pl.BlockSpec((D, N), lambda b, i: (0, 0))],
        out_specs=pl.BlockSpec((1, r, tm // r, N), lambda b, i: (b, 0, i, 0)),
        compiler_params=_cparams(("parallel", "parallel")),
        name=f"inproj_perm_{r}",
    )(x3, g, perm, w)


def _attn_a_kernel(lam_ref, cfar_ref, q_ref, k_ref, vt_ref, bias_ref, g_ref, o_ref, *, out_scale):
    h, qi = pl.program_id(1), pl.program_id(2)
    t = T_ATT
    nk = k_ref.shape[1] // t
    q = q_ref[0]
    low_half = lax.broadcasted_iota(jnp.int32, (1, LANES), 1) < HALF_LANES
    zero = jnp.zeros_like(q)
    qs = jnp.concatenate([jnp.where(low_half, q, zero), jnp.where(low_half, zero, q)], axis=0)

    blocks, sts, shifts = [], [], []
    m = None
    for d in range(-1, nk - 1):
        a = lax.rem(qi + (d + nk), nk)
        delta = a - qi
        kb = k_ref[0, pl.ds(pl.multiple_of(a * t, t), t), :]
        st = lax.dot_general(kb, qs, (((1,), (1,)), ((), ())), preferred_element_type=F32)
        if d <= 1:
            tile = bias_ref[0, jnp.clip(delta, -2, 2) + 2]
            st = st + jnp.concatenate([tile, tile], axis=1)
            shift = None
            cm = jnp.max(st, axis=0, keepdims=True)
        else:
            shift = cfar_ref[2 * h + (delta > 0).astype(jnp.int32)]
            cm = jnp.max(st, axis=0, keepdims=True) + shift
        m = cm if m is None else jnp.maximum(m, cm)
        blocks.append(a)
        sts.append(st)
        shifts.append(shift)

    l = jnp.zeros_like(m)
    accs = [jnp.zeros((LANES, t), F32), jnp.zeros((LANES, t), F32)]
    for a, st, shift in zip(blocks, sts, shifts):
        p = jnp.exp2(st - (m if shift is None else m - shift))
        l = l + jnp.sum(p, axis=0, keepdims=True)
        pb = p.astype(BF16)
        vt = vt_ref[0, 0, a]
        for c in range(2):
            accs[c] = accs[c] + jnp.dot(vt, pb[:, c * t:(c + 1) * t], preferred_element_type=F32)

    ot = accs[0] / l[:, :t] - lam_ref[0] * (accs[1] / l[:, t:])
    o = ot.T
    ms = jnp.mean(o * o, axis=-1, keepdims=True)
    o_ref[0] = (o * lax.rsqrt(ms + EPS) * g_ref[...] * out_scale).astype(o_ref.dtype)


def _attn_a(proj3, vt, lam, cfar, bias5, subln_g, lam_init):
    B, S, _ = proj3.shape
    t = T_ATT
    nk = S // t
    kern = functools.partial(_attn_a_kernel, out_scale=1.0 - lam_init)
    return pl.pallas_call(
        kern,
        out_shape=jax.ShapeDtypeStruct((B, S, HA * 2 * DA), BF16),
        grid=(B, HA, nk),
        in_specs=[
            pl.BlockSpec(memory_space=pltpu.SMEM),
            pl.BlockSpec(memory_space=pltpu.SMEM),
            pl.BlockSpec((1, t, LANES), lambda b, h, qi: (b, qi, h)),
            pl.BlockSpec((1, S, LANES), lambda b, h, qi: (b, 0, HA + h)),
            pl.BlockSpec((1, 1, nk, LANES, t), lambda b, h, qi: (b, h, 0, 0, 0)),
            pl.BlockSpec((1, 5, t, t), lambda b, h, qi: (h, 0, 0, 0)),
            pl.BlockSpec((1, LANES), lambda b, h, qi: (0, 0)),
        ],
        out_specs=pl.BlockSpec((1, t, LANES), lambda b, h, qi: (b, qi, h)),
        compiler_params=_cparams(("parallel", "parallel", "arbitrary")),
        name="diff_attn",
    )(lam, cfar, proj3, proj3, vt, bias5, subln_g)


def _attn_a_bias(rel_bias):
    t = T_ATT
    tab = rel_bias[:, :HA].astype(F32) * LOG2E
    d = jnp.arange(-1, 2, dtype=jnp.int32)[:, None, None] * t
    rel = d + jnp.arange(t, dtype=jnp.int32)[None, :, None] - jnp.arange(t, dtype=jnp.int32)[None, None, :]
    near = _bias_lookup(_t5_bucket(rel), tab)
    far = tab[_t5_bucket(jnp.array([-(t + 1), t + 1], dtype=jnp.int32))].T
    fill = lambda side: jnp.broadcast_to(far[:, side, None, None, None], (HA, 1, t, t))
    tiles = jnp.concatenate([fill(0), near, fill(1)], axis=1)
    return tiles, far.reshape(2 * HA)


def _attn_b_kernel(q_ref, k_ref, v_ref, bias_ref, o_ref, lse_ref, *, sub_len, r, sp, qp):
    nblk = sub_len // QB_DIL
    low_half = lax.broadcasted_iota(jnp.int32, (1, LANES), 1) < HALF_LANES
    for si in range(sp):
        s = si if sp == r else pl.program_id(2) * sp + si
        for qb in range(qp):
            i = pl.program_id(1) * qp + qb
            start = jnp.clip(i * QB_DIL - HALF_WIN, 0, sub_len - KW_DIL)
            start = pl.multiple_of(start, HALF_WIN)
            variant = jnp.where(i == 0, 0, jnp.where(i == nblk - 1, 2, 1))
            q = q_ref[0, si, qb * QB_DIL:(qb + 1) * QB_DIL, :]
            kw = k_ref[0, s, pl.ds(start, KW_DIL), :]
            vw = v_ref[0, s, pl.ds(start, KW_DIL), :]
            rows = (slice(qb * QB_DIL, (qb + 1) * QB_DIL) if r == 1
                    else pl.ds(s, QB_DIL, stride=r))
            for j in range(HB // 2):
                cols = slice(j * LANES, (j + 1) * LANES)
                qpair, kp, vp = q[:, cols], kw[:, cols], vw[:, cols]
                outs, lses = [], []
                for c in range(2):
                    qc = jnp.where(low_half if c == 0 else jnp.logical_not(low_half), qpair,
                                   jnp.zeros_like(qpair))
                    sc = lax.dot_general(qc, kp, (((1,), (1,)), ((), ())),
                                         preferred_element_type=F32)
                    sc = sc + bias_ref[2 * j + c, variant]
                    m = jnp.max(sc, axis=-1, keepdims=True)
                    p = jnp.exp(sc - m)
                    l = jnp.sum(p, axis=-1, keepdims=True)
                    outs.append(jnp.dot(p.astype(BF16), vp, preferred_element_type=F32) / l)
                    lses.append(m + jnp.log(l))
                o_ref[0, j, rows, :] = jnp.where(low_half, outs[0], outs[1])
                lse_ref[0, j, rows, :] = jnp.where(low_half, lses[0], lses[1])


def _attn_b(qkv4, bias3, g, cols):
    B, r, L, _ = qkv4.shape
    S = r * L
    width = HB * DB
    nblk = L // QB_DIL
    sp = min(r, ITEMS_DIL)
    qp = ITEMS_DIL // sp
    qcol, kcol, vcol = cols
    kern = functools.partial(_attn_b_kernel, sub_len=L, r=r, sp=sp, qp=qp)
    slab = jax.ShapeDtypeStruct((B, N_SLABS, S, LANES), F32)
    slab_spec = pl.BlockSpec((1, N_SLABS, QB_DIL * r * qp, LANES), lambda b, i, s: (b, 0, i, 0))
    return pl.pallas_call(
        kern,
        out_shape=[slab, slab],
        grid=(B, nblk // qp, r // sp),
        in_specs=[
            pl.BlockSpec((1, sp, QB_DIL * qp, width), lambda b, i, s: (b, s, i, qcol)),
            pl.BlockSpec((1, r, L, width), lambda b, i, s: (b, 0, 0, kcol)),
            pl.BlockSpec((1, r, L, width), lambda b, i, s: (b, 0, 0, vcol)),
            pl.BlockSpec((HB, 3, QB_DIL, KW_DIL), lambda b, i, s: (0, 0, 0, 0)),
        ],
        out_specs=[slab_spec, slab_spec],
        compiler_params=_cparams(("parallel", "arbitrary", "arbitrary")),
        name=f"dilated_attn_{g}",
    )(qkv4, qkv4, qkv4, bias3)


def _attn_b_bias(rel_bias, g):
    r = DILATIONS[g]
    tab = rel_bias[:, HA + g * HB: HA + (g + 1) * HB].astype(F32)
    off = jnp.arange(3, dtype=jnp.int32)[:, None, None] * HALF_WIN
    rel = (jnp.arange(KW_DIL, dtype=jnp.int32)[None, None, :] - off
           - jnp.arange(QB_DIL, dtype=jnp.int32)[None, :, None])
    bias = _bias_lookup(_t5_bucket(rel * r), tab)
    return jnp.where((jnp.abs(rel) <= HALF_WIN)[None], bias, NEG)


def _sgu_kernel(zu_ref, zv_ref, lng_ref, lnb_ref, ws_ref, bs_ref, o_ref):
    u = jax.nn.gelu(zu_ref[...].astype(F32))
    v = jax.nn.gelu(zv_ref[...].astype(F32))
    mu = jnp.mean(v, axis=-1, keepdims=True)
    var = jnp.mean(jnp.square(v - mu), axis=-1, keepdims=True)
    v = ((v - mu) * lax.rsqrt(var + EPS) * lng_ref[...] + lnb_ref[...]).astype(BF16)
    gd = v.shape[1] // C_GROUPS
    for n in range(v.shape[0] // CHUNK):
        rows = slice(n * CHUNK, (n + 1) * CHUNK)
        for g in range(C_GROUPS):
            cols = slice(g * gd, (g + 1) * gd)
            mixed = jnp.dot(ws_ref[g], v[rows, cols], preferred_element_type=F32) + bs_ref[:, cols]
            o_ref[rows, cols] = (u[rows, cols] * mixed).astype(o_ref.dtype)


def _sgu(proj2, ln_g, ln_b, w_s, b_exp):
    T = proj2.shape[0]
    tm = min(TM_SGU, T)
    w = MIX_W
    return pl.pallas_call(
        _sgu_kernel,
        out_shape=jax.ShapeDtypeStruct((T, w), BF16),
        grid=(T // tm,),
        in_specs=[pl.BlockSpec((tm, w), lambda i: (i, COL_ZU)),
                  pl.BlockSpec((tm, w), lambda i: (i, COL_ZU + 1)),
                  pl.BlockSpec((1, w), lambda i: (0, 0)),
                  pl.BlockSpec((1, w), lambda i: (0, 0)),
                  pl.BlockSpec((C_GROUPS, CHUNK, CHUNK), lambda i: (0, 0, 0)),
                  pl.BlockSpec((CHUNK, w), lambda i: (0, 0))],
        out_specs=pl.BlockSpec((tm, w), lambda i: (i, 0)),
        compiler_params=_cparams(("parallel",)),
        name="sgu",
    )(proj2, proj2, ln_g, ln_b, w_s, b_exp)


def _route(logits):
    lane = lax.broadcasted_iota(jnp.int32, logits.shape, 1)
    big = jnp.int32(LANES)
    is_grp = (lane >= N_EXPERTS) & (lane < N_EXPERTS + N_GROUPS)
    gl = jnp.where(is_grp, logits, NEG)
    gmax = jnp.max(gl, axis=-1, keepdims=True)
    g_idx = jnp.min(jnp.where(is_grp & (gl == gmax), lane, big), axis=-1, keepdims=True) - N_EXPERTS
    g_w = 1.0 / jnp.sum(jnp.where(is_grp, jnp.exp(gl - gmax), 0.0), axis=-1, keepdims=True)
    in_grp = (lane >= g_idx * E_PER_GROUP) & (lane < (g_idx + 1) * E_PER_GROUP)
    sel = jnp.where(in_grp, logits, NEG)
    v1 = jnp.max(sel, axis=-1, keepdims=True)
    i1 = jnp.min(jnp.where(in_grp & (sel == v1), lane, big), axis=-1, keepdims=True)
    rest = in_grp & (lane != i1)
    sel2 = jnp.where(rest, logits, NEG)
    v2 = jnp.max(sel2, axis=-1, keepdims=True)
    i2 = jnp.min(jnp.where(rest & (sel2 == v2), lane, big), axis=-1, keepdims=True)
    e2 = jnp.exp(v2 - v1)
    w1 = g_w / (1.0 + e2)
    w2 = g_w * e2 / (1.0 + e2)
    return jnp.where(lane == i1, w1, jnp.where(lane == i2, w2, 0.0))


def _mix_kernel(x_ref, ya_ref, ob0_ref, ob1_ref, ob2_ref, ls0_ref, ls1_ref, ls2_ref, yc_ref,
                g0_ref, g1_ref, g2_ref, wb_ref, wo_ref, nf_ref, wr_ref, br_ref,
                xo_ref, h_ref, comb_ref):
    slabs = []
    for j in range(N_SLABS):
        ls0, ls1, ls2 = ls0_ref[0, j], ls1_ref[0, j], ls2_ref[0, j]
        mx = jnp.maximum(jnp.maximum(ls0, ls1), ls2)
        e0, e1, e2 = jnp.exp(ls0 - mx), jnp.exp(ls1 - mx), jnp.exp(ls2 - mx)
        yb = (e0 * ob0_ref[0, j] + e1 * ob1_ref[0, j] + e2 * ob2_ref[0, j]) / (e0 + e1 + e2)
        slabs.append(yb.astype(BF16))
    yb = jnp.concatenate(slabs, axis=-1)
    merged = jax.nn.sigmoid(g0_ref[...].astype(F32)) * jnp.dot(ya_ref[...], wb_ref[0],
                                                               preferred_element_type=F32)
    merged += jax.nn.sigmoid(g1_ref[...].astype(F32)) * jnp.dot(yb, wb_ref[1],
                                                                preferred_element_type=F32)
    merged += jax.nn.sigmoid(g2_ref[...].astype(F32)) * jnp.dot(yc_ref[...], wb_ref[2],
                                                                preferred_element_type=F32)
    xn = x_ref[...] + jnp.dot(merged.astype(BF16), wo_ref[...], preferred_element_type=F32)
    xo_ref[...] = xn
    h = _rms_bf16(xn, nf_ref[...])
    h_ref[...] = h
    logits = jnp.dot(h, wr_ref[...], preferred_element_type=F32) + br_ref[...]
    comb_ref[...] = _route(logits)


def _mix(x2, ya, obs, lses, yc, proj2, wb, wo, nf, wr, br):
    T, D = x2.shape
    S = obs[0].shape[2]
    tm = min(TM_MIX, S)
    per_b = S // tm
    w = MIX_W
    row = lambda width: pl.BlockSpec((tm, width), lambda i: (i, 0))
    full = lambda a: pl.BlockSpec(a.shape, lambda i: (0,) * a.ndim)
    gate = lambda n: pl.BlockSpec((tm, D), lambda i: (i, COL_GATE + n))
    slab = pl.BlockSpec((1, N_SLABS, tm, LANES), lambda i: (i // per_b, 0, i % per_b, 0))
    return pl.pallas_call(
        _mix_kernel,
        out_shape=[jax.ShapeDtypeStruct((T, D), F32), jax.ShapeDtypeStruct((T, D), BF16),
                   jax.ShapeDtypeStruct((T, LANES), F32)],
        grid=(T // tm,),
        in_specs=[row(D), row(w), slab, slab, slab, slab, slab, slab, row(w),
                  gate(0), gate(1), gate(2), full(wb), full(wo), full(nf), full(wr), full(br)],
        out_specs=[row(D), row(D), row(LANES)],
        compiler_params=_cparams(("parallel",)),
        name="mix",
    )(x2, ya, obs[0], obs[1], obs[2], lses[0], lses[1], lses[2], yc, proj2, proj2, proj2,
      wb, wo, nf, wr, br)


def _moe_kernel(h_ref, comb_ref, x_ref, wg_ref, wu_ref, wd_ref, nfin_ref, o_ref, acc_scr,
                *, final_norm):
    e = pl.program_id(1)

    @pl.when(e == 0)
    def _():
        acc_scr[...] = jnp.zeros(acc_scr.shape, F32)

    h = h_ref[...]
    lane = lax.broadcasted_iota(jnp.int32, comb_ref.shape, 1)
    c = jnp.sum(jnp.where(lane == e, comb_ref[...], 0.0), axis=-1, keepdims=True)
    hid = (jax.nn.silu(jnp.dot(h, wg_ref[0].astype(BF16), preferred_element_type=F32))
           * jnp.dot(h, wu_ref[0].astype(BF16), preferred_element_type=F32))
    acc_scr[...] += c * jnp.dot(hid.astype(BF16), wd_ref[0].astype(BF16),
                                preferred_element_type=F32)

    @pl.when(e == pl.num_programs(1) - 1)
    def _():
        xn = x_ref[...] + acc_scr[...]
        if final_norm:
            ms = jnp.mean(xn * xn, axis=-1, keepdims=True)
            xn = xn * lax.rsqrt(ms + EPS) * nfin_ref[...]
        o_ref[...] = xn


def _moe_dense(h, comb, x2, wg, wu, wd, nfin, final_norm):
    T, D = x2.shape
    tm = min(TM_MOE, T)
    F = wg.shape[2]
    kern = functools.partial(_moe_kernel, final_norm=final_norm)
    return pl.pallas_call(
        kern,
        out_shape=jax.ShapeDtypeStruct((T, D), F32),
        grid=(T // tm, N_EXPERTS),
        in_specs=[pl.BlockSpec((tm, D), lambda i, e: (i, 0)),
                  pl.BlockSpec((tm, LANES), lambda i, e: (i, 0)),
                  pl.BlockSpec((tm, D), lambda i, e: (i, 0)),
                  pl.BlockSpec((1, D, F), lambda i, e: (e, 0, 0)),
                  pl.BlockSpec((1, D, F), lambda i, e: (e, 0, 0)),
                  pl.BlockSpec((1, F, D), lambda i, e: (e, 0, 0)),
                  pl.BlockSpec((1, D), lambda i, e: (0, 0))],
        out_specs=pl.BlockSpec((tm, D), lambda i, e: (i, 0)),
        scratch_shapes=[pltpu.VMEM((tm, D), F32)],
        compiler_params=_cparams(("parallel", "arbitrary")),
        name="moe_dense",
    )(h, comb, x2, wg, wu, wd, nfin)


def _moe_dispatch_kernel(h_ref, comb_ref, o_ref, cnt_ref):
    tm = h_ref.shape[0]
    comb = comb_ref[...]
    hi = comb.astype(BF16)
    lo = (comb - hi.astype(F32)).astype(BF16)
    haug = jnp.concatenate([h_ref[...], hi, lo], axis=1)
    a_t = comb.T[:N_EXPERTS] > 0.0
    a_f = jnp.where(a_t, 1.0, 0.0)
    before = (lax.broadcasted_iota(jnp.int32, (tm, tm), 0)
              < lax.broadcasted_iota(jnp.int32, (tm, tm), 1))
    rank_t = jnp.dot(a_f.astype(BF16), jnp.where(before, 1.0, 0.0).astype(BF16),
                     preferred_element_type=F32)
    slot = lax.broadcasted_iota(jnp.int32, (MOE_CAP, tm), 0).astype(F32)
    blocks = [jnp.where((slot == rank_t[e:e + 1]) & a_t[e:e + 1], 1.0, 0.0).astype(BF16)
              for e in range(N_EXPERTS)]
    res = jnp.dot(jnp.concatenate(blocks, axis=0), haug, preferred_element_type=F32)
    res = res.astype(o_ref.dtype)
    for e in range(N_EXPERTS):
        o_ref[0, e] = res[e * MOE_CAP:(e + 1) * MOE_CAP]
    cnt_ref[0] = jnp.broadcast_to(jnp.sum(a_f, axis=1, keepdims=True), cnt_ref.shape[1:])


def _moe_dispatch(h, comb):
    T, D = h.shape
    tm = min(TM_DISP, T)
    n = T // tm
    return pl.pallas_call(
        _moe_dispatch_kernel,
        out_shape=[jax.ShapeDtypeStruct((n, N_EXPERTS, MOE_CAP, D + 2 * LANES), BF16),
                   jax.ShapeDtypeStruct((n, N_EXPERTS, LANES), F32)],
        grid=(n,),
        in_specs=[pl.BlockSpec((tm, D), lambda i: (i, 0)),
                  pl.BlockSpec((tm, LANES), lambda i: (i, 0))],
        out_specs=[pl.BlockSpec((1, N_EXPERTS, MOE_CAP, D + 2 * LANES), lambda i: (i, 0, 0, 0)),
                   pl.BlockSpec((1, N_EXPERTS, LANES), lambda i: (i, 0, 0))],
        compiler_params=_cparams(("parallel",)),
        name="moe_dispatch",
    )(h, comb)


def _moe_ffn_kernel(s_ref, wg_ref, wu_ref, wd_ref, o_ref, wg_scr, wu_scr, wd_scr):
    e = pl.program_id(0)

    @pl.when(pl.program_id(1) == 0)
    def _():
        wg_scr[...] = wg_ref[0].astype(BF16)
        wu_scr[...] = wu_ref[0].astype(BF16)
        wd_scr[...] = wd_ref[0].astype(BF16)

    g = s_ref.shape[0]
    D = o_ref.shape[-1]
    rows = jnp.concatenate([s_ref[t, 0] for t in range(g)], axis=0)
    h = rows[:, :D]
    wparts = rows[:, D:].astype(F32)
    lane = lax.broadcasted_iota(jnp.int32, wparts.shape, 1)
    w = jnp.sum(jnp.where(lane % LANES == e, wparts, 0.0), axis=-1, keepdims=True)
    hid = (jax.nn.silu(jnp.dot(h, wg_scr[...], preferred_element_type=F32))
           * jnp.dot(h, wu_scr[...], preferred_element_type=F32))
    y = (w * jnp.dot(hid.astype(BF16), wd_scr[...], preferred_element_type=F32)).astype(o_ref.dtype)
    cap = s_ref.shape[2]
    for t in range(g):
        o_ref[t, 0] = y[t * cap:(t + 1) * cap]


def _moe_ffn(srt, wg, wu, wd):
    n, ne, cap, wdt = srt.shape
    D, F = wg.shape[1], wg.shape[2]
    g = min(G_FFN, n)
    return pl.pallas_call(
        _moe_ffn_kernel,
        out_shape=jax.ShapeDtypeStruct((n, ne, cap, D), BF16),
        grid=(ne, n // g),
        in_specs=[pl.BlockSpec((g, 1, cap, wdt), lambda e, c: (c, e, 0, 0)),
                  pl.BlockSpec((1, D, F), lambda e, c: (e, 0, 0)),
                  pl.BlockSpec((1, D, F), lambda e, c: (e, 0, 0)),
                  pl.BlockSpec((1, F, D), lambda e, c: (e, 0, 0))],
        out_specs=pl.BlockSpec((g, 1, cap, D), lambda e, c: (c, e, 0, 0)),
        scratch_shapes=[pltpu.VMEM((D, F), BF16), pltpu.VMEM((D, F), BF16), pltpu.VMEM((F, D), BF16)],
        compiler_params=_cparams(("parallel", "arbitrary")),
        name="moe_ffn",
    )(srt, wg, wu, wd)


def _moe_combine_kernel(y_ref, comb_ref, x_ref, nfin_ref, o_ref, *, final_norm):
    tm = x_ref.shape[0]
    comb = comb_ref[...]
    a = comb > 0.0
    before = (lax.broadcasted_iota(jnp.int32, (tm, tm), 1)
              < lax.broadcasted_iota(jnp.int32, (tm, tm), 0))
    rank = jnp.dot(jnp.where(before, 1.0, 0.0).astype(BF16), jnp.where(a, 1.0, 0.0).astype(BF16),
                   preferred_element_type=F32)
    key = jnp.where(a, rank, -1.0).astype(BF16)
    ncol = N_EXPERTS * MOE_CAP
    spread = (lax.broadcasted_iota(jnp.int32, (LANES, ncol), 1) // MOE_CAP
              == lax.broadcasted_iota(jnp.int32, (LANES, ncol), 0))
    key_all = jnp.dot(key, jnp.where(spread, 1.0, 0.0).astype(BF16), preferred_element_type=F32)
    slot = (lax.broadcasted_iota(jnp.int32, (tm, ncol), 1) % MOE_CAP).astype(F32)
    pc = jnp.where(slot == key_all, 1.0, 0.0).astype(BF16)
    y = jnp.concatenate([y_ref[0, e] for e in range(N_EXPERTS)], axis=0)
    xn = x_ref[...] + jnp.dot(pc, y, preferred_element_type=F32)
    if final_norm:
        ms = jnp.mean(xn * xn, axis=-1, keepdims=True)
        xn = xn * lax.rsqrt(ms + EPS) * nfin_ref[...]
    o_ref[...] = xn


def _moe_combine(y, comb, x2, nfin, final_norm):
    T, D = x2.shape
    n, ne, cap, _ = y.shape
    tm = T // n
    kern = functools.partial(_moe_combine_kernel, final_norm=final_norm)
    return pl.pallas_call(
        kern,
        out_shape=jax.ShapeDtypeStruct((T, D), F32),
        grid=(n,),
        in_specs=[pl.BlockSpec((1, ne, cap, D), lambda i: (i, 0, 0, 0)),
                  pl.BlockSpec((tm, LANES), lambda i: (i, 0)),
                  pl.BlockSpec((tm, D), lambda i: (i, 0)),
                  pl.BlockSpec((1, D), lambda i: (0, 0))],
        out_specs=pl.BlockSpec((tm, D), lambda i: (i, 0)),
        compiler_params=_cparams(("parallel",)),
        name="moe_combine",
    )(y, comb, x2, nfin)


def _moe_fix_kernel(tiles_ref, experts_ref, first_ref, n_ref, h_ref, comb_ref, prev_ref,
                    wg_ref, wu_ref, wd_ref, o_ref):
    del tiles_ref
    s = pl.program_id(0)

    @pl.when(s < n_ref[0])
    def _():
        e = experts_ref[s]
        tm = h_ref.shape[0]
        comb = comb_ref[...]
        a = jnp.where(comb > 0.0, 1.0, 0.0)
        before = (lax.broadcasted_iota(jnp.int32, (tm, tm), 1)
                  < lax.broadcasted_iota(jnp.int32, (tm, tm), 0))
        rank = jnp.dot(jnp.where(before, 1.0, 0.0).astype(BF16), a.astype(BF16),
                       preferred_element_type=F32)
        lane = lax.broadcasted_iota(jnp.int32, comb.shape, 1)
        dropped = (lane == e) & (rank >= MOE_CAP)
        c = jnp.sum(jnp.where(dropped, comb, 0.0), axis=-1, keepdims=True)
        h = h_ref[...]
        hid = (jax.nn.silu(jnp.dot(h, wg_ref[0].astype(BF16), preferred_element_type=F32))
               * jnp.dot(h, wu_ref[0].astype(BF16), preferred_element_type=F32))
        add = c * jnp.dot(hid.astype(BF16), wd_ref[0].astype(BF16), preferred_element_type=F32)
        fresh = first_ref[s] == 1

        @pl.when(fresh)
        def _():
            o_ref[...] = prev_ref[...] + add

        @pl.when(jnp.logical_not(fresh))
        def _():
            o_ref[...] += add


def _moe_fix(tiles, experts, first, n, out, h, comb, wg, wu, wd):
    T, D = out.shape
    tm = min(TM_DISP, T)
    F = wg.shape[2]
    tile = lambda width: pl.BlockSpec((tm, width), lambda s, tl, ex, fi, n: (tl[s], 0))
    wspec = lambda shape: pl.BlockSpec(shape, lambda s, tl, ex, fi, n: (ex[s], 0, 0))
    return pl.pallas_call(
        _moe_fix_kernel,
        out_shape=jax.ShapeDtypeStruct((T, D), F32),
        grid_spec=pltpu.PrefetchScalarGridSpec(
            num_scalar_prefetch=4,
            grid=(MAX_OVF,),
            in_specs=[tile(D), tile(LANES), tile(D), wspec((1, D, F)), wspec((1, D, F)),
                      wspec((1, F, D))],
            out_specs=tile(D)),
        input_output_aliases={6: 0},
        compiler_params=_cparams(("arbitrary",)),
        name="moe_fix",
    )(tiles, experts, first, n, h, comb, out, wg, wu, wd)


def _final_norm_kernel(x_ref, g_ref, o_ref):
    x = x_ref[...]
    ms = jnp.mean(x * x, axis=-1, keepdims=True)
    o_ref[...] = x * lax.rsqrt(ms + EPS) * g_ref[...]


def _final_norm(x2, g):
    T, D = x2.shape
    tm = min(TM_PROJ, T)
    return pl.pallas_call(
        _final_norm_kernel,
        out_shape=jax.ShapeDtypeStruct((T, D), F32),
        grid=(T // tm,),
        in_specs=[pl.BlockSpec((tm, D), lambda i: (i, 0)), pl.BlockSpec((1, D), lambda i: (0, 0))],
        out_specs=pl.BlockSpec((tm, D), lambda i: (i, 0)),
        compiler_params=_cparams(("parallel",)),
        name="final_norm",
    )(x2, g)


def _moe(h, comb, x2, wg, wu, wd, nfin, final_norm):
    srt, cnt = _moe_dispatch(h, comb)
    over = (cnt[:, :, 0] > MOE_CAP).reshape(-1)
    n_ovf = jnp.sum(over.astype(jnp.int32))
    pairs = jnp.nonzero(over, size=MAX_OVF, fill_value=0)[0].astype(jnp.int32)
    pairs = jnp.where(jnp.arange(MAX_OVF) < n_ovf, pairs, pairs[jnp.clip(n_ovf - 1, 0, MAX_OVF - 1)])
    tiles, experts = pairs // N_EXPERTS, pairs % N_EXPERTS
    first = jnp.concatenate([jnp.ones((1,), jnp.int32),
                             (tiles[1:] != tiles[:-1]).astype(jnp.int32)])

    def routed():
        y = _moe_ffn(srt, wg, wu, wd)

        def fixed():
            out = _moe_combine(y, comb, x2, nfin, False)
            out = _moe_fix(tiles, experts, first, n_ovf.reshape(1), out, h, comb, wg, wu, wd)
            return _final_norm(out, nfin) if final_norm else out

        return lax.cond(n_ovf > 0, fixed, lambda: _moe_combine(y, comb, x2, nfin, final_norm))

    return lax.cond(n_ovf > MAX_OVF,
                    lambda: _moe_dense(h, comb, x2, wg, wu, wd, nfin, final_norm), routed)


def kernel(x, rel_bias, norm_mix, w_in, diff_lambda, diff_subln, sgu_ln_g, sgu_ln_b, sgu_w, sgu_b,
           w_branch, w_out, norm_ffn, w_router_grp, b_router_grp, w_router_exp, b_router_exp,
           w_gate, w_up, w_down, norm_final):
    B, S, D = x.shape
    T = B * S
    depth = w_in.shape[0]
    a_out = HA * 2 * DA
    grp_w = HB * DB
    b_cols = 3 * NG_B * grp_w
    qkv_b0 = 3 * a_out
    zc0 = qkv_b0 + b_cols
    gate0 = zc0 + 2 * MIX_W
    qk_scale = DA ** -0.5

    bias_a, cfar = _attn_a_bias(rel_bias)
    bias_b = [_attn_b_bias(rel_bias, g) for g in range(NG_B)]

    col = jnp.arange(w_in.shape[2])
    col_scale = jnp.where(col < a_out, qk_scale * LOG2E,
                          jnp.where((col >= qkv_b0) & (col < qkv_b0 + NG_B * grp_w), qk_scale, 1.0))

    def group_cols(w, g):
        return [w[:, qkv_b0 + (c * NG_B + g) * grp_w: qkv_b0 + (c * NG_B + g + 1) * grp_w]
                for c in range(3)]

    x2 = x.reshape(T, D)
    for i in range(depth):
        w = (w_in[i] * col_scale.astype(F32)).astype(BF16)
        nm = norm_mix[i][None, :]
        w_main = jnp.concatenate([w[:, :2 * a_out], w[:, zc0:]] + group_cols(w, 0), axis=1)
        proj2 = _inproj(x2, nm, w_main)
        x3 = x2.reshape(B, S, D)
        vt = _inproj_t(x3, nm, w[:, 2 * a_out:3 * a_out].T)
        proj3 = proj2.reshape(B, S, proj2.shape[1])

        lam_init = 0.8 - 0.6 * math.exp(-0.3 * i)
        lp = diff_lambda[i].astype(F32)
        lam = jnp.exp(jnp.sum(lp[0] * lp[1])) - jnp.exp(jnp.sum(lp[2] * lp[3])) + lam_init
        ya = _attn_a(proj3, vt, lam.reshape(1), cfar, bias_a, diff_subln[i][None, :], lam_init)

        obs, lses = [], []
        for g in range(NG_B):
            r = DILATIONS[g]
            if r == 1:
                qkv4, cols = proj3[:, None], (COL_QKV0, COL_QKV0 + 1, COL_QKV0 + 2)
            else:
                w_g = jnp.concatenate(group_cols(w, g), axis=1)
                qkv4, cols = _inproj_perm(x3, nm, w_g, r), (0, 1, 2)
            o, l = _attn_b(qkv4, bias_b[g], g, cols)
            obs.append(o)
            lses.append(l)

        b_exp = jnp.repeat(sgu_b[i].T, MIX_W // C_GROUPS, axis=1)
        yc = _sgu(proj2, sgu_ln_g[i][None, :], sgu_ln_b[i][None, :], sgu_w[i].astype(BF16), b_exp)

        wr = jnp.concatenate([w_router_exp[i].transpose(1, 0, 2).reshape(D, N_EXPERTS),
                              w_router_grp[i]], axis=1)
        wr = jnp.pad(wr, ((0, 0), (0, LANES - wr.shape[1]))).astype(BF16)
        br = jnp.concatenate([b_router_exp[i].reshape(N_EXPERTS), b_router_grp[i]])
        br = jnp.pad(br, (0, LANES - br.shape[0]))[None, :].astype(F32)

        x2, h, comb = _mix(x2, ya.reshape(T, a_out), obs, lses, yc, proj2,
                           w_branch[i].astype(BF16), w_out[i].astype(BF16), norm_ffn[i][None, :],
                           wr, br)
        x2 = _moe(h, comb, x2, w_gate[i], w_up[i], w_down[i], norm_final[None, :], i == depth - 1)
    return x2.reshape(B, S, D)
```

```python
import functools
import math

import jax
import jax.numpy as jnp
from jax import lax
from jax.experimental import pallas as pl
from jax.experimental.pallas import tpu as pltpu

F32 = jnp.float32
BF16 = jnp.bfloat16

EPS = 1e-6
NEG = -1e30
LOG2E = 1.4426950408889634
LANES = 128
HALF_LANES = LANES // 2
VMEM_LIMIT = 48 * 1024 * 1024

HA = 4
DA = 64
MIX_W = 512
WINDOWS = (128, 512, 2048)
DILATIONS = (1, 4, 16)
NG_B = 3
HB = 8
DB = 64
HALF_WIN = 64
CHUNK = 128
C_GROUPS = 4
N_BRANCH = 3
N_BUCKETS = 32
MAX_DIST = 128
N_GROUPS = 4
E_PER_GROUP = 4
N_EXPERTS = N_GROUPS * E_PER_GROUP
N_SLABS = MIX_W // LANES

TM_PROJ = 1024
TN_PROJ = 3328
TM_PERM = 512
PERM_BLK = 256
T_ATT = 512
QB_DIL = 128
KW_DIL = QB_DIL + 2 * HALF_WIN
ITEMS_DIL = 4
TM_SGU = 512
TM_MIX = 512
TM_MOE = 1024
TM_DISP = 256
MOE_CAP = HALF_LANES
G_FFN = 8
FFN_ROW_STEP = 64
BF16_ROWS = 16
MAX_OVF = 64

COL_ZU = 2
COL_GATE = 2
COL_QKV0 = 10


def _cparams(sem):
    return pltpu.CompilerParams(dimension_semantics=sem, vmem_limit_bytes=VMEM_LIMIT)


def _t5_bucket(rel):
    nb = N_BUCKETS // 2
    max_exact = nb // 2
    ret = (rel > 0).astype(jnp.int32) * nb
    n = jnp.abs(rel)
    nf = jnp.maximum(n, 1).astype(F32)
    large = max_exact + (jnp.log(nf / max_exact) / math.log(MAX_DIST / max_exact)
                         * (nb - max_exact)).astype(jnp.int32)
    large = jnp.minimum(large, nb - 1)
    return ret + jnp.where(n < max_exact, n, large)


def _bias_lookup(bucket, tab):
    out = jnp.zeros((tab.shape[1],) + bucket.shape, F32)
    expand = (slice(None),) + (None,) * bucket.ndim
    for b in range(N_BUCKETS):
        out = jnp.where(bucket[None] == b, tab[b][expand], out)
    return out


def _rms_bf16(x, g):
    ms = jnp.mean(x * x, axis=-1, keepdims=True)
    return (x * lax.rsqrt(ms + EPS) * g).astype(BF16)


def _inproj_kernel(x_ref, g_ref, w_ref, o_ref, h_scr):
    @pl.when(pl.program_id(1) == 0)
    def _():
        h_scr[...] = _rms_bf16(x_ref[...], g_ref[...])

    o_ref[...] = jnp.dot(h_scr[...], w_ref[...], preferred_element_type=F32).astype(o_ref.dtype)


def _inproj(x2, g, w):
    T, D = x2.shape
    N = w.shape[1]
    tm = min(TM_PROJ, T)
    return pl.pallas_call(
        _inproj_kernel,
        out_shape=jax.ShapeDtypeStruct((T, N), BF16),
        grid=(T // tm, N // TN_PROJ),
        in_specs=[pl.BlockSpec((tm, D), lambda i, j: (i, 0)),
                  pl.BlockSpec((1, D), lambda i, j: (0, 0)),
                  pl.BlockSpec((D, TN_PROJ), lambda i, j: (0, j))],
        out_specs=pl.BlockSpec((tm, TN_PROJ), lambda i, j: (i, j)),
        scratch_shapes=[pltpu.VMEM((tm, D), BF16)],
        compiler_params=_cparams(("parallel", "arbitrary")),
        name="inproj",
    )(x2, g, w)


def _inproj_t_kernel(x_ref, g_ref, wt_ref, o_ref):
    h = _rms_bf16(x_ref[0], g_ref[...])
    res = lax.dot_general(wt_ref[...], h, (((1,), (1,)), ((), ())),
                          preferred_element_type=F32).astype(o_ref.dtype)
    for hd in range(o_ref.shape[1]):
        for n in range(o_ref.shape[2]):
            o_ref[0, hd, n] = res[hd * LANES:(hd + 1) * LANES, n * T_ATT:(n + 1) * T_ATT]


def _inproj_t(x3, g, wt):
    B, S, D = x3.shape
    N = wt.shape[0]
    tm = min(TM_PROJ, S)
    nh, nb = N // LANES, tm // T_ATT
    return pl.pallas_call(
        _inproj_t_kernel,
        out_shape=jax.ShapeDtypeStruct((B, nh, S // T_ATT, LANES, T_ATT), BF16),
        grid=(B, S // tm),
        in_specs=[pl.BlockSpec((1, tm, D), lambda b, i: (b, i, 0)),
                  pl.BlockSpec((1, D), lambda b, i: (0, 0)),
                  pl.BlockSpec((N, D), lambda b, i: (0, 0))],
        out_specs=pl.BlockSpec((1, nh, nb, LANES, T_ATT), lambda b, i: (b, 0, i, 0, 0)),
        compiler_params=_cparams(("parallel", "parallel")),
        name="inproj_t",
    )(x3, g, wt)


def _inproj_perm_kernel(x_ref, g_ref, p_ref, w_ref, o_ref, *, r):
    h = _rms_bf16(x_ref[0], g_ref[...])
    nblk = h.shape[0] // PERM_BLK
    hp = jnp.concatenate(
        [jnp.dot(p_ref[...], h[k * PERM_BLK:(k + 1) * PERM_BLK], preferred_element_type=F32)
         for k in range(nblk)], axis=0).astype(BF16)
    res = jnp.dot(hp, w_ref[...], preferred_element_type=F32).astype(o_ref.dtype)
    n = PERM_BLK // r
    for k in range(nblk):
        for s in range(r):
            o_ref[0, s, k * n:(k + 1) * n, :] = res[k * PERM_BLK + s * n:k * PERM_BLK + (s + 1) * n, :]


def _inproj_perm(x3, g, w, r):
    B, S, D = x3.shape
    N = w.shape[1]
    tm = min(TM_PERM, S)
    n = PERM_BLK // r
    o = jnp.arange(PERM_BLK, dtype=jnp.int32)
    src = (o % n) * r + o // n
    perm = (src[:, None] == jnp.arange(PERM_BLK, dtype=jnp.int32)[None, :]).astype(BF16)
    kern = functools.partial(_inproj_perm_kernel, r=r)
    return pl.pallas_call(
        kern,
        out_shape=jax.ShapeDtypeStruct((B, r, S // r, N), BF16),
        grid=(B, S // tm),
        in_specs=[pl.BlockSpec((1, tm, D), lambda b, i: (b, i, 0)),
                  pl.BlockSpec((1, D), lambda b, i: (0, 0)),
                  pl.BlockSpec((PERM_BLK, PERM_BLK), lambda b, i: (0, 0)),
                  pl.BlockSpec((D, N), lambda b, i: (0, 0))],
        out_specs=pl.BlockSpec((1, r, tm // r, N), lambda b, i: (b, 0, i, 0)),
        compiler_params=_cparams(("parallel", "parallel")),
        name=f"inproj_perm_{r}",
    )(x3, g, perm, w)


def _attn_a_kernel(lam_ref, cfar_ref, q_ref, k_ref, vt_ref, bias_ref, g_ref, o_ref, *, out_scale):
    h, qi = pl.program_id(1), pl.program_id(2)
    t = T_ATT
    nk = k_ref.shape[1] // t
    q = q_ref[0]
    low_half = lax.broadcasted_iota(jnp.int32, (1, LANES), 1) < HALF_LANES
    zero = jnp.zeros_like(q)
    qs = jnp.concatenate([jnp.where(low_half, q, zero), jnp.where(low_half, zero, q)], axis=0)

    blocks, sts, shifts = [], [], []
    m = None
    for d in range(-1, nk - 1):
        a = lax.rem(qi + (d + nk), nk)
        delta = a - qi
        kb = k_ref[0, pl.ds(pl.multiple_of(a * t, t), t), :]
        st = lax.dot_general(kb, qs, (((1,), (1,)), ((), ())), preferred_element_type=F32)
        if d <= 1:
            tile = bias_ref[0, jnp.clip(delta, -2, 2) + 2]
            st = st + jnp.concatenate([tile, tile], axis=1)
            shift = None
            cm = jnp.max(st, axis=0, keepdims=True)
        else:
            shift = cfar_ref[2 * h + (delta > 0).astype(jnp.int32)]
            cm = jnp.max(st, axis=0, keepdims=True) + shift
        m = cm if m is None else jnp.maximum(m, cm)
        blocks.append(a)
        sts.append(st)
        shifts.append(shift)

    l = jnp.zeros_like(m)
    accs = [jnp.zeros((LANES, t), F32), jnp.zeros((LANES, t), F32)]
    for a, st, shift in zip(blocks, sts, shifts):
        p = jnp.exp2(st - (m if shift is None else m - shift))
        l = l + jnp.sum(p, axis=0, keepdims=True)
        pb = p.astype(BF16)
        vt = vt_ref[0, 0, a]
        for c in range(2):
            accs[c] = accs[c] + jnp.dot(vt, pb[:, c * t:(c + 1) * t], preferred_element_type=F32)

    ot = accs[0] / l[:, :t] - lam_ref[0] * (accs[1] / l[:, t:])
    o = ot.T
    ms = jnp.mean(o * o, axis=-1, keepdims=True)
    o_ref[0] = (o * lax.rsqrt(ms + EPS) * g_ref[...] * out_scale).astype(o_ref.dtype)


def _attn_a(proj3, vt, lam, cfar, bias5, subln_g, lam_init):
    B, S, _ = proj3.shape
    t = T_ATT
    nk = S // t
    kern = functools.partial(_attn_a_kernel, out_scale=1.0 - lam_init)
    return pl.pallas_call(
        kern,
        out_shape=jax.ShapeDtypeStruct((B, S, HA * 2 * DA), BF16),
        grid=(B, HA, nk),
        in_specs=[
            pl.BlockSpec(memory_space=pltpu.SMEM),
            pl.BlockSpec(memory_space=pltpu.SMEM),
            pl.BlockSpec((1, t, LANES), lambda b, h, qi: (b, qi, h)),
            pl.BlockSpec((1, S, LANES), lambda b, h, qi: (b, 0, HA + h)),
            pl.BlockSpec((1, 1, nk, LANES, t), lambda b, h, qi: (b, h, 0, 0, 0)),
            pl.BlockSpec((1, 5, t, t), lambda b, h, qi: (h, 0, 0, 0)),
            pl.BlockSpec((1, LANES), lambda b, h, qi: (0, 0)),
        ],
        out_specs=pl.BlockSpec((1, t, LANES), lambda b, h, qi: (b, qi, h)),
        compiler_params=_cparams(("parallel", "parallel", "arbitrary")),
        name="diff_attn",
    )(lam, cfar, proj3, proj3, vt, bias5, subln_g)


def _attn_a_bias(rel_bias):
    t = T_ATT
    tab = rel_bias[:, :HA].astype(F32) * LOG2E
    d = jnp.arange(-1, 2, dtype=jnp.int32)[:, None, None] * t
    rel = d + jnp.arange(t, dtype=jnp.int32)[None, :, None] - jnp.arange(t, dtype=jnp.int32)[None, None, :]
    near = _bias_lookup(_t5_bucket(rel), tab)
    far = tab[_t5_bucket(jnp.array([-(t + 1), t + 1], dtype=jnp.int32))].T
    fill = lambda side: jnp.broadcast_to(far[:, side, None, None, None], (HA, 1, t, t))
    tiles = jnp.concatenate([fill(0), near, fill(1)], axis=1)
    return tiles, far.reshape(2 * HA)


def _attn_b_kernel(q_ref, k_ref, v_ref, bias_ref, o_ref, lse_ref, *, sub_len, r, sp, qp):
    nblk = sub_len // QB_DIL
    low_half = lax.broadcasted_iota(jnp.int32, (1, LANES), 1) < HALF_LANES
    for si in range(sp):
        s = si if sp == r else pl.program_id(2) * sp + si
        for qb in range(qp):
            i = pl.program_id(1) * qp + qb
            start = jnp.clip(i * QB_DIL - HALF_WIN, 0, sub_len - KW_DIL)
            start = pl.multiple_of(start, HALF_WIN)
            variant = jnp.where(i == 0, 0, jnp.where(i == nblk - 1, 2, 1))
            q = q_ref[0, si, qb * QB_DIL:(qb + 1) * QB_DIL, :]
            kw = k_ref[0, s, pl.ds(start, KW_DIL), :]
            vw = v_ref[0, s, pl.ds(start, KW_DIL), :]
            rows = (slice(qb * QB_DIL, (qb + 1) * QB_DIL) if r == 1
                    else pl.ds(s, QB_DIL, stride=r))
            for j in range(HB // 2):
                cols = slice(j * LANES, (j + 1) * LANES)
                qpair, kp, vp = q[:, cols], kw[:, cols], vw[:, cols]
                outs, lses = [], []
                for c in range(2):
                    qc = jnp.where(low_half if c == 0 else jnp.logical_not(low_half), qpair,
                                   jnp.zeros_like(qpair))
                    sc = lax.dot_general(qc, kp, (((1,), (1,)), ((), ())),
                                         preferred_element_type=F32)
                    sc = sc + bias_ref[2 * j + c, variant]
                    m = jnp.max(sc, axis=-1, keepdims=True)
                    p = jnp.exp(sc - m)
                    l = jnp.sum(p, axis=-1, keepdims=True)
                    outs.append(jnp.dot(p.astype(BF16), vp, preferred_element_type=F32) / l)
                    lses.append(m + jnp.log(l))
                o_ref[0, j, rows, :] = jnp.where(low_half, outs[0], outs[1])
                lse_ref[0, j, rows, :] = jnp.where(low_half, lses[0], lses[1])


def _attn_b(qkv4, bias3, g, cols):
    B, r, L, _ = qkv4.shape
    S = r * L
    width = HB * DB
    nblk = L // QB_DIL
    sp = min(r, ITEMS_DIL)
    qp = ITEMS_DIL // sp
    qcol, kcol, vcol = cols
    kern = functools.partial(_attn_b_kernel, sub_len=L, r=r, sp=sp, qp=qp)
    slab = jax.ShapeDtypeStruct((B, N_SLABS, S, LANES), F32)
    slab_spec = pl.BlockSpec((1, N_SLABS, QB_DIL * r * qp, LANES), lambda b, i, s: (b, 0, i, 0))
    return pl.pallas_call(
        kern,
        out_shape=[slab, slab],
        grid=(B, nblk // qp, r // sp),
        in_specs=[
            pl.BlockSpec((1, sp, QB_DIL * qp, width), lambda b, i, s: (b, s, i, qcol)),
            pl.BlockSpec((1, r, L, width), lambda b, i, s: (b, 0, 0, kcol)),
            pl.BlockSpec((1, r, L, width), lambda b, i, s: (b, 0, 0, vcol)),
            pl.BlockSpec((HB, 3, QB_DIL, KW_DIL), lambda b, i, s: (0, 0, 0, 0)),
        ],
        out_specs=[slab_spec, slab_spec],
        compiler_params=_cparams(("parallel", "arbitrary", "arbitrary")),
        name=f"dilated_attn_{g}",
    )(qkv4, qkv4, qkv4, bias3)


def _attn_b_bias(rel_bias, g):
    r = DILATIONS[g]
    tab = rel_bias[:, HA + g * HB: HA + (g + 1) * HB].astype(F32)
    off = jnp.arange(3, dtype=jnp.int32)[:, None, None] * HALF_WIN
    rel = (jnp.arange(KW_DIL, dtype=jnp.int32)[None, None, :] - off
           - jnp.arange(QB_DIL, dtype=jnp.int32)[None, :, None])
    bias = _bias_lookup(_t5_bucket(rel * r), tab)
    return jnp.where((jnp.abs(rel) <= HALF_WIN)[None], bias, NEG)


def _sgu_kernel(zu_ref, zv_ref, lng_ref, lnb_ref, ws_ref, bs_ref, o_ref):
    u = jax.nn.gelu(zu_ref[...].astype(F32))
    v = jax.nn.gelu(zv_ref[...].astype(F32))
    mu = jnp.mean(v, axis=-1, keepdims=True)
    var = jnp.mean(jnp.square(v - mu), axis=-1, keepdims=True)
    v = ((v - mu) * lax.rsqrt(var + EPS) * lng_ref[...] + lnb_ref[...]).astype(BF16)
    gd = v.shape[1] // C_GROUPS
    for n in range(v.shape[0] // CHUNK):
        rows = slice(n * CHUNK, (n + 1) * CHUNK)
        for g in range(C_GROUPS):
            cols = slice(g * gd, (g + 1) * gd)
            mixed = jnp.dot(ws_ref[g], v[rows, cols], preferred_element_type=F32) + bs_ref[:, cols]
            o_ref[rows, cols] = (u[rows, cols] * mixed).astype(o_ref.dtype)


def _sgu(proj2, ln_g, ln_b, w_s, b_exp):
    T = proj2.shape[0]
    tm = min(TM_SGU, T)
    w = MIX_W
    return pl.pallas_call(
        _sgu_kernel,
        out_shape=jax.ShapeDtypeStruct((T, w), BF16),
        grid=(T // tm,),
        in_specs=[pl.BlockSpec((tm, w), lambda i: (i, COL_ZU)),
                  pl.BlockSpec((tm, w), lambda i: (i, COL_ZU + 1)),
                  pl.BlockSpec((1, w), lambda i: (0, 0)),
                  pl.BlockSpec((1, w), lambda i: (0, 0)),
                  pl.BlockSpec((C_GROUPS, CHUNK, CHUNK), lambda i: (0, 0, 0)),
                  pl.BlockSpec((CHUNK, w), lambda i: (0, 0))],
        out_specs=pl.BlockSpec((tm, w), lambda i: (i, 0)),
        compiler_params=_cparams(("parallel",)),
        name="sgu",
    )(proj2, proj2, ln_g, ln_b, w_s, b_exp)


def _route(logits):
    lane = lax.broadcasted_iota(jnp.int32, logits.shape, 1)
    big = jnp.int32(LANES)
    is_grp = (lane >= N_EXPERTS) & (lane < N_EXPERTS + N_GROUPS)
    gl = jnp.where(is_grp, logits, NEG)
    gmax = jnp.max(gl, axis=-1, keepdims=True)
    g_idx = jnp.min(jnp.where(is_grp & (gl == gmax), lane, big), axis=-1, keepdims=True) - N_EXPERTS
    g_w = 1.0 / jnp.sum(jnp.where(is_grp, jnp.exp(gl - gmax), 0.0), axis=-1, keepdims=True)
    in_grp = (lane >= g_idx * E_PER_GROUP) & (lane < (g_idx + 1) * E_PER_GROUP)
    sel = jnp.where(in_grp, logits, NEG)
    v1 = jnp.max(sel, axis=-1, keepdims=True)
    i1 = jnp.min(jnp.where(in_grp & (sel == v1), lane, big), axis=-1, keepdims=True)
    rest = in_grp & (lane != i1)
    sel2 = jnp.where(rest, logits, NEG)
    v2 = jnp.max(sel2, axis=-1, keepdims=True)
    i2 = jnp.min(jnp.where(rest & (sel2 == v2), lane, big), axis=-1, keepdims=True)
    e2 = jnp.exp(v2 - v1)
    w1 = g_w / (1.0 + e2)
    w2 = g_w * e2 / (1.0 + e2)
    return jnp.where(lane == i1, w1, jnp.where(lane == i2, w2, 0.0))


def _mix_kernel(x_ref, ya_ref, ob0_ref, ob1_ref, ob2_ref, ls0_ref, ls1_ref, ls2_ref, yc_ref,
                g0_ref, g1_ref, g2_ref, wb_ref, wo_ref, nf_ref, wr_ref, br_ref,
                xo_ref, h_ref, comb_ref):
    slabs = []
    for j in range(N_SLABS):
        ls0, ls1, ls2 = ls0_ref[0, j], ls1_ref[0, j], ls2_ref[0, j]
        mx = jnp.maximum(jnp.maximum(ls0, ls1), ls2)
        e0, e1, e2 = jnp.exp(ls0 - mx), jnp.exp(ls1 - mx), jnp.exp(ls2 - mx)
        yb = (e0 * ob0_ref[0, j] + e1 * ob1_ref[0, j] + e2 * ob2_ref[0, j]) / (e0 + e1 + e2)
        slabs.append(yb.astype(BF16))
    yb = jnp.concatenate(slabs, axis=-1)
    merged = jax.nn.sigmoid(g0_ref[...].astype(F32)) * jnp.dot(ya_ref[...], wb_ref[0],
                                                               preferred_element_type=F32)
    merged += jax.nn.sigmoid(g1_ref[...].astype(F32)) * jnp.dot(yb, wb_ref[1],
                                                                preferred_element_type=F32)
    merged += jax.nn.sigmoid(g2_ref[...].astype(F32)) * jnp.dot(yc_ref[...], wb_ref[2],
                                                                preferred_element_type=F32)
    xn = x_ref[...] + jnp.dot(merged.astype(BF16), wo_ref[...], preferred_element_type=F32)
    xo_ref[...] = xn
    h = _rms_bf16(xn, nf_ref[...])
    h_ref[...] = h
    logits = jnp.dot(h, wr_ref[...], preferred_element_type=F32) + br_ref[...]
    comb_ref[...] = _route(logits)


def _mix(x2, ya, obs, lses, yc, proj2, wb, wo, nf, wr, br):
    T, D = x2.shape
    S = obs[0].shape[2]
    tm = min(TM_MIX, S)
    per_b = S // tm
    w = MIX_W
    row = lambda width: pl.BlockSpec((tm, width), lambda i: (i, 0))
    full = lambda a: pl.BlockSpec(a.shape, lambda i: (0,) * a.ndim)
    gate = lambda n: pl.BlockSpec((tm, D), lambda i: (i, COL_GATE + n))
    slab = pl.BlockSpec((1, N_SLABS, tm, LANES), lambda i: (i // per_b, 0, i % per_b, 0))
    return pl.pallas_call(
        _mix_kernel,
        out_shape=[jax.ShapeDtypeStruct((T, D), F32), jax.ShapeDtypeStruct((T, D), BF16),
                   jax.ShapeDtypeStruct((T, LANES), F32)],
        grid=(T // tm,),
        in_specs=[row(D), row(w), slab, slab, slab, slab, slab, slab, row(w),
                  gate(0), gate(1), gate(2), full(wb), full(wo), full(nf), full(wr), full(br)],
        out_specs=[row(D), row(D), row(LANES)],
        compiler_params=_cparams(("parallel",)),
        name="mix",
    )(x2, ya, obs[0], obs[1], obs[2], lses[0], lses[1], lses[2], yc, proj2, proj2, proj2,
      wb, wo, nf, wr, br)


def _moe_kernel(h_ref, comb_ref, x_ref, wg_ref, wu_ref, wd_ref, nfin_ref, o_ref, acc_scr,
                *, final_norm):
    e = pl.program_id(1)

    @pl.when(e == 0)
    def _():
        acc_scr[...] = jnp.zeros(acc_scr.shape, F32)

    h = h_ref[...]
    lane = lax.broadcasted_iota(jnp.int32, comb_ref.shape, 1)
    c = jnp.sum(jnp.where(lane == e, comb_ref[...], 0.0), axis=-1, keepdims=True)
    hid = (jax.nn.silu(jnp.dot(h, wg_ref[0].astype(BF16), preferred_element_type=F32))
           * jnp.dot(h, wu_ref[0].astype(BF16), preferred_element_type=F32))
    acc_scr[...] += c * jnp.dot(hid.astype(BF16), wd_ref[0].astype(BF16),
                                preferred_element_type=F32)

    @pl.when(e == pl.num_programs(1) - 1)
    def _():
        xn = x_ref[...] + acc_scr[...]
        if final_norm:
            ms = jnp.mean(xn * xn, axis=-1, keepdims=True)
            xn = xn * lax.rsqrt(ms + EPS) * nfin_ref[...]
        o_ref[...] = xn


def _moe_dense(h, comb, x2, wg, wu, wd, nfin, final_norm):
    T, D = x2.shape
    tm = min(TM_MOE, T)
    F = wg.shape[2]
    kern = functools.partial(_moe_kernel, final_norm=final_norm)
    return pl.pallas_call(
        kern,
        out_shape=jax.ShapeDtypeStruct((T, D), F32),
        grid=(T // tm, N_EXPERTS),
        in_specs=[pl.BlockSpec((tm, D), lambda i, e: (i, 0)),
                  pl.BlockSpec((tm, LANES), lambda i, e: (i, 0)),
                  pl.BlockSpec((tm, D), lambda i, e: (i, 0)),
                  pl.BlockSpec((1, D, F), lambda i, e: (e, 0, 0)),
                  pl.BlockSpec((1, D, F), lambda i, e: (e, 0, 0)),
                  pl.BlockSpec((1, F, D), lambda i, e: (e, 0, 0)),
                  pl.BlockSpec((1, D), lambda i, e: (0, 0))],
        out_specs=pl.BlockSpec((tm, D), lambda i, e: (i, 0)),
        scratch_shapes=[pltpu.VMEM((tm, D), F32)],
        compiler_params=_cparams(("parallel", "arbitrary")),
        name="moe_dense",
    )(h, comb, x2, wg, wu, wd, nfin)


def _moe_dispatch_kernel(h_ref, comb_ref, o_ref, cnt_ref):
    tm = h_ref.shape[0]
    comb = comb_ref[...]
    hi = comb.astype(BF16)
    lo = (comb - hi.astype(F32)).astype(BF16)
    haug = jnp.concatenate([h_ref[...], hi, lo], axis=1)
    a_t = comb.T[:N_EXPERTS] > 0.0
    a_f = jnp.where(a_t, 1.0, 0.0)
    before = (lax.broadcasted_iota(jnp.int32, (tm, tm), 0)
              < lax.broadcasted_iota(jnp.int32, (tm, tm), 1))
    rank_t = jnp.dot(a_f.astype(BF16), jnp.where(before, 1.0, 0.0).astype(BF16),
                     preferred_element_type=F32)
    slot = lax.broadcasted_iota(jnp.int32, (MOE_CAP, tm), 0).astype(F32)
    blocks = [jnp.where((slot == rank_t[e:e + 1]) & a_t[e:e + 1], 1.0, 0.0).astype(BF16)
              for e in range(N_EXPERTS)]
    res = jnp.dot(jnp.concatenate(blocks, axis=0), haug, preferred_element_type=F32)
    res = res.astype(o_ref.dtype)
    for e in range(N_EXPERTS):
        o_ref[0, e] = res[e * MOE_CAP:(e + 1) * MOE_CAP]
    cnt_ref[0] = jnp.broadcast_to(jnp.sum(a_f, axis=1, keepdims=True), cnt_ref.shape[1:])


def _moe_dispatch(h, comb):
    T, D = h.shape
    tm = min(TM_DISP, T)
    n = T // tm
    return pl.pallas_call(
        _moe_dispatch_kernel,
        out_shape=[jax.ShapeDtypeStruct((n, N_EXPERTS, MOE_CAP, D + 2 * LANES), BF16),
                   jax.ShapeDtypeStruct((n, N_EXPERTS, LANES), F32)],
        grid=(n,),
        in_specs=[pl.BlockSpec((tm, D), lambda i: (i, 0)),
                  pl.BlockSpec((tm, LANES), lambda i: (i, 0))],
        out_specs=[pl.BlockSpec((1, N_EXPERTS, MOE_CAP, D + 2 * LANES), lambda i: (i, 0, 0, 0)),
                   pl.BlockSpec((1, N_EXPERTS, LANES), lambda i: (i, 0, 0))],
        compiler_params=_cparams(("parallel",)),
        name="moe_dispatch",
    )(h, comb)


def _moe_ffn_kernel(n16_ref, s_ref, wg_ref, wu_ref, wd_ref, o_ref,
                    wg_scr, wu_scr, wd_scr, lhs_scr, y_scr):
    e, c = pl.program_id(0), pl.program_id(1)

    @pl.when(c == 0)
    def _():
        wg_scr[...] = wg_ref[0].astype(BF16)
        wu_scr[...] = wu_ref[0].astype(BF16)
        wd_scr[...] = wd_ref[0].astype(BF16)

    g, _, cap, _ = s_ref.shape
    D = o_ref.shape[-1]
    lhs_scr[...] = jnp.zeros(lhs_scr.shape, lhs_scr.dtype)
    offs = []
    off = jnp.int32(0)
    for t in range(g):
        offs.append(off)
        lhs_scr[pl.ds(pl.multiple_of(off, BF16_ROWS), cap), :] = s_ref[t, 0]
        off = off + n16_ref[(c * g + t) * N_EXPERTS + e]
    total = off

    def run(nrows):
        rows = lhs_scr[:nrows]
        h = rows[:, :D]
        wparts = rows[:, D:].astype(F32)
        lane = lax.broadcasted_iota(jnp.int32, wparts.shape, 1)
        w = jnp.sum(jnp.where(lane % LANES == e, wparts, 0.0), axis=-1, keepdims=True)
        hid = (jax.nn.silu(jnp.dot(h, wg_scr[...], preferred_element_type=F32))
               * jnp.dot(h, wu_scr[...], preferred_element_type=F32))
        y = w * jnp.dot(hid.astype(BF16), wd_scr[...], preferred_element_type=F32)
        y_scr[:nrows] = y.astype(y_scr.dtype)
        if nrows < y_scr.shape[0]:
            y_scr[nrows:] = jnp.zeros((y_scr.shape[0] - nrows, D), y_scr.dtype)

    classes = tuple(range(g * cap // 2, g * cap + 1, FFN_ROW_STEP))
    lower = 0
    for nrows in classes:
        pl.when((total > lower) & (total <= nrows))(functools.partial(run, nrows))
        lower = nrows

    @pl.when(total == 0)
    def _():
        y_scr[...] = jnp.zeros(y_scr.shape, y_scr.dtype)

    for t in range(g):
        o_ref[t, 0] = y_scr[pl.ds(pl.multiple_of(offs[t], BF16_ROWS), cap), :]


def _moe_ffn(srt, n16, wg, wu, wd):
    n, ne, cap, wdt = srt.shape
    D, F = wg.shape[1], wg.shape[2]
    g = min(G_FFN, n)
    return pl.pallas_call(
        _moe_ffn_kernel,
        out_shape=jax.ShapeDtypeStruct((n, ne, cap, D), BF16),
        grid_spec=pltpu.PrefetchScalarGridSpec(
            num_scalar_prefetch=1,
            grid=(ne, n // g),
            in_specs=[pl.BlockSpec((g, 1, cap, wdt), lambda e, c, n16: (c, e, 0, 0)),
                      pl.BlockSpec((1, D, F), lambda e, c, n16: (e, 0, 0)),
                      pl.BlockSpec((1, D, F), lambda e, c, n16: (e, 0, 0)),
                      pl.BlockSpec((1, F, D), lambda e, c, n16: (e, 0, 0))],
            out_specs=pl.BlockSpec((g, 1, cap, D), lambda e, c, n16: (c, e, 0, 0)),
            scratch_shapes=[pltpu.VMEM((D, F), BF16), pltpu.VMEM((D, F), BF16),
                            pltpu.VMEM((F, D), BF16), pltpu.VMEM((g * cap, wdt), BF16),
                            pltpu.VMEM((g * cap, D), BF16)]),
        compiler_params=_cparams(("parallel", "arbitrary")),
        name="moe_ffn",
    )(n16, srt, wg, wu, wd)


def _moe_combine_kernel(y_ref, comb_ref, x_ref, nfin_ref, o_ref, *, final_norm):
    tm = x_ref.shape[0]
    comb = comb_ref[...]
    a = comb > 0.0
    before = (lax.broadcasted_iota(jnp.int32, (tm, tm), 1)
              < lax.broadcasted_iota(jnp.int32, (tm, tm), 0))
    rank = jnp.dot(jnp.where(before, 1.0, 0.0).astype(BF16), jnp.where(a, 1.0, 0.0).astype(BF16),
                   preferred_element_type=F32)
    key = jnp.where(a, rank, -1.0).astype(BF16)
    ncol = N_EXPERTS * MOE_CAP
    spread = (lax.broadcasted_iota(jnp.int32, (LANES, ncol), 1) // MOE_CAP
              == lax.broadcasted_iota(jnp.int32, (LANES, ncol), 0))
    key_all = jnp.dot(key, jnp.where(spread, 1.0, 0.0).astype(BF16), preferred_element_type=F32)
    slot = (lax.broadcasted_iota(jnp.int32, (tm, ncol), 1) % MOE_CAP).astype(F32)
    pc = jnp.where(slot == key_all, 1.0, 0.0).astype(BF16)
    y = jnp.concatenate([y_ref[0, e] for e in range(N_EXPERTS)], axis=0)
    xn = x_ref[...] + jnp.dot(pc, y, preferred_element_type=F32)
    if final_norm:
        ms = jnp.mean(xn * xn, axis=-1, keepdims=True)
        xn = xn * lax.rsqrt(ms + EPS) * nfin_ref[...]
    o_ref[...] = xn


def _moe_combine(y, comb, x2, nfin, final_norm):
    T, D = x2.shape
    n, ne, cap, _ = y.shape
    tm = T // n
    kern = functools.partial(_moe_combine_kernel, final_norm=final_norm)
    return pl.pallas_call(
        kern,
        out_shape=jax.ShapeDtypeStruct((T, D), F32),
        grid=(n,),
        in_specs=[pl.BlockSpec((1, ne, cap, D), lambda i: (i, 0, 0, 0)),
                  pl.BlockSpec((tm, LANES), lambda i: (i, 0)),
                  pl.BlockSpec((tm, D), lambda i: (i, 0)),
                  pl.BlockSpec((1, D), lambda i: (0, 0))],
        out_specs=pl.BlockSpec((tm, D), lambda i: (i, 0)),
        compiler_params=_cparams(("parallel",)),
        name="moe_combine",
    )(y, comb, x2, nfin)


def _moe_fix_kernel(tiles_ref, experts_ref, first_ref, n_ref, h_ref, comb_ref, prev_ref,
                    wg_ref, wu_ref, wd_ref, o_ref):
    del tiles_ref
    s = pl.program_id(0)

    @pl.when(s < n_ref[0])
    def _():
        e = experts_ref[s]
        tm = h_ref.shape[0]
        comb = comb_ref[...]
        a = jnp.where(comb > 0.0, 1.0, 0.0)
        before = (lax.broadcasted_iota(jnp.int32, (tm, tm), 1)
                  < lax.broadcasted_iota(jnp.int32, (tm, tm), 0))
        rank = jnp.dot(jnp.where(before, 1.0, 0.0).astype(BF16), a.astype(BF16),
                       preferred_element_type=F32)
        lane = lax.broadcasted_iota(jnp.int32, comb.shape, 1)
        dropped = (lane == e) & (rank >= MOE_CAP)
        c = jnp.sum(jnp.where(dropped, comb, 0.0), axis=-1, keepdims=True)
        h = h_ref[...]
        hid = (jax.nn.silu(jnp.dot(h, wg_ref[0].astype(BF16), preferred_element_type=F32))
               * jnp.dot(h, wu_ref[0].astype(BF16), preferred_element_type=F32))
        add = c * jnp.dot(hid.astype(BF16), wd_ref[0].astype(BF16), preferred_element_type=F32)
        fresh = first_ref[s] == 1

        @pl.when(fresh)
        def _():
            o_ref[...] = prev_ref[...] + add

        @pl.when(jnp.logical_not(fresh))
        def _():
            o_ref[...] += add


def _moe_fix(tiles, experts, first, n, out, h, comb, wg, wu, wd):
    T, D = out.shape
    tm = min(TM_DISP, T)
    F = wg.shape[2]
    tile = lambda width: pl.BlockSpec((tm, width), lambda s, tl, ex, fi, n: (tl[s], 0))
    wspec = lambda shape: pl.BlockSpec(shape, lambda s, tl, ex, fi, n: (ex[s], 0, 0))
    return pl.pallas_call(
        _moe_fix_kernel,
        out_shape=jax.ShapeDtypeStruct((T, D), F32),
        grid_spec=pltpu.PrefetchScalarGridSpec(
            num_scalar_prefetch=4,
            grid=(MAX_OVF,),
            in_specs=[tile(D), tile(LANES), tile(D), wspec((1, D, F)), wspec((1, D, F)),
                      wspec((1, F, D))],
            out_specs=tile(D)),
        input_output_aliases={6: 0},
        compiler_params=_cparams(("arbitrary",)),
        name="moe_fix",
    )(tiles, experts, first, n, h, comb, out, wg, wu, wd)


def _final_norm_kernel(x_ref, g_ref, o_ref):
    x = x_ref[...]
    ms = jnp.mean(x * x, axis=-1, keepdims=True)
    o_ref[...] = x * lax.rsqrt(ms + EPS) * g_ref[...]


def _final_norm(x2, g):
    T, D = x2.shape
    tm = min(TM_PROJ, T)
    return pl.pallas_call(
        _final_norm_kernel,
        out_shape=jax.ShapeDtypeStruct((T, D), F32),
        grid=(T // tm,),
        in_specs=[pl.BlockSpec((tm, D), lambda i: (i, 0)), pl.BlockSpec((1, D), lambda i: (0, 0))],
        out_specs=pl.BlockSpec((tm, D), lambda i: (i, 0)),
        compiler_params=_cparams(("parallel",)),
        name="final_norm",
    )(x2, g)


def _moe(h, comb, x2, wg, wu, wd, nfin, final_norm):
    srt, cnt = _moe_dispatch(h, comb)
    over = (cnt[:, :, 0] > MOE_CAP).reshape(-1)
    n_ovf = jnp.sum(over.astype(jnp.int32))
    pairs = jnp.nonzero(over, size=MAX_OVF, fill_value=0)[0].astype(jnp.int32)
    pairs = jnp.where(jnp.arange(MAX_OVF) < n_ovf, pairs, pairs[jnp.clip(n_ovf - 1, 0, MAX_OVF - 1)])
    tiles, experts = pairs // N_EXPERTS, pairs % N_EXPERTS
    first = jnp.concatenate([jnp.ones((1,), jnp.int32),
                             (tiles[1:] != tiles[:-1]).astype(jnp.int32)])

    used = jnp.minimum(cnt[:, :, 0], MOE_CAP).astype(jnp.int32).reshape(-1)
    n16 = (used + (BF16_ROWS - 1)) // BF16_ROWS * BF16_ROWS

    def routed():
        y = _moe_ffn(srt, n16, wg, wu, wd)

        def fixed():
            out = _moe_combine(y, comb, x2, nfin, False)
            out = _moe_fix(tiles, experts, first, n_ovf.reshape(1), out, h, comb, wg, wu, wd)
            return _final_norm(out, nfin) if final_norm else out

        return lax.cond(n_ovf > 0, fixed, lambda: _moe_combine(y, comb, x2, nfin, final_norm))

    return lax.cond(n_ovf > MAX_OVF,
                    lambda: _moe_dense(h, comb, x2, wg, wu, wd, nfin, final_norm), routed)


def kernel(x, rel_bias, norm_mix, w_in, diff_lambda, diff_subln, sgu_ln_g, sgu_ln_b, sgu_w, sgu_b,
           w_branch, w_out, norm_ffn, w_router_grp, b_router_grp, w_router_exp, b_router_exp,
           w_gate, w_up, w_down, norm_final):
    B, S, D = x.shape
    T = B * S
    depth = w_in.shape[0]
    a_out = HA * 2 * DA
    grp_w = HB * DB
    b_cols = 3 * NG_B * grp_w
    qkv_b0 = 3 * a_out
    zc0 = qkv_b0 + b_cols
    gate0 = zc0 + 2 * MIX_W
    qk_scale = DA ** -0.5

    bias_a, cfar = _attn_a_bias(rel_bias)
    bias_b = [_attn_b_bias(rel_bias, g) for g in range(NG_B)]

    col = jnp.arange(w_in.shape[2])
    col_scale = jnp.where(col < a_out, qk_scale * LOG2E,
                          jnp.where((col >= qkv_b0) & (col < qkv_b0 + NG_B * grp_w), qk_scale, 1.0))

    def group_cols(w, g):
        return [w[:, qkv_b0 + (c * NG_B + g) * grp_w: qkv_b0 + (c * NG_B + g + 1) * grp_w]
                for c in range(3)]

    x2 = x.reshape(T, D)
    for i in range(depth):
        w = (w_in[i] * col_scale.astype(F32)).astype(BF16)
        nm = norm_mix[i][None, :]
        w_main = jnp.concatenate([w[:, :2 * a_out], w[:, zc0:]] + group_cols(w, 0), axis=1)
        proj2 = _inproj(x2, nm, w_main)
        x3 = x2.reshape(B, S, D)
        vt = _inproj_t(x3, nm, w[:, 2 * a_out:3 * a_out].T)
        proj3 = proj2.reshape(B, S, proj2.shape[1])

        lam_init = 0.8 - 0.6 * math.exp(-0.3 * i)
        lp = diff_lambda[i].astype(F32)
        lam = jnp.exp(jnp.sum(lp[0] * lp[1])) - jnp.exp(jnp.sum(lp[2] * lp[3])) + lam_init
        ya = _attn_a(proj3, vt, lam.reshape(1), cfar, bias_a, diff_subln[i][None, :], lam_init)

        obs, lses = [], []
        for g in range(NG_B):
            r = DILATIONS[g]
            if r == 1:
                qkv4, cols = proj3[:, None], (COL_QKV0, COL_QKV0 + 1, COL_QKV0 + 2)
            else:
                w_g = jnp.concatenate(group_cols(w, g), axis=1)
                qkv4, cols = _inproj_perm(x3, nm, w_g, r), (0, 1, 2)
            o, l = _attn_b(qkv4, bias_b[g], g, cols)
            obs.append(o)
            lses.append(l)

        b_exp = jnp.repeat(sgu_b[i].T, MIX_W // C_GROUPS, axis=1)
        yc = _sgu(proj2, sgu_ln_g[i][None, :], sgu_ln_b[i][None, :], sgu_w[i].astype(BF16), b_exp)

        wr = jnp.concatenate([w_router_exp[i].transpose(1, 0, 2).reshape(D, N_EXPERTS),
                              w_router_grp[i]], axis=1)
        wr = jnp.pad(wr, ((0, 0), (0, LANES - wr.shape[1]))).astype(BF16)
        br = jnp.concatenate([b_router_exp[i].reshape(N_EXPERTS), b_router_grp[i]])
        br = jnp.pad(br, (0, LANES - br.shape[0]))[None, :].astype(F32)

        x2, h, comb = _mix(x2, ya.reshape(T, a_out), obs, lses, yc, proj2,
                           w_branch[i].astype(BF16), w_out[i].astype(BF16), norm_ffn[i][None, :],
                           wr, br)
        x2 = _moe(h, comb, x2, w_gate[i], w_up[i], w_down[i], norm_final[None, :], i == depth - 1)
    return x2.reshape(B, S, D)
```

```python
import functools
import math

import jax
import jax.numpy as jnp
from jax import lax
from jax.experimental import pallas as pl
from jax.experimental.pallas import tpu as pltpu

F32 = jnp.float32
BF16 = jnp.bfloat16

EPS = 1e-6
NEG = -1e30
LOG2E = 1.4426950408889634
LANES = 128
HALF_LANES = LANES // 2
VMEM_LIMIT = 48 * 1024 * 1024

HA = 4
DA = 64
MIX_W = 512
WINDOWS = (128, 512, 2048)
DILATIONS = (1, 4, 16)
NG_B = 3
HB = 8
DB = 64
HALF_WIN = 64
CHUNK = 128
C_GROUPS = 4
N_BRANCH = 3
N_BUCKETS = 32
MAX_DIST = 128
N_GROUPS = 4
E_PER_GROUP = 4
N_EXPERTS = N_GROUPS * E_PER_GROUP
N_SLABS = MIX_W // LANES

TM_PROJ = 1024
TN_PROJ = 3328
TM_PERM = 512
PERM_BLK = 256
T_ATT = 512
QB_DIL = 128
KW_DIL = QB_DIL + 2 * HALF_WIN
ITEMS_DIL = 4
TM_SGU = 512
TM_MIX = 512
TM_MOE = 1024
TM_DISP = 256
MOE_CAP = HALF_LANES
G_FFN = 8
FFN_ROW_STEP = 64
BF16_ROWS = 16
MAX_OVF = 64

COL_ZU = 2
COL_GATE = 2
COL_QKV0 = 10


def _cparams(sem):
    return pltpu.CompilerParams(dimension_semantics=sem, vmem_limit_bytes=VMEM_LIMIT)


def _t5_bucket(rel):
    nb = N_BUCKETS // 2
    max_exact = nb // 2
    ret = (rel > 0).astype(jnp.int32) * nb
    n = jnp.abs(rel)
    nf = jnp.maximum(n, 1).astype(F32)
    large = max_exact + (jnp.log(nf / max_exact) / math.log(MAX_DIST / max_exact)
                         * (nb - max_exact)).astype(jnp.int32)
    large = jnp.minimum(large, nb - 1)
    return ret + jnp.where(n < max_exact, n, large)


def _bias_lookup(bucket, tab):
    out = jnp.zeros((tab.shape[1],) + bucket.shape, F32)
    expand = (slice(None),) + (None,) * bucket.ndim
    for b in range(N_BUCKETS):
        out = jnp.where(bucket[None] == b, tab[b][expand], out)
    return out


def _rms_bf16(x, g):
    ms = jnp.mean(x * x, axis=-1, keepdims=True)
    return (x * lax.rsqrt(ms + EPS) * g).astype(BF16)


def _inproj_kernel(x_ref, g_ref, w_ref, o_ref, h_scr):
    @pl.when(pl.program_id(1) == 0)
    def _():
        h_scr[...] = _rms_bf16(x_ref[...], g_ref[...])

    o_ref[...] = jnp.dot(h_scr[...], w_ref[...], preferred_element_type=F32).astype(o_ref.dtype)


def _inproj(x2, g, w):
    T, D = x2.shape
    N = w.shape[1]
    tm = min(TM_PROJ, T)
    return pl.pallas_call(
        _inproj_kernel,
        out_shape=jax.ShapeDtypeStruct((T, N), BF16),
        grid=(T // tm, N // TN_PROJ),
        in_specs=[pl.BlockSpec((tm, D), lambda i, j: (i, 0)),
                  pl.BlockSpec((1, D), lambda i, j: (0, 0)),
                  pl.BlockSpec((D, TN_PROJ), lambda i, j: (0, j))],
        out_specs=pl.BlockSpec((tm, TN_PROJ), lambda i, j: (i, j)),
        scratch_shapes=[pltpu.VMEM((tm, D), BF16)],
        compiler_params=_cparams(("parallel", "arbitrary")),
        name="inproj",
    )(x2, g, w)


def _inproj_t_kernel(x_ref, g_ref, wt_ref, o_ref):
    h = _rms_bf16(x_ref[0], g_ref[...])
    res = lax.dot_general(wt_ref[...], h, (((1,), (1,)), ((), ())),
                          preferred_element_type=F32).astype(o_ref.dtype)
    for hd in range(o_ref.shape[1]):
        for n in range(o_ref.shape[2]):
            o_ref[0, hd, n] = res[hd * LANES:(hd + 1) * LANES, n * T_ATT:(n + 1) * T_ATT]


def _inproj_t(x3, g, wt):
    B, S, D = x3.shape
    N = wt.shape[0]
    tm = min(TM_PROJ, S)
    nh, nb = N // LANES, tm // T_ATT
    return pl.pallas_call(
        _inproj_t_kernel,
        out_shape=jax.ShapeDtypeStruct((B, nh, S // T_ATT, LANES, T_ATT), BF16),
        grid=(B, S // tm),
        in_specs=[pl.BlockSpec((1, tm, D), lambda b, i: (b, i, 0)),
                  pl.BlockSpec((1, D), lambda b, i: (0, 0)),
                  pl.BlockSpec((N, D), lambda b, i: (0, 0))],
        out_specs=pl.BlockSpec((1, nh, nb, LANES, T_ATT), lambda b, i: (b, 0, i, 0, 0)),
        compiler_params=_cparams(("parallel", "parallel")),
        name="inproj_t",
    )(x3, g, wt)


def _inproj_perm_kernel(x_ref, g_ref, p_ref, w_ref, o_ref, *, r):
    h = _rms_bf16(x_ref[0], g_ref[...])
    nblk = h.shape[0] // PERM_BLK
    hp = jnp.concatenate(
        [jnp.dot(p_ref[...], h[k * PERM_BLK:(k + 1) * PERM_BLK], preferred_element_type=F32)
         for k in range(nblk)], axis=0).astype(BF16)
    res = jnp.dot(hp, w_ref[...], preferred_element_type=F32).astype(o_ref.dtype)
    n = PERM_BLK // r
    for k in range(nblk):
        for s in range(r):
            o_ref[0, s, k * n:(k + 1) * n, :] = res[k * PERM_BLK + s * n:k * PERM_BLK + (s + 1) * n, :]


def _inproj_perm(x3, g, w, r):
    B, S, D = x3.shape
    N = w.shape[1]
    tm = min(TM_PERM, S)
    n = PERM_BLK // r
    o = jnp.arange(PERM_BLK, dtype=jnp.int32)
    src = (o % n) * r + o // n
    perm = (src[:, None] == jnp.arange(PERM_BLK, dtype=jnp.int32)[None, :]).astype(BF16)
    kern = functools.partial(_inproj_perm_kernel, r=r)
    return pl.pallas_call(
        kern,
        out_shape=jax.ShapeDtypeStruct((B, r, S // r, N), BF16),
        grid=(B, S // tm),
        in_specs=[pl.BlockSpec((1, tm, D), lambda b, i: (b, i, 0)),
                  pl.BlockSpec((1, D), lambda b, i: (0, 0)),
                  pl.BlockSpec((PERM_BLK, PERM_BLK), lambda b, i: (0, 0)),
                  pl.BlockSpec((D, N), lambda b, i: (0, 0))],
        out_specs=pl.BlockSpec((1, r, tm // r, N), lambda b, i: (b, 0, i, 0)),
        compiler_params=_cparams(("parallel", "parallel")),
        name=f"inproj_perm_{r}",
    )(x3, g, perm, w)


def _attn_a_kernel(lam_ref, cfar_ref, q_ref, k_ref, vt_ref, bias_ref, g_ref, o_ref, *, out_scale):
    h, qi = pl.program_id(1), pl.program_id(2)
    t = T_ATT
    nk = k_ref.shape[1] // t
    q = q_ref[0]
    low_half = lax.broadcasted_iota(jnp.int32, (1, LANES), 1) < HALF_LANES
    zero = jnp.zeros_like(q)
    qs = jnp.concatenate([jnp.where(low_half, q, zero), jnp.where(low_half, zero, q)], axis=0)

    blocks, sts, shifts = [], [], []
    m = None
    for d in range(-1, nk - 1):
        a = lax.rem(qi + (d + nk), nk)
        delta = a - qi
        kb = k_ref[0, pl.ds(pl.multiple_of(a * t, t), t), :]
        st = lax.dot_general(kb, qs, (((1,), (1,)), ((), ())), preferred_element_type=F32)
        if d <= 1:
            tile = bias_ref[0, jnp.clip(delta, -2, 2) + 2]
            st = st + jnp.concatenate([tile, tile], axis=1)
            shift = None
            cm = jnp.max(st, axis=0, keepdims=True)
        else:
            shift = cfar_ref[2 * h + (delta > 0).astype(jnp.int32)]
            cm = jnp.max(st, axis=0, keepdims=True) + shift
        m = cm if m is None else jnp.maximum(m, cm)
        blocks.append(a)
        sts.append(st)
        shifts.append(shift)

    l = jnp.zeros_like(m)
    accs = [jnp.zeros((LANES, t), F32), jnp.zeros((LANES, t), F32)]
    for a, st, shift in zip(blocks, sts, shifts):
        p = jnp.exp2(st - (m if shift is None else m - shift))
        l = l + jnp.sum(p, axis=0, keepdims=True)
        pb = p.astype(BF16)
        vt = vt_ref[0, 0, a]
        for c in range(2):
            accs[c] = accs[c] + jnp.dot(vt, pb[:, c * t:(c + 1) * t], preferred_element_type=F32)

    ot = accs[0] / l[:, :t] - lam_ref[0] * (accs[1] / l[:, t:])
    o = ot.T
    ms = jnp.mean(o * o, axis=-1, keepdims=True)
    o_ref[0] = (o * lax.rsqrt(ms + EPS) * g_ref[...] * out_scale).astype(o_ref.dtype)


def _attn_a(proj3, vt, lam, cfar, bias5, subln_g, lam_init):
    B, S, _ = proj3.shape
    t = T_ATT
    nk = S // t
    kern = functools.partial(_attn_a_kernel, out_scale=1.0 - lam_init)
    return pl.pallas_call(
        kern,
        out_shape=jax.ShapeDtypeStruct((B, S, HA * 2 * DA), BF16),
        grid=(B, HA, nk),
        in_specs=[
            pl.BlockSpec(memory_space=pltpu.SMEM),
            pl.BlockSpec(memory_space=pltpu.SMEM),
            pl.BlockSpec((1, t, LANES), lambda b, h, qi: (b, qi, h)),
            pl.BlockSpec((1, S, LANES), lambda b, h, qi: (b, 0, HA + h)),
            pl.BlockSpec((1, 1, nk, LANES, t), lambda b, h, qi: (b, h, 0, 0, 0)),
            pl.BlockSpec((1, 5, t, t), lambda b, h, qi: (h, 0, 0, 0)),
            pl.BlockSpec((1, LANES), lambda b, h, qi: (0, 0)),
        ],
        out_specs=pl.BlockSpec((1, t, LANES), lambda b, h, qi: (b, qi, h)),
        compiler_params=_cparams(("parallel", "parallel", "arbitrary")),
        name="diff_attn",
    )(lam, cfar, proj3, proj3, vt, bias5, subln_g)


def _attn_a_bias(rel_bias):
    t = T_ATT
    tab = rel_bias[:, :HA].astype(F32) * LOG2E
    d = jnp.arange(-1, 2, dtype=jnp.int32)[:, None, None] * t
    rel = d + jnp.arange(t, dtype=jnp.int32)[None, :, None] - jnp.arange(t, dtype=jnp.int32)[None, None, :]
    near = _bias_lookup(_t5_bucket(rel), tab)
    far = tab[_t5_bucket(jnp.array([-(t + 1), t + 1], dtype=jnp.int32))].T
    fill = lambda side: jnp.broadcast_to(far[:, side, None, None, None], (HA, 1, t, t))
    tiles = jnp.concatenate([fill(0), near, fill(1)], axis=1)
    return tiles, far.reshape(2 * HA)


def _attn_b_kernel(q_ref, k_ref, v_ref, bias_ref, o_ref, lse_ref, *, sub_len, r, sp, qp):
    nblk = sub_len // QB_DIL
    low_half = lax.broadcasted_iota(jnp.int32, (1, LANES), 1) < HALF_LANES
    for si in range(sp):
        s = si if sp == r else pl.program_id(2) * sp + si
        for qb in range(qp):
            i = pl.program_id(1) * qp + qb
            start = jnp.clip(i * QB_DIL - HALF_WIN, 0, sub_len - KW_DIL)
            start = pl.multiple_of(start, HALF_WIN)
            variant = jnp.where(i == 0, 0, jnp.where(i == nblk - 1, 2, 1))
            q = q_ref[0, si, qb * QB_DIL:(qb + 1) * QB_DIL, :]
            kw = k_ref[0, s, pl.ds(start, KW_DIL), :]
            vw = v_ref[0, s, pl.ds(start, KW_DIL), :]
            rows = (slice(qb * QB_DIL, (qb + 1) * QB_DIL) if r == 1
                    else pl.ds(s, QB_DIL, stride=r))
            for j in range(HB // 2):
                cols = slice(j * LANES, (j + 1) * LANES)
                qpair, kp, vp = q[:, cols], kw[:, cols], vw[:, cols]
                outs, lses = [], []
                for c in range(2):
                    qc = jnp.where(low_half if c == 0 else jnp.logical_not(low_half), qpair,
                                   jnp.zeros_like(qpair))
                    sc = lax.dot_general(qc, kp, (((1,), (1,)), ((), ())),
                                         preferred_element_type=F32)
                    sc = sc + bias_ref[2 * j + c, variant]
                    m = jnp.max(sc, axis=-1, keepdims=True)
                    p = jnp.exp(sc - m)
                    l = jnp.sum(p, axis=-1, keepdims=True)
                    outs.append(jnp.dot(p.astype(BF16), vp, preferred_element_type=F32) / l)
                    lses.append(m + jnp.log(l))
                o_ref[0, j, rows, :] = jnp.where(low_half, outs[0], outs[1])
                lse_ref[0, j, rows, :] = jnp.where(low_half, lses[0], lses[1])


def _attn_b(qkv4, bias3, g, cols):
    B, r, L, _ = qkv4.shape
    S = r * L
    width = HB * DB
    nblk = L // QB_DIL
    sp = min(r, ITEMS_DIL)
    qp = ITEMS_DIL // sp
    qcol, kcol, vcol = cols
    kern = functools.partial(_attn_b_kernel, sub_len=L, r=r, sp=sp, qp=qp)
    slab = jax.ShapeDtypeStruct((B, N_SLABS, S, LANES), F32)
    slab_spec = pl.BlockSpec((1, N_SLABS, QB_DIL * r * qp, LANES), lambda b, i, s: (b, 0, i, 0))
    return pl.pallas_call(
        kern,
        out_shape=[slab, slab],
        grid=(B, nblk // qp, r // sp),
        in_specs=[
            pl.BlockSpec((1, sp, QB_DIL * qp, width), lambda b, i, s: (b, s, i, qcol)),
            pl.BlockSpec((1, r, L, width), lambda b, i, s: (b, 0, 0, kcol)),
            pl.BlockSpec((1, r, L, width), lambda b, i, s: (b, 0, 0, vcol)),
            pl.BlockSpec((HB, 3, QB_DIL, KW_DIL), lambda b, i, s: (0, 0, 0, 0)),
        ],
        out_specs=[slab_spec, slab_spec],
        compiler_params=_cparams(("parallel", "arbitrary", "arbitrary")),
        name=f"dilated_attn_{g}",
    )(qkv4, qkv4, qkv4, bias3)


def _attn_b_bias(rel_bias, g):
    r = DILATIONS[g]
    tab = rel_bias[:, HA + g * HB: HA + (g + 1) * HB].astype(F32)
    off = jnp.arange(3, dtype=jnp.int32)[:, None, None] * HALF_WIN
    rel = (jnp.arange(KW_DIL, dtype=jnp.int32)[None, None, :] - off
           - jnp.arange(QB_DIL, dtype=jnp.int32)[None, :, None])
    bias = _bias_lookup(_t5_bucket(rel * r), tab)
    return jnp.where((jnp.abs(rel) <= HALF_WIN)[None], bias, NEG)


def _sgu_kernel(zu_ref, zv_ref, lng_ref, lnb_ref, ws_ref, bs_ref, o_ref):
    u = jax.nn.gelu(zu_ref[...].astype(F32))
    v = jax.nn.gelu(zv_ref[...].astype(F32))
    mu = jnp.mean(v, axis=-1, keepdims=True)
    var = jnp.mean(jnp.square(v - mu), axis=-1, keepdims=True)
    v = ((v - mu) * lax.rsqrt(var + EPS) * lng_ref[...] + lnb_ref[...]).astype(BF16)
    gd = v.shape[1] // C_GROUPS
    for n in range(v.shape[0] // CHUNK):
        rows = slice(n * CHUNK, (n + 1) * CHUNK)
        for g in range(C_GROUPS):
            cols = slice(g * gd, (g + 1) * gd)
            mixed = jnp.dot(ws_ref[g], v[rows, cols], preferred_element_type=F32) + bs_ref[:, cols]
            o_ref[rows, cols] = (u[rows, cols] * mixed).astype(o_ref.dtype)


def _sgu(proj2, ln_g, ln_b, w_s, b_exp):
    T = proj2.shape[0]
    tm = min(TM_SGU, T)
    w = MIX_W
    return pl.pallas_call(
        _sgu_kernel,
        out_shape=jax.ShapeDtypeStruct((T, w), BF16),
        grid=(T // tm,),
        in_specs=[pl.BlockSpec((tm, w), lambda i: (i, COL_ZU)),
                  pl.BlockSpec((tm, w), lambda i: (i, COL_ZU + 1)),
                  pl.BlockSpec((1, w), lambda i: (0, 0)),
                  pl.BlockSpec((1, w), lambda i: (0, 0)),
                  pl.BlockSpec((C_GROUPS, CHUNK, CHUNK), lambda i: (0, 0, 0)),
                  pl.BlockSpec((CHUNK, w), lambda i: (0, 0))],
        out_specs=pl.BlockSpec((tm, w), lambda i: (i, 0)),
        compiler_params=_cparams(("parallel",)),
        name="sgu",
    )(proj2, proj2, ln_g, ln_b, w_s, b_exp)


def _route(logits):
    lane = lax.broadcasted_iota(jnp.int32, logits.shape, 1)
    big = jnp.int32(LANES)
    is_grp = (lane >= N_EXPERTS) & (lane < N_EXPERTS + N_GROUPS)
    gl = jnp.where(is_grp, logits, NEG)
    gmax = jnp.max(gl, axis=-1, keepdims=True)
    g_idx = jnp.min(jnp.where(is_grp & (gl == gmax), lane, big), axis=-1, keepdims=True) - N_EXPERTS
    g_w = 1.0 / jnp.sum(jnp.where(is_grp, jnp.exp(gl - gmax), 0.0), axis=-1, keepdims=True)
    in_grp = (lane >= g_idx * E_PER_GROUP) & (lane < (g_idx + 1) * E_PER_GROUP)
    sel = jnp.where(in_grp, logits, NEG)
    v1 = jnp.max(sel, axis=-1, keepdims=True)
    i1 = jnp.min(jnp.where(in_grp & (sel == v1), lane, big), axis=-1, keepdims=True)
    rest = in_grp & (lane != i1)
    sel2 = jnp.where(rest, logits, NEG)
    v2 = jnp.max(sel2, axis=-1, keepdims=True)
    i2 = jnp.min(jnp.where(rest & (sel2 == v2), lane, big), axis=-1, keepdims=True)
    e2 = jnp.exp(v2 - v1)
    w1 = g_w / (1.0 + e2)
    w2 = g_w * e2 / (1.0 + e2)
    return jnp.where(lane == i1, w1, jnp.where(lane == i2, w2, 0.0))


def _mix_kernel(x_ref, ya_ref, ob0_ref, ob1_ref, ob2_ref, ls0_ref, ls1_ref, ls2_ref, yc_ref,
                g0_ref, g1_ref, g2_ref, wb_ref, wo_ref, nf_ref, wr_ref, br_ref,
                xo_ref, h_ref, comb_ref):
    slabs = []
    for j in range(N_SLABS):
        ls0, ls1, ls2 = ls0_ref[0, j], ls1_ref[0, j], ls2_ref[0, j]
        mx = jnp.maximum(jnp.maximum(ls0, ls1), ls2)
        e0, e1, e2 = jnp.exp(ls0 - mx), jnp.exp(ls1 - mx), jnp.exp(ls2 - mx)
        yb = (e0 * ob0_ref[0, j] + e1 * ob1_ref[0, j] + e2 * ob2_ref[0, j]) / (e0 + e1 + e2)
        slabs.append(yb.astype(BF16))
    yb = jnp.concatenate(slabs, axis=-1)
    merged = jax.nn.sigmoid(g0_ref[...].astype(F32)) * jnp.dot(ya_ref[...], wb_ref[0],
                                                               preferred_element_type=F32)
    merged += jax.nn.sigmoid(g1_ref[...].astype(F32)) * jnp.dot(yb, wb_ref[1],
                                                                preferred_element_type=F32)
    merged += jax.nn.sigmoid(g2_ref[...].astype(F32)) * jnp.dot(yc_ref[...], wb_ref[2],
                                                                preferred_element_type=F32)
    xn = x_ref[...] + jnp.dot(merged.astype(BF16), wo_ref[...], preferred_element_type=F32)
    xo_ref[...] = xn
    h = _rms_bf16(xn, nf_ref[...])
    h_ref[...] = h
    logits = jnp.dot(h, wr_ref[...], preferred_element_type=F32) + br_ref[...]
    comb_ref[...] = _route(logits)


def _mix(x2, ya, obs, lses, yc, proj2, wb, wo, nf, wr, br):
    T, D = x2.shape
    S = obs[0].shape[2]
    tm = min(TM_MIX, S)
    per_b = S // tm
    w = MIX_W
    row = lambda width: pl.BlockSpec((tm, width), lambda i: (i, 0))
    full = lambda a: pl.BlockSpec(a.shape, lambda i: (0,) * a.ndim)
    gate = lambda n: pl.BlockSpec((tm, D), lambda i: (i, COL_GATE + n))
    slab = pl.BlockSpec((1, N_SLABS, tm, LANES), lambda i: (i // per_b, 0, i % per_b, 0))
    return pl.pallas_call(
        _mix_kernel,
        out_shape=[jax.ShapeDtypeStruct((T, D), F32), jax.ShapeDtypeStruct((T, D), BF16),
                   jax.ShapeDtypeStruct((T, LANES), F32)],
        grid=(T // tm,),
        in_specs=[row(D), row(w), slab, slab, slab, slab, slab, slab, row(w),
                  gate(0), gate(1), gate(2), full(wb), full(wo), full(nf), full(wr), full(br)],
        out_specs=[row(D), row(D), row(LANES)],
        compiler_params=_cparams(("parallel",)),
        name="mix",
    )(x2, ya, obs[0], obs[1], obs[2], lses[0], lses[1], lses[2], yc, proj2, proj2, proj2,
      wb, wo, nf, wr, br)


def _moe_kernel(h_ref, comb_ref, x_ref, wg_ref, wu_ref, wd_ref, nfin_ref, o_ref, acc_scr,
                *, final_norm):
    e = pl.program_id(1)

    @pl.when(e == 0)
    def _():
        acc_scr[...] = jnp.zeros(acc_scr.shape, F32)

    h = h_ref[...]
    lane = lax.broadcasted_iota(jnp.int32, comb_ref.shape, 1)
    c = jnp.sum(jnp.where(lane == e, comb_ref[...], 0.0), axis=-1, keepdims=True)
    hid = (jax.nn.silu(jnp.dot(h, wg_ref[0].astype(BF16), preferred_element_type=F32))
           * jnp.dot(h, wu_ref[0].astype(BF16), preferred_element_type=F32))
    acc_scr[...] += c * jnp.dot(hid.astype(BF16), wd_ref[0].astype(BF16),
                                preferred_element_type=F32)

    @pl.when(e == pl.num_programs(1) - 1)
    def _():
        xn = x_ref[...] + acc_scr[...]
        if final_norm:
            ms = jnp.mean(xn * xn, axis=-1, keepdims=True)
            xn = xn * lax.rsqrt(ms + EPS) * nfin_ref[...]
        o_ref[...] = xn


def _moe_dense(h, comb, x2, wg, wu, wd, nfin, final_norm):
    T, D = x2.shape
    tm = min(TM_MOE, T)
    F = wg.shape[2]
    kern = functools.partial(_moe_kernel, final_norm=final_norm)
    return pl.pallas_call(
        kern,
        out_shape=jax.ShapeDtypeStruct((T, D), F32),
        grid=(T // tm, N_EXPERTS),
        in_specs=[pl.BlockSpec((tm, D), lambda i, e: (i, 0)),
                  pl.BlockSpec((tm, LANES), lambda i, e: (i, 0)),
                  pl.BlockSpec((tm, D), lambda i, e: (i, 0)),
                  pl.BlockSpec((1, D, F), lambda i, e: (e, 0, 0)),
                  pl.BlockSpec((1, D, F), lambda i, e: (e, 0, 0)),
                  pl.BlockSpec((1, F, D), lambda i, e: (e, 0, 0)),
                  pl.BlockSpec((1, D), lambda i, e: (0, 0))],
        out_specs=pl.BlockSpec((tm, D), lambda i, e: (i, 0)),
        scratch_shapes=[pltpu.VMEM((tm, D), F32)],
        compiler_params=_cparams(("parallel", "arbitrary")),
        name="moe_dense",
    )(h, comb, x2, wg, wu, wd, nfin)


def _moe_dispatch_kernel(h_ref, comb_ref, o_ref, cnt_ref):
    tm = h_ref.shape[0]
    comb = comb_ref[...]
    hi = comb.astype(BF16)
    lo = (comb - hi.astype(F32)).astype(BF16)
    haug = jnp.concatenate([h_ref[...], hi, lo], axis=1)
    a_t = comb.T[:N_EXPERTS] > 0.0
    a_f = jnp.where(a_t, 1.0, 0.0)
    before = (lax.broadcasted_iota(jnp.int32, (tm, tm), 0)
              < lax.broadcasted_iota(jnp.int32, (tm, tm), 1))
    rank_t = jnp.dot(a_f.astype(BF16), jnp.where(before, 1.0, 0.0).astype(BF16),
                     preferred_element_type=F32)
    slot = lax.broadcasted_iota(jnp.int32, (MOE_CAP, tm), 0).astype(F32)
    blocks = [jnp.where((slot == rank_t[e:e + 1]) & a_t[e:e + 1], 1.0, 0.0).astype(BF16)
              for e in range(N_EXPERTS)]
    res = jnp.dot(jnp.concatenate(blocks, axis=0), haug, preferred_element_type=F32)
    res = res.astype(o_ref.dtype)
    for e in range(N_EXPERTS):
        o_ref[0, e] = res[e * MOE_CAP:(e + 1) * MOE_CAP]
    cnt_ref[0] = jnp.broadcast_to(jnp.sum(a_f, axis=1, keepdims=True), cnt_ref.shape[1:])


def _moe_dispatch(h, comb):
    T, D = h.shape
    tm = min(TM_DISP, T)
    n = T // tm
    return pl.pallas_call(
        _moe_dispatch_kernel,
        out_shape=[jax.ShapeDtypeStruct((n, N_EXPERTS, MOE_CAP, D + 2 * LANES), BF16),
                   jax.ShapeDtypeStruct((n, N_EXPERTS, LANES), F32)],
        grid=(n,),
        in_specs=[pl.BlockSpec((tm, D), lambda i: (i, 0)),
                  pl.BlockSpec((tm, LANES), lambda i: (i, 0))],
        out_specs=[pl.BlockSpec((1, N_EXPERTS, MOE_CAP, D + 2 * LANES), lambda i: (i, 0, 0, 0)),
                   pl.BlockSpec((1, N_EXPERTS, LANES), lambda i: (i, 0, 0))],
        compiler_params=_cparams(("parallel",)),
        name="moe_dispatch",
    )(h, comb)


def _moe_ffn_kernel(n16_ref, s_ref, wg_ref, wu_ref, wd_ref, o_ref,
                    wg_scr, wu_scr, wd_scr, lhs_scr, y_scr):
    e, c = pl.program_id(0), pl.program_id(1)

    @pl.when(c == 0)
    def _():
        wg_scr[...] = wg_ref[0].astype(BF16)
        wu_scr[...] = wu_ref[0].astype(BF16)
        wd_scr[...] = wd_ref[0].astype(BF16)

    g, _, cap, _ = s_ref.shape
    D = o_ref.shape[-1]
    @pl.when((e == 0) & (c == 0))
    def _():
        lhs_scr[...] = jnp.zeros(lhs_scr.shape, lhs_scr.dtype)
        y_scr[...] = jnp.zeros(y_scr.shape, y_scr.dtype)

    offs = []
    off = jnp.int32(0)
    for t in range(g):
        offs.append(off)
        lhs_scr[pl.ds(pl.multiple_of(off, BF16_ROWS), cap), :] = s_ref[t, 0]
        off = off + n16_ref[(c * g + t) * N_EXPERTS + e]
    total = off

    def run(nrows):
        rows = lhs_scr[:nrows]
        h = rows[:, :D]
        wparts = rows[:, D:].astype(F32)
        lane = lax.broadcasted_iota(jnp.int32, wparts.shape, 1)
        w = jnp.sum(jnp.where(lane % LANES == e, wparts, 0.0), axis=-1, keepdims=True)
        hid = (jax.nn.silu(jnp.dot(h, wg_scr[...], preferred_element_type=F32))
               * jnp.dot(h, wu_scr[...], preferred_element_type=F32))
        y = w * jnp.dot(hid.astype(BF16), wd_scr[...], preferred_element_type=F32)
        y_scr[:nrows] = y.astype(y_scr.dtype)

    classes = tuple(range(g * cap // 2, g * cap + 1, FFN_ROW_STEP))
    lower = 0
    for nrows in classes:
        pl.when((total > lower) & (total <= nrows))(functools.partial(run, nrows))
        lower = nrows

    for t in range(g):
        o_ref[t, 0] = y_scr[pl.ds(pl.multiple_of(offs[t], BF16_ROWS), cap), :]


def _moe_ffn(srt, n16, wg, wu, wd):
    n, ne, cap, wdt = srt.shape
    D, F = wg.shape[1], wg.shape[2]
    g = min(G_FFN, n)
    return pl.pallas_call(
        _moe_ffn_kernel,
        out_shape=jax.ShapeDtypeStruct((n, ne, cap, D), BF16),
        grid_spec=pltpu.PrefetchScalarGridSpec(
            num_scalar_prefetch=1,
            grid=(ne, n // g),
            in_specs=[pl.BlockSpec((g, 1, cap, wdt), lambda e, c, n16: (c, e, 0, 0)),
                      pl.BlockSpec((1, D, F), lambda e, c, n16: (e, 0, 0)),
                      pl.BlockSpec((1, D, F), lambda e, c, n16: (e, 0, 0)),
                      pl.BlockSpec((1, F, D), lambda e, c, n16: (e, 0, 0))],
            out_specs=pl.BlockSpec((g, 1, cap, D), lambda e, c, n16: (c, e, 0, 0)),
            scratch_shapes=[pltpu.VMEM((D, F), BF16), pltpu.VMEM((D, F), BF16),
                            pltpu.VMEM((F, D), BF16), pltpu.VMEM((g * cap, wdt), BF16),
                            pltpu.VMEM((g * cap, D), BF16)]),
        compiler_params=_cparams(("arbitrary", "arbitrary")),
        name="moe_ffn",
    )(n16, srt, wg, wu, wd)


def _moe_combine_kernel(y_ref, comb_ref, x_ref, nfin_ref, o_ref, *, final_norm):
    tm = x_ref.shape[0]
    comb = comb_ref[...]
    a = comb > 0.0
    before = (lax.broadcasted_iota(jnp.int32, (tm, tm), 1)
              < lax.broadcasted_iota(jnp.int32, (tm, tm), 0))
    rank = jnp.dot(jnp.where(before, 1.0, 0.0).astype(BF16), jnp.where(a, 1.0, 0.0).astype(BF16),
                   preferred_element_type=F32)
    key = jnp.where(a, rank, -1.0).astype(BF16)
    ncol = N_EXPERTS * MOE_CAP
    spread = (lax.broadcasted_iota(jnp.int32, (LANES, ncol), 1) // MOE_CAP
              == lax.broadcasted_iota(jnp.int32, (LANES, ncol), 0))
    key_all = jnp.dot(key, jnp.where(spread, 1.0, 0.0).astype(BF16), preferred_element_type=F32)
    slot = (lax.broadcasted_iota(jnp.int32, (tm, ncol), 1) % MOE_CAP).astype(F32)
    pc = jnp.where(slot == key_all, 1.0, 0.0).astype(BF16)
    y = jnp.concatenate([y_ref[0, e] for e in range(N_EXPERTS)], axis=0)
    xn = x_ref[...] + jnp.dot(pc, y, preferred_element_type=F32)
    if final_norm:
        ms = jnp.mean(xn * xn, axis=-1, keepdims=True)
        xn = xn * lax.rsqrt(ms + EPS) * nfin_ref[...]
    o_ref[...] = xn


def _moe_combine(y, comb, x2, nfin, final_norm):
    T, D = x2.shape
    n, ne, cap, _ = y.shape
    tm = T // n
    kern = functools.partial(_moe_combine_kernel, final_norm=final_norm)
    return pl.pallas_call(
        kern,
        out_shape=jax.ShapeDtypeStruct((T, D), F32),
        grid=(n,),
        in_specs=[pl.BlockSpec((1, ne, cap, D), lambda i: (i, 0, 0, 0)),
                  pl.BlockSpec((tm, LANES), lambda i: (i, 0)),
                  pl.BlockSpec((tm, D), lambda i: (i, 0)),
                  pl.BlockSpec((1, D), lambda i: (0, 0))],
        out_specs=pl.BlockSpec((tm, D), lambda i: (i, 0)),
        compiler_params=_cparams(("parallel",)),
        name="moe_combine",
    )(y, comb, x2, nfin)


def _moe_fix_kernel(tiles_ref, experts_ref, first_ref, n_ref, h_ref, comb_ref, prev_ref,
                    wg_ref, wu_ref, wd_ref, o_ref):
    del tiles_ref
    s = pl.program_id(0)

    @pl.when(s < n_ref[0])
    def _():
        e = experts_ref[s]
        tm = h_ref.shape[0]
        comb = comb_ref[...]
        a = jnp.where(comb > 0.0, 1.0, 0.0)
        before = (lax.broadcasted_iota(jnp.int32, (tm, tm), 1)
                  < lax.broadcasted_iota(jnp.int32, (tm, tm), 0))
        rank = jnp.dot(jnp.where(before, 1.0, 0.0).astype(BF16), a.astype(BF16),
                       preferred_element_type=F32)
        lane = lax.broadcasted_iota(jnp.int32, comb.shape, 1)
        dropped = (lane == e) & (rank >= MOE_CAP)
        c = jnp.sum(jnp.where(dropped, comb, 0.0), axis=-1, keepdims=True)
        h = h_ref[...]
        hid = (jax.nn.silu(jnp.dot(h, wg_ref[0].astype(BF16), preferred_element_type=F32))
               * jnp.dot(h, wu_ref[0].astype(BF16), preferred_element_type=F32))
        add = c * jnp.dot(hid.astype(BF16), wd_ref[0].astype(BF16), preferred_element_type=F32)
        fresh = first_ref[s] == 1

        @pl.when(fresh)
        def _():
            o_ref[...] = prev_ref[...] + add

        @pl.when(jnp.logical_not(fresh))
        def _():
            o_ref[...] += add


def _moe_fix(tiles, experts, first, n, out, h, comb, wg, wu, wd):
    T, D = out.shape
    tm = min(TM_DISP, T)
    F = wg.shape[2]
    tile = lambda width: pl.BlockSpec((tm, width), lambda s, tl, ex, fi, n: (tl[s], 0))
    wspec = lambda shape: pl.BlockSpec(shape, lambda s, tl, ex, fi, n: (ex[s], 0, 0))
    return pl.pallas_call(
        _moe_fix_kernel,
        out_shape=jax.ShapeDtypeStruct((T, D), F32),
        grid_spec=pltpu.PrefetchScalarGridSpec(
            num_scalar_prefetch=4,
            grid=(MAX_OVF,),
            in_specs=[tile(D), tile(LANES), tile(D), wspec((1, D, F)), wspec((1, D, F)),
                      wspec((1, F, D))],
            out_specs=tile(D)),
        input_output_aliases={6: 0},
        compiler_params=_cparams(("arbitrary",)),
        name="moe_fix",
    )(tiles, experts, first, n, h, comb, out, wg, wu, wd)


def _final_norm_kernel(x_ref, g_ref, o_ref):
    x = x_ref[...]
    ms = jnp.mean(x * x, axis=-1, keepdims=True)
    o_ref[...] = x * lax.rsqrt(ms + EPS) * g_ref[...]


def _final_norm(x2, g):
    T, D = x2.shape
    tm = min(TM_PROJ, T)
    return pl.pallas_call(
        _final_norm_kernel,
        out_shape=jax.ShapeDtypeStruct((T, D), F32),
        grid=(T // tm,),
        in_specs=[pl.BlockSpec((tm, D), lambda i: (i, 0)), pl.BlockSpec((1, D), lambda i: (0, 0))],
        out_specs=pl.BlockSpec((tm, D), lambda i: (i, 0)),
        compiler_params=_cparams(("parallel",)),
        name="final_norm",
    )(x2, g)


def _moe(h, comb, x2, wg, wu, wd, nfin, final_norm):
    srt, cnt = _moe_dispatch(h, comb)
    over = (cnt[:, :, 0] > MOE_CAP).reshape(-1)
    n_ovf = jnp.sum(over.astype(jnp.int32))
    pairs = jnp.nonzero(over, size=MAX_OVF, fill_value=0)[0].astype(jnp.int32)
    pairs = jnp.where(jnp.arange(MAX_OVF) < n_ovf, pairs, pairs[jnp.clip(n_ovf - 1, 0, MAX_OVF - 1)])
    tiles, experts = pairs // N_EXPERTS, pairs % N_EXPERTS
    first = jnp.concatenate([jnp.ones((1,), jnp.int32),
                             (tiles[1:] != tiles[:-1]).astype(jnp.int32)])

    used = jnp.minimum(cnt[:, :, 0], MOE_CAP).astype(jnp.int32).reshape(-1)
    n16 = (used + (BF16_ROWS - 1)) // BF16_ROWS * BF16_ROWS

    def routed():
        y = _moe_ffn(srt, n16, wg, wu, wd)

        def fixed():
            out = _moe_combine(y, comb, x2, nfin, False)
            out = _moe_fix(tiles, experts, first, n_ovf.reshape(1), out, h, comb, wg, wu, wd)
            return _final_norm(out, nfin) if final_norm else out

        return lax.cond(n_ovf > 0, fixed, lambda: _moe_combine(y, comb, x2, nfin, final_norm))

    return lax.cond(n_ovf > MAX_OVF,
                    lambda: _moe_dense(h, comb, x2, wg, wu, wd, nfin, final_norm), routed)


def kernel(x, rel_bias, norm_mix, w_in, diff_lambda, diff_subln, sgu_ln_g, sgu_ln_b, sgu_w, sgu_b,
           w_branch, w_out, norm_ffn, w_router_grp, b_router_grp, w_router_exp, b_router_exp,
           w_gate, w_up, w_down, norm_final):
    B, S, D = x.shape
    T = B * S
    depth = w_in.shape[0]
    a_out = HA * 2 * DA
    grp_w = HB * DB
    b_cols = 3 * NG_B * grp_w
    qkv_b0 = 3 * a_out
    zc0 = qkv_b0 + b_cols
    gate0 = zc0 + 2 * MIX_W
    qk_scale = DA ** -0.5

    bias_a, cfar = _attn_a_bias(rel_bias)
    bias_b = [_attn_b_bias(rel_bias, g) for g in range(NG_B)]

    col = jnp.arange(w_in.shape[2])
    col_scale = jnp.where(col < a_out, qk_scale * LOG2E,
                          jnp.where((col >= qkv_b0) & (col < qkv_b0 + NG_B * grp_w), qk_scale, 1.0))

    def group_cols(w, g):
        return [w[:, qkv_b0 + (c * NG_B + g) * grp_w: qkv_b0 + (c * NG_B + g + 1) * grp_w]
                for c in range(3)]

    x2 = x.reshape(T, D)
    for i in range(depth):
        w = (w_in[i] * col_scale.astype(F32)).astype(BF16)
        nm = norm_mix[i][None, :]
        w_main = jnp.concatenate([w[:, :2 * a_out], w[:, zc0:]] + group_cols(w, 0), axis=1)
        proj2 = _inproj(x2, nm, w_main)
        x3 = x2.reshape(B, S, D)
        vt = _inproj_t(x3, nm, w[:, 2 * a_out:3 * a_out].T)
        proj3 = proj2.reshape(B, S, proj2.shape[1])

        lam_init = 0.8 - 0.6 * math.exp(-0.3 * i)
        lp = diff_lambda[i].astype(F32)
        lam = jnp.exp(jnp.sum(lp[0] * lp[1])) - jnp.exp(jnp.sum(lp[2] * lp[3])) + lam_init
        ya = _attn_a(proj3, vt, lam.reshape(1), cfar, bias_a, diff_subln[i][None, :], lam_init)

        obs, lses = [], []
        for g in range(NG_B):
            r = DILATIONS[g]
            if r == 1:
                qkv4, cols = proj3[:, None], (COL_QKV0, COL_QKV0 + 1, COL_QKV0 + 2)
            else:
                w_g = jnp.concatenate(group_cols(w, g), axis=1)
                qkv4, cols = _inproj_perm(x3, nm, w_g, r), (0, 1, 2)
            o, l = _attn_b(qkv4, bias_b[g], g, cols)
            obs.append(o)
            lses.append(l)

        b_exp = jnp.repeat(sgu_b[i].T, MIX_W // C_GROUPS, axis=1)
        yc = _sgu(proj2, sgu_ln_g[i][None, :], sgu_ln_b[i][None, :], sgu_w[i].astype(BF16), b_exp)

        wr = jnp.concatenate([w_router_exp[i].transpose(1, 0, 2).reshape(D, N_EXPERTS),
                              w_router_grp[i]], axis=1)
        wr = jnp.pad(wr, ((0, 0), (0, LANES - wr.shape[1]))).astype(BF16)
        br = jnp.concatenate([b_router_exp[i].reshape(N_EXPERTS), b_router_grp[i]])
        br = jnp.pad(br, (0, LANES - br.shape[0]))[None, :].astype(F32)

        x2, h, comb = _mix(x2, ya.reshape(T, a_out), obs, lses, yc, proj2,
                           w_branch[i].astype(BF16), w_out[i].astype(BF16), norm_ffn[i][None, :],
                           wr, br)
        x2 = _moe(h, comb, x2, w_gate[i], w_up[i], w_down[i], norm_final[None, :], i == depth - 1)
    return x2.reshape(B, S, D)
```

```python
import functools
import math

import jax
import jax.numpy as jnp
from jax import lax
from jax.experimental import pallas as pl
from jax.experimental.pallas import tpu as pltpu

F32 = jnp.float32
BF16 = jnp.bfloat16

EPS = 1e-6
NEG = -1e30
LOG2E = 1.4426950408889634
LANES = 128
HALF_LANES = LANES // 2
VMEM_LIMIT = 48 * 1024 * 1024

HA = 4
DA = 64
MIX_W = 512
WINDOWS = (128, 512, 2048)
DILATIONS = (1, 4, 16)
NG_B = 3
HB = 8
DB = 64
HALF_WIN = 64
CHUNK = 128
C_GROUPS = 4
N_BRANCH = 3
N_BUCKETS = 32
MAX_DIST = 128
N_GROUPS = 4
E_PER_GROUP = 4
N_EXPERTS = N_GROUPS * E_PER_GROUP
N_SLABS = MIX_W // LANES

TM_PROJ = 1024
TN_PROJ = 3328
TM_PERM = 512
PERM_BLK = 256
T_ATT = 512
QB_DIL = 128
KW_DIL = QB_DIL + 2 * HALF_WIN
ITEMS_DIL = 4
TM_SGU = 512
TM_MIX = 512
TM_MOE = 1024
TM_DISP = 256
MOE_CAP = HALF_LANES
G_FFN = 16
FFN_ROW_STEP = 64
BF16_ROWS = 16
MAX_OVF = 64

COL_ZU = 2
COL_GATE = 2
COL_QKV0 = 10


def _cparams(sem):
    return pltpu.CompilerParams(dimension_semantics=sem, vmem_limit_bytes=VMEM_LIMIT)


def _t5_bucket(rel):
    nb = N_BUCKETS // 2
    max_exact = nb // 2
    ret = (rel > 0).astype(jnp.int32) * nb
    n = jnp.abs(rel)
    nf = jnp.maximum(n, 1).astype(F32)
    large = max_exact + (jnp.log(nf / max_exact) / math.log(MAX_DIST / max_exact)
                         * (nb - max_exact)).astype(jnp.int32)
    large = jnp.minimum(large, nb - 1)
    return ret + jnp.where(n < max_exact, n, large)


def _bias_lookup(bucket, tab):
    out = jnp.zeros((tab.shape[1],) + bucket.shape, F32)
    expand = (slice(None),) + (None,) * bucket.ndim
    for b in range(N_BUCKETS):
        out = jnp.where(bucket[None] == b, tab[b][expand], out)
    return out


def _rms_bf16(x, g):
    ms = jnp.mean(x * x, axis=-1, keepdims=True)
    return (x * lax.rsqrt(ms + EPS) * g).astype(BF16)


def _inproj_kernel(x_ref, g_ref, w_ref, o_ref, h_scr):
    @pl.when(pl.program_id(1) == 0)
    def _():
        h_scr[...] = _rms_bf16(x_ref[...], g_ref[...])

    o_ref[...] = jnp.dot(h_scr[...], w_ref[...], preferred_element_type=F32).astype(o_ref.dtype)


def _inproj(x2, g, w):
    T, D = x2.shape
    N = w.shape[1]
    tm = min(TM_PROJ, T)
    return pl.pallas_call(
        _inproj_kernel,
        out_shape=jax.ShapeDtypeStruct((T, N), BF16),
        grid=(T // tm, N // TN_PROJ),
        in_specs=[pl.BlockSpec((tm, D), lambda i, j: (i, 0)),
                  pl.BlockSpec((1, D), lambda i, j: (0, 0)),
                  pl.BlockSpec((D, TN_PROJ), lambda i, j: (0, j))],
        out_specs=pl.BlockSpec((tm, TN_PROJ), lambda i, j: (i, j)),
        scratch_shapes=[pltpu.VMEM((tm, D), BF16)],
        compiler_params=_cparams(("parallel", "arbitrary")),
        name="inproj",
    )(x2, g, w)


def _inproj_t_kernel(x_ref, g_ref, wt_ref, o_ref):
    h = _rms_bf16(x_ref[0], g_ref[...])
    res = lax.dot_general(wt_ref[...], h, (((1,), (1,)), ((), ())),
                          preferred_element_type=F32).astype(o_ref.dtype)
    for hd in range(o_ref.shape[1]):
        for n in range(o_ref.shape[2]):
            o_ref[0, hd, n] = res[hd * LANES:(hd + 1) * LANES, n * T_ATT:(n + 1) * T_ATT]


def _inproj_t(x3, g, wt):
    B, S, D = x3.shape
    N = wt.shape[0]
    tm = min(TM_PROJ, S)
    nh, nb = N // LANES, tm // T_ATT
    return pl.pallas_call(
        _inproj_t_kernel,
        out_shape=jax.ShapeDtypeStruct((B, nh, S // T_ATT, LANES, T_ATT), BF16),
        grid=(B, S // tm),
        in_specs=[pl.BlockSpec((1, tm, D), lambda b, i: (b, i, 0)),
                  pl.BlockSpec((1, D), lambda b, i: (0, 0)),
                  pl.BlockSpec((N, D), lambda b, i: (0, 0))],
        out_specs=pl.BlockSpec((1, nh, nb, LANES, T_ATT), lambda b, i: (b, 0, i, 0, 0)),
        compiler_params=_cparams(("parallel", "parallel")),
        name="inproj_t",
    )(x3, g, wt)


def _inproj_perm_kernel(x_ref, g_ref, p_ref, w_ref, o_ref, *, r):
    h = _rms_bf16(x_ref[0], g_ref[...])
    nblk = h.shape[0] // PERM_BLK
    hp = jnp.concatenate(
        [jnp.dot(p_ref[...], h[k * PERM_BLK:(k + 1) * PERM_BLK], preferred_element_type=F32)
         for k in range(nblk)], axis=0).astype(BF16)
    res = jnp.dot(hp, w_ref[...], preferred_element_type=F32).astype(o_ref.dtype)
    n = PERM_BLK // r
    for k in range(nblk):
        for s in range(r):
            o_ref[0, s, k * n:(k + 1) * n, :] = res[k * PERM_BLK + s * n:k * PERM_BLK + (s + 1) * n, :]


def _inproj_perm(x3, g, w, r):
    B, S, D = x3.shape
    N = w.shape[1]
    tm = min(TM_PERM, S)
    n = PERM_BLK // r
    o = jnp.arange(PERM_BLK, dtype=jnp.int32)
    src = (o % n) * r + o // n
    perm = (src[:, None] == jnp.arange(PERM_BLK, dtype=jnp.int32)[None, :]).astype(BF16)
    kern = functools.partial(_inproj_perm_kernel, r=r)
    return pl.pallas_call(
        kern,
        out_shape=jax.ShapeDtypeStruct((B, r, S // r, N), BF16),
        grid=(B, S // tm),
        in_specs=[pl.BlockSpec((1, tm, D), lambda b, i: (b, i, 0)),
                  pl.BlockSpec((1, D), lambda b, i: (0, 0)),
                  pl.BlockSpec((PERM_BLK, PERM_BLK), lambda b, i: (0, 0)),
                  pl.BlockSpec((D, N), lambda b, i: (0, 0))],
        out_specs=pl.BlockSpec((1, r, tm // r, N), lambda b, i: (b, 0, i, 0)),
        compiler_params=_cparams(("parallel", "parallel")),
        name=f"inproj_perm_{r}",
    )(x3, g, perm, w)


def _attn_a_kernel(lam_ref, cfar_ref, q_ref, k_ref, vt_ref, bias_ref, g_ref, o_ref, *, out_scale):
    h, qi = pl.program_id(1), pl.program_id(2)
    t = T_ATT
    nk = k_ref.shape[1] // t
    q = q_ref[0]
    low_half = lax.broadcasted_iota(jnp.int32, (1, LANES), 1) < HALF_LANES
    zero = jnp.zeros_like(q)
    qs = jnp.concatenate([jnp.where(low_half, q, zero), jnp.where(low_half, zero, q)], axis=0)

    blocks, sts, shifts = [], [], []
    m = None
    for d in range(-1, nk - 1):
        a = lax.rem(qi + (d + nk), nk)
        delta = a - qi
        kb = k_ref[0, pl.ds(pl.multiple_of(a * t, t), t), :]
        st = lax.dot_general(kb, qs, (((1,), (1,)), ((), ())), preferred_element_type=F32)
        if d <= 1:
            tile = bias_ref[0, jnp.clip(delta, -2, 2) + 2]
            st = st + jnp.concatenate([tile, tile], axis=1)
            shift = None
            cm = jnp.max(st, axis=0, keepdims=True)
        else:
            shift = cfar_ref[2 * h + (delta > 0).astype(jnp.int32)]
            cm = jnp.max(st, axis=0, keepdims=True) + shift
        m = cm if m is None else jnp.maximum(m, cm)
        blocks.append(a)
        sts.append(st)
        shifts.append(shift)

    l = jnp.zeros_like(m)
    accs = [jnp.zeros((LANES, t), F32), jnp.zeros((LANES, t), F32)]
    for a, st, shift in zip(blocks, sts, shifts):
        p = jnp.exp2(st - (m if shift is None else m - shift))
        l = l + jnp.sum(p, axis=0, keepdims=True)
        pb = p.astype(BF16)
        vt = vt_ref[0, 0, a]
        for c in range(2):
            accs[c] = accs[c] + jnp.dot(vt, pb[:, c * t:(c + 1) * t], preferred_element_type=F32)

    ot = accs[0] / l[:, :t] - lam_ref[0] * (accs[1] / l[:, t:])
    o = ot.T
    ms = jnp.mean(o * o, axis=-1, keepdims=True)
    o_ref[0] = (o * lax.rsqrt(ms + EPS) * g_ref[...] * out_scale).astype(o_ref.dtype)


def _attn_a(proj3, vt, lam, cfar, bias5, subln_g, lam_init):
    B, S, _ = proj3.shape
    t = T_ATT
    nk = S // t
    kern = functools.partial(_attn_a_kernel, out_scale=1.0 - lam_init)
    return pl.pallas_call(
        kern,
        out_shape=jax.ShapeDtypeStruct((B, S, HA * 2 * DA), BF16),
        grid=(B, HA, nk),
        in_specs=[
            pl.BlockSpec(memory_space=pltpu.SMEM),
            pl.BlockSpec(memory_space=pltpu.SMEM),
            pl.BlockSpec((1, t, LANES), lambda b, h, qi: (b, qi, h)),
            pl.BlockSpec((1, S, LANES), lambda b, h, qi: (b, 0, HA + h)),
            pl.BlockSpec((1, 1, nk, LANES, t), lambda b, h, qi: (b, h, 0, 0, 0)),
            pl.BlockSpec((1, 5, t, t), lambda b, h, qi: (h, 0, 0, 0)),
            pl.BlockSpec((1, LANES), lambda b, h, qi: (0, 0)),
        ],
        out_specs=pl.BlockSpec((1, t, LANES), lambda b, h, qi: (b, qi, h)),
        compiler_params=_cparams(("parallel", "parallel", "arbitrary")),
        name="diff_attn",
    )(lam, cfar, proj3, proj3, vt, bias5, subln_g)


def _attn_a_bias(rel_bias):
    t = T_ATT
    tab = rel_bias[:, :HA].astype(F32) * LOG2E
    d = jnp.arange(-1, 2, dtype=jnp.int32)[:, None, None] * t
    rel = d + jnp.arange(t, dtype=jnp.int32)[None, :, None] - jnp.arange(t, dtype=jnp.int32)[None, None, :]
    near = _bias_lookup(_t5_bucket(rel), tab)
    far = tab[_t5_bucket(jnp.array([-(t + 1), t + 1], dtype=jnp.int32))].T
    fill = lambda side: jnp.broadcast_to(far[:, side, None, None, None], (HA, 1, t, t))
    tiles = jnp.concatenate([fill(0), near, fill(1)], axis=1)
    return tiles, far.reshape(2 * HA)


def _attn_b_kernel(q_ref, k_ref, v_ref, bias_ref, o_ref, lse_ref, *, sub_len, r, sp, qp):
    nblk = sub_len // QB_DIL
    low_half = lax.broadcasted_iota(jnp.int32, (1, LANES), 1) < HALF_LANES
    for si in range(sp):
        s = si if sp == r else pl.program_id(2) * sp + si
        for qb in range(qp):
            i = pl.program_id(1) * qp + qb
            start = jnp.clip(i * QB_DIL - HALF_WIN, 0, sub_len - KW_DIL)
            start = pl.multiple_of(start, HALF_WIN)
            variant = jnp.where(i == 0, 0, jnp.where(i == nblk - 1, 2, 1))
            q = q_ref[0, si, qb * QB_DIL:(qb + 1) * QB_DIL, :]
            kw = k_ref[0, s, pl.ds(start, KW_DIL), :]
            vw = v_ref[0, s, pl.ds(start, KW_DIL), :]
            rows = (slice(qb * QB_DIL, (qb + 1) * QB_DIL) if r == 1
                    else pl.ds(s, QB_DIL, stride=r))
            for j in range(HB // 2):
                cols = slice(j * LANES, (j + 1) * LANES)
                qpair, kp, vp = q[:, cols], kw[:, cols], vw[:, cols]
                outs, lses = [], []
                for c in range(2):
                    qc = jnp.where(low_half if c == 0 else jnp.logical_not(low_half), qpair,
                                   jnp.zeros_like(qpair))
                    sc = lax.dot_general(qc, kp, (((1,), (1,)), ((), ())),
                                         preferred_element_type=F32)
                    sc = sc + bias_ref[2 * j + c, variant]
                    m = jnp.max(sc, axis=-1, keepdims=True)
                    p = jnp.exp(sc - m)
                    l = jnp.sum(p, axis=-1, keepdims=True)
                    outs.append(jnp.dot(p.astype(BF16), vp, preferred_element_type=F32) / l)
                    lses.append(m + jnp.log(l))
                o_ref[0, j, rows, :] = jnp.where(low_half, outs[0], outs[1])
                lse_ref[0, j, rows, :] = jnp.where(low_half, lses[0], lses[1])


def _attn_b(qkv4, bias3, g, cols):
    B, r, L, _ = qkv4.shape
    S = r * L
    width = HB * DB
    nblk = L // QB_DIL
    sp = min(r, ITEMS_DIL)
    qp = ITEMS_DIL // sp
    qcol, kcol, vcol = cols
    kern = functools.partial(_attn_b_kernel, sub_len=L, r=r, sp=sp, qp=qp)
    slab = jax.ShapeDtypeStruct((B, N_SLABS, S, LANES), F32)
    slab_spec = pl.BlockSpec((1, N_SLABS, QB_DIL * r * qp, LANES), lambda b, i, s: (b, 0, i, 0))
    return pl.pallas_call(
        kern,
        out_shape=[slab, slab],
        grid=(B, nblk // qp, r // sp),
        in_specs=[
            pl.BlockSpec((1, sp, QB_DIL * qp, width), lambda b, i, s: (b, s, i, qcol)),
            pl.BlockSpec((1, r, L, width), lambda b, i, s: (b, 0, 0, kcol)),
            pl.BlockSpec((1, r, L, width), lambda b, i, s: (b, 0, 0, vcol)),
            pl.BlockSpec((HB, 3, QB_DIL, KW_DIL), lambda b, i, s: (0, 0, 0, 0)),
        ],
        out_specs=[slab_spec, slab_spec],
        compiler_params=_cparams(("parallel", "arbitrary", "arbitrary")),
        name=f"dilated_attn_{g}",
    )(qkv4, qkv4, qkv4, bias3)


def _attn_b_bias(rel_bias, g):
    r = DILATIONS[g]
    tab = rel_bias[:, HA + g * HB: HA + (g + 1) * HB].astype(F32)
    off = jnp.arange(3, dtype=jnp.int32)[:, None, None] * HALF_WIN
    rel = (jnp.arange(KW_DIL, dtype=jnp.int32)[None, None, :] - off
           - jnp.arange(QB_DIL, dtype=jnp.int32)[None, :, None])
    bias = _bias_lookup(_t5_bucket(rel * r), tab)
    return jnp.where((jnp.abs(rel) <= HALF_WIN)[None], bias, NEG)


def _sgu_kernel(zu_ref, zv_ref, lng_ref, lnb_ref, ws_ref, bs_ref, o_ref):
    u = jax.nn.gelu(zu_ref[...].astype(F32))
    v = jax.nn.gelu(zv_ref[...].astype(F32))
    mu = jnp.mean(v, axis=-1, keepdims=True)
    var = jnp.mean(jnp.square(v - mu), axis=-1, keepdims=True)
    v = ((v - mu) * lax.rsqrt(var + EPS) * lng_ref[...] + lnb_ref[...]).astype(BF16)
    gd = v.shape[1] // C_GROUPS
    for n in range(v.shape[0] // CHUNK):
        rows = slice(n * CHUNK, (n + 1) * CHUNK)
        for g in range(C_GROUPS):
            cols = slice(g * gd, (g + 1) * gd)
            mixed = jnp.dot(ws_ref[g], v[rows, cols], preferred_element_type=F32) + bs_ref[:, cols]
            o_ref[rows, cols] = (u[rows, cols] * mixed).astype(o_ref.dtype)


def _sgu(proj2, ln_g, ln_b, w_s, b_exp):
    T = proj2.shape[0]
    tm = min(TM_SGU, T)
    w = MIX_W
    return pl.pallas_call(
        _sgu_kernel,
        out_shape=jax.ShapeDtypeStruct((T, w), BF16),
        grid=(T // tm,),
        in_specs=[pl.BlockSpec((tm, w), lambda i: (i, COL_ZU)),
                  pl.BlockSpec((tm, w), lambda i: (i, COL_ZU + 1)),
                  pl.BlockSpec((1, w), lambda i: (0, 0)),
                  pl.BlockSpec((1, w), lambda i: (0, 0)),
                  pl.BlockSpec((C_GROUPS, CHUNK, CHUNK), lambda i: (0, 0, 0)),
                  pl.BlockSpec((CHUNK, w), lambda i: (0, 0))],
        out_specs=pl.BlockSpec((tm, w), lambda i: (i, 0)),
        compiler_params=_cparams(("parallel",)),
        name="sgu",
    )(proj2, proj2, ln_g, ln_b, w_s, b_exp)


def _route(logits):
    lane = lax.broadcasted_iota(jnp.int32, logits.shape, 1)
    big = jnp.int32(LANES)
    is_grp = (lane >= N_EXPERTS) & (lane < N_EXPERTS + N_GROUPS)
    gl = jnp.where(is_grp, logits, NEG)
    gmax = jnp.max(gl, axis=-1, keepdims=True)
    g_idx = jnp.min(jnp.where(is_grp & (gl == gmax), lane, big), axis=-1, keepdims=True) - N_EXPERTS
    g_w = 1.0 / jnp.sum(jnp.where(is_grp, jnp.exp(gl - gmax), 0.0), axis=-1, keepdims=True)
    in_grp = (lane >= g_idx * E_PER_GROUP) & (lane < (g_idx + 1) * E_PER_GROUP)
    sel = jnp.where(in_grp, logits, NEG)
    v1 = jnp.max(sel, axis=-1, keepdims=True)
    i1 = jnp.min(jnp.where(in_grp & (sel == v1), lane, big), axis=-1, keepdims=True)
    rest = in_grp & (lane != i1)
    sel2 = jnp.where(rest, logits, NEG)
    v2 = jnp.max(sel2, axis=-1, keepdims=True)
    i2 = jnp.min(jnp.where(rest & (sel2 == v2), lane, big), axis=-1, keepdims=True)
    e2 = jnp.exp(v2 - v1)
    w1 = g_w / (1.0 + e2)
    w2 = g_w * e2 / (1.0 + e2)
    return jnp.where(lane == i1, w1, jnp.where(lane == i2, w2, 0.0))


def _mix_kernel(x_ref, ya_ref, ob0_ref, ob1_ref, ob2_ref, ls0_ref, ls1_ref, ls2_ref, yc_ref,
                g0_ref, g1_ref, g2_ref, wb_ref, wo_ref, nf_ref, wr_ref, br_ref,
                xo_ref, h_ref, comb_ref):
    slabs = []
    for j in range(N_SLABS):
        ls0, ls1, ls2 = ls0_ref[0, j], ls1_ref[0, j], ls2_ref[0, j]
        mx = jnp.maximum(jnp.maximum(ls0, ls1), ls2)
        e0, e1, e2 = jnp.exp(ls0 - mx), jnp.exp(ls1 - mx), jnp.exp(ls2 - mx)
        yb = (e0 * ob0_ref[0, j] + e1 * ob1_ref[0, j] + e2 * ob2_ref[0, j]) / (e0 + e1 + e2)
        slabs.append(yb.astype(BF16))
    yb = jnp.concatenate(slabs, axis=-1)
    merged = jax.nn.sigmoid(g0_ref[...].astype(F32)) * jnp.dot(ya_ref[...], wb_ref[0],
                                                               preferred_element_type=F32)
    merged += jax.nn.sigmoid(g1_ref[...].astype(F32)) * jnp.dot(yb, wb_ref[1],
                                                                preferred_element_type=F32)
    merged += jax.nn.sigmoid(g2_ref[...].astype(F32)) * jnp.dot(yc_ref[...], wb_ref[2],
                                                                preferred_element_type=F32)
    xn = x_ref[...] + jnp.dot(merged.astype(BF16), wo_ref[...], preferred_element_type=F32)
    xo_ref[...] = xn
    h = _rms_bf16(xn, nf_ref[...])
    h_ref[...] = h
    logits = jnp.dot(h, wr_ref[...], preferred_element_type=F32) + br_ref[...]
    comb_ref[...] = _route(logits)


def _mix(x2, ya, obs, lses, yc, proj2, wb, wo, nf, wr, br):
    T, D = x2.shape
    S = obs[0].shape[2]
    tm = min(TM_MIX, S)
    per_b = S // tm
    w = MIX_W
    row = lambda width: pl.BlockSpec((tm, width), lambda i: (i, 0))
    full = lambda a: pl.BlockSpec(a.shape, lambda i: (0,) * a.ndim)
    gate = lambda n: pl.BlockSpec((tm, D), lambda i: (i, COL_GATE + n))
    slab = pl.BlockSpec((1, N_SLABS, tm, LANES), lambda i: (i // per_b, 0, i % per_b, 0))
    return pl.pallas_call(
        _mix_kernel,
        out_shape=[jax.ShapeDtypeStruct((T, D), F32), jax.ShapeDtypeStruct((T, D), BF16),
                   jax.ShapeDtypeStruct((T, LANES), F32)],
        grid=(T // tm,),
        in_specs=[row(D), row(w), slab, slab, slab, slab, slab, slab, row(w),
                  gate(0), gate(1), gate(2), full(wb), full(wo), full(nf), full(wr), full(br)],
        out_specs=[row(D), row(D), row(LANES)],
        compiler_params=_cparams(("parallel",)),
        name="mix",
    )(x2, ya, obs[0], obs[1], obs[2], lses[0], lses[1], lses[2], yc, proj2, proj2, proj2,
      wb, wo, nf, wr, br)


def _moe_kernel(h_ref, comb_ref, x_ref, wg_ref, wu_ref, wd_ref, nfin_ref, o_ref, acc_scr,
                *, final_norm):
    e = pl.program_id(1)

    @pl.when(e == 0)
    def _():
        acc_scr[...] = jnp.zeros(acc_scr.shape, F32)

    h = h_ref[...]
    lane = lax.broadcasted_iota(jnp.int32, comb_ref.shape, 1)
    c = jnp.sum(jnp.where(lane == e, comb_ref[...], 0.0), axis=-1, keepdims=True)
    hid = (jax.nn.silu(jnp.dot(h, wg_ref[0].astype(BF16), preferred_element_type=F32))
           * jnp.dot(h, wu_ref[0].astype(BF16), preferred_element_type=F32))
    acc_scr[...] += c * jnp.dot(hid.astype(BF16), wd_ref[0].astype(BF16),
                                preferred_element_type=F32)

    @pl.when(e == pl.num_programs(1) - 1)
    def _():
        xn = x_ref[...] + acc_scr[...]
        if final_norm:
            ms = jnp.mean(xn * xn, axis=-1, keepdims=True)
            xn = xn * lax.rsqrt(ms + EPS) * nfin_ref[...]
        o_ref[...] = xn


def _moe_dense(h, comb, x2, wg, wu, wd, nfin, final_norm):
    T, D = x2.shape
    tm = min(TM_MOE, T)
    F = wg.shape[2]
    kern = functools.partial(_moe_kernel, final_norm=final_norm)
    return pl.pallas_call(
        kern,
        out_shape=jax.ShapeDtypeStruct((T, D), F32),
        grid=(T // tm, N_EXPERTS),
        in_specs=[pl.BlockSpec((tm, D), lambda i, e: (i, 0)),
                  pl.BlockSpec((tm, LANES), lambda i, e: (i, 0)),
                  pl.BlockSpec((tm, D), lambda i, e: (i, 0)),
                  pl.BlockSpec((1, D, F), lambda i, e: (e, 0, 0)),
                  pl.BlockSpec((1, D, F), lambda i, e: (e, 0, 0)),
                  pl.BlockSpec((1, F, D), lambda i, e: (e, 0, 0)),
                  pl.BlockSpec((1, D), lambda i, e: (0, 0))],
        out_specs=pl.BlockSpec((tm, D), lambda i, e: (i, 0)),
        scratch_shapes=[pltpu.VMEM((tm, D), F32)],
        compiler_params=_cparams(("parallel", "arbitrary")),
        name="moe_dense",
    )(h, comb, x2, wg, wu, wd, nfin)


def _moe_dispatch_kernel(h_ref, comb_ref, o_ref, cnt_ref):
    tm = h_ref.shape[0]
    comb = comb_ref[...]
    hi = comb.astype(BF16)
    lo = (comb - hi.astype(F32)).astype(BF16)
    haug = jnp.concatenate([h_ref[...], hi, lo], axis=1)
    a_t = comb.T[:N_EXPERTS] > 0.0
    a_f = jnp.where(a_t, 1.0, 0.0)
    before = (lax.broadcasted_iota(jnp.int32, (tm, tm), 0)
              < lax.broadcasted_iota(jnp.int32, (tm, tm), 1))
    rank_t = jnp.dot(a_f.astype(BF16), jnp.where(before, 1.0, 0.0).astype(BF16),
                     preferred_element_type=F32)
    slot = lax.broadcasted_iota(jnp.int32, (MOE_CAP, tm), 0).astype(F32)
    blocks = [jnp.where((slot == rank_t[e:e + 1]) & a_t[e:e + 1], 1.0, 0.0).astype(BF16)
              for e in range(N_EXPERTS)]
    res = jnp.dot(jnp.concatenate(blocks, axis=0), haug, preferred_element_type=F32)
    res = res.astype(o_ref.dtype)
    for e in range(N_EXPERTS):
        o_ref[0, e] = res[e * MOE_CAP:(e + 1) * MOE_CAP]
    cnt_ref[0] = jnp.broadcast_to(jnp.sum(a_f, axis=1, keepdims=True), cnt_ref.shape[1:])


def _moe_dispatch(h, comb):
    T, D = h.shape
    tm = min(TM_DISP, T)
    n = T // tm
    return pl.pallas_call(
        _moe_dispatch_kernel,
        out_shape=[jax.ShapeDtypeStruct((n, N_EXPERTS, MOE_CAP, D + 2 * LANES), BF16),
                   jax.ShapeDtypeStruct((n, N_EXPERTS, LANES), F32)],
        grid=(n,),
        in_specs=[pl.BlockSpec((tm, D), lambda i: (i, 0)),
                  pl.BlockSpec((tm, LANES), lambda i: (i, 0))],
        out_specs=[pl.BlockSpec((1, N_EXPERTS, MOE_CAP, D + 2 * LANES), lambda i: (i, 0, 0, 0)),
                   pl.BlockSpec((1, N_EXPERTS, LANES), lambda i: (i, 0, 0))],
        compiler_params=_cparams(("parallel",)),
        name="moe_dispatch",
    )(h, comb)


def _moe_ffn_kernel(n16_ref, s_ref, wg_ref, wu_ref, wd_ref, o_ref,
                    wg_scr, wu_scr, wd_scr, lhs_scr, y_scr):
    e, c = pl.program_id(0), pl.program_id(1)

    @pl.when(c == 0)
    def _():
        wg_scr[...] = wg_ref[0].astype(BF16)
        wu_scr[...] = wu_ref[0].astype(BF16)
        wd_scr[...] = wd_ref[0].astype(BF16)

    g, _, cap, _ = s_ref.shape
    D = o_ref.shape[-1]
    @pl.when((e == 0) & (c == 0))
    def _():
        lhs_scr[...] = jnp.zeros(lhs_scr.shape, lhs_scr.dtype)
        y_scr[...] = jnp.zeros(y_scr.shape, y_scr.dtype)

    offs = []
    off = jnp.int32(0)
    for t in range(g):
        offs.append(off)
        lhs_scr[pl.ds(pl.multiple_of(off, BF16_ROWS), cap), :] = s_ref[t, 0]
        off = off + n16_ref[(c * g + t) * N_EXPERTS + e]
    total = off

    def run(nrows):
        rows = lhs_scr[:nrows]
        h = rows[:, :D]
        wparts = rows[:, D:].astype(F32)
        lane = lax.broadcasted_iota(jnp.int32, wparts.shape, 1)
        w = jnp.sum(jnp.where(lane % LANES == e, wparts, 0.0), axis=-1, keepdims=True)
        hid = (jax.nn.silu(jnp.dot(h, wg_scr[...], preferred_element_type=F32))
               * jnp.dot(h, wu_scr[...], preferred_element_type=F32))
        y = w * jnp.dot(hid.astype(BF16), wd_scr[...], preferred_element_type=F32)
        y_scr[:nrows] = y.astype(y_scr.dtype)

    classes = tuple(range(g * cap // 2, g * cap + 1, FFN_ROW_STEP))
    lower = 0
    for nrows in classes:
        pl.when((total > lower) & (total <= nrows))(functools.partial(run, nrows))
        lower = nrows

    for t in range(g):
        o_ref[t, 0] = y_scr[pl.ds(pl.multiple_of(offs[t], BF16_ROWS), cap), :]


def _moe_ffn(srt, n16, wg, wu, wd):
    n, ne, cap, wdt = srt.shape
    D, F = wg.shape[1], wg.shape[2]
    g = math.gcd(G_FFN, n)
    return pl.pallas_call(
        _moe_ffn_kernel,
        out_shape=jax.ShapeDtypeStruct((n, ne, cap, D), BF16),
        grid_spec=pltpu.PrefetchScalarGridSpec(
            num_scalar_prefetch=1,
            grid=(ne, n // g),
            in_specs=[pl.BlockSpec((g, 1, cap, wdt), lambda e, c, n16: (c, e, 0, 0)),
                      pl.BlockSpec((1, D, F), lambda e, c, n16: (e, 0, 0)),
                      pl.BlockSpec((1, D, F), lambda e, c, n16: (e, 0, 0)),
                      pl.BlockSpec((1, F, D), lambda e, c, n16: (e, 0, 0))],
            out_specs=pl.BlockSpec((g, 1, cap, D), lambda e, c, n16: (c, e, 0, 0)),
            scratch_shapes=[pltpu.VMEM((D, F), BF16), pltpu.VMEM((D, F), BF16),
                            pltpu.VMEM((F, D), BF16), pltpu.VMEM((g * cap, wdt), BF16),
                            pltpu.VMEM((g * cap, D), BF16)]),
        compiler_params=_cparams(("arbitrary", "arbitrary")),
        name="moe_ffn",
    )(n16, srt, wg, wu, wd)


def _moe_combine_kernel(y_ref, comb_ref, x_ref, nfin_ref, o_ref, *, final_norm):
    tm = x_ref.shape[0]
    comb = comb_ref[...]
    a = comb > 0.0
    before = (lax.broadcasted_iota(jnp.int32, (tm, tm), 1)
              < lax.broadcasted_iota(jnp.int32, (tm, tm), 0))
    rank = jnp.dot(jnp.where(before, 1.0, 0.0).astype(BF16), jnp.where(a, 1.0, 0.0).astype(BF16),
                   preferred_element_type=F32)
    key = jnp.where(a, rank, -1.0).astype(BF16)
    ncol = N_EXPERTS * MOE_CAP
    spread = (lax.broadcasted_iota(jnp.int32, (LANES, ncol), 1) // MOE_CAP
              == lax.broadcasted_iota(jnp.int32, (LANES, ncol), 0))
    key_all = jnp.dot(key, jnp.where(spread, 1.0, 0.0).astype(BF16), preferred_element_type=F32)
    slot = (lax.broadcasted_iota(jnp.int32, (tm, ncol), 1) % MOE_CAP).astype(F32)
    pc = jnp.where(slot == key_all, 1.0, 0.0).astype(BF16)
    y = jnp.concatenate([y_ref[0, e] for e in range(N_EXPERTS)], axis=0)
    xn = x_ref[...] + jnp.dot(pc, y, preferred_element_type=F32)
    if final_norm:
        ms = jnp.mean(xn * xn, axis=-1, keepdims=True)
        xn = xn * lax.rsqrt(ms + EPS) * nfin_ref[...]
    o_ref[...] = xn


def _moe_combine(y, comb, x2, nfin, final_norm):
    T, D = x2.shape
    n, ne, cap, _ = y.shape
    tm = T // n
    kern = functools.partial(_moe_combine_kernel, final_norm=final_norm)
    return pl.pallas_call(
        kern,
        out_shape=jax.ShapeDtypeStruct((T, D), F32),
        grid=(n,),
        in_specs=[pl.BlockSpec((1, ne, cap, D), lambda i: (i, 0, 0, 0)),
                  pl.BlockSpec((tm, LANES), lambda i: (i, 0)),
                  pl.BlockSpec((tm, D), lambda i: (i, 0)),
                  pl.BlockSpec((1, D), lambda i: (0, 0))],
        out_specs=pl.BlockSpec((tm, D), lambda i: (i, 0)),
        compiler_params=_cparams(("parallel",)),
        name="moe_combine",
    )(y, comb, x2, nfin)


def _moe_fix_kernel(tiles_ref, experts_ref, first_ref, n_ref, h_ref, comb_ref, prev_ref,
                    wg_ref, wu_ref, wd_ref, o_ref):
    del tiles_ref
    s = pl.program_id(0)

    @pl.when(s < n_ref[0])
    def _():
        e = experts_ref[s]
        tm = h_ref.shape[0]
        comb = comb_ref[...]
        a = jnp.where(comb > 0.0, 1.0, 0.0)
        before = (lax.broadcasted_iota(jnp.int32, (tm, tm), 1)
                  < lax.broadcasted_iota(jnp.int32, (tm, tm), 0))
        rank = jnp.dot(jnp.where(before, 1.0, 0.0).astype(BF16), a.astype(BF16),
                       preferred_element_type=F32)
        lane = lax.broadcasted_iota(jnp.int32, comb.shape, 1)
        dropped = (lane == e) & (rank >= MOE_CAP)
        c = jnp.sum(jnp.where(dropped, comb, 0.0), axis=-1, keepdims=True)
        h = h_ref[...]
        hid = (jax.nn.silu(jnp.dot(h, wg_ref[0].astype(BF16), preferred_element_type=F32))
               * jnp.dot(h, wu_ref[0].astype(BF16), preferred_element_type=F32))
        add = c * jnp.dot(hid.astype(BF16), wd_ref[0].astype(BF16), preferred_element_type=F32)
        fresh = first_ref[s] == 1

        @pl.when(fresh)
        def _():
            o_ref[...] = prev_ref[...] + add

        @pl.when(jnp.logical_not(fresh))
        def _():
            o_ref[...] += add


def _moe_fix(tiles, experts, first, n, out, h, comb, wg, wu, wd):
    T, D = out.shape
    tm = min(TM_DISP, T)
    F = wg.shape[2]
    tile = lambda width: pl.BlockSpec((tm, width), lambda s, tl, ex, fi, n: (tl[s], 0))
    wspec = lambda shape: pl.BlockSpec(shape, lambda s, tl, ex, fi, n: (ex[s], 0, 0))
    return pl.pallas_call(
        _moe_fix_kernel,
        out_shape=jax.ShapeDtypeStruct((T, D), F32),
        grid_spec=pltpu.PrefetchScalarGridSpec(
            num_scalar_prefetch=4,
            grid=(MAX_OVF,),
            in_specs=[tile(D), tile(LANES), tile(D), wspec((1, D, F)), wspec((1, D, F)),
                      wspec((1, F, D))],
            out_specs=tile(D)),
        input_output_aliases={6: 0},
        compiler_params=_cparams(("arbitrary",)),
        name="moe_fix",
    )(tiles, experts, first, n, h, comb, out, wg, wu, wd)


def _final_norm_kernel(x_ref, g_ref, o_ref):
    x = x_ref[...]
    ms = jnp.mean(x * x, axis=-1, keepdims=True)
    o_ref[...] = x * lax.rsqrt(ms + EPS) * g_ref[...]


def _final_norm(x2, g):
    T, D = x2.shape
    tm = min(TM_PROJ, T)
    return pl.pallas_call(
        _final_norm_kernel,
        out_shape=jax.ShapeDtypeStruct((T, D), F32),
        grid=(T // tm,),
        in_specs=[pl.BlockSpec((tm, D), lambda i: (i, 0)), pl.BlockSpec((1, D), lambda i: (0, 0))],
        out_specs=pl.BlockSpec((tm, D), lambda i: (i, 0)),
        compiler_params=_cparams(("parallel",)),
        name="final_norm",
    )(x2, g)


def _moe(h, comb, x2, wg, wu, wd, nfin, final_norm):
    srt, cnt = _moe_dispatch(h, comb)
    over = (cnt[:, :, 0] > MOE_CAP).reshape(-1)
    n_ovf = jnp.sum(over.astype(jnp.int32))
    pairs = jnp.nonzero(over, size=MAX_OVF, fill_value=0)[0].astype(jnp.int32)
    pairs = jnp.where(jnp.arange(MAX_OVF) < n_ovf, pairs, pairs[jnp.clip(n_ovf - 1, 0, MAX_OVF - 1)])
    tiles, experts = pairs // N_EXPERTS, pairs % N_EXPERTS
    first = jnp.concatenate([jnp.ones((1,), jnp.int32),
                             (tiles[1:] != tiles[:-1]).astype(jnp.int32)])

    used = jnp.minimum(cnt[:, :, 0], MOE_CAP).astype(jnp.int32).reshape(-1)
    n16 = (used + (BF16_ROWS - 1)) // BF16_ROWS * BF16_ROWS

    def routed():
        y = _moe_ffn(srt, n16, wg, wu, wd)

        def fixed():
            out = _moe_combine(y, comb, x2, nfin, False)
            out = _moe_fix(tiles, experts, first, n_ovf.reshape(1), out, h, comb, wg, wu, wd)
            return _final_norm(out, nfin) if final_norm else out

        return lax.cond(n_ovf > 0, fixed, lambda: _moe_combine(y, comb, x2, nfin, final_norm))

    return lax.cond(n_ovf > MAX_OVF,
                    lambda: _moe_dense(h, comb, x2, wg, wu, wd, nfin, final_norm), routed)


def kernel(x, rel_bias, norm_mix, w_in, diff_lambda, diff_subln, sgu_ln_g, sgu_ln_b, sgu_w, sgu_b,
           w_branch, w_out, norm_ffn, w_router_grp, b_router_grp, w_router_exp, b_router_exp,
           w_gate, w_up, w_down, norm_final):
    B, S, D = x.shape
    T = B * S
    depth = w_in.shape[0]
    a_out = HA * 2 * DA
    grp_w = HB * DB
    b_cols = 3 * NG_B * grp_w
    qkv_b0 = 3 * a_out
    zc0 = qkv_b0 + b_cols
    gate0 = zc0 + 2 * MIX_W
    qk_scale = DA ** -0.5

    bias_a, cfar = _attn_a_bias(rel_bias)
    bias_b = [_attn_b_bias(rel_bias, g) for g in range(NG_B)]

    col = jnp.arange(w_in.shape[2])
    col_scale = jnp.where(col < a_out, qk_scale * LOG2E,
                          jnp.where((col >= qkv_b0) & (col < qkv_b0 + NG_B * grp_w), qk_scale, 1.0))

    def group_cols(w, g):
        return [w[:, qkv_b0 + (c * NG_B + g) * grp_w: qkv_b0 + (c * NG_B + g + 1) * grp_w]
                for c in range(3)]

    x2 = x.reshape(T, D)
    for i in range(depth):
        w = (w_in[i] * col_scale.astype(F32)).astype(BF16)
        nm = norm_mix[i][None, :]
        w_main = jnp.concatenate([w[:, :2 * a_out], w[:, zc0:]] + group_cols(w, 0), axis=1)
        proj2 = _inproj(x2, nm, w_main)
        x3 = x2.reshape(B, S, D)
        vt = _inproj_t(x3, nm, w[:, 2 * a_out:3 * a_out].T)
        proj3 = proj2.reshape(B, S, proj2.shape[1])

        lam_init = 0.8 - 0.6 * math.exp(-0.3 * i)
        lp = diff_lambda[i].astype(F32)
        lam = jnp.exp(jnp.sum(lp[0] * lp[1])) - jnp.exp(jnp.sum(lp[2] * lp[3])) + lam_init
        ya = _attn_a(proj3, vt, lam.reshape(1), cfar, bias_a, diff_subln[i][None, :], lam_init)

        obs, lses = [], []
        for g in range(NG_B):
            r = DILATIONS[g]
            if r == 1:
                qkv4, cols = proj3[:, None], (COL_QKV0, COL_QKV0 + 1, COL_QKV0 + 2)
            else:
                w_g = jnp.concatenate(group_cols(w, g), axis=1)
                qkv4, cols = _inproj_perm(x3, nm, w_g, r), (0, 1, 2)
            o, l = _attn_b(qkv4, bias_b[g], g, cols)
            obs.append(o)
            lses.append(l)

        b_exp = jnp.repeat(sgu_b[i].T, MIX_W // C_GROUPS, axis=1)
        yc = _sgu(proj2, sgu_ln_g[i][None, :], sgu_ln_b[i][None, :], sgu_w[i].astype(BF16), b_exp)

        wr = jnp.concatenate([w_router_exp[i].transpose(1, 0, 2).reshape(D, N_EXPERTS),
                              w_router_grp[i]], axis=1)
        wr = jnp.pad(wr, ((0, 0), (0, LANES - wr.shape[1]))).astype(BF16)
        br = jnp.concatenate([b_router_exp[i].reshape(N_EXPERTS), b_router_grp[i]])
        br = jnp.pad(br, (0, LANES - br.shape[0]))[None, :].astype(F32)

        x2, h, comb = _mix(x2, ya.reshape(T, a_out), obs, lses, yc, proj2,
                           w_branch[i].astype(BF16), w_out[i].astype(BF16), norm_ffn[i][None, :],
                           wr, br)
        x2 = _moe(h, comb, x2, w_gate[i], w_up[i], w_down[i], norm_final[None, :], i == depth - 1)
    return x2.reshape(B, S, D)
```

```python
import functools
import math

import jax
import jax.numpy as jnp
from jax import lax
from jax.experimental import pallas as pl
from jax.experimental.pallas import tpu as pltpu

F32 = jnp.float32
BF16 = jnp.bfloat16

EPS = 1e-6
NEG = -1e30
LOG2E = 1.4426950408889634
LANES = 128
HALF_LANES = LANES // 2
VMEM_LIMIT = 48 * 1024 * 1024

HA = 4
DA = 64
MIX_W = 512
WINDOWS = (128, 512, 2048)
DILATIONS = (1, 4, 16)
NG_B = 3
HB = 8
DB = 64
HALF_WIN = 64
CHUNK = 128
C_GROUPS = 4
N_BRANCH = 3
N_BUCKETS = 32
MAX_DIST = 128
N_GROUPS = 4
E_PER_GROUP = 4
N_EXPERTS = N_GROUPS * E_PER_GROUP
N_SLABS = MIX_W // LANES

TM_PROJ = 1024
TN_PROJ = 3328
TM_PERM = 512
PERM_BLK = 256
T_ATT = 512
QB_DIL = 128
KW_DIL = QB_DIL + 2 * HALF_WIN
ITEMS_DIL = 4
TM_SGU = 512
TM_MIX = 512
TM_MOE = 1024
TM_DISP = 256
MOE_CAP = HALF_LANES
G_FFN = 16
FFN_ROW_STEP = 64
BF16_ROWS = 16
MAX_OVF = 64

COL_ZU = 2
COL_GATE = 2
COL_QKV0 = 10


def _cparams(sem):
    return pltpu.CompilerParams(dimension_semantics=sem, vmem_limit_bytes=VMEM_LIMIT)


def _t5_bucket(rel):
    nb = N_BUCKETS // 2
    max_exact = nb // 2
    ret = (rel > 0).astype(jnp.int32) * nb
    n = jnp.abs(rel)
    nf = jnp.maximum(n, 1).astype(F32)
    large = max_exact + (jnp.log(nf / max_exact) / math.log(MAX_DIST / max_exact)
                         * (nb - max_exact)).astype(jnp.int32)
    large = jnp.minimum(large, nb - 1)
    return ret + jnp.where(n < max_exact, n, large)


def _bias_lookup(bucket, tab):
    out = jnp.zeros((tab.shape[1],) + bucket.shape, F32)
    expand = (slice(None),) + (None,) * bucket.ndim
    for b in range(N_BUCKETS):
        out = jnp.where(bucket[None] == b, tab[b][expand], out)
    return out


def _rms_bf16(x, g):
    ms = jnp.mean(x * x, axis=-1, keepdims=True)
    return (x * lax.rsqrt(ms + EPS) * g).astype(BF16)


def _inproj_kernel(x_ref, g_ref, w_ref, o_ref, h_scr):
    @pl.when(pl.program_id(1) == 0)
    def _():
        h_scr[...] = _rms_bf16(x_ref[...], g_ref[...])

    o_ref[...] = jnp.dot(h_scr[...], w_ref[...], preferred_element_type=F32).astype(o_ref.dtype)


def _inproj(x2, g, w):
    T, D = x2.shape
    N = w.shape[1]
    tm = min(TM_PROJ, T)
    return pl.pallas_call(
        _inproj_kernel,
        out_shape=jax.ShapeDtypeStruct((T, N), BF16),
        grid=(T // tm, N // TN_PROJ),
        in_specs=[pl.BlockSpec((tm, D), lambda i, j: (i, 0)),
                  pl.BlockSpec((1, D), lambda i, j: (0, 0)),
                  pl.BlockSpec((D, TN_PROJ), lambda i, j: (0, j))],
        out_specs=pl.BlockSpec((tm, TN_PROJ), lambda i, j: (i, j)),
        scratch_shapes=[pltpu.VMEM((tm, D), BF16)],
        compiler_params=_cparams(("parallel", "arbitrary")),
        name="inproj",
    )(x2, g, w)


def _inproj_t_kernel(x_ref, g_ref, wt_ref, o_ref):
    h = _rms_bf16(x_ref[0], g_ref[...])
    res = lax.dot_general(wt_ref[...], h, (((1,), (1,)), ((), ())),
                          preferred_element_type=F32).astype(o_ref.dtype)
    for hd in range(o_ref.shape[1]):
        for n in range(o_ref.shape[2]):
            o_ref[0, hd, n] = res[hd * LANES:(hd + 1) * LANES, n * T_ATT:(n + 1) * T_ATT]


def _inproj_t(x3, g, wt):
    B, S, D = x3.shape
    N = wt.shape[0]
    tm = min(TM_PROJ, S)
    nh, nb = N // LANES, tm // T_ATT
    return pl.pallas_call(
        _inproj_t_kernel,
        out_shape=jax.ShapeDtypeStruct((B, nh, S // T_ATT, LANES, T_ATT), BF16),
        grid=(B, S // tm),
        in_specs=[pl.BlockSpec((1, tm, D), lambda b, i: (b, i, 0)),
                  pl.BlockSpec((1, D), lambda b, i: (0, 0)),
                  pl.BlockSpec((N, D), lambda b, i: (0, 0))],
        out_specs=pl.BlockSpec((1, nh, nb, LANES, T_ATT), lambda b, i: (b, 0, i, 0, 0)),
        compiler_params=_cparams(("parallel", "parallel")),
        name="inproj_t",
    )(x3, g, wt)


def _inproj_perm_kernel(x_ref, g_ref, p_ref, w_ref, o_ref, *, r):
    h = _rms_bf16(x_ref[0], g_ref[...])
    nblk = h.shape[0] // PERM_BLK
    hp = jnp.concatenate(
        [jnp.dot(p_ref[...], h[k * PERM_BLK:(k + 1) * PERM_BLK], preferred_element_type=F32)
         for k in range(nblk)], axis=0).astype(BF16)
    res = jnp.dot(hp, w_ref[...], preferred_element_type=F32).astype(o_ref.dtype)
    n = PERM_BLK // r
    for k in range(nblk):
        for s in range(r):
            o_ref[0, s, k * n:(k + 1) * n, :] = res[k * PERM_BLK + s * n:k * PERM_BLK + (s + 1) * n, :]


def _inproj_perm(x3, g, w, r):
    B, S, D = x3.shape
    N = w.shape[1]
    tm = min(TM_PERM, S)
    n = PERM_BLK // r
    o = jnp.arange(PERM_BLK, dtype=jnp.int32)
    src = (o % n) * r + o // n
    perm = (src[:, None] == jnp.arange(PERM_BLK, dtype=jnp.int32)[None, :]).astype(BF16)
    kern = functools.partial(_inproj_perm_kernel, r=r)
    return pl.pallas_call(
        kern,
        out_shape=jax.ShapeDtypeStruct((B, r, S // r, N), BF16),
        grid=(B, S // tm),
        in_specs=[pl.BlockSpec((1, tm, D), lambda b, i: (b, i, 0)),
                  pl.BlockSpec((1, D), lambda b, i: (0, 0)),
                  pl.BlockSpec((PERM_BLK, PERM_BLK), lambda b, i: (0, 0)),
                  pl.BlockSpec((D, N), lambda b, i: (0, 0))],
        out_specs=pl.BlockSpec((1, r, tm // r, N), lambda b, i: (b, 0, i, 0)),
        compiler_params=_cparams(("parallel", "parallel")),
        name=f"inproj_perm_{r}",
    )(x3, g, perm, w)


def _attn_a_kernel(lam_ref, cfar_ref, q_ref, k_ref, vt_ref, bias_ref, g_ref, o_ref,
                   st0_scr, st1_scr, m0_scr, m1_scr, acc_scr, l_scr, *, out_scale, nq, n_blocks):
    k = pl.program_id(0)
    t = T_ATT
    nk = k_ref.shape[1] // t
    n1 = jnp.minimum(k // 2, n_blocks - 1)
    n2 = jnp.maximum(k - 1, 0) // 2
    h1, qi1 = (n1 // nq) % HA, n1 % nq
    h2, qi2 = (n2 // nq) % HA, n2 % nq

    @pl.when(k == 0)
    def _():
        st1_scr[...] = jnp.zeros(st1_scr.shape, F32)
        m1_scr[...] = jnp.zeros(m1_scr.shape, F32)
        acc_scr[...] = jnp.zeros(acc_scr.shape, F32)
        l_scr[...] = jnp.ones(l_scr.shape, F32)

    low_half = lax.broadcasted_iota(jnp.int32, (1, LANES), 1) < HALF_LANES

    def both(cmap, st_w, m_w, st_r, m_r):
        q = q_ref[0]
        zero = jnp.zeros_like(q)
        qc = jnp.where(low_half, q, zero) if cmap == 0 else jnp.where(low_half, zero, q)
        m_prev = m_r[...]
        l = jnp.zeros((1, t), F32)
        acc = jnp.zeros((LANES, t), F32)
        m_new = None
        for j, d in enumerate(range(-1, nk - 1)):
            a1 = lax.rem(qi1 + (d + nk), nk)
            delta1 = a1 - qi1
            kb = k_ref[0, pl.ds(pl.multiple_of(a1 * t, t), t), :]
            st = lax.dot_general(kb, qc, (((1,), (1,)), ((), ())), preferred_element_type=F32)
            if d <= 1:
                st = st + bias_ref[0, jnp.clip(delta1, -2, 2) + 2]
                cm = jnp.max(st, axis=0, keepdims=True)
            else:
                cm = (jnp.max(st, axis=0, keepdims=True)
                      + cfar_ref[2 * h1 + (delta1 > 0).astype(jnp.int32)])
            st_w[j] = st
            m_new = cm if m_new is None else jnp.maximum(m_new, cm)

            a2 = lax.rem(qi2 + (d + nk), nk)
            if d <= 1:
                shifted = m_prev
            else:
                shifted = m_prev - cfar_ref[2 * h2 + (a2 > qi2).astype(jnp.int32)]
            p = jnp.exp2(st_r[j] - shifted)
            l = l + jnp.sum(p, axis=0, keepdims=True)
            acc = acc + jnp.dot(vt_ref[0, 0, a2], p.astype(BF16), preferred_element_type=F32)
        m_w[...] = m_new
        return l, acc

    @pl.when(k % 2 == 0)
    def _():
        l1, acc1 = both(0, st0_scr, m0_scr, st1_scr, m1_scr)
        ot = acc_scr[...] / l_scr[...] - lam_ref[0] * (acc1 / l1)
        o = ot.T
        ms = jnp.mean(o * o, axis=-1, keepdims=True)
        o_ref[0] = (o * lax.rsqrt(ms + EPS) * g_ref[...] * out_scale).astype(o_ref.dtype)

    @pl.when(k % 2 == 1)
    def _():
        l0, acc0 = both(1, st1_scr, m1_scr, st0_scr, m0_scr)
        acc_scr[...] = acc0
        l_scr[...] = l0


def _attn_a(proj3, vt, lam, cfar, bias5, subln_g, lam_init):
    B, S, _ = proj3.shape
    t = T_ATT
    nk = S // t
    n_blocks = B * HA * nk

    def scored(k):
        n = jnp.minimum(k // 2, n_blocks - 1)
        return n // (HA * nk), (n // nk) % HA, n % nk

    def lagged(k, lag):
        n = jnp.maximum(k - lag, 0) // 2
        return n // (HA * nk), (n // nk) % HA, n % nk

    def q_map(k):
        b, h, qi = scored(k)
        return b, qi, h

    def k_map(k):
        b, h, _ = scored(k)
        return b, 0, HA + h

    def vt_map(k):
        b, h, _ = lagged(k, 1)
        return b, h, 0, 0, 0

    def out_map(k):
        b, h, qi = lagged(k, 2)
        return b, qi, h

    kern = functools.partial(_attn_a_kernel, out_scale=1.0 - lam_init, nq=nk, n_blocks=n_blocks)
    return pl.pallas_call(
        kern,
        out_shape=jax.ShapeDtypeStruct((B, S, HA * 2 * DA), BF16),
        grid=(2 * n_blocks + 1,),
        in_specs=[
            pl.BlockSpec(memory_space=pltpu.SMEM),
            pl.BlockSpec(memory_space=pltpu.SMEM),
            pl.BlockSpec((1, t, LANES), q_map),
            pl.BlockSpec((1, S, LANES), k_map),
            pl.BlockSpec((1, 1, nk, LANES, t), vt_map),
            pl.BlockSpec((1, 5, t, t), lambda k: (scored(k)[1], 0, 0, 0)),
            pl.BlockSpec((1, LANES), lambda k: (0, 0)),
        ],
        out_specs=pl.BlockSpec((1, t, LANES), out_map),
        scratch_shapes=[pltpu.VMEM((nk, t, t), F32), pltpu.VMEM((nk, t, t), F32),
                        pltpu.VMEM((1, t), F32), pltpu.VMEM((1, t), F32),
                        pltpu.VMEM((LANES, t), F32), pltpu.VMEM((1, t), F32)],
        compiler_params=_cparams(("arbitrary",)),
        name="diff_attn",
    )(lam, cfar, proj3, proj3, vt, bias5, subln_g)


def _attn_a_bias(rel_bias):
    t = T_ATT
    tab = rel_bias[:, :HA].astype(F32) * LOG2E
    d = jnp.arange(-1, 2, dtype=jnp.int32)[:, None, None] * t
    rel = d + jnp.arange(t, dtype=jnp.int32)[None, :, None] - jnp.arange(t, dtype=jnp.int32)[None, None, :]
    near = _bias_lookup(_t5_bucket(rel), tab)
    far = tab[_t5_bucket(jnp.array([-(t + 1), t + 1], dtype=jnp.int32))].T
    fill = lambda side: jnp.broadcast_to(far[:, side, None, None, None], (HA, 1, t, t))
    tiles = jnp.concatenate([fill(0), near, fill(1)], axis=1)
    return tiles, far.reshape(2 * HA)


def _attn_b_kernel(q_ref, k_ref, v_ref, bias_ref, o_ref, lse_ref, *, sub_len, r, sp, qp):
    nblk = sub_len // QB_DIL
    low_half = lax.broadcasted_iota(jnp.int32, (1, LANES), 1) < HALF_LANES
    for si in range(sp):
        s = si if sp == r else pl.program_id(2) * sp + si
        for qb in range(qp):
            i = pl.program_id(1) * qp + qb
            start = jnp.clip(i * QB_DIL - HALF_WIN, 0, sub_len - KW_DIL)
            start = pl.multiple_of(start, HALF_WIN)
            variant = jnp.where(i == 0, 0, jnp.where(i == nblk - 1, 2, 1))
            q = q_ref[0, si, qb * QB_DIL:(qb + 1) * QB_DIL, :]
            kw = k_ref[0, s, pl.ds(start, KW_DIL), :]
            vw = v_ref[0, s, pl.ds(start, KW_DIL), :]
            rows = (slice(qb * QB_DIL, (qb + 1) * QB_DIL) if r == 1
                    else pl.ds(s, QB_DIL, stride=r))
            for j in range(HB // 2):
                cols = slice(j * LANES, (j + 1) * LANES)
                qpair, kp, vp = q[:, cols], kw[:, cols], vw[:, cols]
                outs, lses = [], []
                for c in range(2):
                    qc = jnp.where(low_half if c == 0 else jnp.logical_not(low_half), qpair,
                                   jnp.zeros_like(qpair))
                    sc = lax.dot_general(qc, kp, (((1,), (1,)), ((), ())),
                                         preferred_element_type=F32)
                    sc = sc + bias_ref[2 * j + c, variant]
                    m = jnp.max(sc, axis=-1, keepdims=True)
                    p = jnp.exp(sc - m)
                    l = jnp.sum(p, axis=-1, keepdims=True)
                    outs.append(jnp.dot(p.astype(BF16), vp, preferred_element_type=F32) / l)
                    lses.append(m + jnp.log(l))
                o_ref[0, j, rows, :] = jnp.where(low_half, outs[0], outs[1])
                lse_ref[0, j, rows, :] = jnp.where(low_half, lses[0], lses[1])


def _attn_b(qkv4, bias3, g, cols):
    B, r, L, _ = qkv4.shape
    S = r * L
    width = HB * DB
    nblk = L // QB_DIL
    sp = min(r, ITEMS_DIL)
    qp = ITEMS_DIL // sp
    qcol, kcol, vcol = cols
    kern = functools.partial(_attn_b_kernel, sub_len=L, r=r, sp=sp, qp=qp)
    slab = jax.ShapeDtypeStruct((B, N_SLABS, S, LANES), F32)
    slab_spec = pl.BlockSpec((1, N_SLABS, QB_DIL * r * qp, LANES), lambda b, i, s: (b, 0, i, 0))
    return pl.pallas_call(
        kern,
        out_shape=[slab, slab],
        grid=(B, nblk // qp, r // sp),
        in_specs=[
            pl.BlockSpec((1, sp, QB_DIL * qp, width), lambda b, i, s: (b, s, i, qcol)),
            pl.BlockSpec((1, r, L, width), lambda b, i, s: (b, 0, 0, kcol)),
            pl.BlockSpec((1, r, L, width), lambda b, i, s: (b, 0, 0, vcol)),
            pl.BlockSpec((HB, 3, QB_DIL, KW_DIL), lambda b, i, s: (0, 0, 0, 0)),
        ],
        out_specs=[slab_spec, slab_spec],
        compiler_params=_cparams(("parallel", "arbitrary", "arbitrary")),
        name=f"dilated_attn_{g}",
    )(qkv4, qkv4, qkv4, bias3)


def _attn_b_bias(rel_bias, g):
    r = DILATIONS[g]
    tab = rel_bias[:, HA + g * HB: HA + (g + 1) * HB].astype(F32)
    off = jnp.arange(3, dtype=jnp.int32)[:, None, None] * HALF_WIN
    rel = (jnp.arange(KW_DIL, dtype=jnp.int32)[None, None, :] - off
           - jnp.arange(QB_DIL, dtype=jnp.int32)[None, :, None])
    bias = _bias_lookup(_t5_bucket(rel * r), tab)
    return jnp.where((jnp.abs(rel) <= HALF_WIN)[None], bias, NEG)


def _sgu_kernel(zu_ref, zv_ref, lng_ref, lnb_ref, ws_ref, bs_ref, o_ref):
    u = jax.nn.gelu(zu_ref[...].astype(F32))
    v = jax.nn.gelu(zv_ref[...].astype(F32))
    mu = jnp.mean(v, axis=-1, keepdims=True)
    var = jnp.mean(jnp.square(v - mu), axis=-1, keepdims=True)
    v = ((v - mu) * lax.rsqrt(var + EPS) * lng_ref[...] + lnb_ref[...]).astype(BF16)
    gd = v.shape[1] // C_GROUPS
    for n in range(v.shape[0] // CHUNK):
        rows = slice(n * CHUNK, (n + 1) * CHUNK)
        for g in range(C_GROUPS):
            cols = slice(g * gd, (g + 1) * gd)
            mixed = jnp.dot(ws_ref[g], v[rows, cols], preferred_element_type=F32) + bs_ref[:, cols]
            o_ref[rows, cols] = (u[rows, cols] * mixed).astype(o_ref.dtype)


def _sgu(proj2, ln_g, ln_b, w_s, b_exp):
    T = proj2.shape[0]
    tm = min(TM_SGU, T)
    w = MIX_W
    return pl.pallas_call(
        _sgu_kernel,
        out_shape=jax.ShapeDtypeStruct((T, w), BF16),
        grid=(T // tm,),
        in_specs=[pl.BlockSpec((tm, w), lambda i: (i, COL_ZU)),
                  pl.BlockSpec((tm, w), lambda i: (i, COL_ZU + 1)),
                  pl.BlockSpec((1, w), lambda i: (0, 0)),
                  pl.BlockSpec((1, w), lambda i: (0, 0)),
                  pl.BlockSpec((C_GROUPS, CHUNK, CHUNK), lambda i: (0, 0, 0)),
                  pl.BlockSpec((CHUNK, w), lambda i: (0, 0))],
        out_specs=pl.BlockSpec((tm, w), lambda i: (i, 0)),
        compiler_params=_cparams(("parallel",)),
        name="sgu",
    )(proj2, proj2, ln_g, ln_b, w_s, b_exp)


def _route(logits):
    lane = lax.broadcasted_iota(jnp.int32, logits.shape, 1)
    big = jnp.int32(LANES)
    is_grp = (lane >= N_EXPERTS) & (lane < N_EXPERTS + N_GROUPS)
    gl = jnp.where(is_grp, logits, NEG)
    gmax = jnp.max(gl, axis=-1, keepdims=True)
    g_idx = jnp.min(jnp.where(is_grp & (gl == gmax), lane, big), axis=-1, keepdims=True) - N_EXPERTS
    g_w = 1.0 / jnp.sum(jnp.where(is_grp, jnp.exp(gl - gmax), 0.0), axis=-1, keepdims=True)
    in_grp = (lane >= g_idx * E_PER_GROUP) & (lane < (g_idx + 1) * E_PER_GROUP)
    sel = jnp.where(in_grp, logits, NEG)
    v1 = jnp.max(sel, axis=-1, keepdims=True)
    i1 = jnp.min(jnp.where(in_grp & (sel == v1), lane, big), axis=-1, keepdims=True)
    rest = in_grp & (lane != i1)
    sel2 = jnp.where(rest, logits, NEG)
    v2 = jnp.max(sel2, axis=-1, keepdims=True)
    i2 = jnp.min(jnp.where(rest & (sel2 == v2), lane, big), axis=-1, keepdims=True)
    e2 = jnp.exp(v2 - v1)
    w1 = g_w / (1.0 + e2)
    w2 = g_w * e2 / (1.0 + e2)
    return jnp.where(lane == i1, w1, jnp.where(lane == i2, w2, 0.0))


def _mix_kernel(x_ref, ya_ref, ob0_ref, ob1_ref, ob2_ref, ls0_ref, ls1_ref, ls2_ref, yc_ref,
                g0_ref, g1_ref, g2_ref, wb_ref, wo_ref, nf_ref, wr_ref, br_ref,
                xo_ref, h_ref, comb_ref):
    slabs = []
    for j in range(N_SLABS):
        ls0, ls1, ls2 = ls0_ref[0, j], ls1_ref[0, j], ls2_ref[0, j]
        mx = jnp.maximum(jnp.maximum(ls0, ls1), ls2)
        e0, e1, e2 = jnp.exp(ls0 - mx), jnp.exp(ls1 - mx), jnp.exp(ls2 - mx)
        yb = (e0 * ob0_ref[0, j] + e1 * ob1_ref[0, j] + e2 * ob2_ref[0, j]) / (e0 + e1 + e2)
        slabs.append(yb.astype(BF16))
    yb = jnp.concatenate(slabs, axis=-1)
    merged = jax.nn.sigmoid(g0_ref[...].astype(F32)) * jnp.dot(ya_ref[...], wb_ref[0],
                                                               preferred_element_type=F32)
    merged += jax.nn.sigmoid(g1_ref[...].astype(F32)) * jnp.dot(yb, wb_ref[1],
                                                                preferred_element_type=F32)
    merged += jax.nn.sigmoid(g2_ref[...].astype(F32)) * jnp.dot(yc_ref[...], wb_ref[2],
                                                                preferred_element_type=F32)
    xn = x_ref[...] + jnp.dot(merged.astype(BF16), wo_ref[...], preferred_element_type=F32)
    xo_ref[...] = xn
    h = _rms_bf16(xn, nf_ref[...])
    h_ref[...] = h
    logits = jnp.dot(h, wr_ref[...], preferred_element_type=F32) + br_ref[...]
    comb_ref[...] = _route(logits)


def _mix(x2, ya, obs, lses, yc, proj2, wb, wo, nf, wr, br):
    T, D = x2.shape
    S = obs[0].shape[2]
    tm = min(TM_MIX, S)
    per_b = S // tm
    w = MIX_W
    row = lambda width: pl.BlockSpec((tm, width), lambda i: (i, 0))
    full = lambda a: pl.BlockSpec(a.shape, lambda i: (0,) * a.ndim)
    gate = lambda n: pl.BlockSpec((tm, D), lambda i: (i, COL_GATE + n))
    slab = pl.BlockSpec((1, N_SLABS, tm, LANES), lambda i: (i // per_b, 0, i % per_b, 0))
    return pl.pallas_call(
        _mix_kernel,
        out_shape=[jax.ShapeDtypeStruct((T, D), F32), jax.ShapeDtypeStruct((T, D), BF16),
                   jax.ShapeDtypeStruct((T, LANES), F32)],
        grid=(T // tm,),
        in_specs=[row(D), row(w), slab, slab, slab, slab, slab, slab, row(w),
                  gate(0), gate(1), gate(2), full(wb), full(wo), full(nf), full(wr), full(br)],
        out_specs=[row(D), row(D), row(LANES)],
        compiler_params=_cparams(("parallel",)),
        name="mix",
    )(x2, ya, obs[0], obs[1], obs[2], lses[0], lses[1], lses[2], yc, proj2, proj2, proj2,
      wb, wo, nf, wr, br)


def _moe_kernel(h_ref, comb_ref, x_ref, wg_ref, wu_ref, wd_ref, nfin_ref, o_ref, acc_scr,
                *, final_norm):
    e = pl.program_id(1)

    @pl.when(e == 0)
    def _():
        acc_scr[...] = jnp.zeros(acc_scr.shape, F32)

    h = h_ref[...]
    lane = lax.broadcasted_iota(jnp.int32, comb_ref.shape, 1)
    c = jnp.sum(jnp.where(lane == e, comb_ref[...], 0.0), axis=-1, keepdims=True)
    hid = (jax.nn.silu(jnp.dot(h, wg_ref[0].astype(BF16), preferred_element_type=F32))
           * jnp.dot(h, wu_ref[0].astype(BF16), preferred_element_type=F32))
    acc_scr[...] += c * jnp.dot(hid.astype(BF16), wd_ref[0].astype(BF16),
                                preferred_element_type=F32)

    @pl.when(e == pl.num_programs(1) - 1)
    def _():
        xn = x_ref[...] + acc_scr[...]
        if final_norm:
            ms = jnp.mean(xn * xn, axis=-1, keepdims=True)
            xn = xn * lax.rsqrt(ms + EPS) * nfin_ref[...]
        o_ref[...] = xn


def _moe_dense(h, comb, x2, wg, wu, wd, nfin, final_norm):
    T, D = x2.shape
    tm = min(TM_MOE, T)
    F = wg.shape[2]
    kern = functools.partial(_moe_kernel, final_norm=final_norm)
    return pl.pallas_call(
        kern,
        out_shape=jax.ShapeDtypeStruct((T, D), F32),
        grid=(T // tm, N_EXPERTS),
        in_specs=[pl.BlockSpec((tm, D), lambda i, e: (i, 0)),
                  pl.BlockSpec((tm, LANES), lambda i, e: (i, 0)),
                  pl.BlockSpec((tm, D), lambda i, e: (i, 0)),
                  pl.BlockSpec((1, D, F), lambda i, e: (e, 0, 0)),
                  pl.BlockSpec((1, D, F), lambda i, e: (e, 0, 0)),
                  pl.BlockSpec((1, F, D), lambda i, e: (e, 0, 0)),
                  pl.BlockSpec((1, D), lambda i, e: (0, 0))],
        out_specs=pl.BlockSpec((tm, D), lambda i, e: (i, 0)),
        scratch_shapes=[pltpu.VMEM((tm, D), F32)],
        compiler_params=_cparams(("parallel", "arbitrary")),
        name="moe_dense",
    )(h, comb, x2, wg, wu, wd, nfin)


def _moe_dispatch_kernel(h_ref, comb_ref, o_ref, cnt_ref):
    tm = h_ref.shape[0]
    comb = comb_ref[...]
    hi = comb.astype(BF16)
    lo = (comb - hi.astype(F32)).astype(BF16)
    haug = jnp.concatenate([h_ref[...], hi, lo], axis=1)
    a_t = comb.T[:N_EXPERTS] > 0.0
    a_f = jnp.where(a_t, 1.0, 0.0)
    before = (lax.broadcasted_iota(jnp.int32, (tm, tm), 0)
              < lax.broadcasted_iota(jnp.int32, (tm, tm), 1))
    rank_t = jnp.dot(a_f.astype(BF16), jnp.where(before, 1.0, 0.0).astype(BF16),
                     preferred_element_type=F32)
    slot = lax.broadcasted_iota(jnp.int32, (MOE_CAP, tm), 0).astype(F32)
    blocks = [jnp.where((slot == rank_t[e:e + 1]) & a_t[e:e + 1], 1.0, 0.0).astype(BF16)
              for e in range(N_EXPERTS)]
    res = jnp.dot(jnp.concatenate(blocks, axis=0), haug, preferred_element_type=F32)
    res = res.astype(o_ref.dtype)
    for e in range(N_EXPERTS):
        o_ref[0, e] = res[e * MOE_CAP:(e + 1) * MOE_CAP]
    cnt_ref[0] = jnp.broadcast_to(jnp.sum(a_f, axis=1, keepdims=True), cnt_ref.shape[1:])


def _moe_dispatch(h, comb):
    T, D = h.shape
    tm = min(TM_DISP, T)
    n = T // tm
    return pl.pallas_call(
        _moe_dispatch_kernel,
        out_shape=[jax.ShapeDtypeStruct((n, N_EXPERTS, MOE_CAP, D + 2 * LANES), BF16),
                   jax.ShapeDtypeStruct((n, N_EXPERTS, LANES), F32)],
        grid=(n,),
        in_specs=[pl.BlockSpec((tm, D), lambda i: (i, 0)),
                  pl.BlockSpec((tm, LANES), lambda i: (i, 0))],
        out_specs=[pl.BlockSpec((1, N_EXPERTS, MOE_CAP, D + 2 * LANES), lambda i: (i, 0, 0, 0)),
                   pl.BlockSpec((1, N_EXPERTS, LANES), lambda i: (i, 0, 0))],
        compiler_params=_cparams(("parallel",)),
        name="moe_dispatch",
    )(h, comb)


def _moe_ffn_kernel(n16_ref, s_ref, wg_ref, wu_ref, wd_ref, o_ref,
                    wg_scr, wu_scr, wd_scr, lhs_scr, y_scr):
    e, c = pl.program_id(0), pl.program_id(1)

    @pl.when(c == 0)
    def _():
        wg_scr[...] = wg_ref[0].astype(BF16)
        wu_scr[...] = wu_ref[0].astype(BF16)
        wd_scr[...] = wd_ref[0].astype(BF16)

    g, _, cap, _ = s_ref.shape
    D = o_ref.shape[-1]
    @pl.when((e == 0) & (c == 0))
    def _():
        lhs_scr[...] = jnp.zeros(lhs_scr.shape, lhs_scr.dtype)
        y_scr[...] = jnp.zeros(y_scr.shape, y_scr.dtype)

    offs = []
    off = jnp.int32(0)
    for t in range(g):
        offs.append(off)
        lhs_scr[pl.ds(pl.multiple_of(off, BF16_ROWS), cap), :] = s_ref[t, 0]
        off = off + n16_ref[(c * g + t) * N_EXPERTS + e]
    total = off

    def run(nrows):
        rows = lhs_scr[:nrows]
        h = rows[:, :D]
        wparts = rows[:, D:].astype(F32)
        lane = lax.broadcasted_iota(jnp.int32, wparts.shape, 1)
        w = jnp.sum(jnp.where(lane % LANES == e, wparts, 0.0), axis=-1, keepdims=True)
        hid = (jax.nn.silu(jnp.dot(h, wg_scr[...], preferred_element_type=F32))
               * jnp.dot(h, wu_scr[...], preferred_element_type=F32))
        y = w * jnp.dot(hid.astype(BF16), wd_scr[...], preferred_element_type=F32)
        y_scr[:nrows] = y.astype(y_scr.dtype)

    classes = tuple(range(g * cap // 2, g * cap + 1, FFN_ROW_STEP))
    lower = 0
    for nrows in classes:
        pl.when((total > lower) & (total <= nrows))(functools.partial(run, nrows))
        lower = nrows

    for t in range(g):
        o_ref[t, 0] = y_scr[pl.ds(pl.multiple_of(offs[t], BF16_ROWS), cap), :]


def _moe_ffn(srt, n16, wg, wu, wd):
    n, ne, cap, wdt = srt.shape
    D, F = wg.shape[1], wg.shape[2]
    g = math.gcd(G_FFN, n)
    return pl.pallas_call(
        _moe_ffn_kernel,
        out_shape=jax.ShapeDtypeStruct((n, ne, cap, D), BF16),
        grid_spec=pltpu.PrefetchScalarGridSpec(
            num_scalar_prefetch=1,
            grid=(ne, n // g),
            in_specs=[pl.BlockSpec((g, 1, cap, wdt), lambda e, c, n16: (c, e, 0, 0)),
                      pl.BlockSpec((1, D, F), lambda e, c, n16: (e, 0, 0)),
                      pl.BlockSpec((1, D, F), lambda e, c, n16: (e, 0, 0)),
                      pl.BlockSpec((1, F, D), lambda e, c, n16: (e, 0, 0))],
            out_specs=pl.BlockSpec((g, 1, cap, D), lambda e, c, n16: (c, e, 0, 0)),
            scratch_shapes=[pltpu.VMEM((D, F), BF16), pltpu.VMEM((D, F), BF16),
                            pltpu.VMEM((F, D), BF16), pltpu.VMEM((g * cap, wdt), BF16),
                            pltpu.VMEM((g * cap, D), BF16)]),
        compiler_params=_cparams(("arbitrary", "arbitrary")),
        name="moe_ffn",
    )(n16, srt, wg, wu, wd)


def _moe_combine_kernel(y_ref, comb_ref, x_ref, nfin_ref, o_ref, *, final_norm):
    tm = x_ref.shape[0]
    comb = comb_ref[...]
    a = comb > 0.0
    before = (lax.broadcasted_iota(jnp.int32, (tm, tm), 1)
              < lax.broadcasted_iota(jnp.int32, (tm, tm), 0))
    rank = jnp.dot(jnp.where(before, 1.0, 0.0).astype(BF16), jnp.where(a, 1.0, 0.0).astype(BF16),
                   preferred_element_type=F32)
    key = jnp.where(a, rank, -1.0).astype(BF16)
    ncol = N_EXPERTS * MOE_CAP
    spread = (lax.broadcasted_iota(jnp.int32, (LANES, ncol), 1) // MOE_CAP
              == lax.broadcasted_iota(jnp.int32, (LANES, ncol), 0))
    key_all = jnp.dot(key, jnp.where(spread, 1.0, 0.0).astype(BF16), preferred_element_type=F32)
    slot = (lax.broadcasted_iota(jnp.int32, (tm, ncol), 1) % MOE_CAP).astype(F32)
    pc = jnp.where(slot == key_all, 1.0, 0.0).astype(BF16)
    y = jnp.concatenate([y_ref[0, e] for e in range(N_EXPERTS)], axis=0)
    xn = x_ref[...] + jnp.dot(pc, y, preferred_element_type=F32)
    if final_norm:
        ms = jnp.mean(xn * xn, axis=-1, keepdims=True)
        xn = xn * lax.rsqrt(ms + EPS) * nfin_ref[...]
    o_ref[...] = xn


def _moe_combine(y, comb, x2, nfin, final_norm):
    T, D = x2.shape
    n, ne, cap, _ = y.shape
    tm = T // n
    kern = functools.partial(_moe_combine_kernel, final_norm=final_norm)
    return pl.pallas_call(
        kern,
        out_shape=jax.ShapeDtypeStruct((T, D), F32),
        grid=(n,),
        in_specs=[pl.BlockSpec((1, ne, cap, D), lambda i: (i, 0, 0, 0)),
                  pl.BlockSpec((tm, LANES), lambda i: (i, 0)),
                  pl.BlockSpec((tm, D), lambda i: (i, 0)),
                  pl.BlockSpec((1, D), lambda i: (0, 0))],
        out_specs=pl.BlockSpec((tm, D), lambda i: (i, 0)),
        compiler_params=_cparams(("parallel",)),
        name="moe_combine",
    )(y, comb, x2, nfin)


def _moe_fix_kernel(tiles_ref, experts_ref, first_ref, n_ref, h_ref, comb_ref, prev_ref,
                    wg_ref, wu_ref, wd_ref, o_ref):
    del tiles_ref
    s = pl.program_id(0)

    @pl.when(s < n_ref[0])
    def _():
        e = experts_ref[s]
        tm = h_ref.shape[0]
        comb = comb_ref[...]
        a = jnp.where(comb > 0.0, 1.0, 0.0)
        before = (lax.broadcasted_iota(jnp.int32, (tm, tm), 1)
                  < lax.broadcasted_iota(jnp.int32, (tm, tm), 0))
        rank = jnp.dot(jnp.where(before, 1.0, 0.0).astype(BF16), a.astype(BF16),
                       preferred_element_type=F32)
        lane = lax.broadcasted_iota(jnp.int32, comb.shape, 1)
        dropped = (lane == e) & (rank >= MOE_CAP)
        c = jnp.sum(jnp.where(dropped, comb, 0.0), axis=-1, keepdims=True)
        h = h_ref[...]
        hid = (jax.nn.silu(jnp.dot(h, wg_ref[0].astype(BF16), preferred_element_type=F32))
               * jnp.dot(h, wu_ref[0].astype(BF16), preferred_element_type=F32))
        add = c * jnp.dot(hid.astype(BF16), wd_ref[0].astype(BF16), preferred_element_type=F32)
        fresh = first_ref[s] == 1

        @pl.when(fresh)
        def _():
            o_ref[...] = prev_ref[...] + add

        @pl.when(jnp.logical_not(fresh))
        def _():
            o_ref[...] += add


def _moe_fix(tiles, experts, first, n, out, h, comb, wg, wu, wd):
    T, D = out.shape
    tm = min(TM_DISP, T)
    F = wg.shape[2]
    tile = lambda width: pl.BlockSpec((tm, width), lambda s, tl, ex, fi, n: (tl[s], 0))
    wspec = lambda shape: pl.BlockSpec(shape, lambda s, tl, ex, fi, n: (ex[s], 0, 0))
    return pl.pallas_call(
        _moe_fix_kernel,
        out_shape=jax.ShapeDtypeStruct((T, D), F32),
        grid_spec=pltpu.PrefetchScalarGridSpec(
            num_scalar_prefetch=4,
            grid=(MAX_OVF,),
            in_specs=[tile(D), tile(LANES), tile(D), wspec((1, D, F)), wspec((1, D, F)),
                      wspec((1, F, D))],
            out_specs=tile(D)),
        input_output_aliases={6: 0},
        compiler_params=_cparams(("arbitrary",)),
        name="moe_fix",
    )(tiles, experts, first, n, h, comb, out, wg, wu, wd)


def _final_norm_kernel(x_ref, g_ref, o_ref):
    x = x_ref[...]
    ms = jnp.mean(x * x, axis=-1, keepdims=True)
    o_ref[...] = x * lax.rsqrt(ms + EPS) * g_ref[...]


def _final_norm(x2, g):
    T, D = x2.shape
    tm = min(TM_PROJ, T)
    return pl.pallas_call(
        _final_norm_kernel,
        out_shape=jax.ShapeDtypeStruct((T, D), F32),
        grid=(T // tm,),
        in_specs=[pl.BlockSpec((tm, D), lambda i: (i, 0)), pl.BlockSpec((1, D), lambda i: (0, 0))],
        out_specs=pl.BlockSpec((tm, D), lambda i: (i, 0)),
        compiler_params=_cparams(("parallel",)),
        name="final_norm",
    )(x2, g)


def _moe(h, comb, x2, wg, wu, wd, nfin, final_norm):
    srt, cnt = _moe_dispatch(h, comb)
    over = (cnt[:, :, 0] > MOE_CAP).reshape(-1)
    n_ovf = jnp.sum(over.astype(jnp.int32))
    pairs = jnp.nonzero(over, size=MAX_OVF, fill_value=0)[0].astype(jnp.int32)
    pairs = jnp.where(jnp.arange(MAX_OVF) < n_ovf, pairs, pairs[jnp.clip(n_ovf - 1, 0, MAX_OVF - 1)])
    tiles, experts = pairs // N_EXPERTS, pairs % N_EXPERTS
    first = jnp.concatenate([jnp.ones((1,), jnp.int32),
                             (tiles[1:] != tiles[:-1]).astype(jnp.int32)])

    used = jnp.minimum(cnt[:, :, 0], MOE_CAP).astype(jnp.int32).reshape(-1)
    n16 = (used + (BF16_ROWS - 1)) // BF16_ROWS * BF16_ROWS

    def routed():
        y = _moe_ffn(srt, n16, wg, wu, wd)

        def fixed():
            out = _moe_combine(y, comb, x2, nfin, False)
            out = _moe_fix(tiles, experts, first, n_ovf.reshape(1), out, h, comb, wg, wu, wd)
            return _final_norm(out, nfin) if final_norm else out

        return lax.cond(n_ovf > 0, fixed, lambda: _moe_combine(y, comb, x2, nfin, final_norm))

    return lax.cond(n_ovf > MAX_OVF,
                    lambda: _moe_dense(h, comb, x2, wg, wu, wd, nfin, final_norm), routed)


def kernel(x, rel_bias, norm_mix, w_in, diff_lambda, diff_subln, sgu_ln_g, sgu_ln_b, sgu_w, sgu_b,
           w_branch, w_out, norm_ffn, w_router_grp, b_router_grp, w_router_exp, b_router_exp,
           w_gate, w_up, w_down, norm_final):
    B, S, D = x.shape
    T = B * S
    depth = w_in.shape[0]
    a_out = HA * 2 * DA
    grp_w = HB * DB
    b_cols = 3 * NG_B * grp_w
    qkv_b0 = 3 * a_out
    zc0 = qkv_b0 + b_cols
    gate0 = zc0 + 2 * MIX_W
    qk_scale = DA ** -0.5

    bias_a, cfar = _attn_a_bias(rel_bias)
    bias_b = [_attn_b_bias(rel_bias, g) for g in range(NG_B)]

    col = jnp.arange(w_in.shape[2])
    col_scale = jnp.where(col < a_out, qk_scale * LOG2E,
                          jnp.where((col >= qkv_b0) & (col < qkv_b0 + NG_B * grp_w), qk_scale, 1.0))

    def group_cols(w, g):
        return [w[:, qkv_b0 + (c * NG_B + g) * grp_w: qkv_b0 + (c * NG_B + g + 1) * grp_w]
                for c in range(3)]

    x2 = x.reshape(T, D)
    for i in range(depth):
        w = (w_in[i] * col_scale.astype(F32)).astype(BF16)
        nm = norm_mix[i][None, :]
        w_main = jnp.concatenate([w[:, :2 * a_out], w[:, zc0:]] + group_cols(w, 0), axis=1)
        proj2 = _inproj(x2, nm, w_main)
        x3 = x2.reshape(B, S, D)
        vt = _inproj_t(x3, nm, w[:, 2 * a_out:3 * a_out].T)
        proj3 = proj2.reshape(B, S, proj2.shape[1])

        lam_init = 0.8 - 0.6 * math.exp(-0.3 * i)
        lp = diff_lambda[i].astype(F32)
        lam = jnp.exp(jnp.sum(lp[0] * lp[1])) - jnp.exp(jnp.sum(lp[2] * lp[3])) + lam_init
        ya = _attn_a(proj3, vt, lam.reshape(1), cfar, bias_a, diff_subln[i][None, :], lam_init)

        obs, lses = [], []
        for g in range(NG_B):
            r = DILATIONS[g]
            if r == 1:
                qkv4, cols = proj3[:, None], (COL_QKV0, COL_QKV0 + 1, COL_QKV0 + 2)
            else:
                w_g = jnp.concatenate(group_cols(w, g), axis=1)
                qkv4, cols = _inproj_perm(x3, nm, w_g, r), (0, 1, 2)
            o, l = _attn_b(qkv4, bias_b[g], g, cols)
            obs.append(o)
            lses.append(l)

        b_exp = jnp.repeat(sgu_b[i].T, MIX_W // C_GROUPS, axis=1)
        yc = _sgu(proj2, sgu_ln_g[i][None, :], sgu_ln_b[i][None, :], sgu_w[i].astype(BF16), b_exp)

        wr = jnp.concatenate([w_router_exp[i].transpose(1, 0, 2).reshape(D, N_EXPERTS),
                              w_router_grp[i]], axis=1)
        wr = jnp.pad(wr, ((0, 0), (0, LANES - wr.shape[1]))).astype(BF16)
        br = jnp.concatenate([b_router_exp[i].reshape(N_EXPERTS), b_router_grp[i]])
        br = jnp.pad(br, (0, LANES - br.shape[0]))[None, :].astype(F32)

        x2, h, comb = _mix(x2, ya.reshape(T, a_out), obs, lses, yc, proj2,
                           w_branch[i].astype(BF16), w_out[i].astype(BF16), norm_ffn[i][None, :],
                           wr, br)
        x2 = _moe(h, comb, x2, w_gate[i], w_up[i], w_down[i], norm_final[None, :], i == depth - 1)
    return x2.reshape(B, S, D)
```

```python
import functools
import math

import jax
import jax.numpy as jnp
from jax import lax
from jax.experimental import pallas as pl
from jax.experimental.pallas import tpu as pltpu

F32 = jnp.float32
BF16 = jnp.bfloat16

EPS = 1e-6
NEG = -1e30
LOG2E = 1.4426950408889634
LANES = 128
HALF_LANES = LANES // 2
VMEM_LIMIT = 48 * 1024 * 1024

HA = 4
DA = 64
MIX_W = 512
WINDOWS = (128, 512, 2048)
DILATIONS = (1, 4, 16)
NG_B = 3
HB = 8
DB = 64
HALF_WIN = 64
CHUNK = 128
C_GROUPS = 4
N_BRANCH = 3
N_BUCKETS = 32
MAX_DIST = 128
N_GROUPS = 4
E_PER_GROUP = 4
N_EXPERTS = N_GROUPS * E_PER_GROUP
N_SLABS = MIX_W // LANES

TM_PROJ = 1024
TN_PROJ = 3328
TM_PERM = 512
PERM_BLK = 256
T_ATT = 512
QB_DIL = 128
KW_DIL = QB_DIL + 2 * HALF_WIN
ITEMS_DIL = 4
TM_SGU = 512
TM_MIX = 512
TM_MOE = 1024
TM_DISP = 256
MOE_CAP = HALF_LANES
G_FFN = 16
TILES_PER_STEP = 4
FFN_ROW_STEP = 64
BF16_ROWS = 16
MAX_OVF = 64

COL_ZU = 2
COL_GATE = 2
COL_QKV0 = 10


def _cparams(sem):
    return pltpu.CompilerParams(dimension_semantics=sem, vmem_limit_bytes=VMEM_LIMIT)


def _t5_bucket(rel):
    nb = N_BUCKETS // 2
    max_exact = nb // 2
    ret = (rel > 0).astype(jnp.int32) * nb
    n = jnp.abs(rel)
    nf = jnp.maximum(n, 1).astype(F32)
    large = max_exact + (jnp.log(nf / max_exact) / math.log(MAX_DIST / max_exact)
                         * (nb - max_exact)).astype(jnp.int32)
    large = jnp.minimum(large, nb - 1)
    return ret + jnp.where(n < max_exact, n, large)


def _bias_lookup(bucket, tab):
    out = jnp.zeros((tab.shape[1],) + bucket.shape, F32)
    expand = (slice(None),) + (None,) * bucket.ndim
    for b in range(N_BUCKETS):
        out = jnp.where(bucket[None] == b, tab[b][expand], out)
    return out


def _rms_bf16(x, g):
    ms = jnp.mean(x * x, axis=-1, keepdims=True)
    return (x * lax.rsqrt(ms + EPS) * g).astype(BF16)


def _inproj_kernel(x_ref, g_ref, w_ref, o_ref, h_scr):
    @pl.when(pl.program_id(1) == 0)
    def _():
        h_scr[...] = _rms_bf16(x_ref[...], g_ref[...])

    o_ref[...] = jnp.dot(h_scr[...], w_ref[...], preferred_element_type=F32).astype(o_ref.dtype)


def _inproj(x2, g, w):
    T, D = x2.shape
    N = w.shape[1]
    tm = min(TM_PROJ, T)
    return pl.pallas_call(
        _inproj_kernel,
        out_shape=jax.ShapeDtypeStruct((T, N), BF16),
        grid=(T // tm, N // TN_PROJ),
        in_specs=[pl.BlockSpec((tm, D), lambda i, j: (i, 0)),
                  pl.BlockSpec((1, D), lambda i, j: (0, 0)),
                  pl.BlockSpec((D, TN_PROJ), lambda i, j: (0, j))],
        out_specs=pl.BlockSpec((tm, TN_PROJ), lambda i, j: (i, j)),
        scratch_shapes=[pltpu.VMEM((tm, D), BF16)],
        compiler_params=_cparams(("parallel", "arbitrary")),
        name="inproj",
    )(x2, g, w)


def _inproj_t_kernel(x_ref, g_ref, wt_ref, o_ref):
    h = _rms_bf16(x_ref[0], g_ref[...])
    res = lax.dot_general(wt_ref[...], h, (((1,), (1,)), ((), ())),
                          preferred_element_type=F32).astype(o_ref.dtype)
    for hd in range(o_ref.shape[1]):
        for n in range(o_ref.shape[2]):
            o_ref[0, hd, n] = res[hd * LANES:(hd + 1) * LANES, n * T_ATT:(n + 1) * T_ATT]


def _inproj_t(x3, g, wt):
    B, S, D = x3.shape
    N = wt.shape[0]
    tm = min(TM_PROJ, S)
    nh, nb = N // LANES, tm // T_ATT
    return pl.pallas_call(
        _inproj_t_kernel,
        out_shape=jax.ShapeDtypeStruct((B, nh, S // T_ATT, LANES, T_ATT), BF16),
        grid=(B, S // tm),
        in_specs=[pl.BlockSpec((1, tm, D), lambda b, i: (b, i, 0)),
                  pl.BlockSpec((1, D), lambda b, i: (0, 0)),
                  pl.BlockSpec((N, D), lambda b, i: (0, 0))],
        out_specs=pl.BlockSpec((1, nh, nb, LANES, T_ATT), lambda b, i: (b, 0, i, 0, 0)),
        compiler_params=_cparams(("parallel", "parallel")),
        name="inproj_t",
    )(x3, g, wt)


def _inproj_perm_kernel(x_ref, g_ref, p_ref, w_ref, o_ref, *, r):
    h = _rms_bf16(x_ref[0], g_ref[...])
    nblk = h.shape[0] // PERM_BLK
    hp = jnp.concatenate(
        [jnp.dot(p_ref[...], h[k * PERM_BLK:(k + 1) * PERM_BLK], preferred_element_type=F32)
         for k in range(nblk)], axis=0).astype(BF16)
    res = jnp.dot(hp, w_ref[...], preferred_element_type=F32).astype(o_ref.dtype)
    n = PERM_BLK // r
    for k in range(nblk):
        for s in range(r):
            o_ref[0, s, k * n:(k + 1) * n, :] = res[k * PERM_BLK + s * n:k * PERM_BLK + (s + 1) * n, :]


def _inproj_perm(x3, g, w, r):
    B, S, D = x3.shape
    N = w.shape[1]
    tm = min(TM_PERM, S)
    n = PERM_BLK // r
    o = jnp.arange(PERM_BLK, dtype=jnp.int32)
    src = (o % n) * r + o // n
    perm = (src[:, None] == jnp.arange(PERM_BLK, dtype=jnp.int32)[None, :]).astype(BF16)
    kern = functools.partial(_inproj_perm_kernel, r=r)
    return pl.pallas_call(
        kern,
        out_shape=jax.ShapeDtypeStruct((B, r, S // r, N), BF16),
        grid=(B, S // tm),
        in_specs=[pl.BlockSpec((1, tm, D), lambda b, i: (b, i, 0)),
                  pl.BlockSpec((1, D), lambda b, i: (0, 0)),
                  pl.BlockSpec((PERM_BLK, PERM_BLK), lambda b, i: (0, 0)),
                  pl.BlockSpec((D, N), lambda b, i: (0, 0))],
        out_specs=pl.BlockSpec((1, r, tm // r, N), lambda b, i: (b, 0, i, 0)),
        compiler_params=_cparams(("parallel", "parallel")),
        name=f"inproj_perm_{r}",
    )(x3, g, perm, w)


def _attn_a_kernel(lam_ref, cfar_ref, q_ref, k_ref, vt_ref, bias_ref, g_ref, o_ref,
                   st0_scr, st1_scr, m0_scr, m1_scr, acc_scr, l_scr, *, out_scale, nq, n_blocks):
    k = pl.program_id(0)
    t = T_ATT
    nk = k_ref.shape[1] // t
    n1 = jnp.minimum(k // 2, n_blocks - 1)
    n2 = jnp.maximum(k - 1, 0) // 2
    h1, qi1 = (n1 // nq) % HA, n1 % nq
    h2, qi2 = (n2 // nq) % HA, n2 % nq

    @pl.when(k == 0)
    def _():
        st1_scr[...] = jnp.zeros(st1_scr.shape, F32)
        m1_scr[...] = jnp.zeros(m1_scr.shape, F32)
        acc_scr[...] = jnp.zeros(acc_scr.shape, F32)
        l_scr[...] = jnp.ones(l_scr.shape, F32)

    low_half = lax.broadcasted_iota(jnp.int32, (1, LANES), 1) < HALF_LANES

    def both(cmap, st_w, m_w, st_r, m_r):
        q = q_ref[0]
        zero = jnp.zeros_like(q)
        qc = jnp.where(low_half, q, zero) if cmap == 0 else jnp.where(low_half, zero, q)
        m_prev = m_r[...]
        l = jnp.zeros((1, t), F32)
        acc = jnp.zeros((LANES, t), F32)
        m_new = None
        for j, d in enumerate(range(-1, nk - 1)):
            a1 = lax.rem(qi1 + (d + nk), nk)
            delta1 = a1 - qi1
            kb = k_ref[0, pl.ds(pl.multiple_of(a1 * t, t), t), :]
            st = lax.dot_general(kb, qc, (((1,), (1,)), ((), ())), preferred_element_type=F32)
            if d <= 1:
                st = st + bias_ref[0, jnp.clip(delta1, -2, 2) + 2]
                cm = jnp.max(st, axis=0, keepdims=True)
            else:
                cm = (jnp.max(st, axis=0, keepdims=True)
                      + cfar_ref[2 * h1 + (delta1 > 0).astype(jnp.int32)])
            st_w[j] = st
            m_new = cm if m_new is None else jnp.maximum(m_new, cm)

            a2 = lax.rem(qi2 + (d + nk), nk)
            if d <= 1:
                shifted = m_prev
            else:
                shifted = m_prev - cfar_ref[2 * h2 + (a2 > qi2).astype(jnp.int32)]
            p = jnp.exp2(st_r[j] - shifted)
            l = l + jnp.sum(p, axis=0, keepdims=True)
            acc = acc + jnp.dot(vt_ref[0, 0, a2], p.astype(BF16), preferred_element_type=F32)
        m_w[...] = m_new
        return l, acc

    @pl.when(k % 2 == 0)
    def _():
        l1, acc1 = both(0, st0_scr, m0_scr, st1_scr, m1_scr)
        ot = acc_scr[...] / l_scr[...] - lam_ref[0] * (acc1 / l1)
        o = ot.T
        ms = jnp.mean(o * o, axis=-1, keepdims=True)
        o_ref[0] = (o * lax.rsqrt(ms + EPS) * g_ref[...] * out_scale).astype(o_ref.dtype)

    @pl.when(k % 2 == 1)
    def _():
        l0, acc0 = both(1, st1_scr, m1_scr, st0_scr, m0_scr)
        acc_scr[...] = acc0
        l_scr[...] = l0


def _attn_a(proj3, vt, lam, cfar, bias5, subln_g, lam_init):
    B, S, _ = proj3.shape
    t = T_ATT
    nk = S // t
    n_blocks = B * HA * nk

    def scored(k):
        n = jnp.minimum(k // 2, n_blocks - 1)
        return n // (HA * nk), (n // nk) % HA, n % nk

    def lagged(k, lag):
        n = jnp.maximum(k - lag, 0) // 2
        return n // (HA * nk), (n // nk) % HA, n % nk

    def q_map(k):
        b, h, qi = scored(k)
        return b, qi, h

    def k_map(k):
        b, h, _ = scored(k)
        return b, 0, HA + h

    def vt_map(k):
        b, h, _ = lagged(k, 1)
        return b, h, 0, 0, 0

    def out_map(k):
        b, h, qi = lagged(k, 2)
        return b, qi, h

    kern = functools.partial(_attn_a_kernel, out_scale=1.0 - lam_init, nq=nk, n_blocks=n_blocks)
    return pl.pallas_call(
        kern,
        out_shape=jax.ShapeDtypeStruct((B, S, HA * 2 * DA), BF16),
        grid=(2 * n_blocks + 1,),
        in_specs=[
            pl.BlockSpec(memory_space=pltpu.SMEM),
            pl.BlockSpec(memory_space=pltpu.SMEM),
            pl.BlockSpec((1, t, LANES), q_map),
            pl.BlockSpec((1, S, LANES), k_map),
            pl.BlockSpec((1, 1, nk, LANES, t), vt_map),
            pl.BlockSpec((1, 5, t, t), lambda k: (scored(k)[1], 0, 0, 0)),
            pl.BlockSpec((1, LANES), lambda k: (0, 0)),
        ],
        out_specs=pl.BlockSpec((1, t, LANES), out_map),
        scratch_shapes=[pltpu.VMEM((nk, t, t), F32), pltpu.VMEM((nk, t, t), F32),
                        pltpu.VMEM((1, t), F32), pltpu.VMEM((1, t), F32),
                        pltpu.VMEM((LANES, t), F32), pltpu.VMEM((1, t), F32)],
        compiler_params=_cparams(("arbitrary",)),
        name="diff_attn",
    )(lam, cfar, proj3, proj3, vt, bias5, subln_g)


def _attn_a_bias(rel_bias):
    t = T_ATT
    tab = rel_bias[:, :HA].astype(F32) * LOG2E
    d = jnp.arange(-1, 2, dtype=jnp.int32)[:, None, None] * t
    rel = d + jnp.arange(t, dtype=jnp.int32)[None, :, None] - jnp.arange(t, dtype=jnp.int32)[None, None, :]
    near = _bias_lookup(_t5_bucket(rel), tab)
    far = tab[_t5_bucket(jnp.array([-(t + 1), t + 1], dtype=jnp.int32))].T
    fill = lambda side: jnp.broadcast_to(far[:, side, None, None, None], (HA, 1, t, t))
    tiles = jnp.concatenate([fill(0), near, fill(1)], axis=1)
    return tiles, far.reshape(2 * HA)


def _attn_b_kernel(q_ref, k_ref, v_ref, bias_ref, o_ref, lse_ref, *, sub_len, r, sp, qp):
    nblk = sub_len // QB_DIL
    low_half = lax.broadcasted_iota(jnp.int32, (1, LANES), 1) < HALF_LANES
    for si in range(sp):
        s = si if sp == r else pl.program_id(2) * sp + si
        for qb in range(qp):
            i = pl.program_id(1) * qp + qb
            start = jnp.clip(i * QB_DIL - HALF_WIN, 0, sub_len - KW_DIL)
            start = pl.multiple_of(start, HALF_WIN)
            variant = jnp.where(i == 0, 0, jnp.where(i == nblk - 1, 2, 1))
            q = q_ref[0, si, qb * QB_DIL:(qb + 1) * QB_DIL, :]
            kw = k_ref[0, s, pl.ds(start, KW_DIL), :]
            vw = v_ref[0, s, pl.ds(start, KW_DIL), :]
            rows = (slice(qb * QB_DIL, (qb + 1) * QB_DIL) if r == 1
                    else pl.ds(s, QB_DIL, stride=r))
            for j in range(HB // 2):
                cols = slice(j * LANES, (j + 1) * LANES)
                qpair, kp, vp = q[:, cols], kw[:, cols], vw[:, cols]
                outs, lses = [], []
                for c in range(2):
                    qc = jnp.where(low_half if c == 0 else jnp.logical_not(low_half), qpair,
                                   jnp.zeros_like(qpair))
                    sc = lax.dot_general(qc, kp, (((1,), (1,)), ((), ())),
                                         preferred_element_type=F32)
                    sc = sc + bias_ref[2 * j + c, variant]
                    m = jnp.max(sc, axis=-1, keepdims=True)
                    p = jnp.exp(sc - m)
                    l = jnp.sum(p, axis=-1, keepdims=True)
                    outs.append(jnp.dot(p.astype(BF16), vp, preferred_element_type=F32) / l)
                    lses.append(m + jnp.log(l))
                o_ref[0, j, rows, :] = jnp.where(low_half, outs[0], outs[1])
                lse_ref[0, j, rows, :] = jnp.where(low_half, lses[0], lses[1])


def _attn_b(qkv4, bias3, g, cols):
    B, r, L, _ = qkv4.shape
    S = r * L
    width = HB * DB
    nblk = L // QB_DIL
    sp = min(r, ITEMS_DIL)
    qp = ITEMS_DIL // sp
    qcol, kcol, vcol = cols
    kern = functools.partial(_attn_b_kernel, sub_len=L, r=r, sp=sp, qp=qp)
    slab = jax.ShapeDtypeStruct((B, N_SLABS, S, LANES), F32)
    slab_spec = pl.BlockSpec((1, N_SLABS, QB_DIL * r * qp, LANES), lambda b, i, s: (b, 0, i, 0))
    return pl.pallas_call(
        kern,
        out_shape=[slab, slab],
        grid=(B, nblk // qp, r // sp),
        in_specs=[
            pl.BlockSpec((1, sp, QB_DIL * qp, width), lambda b, i, s: (b, s, i, qcol)),
            pl.BlockSpec((1, r, L, width), lambda b, i, s: (b, 0, 0, kcol)),
            pl.BlockSpec((1, r, L, width), lambda b, i, s: (b, 0, 0, vcol)),
            pl.BlockSpec((HB, 3, QB_DIL, KW_DIL), lambda b, i, s: (0, 0, 0, 0)),
        ],
        out_specs=[slab_spec, slab_spec],
        compiler_params=_cparams(("parallel", "arbitrary", "arbitrary")),
        name=f"dilated_attn_{g}",
    )(qkv4, qkv4, qkv4, bias3)


def _attn_b_bias(rel_bias, g):
    r = DILATIONS[g]
    tab = rel_bias[:, HA + g * HB: HA + (g + 1) * HB].astype(F32)
    off = jnp.arange(3, dtype=jnp.int32)[:, None, None] * HALF_WIN
    rel = (jnp.arange(KW_DIL, dtype=jnp.int32)[None, None, :] - off
           - jnp.arange(QB_DIL, dtype=jnp.int32)[None, :, None])
    bias = _bias_lookup(_t5_bucket(rel * r), tab)
    return jnp.where((jnp.abs(rel) <= HALF_WIN)[None], bias, NEG)


def _sgu_kernel(zu_ref, zv_ref, lng_ref, lnb_ref, ws_ref, bs_ref, o_ref):
    u = jax.nn.gelu(zu_ref[...].astype(F32))
    v = jax.nn.gelu(zv_ref[...].astype(F32))
    mu = jnp.mean(v, axis=-1, keepdims=True)
    var = jnp.mean(jnp.square(v - mu), axis=-1, keepdims=True)
    v = ((v - mu) * lax.rsqrt(var + EPS) * lng_ref[...] + lnb_ref[...]).astype(BF16)
    gd = v.shape[1] // C_GROUPS
    for n in range(v.shape[0] // CHUNK):
        rows = slice(n * CHUNK, (n + 1) * CHUNK)
        for g in range(C_GROUPS):
            cols = slice(g * gd, (g + 1) * gd)
            mixed = jnp.dot(ws_ref[g], v[rows, cols], preferred_element_type=F32) + bs_ref[:, cols]
            o_ref[rows, cols] = (u[rows, cols] * mixed).astype(o_ref.dtype)


def _sgu(proj2, ln_g, ln_b, w_s, b_exp):
    T = proj2.shape[0]
    tm = min(TM_SGU, T)
    w = MIX_W
    return pl.pallas_call(
        _sgu_kernel,
        out_shape=jax.ShapeDtypeStruct((T, w), BF16),
        grid=(T // tm,),
        in_specs=[pl.BlockSpec((tm, w), lambda i: (i, COL_ZU)),
                  pl.BlockSpec((tm, w), lambda i: (i, COL_ZU + 1)),
                  pl.BlockSpec((1, w), lambda i: (0, 0)),
                  pl.BlockSpec((1, w), lambda i: (0, 0)),
                  pl.BlockSpec((C_GROUPS, CHUNK, CHUNK), lambda i: (0, 0, 0)),
                  pl.BlockSpec((CHUNK, w), lambda i: (0, 0))],
        out_specs=pl.BlockSpec((tm, w), lambda i: (i, 0)),
        compiler_params=_cparams(("parallel",)),
        name="sgu",
    )(proj2, proj2, ln_g, ln_b, w_s, b_exp)


def _route(logits):
    lane = lax.broadcasted_iota(jnp.int32, logits.shape, 1)
    big = jnp.int32(LANES)
    is_grp = (lane >= N_EXPERTS) & (lane < N_EXPERTS + N_GROUPS)
    gl = jnp.where(is_grp, logits, NEG)
    gmax = jnp.max(gl, axis=-1, keepdims=True)
    g_idx = jnp.min(jnp.where(is_grp & (gl == gmax), lane, big), axis=-1, keepdims=True) - N_EXPERTS
    g_w = 1.0 / jnp.sum(jnp.where(is_grp, jnp.exp(gl - gmax), 0.0), axis=-1, keepdims=True)
    in_grp = (lane >= g_idx * E_PER_GROUP) & (lane < (g_idx + 1) * E_PER_GROUP)
    sel = jnp.where(in_grp, logits, NEG)
    v1 = jnp.max(sel, axis=-1, keepdims=True)
    i1 = jnp.min(jnp.where(in_grp & (sel == v1), lane, big), axis=-1, keepdims=True)
    rest = in_grp & (lane != i1)
    sel2 = jnp.where(rest, logits, NEG)
    v2 = jnp.max(sel2, axis=-1, keepdims=True)
    i2 = jnp.min(jnp.where(rest & (sel2 == v2), lane, big), axis=-1, keepdims=True)
    e2 = jnp.exp(v2 - v1)
    w1 = g_w / (1.0 + e2)
    w2 = g_w * e2 / (1.0 + e2)
    return jnp.where(lane == i1, w1, jnp.where(lane == i2, w2, 0.0))


def _mix_kernel(x_ref, ya_ref, ob0_ref, ob1_ref, ob2_ref, ls0_ref, ls1_ref, ls2_ref, yc_ref,
                g0_ref, g1_ref, g2_ref, wb_ref, wo_ref, nf_ref, wr_ref, br_ref,
                xo_ref, h_ref, comb_ref):
    slabs = []
    for j in range(N_SLABS):
        ls0, ls1, ls2 = ls0_ref[0, j], ls1_ref[0, j], ls2_ref[0, j]
        mx = jnp.maximum(jnp.maximum(ls0, ls1), ls2)
        e0, e1, e2 = jnp.exp(ls0 - mx), jnp.exp(ls1 - mx), jnp.exp(ls2 - mx)
        yb = (e0 * ob0_ref[0, j] + e1 * ob1_ref[0, j] + e2 * ob2_ref[0, j]) / (e0 + e1 + e2)
        slabs.append(yb.astype(BF16))
    yb = jnp.concatenate(slabs, axis=-1)
    merged = jax.nn.sigmoid(g0_ref[...].astype(F32)) * jnp.dot(ya_ref[...], wb_ref[0],
                                                               preferred_element_type=F32)
    merged += jax.nn.sigmoid(g1_ref[...].astype(F32)) * jnp.dot(yb, wb_ref[1],
                                                                preferred_element_type=F32)
    merged += jax.nn.sigmoid(g2_ref[...].astype(F32)) * jnp.dot(yc_ref[...], wb_ref[2],
                                                                preferred_element_type=F32)
    xn = x_ref[...] + jnp.dot(merged.astype(BF16), wo_ref[...], preferred_element_type=F32)
    xo_ref[...] = xn
    h = _rms_bf16(xn, nf_ref[...])
    h_ref[...] = h
    logits = jnp.dot(h, wr_ref[...], preferred_element_type=F32) + br_ref[...]
    comb_ref[...] = _route(logits)


def _mix(x2, ya, obs, lses, yc, proj2, wb, wo, nf, wr, br):
    T, D = x2.shape
    S = obs[0].shape[2]
    tm = min(TM_MIX, S)
    per_b = S // tm
    w = MIX_W
    row = lambda width: pl.BlockSpec((tm, width), lambda i: (i, 0))
    full = lambda a: pl.BlockSpec(a.shape, lambda i: (0,) * a.ndim)
    gate = lambda n: pl.BlockSpec((tm, D), lambda i: (i, COL_GATE + n))
    slab = pl.BlockSpec((1, N_SLABS, tm, LANES), lambda i: (i // per_b, 0, i % per_b, 0))
    return pl.pallas_call(
        _mix_kernel,
        out_shape=[jax.ShapeDtypeStruct((T, D), F32), jax.ShapeDtypeStruct((T, D), BF16),
                   jax.ShapeDtypeStruct((T, LANES), F32)],
        grid=(T // tm,),
        in_specs=[row(D), row(w), slab, slab, slab, slab, slab, slab, row(w),
                  gate(0), gate(1), gate(2), full(wb), full(wo), full(nf), full(wr), full(br)],
        out_specs=[row(D), row(D), row(LANES)],
        compiler_params=_cparams(("parallel",)),
        name="mix",
    )(x2, ya, obs[0], obs[1], obs[2], lses[0], lses[1], lses[2], yc, proj2, proj2, proj2,
      wb, wo, nf, wr, br)


def _moe_kernel(h_ref, comb_ref, x_ref, wg_ref, wu_ref, wd_ref, nfin_ref, o_ref, acc_scr,
                *, final_norm):
    e = pl.program_id(1)

    @pl.when(e == 0)
    def _():
        acc_scr[...] = jnp.zeros(acc_scr.shape, F32)

    h = h_ref[...]
    lane = lax.broadcasted_iota(jnp.int32, comb_ref.shape, 1)
    c = jnp.sum(jnp.where(lane == e, comb_ref[...], 0.0), axis=-1, keepdims=True)
    hid = (jax.nn.silu(jnp.dot(h, wg_ref[0].astype(BF16), preferred_element_type=F32))
           * jnp.dot(h, wu_ref[0].astype(BF16), preferred_element_type=F32))
    acc_scr[...] += c * jnp.dot(hid.astype(BF16), wd_ref[0].astype(BF16),
                                preferred_element_type=F32)

    @pl.when(e == pl.num_programs(1) - 1)
    def _():
        xn = x_ref[...] + acc_scr[...]
        if final_norm:
            ms = jnp.mean(xn * xn, axis=-1, keepdims=True)
            xn = xn * lax.rsqrt(ms + EPS) * nfin_ref[...]
        o_ref[...] = xn


def _moe_dense(h, comb, x2, wg, wu, wd, nfin, final_norm):
    T, D = x2.shape
    tm = min(TM_MOE, T)
    F = wg.shape[2]
    kern = functools.partial(_moe_kernel, final_norm=final_norm)
    return pl.pallas_call(
        kern,
        out_shape=jax.ShapeDtypeStruct((T, D), F32),
        grid=(T // tm, N_EXPERTS),
        in_specs=[pl.BlockSpec((tm, D), lambda i, e: (i, 0)),
                  pl.BlockSpec((tm, LANES), lambda i, e: (i, 0)),
                  pl.BlockSpec((tm, D), lambda i, e: (i, 0)),
                  pl.BlockSpec((1, D, F), lambda i, e: (e, 0, 0)),
                  pl.BlockSpec((1, D, F), lambda i, e: (e, 0, 0)),
                  pl.BlockSpec((1, F, D), lambda i, e: (e, 0, 0)),
                  pl.BlockSpec((1, D), lambda i, e: (0, 0))],
        out_specs=pl.BlockSpec((tm, D), lambda i, e: (i, 0)),
        scratch_shapes=[pltpu.VMEM((tm, D), F32)],
        compiler_params=_cparams(("parallel", "arbitrary")),
        name="moe_dense",
    )(h, comb, x2, wg, wu, wd, nfin)


def _moe_dispatch_kernel(h_ref, comb_ref, o_ref, cnt_ref):
    nt = o_ref.shape[0]
    tm = h_ref.shape[0] // nt
    before = jnp.where(lax.broadcasted_iota(jnp.int32, (tm, tm), 0)
                       < lax.broadcasted_iota(jnp.int32, (tm, tm), 1), 1.0, 0.0).astype(BF16)
    slot = lax.broadcasted_iota(jnp.int32, (MOE_CAP, tm), 0).astype(F32)
    for u in range(nt):
        rows = slice(u * tm, (u + 1) * tm)
        comb = comb_ref[rows, :]
        hi = comb.astype(BF16)
        lo = (comb - hi.astype(F32)).astype(BF16)
        haug = jnp.concatenate([h_ref[rows, :], hi, lo], axis=1)
        a_t = comb.T[:N_EXPERTS] > 0.0
        a_f = jnp.where(a_t, 1.0, 0.0)
        rank_t = jnp.dot(a_f.astype(BF16), before, preferred_element_type=F32)
        blocks = [jnp.where((slot == rank_t[e:e + 1]) & a_t[e:e + 1], 1.0, 0.0).astype(BF16)
                  for e in range(N_EXPERTS)]
        res = jnp.dot(jnp.concatenate(blocks, axis=0), haug, preferred_element_type=F32)
        res = res.astype(o_ref.dtype)
        for e in range(N_EXPERTS):
            o_ref[u, e] = res[e * MOE_CAP:(e + 1) * MOE_CAP]
        cnt_ref[u] = jnp.broadcast_to(jnp.sum(a_f, axis=1, keepdims=True), cnt_ref.shape[1:])


def _moe_dispatch(h, comb):
    T, D = h.shape
    tm = min(TM_DISP, T)
    n = T // tm
    nt = math.gcd(TILES_PER_STEP, n)
    return pl.pallas_call(
        _moe_dispatch_kernel,
        out_shape=[jax.ShapeDtypeStruct((n, N_EXPERTS, MOE_CAP, D + 2 * LANES), BF16),
                   jax.ShapeDtypeStruct((n, N_EXPERTS, LANES), F32)],
        grid=(n // nt,),
        in_specs=[pl.BlockSpec((nt * tm, D), lambda i: (i, 0)),
                  pl.BlockSpec((nt * tm, LANES), lambda i: (i, 0))],
        out_specs=[pl.BlockSpec((nt, N_EXPERTS, MOE_CAP, D + 2 * LANES), lambda i: (i, 0, 0, 0)),
                   pl.BlockSpec((nt, N_EXPERTS, LANES), lambda i: (i, 0, 0))],
        compiler_params=_cparams(("parallel",)),
        name="moe_dispatch",
    )(h, comb)


def _moe_ffn_kernel(n16_ref, s_ref, wg_ref, wu_ref, wd_ref, o_ref,
                    wg_scr, wu_scr, wd_scr, lhs_scr, y_scr):
    e, c = pl.program_id(0), pl.program_id(1)

    @pl.when(c == 0)
    def _():
        wg_scr[...] = wg_ref[0].astype(BF16)
        wu_scr[...] = wu_ref[0].astype(BF16)
        wd_scr[...] = wd_ref[0].astype(BF16)

    g, _, cap, _ = s_ref.shape
    D = o_ref.shape[-1]
    @pl.when((e == 0) & (c == 0))
    def _():
        lhs_scr[...] = jnp.zeros(lhs_scr.shape, lhs_scr.dtype)
        y_scr[...] = jnp.zeros(y_scr.shape, y_scr.dtype)

    offs = []
    off = jnp.int32(0)
    for t in range(g):
        offs.append(off)
        lhs_scr[pl.ds(pl.multiple_of(off, BF16_ROWS), cap), :] = s_ref[t, 0]
        off = off + n16_ref[(c * g + t) * N_EXPERTS + e]
    total = off

    def run(nrows):
        rows = lhs_scr[:nrows]
        h = rows[:, :D]
        wparts = rows[:, D:].astype(F32)
        lane = lax.broadcasted_iota(jnp.int32, wparts.shape, 1)
        w = jnp.sum(jnp.where(lane % LANES == e, wparts, 0.0), axis=-1, keepdims=True)
        hid = (jax.nn.silu(jnp.dot(h, wg_scr[...], preferred_element_type=F32))
               * jnp.dot(h, wu_scr[...], preferred_element_type=F32))
        y = w * jnp.dot(hid.astype(BF16), wd_scr[...], preferred_element_type=F32)
        y_scr[:nrows] = y.astype(y_scr.dtype)

    classes = tuple(range(g * cap // 2, g * cap + 1, FFN_ROW_STEP))
    lower = 0
    for nrows in classes:
        pl.when((total > lower) & (total <= nrows))(functools.partial(run, nrows))
        lower = nrows

    for t in range(g):
        o_ref[t, 0] = y_scr[pl.ds(pl.multiple_of(offs[t], BF16_ROWS), cap), :]


def _moe_ffn(srt, n16, wg, wu, wd):
    n, ne, cap, wdt = srt.shape
    D, F = wg.shape[1], wg.shape[2]
    g = math.gcd(G_FFN, n)
    return pl.pallas_call(
        _moe_ffn_kernel,
        out_shape=jax.ShapeDtypeStruct((n, ne, cap, D), BF16),
        grid_spec=pltpu.PrefetchScalarGridSpec(
            num_scalar_prefetch=1,
            grid=(ne, n // g),
            in_specs=[pl.BlockSpec((g, 1, cap, wdt), lambda e, c, n16: (c, e, 0, 0)),
                      pl.BlockSpec((1, D, F), lambda e, c, n16: (e, 0, 0)),
                      pl.BlockSpec((1, D, F), lambda e, c, n16: (e, 0, 0)),
                      pl.BlockSpec((1, F, D), lambda e, c, n16: (e, 0, 0))],
            out_specs=pl.BlockSpec((g, 1, cap, D), lambda e, c, n16: (c, e, 0, 0)),
            scratch_shapes=[pltpu.VMEM((D, F), BF16), pltpu.VMEM((D, F), BF16),
                            pltpu.VMEM((F, D), BF16), pltpu.VMEM((g * cap, wdt), BF16),
                            pltpu.VMEM((g * cap, D), BF16)]),
        compiler_params=_cparams(("arbitrary", "arbitrary")),
        name="moe_ffn",
    )(n16, srt, wg, wu, wd)


def _moe_combine_kernel(y_ref, comb_ref, x_ref, nfin_ref, o_ref, *, final_norm):
    nt = y_ref.shape[0]
    tm = x_ref.shape[0] // nt
    ncol = N_EXPERTS * MOE_CAP
    before = jnp.where(lax.broadcasted_iota(jnp.int32, (tm, tm), 1)
                       < lax.broadcasted_iota(jnp.int32, (tm, tm), 0), 1.0, 0.0).astype(BF16)
    spread = jnp.where(lax.broadcasted_iota(jnp.int32, (LANES, ncol), 1) // MOE_CAP
                       == lax.broadcasted_iota(jnp.int32, (LANES, ncol), 0), 1.0, 0.0).astype(BF16)
    slot = (lax.broadcasted_iota(jnp.int32, (tm, ncol), 1) % MOE_CAP).astype(F32)
    for u in range(nt):
        rows = slice(u * tm, (u + 1) * tm)
        a = comb_ref[rows, :] > 0.0
        rank = jnp.dot(before, jnp.where(a, 1.0, 0.0).astype(BF16), preferred_element_type=F32)
        key = jnp.where(a, rank, -1.0).astype(BF16)
        key_all = jnp.dot(key, spread, preferred_element_type=F32)
        pc = jnp.where(slot == key_all, 1.0, 0.0).astype(BF16)
        y = jnp.concatenate([y_ref[u, e] for e in range(N_EXPERTS)], axis=0)
        xn = x_ref[rows, :] + jnp.dot(pc, y, preferred_element_type=F32)
        if final_norm:
            ms = jnp.mean(xn * xn, axis=-1, keepdims=True)
            xn = xn * lax.rsqrt(ms + EPS) * nfin_ref[...]
        o_ref[rows, :] = xn


def _moe_combine(y, comb, x2, nfin, final_norm):
    T, D = x2.shape
    n, ne, cap, _ = y.shape
    tm = T // n
    nt = math.gcd(TILES_PER_STEP, n)
    kern = functools.partial(_moe_combine_kernel, final_norm=final_norm)
    return pl.pallas_call(
        kern,
        out_shape=jax.ShapeDtypeStruct((T, D), F32),
        grid=(n // nt,),
        in_specs=[pl.BlockSpec((nt, ne, cap, D), lambda i: (i, 0, 0, 0)),
                  pl.BlockSpec((nt * tm, LANES), lambda i: (i, 0)),
                  pl.BlockSpec((nt * tm, D), lambda i: (i, 0)),
                  pl.BlockSpec((1, D), lambda i: (0, 0))],
        out_specs=pl.BlockSpec((nt * tm, D), lambda i: (i, 0)),
        compiler_params=_cparams(("parallel",)),
        name="moe_combine",
    )(y, comb, x2, nfin)


def _moe_fix_kernel(tiles_ref, experts_ref, first_ref, n_ref, h_ref, comb_ref, prev_ref,
                    wg_ref, wu_ref, wd_ref, o_ref):
    del tiles_ref
    s = pl.program_id(0)

    @pl.when(s < n_ref[0])
    def _():
        e = experts_ref[s]
        tm = h_ref.shape[0]
        comb = comb_ref[...]
        a = jnp.where(comb > 0.0, 1.0, 0.0)
        before = (lax.broadcasted_iota(jnp.int32, (tm, tm), 1)
                  < lax.broadcasted_iota(jnp.int32, (tm, tm), 0))
        rank = jnp.dot(jnp.where(before, 1.0, 0.0).astype(BF16), a.astype(BF16),
                       preferred_element_type=F32)
        lane = lax.broadcasted_iota(jnp.int32, comb.shape, 1)
        dropped = (lane == e) & (rank >= MOE_CAP)
        c = jnp.sum(jnp.where(dropped, comb, 0.0), axis=-1, keepdims=True)
        h = h_ref[...]
        hid = (jax.nn.silu(jnp.dot(h, wg_ref[0].astype(BF16), preferred_element_type=F32))
               * jnp.dot(h, wu_ref[0].astype(BF16), preferred_element_type=F32))
        add = c * jnp.dot(hid.astype(BF16), wd_ref[0].astype(BF16), preferred_element_type=F32)
        fresh = first_ref[s] == 1

        @pl.when(fresh)
        def _():
            o_ref[...] = prev_ref[...] + add

        @pl.when(jnp.logical_not(fresh))
        def _():
            o_ref[...] += add


def _moe_fix(tiles, experts, first, n, out, h, comb, wg, wu, wd):
    T, D = out.shape
    tm = min(TM_DISP, T)
    F = wg.shape[2]
    tile = lambda width: pl.BlockSpec((tm, width), lambda s, tl, ex, fi, n: (tl[s], 0))
    wspec = lambda shape: pl.BlockSpec(shape, lambda s, tl, ex, fi, n: (ex[s], 0, 0))
    return pl.pallas_call(
        _moe_fix_kernel,
        out_shape=jax.ShapeDtypeStruct((T, D), F32),
        grid_spec=pltpu.PrefetchScalarGridSpec(
            num_scalar_prefetch=4,
            grid=(MAX_OVF,),
            in_specs=[tile(D), tile(LANES), tile(D), wspec((1, D, F)), wspec((1, D, F)),
                      wspec((1, F, D))],
            out_specs=tile(D)),
        input_output_aliases={6: 0},
        compiler_params=_cparams(("arbitrary",)),
        name="moe_fix",
    )(tiles, experts, first, n, h, comb, out, wg, wu, wd)


def _final_norm_kernel(x_ref, g_ref, o_ref):
    x = x_ref[...]
    ms = jnp.mean(x * x, axis=-1, keepdims=True)
    o_ref[...] = x * lax.rsqrt(ms + EPS) * g_ref[...]


def _final_norm(x2, g):
    T, D = x2.shape
    tm = min(TM_PROJ, T)
    return pl.pallas_call(
        _final_norm_kernel,
        out_shape=jax.ShapeDtypeStruct((T, D), F32),
        grid=(T // tm,),
        in_specs=[pl.BlockSpec((tm, D), lambda i: (i, 0)), pl.BlockSpec((1, D), lambda i: (0, 0))],
        out_specs=pl.BlockSpec((tm, D), lambda i: (i, 0)),
        compiler_params=_cparams(("parallel",)),
        name="final_norm",
    )(x2, g)


def _moe(h, comb, x2, wg, wu, wd, nfin, final_norm):
    srt, cnt = _moe_dispatch(h, comb)
    over = (cnt[:, :, 0] > MOE_CAP).reshape(-1)
    n_ovf = jnp.sum(over.astype(jnp.int32))
    pairs = jnp.nonzero(over, size=MAX_OVF, fill_value=0)[0].astype(jnp.int32)
    pairs = jnp.where(jnp.arange(MAX_OVF) < n_ovf, pairs, pairs[jnp.clip(n_ovf - 1, 0, MAX_OVF - 1)])
    tiles, experts = pairs // N_EXPERTS, pairs % N_EXPERTS
    first = jnp.concatenate([jnp.ones((1,), jnp.int32),
                             (tiles[1:] != tiles[:-1]).astype(jnp.int32)])

    used = jnp.minimum(cnt[:, :, 0], MOE_CAP).astype(jnp.int32).reshape(-1)
    n16 = (used + (BF16_ROWS - 1)) // BF16_ROWS * BF16_ROWS

    def routed():
        y = _moe_ffn(srt, n16, wg, wu, wd)

        def fixed():
            out = _moe_combine(y, comb, x2, nfin, False)
            out = _moe_fix(tiles, experts, first, n_ovf.reshape(1), out, h, comb, wg, wu, wd)
            return _final_norm(out, nfin) if final_norm else out

        return lax.cond(n_ovf > 0, fixed, lambda: _moe_combine(y, comb, x2, nfin, final_norm))

    return lax.cond(n_ovf > MAX_OVF,
                    lambda: _moe_dense(h, comb, x2, wg, wu, wd, nfin, final_norm), routed)


def kernel(x, rel_bias, norm_mix, w_in, diff_lambda, diff_subln, sgu_ln_g, sgu_ln_b, sgu_w, sgu_b,
           w_branch, w_out, norm_ffn, w_router_grp, b_router_grp, w_router_exp, b_router_exp,
           w_gate, w_up, w_down, norm_final):
    B, S, D = x.shape
    T = B * S
    depth = w_in.shape[0]
    a_out = HA * 2 * DA
    grp_w = HB * DB
    b_cols = 3 * NG_B * grp_w
    qkv_b0 = 3 * a_out
    zc0 = qkv_b0 + b_cols
    gate0 = zc0 + 2 * MIX_W
    qk_scale = DA ** -0.5

    bias_a, cfar = _attn_a_bias(rel_bias)
    bias_b = [_attn_b_bias(rel_bias, g) for g in range(NG_B)]

    col = jnp.arange(w_in.shape[2])
    col_scale = jnp.where(col < a_out, qk_scale * LOG2E,
                          jnp.where((col >= qkv_b0) & (col < qkv_b0 + NG_B * grp_w), qk_scale, 1.0))

    def group_cols(w, g):
        return [w[:, qkv_b0 + (c * NG_B + g) * grp_w: qkv_b0 + (c * NG_B + g + 1) * grp_w]
                for c in range(3)]

    x2 = x.reshape(T, D)
    for i in range(depth):
        w = (w_in[i] * col_scale.astype(F32)).astype(BF16)
        nm = norm_mix[i][None, :]
        w_main = jnp.concatenate([w[:, :2 * a_out], w[:, zc0:]] + group_cols(w, 0), axis=1)
        proj2 = _inproj(x2, nm, w_main)
        x3 = x2.reshape(B, S, D)
        vt = _inproj_t(x3, nm, w[:, 2 * a_out:3 * a_out].T)
        proj3 = proj2.reshape(B, S, proj2.shape[1])

        lam_init = 0.8 - 0.6 * math.exp(-0.3 * i)
        lp = diff_lambda[i].astype(F32)
        lam = jnp.exp(jnp.sum(lp[0] * lp[1])) - jnp.exp(jnp.sum(lp[2] * lp[3])) + lam_init
        ya = _attn_a(proj3, vt, lam.reshape(1), cfar, bias_a, diff_subln[i][None, :], lam_init)

        obs, lses = [], []
        for g in range(NG_B):
            r = DILATIONS[g]
            if r == 1:
                qkv4, cols = proj3[:, None], (COL_QKV0, COL_QKV0 + 1, COL_QKV0 + 2)
            else:
                w_g = jnp.concatenate(group_cols(w, g), axis=1)
                qkv4, cols = _inproj_perm(x3, nm, w_g, r), (0, 1, 2)
            o, l = _attn_b(qkv4, bias_b[g], g, cols)
            obs.append(o)
            lses.append(l)

        b_exp = jnp.repeat(sgu_b[i].T, MIX_W // C_GROUPS, axis=1)
        yc = _sgu(proj2, sgu_ln_g[i][None, :], sgu_ln_b[i][None, :], sgu_w[i].astype(BF16), b_exp)

        wr = jnp.concatenate([w_router_exp[i].transpose(1, 0, 2).reshape(D, N_EXPERTS),
                              w_router_grp[i]], axis=1)
        wr = jnp.pad(wr, ((0, 0), (0, LANES - wr.shape[1]))).astype(BF16)
        br = jnp.concatenate([b_router_exp[i].reshape(N_EXPERTS), b_router_grp[i]])
        br = jnp.pad(br, (0, LANES - br.shape[0]))[None, :].astype(F32)

        x2, h, comb = _mix(x2, ya.reshape(T, a_out), obs, lses, yc, proj2,
                           w_branch[i].astype(BF16), w_out[i].astype(BF16), norm_ffn[i][None, :],
                           wr, br)
        x2 = _moe(h, comb, x2, w_gate[i], w_up[i], w_down[i], norm_final[None, :], i == depth - 1)
    return x2.reshape(B, S, D)
```

```python
import functools
import math

import jax
import jax.numpy as jnp
from jax import lax
from jax.experimental import pallas as pl
from jax.experimental.pallas import tpu as pltpu

F32 = jnp.float32
BF16 = jnp.bfloat16

EPS = 1e-6
NEG = -1e30
LOG2E = 1.4426950408889634
LANES = 128
HALF_LANES = LANES // 2
VMEM_LIMIT = 48 * 1024 * 1024

HA = 4
DA = 64
MIX_W = 512
WINDOWS = (128, 512, 2048)
DILATIONS = (1, 4, 16)
NG_B = 3
HB = 8
DB = 64
HALF_WIN = 64
CHUNK = 128
C_GROUPS = 4
N_BRANCH = 3
N_BUCKETS = 32
MAX_DIST = 128
N_GROUPS = 4
E_PER_GROUP = 4
N_EXPERTS = N_GROUPS * E_PER_GROUP
N_SLABS = MIX_W // LANES

TM_PROJ = 1024
TN_PROJ = 3328
TM_PERM = 1024
PERM_BLK = 256
T_ATT = 512
QB_DIL = 128
KW_DIL = QB_DIL + 2 * HALF_WIN
ITEMS_DIL = 8
SUBS_DIL = 4
OUT_ROWS_DIL = 2048
TM_SGU = 2048
TM_MIX = 512
TM_MOE = 1024
TM_DISP = 256
MOE_CAP = HALF_LANES
G_FFN = 16
TILES_PER_STEP = 4
FFN_ROW_STEP = 64
BF16_ROWS = 16
MAX_OVF = 64

COL_ZU = 2
COL_GATE = 2
COL_QKV0 = 10


def _cparams(sem):
    return pltpu.CompilerParams(dimension_semantics=sem, vmem_limit_bytes=VMEM_LIMIT)


def _t5_bucket(rel):
    nb = N_BUCKETS // 2
    max_exact = nb // 2
    ret = (rel > 0).astype(jnp.int32) * nb
    n = jnp.abs(rel)
    nf = jnp.maximum(n, 1).astype(F32)
    large = max_exact + (jnp.log(nf / max_exact) / math.log(MAX_DIST / max_exact)
                         * (nb - max_exact)).astype(jnp.int32)
    large = jnp.minimum(large, nb - 1)
    return ret + jnp.where(n < max_exact, n, large)


def _bias_lookup(bucket, tab):
    out = jnp.zeros((tab.shape[1],) + bucket.shape, F32)
    expand = (slice(None),) + (None,) * bucket.ndim
    for b in range(N_BUCKETS):
        out = jnp.where(bucket[None] == b, tab[b][expand], out)
    return out


def _rms_bf16(x, g):
    ms = jnp.mean(x * x, axis=-1, keepdims=True)
    return (x * lax.rsqrt(ms + EPS) * g).astype(BF16)


def _inproj_kernel(x_ref, g_ref, w_ref, o_ref, h_scr):
    @pl.when(pl.program_id(1) == 0)
    def _():
        h_scr[...] = _rms_bf16(x_ref[...], g_ref[...])

    o_ref[...] = jnp.dot(h_scr[...], w_ref[...], preferred_element_type=F32).astype(o_ref.dtype)


def _inproj(x2, g, w):
    T, D = x2.shape
    N = w.shape[1]
    tm = min(TM_PROJ, T)
    return pl.pallas_call(
        _inproj_kernel,
        out_shape=jax.ShapeDtypeStruct((T, N), BF16),
        grid=(T // tm, N // TN_PROJ),
        in_specs=[pl.BlockSpec((tm, D), lambda i, j: (i, 0)),
                  pl.BlockSpec((1, D), lambda i, j: (0, 0)),
                  pl.BlockSpec((D, TN_PROJ), lambda i, j: (0, j))],
        out_specs=pl.BlockSpec((tm, TN_PROJ), lambda i, j: (i, j)),
        scratch_shapes=[pltpu.VMEM((tm, D), BF16)],
        compiler_params=_cparams(("parallel", "arbitrary")),
        name="inproj",
    )(x2, g, w)


def _inproj_t_kernel(x_ref, g_ref, wt_ref, o_ref):
    h = _rms_bf16(x_ref[0], g_ref[...])
    res = lax.dot_general(wt_ref[...], h, (((1,), (1,)), ((), ())),
                          preferred_element_type=F32).astype(o_ref.dtype)
    for hd in range(o_ref.shape[1]):
        for n in range(o_ref.shape[2]):
            o_ref[0, hd, n] = res[hd * LANES:(hd + 1) * LANES, n * T_ATT:(n + 1) * T_ATT]


def _inproj_t(x3, g, wt):
    B, S, D = x3.shape
    N = wt.shape[0]
    tm = min(TM_PROJ, S)
    nh, nb = N // LANES, tm // T_ATT
    return pl.pallas_call(
        _inproj_t_kernel,
        out_shape=jax.ShapeDtypeStruct((B, nh, S // T_ATT, LANES, T_ATT), BF16),
        grid=(B, S // tm),
        in_specs=[pl.BlockSpec((1, tm, D), lambda b, i: (b, i, 0)),
                  pl.BlockSpec((1, D), lambda b, i: (0, 0)),
                  pl.BlockSpec((N, D), lambda b, i: (0, 0))],
        out_specs=pl.BlockSpec((1, nh, nb, LANES, T_ATT), lambda b, i: (b, 0, i, 0, 0)),
        compiler_params=_cparams(("parallel", "parallel")),
        name="inproj_t",
    )(x3, g, wt)


def _inproj_perm_kernel(x_ref, g_ref, p_ref, w_ref, o_ref, *, r):
    h = _rms_bf16(x_ref[0], g_ref[...])
    nblk = h.shape[0] // PERM_BLK
    hp = jnp.concatenate(
        [jnp.dot(p_ref[...], h[k * PERM_BLK:(k + 1) * PERM_BLK], preferred_element_type=F32)
         for k in range(nblk)], axis=0).astype(BF16)
    res = jnp.dot(hp, w_ref[...], preferred_element_type=F32).astype(o_ref.dtype)
    n = PERM_BLK // r
    for k in range(nblk):
        for s in range(r):
            o_ref[0, s, k * n:(k + 1) * n, :] = res[k * PERM_BLK + s * n:k * PERM_BLK + (s + 1) * n, :]


def _inproj_perm(x3, g, w, r):
    B, S, D = x3.shape
    N = w.shape[1]
    tm = min(TM_PERM, S)
    n = PERM_BLK // r
    o = jnp.arange(PERM_BLK, dtype=jnp.int32)
    src = (o % n) * r + o // n
    perm = (src[:, None] == jnp.arange(PERM_BLK, dtype=jnp.int32)[None, :]).astype(BF16)
    kern = functools.partial(_inproj_perm_kernel, r=r)
    return pl.pallas_call(
        kern,
        out_shape=jax.ShapeDtypeStruct((B, r, S // r, N), BF16),
        grid=(B, S // tm),
        in_specs=[pl.BlockSpec((1, tm, D), lambda b, i: (b, i, 0)),
                  pl.BlockSpec((1, D), lambda b, i: (0, 0)),
                  pl.BlockSpec((PERM_BLK, PERM_BLK), lambda b, i: (0, 0)),
                  pl.BlockSpec((D, N), lambda b, i: (0, 0))],
        out_specs=pl.BlockSpec((1, r, tm // r, N), lambda b, i: (b, 0, i, 0)),
        compiler_params=_cparams(("parallel", "parallel")),
        name=f"inproj_perm_{r}",
    )(x3, g, perm, w)


def _attn_a_kernel(lam_ref, cfar_ref, q_ref, k_ref, vt_ref, bias_ref, g_ref, o_ref,
                   st0_scr, st1_scr, m0_scr, m1_scr, acc_scr, l_scr, *, out_scale, nq, n_blocks):
    k = pl.program_id(0)
    t = T_ATT
    nk = k_ref.shape[1] // t
    n1 = jnp.minimum(k // 2, n_blocks - 1)
    n2 = jnp.maximum(k - 1, 0) // 2
    h1, qi1 = (n1 // nq) % HA, n1 % nq
    h2, qi2 = (n2 // nq) % HA, n2 % nq

    @pl.when(k == 0)
    def _():
        st1_scr[...] = jnp.zeros(st1_scr.shape, F32)
        m1_scr[...] = jnp.zeros(m1_scr.shape, F32)
        acc_scr[...] = jnp.zeros(acc_scr.shape, F32)
        l_scr[...] = jnp.ones(l_scr.shape, F32)

    low_half = lax.broadcasted_iota(jnp.int32, (1, LANES), 1) < HALF_LANES

    def both(cmap, st_w, m_w, st_r, m_r):
        q = q_ref[0]
        zero = jnp.zeros_like(q)
        qc = jnp.where(low_half, q, zero) if cmap == 0 else jnp.where(low_half, zero, q)
        m_prev = m_r[...]
        l = jnp.zeros((1, t), F32)
        acc = jnp.zeros((LANES, t), F32)
        m_new = None
        for j, d in enumerate(range(-1, nk - 1)):
            a1 = lax.rem(qi1 + (d + nk), nk)
            delta1 = a1 - qi1
            kb = k_ref[0, pl.ds(pl.multiple_of(a1 * t, t), t), :]
            st = lax.dot_general(kb, qc, (((1,), (1,)), ((), ())), preferred_element_type=F32)
            if d <= 1:
                st = st + bias_ref[0, jnp.clip(delta1, -2, 2) + 2]
                cm = jnp.max(st, axis=0, keepdims=True)
            else:
                cm = (jnp.max(st, axis=0, keepdims=True)
                      + cfar_ref[2 * h1 + (delta1 > 0).astype(jnp.int32)])
            st_w[j] = st
            m_new = cm if m_new is None else jnp.maximum(m_new, cm)

            a2 = lax.rem(qi2 + (d + nk), nk)
            if d <= 1:
                shifted = m_prev
            else:
                shifted = m_prev - cfar_ref[2 * h2 + (a2 > qi2).astype(jnp.int32)]
            p = jnp.exp2(st_r[j] - shifted)
            l = l + jnp.sum(p, axis=0, keepdims=True)
            acc = acc + jnp.dot(vt_ref[0, 0, a2], p.astype(BF16), preferred_element_type=F32)
        m_w[...] = m_new
        return l, acc

    @pl.when(k % 2 == 0)
    def _():
        l1, acc1 = both(0, st0_scr, m0_scr, st1_scr, m1_scr)
        ot = acc_scr[...] / l_scr[...] - lam_ref[0] * (acc1 / l1)
        o = ot.T
        ms = jnp.mean(o * o, axis=-1, keepdims=True)
        o_ref[0] = (o * lax.rsqrt(ms + EPS) * g_ref[...] * out_scale).astype(o_ref.dtype)

    @pl.when(k % 2 == 1)
    def _():
        l0, acc0 = both(1, st1_scr, m1_scr, st0_scr, m0_scr)
        acc_scr[...] = acc0
        l_scr[...] = l0


def _attn_a(proj3, vt, lam, cfar, bias5, subln_g, lam_init):
    B, S, _ = proj3.shape
    t = T_ATT
    nk = S // t
    n_blocks = B * HA * nk

    def scored(k):
        n = jnp.minimum(k // 2, n_blocks - 1)
        return n // (HA * nk), (n // nk) % HA, n % nk

    def lagged(k, lag):
        n = jnp.maximum(k - lag, 0) // 2
        return n // (HA * nk), (n // nk) % HA, n % nk

    def q_map(k):
        b, h, qi = scored(k)
        return b, qi, h

    def k_map(k):
        b, h, _ = scored(k)
        return b, 0, HA + h

    def vt_map(k):
        b, h, _ = lagged(k, 1)
        return b, h, 0, 0, 0

    def out_map(k):
        b, h, qi = lagged(k, 2)
        return b, qi, h

    kern = functools.partial(_attn_a_kernel, out_scale=1.0 - lam_init, nq=nk, n_blocks=n_blocks)
    return pl.pallas_call(
        kern,
        out_shape=jax.ShapeDtypeStruct((B, S, HA * 2 * DA), BF16),
        grid=(2 * n_blocks + 1,),
        in_specs=[
            pl.BlockSpec(memory_space=pltpu.SMEM),
            pl.BlockSpec(memory_space=pltpu.SMEM),
            pl.BlockSpec((1, t, LANES), q_map),
            pl.BlockSpec((1, S, LANES), k_map),
            pl.BlockSpec((1, 1, nk, LANES, t), vt_map),
            pl.BlockSpec((1, 5, t, t), lambda k: (scored(k)[1], 0, 0, 0)),
            pl.BlockSpec((1, LANES), lambda k: (0, 0)),
        ],
        out_specs=pl.BlockSpec((1, t, LANES), out_map),
        scratch_shapes=[pltpu.VMEM((nk, t, t), F32), pltpu.VMEM((nk, t, t), F32),
                        pltpu.VMEM((1, t), F32), pltpu.VMEM((1, t), F32),
                        pltpu.VMEM((LANES, t), F32), pltpu.VMEM((1, t), F32)],
        compiler_params=_cparams(("arbitrary",)),
        name="diff_attn",
    )(lam, cfar, proj3, proj3, vt, bias5, subln_g)


def _attn_a_bias(rel_bias):
    t = T_ATT
    tab = rel_bias[:, :HA].astype(F32) * LOG2E
    d = jnp.arange(-1, 2, dtype=jnp.int32)[:, None, None] * t
    rel = d + jnp.arange(t, dtype=jnp.int32)[None, :, None] - jnp.arange(t, dtype=jnp.int32)[None, None, :]
    near = _bias_lookup(_t5_bucket(rel), tab)
    far = tab[_t5_bucket(jnp.array([-(t + 1), t + 1], dtype=jnp.int32))].T
    fill = lambda side: jnp.broadcast_to(far[:, side, None, None, None], (HA, 1, t, t))
    tiles = jnp.concatenate([fill(0), near, fill(1)], axis=1)
    return tiles, far.reshape(2 * HA)


def _attn_b_kernel(q_ref, k_ref, v_ref, bias_ref, o_ref, lse_ref, *, sub_len, r, sp, qp):
    nblk = sub_len // QB_DIL
    low_half = lax.broadcasted_iota(jnp.int32, (1, LANES), 1) < HALF_LANES
    for si in range(sp):
        s = si if sp == r else pl.program_id(2) * sp + si
        for qb in range(qp):
            i = pl.program_id(1) * qp + qb
            start = jnp.clip(i * QB_DIL - HALF_WIN, 0, sub_len - KW_DIL)
            start = pl.multiple_of(start, HALF_WIN)
            variant = jnp.where(i == 0, 0, jnp.where(i == nblk - 1, 2, 1))
            q = q_ref[0, si, qb * QB_DIL:(qb + 1) * QB_DIL, :]
            kw = k_ref[0, s, pl.ds(start, KW_DIL), :]
            vw = v_ref[0, s, pl.ds(start, KW_DIL), :]
            rows = (slice(qb * QB_DIL, (qb + 1) * QB_DIL) if r == 1
                    else pl.ds(qb * QB_DIL * r + s, QB_DIL, stride=r))
            for j in range(HB // 2):
                cols = slice(j * LANES, (j + 1) * LANES)
                qpair, kp, vp = q[:, cols], kw[:, cols], vw[:, cols]
                outs, lses = [], []
                for c in range(2):
                    qc = jnp.where(low_half if c == 0 else jnp.logical_not(low_half), qpair,
                                   jnp.zeros_like(qpair))
                    sc = lax.dot_general(qc, kp, (((1,), (1,)), ((), ())),
                                         preferred_element_type=F32)
                    sc = sc + bias_ref[2 * j + c, variant]
                    m = jnp.max(sc, axis=-1, keepdims=True)
                    p = jnp.exp(sc - m)
                    l = jnp.sum(p, axis=-1, keepdims=True)
                    outs.append(jnp.dot(p.astype(BF16), vp, preferred_element_type=F32) / l)
                    lses.append(m + jnp.log(l))
                o_ref[0, j, rows, :] = jnp.where(low_half, outs[0], outs[1])
                lse_ref[0, j, rows, :] = jnp.where(low_half, lses[0], lses[1])


def _attn_b(qkv4, bias3, g, cols):
    B, r, L, _ = qkv4.shape
    S = r * L
    width = HB * DB
    nblk = L // QB_DIL
    sp = min(r, SUBS_DIL)
    qp = max(1, min(ITEMS_DIL // sp, OUT_ROWS_DIL // (QB_DIL * r)))
    qcol, kcol, vcol = cols
    kern = functools.partial(_attn_b_kernel, sub_len=L, r=r, sp=sp, qp=qp)
    slab = jax.ShapeDtypeStruct((B, N_SLABS, S, LANES), F32)
    slab_spec = pl.BlockSpec((1, N_SLABS, QB_DIL * r * qp, LANES), lambda b, i, s: (b, 0, i, 0))
    return pl.pallas_call(
        kern,
        out_shape=[slab, slab],
        grid=(B, nblk // qp, r // sp),
        in_specs=[
            pl.BlockSpec((1, sp, QB_DIL * qp, width), lambda b, i, s: (b, s, i, qcol)),
            pl.BlockSpec((1, r, L, width), lambda b, i, s: (b, 0, 0, kcol)),
            pl.BlockSpec((1, r, L, width), lambda b, i, s: (b, 0, 0, vcol)),
            pl.BlockSpec((HB, 3, QB_DIL, KW_DIL), lambda b, i, s: (0, 0, 0, 0)),
        ],
        out_specs=[slab_spec, slab_spec],
        compiler_params=_cparams(("parallel", "arbitrary", "arbitrary")),
        name=f"dilated_attn_{g}",
    )(qkv4, qkv4, qkv4, bias3)


def _attn_b_bias(rel_bias, g):
    r = DILATIONS[g]
    tab = rel_bias[:, HA + g * HB: HA + (g + 1) * HB].astype(F32)
    off = jnp.arange(3, dtype=jnp.int32)[:, None, None] * HALF_WIN
    rel = (jnp.arange(KW_DIL, dtype=jnp.int32)[None, None, :] - off
           - jnp.arange(QB_DIL, dtype=jnp.int32)[None, :, None])
    bias = _bias_lookup(_t5_bucket(rel * r), tab)
    return jnp.where((jnp.abs(rel) <= HALF_WIN)[None], bias, NEG)


def _sgu_kernel(zu_ref, zv_ref, lng_ref, lnb_ref, ws_ref, bs_ref, o_ref):
    u = jax.nn.gelu(zu_ref[...].astype(F32))
    v = jax.nn.gelu(zv_ref[...].astype(F32))
    mu = jnp.mean(v, axis=-1, keepdims=True)
    var = jnp.mean(jnp.square(v - mu), axis=-1, keepdims=True)
    v = ((v - mu) * lax.rsqrt(var + EPS) * lng_ref[...] + lnb_ref[...]).astype(BF16)
    gd = v.shape[1] // C_GROUPS
    for n in range(v.shape[0] // CHUNK):
        rows = slice(n * CHUNK, (n + 1) * CHUNK)
        for g in range(C_GROUPS):
            cols = slice(g * gd, (g + 1) * gd)
            mixed = jnp.dot(ws_ref[g], v[rows, cols], preferred_element_type=F32) + bs_ref[:, cols]
            o_ref[rows, cols] = (u[rows, cols] * mixed).astype(o_ref.dtype)


def _sgu(proj2, ln_g, ln_b, w_s, b_exp):
    T = proj2.shape[0]
    tm = min(TM_SGU, T)
    w = MIX_W
    return pl.pallas_call(
        _sgu_kernel,
        out_shape=jax.ShapeDtypeStruct((T, w), BF16),
        grid=(T // tm,),
        in_specs=[pl.BlockSpec((tm, w), lambda i: (i, COL_ZU)),
                  pl.BlockSpec((tm, w), lambda i: (i, COL_ZU + 1)),
                  pl.BlockSpec((1, w), lambda i: (0, 0)),
                  pl.BlockSpec((1, w), lambda i: (0, 0)),
                  pl.BlockSpec((C_GROUPS, CHUNK, CHUNK), lambda i: (0, 0, 0)),
                  pl.BlockSpec((CHUNK, w), lambda i: (0, 0))],
        out_specs=pl.BlockSpec((tm, w), lambda i: (i, 0)),
        compiler_params=_cparams(("parallel",)),
        name="sgu",
    )(proj2, proj2, ln_g, ln_b, w_s, b_exp)


def _route(logits):
    lane = lax.broadcasted_iota(jnp.int32, logits.shape, 1)
    big = jnp.int32(LANES)
    is_grp = (lane >= N_EXPERTS) & (lane < N_EXPERTS + N_GROUPS)
    gl = jnp.where(is_grp, logits, NEG)
    gmax = jnp.max(gl, axis=-1, keepdims=True)
    g_idx = jnp.min(jnp.where(is_grp & (gl == gmax), lane, big), axis=-1, keepdims=True) - N_EXPERTS
    g_w = 1.0 / jnp.sum(jnp.where(is_grp, jnp.exp(gl - gmax), 0.0), axis=-1, keepdims=True)
    in_grp = (lane >= g_idx * E_PER_GROUP) & (lane < (g_idx + 1) * E_PER_GROUP)
    sel = jnp.where(in_grp, logits, NEG)
    v1 = jnp.max(sel, axis=-1, keepdims=True)
    i1 = jnp.min(jnp.where(in_grp & (sel == v1), lane, big), axis=-1, keepdims=True)
    rest = in_grp & (lane != i1)
    sel2 = jnp.where(rest, logits, NEG)
    v2 = jnp.max(sel2, axis=-1, keepdims=True)
    i2 = jnp.min(jnp.where(rest & (sel2 == v2), lane, big), axis=-1, keepdims=True)
    e2 = jnp.exp(v2 - v1)
    w1 = g_w / (1.0 + e2)
    w2 = g_w * e2 / (1.0 + e2)
    return jnp.where(lane == i1, w1, jnp.where(lane == i2, w2, 0.0))


def _mix_kernel(x_ref, ya_ref, ob0_ref, ob1_ref, ob2_ref, ls0_ref, ls1_ref, ls2_ref, yc_ref,
                g0_ref, g1_ref, g2_ref, wb_ref, wo_ref, nf_ref, wr_ref, br_ref,
                xo_ref, h_ref, comb_ref):
    slabs = []
    for j in range(N_SLABS):
        ls0, ls1, ls2 = ls0_ref[0, j], ls1_ref[0, j], ls2_ref[0, j]
        mx = jnp.maximum(jnp.maximum(ls0, ls1), ls2)
        e0, e1, e2 = jnp.exp(ls0 - mx), jnp.exp(ls1 - mx), jnp.exp(ls2 - mx)
        yb = (e0 * ob0_ref[0, j] + e1 * ob1_ref[0, j] + e2 * ob2_ref[0, j]) / (e0 + e1 + e2)
        slabs.append(yb.astype(BF16))
    yb = jnp.concatenate(slabs, axis=-1)
    merged = jax.nn.sigmoid(g0_ref[...].astype(F32)) * jnp.dot(ya_ref[...], wb_ref[0],
                                                               preferred_element_type=F32)
    merged += jax.nn.sigmoid(g1_ref[...].astype(F32)) * jnp.dot(yb, wb_ref[1],
                                                                preferred_element_type=F32)
    merged += jax.nn.sigmoid(g2_ref[...].astype(F32)) * jnp.dot(yc_ref[...], wb_ref[2],
                                                                preferred_element_type=F32)
    xn = x_ref[...] + jnp.dot(merged.astype(BF16), wo_ref[...], preferred_element_type=F32)
    xo_ref[...] = xn
    h = _rms_bf16(xn, nf_ref[...])
    h_ref[...] = h
    logits = jnp.dot(h, wr_ref[...], preferred_element_type=F32) + br_ref[...]
    comb_ref[...] = _route(logits)


def _mix(x2, ya, obs, lses, yc, proj2, wb, wo, nf, wr, br):
    T, D = x2.shape
    S = obs[0].shape[2]
    tm = min(TM_MIX, S)
    per_b = S // tm
    w = MIX_W
    row = lambda width: pl.BlockSpec((tm, width), lambda i: (i, 0))
    full = lambda a: pl.BlockSpec(a.shape, lambda i: (0,) * a.ndim)
    gate = lambda n: pl.BlockSpec((tm, D), lambda i: (i, COL_GATE + n))
    slab = pl.BlockSpec((1, N_SLABS, tm, LANES), lambda i: (i // per_b, 0, i % per_b, 0))
    return pl.pallas_call(
        _mix_kernel,
        out_shape=[jax.ShapeDtypeStruct((T, D), F32), jax.ShapeDtypeStruct((T, D), BF16),
                   jax.ShapeDtypeStruct((T, LANES), F32)],
        grid=(T // tm,),
        in_specs=[row(D), row(w), slab, slab, slab, slab, slab, slab, row(w),
                  gate(0), gate(1), gate(2), full(wb), full(wo), full(nf), full(wr), full(br)],
        out_specs=[row(D), row(D), row(LANES)],
        compiler_params=_cparams(("parallel",)),
        name="mix",
    )(x2, ya, obs[0], obs[1], obs[2], lses[0], lses[1], lses[2], yc, proj2, proj2, proj2,
      wb, wo, nf, wr, br)


def _moe_kernel(h_ref, comb_ref, x_ref, wg_ref, wu_ref, wd_ref, nfin_ref, o_ref, acc_scr,
                *, final_norm):
    e = pl.program_id(1)

    @pl.when(e == 0)
    def _():
        acc_scr[...] = jnp.zeros(acc_scr.shape, F32)

    h = h_ref[...]
    lane = lax.broadcasted_iota(jnp.int32, comb_ref.shape, 1)
    c = jnp.sum(jnp.where(lane == e, comb_ref[...], 0.0), axis=-1, keepdims=True)
    hid = (jax.nn.silu(jnp.dot(h, wg_ref[0].astype(BF16), preferred_element_type=F32))
           * jnp.dot(h, wu_ref[0].astype(BF16), preferred_element_type=F32))
    acc_scr[...] += c * jnp.dot(hid.astype(BF16), wd_ref[0].astype(BF16),
                                preferred_element_type=F32)

    @pl.when(e == pl.num_programs(1) - 1)
    def _():
        xn = x_ref[...] + acc_scr[...]
        if final_norm:
            ms = jnp.mean(xn * xn, axis=-1, keepdims=True)
            xn = xn * lax.rsqrt(ms + EPS) * nfin_ref[...]
        o_ref[...] = xn


def _moe_dense(h, comb, x2, wg, wu, wd, nfin, final_norm):
    T, D = x2.shape
    tm = min(TM_MOE, T)
    F = wg.shape[2]
    kern = functools.partial(_moe_kernel, final_norm=final_norm)
    return pl.pallas_call(
        kern,
        out_shape=jax.ShapeDtypeStruct((T, D), F32),
        grid=(T // tm, N_EXPERTS),
        in_specs=[pl.BlockSpec((tm, D), lambda i, e: (i, 0)),
                  pl.BlockSpec((tm, LANES), lambda i, e: (i, 0)),
                  pl.BlockSpec((tm, D), lambda i, e: (i, 0)),
                  pl.BlockSpec((1, D, F), lambda i, e: (e, 0, 0)),
                  pl.BlockSpec((1, D, F), lambda i, e: (e, 0, 0)),
                  pl.BlockSpec((1, F, D), lambda i, e: (e, 0, 0)),
                  pl.BlockSpec((1, D), lambda i, e: (0, 0))],
        out_specs=pl.BlockSpec((tm, D), lambda i, e: (i, 0)),
        scratch_shapes=[pltpu.VMEM((tm, D), F32)],
        compiler_params=_cparams(("parallel", "arbitrary")),
        name="moe_dense",
    )(h, comb, x2, wg, wu, wd, nfin)


def _moe_dispatch_kernel(h_ref, comb_ref, o_ref, cnt_ref):
    nt = o_ref.shape[0]
    tm = h_ref.shape[0] // nt
    before = jnp.where(lax.broadcasted_iota(jnp.int32, (tm, tm), 0)
                       < lax.broadcasted_iota(jnp.int32, (tm, tm), 1), 1.0, 0.0).astype(BF16)
    slot = lax.broadcasted_iota(jnp.int32, (MOE_CAP, tm), 0).astype(F32)
    for u in range(nt):
        rows = slice(u * tm, (u + 1) * tm)
        comb = comb_ref[rows, :]
        hi = comb.astype(BF16)
        lo = (comb - hi.astype(F32)).astype(BF16)
        haug = jnp.concatenate([h_ref[rows, :], hi, lo], axis=1)
        a_t = comb.T[:N_EXPERTS] > 0.0
        a_f = jnp.where(a_t, 1.0, 0.0)
        rank_t = jnp.dot(a_f.astype(BF16), before, preferred_element_type=F32)
        blocks = [jnp.where((slot == rank_t[e:e + 1]) & a_t[e:e + 1], 1.0, 0.0).astype(BF16)
                  for e in range(N_EXPERTS)]
        res = jnp.dot(jnp.concatenate(blocks, axis=0), haug, preferred_element_type=F32)
        res = res.astype(o_ref.dtype)
        for e in range(N_EXPERTS):
            o_ref[u, e] = res[e * MOE_CAP:(e + 1) * MOE_CAP]
        cnt_ref[u] = jnp.broadcast_to(jnp.sum(a_f, axis=1, keepdims=True), cnt_ref.shape[1:])


def _moe_dispatch(h, comb):
    T, D = h.shape
    tm = min(TM_DISP, T)
    n = T // tm
    nt = math.gcd(TILES_PER_STEP, n)
    return pl.pallas_call(
        _moe_dispatch_kernel,
        out_shape=[jax.ShapeDtypeStruct((n, N_EXPERTS, MOE_CAP, D + 2 * LANES), BF16),
                   jax.ShapeDtypeStruct((n, N_EXPERTS, LANES), F32)],
        grid=(n // nt,),
        in_specs=[pl.BlockSpec((nt * tm, D), lambda i: (i, 0)),
                  pl.BlockSpec((nt * tm, LANES), lambda i: (i, 0))],
        out_specs=[pl.BlockSpec((nt, N_EXPERTS, MOE_CAP, D + 2 * LANES), lambda i: (i, 0, 0, 0)),
                   pl.BlockSpec((nt, N_EXPERTS, LANES), lambda i: (i, 0, 0))],
        compiler_params=_cparams(("parallel",)),
        name="moe_dispatch",
    )(h, comb)


def _moe_ffn_kernel(n16_ref, s_ref, wg_ref, wu_ref, wd_ref, o_ref,
                    wg_scr, wu_scr, wd_scr, lhs_scr, y_scr):
    e, c = pl.program_id(0), pl.program_id(1)

    @pl.when(c == 0)
    def _():
        wg_scr[...] = wg_ref[0].astype(BF16)
        wu_scr[...] = wu_ref[0].astype(BF16)
        wd_scr[...] = wd_ref[0].astype(BF16)

    g, _, cap, _ = s_ref.shape
    D = o_ref.shape[-1]
    @pl.when((e == 0) & (c == 0))
    def _():
        lhs_scr[...] = jnp.zeros(lhs_scr.shape, lhs_scr.dtype)
        y_scr[...] = jnp.zeros(y_scr.shape, y_scr.dtype)

    offs = []
    off = jnp.int32(0)
    for t in range(g):
        offs.append(off)
        lhs_scr[pl.ds(pl.multiple_of(off, BF16_ROWS), cap), :] = s_ref[t, 0]
        off = off + n16_ref[(c * g + t) * N_EXPERTS + e]
    total = off

    def run(nrows):
        rows = lhs_scr[:nrows]
        h = rows[:, :D]
        wparts = rows[:, D:].astype(F32)
        lane = lax.broadcasted_iota(jnp.int32, wparts.shape, 1)
        w = jnp.sum(jnp.where(lane % LANES == e, wparts, 0.0), axis=-1, keepdims=True)
        hid = (jax.nn.silu(jnp.dot(h, wg_scr[...], preferred_element_type=F32))
               * jnp.dot(h, wu_scr[...], preferred_element_type=F32))
        y = w * jnp.dot(hid.astype(BF16), wd_scr[...], preferred_element_type=F32)
        y_scr[:nrows] = y.astype(y_scr.dtype)

    classes = tuple(range(g * cap // 2, g * cap + 1, FFN_ROW_STEP))
    lower = 0
    for nrows in classes:
        pl.when((total > lower) & (total <= nrows))(functools.partial(run, nrows))
        lower = nrows

    for t in range(g):
        o_ref[t, 0] = y_scr[pl.ds(pl.multiple_of(offs[t], BF16_ROWS), cap), :]


def _moe_ffn(srt, n16, wg, wu, wd):
    n, ne, cap, wdt = srt.shape
    D, F = wg.shape[1], wg.shape[2]
    g = math.gcd(G_FFN, n)
    return pl.pallas_call(
        _moe_ffn_kernel,
        out_shape=jax.ShapeDtypeStruct((n, ne, cap, D), BF16),
        grid_spec=pltpu.PrefetchScalarGridSpec(
            num_scalar_prefetch=1,
            grid=(ne, n // g),
            in_specs=[pl.BlockSpec((g, 1, cap, wdt), lambda e, c, n16: (c, e, 0, 0)),
                      pl.BlockSpec((1, D, F), lambda e, c, n16: (e, 0, 0)),
                      pl.BlockSpec((1, D, F), lambda e, c, n16: (e, 0, 0)),
                      pl.BlockSpec((1, F, D), lambda e, c, n16: (e, 0, 0))],
            out_specs=pl.BlockSpec((g, 1, cap, D), lambda e, c, n16: (c, e, 0, 0)),
            scratch_shapes=[pltpu.VMEM((D, F), BF16), pltpu.VMEM((D, F), BF16),
                            pltpu.VMEM((F, D), BF16), pltpu.VMEM((g * cap, wdt), BF16),
                            pltpu.VMEM((g * cap, D), BF16)]),
        compiler_params=_cparams(("arbitrary", "arbitrary")),
        name="moe_ffn",
    )(n16, srt, wg, wu, wd)


def _moe_combine_kernel(y_ref, comb_ref, x_ref, nfin_ref, o_ref, *, final_norm):
    nt = y_ref.shape[0]
    tm = x_ref.shape[0] // nt
    ncol = N_EXPERTS * MOE_CAP
    before = jnp.where(lax.broadcasted_iota(jnp.int32, (tm, tm), 1)
                       < lax.broadcasted_iota(jnp.int32, (tm, tm), 0), 1.0, 0.0).astype(BF16)
    spread = jnp.where(lax.broadcasted_iota(jnp.int32, (LANES, ncol), 1) // MOE_CAP
                       == lax.broadcasted_iota(jnp.int32, (LANES, ncol), 0), 1.0, 0.0).astype(BF16)
    slot = (lax.broadcasted_iota(jnp.int32, (tm, ncol), 1) % MOE_CAP).astype(F32)
    for u in range(nt):
        rows = slice(u * tm, (u + 1) * tm)
        a = comb_ref[rows, :] > 0.0
        rank = jnp.dot(before, jnp.where(a, 1.0, 0.0).astype(BF16), preferred_element_type=F32)
        key = jnp.where(a, rank, -1.0).astype(BF16)
        key_all = jnp.dot(key, spread, preferred_element_type=F32)
        pc = jnp.where(slot == key_all, 1.0, 0.0).astype(BF16)
        y = jnp.concatenate([y_ref[u, e] for e in range(N_EXPERTS)], axis=0)
        xn = x_ref[rows, :] + jnp.dot(pc, y, preferred_element_type=F32)
        if final_norm:
            ms = jnp.mean(xn * xn, axis=-1, keepdims=True)
            xn = xn * lax.rsqrt(ms + EPS) * nfin_ref[...]
        o_ref[rows, :] = xn


def _moe_combine(y, comb, x2, nfin, final_norm):
    T, D = x2.shape
    n, ne, cap, _ = y.shape
    tm = T // n
    nt = math.gcd(TILES_PER_STEP, n)
    kern = functools.partial(_moe_combine_kernel, final_norm=final_norm)
    return pl.pallas_call(
        kern,
        out_shape=jax.ShapeDtypeStruct((T, D), F32),
        grid=(n // nt,),
        in_specs=[pl.BlockSpec((nt, ne, cap, D), lambda i: (i, 0, 0, 0)),
                  pl.BlockSpec((nt * tm, LANES), lambda i: (i, 0)),
                  pl.BlockSpec((nt * tm, D), lambda i: (i, 0)),
                  pl.BlockSpec((1, D), lambda i: (0, 0))],
        out_specs=pl.BlockSpec((nt * tm, D), lambda i: (i, 0)),
        compiler_params=_cparams(("parallel",)),
        name="moe_combine",
    )(y, comb, x2, nfin)


def _moe_fix_kernel(tiles_ref, experts_ref, first_ref, n_ref, h_ref, comb_ref, prev_ref,
                    wg_ref, wu_ref, wd_ref, o_ref):
    del tiles_ref
    s = pl.program_id(0)

    @pl.when(s < n_ref[0])
    def _():
        e = experts_ref[s]
        tm = h_ref.shape[0]
        comb = comb_ref[...]
        a = jnp.where(comb > 0.0, 1.0, 0.0)
        before = (lax.broadcasted_iota(jnp.int32, (tm, tm), 1)
                  < lax.broadcasted_iota(jnp.int32, (tm, tm), 0))
        rank = jnp.dot(jnp.where(before, 1.0, 0.0).astype(BF16), a.astype(BF16),
                       preferred_element_type=F32)
        lane = lax.broadcasted_iota(jnp.int32, comb.shape, 1)
        dropped = (lane == e) & (rank >= MOE_CAP)
        c = jnp.sum(jnp.where(dropped, comb, 0.0), axis=-1, keepdims=True)
        h = h_ref[...]
        hid = (jax.nn.silu(jnp.dot(h, wg_ref[0].astype(BF16), preferred_element_type=F32))
               * jnp.dot(h, wu_ref[0].astype(BF16), preferred_element_type=F32))
        add = c * jnp.dot(hid.astype(BF16), wd_ref[0].astype(BF16), preferred_element_type=F32)
        fresh = first_ref[s] == 1

        @pl.when(fresh)
        def _():
            o_ref[...] = prev_ref[...] + add

        @pl.when(jnp.logical_not(fresh))
        def _():
            o_ref[...] += add


def _moe_fix(tiles, experts, first, n, out, h, comb, wg, wu, wd):
    T, D = out.shape
    tm = min(TM_DISP, T)
    F = wg.shape[2]
    tile = lambda width: pl.BlockSpec((tm, width), lambda s, tl, ex, fi, n: (tl[s], 0))
    wspec = lambda shape: pl.BlockSpec(shape, lambda s, tl, ex, fi, n: (ex[s], 0, 0))
    return pl.pallas_call(
        _moe_fix_kernel,
        out_shape=jax.ShapeDtypeStruct((T, D), F32),
        grid_spec=pltpu.PrefetchScalarGridSpec(
            num_scalar_prefetch=4,
            grid=(MAX_OVF,),
            in_specs=[tile(D), tile(LANES), tile(D), wspec((1, D, F)), wspec((1, D, F)),
                      wspec((1, F, D))],
            out_specs=tile(D)),
        input_output_aliases={6: 0},
        compiler_params=_cparams(("arbitrary",)),
        name="moe_fix",
    )(tiles, experts, first, n, h, comb, out, wg, wu, wd)


def _final_norm_kernel(x_ref, g_ref, o_ref):
    x = x_ref[...]
    ms = jnp.mean(x * x, axis=-1, keepdims=True)
    o_ref[...] = x * lax.rsqrt(ms + EPS) * g_ref[...]


def _final_norm(x2, g):
    T, D = x2.shape
    tm = min(TM_PROJ, T)
    return pl.pallas_call(
        _final_norm_kernel,
        out_shape=jax.ShapeDtypeStruct((T, D), F32),
        grid=(T // tm,),
        in_specs=[pl.BlockSpec((tm, D), lambda i: (i, 0)), pl.BlockSpec((1, D), lambda i: (0, 0))],
        out_specs=pl.BlockSpec((tm, D), lambda i: (i, 0)),
        compiler_params=_cparams(("parallel",)),
        name="final_norm",
    )(x2, g)


def _moe(h, comb, x2, wg, wu, wd, nfin, final_norm):
    srt, cnt = _moe_dispatch(h, comb)
    over = (cnt[:, :, 0] > MOE_CAP).reshape(-1)
    n_ovf = jnp.sum(over.astype(jnp.int32))
    pairs = jnp.nonzero(over, size=MAX_OVF, fill_value=0)[0].astype(jnp.int32)
    pairs = jnp.where(jnp.arange(MAX_OVF) < n_ovf, pairs, pairs[jnp.clip(n_ovf - 1, 0, MAX_OVF - 1)])
    tiles, experts = pairs // N_EXPERTS, pairs % N_EXPERTS
    first = jnp.concatenate([jnp.ones((1,), jnp.int32),
                             (tiles[1:] != tiles[:-1]).astype(jnp.int32)])

    used = jnp.minimum(cnt[:, :, 0], MOE_CAP).astype(jnp.int32).reshape(-1)
    n16 = (used + (BF16_ROWS - 1)) // BF16_ROWS * BF16_ROWS

    def routed():
        y = _moe_ffn(srt, n16, wg, wu, wd)

        def fixed():
            out = _moe_combine(y, comb, x2, nfin, False)
            out = _moe_fix(tiles, experts, first, n_ovf.reshape(1), out, h, comb, wg, wu, wd)
            return _final_norm(out, nfin) if final_norm else out

        return lax.cond(n_ovf > 0, fixed, lambda: _moe_combine(y, comb, x2, nfin, final_norm))

    return lax.cond(n_ovf > MAX_OVF,
                    lambda: _moe_dense(h, comb, x2, wg, wu, wd, nfin, final_norm), routed)


def kernel(x, rel_bias, norm_mix, w_in, diff_lambda, diff_subln, sgu_ln_g, sgu_ln_b, sgu_w, sgu_b,
           w_branch, w_out, norm_ffn, w_router_grp, b_router_grp, w_router_exp, b_router_exp,
           w_gate, w_up, w_down, norm_final):
    B, S, D = x.shape
    T = B * S
    depth = w_in.shape[0]
    a_out = HA * 2 * DA
    grp_w = HB * DB
    b_cols = 3 * NG_B * grp_w
    qkv_b0 = 3 * a_out
    zc0 = qkv_b0 + b_cols
    gate0 = zc0 + 2 * MIX_W
    qk_scale = DA ** -0.5

    bias_a, cfar = _attn_a_bias(rel_bias)
    bias_b = [_attn_b_bias(rel_bias, g) for g in range(NG_B)]

    col = jnp.arange(w_in.shape[2])
    col_scale = jnp.where(col < a_out, qk_scale * LOG2E,
                          jnp.where((col >= qkv_b0) & (col < qkv_b0 + NG_B * grp_w), qk_scale, 1.0))

    def group_cols(w, g):
        return [w[:, qkv_b0 + (c * NG_B + g) * grp_w: qkv_b0 + (c * NG_B + g + 1) * grp_w]
                for c in range(3)]

    x2 = x.reshape(T, D)
    for i in range(depth):
        w = (w_in[i] * col_scale.astype(F32)).astype(BF16)
        nm = norm_mix[i][None, :]
        w_main = jnp.concatenate([w[:, :2 * a_out], w[:, zc0:]] + group_cols(w, 0), axis=1)
        proj2 = _inproj(x2, nm, w_main)
        x3 = x2.reshape(B, S, D)
        vt = _inproj_t(x3, nm, w[:, 2 * a_out:3 * a_out].T)
        proj3 = proj2.reshape(B, S, proj2.shape[1])

        lam_init = 0.8 - 0.6 * math.exp(-0.3 * i)
        lp = diff_lambda[i].astype(F32)
        lam = jnp.exp(jnp.sum(lp[0] * lp[1])) - jnp.exp(jnp.sum(lp[2] * lp[3])) + lam_init
        ya = _attn_a(proj3, vt, lam.reshape(1), cfar, bias_a, diff_subln[i][None, :], lam_init)

        obs, lses = [], []
        for g in range(NG_B):
            r = DILATIONS[g]
            if r == 1:
                qkv4, cols = proj3[:, None], (COL_QKV0, COL_QKV0 + 1, COL_QKV0 + 2)
            else:
                w_g = jnp.concatenate(group_cols(w, g), axis=1)
                qkv4, cols = _inproj_perm(x3, nm, w_g, r), (0, 1, 2)
            o, l = _attn_b(qkv4, bias_b[g], g, cols)
            obs.append(o)
            lses.append(l)

        b_exp = jnp.repeat(sgu_b[i].T, MIX_W // C_GROUPS, axis=1)
        yc = _sgu(proj2, sgu_ln_g[i][None, :], sgu_ln_b[i][None, :], sgu_w[i].astype(BF16), b_exp)

        wr = jnp.concatenate([w_router_exp[i].transpose(1, 0, 2).reshape(D, N_EXPERTS),
                              w_router_grp[i]], axis=1)
        wr = jnp.pad(wr, ((0, 0), (0, LANES - wr.shape[1]))).astype(BF16)
        br = jnp.concatenate([b_router_exp[i].reshape(N_EXPERTS), b_router_grp[i]])
        br = jnp.pad(br, (0, LANES - br.shape[0]))[None, :].astype(F32)

        x2, h, comb = _mix(x2, ya.reshape(T, a_out), obs, lses, yc, proj2,
                           w_branch[i].astype(BF16), w_out[i].astype(BF16), norm_ffn[i][None, :],
                           wr, br)
        x2 = _moe(h, comb, x2, w_gate[i], w_up[i], w_down[i], norm_final[None, :], i == depth - 1)
    return x2.reshape(B, S, D)
```

```python
import functools
import math

import jax
import jax.numpy as jnp
from jax import lax
from jax.experimental import pallas as pl
from jax.experimental.pallas import tpu as pltpu

F32 = jnp.float32
BF16 = jnp.bfloat16

EPS = 1e-6
NEG = -1e30
LOG2E = 1.4426950408889634
LANES = 128
HALF_LANES = LANES // 2
VMEM_LIMIT = 48 * 1024 * 1024

HA = 4
DA = 64
MIX_W = 512
WINDOWS = (128, 512, 2048)
DILATIONS = (1, 4, 16)
NG_B = 3
HB = 8
DB = 64
HALF_WIN = 64
CHUNK = 128
C_GROUPS = 4
N_BRANCH = 3
N_BUCKETS = 32
MAX_DIST = 128
N_GROUPS = 4
E_PER_GROUP = 4
N_EXPERTS = N_GROUPS * E_PER_GROUP
N_SLABS = MIX_W // LANES

TM_PROJ = 1024
TN_PROJ = 3328
TM_PERM = 1024
PERM_BLK = 256
T_ATT = 512
QB_DIL = 128
KW_DIL = QB_DIL + 2 * HALF_WIN
ITEMS_DIL = 8
SUBS_DIL = 4
OUT_ROWS_DIL = 2048
TM_SGU = 2048
TM_MIX = 512
TM_MOE = 1024
TM_DISP = 256
MOE_CAP = HALF_LANES
G_FFN = 16
TILES_PER_STEP = 4
FFN_ROW_STEP = 64
BF16_ROWS = 16
MAX_OVF = 64

COL_ZU = 2
COL_GATE = 2
COL_QKV0 = 10


def _cparams(sem):
    return pltpu.CompilerParams(dimension_semantics=sem, vmem_limit_bytes=VMEM_LIMIT)


def _t5_bucket(rel):
    nb = N_BUCKETS // 2
    max_exact = nb // 2
    ret = (rel > 0).astype(jnp.int32) * nb
    n = jnp.abs(rel)
    nf = jnp.maximum(n, 1).astype(F32)
    large = max_exact + (jnp.log(nf / max_exact) / math.log(MAX_DIST / max_exact)
                         * (nb - max_exact)).astype(jnp.int32)
    large = jnp.minimum(large, nb - 1)
    return ret + jnp.where(n < max_exact, n, large)


def _bias_lookup(bucket, tab):
    out = jnp.zeros((tab.shape[1],) + bucket.shape, F32)
    expand = (slice(None),) + (None,) * bucket.ndim
    for b in range(N_BUCKETS):
        out = jnp.where(bucket[None] == b, tab[b][expand], out)
    return out


def _rms_bf16(x, g):
    ms = jnp.mean(x * x, axis=-1, keepdims=True)
    return (x * lax.rsqrt(ms + EPS) * g).astype(BF16)


def _inproj_kernel(x_ref, g_ref, w_ref, o_ref, h_scr):
    @pl.when(pl.program_id(1) == 0)
    def _():
        h_scr[...] = _rms_bf16(x_ref[...], g_ref[...])

    o_ref[...] = jnp.dot(h_scr[...], w_ref[...], preferred_element_type=F32).astype(o_ref.dtype)


def _inproj(x2, g, w):
    T, D = x2.shape
    N = w.shape[1]
    tm = min(TM_PROJ, T)
    return pl.pallas_call(
        _inproj_kernel,
        out_shape=jax.ShapeDtypeStruct((T, N), BF16),
        grid=(T // tm, N // TN_PROJ),
        in_specs=[pl.BlockSpec((tm, D), lambda i, j: (i, 0)),
                  pl.BlockSpec((1, D), lambda i, j: (0, 0)),
                  pl.BlockSpec((D, TN_PROJ), lambda i, j: (0, j))],
        out_specs=pl.BlockSpec((tm, TN_PROJ), lambda i, j: (i, j)),
        scratch_shapes=[pltpu.VMEM((tm, D), BF16)],
        compiler_params=_cparams(("parallel", "arbitrary")),
        name="inproj",
    )(x2, g, w)


def _inproj_t_kernel(x_ref, g_ref, wt_ref, o_ref):
    h = _rms_bf16(x_ref[0], g_ref[...])
    res = lax.dot_general(wt_ref[...], h, (((1,), (1,)), ((), ())),
                          preferred_element_type=F32).astype(o_ref.dtype)
    for hd in range(o_ref.shape[1]):
        for n in range(o_ref.shape[2]):
            o_ref[0, hd, n] = res[hd * LANES:(hd + 1) * LANES, n * T_ATT:(n + 1) * T_ATT]


def _inproj_t(x3, g, wt):
    B, S, D = x3.shape
    N = wt.shape[0]
    tm = min(TM_PROJ, S)
    nh, nb = N // LANES, tm // T_ATT
    return pl.pallas_call(
        _inproj_t_kernel,
        out_shape=jax.ShapeDtypeStruct((B, nh, S // T_ATT, LANES, T_ATT), BF16),
        grid=(B, S // tm),
        in_specs=[pl.BlockSpec((1, tm, D), lambda b, i: (b, i, 0)),
                  pl.BlockSpec((1, D), lambda b, i: (0, 0)),
                  pl.BlockSpec((N, D), lambda b, i: (0, 0))],
        out_specs=pl.BlockSpec((1, nh, nb, LANES, T_ATT), lambda b, i: (b, 0, i, 0, 0)),
        compiler_params=_cparams(("parallel", "parallel")),
        name="inproj_t",
    )(x3, g, wt)


def _inproj_perm_kernel(x_ref, g_ref, p_ref, w_ref, o_ref, *, r):
    h = _rms_bf16(x_ref[0], g_ref[...])
    nblk = h.shape[0] // PERM_BLK
    hp = jnp.concatenate(
        [jnp.dot(p_ref[...], h[k * PERM_BLK:(k + 1) * PERM_BLK], preferred_element_type=F32)
         for k in range(nblk)], axis=0).astype(BF16)
    res = jnp.dot(hp, w_ref[...], preferred_element_type=F32).astype(o_ref.dtype)
    n = PERM_BLK // r
    for k in range(nblk):
        for s in range(r):
            o_ref[0, s, k * n:(k + 1) * n, :] = res[k * PERM_BLK + s * n:k * PERM_BLK + (s + 1) * n, :]


def _inproj_perm(x3, g, w, r):
    B, S, D = x3.shape
    N = w.shape[1]
    tm = min(TM_PERM, S)
    n = PERM_BLK // r
    o = jnp.arange(PERM_BLK, dtype=jnp.int32)
    src = (o % n) * r + o // n
    perm = (src[:, None] == jnp.arange(PERM_BLK, dtype=jnp.int32)[None, :]).astype(BF16)
    kern = functools.partial(_inproj_perm_kernel, r=r)
    return pl.pallas_call(
        kern,
        out_shape=jax.ShapeDtypeStruct((B, r, S // r, N), BF16),
        grid=(B, S // tm),
        in_specs=[pl.BlockSpec((1, tm, D), lambda b, i: (b, i, 0)),
                  pl.BlockSpec((1, D), lambda b, i: (0, 0)),
                  pl.BlockSpec((PERM_BLK, PERM_BLK), lambda b, i: (0, 0)),
                  pl.BlockSpec((D, N), lambda b, i: (0, 0))],
        out_specs=pl.BlockSpec((1, r, tm // r, N), lambda b, i: (b, 0, i, 0)),
        compiler_params=_cparams(("parallel", "parallel")),
        name=f"inproj_perm_{r}",
    )(x3, g, perm, w)


def _attn_a_kernel(lam_ref, cfar_ref, q_ref, k_ref, vt_ref, bias_ref, g_ref, o_ref,
                   st0_scr, st1_scr, m0_scr, m1_scr, acc_scr, l_scr, *, out_scale, nq, n_blocks):
    k = pl.program_id(0)
    t = T_ATT
    nk = k_ref.shape[1] // t
    n1 = jnp.minimum(k // 2, n_blocks - 1)
    n2 = jnp.maximum(k - 1, 0) // 2
    h1, qi1 = (n1 // nq) % HA, n1 % nq
    h2, qi2 = (n2 // nq) % HA, n2 % nq

    @pl.when(k == 0)
    def _():
        st1_scr[...] = jnp.zeros(st1_scr.shape, F32)
        m1_scr[...] = jnp.zeros(m1_scr.shape, F32)
        acc_scr[...] = jnp.zeros(acc_scr.shape, F32)
        l_scr[...] = jnp.ones(l_scr.shape, F32)

    low_half = lax.broadcasted_iota(jnp.int32, (1, LANES), 1) < HALF_LANES

    def both(cmap, st_w, m_w, st_r, m_r):
        q = q_ref[0]
        zero = jnp.zeros_like(q)
        qc = jnp.where(low_half, q, zero) if cmap == 0 else jnp.where(low_half, zero, q)
        m_prev = m_r[...]
        l = jnp.zeros((1, t), F32)
        acc = jnp.zeros((LANES, t), F32)
        m_new = None
        for j, d in enumerate(range(-1, nk - 1)):
            a1 = lax.rem(qi1 + (d + nk), nk)
            delta1 = a1 - qi1
            kb = k_ref[0, pl.ds(pl.multiple_of(a1 * t, t), t), :]
            st = lax.dot_general(kb, qc, (((1,), (1,)), ((), ())), preferred_element_type=F32)
            if d <= 1:
                st = st + bias_ref[0, jnp.clip(delta1, -2, 2) + 2]
                cm = jnp.max(st, axis=0, keepdims=True)
            else:
                cm = (jnp.max(st, axis=0, keepdims=True)
                      + cfar_ref[2 * h1 + (delta1 > 0).astype(jnp.int32)])
            st_w[j] = st
            m_new = cm if m_new is None else jnp.maximum(m_new, cm)

            a2 = lax.rem(qi2 + (d + nk), nk)
            if d <= 1:
                shifted = m_prev
            else:
                shifted = m_prev - cfar_ref[2 * h2 + (a2 > qi2).astype(jnp.int32)]
            p = jnp.exp2(st_r[j] - shifted)
            l = l + jnp.sum(p, axis=0, keepdims=True)
            acc = acc + jnp.dot(vt_ref[0, 0, a2], p.astype(BF16), preferred_element_type=F32)
        m_w[...] = m_new
        return l, acc

    @pl.when(k % 2 == 0)
    def _():
        l1, acc1 = both(0, st0_scr, m0_scr, st1_scr, m1_scr)
        ot = acc_scr[...] / l_scr[...] - lam_ref[0] * (acc1 / l1)
        o = ot.T
        ms = jnp.mean(o * o, axis=-1, keepdims=True)
        o_ref[0] = (o * lax.rsqrt(ms + EPS) * g_ref[...] * out_scale).astype(o_ref.dtype)

    @pl.when(k % 2 == 1)
    def _():
        l0, acc0 = both(1, st1_scr, m1_scr, st0_scr, m0_scr)
        acc_scr[...] = acc0
        l_scr[...] = l0


def _attn_a(proj3, vt, lam, cfar, bias5, subln_g, lam_init):
    B, S, _ = proj3.shape
    t = T_ATT
    nk = S // t
    n_blocks = B * HA * nk

    def scored(k):
        n = jnp.minimum(k // 2, n_blocks - 1)
        return n // (HA * nk), (n // nk) % HA, n % nk

    def lagged(k, lag):
        n = jnp.maximum(k - lag, 0) // 2
        return n // (HA * nk), (n // nk) % HA, n % nk

    def q_map(k):
        b, h, qi = scored(k)
        return b, qi, h

    def k_map(k):
        b, h, _ = scored(k)
        return b, 0, HA + h

    def vt_map(k):
        b, h, _ = lagged(k, 1)
        return b, h, 0, 0, 0

    def out_map(k):
        b, h, qi = lagged(k, 2)
        return b, qi, h

    kern = functools.partial(_attn_a_kernel, out_scale=1.0 - lam_init, nq=nk, n_blocks=n_blocks)
    return pl.pallas_call(
        kern,
        out_shape=jax.ShapeDtypeStruct((B, S, HA * 2 * DA), BF16),
        grid=(2 * n_blocks + 1,),
        in_specs=[
            pl.BlockSpec(memory_space=pltpu.SMEM),
            pl.BlockSpec(memory_space=pltpu.SMEM),
            pl.BlockSpec((1, t, LANES), q_map),
            pl.BlockSpec((1, S, LANES), k_map),
            pl.BlockSpec((1, 1, nk, LANES, t), vt_map),
            pl.BlockSpec((1, 5, t, t), lambda k: (scored(k)[1], 0, 0, 0)),
            pl.BlockSpec((1, LANES), lambda k: (0, 0)),
        ],
        out_specs=pl.BlockSpec((1, t, LANES), out_map),
        scratch_shapes=[pltpu.VMEM((nk, t, t), F32), pltpu.VMEM((nk, t, t), F32),
                        pltpu.VMEM((1, t), F32), pltpu.VMEM((1, t), F32),
                        pltpu.VMEM((LANES, t), F32), pltpu.VMEM((1, t), F32)],
        compiler_params=_cparams(("arbitrary",)),
        name="diff_attn",
    )(lam, cfar, proj3, proj3, vt, bias5, subln_g)


def _attn_a_bias(rel_bias):
    t = T_ATT
    tab = rel_bias[:, :HA].astype(F32) * LOG2E
    d = jnp.arange(-1, 2, dtype=jnp.int32)[:, None, None] * t
    rel = d + jnp.arange(t, dtype=jnp.int32)[None, :, None] - jnp.arange(t, dtype=jnp.int32)[None, None, :]
    near = _bias_lookup(_t5_bucket(rel), tab)
    far = tab[_t5_bucket(jnp.array([-(t + 1), t + 1], dtype=jnp.int32))].T
    fill = lambda side: jnp.broadcast_to(far[:, side, None, None, None], (HA, 1, t, t))
    tiles = jnp.concatenate([fill(0), near, fill(1)], axis=1)
    return tiles, far.reshape(2 * HA)


def _attn_b_kernel(q_ref, k_ref, v_ref, bias_ref, o_ref, lse_ref, *, sub_len, r, sp, qp):
    nblk = sub_len // QB_DIL
    low_half = lax.broadcasted_iota(jnp.int32, (1, LANES), 1) < HALF_LANES
    for si in range(sp):
        s = si if sp == r else pl.program_id(2) * sp + si
        for qb in range(qp):
            i = pl.program_id(1) * qp + qb
            start = jnp.clip(i * QB_DIL - HALF_WIN, 0, sub_len - KW_DIL)
            start = pl.multiple_of(start, HALF_WIN)
            variant = jnp.where(i == 0, 0, jnp.where(i == nblk - 1, 2, 1))
            q = q_ref[0, si, qb * QB_DIL:(qb + 1) * QB_DIL, :]
            kw = k_ref[0, s, pl.ds(start, KW_DIL), :]
            vw = v_ref[0, s, pl.ds(start, KW_DIL), :]
            rows = (slice(qb * QB_DIL, (qb + 1) * QB_DIL) if r == 1
                    else pl.ds(qb * QB_DIL * r + s, QB_DIL, stride=r))
            for j in range(HB // 2):
                cols = slice(j * LANES, (j + 1) * LANES)
                qpair, kp, vp = q[:, cols], kw[:, cols], vw[:, cols]
                outs, lses = [], []
                for c in range(2):
                    qc = jnp.where(low_half if c == 0 else jnp.logical_not(low_half), qpair,
                                   jnp.zeros_like(qpair))
                    sc = lax.dot_general(qc, kp, (((1,), (1,)), ((), ())),
                                         preferred_element_type=F32)
                    sc = sc + bias_ref[2 * j + c, variant]
                    m = jnp.max(sc, axis=-1, keepdims=True)
                    p = jnp.exp(sc - m)
                    l = jnp.sum(p, axis=-1, keepdims=True)
                    outs.append(jnp.dot(p.astype(BF16), vp, preferred_element_type=F32) / l)
                    lses.append(m + jnp.log(l))
                o_ref[0, j, rows, :] = jnp.where(low_half, outs[0], outs[1])
                lse_ref[0, j, rows, :] = jnp.where(low_half, lses[0], lses[1])


def _attn_b(qkv4, bias3, g, cols):
    B, r, L, _ = qkv4.shape
    S = r * L
    width = HB * DB
    nblk = L // QB_DIL
    sp = min(r, SUBS_DIL)
    qp = max(1, min(ITEMS_DIL // sp, OUT_ROWS_DIL // (QB_DIL * r)))
    qcol, kcol, vcol = cols
    kern = functools.partial(_attn_b_kernel, sub_len=L, r=r, sp=sp, qp=qp)
    slab = jax.ShapeDtypeStruct((B, N_SLABS, S, LANES), F32)
    slab_spec = pl.BlockSpec((1, N_SLABS, QB_DIL * r * qp, LANES), lambda b, i, s: (b, 0, i, 0))
    return pl.pallas_call(
        kern,
        out_shape=[slab, slab],
        grid=(B, nblk // qp, r // sp),
        in_specs=[
            pl.BlockSpec((1, sp, QB_DIL * qp, width), lambda b, i, s: (b, s, i, qcol)),
            pl.BlockSpec((1, r, L, width), lambda b, i, s: (b, 0, 0, kcol)),
            pl.BlockSpec((1, r, L, width), lambda b, i, s: (b, 0, 0, vcol)),
            pl.BlockSpec((HB, 3, QB_DIL, KW_DIL), lambda b, i, s: (0, 0, 0, 0)),
        ],
        out_specs=[slab_spec, slab_spec],
        compiler_params=_cparams(("parallel", "arbitrary", "arbitrary")),
        name=f"dilated_attn_{g}",
    )(qkv4, qkv4, qkv4, bias3)


def _attn_b_bias(rel_bias, g):
    r = DILATIONS[g]
    tab = rel_bias[:, HA + g * HB: HA + (g + 1) * HB].astype(F32)
    off = jnp.arange(3, dtype=jnp.int32)[:, None, None] * HALF_WIN
    rel = (jnp.arange(KW_DIL, dtype=jnp.int32)[None, None, :] - off
           - jnp.arange(QB_DIL, dtype=jnp.int32)[None, :, None])
    bias = _bias_lookup(_t5_bucket(rel * r), tab)
    return jnp.where((jnp.abs(rel) <= HALF_WIN)[None], bias, NEG)


def _sgu_kernel(zu_ref, zv_ref, lng_ref, lnb_ref, ws_ref, bs_ref, o_ref):
    u = jax.nn.gelu(zu_ref[...].astype(F32))
    v = jax.nn.gelu(zv_ref[...].astype(F32))
    mu = jnp.mean(v, axis=-1, keepdims=True)
    var = jnp.mean(jnp.square(v - mu), axis=-1, keepdims=True)
    v = ((v - mu) * lax.rsqrt(var + EPS) * lng_ref[...] + lnb_ref[...]).astype(BF16)
    gd = v.shape[1] // C_GROUPS
    for n in range(v.shape[0] // CHUNK):
        rows = slice(n * CHUNK, (n + 1) * CHUNK)
        for g in range(C_GROUPS):
            cols = slice(g * gd, (g + 1) * gd)
            mixed = jnp.dot(ws_ref[g], v[rows, cols], preferred_element_type=F32) + bs_ref[:, cols]
            o_ref[rows, cols] = (u[rows, cols] * mixed).astype(o_ref.dtype)


def _sgu(proj2, ln_g, ln_b, w_s, b_exp):
    T = proj2.shape[0]
    tm = min(TM_SGU, T)
    w = MIX_W
    return pl.pallas_call(
        _sgu_kernel,
        out_shape=jax.ShapeDtypeStruct((T, w), BF16),
        grid=(T // tm,),
        in_specs=[pl.BlockSpec((tm, w), lambda i: (i, COL_ZU)),
                  pl.BlockSpec((tm, w), lambda i: (i, COL_ZU + 1)),
                  pl.BlockSpec((1, w), lambda i: (0, 0)),
                  pl.BlockSpec((1, w), lambda i: (0, 0)),
                  pl.BlockSpec((C_GROUPS, CHUNK, CHUNK), lambda i: (0, 0, 0)),
                  pl.BlockSpec((CHUNK, w), lambda i: (0, 0))],
        out_specs=pl.BlockSpec((tm, w), lambda i: (i, 0)),
        compiler_params=_cparams(("parallel",)),
        name="sgu",
    )(proj2, proj2, ln_g, ln_b, w_s, b_exp)


def _route(logits):
    lane = lax.broadcasted_iota(jnp.int32, logits.shape, 1)
    big = jnp.int32(LANES)
    is_grp = (lane >= N_EXPERTS) & (lane < N_EXPERTS + N_GROUPS)
    gl = jnp.where(is_grp, logits, NEG)
    gmax = jnp.max(gl, axis=-1, keepdims=True)
    g_idx = jnp.min(jnp.where(is_grp & (gl == gmax), lane, big), axis=-1, keepdims=True) - N_EXPERTS
    g_w = 1.0 / jnp.sum(jnp.where(is_grp, jnp.exp(gl - gmax), 0.0), axis=-1, keepdims=True)
    in_grp = (lane >= g_idx * E_PER_GROUP) & (lane < (g_idx + 1) * E_PER_GROUP)
    sel = jnp.where(in_grp, logits, NEG)
    v1 = jnp.max(sel, axis=-1, keepdims=True)
    i1 = jnp.min(jnp.where(in_grp & (sel == v1), lane, big), axis=-1, keepdims=True)
    rest = in_grp & (lane != i1)
    sel2 = jnp.where(rest, logits, NEG)
    v2 = jnp.max(sel2, axis=-1, keepdims=True)
    i2 = jnp.min(jnp.where(rest & (sel2 == v2), lane, big), axis=-1, keepdims=True)
    e2 = jnp.exp(v2 - v1)
    w1 = g_w / (1.0 + e2)
    w2 = g_w * e2 / (1.0 + e2)
    return jnp.where(lane == i1, w1, jnp.where(lane == i2, w2, 0.0))


def _mix_kernel(x_ref, ya_ref, ob0_ref, ob1_ref, ob2_ref, ls0_ref, ls1_ref, ls2_ref, yc_ref,
                g0_ref, g1_ref, g2_ref, wb_ref, wo_ref, nf_ref, wr_ref, br_ref,
                xo_ref, h_ref, comb_ref):
    slabs = []
    for j in range(N_SLABS):
        ls0, ls1, ls2 = ls0_ref[0, j], ls1_ref[0, j], ls2_ref[0, j]
        mx = jnp.maximum(jnp.maximum(ls0, ls1), ls2)
        e0, e1, e2 = jnp.exp(ls0 - mx), jnp.exp(ls1 - mx), jnp.exp(ls2 - mx)
        yb = (e0 * ob0_ref[0, j] + e1 * ob1_ref[0, j] + e2 * ob2_ref[0, j]) / (e0 + e1 + e2)
        slabs.append(yb.astype(BF16))
    yb = jnp.concatenate(slabs, axis=-1)
    merged = jax.nn.sigmoid(g0_ref[...].astype(F32)) * jnp.dot(ya_ref[...], wb_ref[0],
                                                               preferred_element_type=F32)
    merged += jax.nn.sigmoid(g1_ref[...].astype(F32)) * jnp.dot(yb, wb_ref[1],
                                                                preferred_element_type=F32)
    merged += jax.nn.sigmoid(g2_ref[...].astype(F32)) * jnp.dot(yc_ref[...], wb_ref[2],
                                                                preferred_element_type=F32)
    xn = x_ref[...] + jnp.dot(merged.astype(BF16), wo_ref[...], preferred_element_type=F32)
    xo_ref[...] = xn
    h = _rms_bf16(xn, nf_ref[...])
    h_ref[...] = h
    logits = jnp.dot(h, wr_ref[...], preferred_element_type=F32) + br_ref[...]
    comb_ref[...] = _route(logits)


def _mix(x2, ya, obs, lses, yc, proj2, wb, wo, nf, wr, br):
    T, D = x2.shape
    S = obs[0].shape[2]
    tm = min(TM_MIX, S)
    per_b = S // tm
    w = MIX_W
    row = lambda width: pl.BlockSpec((tm, width), lambda i: (i, 0))
    full = lambda a: pl.BlockSpec(a.shape, lambda i: (0,) * a.ndim)
    gate = lambda n: pl.BlockSpec((tm, D), lambda i: (i, COL_GATE + n))
    slab = pl.BlockSpec((1, N_SLABS, tm, LANES), lambda i: (i // per_b, 0, i % per_b, 0))
    return pl.pallas_call(
        _mix_kernel,
        out_shape=[jax.ShapeDtypeStruct((T, D), F32), jax.ShapeDtypeStruct((T, D), BF16),
                   jax.ShapeDtypeStruct((T, LANES), F32)],
        grid=(T // tm,),
        in_specs=[row(D), row(w), slab, slab, slab, slab, slab, slab, row(w),
                  gate(0), gate(1), gate(2), full(wb), full(wo), full(nf), full(wr), full(br)],
        out_specs=[row(D), row(D), row(LANES)],
        compiler_params=_cparams(("parallel",)),
        name="mix",
    )(x2, ya, obs[0], obs[1], obs[2], lses[0], lses[1], lses[2], yc, proj2, proj2, proj2,
      wb, wo, nf, wr, br)


def _moe_kernel(h_ref, comb_ref, x_ref, wg_ref, wu_ref, wd_ref, nfin_ref, o_ref, acc_scr,
                *, final_norm):
    e = pl.program_id(1)

    @pl.when(e == 0)
    def _():
        acc_scr[...] = jnp.zeros(acc_scr.shape, F32)

    h = h_ref[...]
    lane = lax.broadcasted_iota(jnp.int32, comb_ref.shape, 1)
    c = jnp.sum(jnp.where(lane == e, comb_ref[...], 0.0), axis=-1, keepdims=True)
    hid = (jax.nn.silu(jnp.dot(h, wg_ref[0].astype(BF16), preferred_element_type=F32))
           * jnp.dot(h, wu_ref[0].astype(BF16), preferred_element_type=F32))
    acc_scr[...] += c * jnp.dot(hid.astype(BF16), wd_ref[0].astype(BF16),
                                preferred_element_type=F32)

    @pl.when(e == pl.num_programs(1) - 1)
    def _():
        xn = x_ref[...] + acc_scr[...]
        if final_norm:
            ms = jnp.mean(xn * xn, axis=-1, keepdims=True)
            xn = xn * lax.rsqrt(ms + EPS) * nfin_ref[...]
        o_ref[...] = xn


def _moe_dense(h, comb, x2, wg, wu, wd, nfin, final_norm):
    T, D = x2.shape
    tm = min(TM_MOE, T)
    F = wg.shape[2]
    kern = functools.partial(_moe_kernel, final_norm=final_norm)
    return pl.pallas_call(
        kern,
        out_shape=jax.ShapeDtypeStruct((T, D), F32),
        grid=(T // tm, N_EXPERTS),
        in_specs=[pl.BlockSpec((tm, D), lambda i, e: (i, 0)),
                  pl.BlockSpec((tm, LANES), lambda i, e: (i, 0)),
                  pl.BlockSpec((tm, D), lambda i, e: (i, 0)),
                  pl.BlockSpec((1, D, F), lambda i, e: (e, 0, 0)),
                  pl.BlockSpec((1, D, F), lambda i, e: (e, 0, 0)),
                  pl.BlockSpec((1, F, D), lambda i, e: (e, 0, 0)),
                  pl.BlockSpec((1, D), lambda i, e: (0, 0))],
        out_specs=pl.BlockSpec((tm, D), lambda i, e: (i, 0)),
        scratch_shapes=[pltpu.VMEM((tm, D), F32)],
        compiler_params=_cparams(("parallel", "arbitrary")),
        name="moe_dense",
    )(h, comb, x2, wg, wu, wd, nfin)


def _moe_dispatch_kernel(h_ref, comb_ref, o_ref, cnt_ref):
    nt = o_ref.shape[0]
    tm = h_ref.shape[0] // nt
    before = jnp.where(lax.broadcasted_iota(jnp.int32, (tm, tm), 0)
                       < lax.broadcasted_iota(jnp.int32, (tm, tm), 1), 1.0, 0.0).astype(BF16)
    slot = lax.broadcasted_iota(jnp.int32, (MOE_CAP, tm), 0).astype(F32)
    for u in range(nt):
        rows = slice(u * tm, (u + 1) * tm)
        comb = comb_ref[rows, :]
        hi = comb.astype(BF16)
        lo = (comb - hi.astype(F32)).astype(BF16)
        haug = jnp.concatenate([h_ref[rows, :], hi, lo], axis=1)
        a_t = comb.T[:N_EXPERTS] > 0.0
        a_f = jnp.where(a_t, 1.0, 0.0)
        rank_t = jnp.dot(a_f.astype(BF16), before, preferred_element_type=F32)
        blocks = [jnp.where((slot == rank_t[e:e + 1]) & a_t[e:e + 1], 1.0, 0.0).astype(BF16)
                  for e in range(N_EXPERTS)]
        res = jnp.dot(jnp.concatenate(blocks, axis=0), haug, preferred_element_type=F32)
        res = res.astype(o_ref.dtype)
        for e in range(N_EXPERTS):
            o_ref[u, e] = res[e * MOE_CAP:(e + 1) * MOE_CAP]
        cnt_ref[u] = jnp.broadcast_to(jnp.sum(a_f, axis=1, keepdims=True), cnt_ref.shape[1:])


def _moe_dispatch(h, comb):
    T, D = h.shape
    tm = min(TM_DISP, T)
    n = T // tm
    nt = math.gcd(TILES_PER_STEP, n)
    return pl.pallas_call(
        _moe_dispatch_kernel,
        out_shape=[jax.ShapeDtypeStruct((n, N_EXPERTS, MOE_CAP, D + 2 * LANES), BF16),
                   jax.ShapeDtypeStruct((n, N_EXPERTS, LANES), F32)],
        grid=(n // nt,),
        in_specs=[pl.BlockSpec((nt * tm, D), lambda i: (i, 0)),
                  pl.BlockSpec((nt * tm, LANES), lambda i: (i, 0))],
        out_specs=[pl.BlockSpec((nt, N_EXPERTS, MOE_CAP, D + 2 * LANES), lambda i: (i, 0, 0, 0)),
                   pl.BlockSpec((nt, N_EXPERTS, LANES), lambda i: (i, 0, 0))],
        compiler_params=_cparams(("parallel",)),
        name="moe_dispatch",
    )(h, comb)


def _moe_ffn_kernel(n16_ref, s_ref, wg_ref, wu_ref, wd_ref, o_ref,
                    wg_scr, wu_scr, wd_scr, lhs_scr, y_scr):
    e, c = pl.program_id(0), pl.program_id(1)

    @pl.when(c == 0)
    def _():
        wg_scr[...] = wg_ref[0].astype(BF16)
        wu_scr[...] = wu_ref[0].astype(BF16)
        wd_scr[...] = wd_ref[0].astype(BF16)

    g, _, cap, _ = s_ref.shape
    D = o_ref.shape[-1]
    @pl.when((e == 0) & (c == 0))
    def _():
        lhs_scr[...] = jnp.zeros(lhs_scr.shape, lhs_scr.dtype)
        y_scr[...] = jnp.zeros(y_scr.shape, y_scr.dtype)

    offs = []
    off = jnp.int32(0)
    for t in range(g):
        offs.append(off)
        lhs_scr[pl.ds(pl.multiple_of(off, BF16_ROWS), cap), :] = s_ref[t, 0]
        off = off + n16_ref[(c * g + t) * N_EXPERTS + e]
    total = off

    def run(nrows):
        rows = lhs_scr[:nrows]
        h = rows[:, :D]
        wparts = rows[:, D:].astype(F32)
        lane = lax.broadcasted_iota(jnp.int32, wparts.shape, 1)
        w = jnp.sum(jnp.where(lane % LANES == e, wparts, 0.0), axis=-1, keepdims=True)
        hid = (jax.nn.silu(jnp.dot(h, wg_scr[...], preferred_element_type=F32))
               * jnp.dot(h, wu_scr[...], preferred_element_type=F32))
        y = w * jnp.dot(hid.astype(BF16), wd_scr[...], preferred_element_type=F32)
        y_scr[:nrows] = y.astype(y_scr.dtype)

    classes = tuple(range(g * cap // 2, g * cap + 1, FFN_ROW_STEP))
    lower = 0
    for nrows in classes:
        pl.when((total > lower) & (total <= nrows))(functools.partial(run, nrows))
        lower = nrows

    for t in range(g):
        o_ref[t, 0] = y_scr[pl.ds(pl.multiple_of(offs[t], BF16_ROWS), cap), :]


def _moe_ffn(srt, n16, wg, wu, wd):
    n, ne, cap, wdt = srt.shape
    D, F = wg.shape[1], wg.shape[2]
    g = math.gcd(G_FFN, n)
    return pl.pallas_call(
        _moe_ffn_kernel,
        out_shape=jax.ShapeDtypeStruct((n, ne, cap, D), BF16),
        grid_spec=pltpu.PrefetchScalarGridSpec(
            num_scalar_prefetch=1,
            grid=(ne, n // g),
            in_specs=[pl.BlockSpec((g, 1, cap, wdt), lambda e, c, n16: (c, e, 0, 0)),
                      pl.BlockSpec((1, D, F), lambda e, c, n16: (e, 0, 0)),
                      pl.BlockSpec((1, D, F), lambda e, c, n16: (e, 0, 0)),
                      pl.BlockSpec((1, F, D), lambda e, c, n16: (e, 0, 0))],
            out_specs=pl.BlockSpec((g, 1, cap, D), lambda e, c, n16: (c, e, 0, 0)),
            scratch_shapes=[pltpu.VMEM((D, F), BF16), pltpu.VMEM((D, F), BF16),
                            pltpu.VMEM((F, D), BF16), pltpu.VMEM((g * cap, wdt), BF16),
                            pltpu.VMEM((g * cap, D), BF16)]),
        compiler_params=_cparams(("arbitrary", "arbitrary")),
        name="moe_ffn",
    )(n16, srt, wg, wu, wd)


def _moe_combine_kernel(skip_ref, y_ref, comb_ref, x_ref, nfin_ref, o_ref, *, final_norm):
    nt = y_ref.shape[0]
    tm = x_ref.shape[0] // nt
    ncol = N_EXPERTS * MOE_CAP
    before = jnp.where(lax.broadcasted_iota(jnp.int32, (tm, tm), 1)
                       < lax.broadcasted_iota(jnp.int32, (tm, tm), 0), 1.0, 0.0).astype(BF16)
    spread = jnp.where(lax.broadcasted_iota(jnp.int32, (LANES, ncol), 1) // MOE_CAP
                       == lax.broadcasted_iota(jnp.int32, (LANES, ncol), 0), 1.0, 0.0).astype(BF16)
    slot = (lax.broadcasted_iota(jnp.int32, (tm, ncol), 1) % MOE_CAP).astype(F32)
    for u in range(nt):
        rows = slice(u * tm, (u + 1) * tm)
        a = comb_ref[rows, :] > 0.0
        rank = jnp.dot(before, jnp.where(a, 1.0, 0.0).astype(BF16), preferred_element_type=F32)
        key = jnp.where(a, rank, -1.0).astype(BF16)
        key_all = jnp.dot(key, spread, preferred_element_type=F32)
        pc = jnp.where(slot == key_all, 1.0, 0.0).astype(BF16)
        y = jnp.concatenate([y_ref[u, e] for e in range(N_EXPERTS)], axis=0)
        xn = x_ref[rows, :] + jnp.dot(pc, y, preferred_element_type=F32)
        if final_norm:
            ms = jnp.mean(xn * xn, axis=-1, keepdims=True)
            later = skip_ref[pl.program_id(0) * nt + u] == 1
            xn = jnp.where(later, xn, xn * lax.rsqrt(ms + EPS) * nfin_ref[...])
        o_ref[rows, :] = xn


def _moe_combine(skip, y, comb, x2, nfin, final_norm):
    T, D = x2.shape
    n, ne, cap, _ = y.shape
    tm = T // n
    nt = math.gcd(TILES_PER_STEP, n)
    kern = functools.partial(_moe_combine_kernel, final_norm=final_norm)
    return pl.pallas_call(
        kern,
        out_shape=jax.ShapeDtypeStruct((T, D), F32),
        grid_spec=pltpu.PrefetchScalarGridSpec(
            num_scalar_prefetch=1,
            grid=(n // nt,),
            in_specs=[pl.BlockSpec((nt, ne, cap, D), lambda i, sk: (i, 0, 0, 0)),
                      pl.BlockSpec((nt * tm, LANES), lambda i, sk: (i, 0)),
                      pl.BlockSpec((nt * tm, D), lambda i, sk: (i, 0)),
                      pl.BlockSpec((1, D), lambda i, sk: (0, 0))],
            out_specs=pl.BlockSpec((nt * tm, D), lambda i, sk: (i, 0))),
        compiler_params=_cparams(("parallel",)),
        name="moe_combine",
    )(skip, y, comb, x2, nfin)


def _moe_fix_kernel(tiles_ref, experts_ref, first_ref, last_ref, n_ref, h_ref, comb_ref, prev_ref,
                    wg_ref, wu_ref, wd_ref, nfin_ref, o_ref, *, final_norm):
    del tiles_ref
    s = pl.program_id(0)

    @pl.when(s < n_ref[0])
    def _():
        e = experts_ref[s]
        tm = h_ref.shape[0]
        comb = comb_ref[...]
        a = jnp.where(comb > 0.0, 1.0, 0.0)
        before = (lax.broadcasted_iota(jnp.int32, (tm, tm), 1)
                  < lax.broadcasted_iota(jnp.int32, (tm, tm), 0))
        rank = jnp.dot(jnp.where(before, 1.0, 0.0).astype(BF16), a.astype(BF16),
                       preferred_element_type=F32)
        lane = lax.broadcasted_iota(jnp.int32, comb.shape, 1)
        dropped = (lane == e) & (rank >= MOE_CAP)
        c = jnp.sum(jnp.where(dropped, comb, 0.0), axis=-1, keepdims=True)
        h = h_ref[...]
        hid = (jax.nn.silu(jnp.dot(h, wg_ref[0].astype(BF16), preferred_element_type=F32))
               * jnp.dot(h, wu_ref[0].astype(BF16), preferred_element_type=F32))
        add = c * jnp.dot(hid.astype(BF16), wd_ref[0].astype(BF16), preferred_element_type=F32)
        fresh = first_ref[s] == 1

        @pl.when(fresh)
        def _():
            o_ref[...] = prev_ref[...] + add

        @pl.when(jnp.logical_not(fresh))
        def _():
            o_ref[...] += add

        if final_norm:
            @pl.when(last_ref[s] == 1)
            def _():
                xn = o_ref[...]
                ms = jnp.mean(xn * xn, axis=-1, keepdims=True)
                o_ref[...] = xn * lax.rsqrt(ms + EPS) * nfin_ref[...]


def _moe_fix(tiles, experts, first, last, n, out, h, comb, wg, wu, wd, nfin, final_norm):
    T, D = out.shape
    tm = min(TM_DISP, T)
    F = wg.shape[2]
    tile = lambda width: pl.BlockSpec((tm, width), lambda s, tl, ex, fi, la, n: (tl[s], 0))
    wspec = lambda shape: pl.BlockSpec(shape, lambda s, tl, ex, fi, la, n: (ex[s], 0, 0))
    kern = functools.partial(_moe_fix_kernel, final_norm=final_norm)
    return pl.pallas_call(
        kern,
        out_shape=jax.ShapeDtypeStruct((T, D), F32),
        grid_spec=pltpu.PrefetchScalarGridSpec(
            num_scalar_prefetch=5,
            grid=(MAX_OVF,),
            in_specs=[tile(D), tile(LANES), tile(D), wspec((1, D, F)), wspec((1, D, F)),
                      wspec((1, F, D)), pl.BlockSpec((1, D), lambda s, tl, ex, fi, la, n: (0, 0))],
            out_specs=tile(D)),
        input_output_aliases={7: 0},
        compiler_params=_cparams(("arbitrary",)),
        name="moe_fix",
    )(tiles, experts, first, last, n, h, comb, out, wg, wu, wd, nfin)


def _moe(h, comb, x2, wg, wu, wd, nfin, final_norm):
    srt, cnt = _moe_dispatch(h, comb)
    over = (cnt[:, :, 0] > MOE_CAP).reshape(-1)
    n_ovf = jnp.sum(over.astype(jnp.int32))
    pairs = jnp.nonzero(over, size=MAX_OVF, fill_value=0)[0].astype(jnp.int32)
    pairs = jnp.where(jnp.arange(MAX_OVF) < n_ovf, pairs, pairs[jnp.clip(n_ovf - 1, 0, MAX_OVF - 1)])
    tiles, experts = pairs // N_EXPERTS, pairs % N_EXPERTS
    change = (tiles[1:] != tiles[:-1]).astype(jnp.int32)
    first = jnp.concatenate([jnp.ones((1,), jnp.int32), change])
    last = jnp.maximum(jnp.concatenate([change, jnp.ones((1,), jnp.int32)]),
                       (jnp.arange(MAX_OVF) == n_ovf - 1).astype(jnp.int32))
    skip = jnp.any(over.reshape(-1, N_EXPERTS), axis=1).astype(jnp.int32)

    used = jnp.minimum(cnt[:, :, 0], MOE_CAP).astype(jnp.int32).reshape(-1)
    n16 = (used + (BF16_ROWS - 1)) // BF16_ROWS * BF16_ROWS

    def routed():
        out = _moe_combine(skip, _moe_ffn(srt, n16, wg, wu, wd), comb, x2, nfin, final_norm)
        return lax.cond(
            n_ovf > 0,
            lambda: _moe_fix(tiles, experts, first, last, n_ovf.reshape(1), out, h, comb,
                             wg, wu, wd, nfin, final_norm),
            lambda: out)

    return lax.cond(n_ovf > MAX_OVF,
                    lambda: _moe_dense(h, comb, x2, wg, wu, wd, nfin, final_norm), routed)


def kernel(x, rel_bias, norm_mix, w_in, diff_lambda, diff_subln, sgu_ln_g, sgu_ln_b, sgu_w, sgu_b,
           w_branch, w_out, norm_ffn, w_router_grp, b_router_grp, w_router_exp, b_router_exp,
           w_gate, w_up, w_down, norm_final):
    B, S, D = x.shape
    T = B * S
    depth = w_in.shape[0]
    a_out = HA * 2 * DA
    grp_w = HB * DB
    b_cols = 3 * NG_B * grp_w
    qkv_b0 = 3 * a_out
    zc0 = qkv_b0 + b_cols
    gate0 = zc0 + 2 * MIX_W
    qk_scale = DA ** -0.5

    bias_a, cfar = _attn_a_bias(rel_bias)
    bias_b = [_attn_b_bias(rel_bias, g) for g in range(NG_B)]

    col = jnp.arange(w_in.shape[2])
    col_scale = jnp.where(col < a_out, qk_scale * LOG2E,
                          jnp.where((col >= qkv_b0) & (col < qkv_b0 + NG_B * grp_w), qk_scale, 1.0))

    def group_cols(w, g):
        return [w[:, qkv_b0 + (c * NG_B + g) * grp_w: qkv_b0 + (c * NG_B + g + 1) * grp_w]
                for c in range(3)]

    x2 = x.reshape(T, D)
    for i in range(depth):
        w = (w_in[i] * col_scale.astype(F32)).astype(BF16)
        nm = norm_mix[i][None, :]
        w_main = jnp.concatenate([w[:, :2 * a_out], w[:, zc0:]] + group_cols(w, 0), axis=1)
        proj2 = _inproj(x2, nm, w_main)
        x3 = x2.reshape(B, S, D)
        vt = _inproj_t(x3, nm, w[:, 2 * a_out:3 * a_out].T)
        proj3 = proj2.reshape(B, S, proj2.shape[1])

        lam_init = 0.8 - 0.6 * math.exp(-0.3 * i)
        lp = diff_lambda[i].astype(F32)
        lam = jnp.exp(jnp.sum(lp[0] * lp[1])) - jnp.exp(jnp.sum(lp[2] * lp[3])) + lam_init
        ya = _attn_a(proj3, vt, lam.reshape(1), cfar, bias_a, diff_subln[i][None, :], lam_init)

        obs, lses = [], []
        for g in range(NG_B):
            r = DILATIONS[g]
            if r == 1:
                qkv4, cols = proj3[:, None], (COL_QKV0, COL_QKV0 + 1, COL_QKV0 + 2)
            else:
                w_g = jnp.concatenate(group_cols(w, g), axis=1)
                qkv4, cols = _inproj_perm(x3, nm, w_g, r), (0, 1, 2)
            o, l = _attn_b(qkv4, bias_b[g], g, cols)
            obs.append(o)
            lses.append(l)

        b_exp = jnp.repeat(sgu_b[i].T, MIX_W // C_GROUPS, axis=1)
        yc = _sgu(proj2, sgu_ln_g[i][None, :], sgu_ln_b[i][None, :], sgu_w[i].astype(BF16), b_exp)

        wr = jnp.concatenate([w_router_exp[i].transpose(1, 0, 2).reshape(D, N_EXPERTS),
                              w_router_grp[i]], axis=1)
        wr = jnp.pad(wr, ((0, 0), (0, LANES - wr.shape[1]))).astype(BF16)
        br = jnp.concatenate([b_router_exp[i].reshape(N_EXPERTS), b_router_grp[i]])
        br = jnp.pad(br, (0, LANES - br.shape[0]))[None, :].astype(F32)

        x2, h, comb = _mix(x2, ya.reshape(T, a_out), obs, lses, yc, proj2,
                           w_branch[i].astype(BF16), w_out[i].astype(BF16), norm_ffn[i][None, :],
                           wr, br)
        x2 = _moe(h, comb, x2, w_gate[i], w_up[i], w_down[i], norm_final[None, :], i == depth - 1)
    return x2.reshape(B, S, D)
```

```python
import functools
import math

import jax
import jax.numpy as jnp
from jax import lax
from jax.experimental import pallas as pl
from jax.experimental.pallas import tpu as pltpu

F32 = jnp.float32
BF16 = jnp.bfloat16

EPS = 1e-6
NEG = -1e30
LOG2E = 1.4426950408889634
LANES = 128
HALF_LANES = LANES // 2
VMEM_LIMIT = 48 * 1024 * 1024

HA = 4
DA = 64
MIX_W = 512
WINDOWS = (128, 512, 2048)
DILATIONS = (1, 4, 16)
NG_B = 3
HB = 8
DB = 64
HALF_WIN = 64
CHUNK = 128
C_GROUPS = 4
N_BRANCH = 3
N_BUCKETS = 32
MAX_DIST = 128
N_GROUPS = 4
E_PER_GROUP = 4
N_EXPERTS = N_GROUPS * E_PER_GROUP
N_SLABS = MIX_W // LANES

TM_PROJ = 1024
TN_PROJ = 3328
TM_PERM = 1024
PERM_BLK = 256
T_ATT = 512
QB_DIL = 128
KW_DIL = QB_DIL + 2 * HALF_WIN
ITEMS_DIL = 8
SUBS_DIL = 4
OUT_ROWS_DIL = 2048
TM_SGU = 2048
TM_MIX = 512
TM_MOE = 1024
TM_DISP = 256
MOE_CAP = HALF_LANES
G_FFN = 16
TILES_PER_STEP = 4
FFN_ROW_STEP = 64
BF16_ROWS = 16
MAX_OVF = 64

COL_ZU = 2
COL_GATE = 2
COL_QKV0 = 10


def _cparams(sem):
    return pltpu.CompilerParams(dimension_semantics=sem, vmem_limit_bytes=VMEM_LIMIT)


def _t5_bucket(rel):
    nb = N_BUCKETS // 2
    max_exact = nb // 2
    ret = (rel > 0).astype(jnp.int32) * nb
    n = jnp.abs(rel)
    nf = jnp.maximum(n, 1).astype(F32)
    large = max_exact + (jnp.log(nf / max_exact) / math.log(MAX_DIST / max_exact)
                         * (nb - max_exact)).astype(jnp.int32)
    large = jnp.minimum(large, nb - 1)
    return ret + jnp.where(n < max_exact, n, large)


def _bias_lookup(bucket, tab):
    out = jnp.zeros((tab.shape[1],) + bucket.shape, F32)
    expand = (slice(None),) + (None,) * bucket.ndim
    for b in range(N_BUCKETS):
        out = jnp.where(bucket[None] == b, tab[b][expand], out)
    return out


def _rms_bf16(x, g):
    ms = jnp.mean(x * x, axis=-1, keepdims=True)
    return (x * lax.rsqrt(ms + EPS) * g).astype(BF16)


def _inproj_kernel(x_ref, g_ref, w_ref, o_ref, h_scr):
    @pl.when(pl.program_id(1) == 0)
    def _():
        h_scr[...] = _rms_bf16(x_ref[...], g_ref[...])

    o_ref[...] = jnp.dot(h_scr[...], w_ref[...], preferred_element_type=F32).astype(o_ref.dtype)


def _inproj(x2, g, w):
    T, D = x2.shape
    N = w.shape[1]
    tm = min(TM_PROJ, T)
    return pl.pallas_call(
        _inproj_kernel,
        out_shape=jax.ShapeDtypeStruct((T, N), BF16),
        grid=(T // tm, N // TN_PROJ),
        in_specs=[pl.BlockSpec((tm, D), lambda i, j: (i, 0)),
                  pl.BlockSpec((1, D), lambda i, j: (0, 0)),
                  pl.BlockSpec((D, TN_PROJ), lambda i, j: (0, j))],
        out_specs=pl.BlockSpec((tm, TN_PROJ), lambda i, j: (i, j)),
        scratch_shapes=[pltpu.VMEM((tm, D), BF16)],
        compiler_params=_cparams(("parallel", "arbitrary")),
        name="inproj",
    )(x2, g, w)


def _inproj_t_kernel(x_ref, g_ref, wt_ref, o_ref):
    h = _rms_bf16(x_ref[0], g_ref[...])
    res = lax.dot_general(wt_ref[...], h, (((1,), (1,)), ((), ())),
                          preferred_element_type=F32).astype(o_ref.dtype)
    for hd in range(o_ref.shape[1]):
        for n in range(o_ref.shape[2]):
            o_ref[0, hd, n] = res[hd * LANES:(hd + 1) * LANES, n * T_ATT:(n + 1) * T_ATT]


def _inproj_t(x3, g, wt):
    B, S, D = x3.shape
    N = wt.shape[0]
    tm = min(TM_PROJ, S)
    nh, nb = N // LANES, tm // T_ATT
    return pl.pallas_call(
        _inproj_t_kernel,
        out_shape=jax.ShapeDtypeStruct((B, nh, S // T_ATT, LANES, T_ATT), BF16),
        grid=(B, S // tm),
        in_specs=[pl.BlockSpec((1, tm, D), lambda b, i: (b, i, 0)),
                  pl.BlockSpec((1, D), lambda b, i: (0, 0)),
                  pl.BlockSpec((N, D), lambda b, i: (0, 0))],
        out_specs=pl.BlockSpec((1, nh, nb, LANES, T_ATT), lambda b, i: (b, 0, i, 0, 0)),
        compiler_params=_cparams(("parallel", "parallel")),
        name="inproj_t",
    )(x3, g, wt)


def _inproj_perm_kernel(x_ref, g_ref, p_ref, w_ref, o_ref, *, r):
    h = _rms_bf16(x_ref[0], g_ref[...])
    nblk = h.shape[0] // PERM_BLK
    hp = jnp.concatenate(
        [jnp.dot(p_ref[...], h[k * PERM_BLK:(k + 1) * PERM_BLK], preferred_element_type=F32)
         for k in range(nblk)], axis=0).astype(BF16)
    res = jnp.dot(hp, w_ref[...], preferred_element_type=F32).astype(o_ref.dtype)
    n = PERM_BLK // r
    for k in range(nblk):
        for s in range(r):
            o_ref[0, s, k * n:(k + 1) * n, :] = res[k * PERM_BLK + s * n:k * PERM_BLK + (s + 1) * n, :]


def _inproj_perm(x3, g, w, r):
    B, S, D = x3.shape
    N = w.shape[1]
    tm = min(TM_PERM, S)
    n = PERM_BLK // r
    o = jnp.arange(PERM_BLK, dtype=jnp.int32)
    src = (o % n) * r + o // n
    perm = (src[:, None] == jnp.arange(PERM_BLK, dtype=jnp.int32)[None, :]).astype(BF16)
    kern = functools.partial(_inproj_perm_kernel, r=r)
    return pl.pallas_call(
        kern,
        out_shape=jax.ShapeDtypeStruct((B, r, S // r, N), BF16),
        grid=(B, S // tm),
        in_specs=[pl.BlockSpec((1, tm, D), lambda b, i: (b, i, 0)),
                  pl.BlockSpec((1, D), lambda b, i: (0, 0)),
                  pl.BlockSpec((PERM_BLK, PERM_BLK), lambda b, i: (0, 0)),
                  pl.BlockSpec((D, N), lambda b, i: (0, 0))],
        out_specs=pl.BlockSpec((1, r, tm // r, N), lambda b, i: (b, 0, i, 0)),
        compiler_params=_cparams(("parallel", "parallel")),
        name=f"inproj_perm_{r}",
    )(x3, g, perm, w)


def _attn_a_kernel(lam_ref, cfar_ref, q_ref, k_ref, vt_ref, bias_ref, g_ref, o_ref,
                   st0_scr, st1_scr, m0_scr, m1_scr, acc_scr, l_scr, *, out_scale, nq, n_blocks):
    k = pl.program_id(0)
    t = T_ATT
    nk = k_ref.shape[1] // t
    n1 = jnp.minimum(k // 2, n_blocks - 1)
    n2 = jnp.maximum(k - 1, 0) // 2
    h1, qi1 = (n1 // nq) % HA, n1 % nq
    h2, qi2 = (n2 // nq) % HA, n2 % nq

    @pl.when(k == 0)
    def _():
        st1_scr[...] = jnp.zeros(st1_scr.shape, F32)
        m1_scr[...] = jnp.zeros(m1_scr.shape, F32)
        acc_scr[...] = jnp.zeros(acc_scr.shape, F32)
        l_scr[...] = jnp.ones(l_scr.shape, F32)

    low_half = lax.broadcasted_iota(jnp.int32, (1, LANES), 1) < HALF_LANES

    def both(cmap, st_w, m_w, st_r, m_r):
        q = q_ref[0]
        zero = jnp.zeros_like(q)
        qc = jnp.where(low_half, q, zero) if cmap == 0 else jnp.where(low_half, zero, q)
        m_prev = m_r[...]
        l = jnp.zeros((1, t), F32)
        acc = jnp.zeros((LANES, t), F32)
        m_new = None
        for j, d in enumerate(range(-1, nk - 1)):
            a1 = lax.rem(qi1 + (d + nk), nk)
            delta1 = a1 - qi1
            kb = k_ref[0, pl.ds(pl.multiple_of(a1 * t, t), t), :]
            st = lax.dot_general(kb, qc, (((1,), (1,)), ((), ())), preferred_element_type=F32)
            if d <= 1:
                st = st + bias_ref[0, jnp.clip(delta1, -2, 2) + 2]
                cm = jnp.max(st, axis=0, keepdims=True)
            else:
                cm = (jnp.max(st, axis=0, keepdims=True)
                      + cfar_ref[2 * h1 + (delta1 > 0).astype(jnp.int32)])
            st_w[j] = st
            m_new = cm if m_new is None else jnp.maximum(m_new, cm)

            a2 = lax.rem(qi2 + (d + nk), nk)
            if d <= 1:
                shifted = m_prev
            else:
                shifted = m_prev - cfar_ref[2 * h2 + (a2 > qi2).astype(jnp.int32)]
            p = jnp.exp2(st_r[j] - shifted)
            l = l + jnp.sum(p, axis=0, keepdims=True)
            acc = acc + jnp.dot(vt_ref[0, 0, a2], p.astype(BF16), preferred_element_type=F32)
        m_w[...] = m_new
        return l, acc

    @pl.when(k % 2 == 0)
    def _():
        l1, acc1 = both(0, st0_scr, m0_scr, st1_scr, m1_scr)
        ot = acc_scr[...] / l_scr[...] - lam_ref[0] * (acc1 / l1)
        o = ot.T
        ms = jnp.mean(o * o, axis=-1, keepdims=True)
        o_ref[0] = (o * lax.rsqrt(ms + EPS) * g_ref[...] * out_scale).astype(o_ref.dtype)

    @pl.when(k % 2 == 1)
    def _():
        l0, acc0 = both(1, st1_scr, m1_scr, st0_scr, m0_scr)
        acc_scr[...] = acc0
        l_scr[...] = l0


def _attn_a(proj3, vt, lam, cfar, bias5, subln_g, lam_init):
    B, S, _ = proj3.shape
    t = T_ATT
    nk = S // t
    n_blocks = B * HA * nk

    def scored(k):
        n = jnp.minimum(k // 2, n_blocks - 1)
        return n // (HA * nk), (n // nk) % HA, n % nk

    def lagged(k, lag):
        n = jnp.maximum(k - lag, 0) // 2
        return n // (HA * nk), (n // nk) % HA, n % nk

    def q_map(k):
        b, h, qi = scored(k)
        return b, qi, h

    def k_map(k):
        b, h, _ = scored(k)
        return b, 0, HA + h

    def vt_map(k):
        b, h, _ = lagged(k, 1)
        return b, h, 0, 0, 0

    def out_map(k):
        b, h, qi = lagged(k, 2)
        return b, qi, h

    kern = functools.partial(_attn_a_kernel, out_scale=1.0 - lam_init, nq=nk, n_blocks=n_blocks)
    return pl.pallas_call(
        kern,
        out_shape=jax.ShapeDtypeStruct((B, S, HA * 2 * DA), BF16),
        grid=(2 * n_blocks + 1,),
        in_specs=[
            pl.BlockSpec(memory_space=pltpu.SMEM),
            pl.BlockSpec(memory_space=pltpu.SMEM),
            pl.BlockSpec((1, t, LANES), q_map),
            pl.BlockSpec((1, S, LANES), k_map),
            pl.BlockSpec((1, 1, nk, LANES, t), vt_map),
            pl.BlockSpec((1, 5, t, t), lambda k: (scored(k)[1], 0, 0, 0)),
            pl.BlockSpec((1, LANES), lambda k: (0, 0)),
        ],
        out_specs=pl.BlockSpec((1, t, LANES), out_map),
        scratch_shapes=[pltpu.VMEM((nk, t, t), F32), pltpu.VMEM((nk, t, t), F32),
                        pltpu.VMEM((1, t), F32), pltpu.VMEM((1, t), F32),
                        pltpu.VMEM((LANES, t), F32), pltpu.VMEM((1, t), F32)],
        compiler_params=_cparams(("arbitrary",)),
        name="diff_attn",
    )(lam, cfar, proj3, proj3, vt, bias5, subln_g)


def _attn_a_bias(rel_bias):
    t = T_ATT
    tab = rel_bias[:, :HA].astype(F32) * LOG2E
    d = jnp.arange(-1, 2, dtype=jnp.int32)[:, None, None] * t
    rel = d + jnp.arange(t, dtype=jnp.int32)[None, :, None] - jnp.arange(t, dtype=jnp.int32)[None, None, :]
    near = _bias_lookup(_t5_bucket(rel), tab)
    far = tab[_t5_bucket(jnp.array([-(t + 1), t + 1], dtype=jnp.int32))].T
    fill = lambda side: jnp.broadcast_to(far[:, side, None, None, None], (HA, 1, t, t))
    tiles = jnp.concatenate([fill(0), near, fill(1)], axis=1)
    return tiles, far.reshape(2 * HA)


def _attn_b_kernel(q_ref, k_ref, v_ref, bias_ref, o_ref, lse_ref, *, sub_len, r, sp, qp):
    nblk = sub_len // QB_DIL
    low_half = lax.broadcasted_iota(jnp.int32, (1, LANES), 1) < HALF_LANES
    for si in range(sp):
        s = si if sp == r else pl.program_id(2) * sp + si
        for qb in range(qp):
            i = pl.program_id(1) * qp + qb
            start = jnp.clip(i * QB_DIL - HALF_WIN, 0, sub_len - KW_DIL)
            start = pl.multiple_of(start, HALF_WIN)
            variant = jnp.where(i == 0, 0, jnp.where(i == nblk - 1, 2, 1))
            q = q_ref[0, si, qb * QB_DIL:(qb + 1) * QB_DIL, :]
            kw = k_ref[0, s, pl.ds(start, KW_DIL), :]
            vw = v_ref[0, s, pl.ds(start, KW_DIL), :]
            rows = (slice(qb * QB_DIL, (qb + 1) * QB_DIL) if r == 1
                    else pl.ds(qb * QB_DIL * r + s, QB_DIL, stride=r))
            lane = lax.broadcasted_iota(jnp.int32, (QB_DIL, LANES), 1)
            lse_tile = jnp.zeros((QB_DIL, LANES), F32)
            for j in range(HB // 2):
                cols = slice(j * LANES, (j + 1) * LANES)
                qpair, kp, vp = q[:, cols], kw[:, cols], vw[:, cols]
                outs, lses = [], []
                for c in range(2):
                    qc = jnp.where(low_half if c == 0 else jnp.logical_not(low_half), qpair,
                                   jnp.zeros_like(qpair))
                    sc = lax.dot_general(qc, kp, (((1,), (1,)), ((), ())),
                                         preferred_element_type=F32)
                    sc = sc + bias_ref[2 * j + c, variant]
                    m = jnp.max(sc, axis=-1, keepdims=True)
                    p = jnp.exp(sc - m)
                    l = jnp.sum(p, axis=-1, keepdims=True)
                    outs.append(jnp.dot(p.astype(BF16), vp, preferred_element_type=F32) / l)
                    lses.append(m + jnp.log(l))
                o_ref[0, j, rows, :] = jnp.where(low_half, outs[0], outs[1])
                for c in range(2):
                    lse_tile = jnp.where(lane == 2 * j + c, lses[c], lse_tile)
            lse_ref[0, rows, :] = lse_tile


def _attn_b(qkv4, bias3, g, cols):
    B, r, L, _ = qkv4.shape
    S = r * L
    width = HB * DB
    nblk = L // QB_DIL
    sp = min(r, SUBS_DIL)
    qp = max(1, min(ITEMS_DIL // sp, OUT_ROWS_DIL // (QB_DIL * r)))
    qcol, kcol, vcol = cols
    kern = functools.partial(_attn_b_kernel, sub_len=L, r=r, sp=sp, qp=qp)
    slab = jax.ShapeDtypeStruct((B, N_SLABS, S, LANES), F32)
    slab_spec = pl.BlockSpec((1, N_SLABS, QB_DIL * r * qp, LANES), lambda b, i, s: (b, 0, i, 0))
    return pl.pallas_call(
        kern,
        out_shape=[slab, jax.ShapeDtypeStruct((B, S, LANES), F32)],
        grid=(B, nblk // qp, r // sp),
        in_specs=[
            pl.BlockSpec((1, sp, QB_DIL * qp, width), lambda b, i, s: (b, s, i, qcol)),
            pl.BlockSpec((1, r, L, width), lambda b, i, s: (b, 0, 0, kcol)),
            pl.BlockSpec((1, r, L, width), lambda b, i, s: (b, 0, 0, vcol)),
            pl.BlockSpec((HB, 3, QB_DIL, KW_DIL), lambda b, i, s: (0, 0, 0, 0)),
        ],
        out_specs=[slab_spec, pl.BlockSpec((1, QB_DIL * r * qp, LANES), lambda b, i, s: (b, i, 0))],
        compiler_params=_cparams(("parallel", "arbitrary", "arbitrary")),
        name=f"dilated_attn_{g}",
    )(qkv4, qkv4, qkv4, bias3)


def _attn_b_bias(rel_bias, g):
    r = DILATIONS[g]
    tab = rel_bias[:, HA + g * HB: HA + (g + 1) * HB].astype(F32)
    off = jnp.arange(3, dtype=jnp.int32)[:, None, None] * HALF_WIN
    rel = (jnp.arange(KW_DIL, dtype=jnp.int32)[None, None, :] - off
           - jnp.arange(QB_DIL, dtype=jnp.int32)[None, :, None])
    bias = _bias_lookup(_t5_bucket(rel * r), tab)
    return jnp.where((jnp.abs(rel) <= HALF_WIN)[None], bias, NEG)


def _sgu_kernel(zu_ref, zv_ref, lng_ref, lnb_ref, ws_ref, bs_ref, o_ref):
    u = jax.nn.gelu(zu_ref[...].astype(F32))
    v = jax.nn.gelu(zv_ref[...].astype(F32))
    mu = jnp.mean(v, axis=-1, keepdims=True)
    var = jnp.mean(jnp.square(v - mu), axis=-1, keepdims=True)
    v = ((v - mu) * lax.rsqrt(var + EPS) * lng_ref[...] + lnb_ref[...]).astype(BF16)
    gd = v.shape[1] // C_GROUPS
    for n in range(v.shape[0] // CHUNK):
        rows = slice(n * CHUNK, (n + 1) * CHUNK)
        for g in range(C_GROUPS):
            cols = slice(g * gd, (g + 1) * gd)
            mixed = jnp.dot(ws_ref[g], v[rows, cols], preferred_element_type=F32) + bs_ref[:, cols]
            o_ref[rows, cols] = (u[rows, cols] * mixed).astype(o_ref.dtype)


def _sgu(proj2, ln_g, ln_b, w_s, b_exp):
    T = proj2.shape[0]
    tm = min(TM_SGU, T)
    w = MIX_W
    return pl.pallas_call(
        _sgu_kernel,
        out_shape=jax.ShapeDtypeStruct((T, w), BF16),
        grid=(T // tm,),
        in_specs=[pl.BlockSpec((tm, w), lambda i: (i, COL_ZU)),
                  pl.BlockSpec((tm, w), lambda i: (i, COL_ZU + 1)),
                  pl.BlockSpec((1, w), lambda i: (0, 0)),
                  pl.BlockSpec((1, w), lambda i: (0, 0)),
                  pl.BlockSpec((C_GROUPS, CHUNK, CHUNK), lambda i: (0, 0, 0)),
                  pl.BlockSpec((CHUNK, w), lambda i: (0, 0))],
        out_specs=pl.BlockSpec((tm, w), lambda i: (i, 0)),
        compiler_params=_cparams(("parallel",)),
        name="sgu",
    )(proj2, proj2, ln_g, ln_b, w_s, b_exp)


def _route(logits):
    lane = lax.broadcasted_iota(jnp.int32, logits.shape, 1)
    big = jnp.int32(LANES)
    is_grp = (lane >= N_EXPERTS) & (lane < N_EXPERTS + N_GROUPS)
    gl = jnp.where(is_grp, logits, NEG)
    gmax = jnp.max(gl, axis=-1, keepdims=True)
    g_idx = jnp.min(jnp.where(is_grp & (gl == gmax), lane, big), axis=-1, keepdims=True) - N_EXPERTS
    g_w = 1.0 / jnp.sum(jnp.where(is_grp, jnp.exp(gl - gmax), 0.0), axis=-1, keepdims=True)
    in_grp = (lane >= g_idx * E_PER_GROUP) & (lane < (g_idx + 1) * E_PER_GROUP)
    sel = jnp.where(in_grp, logits, NEG)
    v1 = jnp.max(sel, axis=-1, keepdims=True)
    i1 = jnp.min(jnp.where(in_grp & (sel == v1), lane, big), axis=-1, keepdims=True)
    rest = in_grp & (lane != i1)
    sel2 = jnp.where(rest, logits, NEG)
    v2 = jnp.max(sel2, axis=-1, keepdims=True)
    i2 = jnp.min(jnp.where(rest & (sel2 == v2), lane, big), axis=-1, keepdims=True)
    e2 = jnp.exp(v2 - v1)
    w1 = g_w / (1.0 + e2)
    w2 = g_w * e2 / (1.0 + e2)
    return jnp.where(lane == i1, w1, jnp.where(lane == i2, w2, 0.0))


def _mix_kernel(x_ref, ya_ref, ob0_ref, ob1_ref, ob2_ref, ls0_ref, ls1_ref, ls2_ref, yc_ref,
                g0_ref, g1_ref, g2_ref, wb_ref, wo_ref, nf_ref, wr_ref, br_ref,
                xo_ref, h_ref, comb_ref):
    ls0, ls1, ls2 = ls0_ref[...], ls1_ref[...], ls2_ref[...]
    mx = jnp.maximum(jnp.maximum(ls0, ls1), ls2)
    es = [jnp.exp(ls0 - mx), jnp.exp(ls1 - mx), jnp.exp(ls2 - mx)]
    inv = 1.0 / (es[0] + es[1] + es[2])
    spread = jnp.where(lax.broadcasted_iota(jnp.int32, (2 * LANES, MIX_W), 1) // DB
                       == lax.broadcasted_iota(jnp.int32, (2 * LANES, MIX_W), 0) % LANES,
                       1.0, 0.0).astype(BF16)
    yb = None
    for e, ob_ref in zip(es, (ob0_ref, ob1_ref, ob2_ref)):
        w = e * inv
        hi = w.astype(BF16)
        lo = (w - hi.astype(F32)).astype(BF16)
        wide = jnp.dot(jnp.concatenate([hi, lo], axis=1), spread, preferred_element_type=F32)
        term = wide * jnp.concatenate([ob_ref[0, j] for j in range(N_SLABS)], axis=-1)
        yb = term if yb is None else yb + term
    yb = yb.astype(BF16)
    merged = jax.nn.sigmoid(g0_ref[...].astype(F32)) * jnp.dot(ya_ref[...], wb_ref[0],
                                                               preferred_element_type=F32)
    merged += jax.nn.sigmoid(g1_ref[...].astype(F32)) * jnp.dot(yb, wb_ref[1],
                                                                preferred_element_type=F32)
    merged += jax.nn.sigmoid(g2_ref[...].astype(F32)) * jnp.dot(yc_ref[...], wb_ref[2],
                                                                preferred_element_type=F32)
    xn = x_ref[...] + jnp.dot(merged.astype(BF16), wo_ref[...], preferred_element_type=F32)
    xo_ref[...] = xn
    h = _rms_bf16(xn, nf_ref[...])
    h_ref[...] = h
    logits = jnp.dot(h, wr_ref[...], preferred_element_type=F32) + br_ref[...]
    comb_ref[...] = _route(logits)


def _mix(x2, ya, obs, lses, yc, proj2, wb, wo, nf, wr, br):
    T, D = x2.shape
    S = obs[0].shape[2]
    tm = min(TM_MIX, S)
    per_b = S // tm
    w = MIX_W
    row = lambda width: pl.BlockSpec((tm, width), lambda i: (i, 0))
    full = lambda a: pl.BlockSpec(a.shape, lambda i: (0,) * a.ndim)
    gate = lambda n: pl.BlockSpec((tm, D), lambda i: (i, COL_GATE + n))
    slab = pl.BlockSpec((1, N_SLABS, tm, LANES), lambda i: (i // per_b, 0, i % per_b, 0))
    return pl.pallas_call(
        _mix_kernel,
        out_shape=[jax.ShapeDtypeStruct((T, D), F32), jax.ShapeDtypeStruct((T, D), BF16),
                   jax.ShapeDtypeStruct((T, LANES), F32)],
        grid=(T // tm,),
        in_specs=[row(D), row(w), slab, slab, slab, row(LANES), row(LANES), row(LANES), row(w),
                  gate(0), gate(1), gate(2), full(wb), full(wo), full(nf), full(wr), full(br)],
        out_specs=[row(D), row(D), row(LANES)],
        compiler_params=_cparams(("parallel",)),
        name="mix",
    )(x2, ya, obs[0], obs[1], obs[2], lses[0], lses[1], lses[2], yc, proj2, proj2, proj2,
      wb, wo, nf, wr, br)


def _moe_kernel(h_ref, comb_ref, x_ref, wg_ref, wu_ref, wd_ref, nfin_ref, o_ref, acc_scr,
                *, final_norm):
    e = pl.program_id(1)

    @pl.when(e == 0)
    def _():
        acc_scr[...] = jnp.zeros(acc_scr.shape, F32)

    h = h_ref[...]
    lane = lax.broadcasted_iota(jnp.int32, comb_ref.shape, 1)
    c = jnp.sum(jnp.where(lane == e, comb_ref[...], 0.0), axis=-1, keepdims=True)
    hid = (jax.nn.silu(jnp.dot(h, wg_ref[0].astype(BF16), preferred_element_type=F32))
           * jnp.dot(h, wu_ref[0].astype(BF16), preferred_element_type=F32))
    acc_scr[...] += c * jnp.dot(hid.astype(BF16), wd_ref[0].astype(BF16),
                                preferred_element_type=F32)

    @pl.when(e == pl.num_programs(1) - 1)
    def _():
        xn = x_ref[...] + acc_scr[...]
        if final_norm:
            ms = jnp.mean(xn * xn, axis=-1, keepdims=True)
            xn = xn * lax.rsqrt(ms + EPS) * nfin_ref[...]
        o_ref[...] = xn


def _moe_dense(h, comb, x2, wg, wu, wd, nfin, final_norm):
    T, D = x2.shape
    tm = min(TM_MOE, T)
    F = wg.shape[2]
    kern = functools.partial(_moe_kernel, final_norm=final_norm)
    return pl.pallas_call(
        kern,
        out_shape=jax.ShapeDtypeStruct((T, D), F32),
        grid=(T // tm, N_EXPERTS),
        in_specs=[pl.BlockSpec((tm, D), lambda i, e: (i, 0)),
                  pl.BlockSpec((tm, LANES), lambda i, e: (i, 0)),
                  pl.BlockSpec((tm, D), lambda i, e: (i, 0)),
                  pl.BlockSpec((1, D, F), lambda i, e: (e, 0, 0)),
                  pl.BlockSpec((1, D, F), lambda i, e: (e, 0, 0)),
                  pl.BlockSpec((1, F, D), lambda i, e: (e, 0, 0)),
                  pl.BlockSpec((1, D), lambda i, e: (0, 0))],
        out_specs=pl.BlockSpec((tm, D), lambda i, e: (i, 0)),
        scratch_shapes=[pltpu.VMEM((tm, D), F32)],
        compiler_params=_cparams(("parallel", "arbitrary")),
        name="moe_dense",
    )(h, comb, x2, wg, wu, wd, nfin)


def _moe_dispatch_kernel(h_ref, comb_ref, o_ref, cnt_ref):
    nt = o_ref.shape[0]
    tm = h_ref.shape[0] // nt
    before = jnp.where(lax.broadcasted_iota(jnp.int32, (tm, tm), 0)
                       < lax.broadcasted_iota(jnp.int32, (tm, tm), 1), 1.0, 0.0).astype(BF16)
    slot = lax.broadcasted_iota(jnp.int32, (MOE_CAP, tm), 0).astype(F32)
    for u in range(nt):
        rows = slice(u * tm, (u + 1) * tm)
        comb = comb_ref[rows, :]
        hi = comb.astype(BF16)
        lo = (comb - hi.astype(F32)).astype(BF16)
        haug = jnp.concatenate([h_ref[rows, :], hi, lo], axis=1)
        a_t = comb.T[:N_EXPERTS] > 0.0
        a_f = jnp.where(a_t, 1.0, 0.0)
        rank_t = jnp.dot(a_f.astype(BF16), before, preferred_element_type=F32)
        blocks = [jnp.where((slot == rank_t[e:e + 1]) & a_t[e:e + 1], 1.0, 0.0).astype(BF16)
                  for e in range(N_EXPERTS)]
        res = jnp.dot(jnp.concatenate(blocks, axis=0), haug, preferred_element_type=F32)
        res = res.astype(o_ref.dtype)
        for e in range(N_EXPERTS):
            o_ref[u, e] = res[e * MOE_CAP:(e + 1) * MOE_CAP]
        cnt_ref[u] = jnp.broadcast_to(jnp.sum(a_f, axis=1, keepdims=True), cnt_ref.shape[1:])


def _moe_dispatch(h, comb):
    T, D = h.shape
    tm = min(TM_DISP, T)
    n = T // tm
    nt = math.gcd(TILES_PER_STEP, n)
    return pl.pallas_call(
        _moe_dispatch_kernel,
        out_shape=[jax.ShapeDtypeStruct((n, N_EXPERTS, MOE_CAP, D + 2 * LANES), BF16),
                   jax.ShapeDtypeStruct((n, N_EXPERTS, LANES), F32)],
        grid=(n // nt,),
        in_specs=[pl.BlockSpec((nt * tm, D), lambda i: (i, 0)),
                  pl.BlockSpec((nt * tm, LANES), lambda i: (i, 0))],
        out_specs=[pl.BlockSpec((nt, N_EXPERTS, MOE_CAP, D + 2 * LANES), lambda i: (i, 0, 0, 0)),
                   pl.BlockSpec((nt, N_EXPERTS, LANES), lambda i: (i, 0, 0))],
        compiler_params=_cparams(("parallel",)),
        name="moe_dispatch",
    )(h, comb)


def _moe_ffn_kernel(n16_ref, s_ref, wg_ref, wu_ref, wd_ref, o_ref,
                    wg_scr, wu_scr, wd_scr, lhs_scr, y_scr):
    e, c = pl.program_id(0), pl.program_id(1)

    @pl.when(c == 0)
    def _():
        wg_scr[...] = wg_ref[0].astype(BF16)
        wu_scr[...] = wu_ref[0].astype(BF16)
        wd_scr[...] = wd_ref[0].astype(BF16)

    g, _, cap, _ = s_ref.shape
    D = o_ref.shape[-1]
    @pl.when((e == 0) & (c == 0))
    def _():
        lhs_scr[...] = jnp.zeros(lhs_scr.shape, lhs_scr.dtype)
        y_scr[...] = jnp.zeros(y_scr.shape, y_scr.dtype)

    offs = []
    off = jnp.int32(0)
    for t in range(g):
        offs.append(off)
        lhs_scr[pl.ds(pl.multiple_of(off, BF16_ROWS), cap), :] = s_ref[t, 0]
        off = off + n16_ref[(c * g + t) * N_EXPERTS + e]
    total = off

    def run(nrows):
        rows = lhs_scr[:nrows]
        h = rows[:, :D]
        wparts = rows[:, D:].astype(F32)
        lane = lax.broadcasted_iota(jnp.int32, wparts.shape, 1)
        w = jnp.sum(jnp.where(lane % LANES == e, wparts, 0.0), axis=-1, keepdims=True)
        hid = (jax.nn.silu(jnp.dot(h, wg_scr[...], preferred_element_type=F32))
               * jnp.dot(h, wu_scr[...], preferred_element_type=F32))
        y = w * jnp.dot(hid.astype(BF16), wd_scr[...], preferred_element_type=F32)
        y_scr[:nrows] = y.astype(y_scr.dtype)

    classes = tuple(range(g * cap // 2, g * cap + 1, FFN_ROW_STEP))
    lower = 0
    for nrows in classes:
        pl.when((total > lower) & (total <= nrows))(functools.partial(run, nrows))
        lower = nrows

    for t in range(g):
        o_ref[t, 0] = y_scr[pl.ds(pl.multiple_of(offs[t], BF16_ROWS), cap), :]


def _moe_ffn(srt, n16, wg, wu, wd):
    n, ne, cap, wdt = srt.shape
    D, F = wg.shape[1], wg.shape[2]
    g = math.gcd(G_FFN, n)
    return pl.pallas_call(
        _moe_ffn_kernel,
        out_shape=jax.ShapeDtypeStruct((n, ne, cap, D), BF16),
        grid_spec=pltpu.PrefetchScalarGridSpec(
            num_scalar_prefetch=1,
            grid=(ne, n // g),
            in_specs=[pl.BlockSpec((g, 1, cap, wdt), lambda e, c, n16: (c, e, 0, 0)),
                      pl.BlockSpec((1, D, F), lambda e, c, n16: (e, 0, 0)),
                      pl.BlockSpec((1, D, F), lambda e, c, n16: (e, 0, 0)),
                      pl.BlockSpec((1, F, D), lambda e, c, n16: (e, 0, 0))],
            out_specs=pl.BlockSpec((g, 1, cap, D), lambda e, c, n16: (c, e, 0, 0)),
            scratch_shapes=[pltpu.VMEM((D, F), BF16), pltpu.VMEM((D, F), BF16),
                            pltpu.VMEM((F, D), BF16), pltpu.VMEM((g * cap, wdt), BF16),
                            pltpu.VMEM((g * cap, D), BF16)]),
        compiler_params=_cparams(("arbitrary", "arbitrary")),
        name="moe_ffn",
    )(n16, srt, wg, wu, wd)


def _moe_combine_kernel(skip_ref, y_ref, comb_ref, x_ref, nfin_ref, o_ref, *, final_norm):
    nt = y_ref.shape[0]
    tm = x_ref.shape[0] // nt
    ncol = N_EXPERTS * MOE_CAP
    before = jnp.where(lax.broadcasted_iota(jnp.int32, (tm, tm), 1)
                       < lax.broadcasted_iota(jnp.int32, (tm, tm), 0), 1.0, 0.0).astype(BF16)
    spread = jnp.where(lax.broadcasted_iota(jnp.int32, (LANES, ncol), 1) // MOE_CAP
                       == lax.broadcasted_iota(jnp.int32, (LANES, ncol), 0), 1.0, 0.0).astype(BF16)
    slot = (lax.broadcasted_iota(jnp.int32, (tm, ncol), 1) % MOE_CAP).astype(F32)
    for u in range(nt):
        rows = slice(u * tm, (u + 1) * tm)
        a = comb_ref[rows, :] > 0.0
        rank = jnp.dot(before, jnp.where(a, 1.0, 0.0).astype(BF16), preferred_element_type=F32)
        key = jnp.where(a, rank, -1.0).astype(BF16)
        key_all = jnp.dot(key, spread, preferred_element_type=F32)
        pc = jnp.where(slot == key_all, 1.0, 0.0).astype(BF16)
        y = jnp.concatenate([y_ref[u, e] for e in range(N_EXPERTS)], axis=0)
        xn = x_ref[rows, :] + jnp.dot(pc, y, preferred_element_type=F32)
        if final_norm:
            ms = jnp.mean(xn * xn, axis=-1, keepdims=True)
            later = skip_ref[pl.program_id(0) * nt + u] == 1
            xn = jnp.where(later, xn, xn * lax.rsqrt(ms + EPS) * nfin_ref[...])
        o_ref[rows, :] = xn


def _moe_combine(skip, y, comb, x2, nfin, final_norm):
    T, D = x2.shape
    n, ne, cap, _ = y.shape
    tm = T // n
    nt = math.gcd(TILES_PER_STEP, n)
    kern = functools.partial(_moe_combine_kernel, final_norm=final_norm)
    return pl.pallas_call(
        kern,
        out_shape=jax.ShapeDtypeStruct((T, D), F32),
        grid_spec=pltpu.PrefetchScalarGridSpec(
            num_scalar_prefetch=1,
            grid=(n // nt,),
            in_specs=[pl.BlockSpec((nt, ne, cap, D), lambda i, sk: (i, 0, 0, 0)),
                      pl.BlockSpec((nt * tm, LANES), lambda i, sk: (i, 0)),
                      pl.BlockSpec((nt * tm, D), lambda i, sk: (i, 0)),
                      pl.BlockSpec((1, D), lambda i, sk: (0, 0))],
            out_specs=pl.BlockSpec((nt * tm, D), lambda i, sk: (i, 0))),
        compiler_params=_cparams(("parallel",)),
        name="moe_combine",
    )(skip, y, comb, x2, nfin)


def _moe_fix_kernel(tiles_ref, experts_ref, first_ref, last_ref, n_ref, h_ref, comb_ref, prev_ref,
                    wg_ref, wu_ref, wd_ref, nfin_ref, o_ref, *, final_norm):
    del tiles_ref
    s = pl.program_id(0)

    @pl.when(s < n_ref[0])
    def _():
        e = experts_ref[s]
        tm = h_ref.shape[0]
        comb = comb_ref[...]
        a = jnp.where(comb > 0.0, 1.0, 0.0)
        before = (lax.broadcasted_iota(jnp.int32, (tm, tm), 1)
                  < lax.broadcasted_iota(jnp.int32, (tm, tm), 0))
        rank = jnp.dot(jnp.where(before, 1.0, 0.0).astype(BF16), a.astype(BF16),
                       preferred_element_type=F32)
        lane = lax.broadcasted_iota(jnp.int32, comb.shape, 1)
        dropped = (lane == e) & (rank >= MOE_CAP)
        c = jnp.sum(jnp.where(dropped, comb, 0.0), axis=-1, keepdims=True)
        h = h_ref[...]
        hid = (jax.nn.silu(jnp.dot(h, wg_ref[0].astype(BF16), preferred_element_type=F32))
               * jnp.dot(h, wu_ref[0].astype(BF16), preferred_element_type=F32))
        add = c * jnp.dot(hid.astype(BF16), wd_ref[0].astype(BF16), preferred_element_type=F32)
        fresh = first_ref[s] == 1

        @pl.when(fresh)
        def _():
            o_ref[...] = prev_ref[...] + add

        @pl.when(jnp.logical_not(fresh))
        def _():
            o_ref[...] += add

        if final_norm:
            @pl.when(last_ref[s] == 1)
            def _():
                xn = o_ref[...]
                ms = jnp.mean(xn * xn, axis=-1, keepdims=True)
                o_ref[...] = xn * lax.rsqrt(ms + EPS) * nfin_ref[...]


def _moe_fix(tiles, experts, first, last, n, out, h, comb, wg, wu, wd, nfin, final_norm):
    T, D = out.shape
    tm = min(TM_DISP, T)
    F = wg.shape[2]
    tile = lambda width: pl.BlockSpec((tm, width), lambda s, tl, ex, fi, la, n: (tl[s], 0))
    wspec = lambda shape: pl.BlockSpec(shape, lambda s, tl, ex, fi, la, n: (ex[s], 0, 0))
    kern = functools.partial(_moe_fix_kernel, final_norm=final_norm)
    return pl.pallas_call(
        kern,
        out_shape=jax.ShapeDtypeStruct((T, D), F32),
        grid_spec=pltpu.PrefetchScalarGridSpec(
            num_scalar_prefetch=5,
            grid=(MAX_OVF,),
            in_specs=[tile(D), tile(LANES), tile(D), wspec((1, D, F)), wspec((1, D, F)),
                      wspec((1, F, D)), pl.BlockSpec((1, D), lambda s, tl, ex, fi, la, n: (0, 0))],
            out_specs=tile(D)),
        input_output_aliases={7: 0},
        compiler_params=_cparams(("arbitrary",)),
        name="moe_fix",
    )(tiles, experts, first, last, n, h, comb, out, wg, wu, wd, nfin)


def _moe(h, comb, x2, wg, wu, wd, nfin, final_norm):
    srt, cnt = _moe_dispatch(h, comb)
    over = (cnt[:, :, 0] > MOE_CAP).reshape(-1)
    n_ovf = jnp.sum(over.astype(jnp.int32))
    pairs = jnp.nonzero(over, size=MAX_OVF, fill_value=0)[0].astype(jnp.int32)
    pairs = jnp.where(jnp.arange(MAX_OVF) < n_ovf, pairs, pairs[jnp.clip(n_ovf - 1, 0, MAX_OVF - 1)])
    tiles, experts = pairs // N_EXPERTS, pairs % N_EXPERTS
    change = (tiles[1:] != tiles[:-1]).astype(jnp.int32)
    first = jnp.concatenate([jnp.ones((1,), jnp.int32), change])
    last = jnp.maximum(jnp.concatenate([change, jnp.ones((1,), jnp.int32)]),
                       (jnp.arange(MAX_OVF) == n_ovf - 1).astype(jnp.int32))
    skip = jnp.any(over.reshape(-1, N_EXPERTS), axis=1).astype(jnp.int32)

    used = jnp.minimum(cnt[:, :, 0], MOE_CAP).astype(jnp.int32).reshape(-1)
    n16 = (used + (BF16_ROWS - 1)) // BF16_ROWS * BF16_ROWS

    def routed():
        out = _moe_combine(skip, _moe_ffn(srt, n16, wg, wu, wd), comb, x2, nfin, final_norm)
        return lax.cond(
            n_ovf > 0,
            lambda: _moe_fix(tiles, experts, first, last, n_ovf.reshape(1), out, h, comb,
                             wg, wu, wd, nfin, final_norm),
            lambda: out)

    return lax.cond(n_ovf > MAX_OVF,
                    lambda: _moe_dense(h, comb, x2, wg, wu, wd, nfin, final_norm), routed)


def kernel(x, rel_bias, norm_mix, w_in, diff_lambda, diff_subln, sgu_ln_g, sgu_ln_b, sgu_w, sgu_b,
           w_branch, w_out, norm_ffn, w_router_grp, b_router_grp, w_router_exp, b_router_exp,
           w_gate, w_up, w_down, norm_final):
    B, S, D = x.shape
    T = B * S
    depth = w_in.shape[0]
    a_out = HA * 2 * DA
    grp_w = HB * DB
    b_cols = 3 * NG_B * grp_w
    qkv_b0 = 3 * a_out
    zc0 = qkv_b0 + b_cols
    gate0 = zc0 + 2 * MIX_W
    qk_scale = DA ** -0.5

    bias_a, cfar = _attn_a_bias(rel_bias)
    bias_b = [_attn_b_bias(rel_bias, g) for g in range(NG_B)]

    col = jnp.arange(w_in.shape[2])
    col_scale = jnp.where(col < a_out, qk_scale * LOG2E,
                          jnp.where((col >= qkv_b0) & (col < qkv_b0 + NG_B * grp_w), qk_scale, 1.0))

    def group_cols(w, g):
        return [w[:, qkv_b0 + (c * NG_B + g) * grp_w: qkv_b0 + (c * NG_B + g + 1) * grp_w]
                for c in range(3)]

    x2 = x.reshape(T, D)
    for i in range(depth):
        w = (w_in[i] * col_scale.astype(F32)).astype(BF16)
        nm = norm_mix[i][None, :]
        w_main = jnp.concatenate([w[:, :2 * a_out], w[:, zc0:]] + group_cols(w, 0), axis=1)
        proj2 = _inproj(x2, nm, w_main)
        x3 = x2.reshape(B, S, D)
        vt = _inproj_t(x3, nm, w[:, 2 * a_out:3 * a_out].T)
        proj3 = proj2.reshape(B, S, proj2.shape[1])

        lam_init = 0.8 - 0.6 * math.exp(-0.3 * i)
        lp = diff_lambda[i].astype(F32)
        lam = jnp.exp(jnp.sum(lp[0] * lp[1])) - jnp.exp(jnp.sum(lp[2] * lp[3])) + lam_init
        ya = _attn_a(proj3, vt, lam.reshape(1), cfar, bias_a, diff_subln[i][None, :], lam_init)

        obs, lses = [], []
        for g in range(NG_B):
            r = DILATIONS[g]
            if r == 1:
                qkv4, cols = proj3[:, None], (COL_QKV0, COL_QKV0 + 1, COL_QKV0 + 2)
            else:
                w_g = jnp.concatenate(group_cols(w, g), axis=1)
                qkv4, cols = _inproj_perm(x3, nm, w_g, r), (0, 1, 2)
            o, l = _attn_b(qkv4, bias_b[g], g, cols)
            obs.append(o)
            lses.append(l.reshape(T, LANES))

        b_exp = jnp.repeat(sgu_b[i].T, MIX_W // C_GROUPS, axis=1)
        yc = _sgu(proj2, sgu_ln_g[i][None, :], sgu_ln_b[i][None, :], sgu_w[i].astype(BF16), b_exp)

        wr = jnp.concatenate([w_router_exp[i].transpose(1, 0, 2).reshape(D, N_EXPERTS),
                              w_router_grp[i]], axis=1)
        wr = jnp.pad(wr, ((0, 0), (0, LANES - wr.shape[1]))).astype(BF16)
        br = jnp.concatenate([b_router_exp[i].reshape(N_EXPERTS), b_router_grp[i]])
        br = jnp.pad(br, (0, LANES - br.shape[0]))[None, :].astype(F32)

        x2, h, comb = _mix(x2, ya.reshape(T, a_out), obs, lses, yc, proj2,
                           w_branch[i].astype(BF16), w_out[i].astype(BF16), norm_ffn[i][None, :],
                           wr, br)
        x2 = _moe(h, comb, x2, w_gate[i], w_up[i], w_down[i], norm_final[None, :], i == depth - 1)
    return x2.reshape(B, S, D)
```

```python
import functools
import math

import jax
import jax.numpy as jnp
from jax import lax
from jax.experimental import pallas as pl
from jax.experimental.pallas import tpu as pltpu

F32 = jnp.float32
BF16 = jnp.bfloat16

EPS = 1e-6
NEG = -1e30
LOG2E = 1.4426950408889634
LANES = 128
HALF_LANES = LANES // 2
VMEM_LIMIT = 48 * 1024 * 1024

HA = 4
DA = 64
MIX_W = 512
WINDOWS = (128, 512, 2048)
DILATIONS = (1, 4, 16)
NG_B = 3
HB = 8
DB = 64
HALF_WIN = 64
CHUNK = 128
C_GROUPS = 4
N_BRANCH = 3
N_BUCKETS = 32
MAX_DIST = 128
N_GROUPS = 4
E_PER_GROUP = 4
N_EXPERTS = N_GROUPS * E_PER_GROUP
N_SLABS = MIX_W // LANES

TM_PROJ = 1024
TN_PROJ = 3328
TM_PERM = 1024
PERM_BLK = 256
T_ATT = 512
QB_DIL = 128
KW_DIL = QB_DIL + 2 * HALF_WIN
ITEMS_DIL = 8
SUBS_DIL = 4
OUT_ROWS_DIL = 2048
TM_SGU = 2048
TM_MIX = 512
TM_MOE = 1024
TM_DISP = 256
MOE_CAP = HALF_LANES
G_FFN = 16
TILES_PER_STEP = 4
FFN_ROW_STEP = 64
BF16_ROWS = 16
MAX_OVF = 64

COL_ZU = 2
COL_GATE = 2
COL_QKV0 = 10


def _cparams(sem):
    return pltpu.CompilerParams(dimension_semantics=sem, vmem_limit_bytes=VMEM_LIMIT)


def _t5_bucket(rel):
    nb = N_BUCKETS // 2
    max_exact = nb // 2
    ret = (rel > 0).astype(jnp.int32) * nb
    n = jnp.abs(rel)
    nf = jnp.maximum(n, 1).astype(F32)
    large = max_exact + (jnp.log(nf / max_exact) / math.log(MAX_DIST / max_exact)
                         * (nb - max_exact)).astype(jnp.int32)
    large = jnp.minimum(large, nb - 1)
    return ret + jnp.where(n < max_exact, n, large)


def _bias_lookup(bucket, tab):
    out = jnp.zeros((tab.shape[1],) + bucket.shape, F32)
    expand = (slice(None),) + (None,) * bucket.ndim
    for b in range(N_BUCKETS):
        out = jnp.where(bucket[None] == b, tab[b][expand], out)
    return out


def _rms_bf16(x, g):
    ms = jnp.mean(x * x, axis=-1, keepdims=True)
    return (x * lax.rsqrt(ms + EPS) * g).astype(BF16)


def _inproj_kernel(x_ref, g_ref, w_ref, o_ref, h_scr):
    @pl.when(pl.program_id(1) == 0)
    def _():
        h_scr[...] = _rms_bf16(x_ref[...], g_ref[...])

    o_ref[...] = jnp.dot(h_scr[...], w_ref[...], preferred_element_type=F32).astype(o_ref.dtype)


def _inproj(x2, g, w):
    T, D = x2.shape
    N = w.shape[1]
    tm = min(TM_PROJ, T)
    return pl.pallas_call(
        _inproj_kernel,
        out_shape=jax.ShapeDtypeStruct((T, N), BF16),
        grid=(T // tm, N // TN_PROJ),
        in_specs=[pl.BlockSpec((tm, D), lambda i, j: (i, 0)),
                  pl.BlockSpec((1, D), lambda i, j: (0, 0)),
                  pl.BlockSpec((D, TN_PROJ), lambda i, j: (0, j))],
        out_specs=pl.BlockSpec((tm, TN_PROJ), lambda i, j: (i, j)),
        scratch_shapes=[pltpu.VMEM((tm, D), BF16)],
        compiler_params=_cparams(("parallel", "arbitrary")),
        name="inproj",
    )(x2, g, w)


def _inproj_t_kernel(x_ref, g_ref, wt_ref, o_ref):
    h = _rms_bf16(x_ref[0], g_ref[...])
    res = lax.dot_general(wt_ref[...], h, (((1,), (1,)), ((), ())),
                          preferred_element_type=F32).astype(o_ref.dtype)
    for hd in range(o_ref.shape[1]):
        for n in range(o_ref.shape[2]):
            o_ref[0, hd, n] = res[hd * LANES:(hd + 1) * LANES, n * T_ATT:(n + 1) * T_ATT]


def _inproj_t(x3, g, wt):
    B, S, D = x3.shape
    N = wt.shape[0]
    tm = min(TM_PROJ, S)
    nh, nb = N // LANES, tm // T_ATT
    return pl.pallas_call(
        _inproj_t_kernel,
        out_shape=jax.ShapeDtypeStruct((B, nh, S // T_ATT, LANES, T_ATT), BF16),
        grid=(B, S // tm),
        in_specs=[pl.BlockSpec((1, tm, D), lambda b, i: (b, i, 0)),
                  pl.BlockSpec((1, D), lambda b, i: (0, 0)),
                  pl.BlockSpec((N, D), lambda b, i: (0, 0))],
        out_specs=pl.BlockSpec((1, nh, nb, LANES, T_ATT), lambda b, i: (b, 0, i, 0, 0)),
        compiler_params=_cparams(("parallel", "parallel")),
        name="inproj_t",
    )(x3, g, wt)


def _inproj_perm_kernel(x_ref, g_ref, p_ref, w_ref, o_ref, *, r):
    h = _rms_bf16(x_ref[0], g_ref[...])
    nblk = h.shape[0] // PERM_BLK
    hp = jnp.concatenate(
        [jnp.dot(p_ref[...], h[k * PERM_BLK:(k + 1) * PERM_BLK], preferred_element_type=F32)
         for k in range(nblk)], axis=0).astype(BF16)
    res = jnp.dot(hp, w_ref[...], preferred_element_type=F32).astype(o_ref.dtype)
    n = PERM_BLK // r
    for k in range(nblk):
        for s in range(r):
            o_ref[0, s, k * n:(k + 1) * n, :] = res[k * PERM_BLK + s * n:k * PERM_BLK + (s + 1) * n, :]


def _inproj_perm(x3, g, w, r):
    B, S, D = x3.shape
    N = w.shape[1]
    tm = min(TM_PERM, S)
    n = PERM_BLK // r
    o = jnp.arange(PERM_BLK, dtype=jnp.int32)
    src = (o % n) * r + o // n
    perm = (src[:, None] == jnp.arange(PERM_BLK, dtype=jnp.int32)[None, :]).astype(BF16)
    kern = functools.partial(_inproj_perm_kernel, r=r)
    return pl.pallas_call(
        kern,
        out_shape=jax.ShapeDtypeStruct((B, r, S // r, N), BF16),
        grid=(B, S // tm),
        in_specs=[pl.BlockSpec((1, tm, D), lambda b, i: (b, i, 0)),
                  pl.BlockSpec((1, D), lambda b, i: (0, 0)),
                  pl.BlockSpec((PERM_BLK, PERM_BLK), lambda b, i: (0, 0)),
                  pl.BlockSpec((D, N), lambda b, i: (0, 0))],
        out_specs=pl.BlockSpec((1, r, tm // r, N), lambda b, i: (b, 0, i, 0)),
        compiler_params=_cparams(("parallel", "parallel")),
        name=f"inproj_perm_{r}",
    )(x3, g, perm, w)


def _attn_a_kernel(lam_ref, cfar_ref, q_ref, k_ref, vt_ref, bias_ref, g_ref, o_ref,
                   st0_scr, st1_scr, m0_scr, m1_scr, acc_scr, l_scr, *, out_scale, nq, n_blocks):
    k = pl.program_id(0)
    t = T_ATT
    nk = k_ref.shape[1] // t
    n1 = jnp.minimum(k // 2, n_blocks - 1)
    n2 = jnp.maximum(k - 1, 0) // 2
    h1, qi1 = (n1 // nq) % HA, n1 % nq
    h2, qi2 = (n2 // nq) % HA, n2 % nq

    @pl.when(k == 0)
    def _():
        st1_scr[...] = jnp.zeros(st1_scr.shape, F32)
        m1_scr[...] = jnp.zeros(m1_scr.shape, F32)
        acc_scr[...] = jnp.zeros(acc_scr.shape, F32)
        l_scr[...] = jnp.ones(l_scr.shape, F32)

    low_half = lax.broadcasted_iota(jnp.int32, (1, LANES), 1) < HALF_LANES

    def both(cmap, st_w, m_w, st_r, m_r):
        q = q_ref[0]
        zero = jnp.zeros_like(q)
        qc = jnp.where(low_half, q, zero) if cmap == 0 else jnp.where(low_half, zero, q)
        m_prev = m_r[...]
        l = jnp.zeros((1, t), F32)
        acc = jnp.zeros((LANES, t), F32)
        m_new = None
        for j, d in enumerate(range(-1, nk - 1)):
            a1 = lax.rem(qi1 + (d + nk), nk)
            delta1 = a1 - qi1
            kb = k_ref[0, pl.ds(pl.multiple_of(a1 * t, t), t), :]
            st = lax.dot_general(kb, qc, (((1,), (1,)), ((), ())), preferred_element_type=F32)
            if d <= 1:
                st = st + bias_ref[0, jnp.clip(delta1, -2, 2) + 2]
                cm = jnp.max(st, axis=0, keepdims=True)
            else:
                cm = (jnp.max(st, axis=0, keepdims=True)
                      + cfar_ref[2 * h1 + (delta1 > 0).astype(jnp.int32)])
            st_w[j] = st
            m_new = cm if m_new is None else jnp.maximum(m_new, cm)

            a2 = lax.rem(qi2 + (d + nk), nk)
            if d <= 1:
                shifted = m_prev
            else:
                shifted = m_prev - cfar_ref[2 * h2 + (a2 > qi2).astype(jnp.int32)]
            p = jnp.exp2(st_r[j] - shifted)
            l = l + jnp.sum(p, axis=0, keepdims=True)
            acc = acc + jnp.dot(vt_ref[0, 0, a2], p.astype(BF16), preferred_element_type=F32)
        m_w[...] = m_new
        return l, acc

    @pl.when(k % 2 == 0)
    def _():
        l1, acc1 = both(0, st0_scr, m0_scr, st1_scr, m1_scr)
        ot = acc_scr[...] / l_scr[...] - lam_ref[0] * (acc1 / l1)
        o = ot.T
        ms = jnp.mean(o * o, axis=-1, keepdims=True)
        o_ref[0] = (o * lax.rsqrt(ms + EPS) * g_ref[...] * out_scale).astype(o_ref.dtype)

    @pl.when(k % 2 == 1)
    def _():
        l0, acc0 = both(1, st1_scr, m1_scr, st0_scr, m0_scr)
        acc_scr[...] = acc0
        l_scr[...] = l0


def _attn_a(proj3, vt, lam, cfar, bias5, subln_g, lam_init):
    B, S, _ = proj3.shape
    t = T_ATT
    nk = S // t
    n_blocks = B * HA * nk

    def scored(k):
        n = jnp.minimum(k // 2, n_blocks - 1)
        return n // (HA * nk), (n // nk) % HA, n % nk

    def lagged(k, lag):
        n = jnp.maximum(k - lag, 0) // 2
        return n // (HA * nk), (n // nk) % HA, n % nk

    def q_map(k):
        b, h, qi = scored(k)
        return b, qi, h

    def k_map(k):
        b, h, _ = scored(k)
        return b, 0, HA + h

    def vt_map(k):
        b, h, _ = lagged(k, 1)
        return b, h, 0, 0, 0

    def out_map(k):
        b, h, qi = lagged(k, 2)
        return b, qi, h

    kern = functools.partial(_attn_a_kernel, out_scale=1.0 - lam_init, nq=nk, n_blocks=n_blocks)
    return pl.pallas_call(
        kern,
        out_shape=jax.ShapeDtypeStruct((B, S, HA * 2 * DA), BF16),
        grid=(2 * n_blocks + 1,),
        in_specs=[
            pl.BlockSpec(memory_space=pltpu.SMEM),
            pl.BlockSpec(memory_space=pltpu.SMEM),
            pl.BlockSpec((1, t, LANES), q_map),
            pl.BlockSpec((1, S, LANES), k_map),
            pl.BlockSpec((1, 1, nk, LANES, t), vt_map),
            pl.BlockSpec((1, 5, t, t), lambda k: (scored(k)[1], 0, 0, 0)),
            pl.BlockSpec((1, LANES), lambda k: (0, 0)),
        ],
        out_specs=pl.BlockSpec((1, t, LANES), out_map),
        scratch_shapes=[pltpu.VMEM((nk, t, t), F32), pltpu.VMEM((nk, t, t), F32),
                        pltpu.VMEM((1, t), F32), pltpu.VMEM((1, t), F32),
                        pltpu.VMEM((LANES, t), F32), pltpu.VMEM((1, t), F32)],
        compiler_params=_cparams(("arbitrary",)),
        name="diff_attn",
    )(lam, cfar, proj3, proj3, vt, bias5, subln_g)


def _attn_a_bias(rel_bias):
    t = T_ATT
    tab = rel_bias[:, :HA].astype(F32) * LOG2E
    d = jnp.arange(-1, 2, dtype=jnp.int32)[:, None, None] * t
    rel = d + jnp.arange(t, dtype=jnp.int32)[None, :, None] - jnp.arange(t, dtype=jnp.int32)[None, None, :]
    near = _bias_lookup(_t5_bucket(rel), tab)
    far = tab[_t5_bucket(jnp.array([-(t + 1), t + 1], dtype=jnp.int32))].T
    fill = lambda side: jnp.broadcast_to(far[:, side, None, None, None], (HA, 1, t, t))
    tiles = jnp.concatenate([fill(0), near, fill(1)], axis=1)
    return tiles, far.reshape(2 * HA)


def _attn_b_kernel(q_ref, k_ref, v_ref, bias_ref, o_ref, lse_ref, *, sub_len, r, sp, qp):
    nblk = sub_len // QB_DIL
    low_half = lax.broadcasted_iota(jnp.int32, (1, LANES), 1) < HALF_LANES
    for si in range(sp):
        s = si if sp == r else pl.program_id(2) * sp + si
        for qb in range(qp):
            i = pl.program_id(1) * qp + qb
            start = jnp.clip(i * QB_DIL - HALF_WIN, 0, sub_len - KW_DIL)
            start = pl.multiple_of(start, HALF_WIN)
            variant = jnp.where(i == 0, 0, jnp.where(i == nblk - 1, 2, 1))
            q = q_ref[0, si, qb * QB_DIL:(qb + 1) * QB_DIL, :]
            kw = k_ref[0, s, pl.ds(start, KW_DIL), :]
            vw = v_ref[0, s, pl.ds(start, KW_DIL), :]
            rows = (slice(qb * QB_DIL, (qb + 1) * QB_DIL) if r == 1
                    else pl.ds(qb * QB_DIL * r + s, QB_DIL, stride=r))
            lane = lax.broadcasted_iota(jnp.int32, (QB_DIL, LANES), 1)
            lse_tile = jnp.zeros((QB_DIL, LANES), F32)
            for j in range(HB // 2):
                cols = slice(j * LANES, (j + 1) * LANES)
                qpair, kp, vp = q[:, cols], kw[:, cols], vw[:, cols]
                outs, lses = [], []
                for c in range(2):
                    qc = jnp.where(low_half if c == 0 else jnp.logical_not(low_half), qpair,
                                   jnp.zeros_like(qpair))
                    sc = lax.dot_general(qc, kp, (((1,), (1,)), ((), ())),
                                         preferred_element_type=F32)
                    sc = sc + bias_ref[2 * j + c, variant]
                    m = jnp.max(sc, axis=-1, keepdims=True)
                    p = jnp.exp(sc - m)
                    l = jnp.sum(p, axis=-1, keepdims=True)
                    outs.append(jnp.dot(p.astype(BF16), vp, preferred_element_type=F32) / l)
                    lses.append(m + jnp.log(l))
                o_ref[0, j, rows, :] = jnp.where(low_half, outs[0], outs[1])
                for c in range(2):
                    lse_tile = jnp.where(lane == 2 * j + c, lses[c], lse_tile)
            lse_ref[0, rows, :] = lse_tile


def _attn_b(qkv4, bias3, g, cols):
    B, r, L, _ = qkv4.shape
    S = r * L
    width = HB * DB
    nblk = L // QB_DIL
    sp = min(r, SUBS_DIL)
    qp = max(1, min(ITEMS_DIL // sp, OUT_ROWS_DIL // (QB_DIL * r)))
    qcol, kcol, vcol = cols
    kern = functools.partial(_attn_b_kernel, sub_len=L, r=r, sp=sp, qp=qp)
    slab = jax.ShapeDtypeStruct((B, N_SLABS, S, LANES), F32)
    slab_spec = pl.BlockSpec((1, N_SLABS, QB_DIL * r * qp, LANES), lambda b, i, s: (b, 0, i, 0))
    return pl.pallas_call(
        kern,
        out_shape=[slab, jax.ShapeDtypeStruct((B, S, LANES), F32)],
        grid=(B, nblk // qp, r // sp),
        in_specs=[
            pl.BlockSpec((1, sp, QB_DIL * qp, width), lambda b, i, s: (b, s, i, qcol)),
            pl.BlockSpec((1, r, L, width), lambda b, i, s: (b, 0, 0, kcol)),
            pl.BlockSpec((1, r, L, width), lambda b, i, s: (b, 0, 0, vcol)),
            pl.BlockSpec((HB, 3, QB_DIL, KW_DIL), lambda b, i, s: (0, 0, 0, 0)),
        ],
        out_specs=[slab_spec, pl.BlockSpec((1, QB_DIL * r * qp, LANES), lambda b, i, s: (b, i, 0))],
        compiler_params=_cparams(("parallel", "arbitrary", "arbitrary")),
        name=f"dilated_attn_{g}",
    )(qkv4, qkv4, qkv4, bias3)


def _attn_b_bias(rel_bias, g):
    r = DILATIONS[g]
    tab = rel_bias[:, HA + g * HB: HA + (g + 1) * HB].astype(F32)
    off = jnp.arange(3, dtype=jnp.int32)[:, None, None] * HALF_WIN
    rel = (jnp.arange(KW_DIL, dtype=jnp.int32)[None, None, :] - off
           - jnp.arange(QB_DIL, dtype=jnp.int32)[None, :, None])
    bias = _bias_lookup(_t5_bucket(rel * r), tab)
    return jnp.where((jnp.abs(rel) <= HALF_WIN)[None], bias, NEG)


def _sgu_kernel(zu_ref, zv_ref, lng_ref, lnb_ref, ws_ref, bs_ref, o_ref):
    u = jax.nn.gelu(zu_ref[...].astype(F32))
    v = jax.nn.gelu(zv_ref[...].astype(F32))
    mu = jnp.mean(v, axis=-1, keepdims=True)
    var = jnp.mean(jnp.square(v - mu), axis=-1, keepdims=True)
    v = ((v - mu) * lax.rsqrt(var + EPS) * lng_ref[...] + lnb_ref[...]).astype(BF16)
    gd = v.shape[1] // C_GROUPS
    for n in range(v.shape[0] // CHUNK):
        rows = slice(n * CHUNK, (n + 1) * CHUNK)
        for g in range(C_GROUPS):
            cols = slice(g * gd, (g + 1) * gd)
            mixed = jnp.dot(ws_ref[g], v[rows, cols], preferred_element_type=F32) + bs_ref[:, cols]
            o_ref[rows, cols] = (u[rows, cols] * mixed).astype(o_ref.dtype)


def _sgu(proj2, ln_g, ln_b, w_s, b_exp):
    T = proj2.shape[0]
    tm = min(TM_SGU, T)
    w = MIX_W
    return pl.pallas_call(
        _sgu_kernel,
        out_shape=jax.ShapeDtypeStruct((T, w), BF16),
        grid=(T // tm,),
        in_specs=[pl.BlockSpec((tm, w), lambda i: (i, COL_ZU)),
                  pl.BlockSpec((tm, w), lambda i: (i, COL_ZU + 1)),
                  pl.BlockSpec((1, w), lambda i: (0, 0)),
                  pl.BlockSpec((1, w), lambda i: (0, 0)),
                  pl.BlockSpec((C_GROUPS, CHUNK, CHUNK), lambda i: (0, 0, 0)),
                  pl.BlockSpec((CHUNK, w), lambda i: (0, 0))],
        out_specs=pl.BlockSpec((tm, w), lambda i: (i, 0)),
        compiler_params=_cparams(("parallel",)),
        name="sgu",
    )(proj2, proj2, ln_g, ln_b, w_s, b_exp)


def _route(logits):
    lane = lax.broadcasted_iota(jnp.int32, logits.shape, 1)
    big = jnp.int32(LANES)
    is_grp = (lane >= N_EXPERTS) & (lane < N_EXPERTS + N_GROUPS)
    gl = jnp.where(is_grp, logits, NEG)
    gmax = jnp.max(gl, axis=-1, keepdims=True)
    g_idx = jnp.min(jnp.where(is_grp & (gl == gmax), lane, big), axis=-1, keepdims=True) - N_EXPERTS
    g_w = 1.0 / jnp.sum(jnp.where(is_grp, jnp.exp(gl - gmax), 0.0), axis=-1, keepdims=True)
    in_grp = (lane >= g_idx * E_PER_GROUP) & (lane < (g_idx + 1) * E_PER_GROUP)
    sel = jnp.where(in_grp, logits, NEG)
    v1 = jnp.max(sel, axis=-1, keepdims=True)
    i1 = jnp.min(jnp.where(in_grp & (sel == v1), lane, big), axis=-1, keepdims=True)
    rest = in_grp & (lane != i1)
    sel2 = jnp.where(rest, logits, NEG)
    v2 = jnp.max(sel2, axis=-1, keepdims=True)
    i2 = jnp.min(jnp.where(rest & (sel2 == v2), lane, big), axis=-1, keepdims=True)
    e2 = jnp.exp(v2 - v1)
    w1 = g_w / (1.0 + e2)
    w2 = g_w * e2 / (1.0 + e2)
    return jnp.where(lane == i1, w1, jnp.where(lane == i2, w2, 0.0))


def _mix_kernel(x_ref, ya_ref, ob0_ref, ob1_ref, ob2_ref, ls0_ref, ls1_ref, ls2_ref, yc_ref,
                g0_ref, g1_ref, g2_ref, wb_ref, wo_ref, nf_ref, wr_ref, br_ref,
                xo_ref, h_ref, comb_ref):
    ls0, ls1, ls2 = ls0_ref[...], ls1_ref[...], ls2_ref[...]
    mx = jnp.maximum(jnp.maximum(ls0, ls1), ls2)
    es = [jnp.exp(ls0 - mx), jnp.exp(ls1 - mx), jnp.exp(ls2 - mx)]
    inv = 1.0 / (es[0] + es[1] + es[2])
    spread = jnp.where(lax.broadcasted_iota(jnp.int32, (2 * LANES, MIX_W), 1) // DB
                       == lax.broadcasted_iota(jnp.int32, (2 * LANES, MIX_W), 0) % LANES,
                       1.0, 0.0).astype(BF16)
    yb = None
    for e, ob_ref in zip(es, (ob0_ref, ob1_ref, ob2_ref)):
        w = e * inv
        hi = w.astype(BF16)
        lo = (w - hi.astype(F32)).astype(BF16)
        wide = jnp.dot(jnp.concatenate([hi, lo], axis=1), spread, preferred_element_type=F32)
        term = wide * jnp.concatenate([ob_ref[0, j] for j in range(N_SLABS)], axis=-1)
        yb = term if yb is None else yb + term
    yb = yb.astype(BF16)
    merged = jax.nn.sigmoid(g0_ref[...].astype(F32)) * jnp.dot(ya_ref[...], wb_ref[0],
                                                               preferred_element_type=F32)
    merged += jax.nn.sigmoid(g1_ref[...].astype(F32)) * jnp.dot(yb, wb_ref[1],
                                                                preferred_element_type=F32)
    merged += jax.nn.sigmoid(g2_ref[...].astype(F32)) * jnp.dot(yc_ref[...], wb_ref[2],
                                                                preferred_element_type=F32)
    xn = x_ref[...] + jnp.dot(merged.astype(BF16), wo_ref[...], preferred_element_type=F32)
    xo_ref[...] = xn
    h = _rms_bf16(xn, nf_ref[...])
    h_ref[...] = h
    logits = jnp.dot(h, wr_ref[...], preferred_element_type=F32) + br_ref[...]
    comb_ref[...] = _route(logits)


def _mix(x2, ya, obs, lses, yc, proj2, wb, wo, nf, wr, br):
    T, D = x2.shape
    S = obs[0].shape[2]
    tm = min(TM_MIX, S)
    per_b = S // tm
    w = MIX_W
    row = lambda width: pl.BlockSpec((tm, width), lambda i: (i, 0))
    full = lambda a: pl.BlockSpec(a.shape, lambda i: (0,) * a.ndim)
    gate = lambda n: pl.BlockSpec((tm, D), lambda i: (i, COL_GATE + n))
    slab = pl.BlockSpec((1, N_SLABS, tm, LANES), lambda i: (i // per_b, 0, i % per_b, 0))
    return pl.pallas_call(
        _mix_kernel,
        out_shape=[jax.ShapeDtypeStruct((T, D), F32), jax.ShapeDtypeStruct((T, D), BF16),
                   jax.ShapeDtypeStruct((T, LANES), F32)],
        grid=(T // tm,),
        in_specs=[row(D), row(w), slab, slab, slab, row(LANES), row(LANES), row(LANES), row(w),
                  gate(0), gate(1), gate(2), full(wb), full(wo), full(nf), full(wr), full(br)],
        out_specs=[row(D), row(D), row(LANES)],
        compiler_params=_cparams(("parallel",)),
        name="mix",
    )(x2, ya, obs[0], obs[1], obs[2], lses[0], lses[1], lses[2], yc, proj2, proj2, proj2,
      wb, wo, nf, wr, br)


def _moe_kernel(h_ref, comb_ref, x_ref, wg_ref, wu_ref, wd_ref, nfin_ref, o_ref, acc_scr,
                *, final_norm):
    e = pl.program_id(1)

    @pl.when(e == 0)
    def _():
        acc_scr[...] = jnp.zeros(acc_scr.shape, F32)

    h = h_ref[...]
    lane = lax.broadcasted_iota(jnp.int32, comb_ref.shape, 1)
    c = jnp.sum(jnp.where(lane == e, comb_ref[...], 0.0), axis=-1, keepdims=True)
    hid = (jax.nn.silu(jnp.dot(h, wg_ref[0].astype(BF16), preferred_element_type=F32))
           * jnp.dot(h, wu_ref[0].astype(BF16), preferred_element_type=F32))
    acc_scr[...] += c * jnp.dot(hid.astype(BF16), wd_ref[0].astype(BF16),
                                preferred_element_type=F32)

    @pl.when(e == pl.num_programs(1) - 1)
    def _():
        xn = x_ref[...] + acc_scr[...]
        if final_norm:
            ms = jnp.mean(xn * xn, axis=-1, keepdims=True)
            xn = xn * lax.rsqrt(ms + EPS) * nfin_ref[...]
        o_ref[...] = xn


def _moe_dense(h, comb, x2, wg, wu, wd, e0, nfin, final_norm):
    T, D = x2.shape
    tm = min(TM_MOE, T)
    F = wg.shape[2]
    kern = functools.partial(_moe_kernel, final_norm=final_norm)
    return pl.pallas_call(
        kern,
        out_shape=jax.ShapeDtypeStruct((T, D), F32),
        grid=(T // tm, N_EXPERTS),
        in_specs=[pl.BlockSpec((tm, D), lambda i, e: (i, 0)),
                  pl.BlockSpec((tm, LANES), lambda i, e: (i, 0)),
                  pl.BlockSpec((tm, D), lambda i, e: (i, 0)),
                  pl.BlockSpec((1, D, F), lambda i, e: (e0 + e, 0, 0)),
                  pl.BlockSpec((1, D, F), lambda i, e: (e0 + e, 0, 0)),
                  pl.BlockSpec((1, F, D), lambda i, e: (e0 + e, 0, 0)),
                  pl.BlockSpec((1, D), lambda i, e: (0, 0))],
        out_specs=pl.BlockSpec((tm, D), lambda i, e: (i, 0)),
        scratch_shapes=[pltpu.VMEM((tm, D), F32)],
        compiler_params=_cparams(("parallel", "arbitrary")),
        name="moe_dense",
    )(h, comb, x2, wg, wu, wd, nfin)


def _moe_dispatch_kernel(h_ref, comb_ref, o_ref, cnt_ref):
    nt = o_ref.shape[0]
    tm = h_ref.shape[0] // nt
    before = jnp.where(lax.broadcasted_iota(jnp.int32, (tm, tm), 0)
                       < lax.broadcasted_iota(jnp.int32, (tm, tm), 1), 1.0, 0.0).astype(BF16)
    slot = lax.broadcasted_iota(jnp.int32, (MOE_CAP, tm), 0).astype(F32)
    for u in range(nt):
        rows = slice(u * tm, (u + 1) * tm)
        comb = comb_ref[rows, :]
        hi = comb.astype(BF16)
        lo = (comb - hi.astype(F32)).astype(BF16)
        haug = jnp.concatenate([h_ref[rows, :], hi, lo], axis=1)
        a_t = comb.T[:N_EXPERTS] > 0.0
        a_f = jnp.where(a_t, 1.0, 0.0)
        rank_t = jnp.dot(a_f.astype(BF16), before, preferred_element_type=F32)
        blocks = [jnp.where((slot == rank_t[e:e + 1]) & a_t[e:e + 1], 1.0, 0.0).astype(BF16)
                  for e in range(N_EXPERTS)]
        res = jnp.dot(jnp.concatenate(blocks, axis=0), haug, preferred_element_type=F32)
        res = res.astype(o_ref.dtype)
        for e in range(N_EXPERTS):
            o_ref[u, e] = res[e * MOE_CAP:(e + 1) * MOE_CAP]
        cnt_ref[u] = jnp.broadcast_to(jnp.sum(a_f, axis=1, keepdims=True), cnt_ref.shape[1:])


def _moe_dispatch(h, comb):
    T, D = h.shape
    tm = min(TM_DISP, T)
    n = T // tm
    nt = math.gcd(TILES_PER_STEP, n)
    return pl.pallas_call(
        _moe_dispatch_kernel,
        out_shape=[jax.ShapeDtypeStruct((n, N_EXPERTS, MOE_CAP, D + 2 * LANES), BF16),
                   jax.ShapeDtypeStruct((n, N_EXPERTS, LANES), F32)],
        grid=(n // nt,),
        in_specs=[pl.BlockSpec((nt * tm, D), lambda i: (i, 0)),
                  pl.BlockSpec((nt * tm, LANES), lambda i: (i, 0))],
        out_specs=[pl.BlockSpec((nt, N_EXPERTS, MOE_CAP, D + 2 * LANES), lambda i: (i, 0, 0, 0)),
                   pl.BlockSpec((nt, N_EXPERTS, LANES), lambda i: (i, 0, 0))],
        compiler_params=_cparams(("parallel",)),
        name="moe_dispatch",
    )(h, comb)


def _moe_ffn_kernel(n16_ref, s_ref, wg_ref, wu_ref, wd_ref, o_ref,
                    wg_scr, wu_scr, wd_scr, lhs_scr, y_scr):
    e, c = pl.program_id(0), pl.program_id(1)

    @pl.when(c == 0)
    def _():
        wg_scr[...] = wg_ref[0].astype(BF16)
        wu_scr[...] = wu_ref[0].astype(BF16)
        wd_scr[...] = wd_ref[0].astype(BF16)

    g, _, cap, _ = s_ref.shape
    D = o_ref.shape[-1]
    @pl.when((e == 0) & (c == 0))
    def _():
        lhs_scr[...] = jnp.zeros(lhs_scr.shape, lhs_scr.dtype)
        y_scr[...] = jnp.zeros(y_scr.shape, y_scr.dtype)

    offs = []
    off = jnp.int32(0)
    for t in range(g):
        offs.append(off)
        lhs_scr[pl.ds(pl.multiple_of(off, BF16_ROWS), cap), :] = s_ref[t, 0]
        off = off + n16_ref[(c * g + t) * N_EXPERTS + e]
    total = off

    def run(nrows):
        rows = lhs_scr[:nrows]
        h = rows[:, :D]
        wparts = rows[:, D:].astype(F32)
        lane = lax.broadcasted_iota(jnp.int32, wparts.shape, 1)
        w = jnp.sum(jnp.where(lane % LANES == e, wparts, 0.0), axis=-1, keepdims=True)
        hid = (jax.nn.silu(jnp.dot(h, wg_scr[...], preferred_element_type=F32))
               * jnp.dot(h, wu_scr[...], preferred_element_type=F32))
        y = w * jnp.dot(hid.astype(BF16), wd_scr[...], preferred_element_type=F32)
        y_scr[:nrows] = y.astype(y_scr.dtype)

    classes = tuple(range(g * cap // 2, g * cap + 1, FFN_ROW_STEP))
    lower = 0
    for nrows in classes:
        pl.when((total > lower) & (total <= nrows))(functools.partial(run, nrows))
        lower = nrows

    for t in range(g):
        o_ref[t, 0] = y_scr[pl.ds(pl.multiple_of(offs[t], BF16_ROWS), cap), :]


def _moe_ffn(srt, n16, wg, wu, wd, e0):
    n, ne, cap, wdt = srt.shape
    D, F = wg.shape[1], wg.shape[2]
    g = math.gcd(G_FFN, n)
    return pl.pallas_call(
        _moe_ffn_kernel,
        out_shape=jax.ShapeDtypeStruct((n, ne, cap, D), BF16),
        grid_spec=pltpu.PrefetchScalarGridSpec(
            num_scalar_prefetch=1,
            grid=(ne, n // g),
            in_specs=[pl.BlockSpec((g, 1, cap, wdt), lambda e, c, n16: (c, e, 0, 0)),
                      pl.BlockSpec((1, D, F), lambda e, c, n16: (e0 + e, 0, 0)),
                      pl.BlockSpec((1, D, F), lambda e, c, n16: (e0 + e, 0, 0)),
                      pl.BlockSpec((1, F, D), lambda e, c, n16: (e0 + e, 0, 0))],
            out_specs=pl.BlockSpec((g, 1, cap, D), lambda e, c, n16: (c, e, 0, 0)),
            scratch_shapes=[pltpu.VMEM((D, F), BF16), pltpu.VMEM((D, F), BF16),
                            pltpu.VMEM((F, D), BF16), pltpu.VMEM((g * cap, wdt), BF16),
                            pltpu.VMEM((g * cap, D), BF16)]),
        compiler_params=_cparams(("arbitrary", "arbitrary")),
        name="moe_ffn",
    )(n16, srt, wg, wu, wd)


def _moe_combine_kernel(skip_ref, y_ref, comb_ref, x_ref, nfin_ref, o_ref, *, final_norm):
    nt = y_ref.shape[0]
    tm = x_ref.shape[0] // nt
    ncol = N_EXPERTS * MOE_CAP
    before = jnp.where(lax.broadcasted_iota(jnp.int32, (tm, tm), 1)
                       < lax.broadcasted_iota(jnp.int32, (tm, tm), 0), 1.0, 0.0).astype(BF16)
    spread = jnp.where(lax.broadcasted_iota(jnp.int32, (LANES, ncol), 1) // MOE_CAP
                       == lax.broadcasted_iota(jnp.int32, (LANES, ncol), 0), 1.0, 0.0).astype(BF16)
    slot = (lax.broadcasted_iota(jnp.int32, (tm, ncol), 1) % MOE_CAP).astype(F32)
    for u in range(nt):
        rows = slice(u * tm, (u + 1) * tm)
        a = comb_ref[rows, :] > 0.0
        rank = jnp.dot(before, jnp.where(a, 1.0, 0.0).astype(BF16), preferred_element_type=F32)
        key = jnp.where(a, rank, -1.0).astype(BF16)
        key_all = jnp.dot(key, spread, preferred_element_type=F32)
        pc = jnp.where(slot == key_all, 1.0, 0.0).astype(BF16)
        y = jnp.concatenate([y_ref[u, e] for e in range(N_EXPERTS)], axis=0)
        xn = x_ref[rows, :] + jnp.dot(pc, y, preferred_element_type=F32)
        if final_norm:
            ms = jnp.mean(xn * xn, axis=-1, keepdims=True)
            later = skip_ref[pl.program_id(0) * nt + u] == 1
            xn = jnp.where(later, xn, xn * lax.rsqrt(ms + EPS) * nfin_ref[...])
        o_ref[rows, :] = xn


def _moe_combine(skip, y, comb, x2, nfin, final_norm):
    T, D = x2.shape
    n, ne, cap, _ = y.shape
    tm = T // n
    nt = math.gcd(TILES_PER_STEP, n)
    kern = functools.partial(_moe_combine_kernel, final_norm=final_norm)
    return pl.pallas_call(
        kern,
        out_shape=jax.ShapeDtypeStruct((T, D), F32),
        grid_spec=pltpu.PrefetchScalarGridSpec(
            num_scalar_prefetch=1,
            grid=(n // nt,),
            in_specs=[pl.BlockSpec((nt, ne, cap, D), lambda i, sk: (i, 0, 0, 0)),
                      pl.BlockSpec((nt * tm, LANES), lambda i, sk: (i, 0)),
                      pl.BlockSpec((nt * tm, D), lambda i, sk: (i, 0)),
                      pl.BlockSpec((1, D), lambda i, sk: (0, 0))],
            out_specs=pl.BlockSpec((nt * tm, D), lambda i, sk: (i, 0))),
        compiler_params=_cparams(("parallel",)),
        name="moe_combine",
    )(skip, y, comb, x2, nfin)


def _moe_fix_kernel(tiles_ref, experts_ref, first_ref, last_ref, n_ref, h_ref, comb_ref, prev_ref,
                    wg_ref, wu_ref, wd_ref, nfin_ref, o_ref, *, final_norm):
    del tiles_ref
    s = pl.program_id(0)

    @pl.when(s < n_ref[0])
    def _():
        e = experts_ref[s]
        tm = h_ref.shape[0]
        comb = comb_ref[...]
        a = jnp.where(comb > 0.0, 1.0, 0.0)
        before = (lax.broadcasted_iota(jnp.int32, (tm, tm), 1)
                  < lax.broadcasted_iota(jnp.int32, (tm, tm), 0))
        rank = jnp.dot(jnp.where(before, 1.0, 0.0).astype(BF16), a.astype(BF16),
                       preferred_element_type=F32)
        lane = lax.broadcasted_iota(jnp.int32, comb.shape, 1)
        dropped = (lane == e) & (rank >= MOE_CAP)
        c = jnp.sum(jnp.where(dropped, comb, 0.0), axis=-1, keepdims=True)
        h = h_ref[...]
        hid = (jax.nn.silu(jnp.dot(h, wg_ref[0].astype(BF16), preferred_element_type=F32))
               * jnp.dot(h, wu_ref[0].astype(BF16), preferred_element_type=F32))
        add = c * jnp.dot(hid.astype(BF16), wd_ref[0].astype(BF16), preferred_element_type=F32)
        fresh = first_ref[s] == 1

        @pl.when(fresh)
        def _():
            o_ref[...] = prev_ref[...] + add

        @pl.when(jnp.logical_not(fresh))
        def _():
            o_ref[...] += add

        if final_norm:
            @pl.when(last_ref[s] == 1)
            def _():
                xn = o_ref[...]
                ms = jnp.mean(xn * xn, axis=-1, keepdims=True)
                o_ref[...] = xn * lax.rsqrt(ms + EPS) * nfin_ref[...]


def _moe_fix(tiles, experts, first, last, n, out, h, comb, wg, wu, wd, e0, nfin, final_norm):
    T, D = out.shape
    tm = min(TM_DISP, T)
    F = wg.shape[2]
    tile = lambda width: pl.BlockSpec((tm, width), lambda s, tl, ex, fi, la, n: (tl[s], 0))
    wspec = lambda shape: pl.BlockSpec(shape, lambda s, tl, ex, fi, la, n: (e0 + ex[s], 0, 0))
    kern = functools.partial(_moe_fix_kernel, final_norm=final_norm)
    return pl.pallas_call(
        kern,
        out_shape=jax.ShapeDtypeStruct((T, D), F32),
        grid_spec=pltpu.PrefetchScalarGridSpec(
            num_scalar_prefetch=5,
            grid=(MAX_OVF,),
            in_specs=[tile(D), tile(LANES), tile(D), wspec((1, D, F)), wspec((1, D, F)),
                      wspec((1, F, D)), pl.BlockSpec((1, D), lambda s, tl, ex, fi, la, n: (0, 0))],
            out_specs=tile(D)),
        input_output_aliases={7: 0},
        compiler_params=_cparams(("arbitrary",)),
        name="moe_fix",
    )(tiles, experts, first, last, n, h, comb, out, wg, wu, wd, nfin)


def _moe(h, comb, x2, wg, wu, wd, e0, nfin, final_norm):
    srt, cnt = _moe_dispatch(h, comb)
    over = (cnt[:, :, 0] > MOE_CAP).reshape(-1)
    n_ovf = jnp.sum(over.astype(jnp.int32))
    pairs = jnp.nonzero(over, size=MAX_OVF, fill_value=0)[0].astype(jnp.int32)
    pairs = jnp.where(jnp.arange(MAX_OVF) < n_ovf, pairs, pairs[jnp.clip(n_ovf - 1, 0, MAX_OVF - 1)])
    tiles, experts = pairs // N_EXPERTS, pairs % N_EXPERTS
    change = (tiles[1:] != tiles[:-1]).astype(jnp.int32)
    first = jnp.concatenate([jnp.ones((1,), jnp.int32), change])
    last = jnp.maximum(jnp.concatenate([change, jnp.ones((1,), jnp.int32)]),
                       (jnp.arange(MAX_OVF) == n_ovf - 1).astype(jnp.int32))
    skip = jnp.any(over.reshape(-1, N_EXPERTS), axis=1).astype(jnp.int32)

    used = jnp.minimum(cnt[:, :, 0], MOE_CAP).astype(jnp.int32).reshape(-1)
    n16 = (used + (BF16_ROWS - 1)) // BF16_ROWS * BF16_ROWS

    def routed():
        out = _moe_combine(skip, _moe_ffn(srt, n16, wg, wu, wd, e0), comb, x2, nfin, final_norm)
        return lax.cond(
            n_ovf > 0,
            lambda: _moe_fix(tiles, experts, first, last, n_ovf.reshape(1), out, h, comb,
                             wg, wu, wd, e0, nfin, final_norm),
            lambda: out)

    return lax.cond(n_ovf > MAX_OVF,
                    lambda: _moe_dense(h, comb, x2, wg, wu, wd, e0, nfin, final_norm), routed)


def kernel(x, rel_bias, norm_mix, w_in, diff_lambda, diff_subln, sgu_ln_g, sgu_ln_b, sgu_w, sgu_b,
           w_branch, w_out, norm_ffn, w_router_grp, b_router_grp, w_router_exp, b_router_exp,
           w_gate, w_up, w_down, norm_final):
    B, S, D = x.shape
    T = B * S
    depth = w_in.shape[0]
    a_out = HA * 2 * DA
    grp_w = HB * DB
    b_cols = 3 * NG_B * grp_w
    qkv_b0 = 3 * a_out
    zc0 = qkv_b0 + b_cols
    gate0 = zc0 + 2 * MIX_W
    qk_scale = DA ** -0.5

    bias_a, cfar = _attn_a_bias(rel_bias)
    bias_b = [_attn_b_bias(rel_bias, g) for g in range(NG_B)]

    col = jnp.arange(w_in.shape[2])
    col_scale = jnp.where(col < a_out, qk_scale * LOG2E,
                          jnp.where((col >= qkv_b0) & (col < qkv_b0 + NG_B * grp_w), qk_scale, 1.0))

    def group_cols(w, g):
        return [w[:, qkv_b0 + (c * NG_B + g) * grp_w: qkv_b0 + (c * NG_B + g + 1) * grp_w]
                for c in range(3)]

    wg_all = w_gate.reshape((-1,) + w_gate.shape[2:])
    wu_all = w_up.reshape((-1,) + w_up.shape[2:])
    wd_all = w_down.reshape((-1,) + w_down.shape[2:])

    x2 = x.reshape(T, D)
    for i in range(depth):
        w = (w_in[i] * col_scale.astype(F32)).astype(BF16)
        nm = norm_mix[i][None, :]
        w_main = jnp.concatenate([w[:, :2 * a_out], w[:, zc0:]] + group_cols(w, 0), axis=1)
        proj2 = _inproj(x2, nm, w_main)
        x3 = x2.reshape(B, S, D)
        vt = _inproj_t(x3, nm, w[:, 2 * a_out:3 * a_out].T)
        proj3 = proj2.reshape(B, S, proj2.shape[1])

        lam_init = 0.8 - 0.6 * math.exp(-0.3 * i)
        lp = diff_lambda[i].astype(F32)
        lam = jnp.exp(jnp.sum(lp[0] * lp[1])) - jnp.exp(jnp.sum(lp[2] * lp[3])) + lam_init
        ya = _attn_a(proj3, vt, lam.reshape(1), cfar, bias_a, diff_subln[i][None, :], lam_init)

        obs, lses = [], []
        for g in range(NG_B):
            r = DILATIONS[g]
            if r == 1:
                qkv4, cols = proj3[:, None], (COL_QKV0, COL_QKV0 + 1, COL_QKV0 + 2)
            else:
                w_g = jnp.concatenate(group_cols(w, g), axis=1)
                qkv4, cols = _inproj_perm(x3, nm, w_g, r), (0, 1, 2)
            o, l = _attn_b(qkv4, bias_b[g], g, cols)
            obs.append(o)
            lses.append(l.reshape(T, LANES))

        b_exp = jnp.repeat(sgu_b[i].T, MIX_W // C_GROUPS, axis=1)
        yc = _sgu(proj2, sgu_ln_g[i][None, :], sgu_ln_b[i][None, :], sgu_w[i].astype(BF16), b_exp)

        wr = jnp.concatenate([w_router_exp[i].transpose(1, 0, 2).reshape(D, N_EXPERTS),
                              w_router_grp[i]], axis=1)
        wr = jnp.pad(wr, ((0, 0), (0, LANES - wr.shape[1]))).astype(BF16)
        br = jnp.concatenate([b_router_exp[i].reshape(N_EXPERTS), b_router_grp[i]])
        br = jnp.pad(br, (0, LANES - br.shape[0]))[None, :].astype(F32)

        x2, h, comb = _mix(x2, ya.reshape(T, a_out), obs, lses, yc, proj2,
                           w_branch[i].astype(BF16), w_out[i].astype(BF16), norm_ffn[i][None, :],
                           wr, br)
        x2 = _moe(h, comb, x2, wg_all, wu_all, wd_all, i * N_EXPERTS, norm_final[None, :],
                  i == depth - 1)
    return x2.reshape(B, S, D)
```

```python
import functools
import math

import jax
import jax.numpy as jnp
from jax import lax
from jax.experimental import pallas as pl
from jax.experimental.pallas import tpu as pltpu

F32 = jnp.float32
BF16 = jnp.bfloat16

EPS = 1e-6
NEG = -1e30
LOG2E = 1.4426950408889634
LN2 = 0.6931471805599453
LANES = 128
HALF_LANES = LANES // 2
VMEM_LIMIT = 48 * 1024 * 1024

HA = 4
DA = 64
MIX_W = 512
WINDOWS = (128, 512, 2048)
DILATIONS = (1, 4, 16)
NG_B = 3
HB = 8
DB = 64
HALF_WIN = 64
CHUNK = 128
C_GROUPS = 4
N_BRANCH = 3
N_BUCKETS = 32
MAX_DIST = 128
N_GROUPS = 4
E_PER_GROUP = 4
N_EXPERTS = N_GROUPS * E_PER_GROUP
N_SLABS = MIX_W // LANES

TM_PROJ = 1024
TN_PROJ = 3328
TM_PERM = 1024
PERM_BLK = 256
T_ATT = 512
QB_DIL = 128
KW_DIL = QB_DIL + 2 * HALF_WIN
ITEMS_DIL = 8
SUBS_DIL = 4
OUT_ROWS_DIL = 2048
TM_SGU = 2048
TM_MIX = 512
TM_MOE = 1024
TM_DISP = 256
MOE_CAP = HALF_LANES
G_FFN = 16
TILES_PER_STEP = 4
FFN_ROW_STEP = 64
BF16_ROWS = 16
MAX_OVF = 64

COL_ZU = 2
COL_GATE = 2
COL_QKV0 = 10


def _cparams(sem):
    return pltpu.CompilerParams(dimension_semantics=sem, vmem_limit_bytes=VMEM_LIMIT)


def _t5_bucket(rel):
    nb = N_BUCKETS // 2
    max_exact = nb // 2
    ret = (rel > 0).astype(jnp.int32) * nb
    n = jnp.abs(rel)
    nf = jnp.maximum(n, 1).astype(F32)
    large = max_exact + (jnp.log(nf / max_exact) / math.log(MAX_DIST / max_exact)
                         * (nb - max_exact)).astype(jnp.int32)
    large = jnp.minimum(large, nb - 1)
    return ret + jnp.where(n < max_exact, n, large)


def _bias_lookup(bucket, tab):
    out = jnp.zeros((tab.shape[1],) + bucket.shape, F32)
    expand = (slice(None),) + (None,) * bucket.ndim
    for b in range(N_BUCKETS):
        out = jnp.where(bucket[None] == b, tab[b][expand], out)
    return out


def _rms_bf16(x, g):
    ms = jnp.mean(x * x, axis=-1, keepdims=True)
    return (x * lax.rsqrt(ms + EPS) * g).astype(BF16)


def _inproj_kernel(x_ref, g_ref, w_ref, wt_ref, o_ref, vt_ref, h_scr):
    @pl.when(pl.program_id(2) == 0)
    def _():
        h = _rms_bf16(x_ref[0], g_ref[...])
        h_scr[...] = h
        res = lax.dot_general(wt_ref[...], h, (((1,), (1,)), ((), ())),
                              preferred_element_type=F32).astype(vt_ref.dtype)
        for hd in range(vt_ref.shape[1]):
            for n in range(vt_ref.shape[2]):
                vt_ref[0, hd, n] = res[hd * LANES:(hd + 1) * LANES, n * T_ATT:(n + 1) * T_ATT]

    o_ref[0] = jnp.dot(h_scr[...], w_ref[...], preferred_element_type=F32).astype(o_ref.dtype)


def _inproj(x3, g, w, wt):
    B, S, D = x3.shape
    N = w.shape[1]
    tm = min(TM_PROJ, S)
    nh, nb = wt.shape[0] // LANES, tm // T_ATT
    return pl.pallas_call(
        _inproj_kernel,
        out_shape=[jax.ShapeDtypeStruct((B, S, N), BF16),
                   jax.ShapeDtypeStruct((B, nh, S // T_ATT, LANES, T_ATT), BF16)],
        grid=(B, S // tm, N // TN_PROJ),
        in_specs=[pl.BlockSpec((1, tm, D), lambda b, i, j: (b, i, 0)),
                  pl.BlockSpec((1, D), lambda b, i, j: (0, 0)),
                  pl.BlockSpec((D, TN_PROJ), lambda b, i, j: (0, j)),
                  pl.BlockSpec(wt.shape, lambda b, i, j: (0, 0))],
        out_specs=[pl.BlockSpec((1, tm, TN_PROJ), lambda b, i, j: (b, i, j)),
                   pl.BlockSpec((1, nh, nb, LANES, T_ATT), lambda b, i, j: (b, 0, i, 0, 0))],
        scratch_shapes=[pltpu.VMEM((tm, D), BF16)],
        compiler_params=_cparams(("parallel", "parallel", "arbitrary")),
        name="inproj",
    )(x3, g, w, wt)


def _inproj_perm_kernel(x_ref, g_ref, p_ref, w_ref, o_ref, *, r):
    h = _rms_bf16(x_ref[0], g_ref[...])
    nblk = h.shape[0] // PERM_BLK
    hp = jnp.concatenate(
        [jnp.dot(p_ref[...], h[k * PERM_BLK:(k + 1) * PERM_BLK], preferred_element_type=F32)
         for k in range(nblk)], axis=0).astype(BF16)
    res = jnp.dot(hp, w_ref[...], preferred_element_type=F32).astype(o_ref.dtype)
    n = PERM_BLK // r
    for k in range(nblk):
        for s in range(r):
            o_ref[0, s, k * n:(k + 1) * n, :] = res[k * PERM_BLK + s * n:k * PERM_BLK + (s + 1) * n, :]


def _inproj_perm(x3, g, w, r):
    B, S, D = x3.shape
    N = w.shape[1]
    tm = min(TM_PERM, S)
    n = PERM_BLK // r
    o = jnp.arange(PERM_BLK, dtype=jnp.int32)
    src = (o % n) * r + o // n
    perm = (src[:, None] == jnp.arange(PERM_BLK, dtype=jnp.int32)[None, :]).astype(BF16)
    kern = functools.partial(_inproj_perm_kernel, r=r)
    return pl.pallas_call(
        kern,
        out_shape=jax.ShapeDtypeStruct((B, r, S // r, N), BF16),
        grid=(B, S // tm),
        in_specs=[pl.BlockSpec((1, tm, D), lambda b, i: (b, i, 0)),
                  pl.BlockSpec((1, D), lambda b, i: (0, 0)),
                  pl.BlockSpec((PERM_BLK, PERM_BLK), lambda b, i: (0, 0)),
                  pl.BlockSpec((D, N), lambda b, i: (0, 0))],
        out_specs=pl.BlockSpec((1, r, tm // r, N), lambda b, i: (b, 0, i, 0)),
        compiler_params=_cparams(("parallel", "parallel")),
        name=f"inproj_perm_{r}",
    )(x3, g, perm, w)


def _attn_a_kernel(lam_ref, cfar_ref, q_ref, k_ref, vt_ref, bias_ref, g_ref, o_ref,
                   st0_scr, st1_scr, m0_scr, m1_scr, acc_scr, l_scr, *, out_scale, nq, n_blocks):
    k = pl.program_id(0)
    t = T_ATT
    nk = k_ref.shape[1] // t
    n1 = jnp.minimum(k // 2, n_blocks - 1)
    n2 = jnp.maximum(k - 1, 0) // 2
    h1, qi1 = (n1 // nq) % HA, n1 % nq
    h2, qi2 = (n2 // nq) % HA, n2 % nq

    @pl.when(k == 0)
    def _():
        st1_scr[...] = jnp.zeros(st1_scr.shape, F32)
        m1_scr[...] = jnp.zeros(m1_scr.shape, F32)
        acc_scr[...] = jnp.zeros(acc_scr.shape, F32)
        l_scr[...] = jnp.ones(l_scr.shape, F32)

    low_half = lax.broadcasted_iota(jnp.int32, (1, LANES), 1) < HALF_LANES

    def both(cmap, st_w, m_w, st_r, m_r):
        q = q_ref[0]
        zero = jnp.zeros_like(q)
        qc = jnp.where(low_half, q, zero) if cmap == 0 else jnp.where(low_half, zero, q)
        m_prev = m_r[...]
        l = jnp.zeros((1, t), F32)
        acc = jnp.zeros((LANES, t), F32)
        m_new = None
        for j, d in enumerate(range(-1, nk - 1)):
            a1 = lax.rem(qi1 + (d + nk), nk)
            delta1 = a1 - qi1
            kb = k_ref[0, pl.ds(pl.multiple_of(a1 * t, t), t), :]
            st = lax.dot_general(kb, qc, (((1,), (1,)), ((), ())), preferred_element_type=F32)
            if d <= 1:
                st = st + bias_ref[0, jnp.clip(delta1, -2, 2) + 2]
                cm = jnp.max(st, axis=0, keepdims=True)
            else:
                cm = (jnp.max(st, axis=0, keepdims=True)
                      + cfar_ref[2 * h1 + (delta1 > 0).astype(jnp.int32)])
            st_w[j] = st
            m_new = cm if m_new is None else jnp.maximum(m_new, cm)

            a2 = lax.rem(qi2 + (d + nk), nk)
            if d <= 1:
                shifted = m_prev
            else:
                shifted = m_prev - cfar_ref[2 * h2 + (a2 > qi2).astype(jnp.int32)]
            p = jnp.exp2(st_r[j] - shifted)
            l = l + jnp.sum(p, axis=0, keepdims=True)
            acc = acc + jnp.dot(vt_ref[0, 0, a2], p.astype(BF16), preferred_element_type=F32)
        m_w[...] = m_new
        return l, acc

    @pl.when(k % 2 == 0)
    def _():
        l1, acc1 = both(0, st0_scr, m0_scr, st1_scr, m1_scr)
        ot = acc_scr[...] / l_scr[...] - lam_ref[0] * (acc1 / l1)
        o = ot.T
        ms = jnp.mean(o * o, axis=-1, keepdims=True)
        o_ref[0] = (o * lax.rsqrt(ms + EPS) * g_ref[...] * out_scale).astype(o_ref.dtype)

    @pl.when(k % 2 == 1)
    def _():
        l0, acc0 = both(1, st1_scr, m1_scr, st0_scr, m0_scr)
        acc_scr[...] = acc0
        l_scr[...] = l0


def _attn_a(proj3, vt, lam, cfar, bias5, subln_g, lam_init):
    B, S, _ = proj3.shape
    t = T_ATT
    nk = S // t
    n_blocks = B * HA * nk

    def scored(k):
        n = jnp.minimum(k // 2, n_blocks - 1)
        return n // (HA * nk), (n // nk) % HA, n % nk

    def lagged(k, lag):
        n = jnp.maximum(k - lag, 0) // 2
        return n // (HA * nk), (n // nk) % HA, n % nk

    def q_map(k):
        b, h, qi = scored(k)
        return b, qi, h

    def k_map(k):
        b, h, _ = scored(k)
        return b, 0, HA + h

    def vt_map(k):
        b, h, _ = lagged(k, 1)
        return b, h, 0, 0, 0

    def out_map(k):
        b, h, qi = lagged(k, 2)
        return b, qi, h

    kern = functools.partial(_attn_a_kernel, out_scale=1.0 - lam_init, nq=nk, n_blocks=n_blocks)
    return pl.pallas_call(
        kern,
        out_shape=jax.ShapeDtypeStruct((B, S, HA * 2 * DA), BF16),
        grid=(2 * n_blocks + 1,),
        in_specs=[
            pl.BlockSpec(memory_space=pltpu.SMEM),
            pl.BlockSpec(memory_space=pltpu.SMEM),
            pl.BlockSpec((1, t, LANES), q_map),
            pl.BlockSpec((1, S, LANES), k_map),
            pl.BlockSpec((1, 1, nk, LANES, t), vt_map),
            pl.BlockSpec((1, 5, t, t), lambda k: (scored(k)[1], 0, 0, 0)),
            pl.BlockSpec((1, LANES), lambda k: (0, 0)),
        ],
        out_specs=pl.BlockSpec((1, t, LANES), out_map),
        scratch_shapes=[pltpu.VMEM((nk, t, t), F32), pltpu.VMEM((nk, t, t), F32),
                        pltpu.VMEM((1, t), F32), pltpu.VMEM((1, t), F32),
                        pltpu.VMEM((LANES, t), F32), pltpu.VMEM((1, t), F32)],
        compiler_params=_cparams(("arbitrary",)),
        name="diff_attn",
    )(lam, cfar, proj3, proj3, vt, bias5, subln_g)


def _attn_a_bias(rel_bias):
    t = T_ATT
    tab = rel_bias[:, :HA].astype(F32) * LOG2E
    d = jnp.arange(-1, 2, dtype=jnp.int32)[:, None, None] * t
    rel = d + jnp.arange(t, dtype=jnp.int32)[None, :, None] - jnp.arange(t, dtype=jnp.int32)[None, None, :]
    near = _bias_lookup(_t5_bucket(rel), tab)
    far = tab[_t5_bucket(jnp.array([-(t + 1), t + 1], dtype=jnp.int32))].T
    fill = lambda side: jnp.broadcast_to(far[:, side, None, None, None], (HA, 1, t, t))
    tiles = jnp.concatenate([fill(0), near, fill(1)], axis=1)
    return tiles, far.reshape(2 * HA)


def _attn_b_kernel(q_ref, k_ref, v_ref, bias_ref, o_ref, lse_ref, *, sub_len, r, sp, qp):
    nblk = sub_len // QB_DIL
    low_half = lax.broadcasted_iota(jnp.int32, (1, LANES), 1) < HALF_LANES
    for si in range(sp):
        s = si if sp == r else pl.program_id(2) * sp + si
        for qb in range(qp):
            i = pl.program_id(1) * qp + qb
            start = jnp.clip(i * QB_DIL - HALF_WIN, 0, sub_len - KW_DIL)
            start = pl.multiple_of(start, HALF_WIN)
            variant = jnp.where(i == 0, 0, jnp.where(i == nblk - 1, 2, 1))
            q = q_ref[0, si, qb * QB_DIL:(qb + 1) * QB_DIL, :]
            kw = k_ref[0, s, pl.ds(start, KW_DIL), :]
            vw = v_ref[0, s, pl.ds(start, KW_DIL), :]
            rows = (slice(qb * QB_DIL, (qb + 1) * QB_DIL) if r == 1
                    else pl.ds(qb * QB_DIL * r + s, QB_DIL, stride=r))
            lane = lax.broadcasted_iota(jnp.int32, (QB_DIL, LANES), 1)
            lse_tile = jnp.zeros((QB_DIL, LANES), F32)
            for j in range(HB // 2):
                cols = slice(j * LANES, (j + 1) * LANES)
                qpair, kp, vp = q[:, cols], kw[:, cols], vw[:, cols]
                outs, lses = [], []
                for c in range(2):
                    qc = jnp.where(low_half if c == 0 else jnp.logical_not(low_half), qpair,
                                   jnp.zeros_like(qpair))
                    sc = lax.dot_general(qc, kp, (((1,), (1,)), ((), ())),
                                         preferred_element_type=F32)
                    sc = sc + bias_ref[2 * j + c, variant]
                    m = jnp.max(sc, axis=-1, keepdims=True)
                    p = jnp.exp2(sc - m)
                    l = jnp.sum(p, axis=-1, keepdims=True)
                    outs.append(jnp.dot(p.astype(BF16), vp, preferred_element_type=F32) / l)
                    lses.append((m + jnp.log2(l)) * LN2)
                o_ref[0, j, rows, :] = jnp.where(low_half, outs[0], outs[1])
                for c in range(2):
                    lse_tile = jnp.where(lane == 2 * j + c, lses[c], lse_tile)
            lse_ref[0, rows, :] = lse_tile


def _attn_b(qkv4, bias3, g, cols):
    B, r, L, _ = qkv4.shape
    S = r * L
    width = HB * DB
    nblk = L // QB_DIL
    sp = min(r, SUBS_DIL)
    qp = max(1, min(ITEMS_DIL // sp, OUT_ROWS_DIL // (QB_DIL * r)))
    qcol, kcol, vcol = cols
    kern = functools.partial(_attn_b_kernel, sub_len=L, r=r, sp=sp, qp=qp)
    slab = jax.ShapeDtypeStruct((B, N_SLABS, S, LANES), F32)
    slab_spec = pl.BlockSpec((1, N_SLABS, QB_DIL * r * qp, LANES), lambda b, i, s: (b, 0, i, 0))
    return pl.pallas_call(
        kern,
        out_shape=[slab, jax.ShapeDtypeStruct((B, S, LANES), F32)],
        grid=(B, nblk // qp, r // sp),
        in_specs=[
            pl.BlockSpec((1, sp, QB_DIL * qp, width), lambda b, i, s: (b, s, i, qcol)),
            pl.BlockSpec((1, r, L, width), lambda b, i, s: (b, 0, 0, kcol)),
            pl.BlockSpec((1, r, L, width), lambda b, i, s: (b, 0, 0, vcol)),
            pl.BlockSpec((HB, 3, QB_DIL, KW_DIL), lambda b, i, s: (0, 0, 0, 0)),
        ],
        out_specs=[slab_spec, pl.BlockSpec((1, QB_DIL * r * qp, LANES), lambda b, i, s: (b, i, 0))],
        compiler_params=_cparams(("parallel", "arbitrary", "arbitrary")),
        name=f"dilated_attn_{g}",
    )(qkv4, qkv4, qkv4, bias3)


def _attn_b_bias(rel_bias, g):
    r = DILATIONS[g]
    tab = rel_bias[:, HA + g * HB: HA + (g + 1) * HB].astype(F32) * LOG2E
    off = jnp.arange(3, dtype=jnp.int32)[:, None, None] * HALF_WIN
    rel = (jnp.arange(KW_DIL, dtype=jnp.int32)[None, None, :] - off
           - jnp.arange(QB_DIL, dtype=jnp.int32)[None, :, None])
    bias = _bias_lookup(_t5_bucket(rel * r), tab)
    return jnp.where((jnp.abs(rel) <= HALF_WIN)[None], bias, NEG)


def _sgu_kernel(zu_ref, zv_ref, lng_ref, lnb_ref, ws_ref, bs_ref, o_ref):
    u = jax.nn.gelu(zu_ref[...].astype(F32))
    v = jax.nn.gelu(zv_ref[...].astype(F32))
    mu = jnp.mean(v, axis=-1, keepdims=True)
    var = jnp.mean(jnp.square(v - mu), axis=-1, keepdims=True)
    v = ((v - mu) * lax.rsqrt(var + EPS) * lng_ref[...] + lnb_ref[...]).astype(BF16)
    gd = v.shape[1] // C_GROUPS
    for n in range(v.shape[0] // CHUNK):
        rows = slice(n * CHUNK, (n + 1) * CHUNK)
        for g in range(C_GROUPS):
            cols = slice(g * gd, (g + 1) * gd)
            mixed = jnp.dot(ws_ref[g], v[rows, cols], preferred_element_type=F32) + bs_ref[:, cols]
            o_ref[rows, cols] = (u[rows, cols] * mixed).astype(o_ref.dtype)


def _sgu(proj2, ln_g, ln_b, w_s, b_exp):
    T = proj2.shape[0]
    tm = min(TM_SGU, T)
    w = MIX_W
    return pl.pallas_call(
        _sgu_kernel,
        out_shape=jax.ShapeDtypeStruct((T, w), BF16),
        grid=(T // tm,),
        in_specs=[pl.BlockSpec((tm, w), lambda i: (i, COL_ZU)),
                  pl.BlockSpec((tm, w), lambda i: (i, COL_ZU + 1)),
                  pl.BlockSpec((1, w), lambda i: (0, 0)),
                  pl.BlockSpec((1, w), lambda i: (0, 0)),
                  pl.BlockSpec((C_GROUPS, CHUNK, CHUNK), lambda i: (0, 0, 0)),
                  pl.BlockSpec((CHUNK, w), lambda i: (0, 0))],
        out_specs=pl.BlockSpec((tm, w), lambda i: (i, 0)),
        compiler_params=_cparams(("parallel",)),
        name="sgu",
    )(proj2, proj2, ln_g, ln_b, w_s, b_exp)


def _route(logits):
    lane = lax.broadcasted_iota(jnp.int32, logits.shape, 1)
    big = jnp.int32(LANES)
    is_grp = (lane >= N_EXPERTS) & (lane < N_EXPERTS + N_GROUPS)
    gl = jnp.where(is_grp, logits, NEG)
    gmax = jnp.max(gl, axis=-1, keepdims=True)
    g_idx = jnp.min(jnp.where(is_grp & (gl == gmax), lane, big), axis=-1, keepdims=True) - N_EXPERTS
    g_w = 1.0 / jnp.sum(jnp.where(is_grp, jnp.exp(gl - gmax), 0.0), axis=-1, keepdims=True)
    in_grp = (lane >= g_idx * E_PER_GROUP) & (lane < (g_idx + 1) * E_PER_GROUP)
    sel = jnp.where(in_grp, logits, NEG)
    v1 = jnp.max(sel, axis=-1, keepdims=True)
    i1 = jnp.min(jnp.where(in_grp & (sel == v1), lane, big), axis=-1, keepdims=True)
    rest = in_grp & (lane != i1)
    sel2 = jnp.where(rest, logits, NEG)
    v2 = jnp.max(sel2, axis=-1, keepdims=True)
    i2 = jnp.min(jnp.where(rest & (sel2 == v2), lane, big), axis=-1, keepdims=True)
    e2 = jnp.exp(v2 - v1)
    w1 = g_w / (1.0 + e2)
    w2 = g_w * e2 / (1.0 + e2)
    return jnp.where(lane == i1, w1, jnp.where(lane == i2, w2, 0.0))


def _mix_kernel(x_ref, ya_ref, ob0_ref, ob1_ref, ob2_ref, ls0_ref, ls1_ref, ls2_ref, yc_ref,
                g0_ref, g1_ref, g2_ref, wb_ref, wo_ref, nf_ref, wr_ref, br_ref,
                xo_ref, h_ref, comb_ref):
    ls0, ls1, ls2 = ls0_ref[...], ls1_ref[...], ls2_ref[...]
    mx = jnp.maximum(jnp.maximum(ls0, ls1), ls2)
    es = [jnp.exp(ls0 - mx), jnp.exp(ls1 - mx), jnp.exp(ls2 - mx)]
    inv = 1.0 / (es[0] + es[1] + es[2])
    spread = jnp.where(lax.broadcasted_iota(jnp.int32, (2 * LANES, MIX_W), 1) // DB
                       == lax.broadcasted_iota(jnp.int32, (2 * LANES, MIX_W), 0) % LANES,
                       1.0, 0.0).astype(BF16)
    yb = None
    for e, ob_ref in zip(es, (ob0_ref, ob1_ref, ob2_ref)):
        w = e * inv
        hi = w.astype(BF16)
        lo = (w - hi.astype(F32)).astype(BF16)
        wide = jnp.dot(jnp.concatenate([hi, lo], axis=1), spread, preferred_element_type=F32)
        term = wide * jnp.concatenate([ob_ref[0, j] for j in range(N_SLABS)], axis=-1)
        yb = term if yb is None else yb + term
    yb = yb.astype(BF16)
    merged = jax.nn.sigmoid(g0_ref[...].astype(F32)) * jnp.dot(ya_ref[...], wb_ref[0],
                                                               preferred_element_type=F32)
    merged += jax.nn.sigmoid(g1_ref[...].astype(F32)) * jnp.dot(yb, wb_ref[1],
                                                                preferred_element_type=F32)
    merged += jax.nn.sigmoid(g2_ref[...].astype(F32)) * jnp.dot(yc_ref[...], wb_ref[2],
                                                                preferred_element_type=F32)
    xn = x_ref[...] + jnp.dot(merged.astype(BF16), wo_ref[...], preferred_element_type=F32)
    xo_ref[...] = xn
    h = _rms_bf16(xn, nf_ref[...])
    h_ref[...] = h
    logits = jnp.dot(h, wr_ref[...], preferred_element_type=F32) + br_ref[...]
    comb_ref[...] = _route(logits)


def _mix(x2, ya, obs, lses, yc, proj2, wb, wo, nf, wr, br):
    T, D = x2.shape
    S = obs[0].shape[2]
    tm = min(TM_MIX, S)
    per_b = S // tm
    w = MIX_W
    row = lambda width: pl.BlockSpec((tm, width), lambda i: (i, 0))
    full = lambda a: pl.BlockSpec(a.shape, lambda i: (0,) * a.ndim)
    gate = lambda n: pl.BlockSpec((tm, D), lambda i: (i, COL_GATE + n))
    slab = pl.BlockSpec((1, N_SLABS, tm, LANES), lambda i: (i // per_b, 0, i % per_b, 0))
    return pl.pallas_call(
        _mix_kernel,
        out_shape=[jax.ShapeDtypeStruct((T, D), F32), jax.ShapeDtypeStruct((T, D), BF16),
                   jax.ShapeDtypeStruct((T, LANES), F32)],
        grid=(T // tm,),
        in_specs=[row(D), row(w), slab, slab, slab, row(LANES), row(LANES), row(LANES), row(w),
                  gate(0), gate(1), gate(2), full(wb), full(wo), full(nf), full(wr), full(br)],
        out_specs=[row(D), row(D), row(LANES)],
        compiler_params=_cparams(("parallel",)),
        name="mix",
    )(x2, ya, obs[0], obs[1], obs[2], lses[0], lses[1], lses[2], yc, proj2, proj2, proj2,
      wb, wo, nf, wr, br)


def _moe_kernel(h_ref, comb_ref, x_ref, wg_ref, wu_ref, wd_ref, nfin_ref, o_ref, acc_scr,
                *, final_norm):
    e = pl.program_id(1)

    @pl.when(e == 0)
    def _():
        acc_scr[...] = jnp.zeros(acc_scr.shape, F32)

    h = h_ref[...]
    lane = lax.broadcasted_iota(jnp.int32, comb_ref.shape, 1)
    c = jnp.sum(jnp.where(lane == e, comb_ref[...], 0.0), axis=-1, keepdims=True)
    hid = (jax.nn.silu(jnp.dot(h, wg_ref[0].astype(BF16), preferred_element_type=F32))
           * jnp.dot(h, wu_ref[0].astype(BF16), preferred_element_type=F32))
    acc_scr[...] += c * jnp.dot(hid.astype(BF16), wd_ref[0].astype(BF16),
                                preferred_element_type=F32)

    @pl.when(e == pl.num_programs(1) - 1)
    def _():
        xn = x_ref[...] + acc_scr[...]
        if final_norm:
            ms = jnp.mean(xn * xn, axis=-1, keepdims=True)
            xn = xn * lax.rsqrt(ms + EPS) * nfin_ref[...]
        o_ref[...] = xn


def _moe_dense(h, comb, x2, wg, wu, wd, e0, nfin, final_norm):
    T, D = x2.shape
    tm = min(TM_MOE, T)
    F = wg.shape[2]
    kern = functools.partial(_moe_kernel, final_norm=final_norm)
    return pl.pallas_call(
        kern,
        out_shape=jax.ShapeDtypeStruct((T, D), F32),
        grid=(T // tm, N_EXPERTS),
        in_specs=[pl.BlockSpec((tm, D), lambda i, e: (i, 0)),
                  pl.BlockSpec((tm, LANES), lambda i, e: (i, 0)),
                  pl.BlockSpec((tm, D), lambda i, e: (i, 0)),
                  pl.BlockSpec((1, D, F), lambda i, e: (e0 + e, 0, 0)),
                  pl.BlockSpec((1, D, F), lambda i, e: (e0 + e, 0, 0)),
                  pl.BlockSpec((1, F, D), lambda i, e: (e0 + e, 0, 0)),
                  pl.BlockSpec((1, D), lambda i, e: (0, 0))],
        out_specs=pl.BlockSpec((tm, D), lambda i, e: (i, 0)),
        scratch_shapes=[pltpu.VMEM((tm, D), F32)],
        compiler_params=_cparams(("parallel", "arbitrary")),
        name="moe_dense",
    )(h, comb, x2, wg, wu, wd, nfin)


def _moe_dispatch_kernel(h_ref, comb_ref, o_ref, cnt_ref):
    nt = o_ref.shape[0]
    tm = h_ref.shape[0] // nt
    before = jnp.where(lax.broadcasted_iota(jnp.int32, (tm, tm), 0)
                       < lax.broadcasted_iota(jnp.int32, (tm, tm), 1), 1.0, 0.0).astype(BF16)
    slot = lax.broadcasted_iota(jnp.int32, (MOE_CAP, tm), 0).astype(F32)
    for u in range(nt):
        rows = slice(u * tm, (u + 1) * tm)
        comb = comb_ref[rows, :]
        hi = comb.astype(BF16)
        lo = (comb - hi.astype(F32)).astype(BF16)
        haug = jnp.concatenate([h_ref[rows, :], hi, lo], axis=1)
        a_t = comb.T[:N_EXPERTS] > 0.0
        a_f = jnp.where(a_t, 1.0, 0.0)
        rank_t = jnp.dot(a_f.astype(BF16), before, preferred_element_type=F32)
        blocks = [jnp.where((slot == rank_t[e:e + 1]) & a_t[e:e + 1], 1.0, 0.0).astype(BF16)
                  for e in range(N_EXPERTS)]
        res = jnp.dot(jnp.concatenate(blocks, axis=0), haug, preferred_element_type=F32)
        res = res.astype(o_ref.dtype)
        for e in range(N_EXPERTS):
            o_ref[u, e] = res[e * MOE_CAP:(e + 1) * MOE_CAP]
        cnt_ref[u] = jnp.broadcast_to(jnp.sum(a_f, axis=1, keepdims=True), cnt_ref.shape[1:])


def _moe_dispatch(h, comb):
    T, D = h.shape
    tm = min(TM_DISP, T)
    n = T // tm
    nt = math.gcd(TILES_PER_STEP, n)
    return pl.pallas_call(
        _moe_dispatch_kernel,
        out_shape=[jax.ShapeDtypeStruct((n, N_EXPERTS, MOE_CAP, D + 2 * LANES), BF16),
                   jax.ShapeDtypeStruct((n, N_EXPERTS, LANES), F32)],
        grid=(n // nt,),
        in_specs=[pl.BlockSpec((nt * tm, D), lambda i: (i, 0)),
                  pl.BlockSpec((nt * tm, LANES), lambda i: (i, 0))],
        out_specs=[pl.BlockSpec((nt, N_EXPERTS, MOE_CAP, D + 2 * LANES), lambda i: (i, 0, 0, 0)),
                   pl.BlockSpec((nt, N_EXPERTS, LANES), lambda i: (i, 0, 0))],
        compiler_params=_cparams(("parallel",)),
        name="moe_dispatch",
    )(h, comb)


def _moe_ffn_kernel(n16_ref, s_ref, wg_ref, wu_ref, wd_ref, o_ref,
                    wg_scr, wu_scr, wd_scr, lhs_scr, y_scr):
    e, c = pl.program_id(0), pl.program_id(1)

    @pl.when(c == 0)
    def _():
        wg_scr[...] = wg_ref[0].astype(BF16)
        wu_scr[...] = wu_ref[0].astype(BF16)
        wd_scr[...] = wd_ref[0].astype(BF16)

    g, _, cap, _ = s_ref.shape
    D = o_ref.shape[-1]
    @pl.when((e == 0) & (c == 0))
    def _():
        lhs_scr[...] = jnp.zeros(lhs_scr.shape, lhs_scr.dtype)
        y_scr[...] = jnp.zeros(y_scr.shape, y_scr.dtype)

    offs = []
    off = jnp.int32(0)
    for t in range(g):
        offs.append(off)
        lhs_scr[pl.ds(pl.multiple_of(off, BF16_ROWS), cap), :] = s_ref[t, 0]
        off = off + n16_ref[(c * g + t) * N_EXPERTS + e]
    total = off

    def run(nrows):
        rows = lhs_scr[:nrows]
        h = rows[:, :D]
        wparts = rows[:, D:].astype(F32)
        lane = lax.broadcasted_iota(jnp.int32, wparts.shape, 1)
        w = jnp.sum(jnp.where(lane % LANES == e, wparts, 0.0), axis=-1, keepdims=True)
        hid = (jax.nn.silu(jnp.dot(h, wg_scr[...], preferred_element_type=F32))
               * jnp.dot(h, wu_scr[...], preferred_element_type=F32))
        y = w * jnp.dot(hid.astype(BF16), wd_scr[...], preferred_element_type=F32)
        y_scr[:nrows] = y.astype(y_scr.dtype)

    classes = tuple(range(g * cap // 2, g * cap + 1, FFN_ROW_STEP))
    lower = 0
    for nrows in classes:
        pl.when((total > lower) & (total <= nrows))(functools.partial(run, nrows))
        lower = nrows

    for t in range(g):
        o_ref[t, 0] = y_scr[pl.ds(pl.multiple_of(offs[t], BF16_ROWS), cap), :]


def _moe_ffn(srt, n16, wg, wu, wd, e0):
    n, ne, cap, wdt = srt.shape
    D, F = wg.shape[1], wg.shape[2]
    g = math.gcd(G_FFN, n)
    return pl.pallas_call(
        _moe_ffn_kernel,
        out_shape=jax.ShapeDtypeStruct((n, ne, cap, D), BF16),
        grid_spec=pltpu.PrefetchScalarGridSpec(
            num_scalar_prefetch=1,
            grid=(ne, n // g),
            in_specs=[pl.BlockSpec((g, 1, cap, wdt), lambda e, c, n16: (c, e, 0, 0)),
                      pl.BlockSpec((1, D, F), lambda e, c, n16: (e0 + e, 0, 0)),
                      pl.BlockSpec((1, D, F), lambda e, c, n16: (e0 + e, 0, 0)),
                      pl.BlockSpec((1, F, D), lambda e, c, n16: (e0 + e, 0, 0))],
            out_specs=pl.BlockSpec((g, 1, cap, D), lambda e, c, n16: (c, e, 0, 0)),
            scratch_shapes=[pltpu.VMEM((D, F), BF16), pltpu.VMEM((D, F), BF16),
                            pltpu.VMEM((F, D), BF16), pltpu.VMEM((g * cap, wdt), BF16),
                            pltpu.VMEM((g * cap, D), BF16)]),
        compiler_params=_cparams(("arbitrary", "arbitrary")),
        name="moe_ffn",
    )(n16, srt, wg, wu, wd)


def _moe_combine_kernel(skip_ref, y_ref, comb_ref, x_ref, nfin_ref, o_ref, *, final_norm):
    nt = y_ref.shape[0]
    tm = x_ref.shape[0] // nt
    ncol = N_EXPERTS * MOE_CAP
    before = jnp.where(lax.broadcasted_iota(jnp.int32, (tm, tm), 1)
                       < lax.broadcasted_iota(jnp.int32, (tm, tm), 0), 1.0, 0.0).astype(BF16)
    spread = jnp.where(lax.broadcasted_iota(jnp.int32, (LANES, ncol), 1) // MOE_CAP
                       == lax.broadcasted_iota(jnp.int32, (LANES, ncol), 0), 1.0, 0.0).astype(BF16)
    slot = (lax.broadcasted_iota(jnp.int32, (tm, ncol), 1) % MOE_CAP).astype(F32)
    for u in range(nt):
        rows = slice(u * tm, (u + 1) * tm)
        a = comb_ref[rows, :] > 0.0
        rank = jnp.dot(before, jnp.where(a, 1.0, 0.0).astype(BF16), preferred_element_type=F32)
        key = jnp.where(a, rank, -1.0).astype(BF16)
        key_all = jnp.dot(key, spread, preferred_element_type=F32)
        pc = jnp.where(slot == key_all, 1.0, 0.0).astype(BF16)
        y = jnp.concatenate([y_ref[u, e] for e in range(N_EXPERTS)], axis=0)
        xn = x_ref[rows, :] + jnp.dot(pc, y, preferred_element_type=F32)
        if final_norm:
            ms = jnp.mean(xn * xn, axis=-1, keepdims=True)
            later = skip_ref[pl.program_id(0) * nt + u] == 1
            xn = jnp.where(later, xn, xn * lax.rsqrt(ms + EPS) * nfin_ref[...])
        o_ref[rows, :] = xn


def _moe_combine(skip, y, comb, x2, nfin, final_norm):
    T, D = x2.shape
    n, ne, cap, _ = y.shape
    tm = T // n
    nt = math.gcd(TILES_PER_STEP, n)
    kern = functools.partial(_moe_combine_kernel, final_norm=final_norm)
    return pl.pallas_call(
        kern,
        out_shape=jax.ShapeDtypeStruct((T, D), F32),
        grid_spec=pltpu.PrefetchScalarGridSpec(
            num_scalar_prefetch=1,
            grid=(n // nt,),
            in_specs=[pl.BlockSpec((nt, ne, cap, D), lambda i, sk: (i, 0, 0, 0)),
                      pl.BlockSpec((nt * tm, LANES), lambda i, sk: (i, 0)),
                      pl.BlockSpec((nt * tm, D), lambda i, sk: (i, 0)),
                      pl.BlockSpec((1, D), lambda i, sk: (0, 0))],
            out_specs=pl.BlockSpec((nt * tm, D), lambda i, sk: (i, 0))),
        compiler_params=_cparams(("parallel",)),
        name="moe_combine",
    )(skip, y, comb, x2, nfin)


def _moe_fix_kernel(tiles_ref, experts_ref, first_ref, last_ref, n_ref, h_ref, comb_ref, prev_ref,
                    wg_ref, wu_ref, wd_ref, nfin_ref, o_ref, *, final_norm):
    del tiles_ref
    s = pl.program_id(0)

    @pl.when(s < n_ref[0])
    def _():
        e = experts_ref[s]
        tm = h_ref.shape[0]
        comb = comb_ref[...]
        a = jnp.where(comb > 0.0, 1.0, 0.0)
        before = (lax.broadcasted_iota(jnp.int32, (tm, tm), 1)
                  < lax.broadcasted_iota(jnp.int32, (tm, tm), 0))
        rank = jnp.dot(jnp.where(before, 1.0, 0.0).astype(BF16), a.astype(BF16),
                       preferred_element_type=F32)
        lane = lax.broadcasted_iota(jnp.int32, comb.shape, 1)
        dropped = (lane == e) & (rank >= MOE_CAP)
        c = jnp.sum(jnp.where(dropped, comb, 0.0), axis=-1, keepdims=True)
        h = h_ref[...]
        hid = (jax.nn.silu(jnp.dot(h, wg_ref[0].astype(BF16), preferred_element_type=F32))
               * jnp.dot(h, wu_ref[0].astype(BF16), preferred_element_type=F32))
        add = c * jnp.dot(hid.astype(BF16), wd_ref[0].astype(BF16), preferred_element_type=F32)
        fresh = first_ref[s] == 1

        @pl.when(fresh)
        def _():
            o_ref[...] = prev_ref[...] + add

        @pl.when(jnp.logical_not(fresh))
        def _():
            o_ref[...] += add

        if final_norm:
            @pl.when(last_ref[s] == 1)
            def _():
                xn = o_ref[...]
                ms = jnp.mean(xn * xn, axis=-1, keepdims=True)
                o_ref[...] = xn * lax.rsqrt(ms + EPS) * nfin_ref[...]


def _moe_fix(tiles, experts, first, last, n, out, h, comb, wg, wu, wd, e0, nfin, final_norm):
    T, D = out.shape
    tm = min(TM_DISP, T)
    F = wg.shape[2]
    tile = lambda width: pl.BlockSpec((tm, width), lambda s, tl, ex, fi, la, n: (tl[s], 0))
    wspec = lambda shape: pl.BlockSpec(shape, lambda s, tl, ex, fi, la, n: (e0 + ex[s], 0, 0))
    kern = functools.partial(_moe_fix_kernel, final_norm=final_norm)
    return pl.pallas_call(
        kern,
        out_shape=jax.ShapeDtypeStruct((T, D), F32),
        grid_spec=pltpu.PrefetchScalarGridSpec(
            num_scalar_prefetch=5,
            grid=(MAX_OVF,),
            in_specs=[tile(D), tile(LANES), tile(D), wspec((1, D, F)), wspec((1, D, F)),
                      wspec((1, F, D)), pl.BlockSpec((1, D), lambda s, tl, ex, fi, la, n: (0, 0))],
            out_specs=tile(D)),
        input_output_aliases={7: 0},
        compiler_params=_cparams(("arbitrary",)),
        name="moe_fix",
    )(tiles, experts, first, last, n, h, comb, out, wg, wu, wd, nfin)


def _moe(h, comb, x2, wg, wu, wd, e0, nfin, final_norm):
    srt, cnt = _moe_dispatch(h, comb)
    over = (cnt[:, :, 0] > MOE_CAP).reshape(-1)
    n_ovf = jnp.sum(over.astype(jnp.int32))
    pairs = jnp.nonzero(over, size=MAX_OVF, fill_value=0)[0].astype(jnp.int32)
    pairs = jnp.where(jnp.arange(MAX_OVF) < n_ovf, pairs, pairs[jnp.clip(n_ovf - 1, 0, MAX_OVF - 1)])
    tiles, experts = pairs // N_EXPERTS, pairs % N_EXPERTS
    change = (tiles[1:] != tiles[:-1]).astype(jnp.int32)
    first = jnp.concatenate([jnp.ones((1,), jnp.int32), change])
    last = jnp.maximum(jnp.concatenate([change, jnp.ones((1,), jnp.int32)]),
                       (jnp.arange(MAX_OVF) == n_ovf - 1).astype(jnp.int32))
    skip = jnp.any(over.reshape(-1, N_EXPERTS), axis=1).astype(jnp.int32)

    used = jnp.minimum(cnt[:, :, 0], MOE_CAP).astype(jnp.int32).reshape(-1)
    n16 = (used + (BF16_ROWS - 1)) // BF16_ROWS * BF16_ROWS

    def routed():
        out = _moe_combine(skip, _moe_ffn(srt, n16, wg, wu, wd, e0), comb, x2, nfin, final_norm)
        return lax.cond(
            n_ovf > 0,
            lambda: _moe_fix(tiles, experts, first, last, n_ovf.reshape(1), out, h, comb,
                             wg, wu, wd, e0, nfin, final_norm),
            lambda: out)

    return lax.cond(n_ovf > MAX_OVF,
                    lambda: _moe_dense(h, comb, x2, wg, wu, wd, e0, nfin, final_norm), routed)


def kernel(x, rel_bias, norm_mix, w_in, diff_lambda, diff_subln, sgu_ln_g, sgu_ln_b, sgu_w, sgu_b,
           w_branch, w_out, norm_ffn, w_router_grp, b_router_grp, w_router_exp, b_router_exp,
           w_gate, w_up, w_down, norm_final):
    B, S, D = x.shape
    T = B * S
    depth = w_in.shape[0]
    a_out = HA * 2 * DA
    grp_w = HB * DB
    b_cols = 3 * NG_B * grp_w
    qkv_b0 = 3 * a_out
    zc0 = qkv_b0 + b_cols
    gate0 = zc0 + 2 * MIX_W
    qk_scale = DA ** -0.5

    bias_a, cfar = _attn_a_bias(rel_bias)
    bias_b = [_attn_b_bias(rel_bias, g) for g in range(NG_B)]

    col = jnp.arange(w_in.shape[2])
    is_q = (col < a_out) | ((col >= qkv_b0) & (col < qkv_b0 + NG_B * grp_w))
    col_scale = jnp.where(is_q, qk_scale * LOG2E, 1.0)

    def group_cols(w, g):
        return [w[:, qkv_b0 + (c * NG_B + g) * grp_w: qkv_b0 + (c * NG_B + g + 1) * grp_w]
                for c in range(3)]

    wg_all = w_gate.reshape((-1,) + w_gate.shape[2:])
    wu_all = w_up.reshape((-1,) + w_up.shape[2:])
    wd_all = w_down.reshape((-1,) + w_down.shape[2:])

    x2 = x.reshape(T, D)
    for i in range(depth):
        w = (w_in[i] * col_scale.astype(F32)).astype(BF16)
        nm = norm_mix[i][None, :]
        w_main = jnp.concatenate([w[:, :2 * a_out], w[:, zc0:]] + group_cols(w, 0), axis=1)
        x3 = x2.reshape(B, S, D)
        proj3, vt = _inproj(x3, nm, w_main, w[:, 2 * a_out:3 * a_out].T)
        proj2 = proj3.reshape(T, proj3.shape[2])

        lam_init = 0.8 - 0.6 * math.exp(-0.3 * i)
        lp = diff_lambda[i].astype(F32)
        lam = jnp.exp(jnp.sum(lp[0] * lp[1])) - jnp.exp(jnp.sum(lp[2] * lp[3])) + lam_init
        ya = _attn_a(proj3, vt, lam.reshape(1), cfar, bias_a, diff_subln[i][None, :], lam_init)

        obs, lses = [], []
        for g in range(NG_B):
            r = DILATIONS[g]
            if r == 1:
                qkv4, cols = proj3[:, None], (COL_QKV0, COL_QKV0 + 1, COL_QKV0 + 2)
            else:
                w_g = jnp.concatenate(group_cols(w, g), axis=1)
                qkv4, cols = _inproj_perm(x3, nm, w_g, r), (0, 1, 2)
            o, l = _attn_b(qkv4, bias_b[g], g, cols)
            obs.append(o)
            lses.append(l.reshape(T, LANES))

        b_exp = jnp.repeat(sgu_b[i].T, MIX_W // C_GROUPS, axis=1)
        yc = _sgu(proj2, sgu_ln_g[i][None, :], sgu_ln_b[i][None, :], sgu_w[i].astype(BF16), b_exp)

        wr = jnp.concatenate([w_router_exp[i].transpose(1, 0, 2).reshape(D, N_EXPERTS),
                              w_router_grp[i]], axis=1)
        wr = jnp.pad(wr, ((0, 0), (0, LANES - wr.shape[1]))).astype(BF16)
        br = jnp.concatenate([b_router_exp[i].reshape(N_EXPERTS), b_router_grp[i]])
        br = jnp.pad(br, (0, LANES - br.shape[0]))[None, :].astype(F32)

        x2, h, comb = _mix(x2, ya.reshape(T, a_out), obs, lses, yc, proj2,
                           w_branch[i].astype(BF16), w_out[i].astype(BF16), norm_ffn[i][None, :],
                           wr, br)
        x2 = _moe(h, comb, x2, wg_all, wu_all, wd_all, i * N_EXPERTS, norm_final[None, :],
                  i == depth - 1)
    return x2.reshape(B, S, D)
```

```python
import functools
import math

import jax
import jax.numpy as jnp
from jax import lax
from jax.experimental import pallas as pl
from jax.experimental.pallas import tpu as pltpu

F32 = jnp.float32
BF16 = jnp.bfloat16

EPS = 1e-6
NEG = -1e30
LOG2E = 1.4426950408889634
LN2 = 0.6931471805599453
LANES = 128
HALF_LANES = LANES // 2
VMEM_LIMIT = 48 * 1024 * 1024

HA = 4
DA = 64
MIX_W = 512
WINDOWS = (128, 512, 2048)
DILATIONS = (1, 4, 16)
NG_B = 3
HB = 8
DB = 64
HALF_WIN = 64
CHUNK = 128
C_GROUPS = 4
N_BRANCH = 3
N_BUCKETS = 32
MAX_DIST = 128
N_GROUPS = 4
E_PER_GROUP = 4
N_EXPERTS = N_GROUPS * E_PER_GROUP
N_SLABS = MIX_W // LANES

TM_PROJ = 1024
TN_PROJ = 3328
TM_PERM = 1024
PERM_BLK = 256
T_ATT = 512
QB_DIL = 128
KW_DIL = QB_DIL + 2 * HALF_WIN
ITEMS_DIL = 8
SUBS_DIL = 4
OUT_ROWS_DIL = 2048
TM_SGU = 2048
TM_MIX = 512
TM_MOE = 1024
TM_DISP = 256
MOE_CAP = HALF_LANES
G_FFN = 16
TILES_PER_STEP = 4
FFN_ROW_STEP = 64
BF16_ROWS = 16
MAX_OVF = 64

COL_ZU = 2
COL_GATE = 2
COL_QKV0 = 10


def _cparams(sem):
    return pltpu.CompilerParams(dimension_semantics=sem, vmem_limit_bytes=VMEM_LIMIT)


def _t5_bucket(rel):
    nb = N_BUCKETS // 2
    max_exact = nb // 2
    ret = (rel > 0).astype(jnp.int32) * nb
    n = jnp.abs(rel)
    nf = jnp.maximum(n, 1).astype(F32)
    large = max_exact + (jnp.log(nf / max_exact) / math.log(MAX_DIST / max_exact)
                         * (nb - max_exact)).astype(jnp.int32)
    large = jnp.minimum(large, nb - 1)
    return ret + jnp.where(n < max_exact, n, large)


def _bias_lookup(bucket, tab):
    out = jnp.zeros((tab.shape[1],) + bucket.shape, F32)
    expand = (slice(None),) + (None,) * bucket.ndim
    for b in range(N_BUCKETS):
        out = jnp.where(bucket[None] == b, tab[b][expand], out)
    return out


def _rms_bf16(x, g):
    ms = jnp.mean(x * x, axis=-1, keepdims=True)
    return (x * lax.rsqrt(ms + EPS) * g).astype(BF16)


def _inproj_kernel(x_ref, g_ref, w_ref, wt_ref, o_ref, vt_ref, h_scr):
    @pl.when(pl.program_id(2) == 0)
    def _():
        h = _rms_bf16(x_ref[0], g_ref[...])
        h_scr[...] = h
        res = lax.dot_general(wt_ref[...], h, (((1,), (1,)), ((), ())),
                              preferred_element_type=F32).astype(vt_ref.dtype)
        for hd in range(vt_ref.shape[1]):
            for n in range(vt_ref.shape[2]):
                vt_ref[0, hd, n] = res[hd * LANES:(hd + 1) * LANES, n * T_ATT:(n + 1) * T_ATT]

    o_ref[0] = jnp.dot(h_scr[...], w_ref[...], preferred_element_type=F32).astype(o_ref.dtype)


def _inproj(x3, g, w, wt):
    B, S, D = x3.shape
    N = w.shape[1]
    tm = min(TM_PROJ, S)
    nh, nb = wt.shape[0] // LANES, tm // T_ATT
    return pl.pallas_call(
        _inproj_kernel,
        out_shape=[jax.ShapeDtypeStruct((B, S, N), BF16),
                   jax.ShapeDtypeStruct((B, nh, S // T_ATT, LANES, T_ATT), BF16)],
        grid=(B, S // tm, N // TN_PROJ),
        in_specs=[pl.BlockSpec((1, tm, D), lambda b, i, j: (b, i, 0)),
                  pl.BlockSpec((1, D), lambda b, i, j: (0, 0)),
                  pl.BlockSpec((D, TN_PROJ), lambda b, i, j: (0, j)),
                  pl.BlockSpec(wt.shape, lambda b, i, j: (0, 0))],
        out_specs=[pl.BlockSpec((1, tm, TN_PROJ), lambda b, i, j: (b, i, j)),
                   pl.BlockSpec((1, nh, nb, LANES, T_ATT), lambda b, i, j: (b, 0, i, 0, 0))],
        scratch_shapes=[pltpu.VMEM((tm, D), BF16)],
        compiler_params=_cparams(("parallel", "parallel", "arbitrary")),
        name="inproj",
    )(x3, g, w, wt)


def _inproj_perm_kernel(x_ref, g_ref, *refs, dilations):
    ng = len(dilations)
    p_refs, w_refs, o_refs = refs[:ng], refs[ng:2 * ng], refs[2 * ng:]
    h = _rms_bf16(x_ref[0], g_ref[...])
    nblk = h.shape[0] // PERM_BLK
    for r, p_ref, w_ref, o_ref in zip(dilations, p_refs, w_refs, o_refs):
        hp = jnp.concatenate(
            [jnp.dot(p_ref[...], h[k * PERM_BLK:(k + 1) * PERM_BLK], preferred_element_type=F32)
             for k in range(nblk)], axis=0).astype(BF16)
        res = jnp.dot(hp, w_ref[...], preferred_element_type=F32).astype(o_ref.dtype)
        n = PERM_BLK // r
        for k in range(nblk):
            for s in range(r):
                o_ref[0, s, k * n:(k + 1) * n, :] = res[k * PERM_BLK + s * n:k * PERM_BLK + (s + 1) * n, :]


def _inproj_perm(x3, g, ws, dilations):
    B, S, D = x3.shape
    tm = min(TM_PERM, S)
    perms = []
    for r in dilations:
        n = PERM_BLK // r
        o = jnp.arange(PERM_BLK, dtype=jnp.int32)
        src = (o % n) * r + o // n
        perms.append((src[:, None] == jnp.arange(PERM_BLK, dtype=jnp.int32)[None, :]).astype(BF16))
    kern = functools.partial(_inproj_perm_kernel, dilations=tuple(dilations))
    return pl.pallas_call(
        kern,
        out_shape=[jax.ShapeDtypeStruct((B, r, S // r, w.shape[1]), BF16)
                   for r, w in zip(dilations, ws)],
        grid=(B, S // tm),
        in_specs=([pl.BlockSpec((1, tm, D), lambda b, i: (b, i, 0)),
                   pl.BlockSpec((1, D), lambda b, i: (0, 0))]
                  + [pl.BlockSpec((PERM_BLK, PERM_BLK), lambda b, i: (0, 0)) for _ in dilations]
                  + [pl.BlockSpec(w.shape, lambda b, i: (0, 0)) for w in ws]),
        out_specs=[pl.BlockSpec((1, r, tm // r, w.shape[1]), lambda b, i: (b, 0, i, 0))
                   for r, w in zip(dilations, ws)],
        compiler_params=_cparams(("parallel", "parallel")),
        name="inproj_perm",
    )(x3, g, *perms, *ws)


def _attn_a_kernel(lam_ref, cfar_ref, q_ref, k_ref, vt_ref, bias_ref, g_ref, o_ref,
                   st0_scr, st1_scr, m0_scr, m1_scr, acc_scr, l_scr, *, out_scale, nq, n_blocks):
    k = pl.program_id(0)
    t = T_ATT
    nk = k_ref.shape[1] // t
    n1 = jnp.minimum(k // 2, n_blocks - 1)
    n2 = jnp.maximum(k - 1, 0) // 2
    h1, qi1 = (n1 // nq) % HA, n1 % nq
    h2, qi2 = (n2 // nq) % HA, n2 % nq

    @pl.when(k == 0)
    def _():
        st1_scr[...] = jnp.zeros(st1_scr.shape, F32)
        m1_scr[...] = jnp.zeros(m1_scr.shape, F32)
        acc_scr[...] = jnp.zeros(acc_scr.shape, F32)
        l_scr[...] = jnp.ones(l_scr.shape, F32)

    low_half = lax.broadcasted_iota(jnp.int32, (1, LANES), 1) < HALF_LANES

    def both(cmap, st_w, m_w, st_r, m_r):
        q = q_ref[0]
        zero = jnp.zeros_like(q)
        qc = jnp.where(low_half, q, zero) if cmap == 0 else jnp.where(low_half, zero, q)
        m_prev = m_r[...]
        l = jnp.zeros((1, t), F32)
        acc = jnp.zeros((LANES, t), F32)
        m_new = None
        for j, d in enumerate(range(-1, nk - 1)):
            a1 = lax.rem(qi1 + (d + nk), nk)
            delta1 = a1 - qi1
            kb = k_ref[0, pl.ds(pl.multiple_of(a1 * t, t), t), :]
            st = lax.dot_general(kb, qc, (((1,), (1,)), ((), ())), preferred_element_type=F32)
            if d <= 1:
                st = st + bias_ref[0, jnp.clip(delta1, -2, 2) + 2]
                cm = jnp.max(st, axis=0, keepdims=True)
            else:
                cm = (jnp.max(st, axis=0, keepdims=True)
                      + cfar_ref[2 * h1 + (delta1 > 0).astype(jnp.int32)])
            st_w[j] = st
            m_new = cm if m_new is None else jnp.maximum(m_new, cm)

            a2 = lax.rem(qi2 + (d + nk), nk)
            if d <= 1:
                shifted = m_prev
            else:
                shifted = m_prev - cfar_ref[2 * h2 + (a2 > qi2).astype(jnp.int32)]
            p = jnp.exp2(st_r[j] - shifted)
            l = l + jnp.sum(p, axis=0, keepdims=True)
            acc = acc + jnp.dot(vt_ref[0, 0, a2], p.astype(BF16), preferred_element_type=F32)
        m_w[...] = m_new
        return l, acc

    @pl.when(k % 2 == 0)
    def _():
        l1, acc1 = both(0, st0_scr, m0_scr, st1_scr, m1_scr)
        ot = acc_scr[...] / l_scr[...] - lam_ref[0] * (acc1 / l1)
        o = ot.T
        ms = jnp.mean(o * o, axis=-1, keepdims=True)
        o_ref[0] = (o * lax.rsqrt(ms + EPS) * g_ref[...] * out_scale).astype(o_ref.dtype)

    @pl.when(k % 2 == 1)
    def _():
        l0, acc0 = both(1, st1_scr, m1_scr, st0_scr, m0_scr)
        acc_scr[...] = acc0
        l_scr[...] = l0


def _attn_a(proj3, vt, lam, cfar, bias5, subln_g, lam_init):
    B, S, _ = proj3.shape
    t = T_ATT
    nk = S // t
    n_blocks = B * HA * nk

    def scored(k):
        n = jnp.minimum(k // 2, n_blocks - 1)
        return n // (HA * nk), (n // nk) % HA, n % nk

    def lagged(k, lag):
        n = jnp.maximum(k - lag, 0) // 2
        return n // (HA * nk), (n // nk) % HA, n % nk

    def q_map(k):
        b, h, qi = scored(k)
        return b, qi, h

    def k_map(k):
        b, h, _ = scored(k)
        return b, 0, HA + h

    def vt_map(k):
        b, h, _ = lagged(k, 1)
        return b, h, 0, 0, 0

    def out_map(k):
        b, h, qi = lagged(k, 2)
        return b, qi, h

    kern = functools.partial(_attn_a_kernel, out_scale=1.0 - lam_init, nq=nk, n_blocks=n_blocks)
    return pl.pallas_call(
        kern,
        out_shape=jax.ShapeDtypeStruct((B, S, HA * 2 * DA), BF16),
        grid=(2 * n_blocks + 1,),
        in_specs=[
            pl.BlockSpec(memory_space=pltpu.SMEM),
            pl.BlockSpec(memory_space=pltpu.SMEM),
            pl.BlockSpec((1, t, LANES), q_map),
            pl.BlockSpec((1, S, LANES), k_map),
            pl.BlockSpec((1, 1, nk, LANES, t), vt_map),
            pl.BlockSpec((1, 5, t, t), lambda k: (scored(k)[1], 0, 0, 0)),
            pl.BlockSpec((1, LANES), lambda k: (0, 0)),
        ],
        out_specs=pl.BlockSpec((1, t, LANES), out_map),
        scratch_shapes=[pltpu.VMEM((nk, t, t), F32), pltpu.VMEM((nk, t, t), F32),
                        pltpu.VMEM((1, t), F32), pltpu.VMEM((1, t), F32),
                        pltpu.VMEM((LANES, t), F32), pltpu.VMEM((1, t), F32)],
        compiler_params=_cparams(("arbitrary",)),
        name="diff_attn",
    )(lam, cfar, proj3, proj3, vt, bias5, subln_g)


def _attn_a_bias(rel_bias):
    t = T_ATT
    tab = rel_bias[:, :HA].astype(F32) * LOG2E
    d = jnp.arange(-1, 2, dtype=jnp.int32)[:, None, None] * t
    rel = d + jnp.arange(t, dtype=jnp.int32)[None, :, None] - jnp.arange(t, dtype=jnp.int32)[None, None, :]
    near = _bias_lookup(_t5_bucket(rel), tab)
    far = tab[_t5_bucket(jnp.array([-(t + 1), t + 1], dtype=jnp.int32))].T
    fill = lambda side: jnp.broadcast_to(far[:, side, None, None, None], (HA, 1, t, t))
    tiles = jnp.concatenate([fill(0), near, fill(1)], axis=1)
    return tiles, far.reshape(2 * HA)


def _attn_b_kernel(q_ref, k_ref, v_ref, bias_ref, o_ref, lse_ref, *, sub_len, r, sp, qp):
    nblk = sub_len // QB_DIL
    low_half = lax.broadcasted_iota(jnp.int32, (1, LANES), 1) < HALF_LANES
    for si in range(sp):
        s = si if sp == r else pl.program_id(2) * sp + si
        for qb in range(qp):
            i = pl.program_id(1) * qp + qb
            start = jnp.clip(i * QB_DIL - HALF_WIN, 0, sub_len - KW_DIL)
            start = pl.multiple_of(start, HALF_WIN)
            variant = jnp.where(i == 0, 0, jnp.where(i == nblk - 1, 2, 1))
            q = q_ref[0, si, qb * QB_DIL:(qb + 1) * QB_DIL, :]
            kw = k_ref[0, s, pl.ds(start, KW_DIL), :]
            vw = v_ref[0, s, pl.ds(start, KW_DIL), :]
            rows = (slice(qb * QB_DIL, (qb + 1) * QB_DIL) if r == 1
                    else pl.ds(qb * QB_DIL * r + s, QB_DIL, stride=r))
            lane = lax.broadcasted_iota(jnp.int32, (QB_DIL, LANES), 1)
            lse_tile = jnp.zeros((QB_DIL, LANES), F32)
            for j in range(HB // 2):
                cols = slice(j * LANES, (j + 1) * LANES)
                qpair, kp, vp = q[:, cols], kw[:, cols], vw[:, cols]
                outs, lses = [], []
                for c in range(2):
                    qc = jnp.where(low_half if c == 0 else jnp.logical_not(low_half), qpair,
                                   jnp.zeros_like(qpair))
                    sc = lax.dot_general(qc, kp, (((1,), (1,)), ((), ())),
                                         preferred_element_type=F32)
                    sc = sc + bias_ref[2 * j + c, variant]
                    m = jnp.max(sc, axis=-1, keepdims=True)
                    p = jnp.exp2(sc - m)
                    l = jnp.sum(p, axis=-1, keepdims=True)
                    outs.append(jnp.dot(p.astype(BF16), vp, preferred_element_type=F32) / l)
                    lses.append((m + jnp.log2(l)) * LN2)
                o_ref[0, j, rows, :] = jnp.where(low_half, outs[0], outs[1])
                for c in range(2):
                    lse_tile = jnp.where(lane == 2 * j + c, lses[c], lse_tile)
            lse_ref[0, rows, :] = lse_tile


def _attn_b(qkv4, bias3, g, cols):
    B, r, L, _ = qkv4.shape
    S = r * L
    width = HB * DB
    nblk = L // QB_DIL
    sp = min(r, SUBS_DIL)
    qp = max(1, min(ITEMS_DIL // sp, OUT_ROWS_DIL // (QB_DIL * r)))
    qcol, kcol, vcol = cols
    kern = functools.partial(_attn_b_kernel, sub_len=L, r=r, sp=sp, qp=qp)
    slab = jax.ShapeDtypeStruct((B, N_SLABS, S, LANES), F32)
    slab_spec = pl.BlockSpec((1, N_SLABS, QB_DIL * r * qp, LANES), lambda b, i, s: (b, 0, i, 0))
    return pl.pallas_call(
        kern,
        out_shape=[slab, jax.ShapeDtypeStruct((B, S, LANES), F32)],
        grid=(B, nblk // qp, r // sp),
        in_specs=[
            pl.BlockSpec((1, sp, QB_DIL * qp, width), lambda b, i, s: (b, s, i, qcol)),
            pl.BlockSpec((1, r, L, width), lambda b, i, s: (b, 0, 0, kcol)),
            pl.BlockSpec((1, r, L, width), lambda b, i, s: (b, 0, 0, vcol)),
            pl.BlockSpec((HB, 3, QB_DIL, KW_DIL), lambda b, i, s: (0, 0, 0, 0)),
        ],
        out_specs=[slab_spec, pl.BlockSpec((1, QB_DIL * r * qp, LANES), lambda b, i, s: (b, i, 0))],
        compiler_params=_cparams(("parallel", "arbitrary", "arbitrary")),
        name=f"dilated_attn_{g}",
    )(qkv4, qkv4, qkv4, bias3)


def _attn_b_bias(rel_bias, g):
    r = DILATIONS[g]
    tab = rel_bias[:, HA + g * HB: HA + (g + 1) * HB].astype(F32) * LOG2E
    off = jnp.arange(3, dtype=jnp.int32)[:, None, None] * HALF_WIN
    rel = (jnp.arange(KW_DIL, dtype=jnp.int32)[None, None, :] - off
           - jnp.arange(QB_DIL, dtype=jnp.int32)[None, :, None])
    bias = _bias_lookup(_t5_bucket(rel * r), tab)
    return jnp.where((jnp.abs(rel) <= HALF_WIN)[None], bias, NEG)


def _sgu_kernel(zu_ref, zv_ref, lng_ref, lnb_ref, ws_ref, bs_ref, o_ref):
    u = jax.nn.gelu(zu_ref[...].astype(F32))
    v = jax.nn.gelu(zv_ref[...].astype(F32))
    mu = jnp.mean(v, axis=-1, keepdims=True)
    var = jnp.mean(jnp.square(v - mu), axis=-1, keepdims=True)
    v = ((v - mu) * lax.rsqrt(var + EPS) * lng_ref[...] + lnb_ref[...]).astype(BF16)
    gd = v.shape[1] // C_GROUPS
    for n in range(v.shape[0] // CHUNK):
        rows = slice(n * CHUNK, (n + 1) * CHUNK)
        for g in range(C_GROUPS):
            cols = slice(g * gd, (g + 1) * gd)
            mixed = jnp.dot(ws_ref[g], v[rows, cols], preferred_element_type=F32) + bs_ref[:, cols]
            o_ref[rows, cols] = (u[rows, cols] * mixed).astype(o_ref.dtype)


def _sgu(proj2, ln_g, ln_b, w_s, b_exp):
    T = proj2.shape[0]
    tm = min(TM_SGU, T)
    w = MIX_W
    return pl.pallas_call(
        _sgu_kernel,
        out_shape=jax.ShapeDtypeStruct((T, w), BF16),
        grid=(T // tm,),
        in_specs=[pl.BlockSpec((tm, w), lambda i: (i, COL_ZU)),
                  pl.BlockSpec((tm, w), lambda i: (i, COL_ZU + 1)),
                  pl.BlockSpec((1, w), lambda i: (0, 0)),
                  pl.BlockSpec((1, w), lambda i: (0, 0)),
                  pl.BlockSpec((C_GROUPS, CHUNK, CHUNK), lambda i: (0, 0, 0)),
                  pl.BlockSpec((CHUNK, w), lambda i: (0, 0))],
        out_specs=pl.BlockSpec((tm, w), lambda i: (i, 0)),
        compiler_params=_cparams(("parallel",)),
        name="sgu",
    )(proj2, proj2, ln_g, ln_b, w_s, b_exp)


def _route(logits):
    lane = lax.broadcasted_iota(jnp.int32, logits.shape, 1)
    big = jnp.int32(LANES)
    is_grp = (lane >= N_EXPERTS) & (lane < N_EXPERTS + N_GROUPS)
    gl = jnp.where(is_grp, logits, NEG)
    gmax = jnp.max(gl, axis=-1, keepdims=True)
    g_idx = jnp.min(jnp.where(is_grp & (gl == gmax), lane, big), axis=-1, keepdims=True) - N_EXPERTS
    g_w = 1.0 / jnp.sum(jnp.where(is_grp, jnp.exp(gl - gmax), 0.0), axis=-1, keepdims=True)
    in_grp = (lane >= g_idx * E_PER_GROUP) & (lane < (g_idx + 1) * E_PER_GROUP)
    sel = jnp.where(in_grp, logits, NEG)
    v1 = jnp.max(sel, axis=-1, keepdims=True)
    i1 = jnp.min(jnp.where(in_grp & (sel == v1), lane, big), axis=-1, keepdims=True)
    rest = in_grp & (lane != i1)
    sel2 = jnp.where(rest, logits, NEG)
    v2 = jnp.max(sel2, axis=-1, keepdims=True)
    i2 = jnp.min(jnp.where(rest & (sel2 == v2), lane, big), axis=-1, keepdims=True)
    e2 = jnp.exp(v2 - v1)
    w1 = g_w / (1.0 + e2)
    w2 = g_w * e2 / (1.0 + e2)
    return jnp.where(lane == i1, w1, jnp.where(lane == i2, w2, 0.0))


def _mix_kernel(x_ref, ya_ref, ob0_ref, ob1_ref, ob2_ref, ls0_ref, ls1_ref, ls2_ref, yc_ref,
                g0_ref, g1_ref, g2_ref, wb_ref, wo_ref, nf_ref, wr_ref, br_ref,
                xo_ref, h_ref, comb_ref):
    ls0, ls1, ls2 = ls0_ref[...], ls1_ref[...], ls2_ref[...]
    mx = jnp.maximum(jnp.maximum(ls0, ls1), ls2)
    es = [jnp.exp(ls0 - mx), jnp.exp(ls1 - mx), jnp.exp(ls2 - mx)]
    inv = 1.0 / (es[0] + es[1] + es[2])
    spread = jnp.where(lax.broadcasted_iota(jnp.int32, (2 * LANES, MIX_W), 1) // DB
                       == lax.broadcasted_iota(jnp.int32, (2 * LANES, MIX_W), 0) % LANES,
                       1.0, 0.0).astype(BF16)
    yb = None
    for e, ob_ref in zip(es, (ob0_ref, ob1_ref, ob2_ref)):
        w = e * inv
        hi = w.astype(BF16)
        lo = (w - hi.astype(F32)).astype(BF16)
        wide = jnp.dot(jnp.concatenate([hi, lo], axis=1), spread, preferred_element_type=F32)
        term = wide * jnp.concatenate([ob_ref[0, j] for j in range(N_SLABS)], axis=-1)
        yb = term if yb is None else yb + term
    yb = yb.astype(BF16)
    merged = jax.nn.sigmoid(g0_ref[...].astype(F32)) * jnp.dot(ya_ref[...], wb_ref[0],
                                                               preferred_element_type=F32)
    merged += jax.nn.sigmoid(g1_ref[...].astype(F32)) * jnp.dot(yb, wb_ref[1],
                                                                preferred_element_type=F32)
    merged += jax.nn.sigmoid(g2_ref[...].astype(F32)) * jnp.dot(yc_ref[...], wb_ref[2],
                                                                preferred_element_type=F32)
    xn = x_ref[...] + jnp.dot(merged.astype(BF16), wo_ref[...], preferred_element_type=F32)
    xo_ref[...] = xn
    h = _rms_bf16(xn, nf_ref[...])
    h_ref[...] = h
    logits = jnp.dot(h, wr_ref[...], preferred_element_type=F32) + br_ref[...]
    comb_ref[...] = _route(logits)


def _mix(x2, ya, obs, lses, yc, proj2, wb, wo, nf, wr, br):
    T, D = x2.shape
    S = obs[0].shape[2]
    tm = min(TM_MIX, S)
    per_b = S // tm
    w = MIX_W
    row = lambda width: pl.BlockSpec((tm, width), lambda i: (i, 0))
    full = lambda a: pl.BlockSpec(a.shape, lambda i: (0,) * a.ndim)
    gate = lambda n: pl.BlockSpec((tm, D), lambda i: (i, COL_GATE + n))
    slab = pl.BlockSpec((1, N_SLABS, tm, LANES), lambda i: (i // per_b, 0, i % per_b, 0))
    return pl.pallas_call(
        _mix_kernel,
        out_shape=[jax.ShapeDtypeStruct((T, D), F32), jax.ShapeDtypeStruct((T, D), BF16),
                   jax.ShapeDtypeStruct((T, LANES), F32)],
        grid=(T // tm,),
        in_specs=[row(D), row(w), slab, slab, slab, row(LANES), row(LANES), row(LANES), row(w),
                  gate(0), gate(1), gate(2), full(wb), full(wo), full(nf), full(wr), full(br)],
        out_specs=[row(D), row(D), row(LANES)],
        compiler_params=_cparams(("parallel",)),
        name="mix",
    )(x2, ya, obs[0], obs[1], obs[2], lses[0], lses[1], lses[2], yc, proj2, proj2, proj2,
      wb, wo, nf, wr, br)


def _moe_kernel(h_ref, comb_ref, x_ref, wg_ref, wu_ref, wd_ref, nfin_ref, o_ref, acc_scr,
                *, final_norm):
    e = pl.program_id(1)

    @pl.when(e == 0)
    def _():
        acc_scr[...] = jnp.zeros(acc_scr.shape, F32)

    h = h_ref[...]
    lane = lax.broadcasted_iota(jnp.int32, comb_ref.shape, 1)
    c = jnp.sum(jnp.where(lane == e, comb_ref[...], 0.0), axis=-1, keepdims=True)
    hid = (jax.nn.silu(jnp.dot(h, wg_ref[0].astype(BF16), preferred_element_type=F32))
           * jnp.dot(h, wu_ref[0].astype(BF16), preferred_element_type=F32))
    acc_scr[...] += c * jnp.dot(hid.astype(BF16), wd_ref[0].astype(BF16),
                                preferred_element_type=F32)

    @pl.when(e == pl.num_programs(1) - 1)
    def _():
        xn = x_ref[...] + acc_scr[...]
        if final_norm:
            ms = jnp.mean(xn * xn, axis=-1, keepdims=True)
            xn = xn * lax.rsqrt(ms + EPS) * nfin_ref[...]
        o_ref[...] = xn


def _moe_dense(h, comb, x2, wg, wu, wd, e0, nfin, final_norm):
    T, D = x2.shape
    tm = min(TM_MOE, T)
    F = wg.shape[2]
    kern = functools.partial(_moe_kernel, final_norm=final_norm)
    return pl.pallas_call(
        kern,
        out_shape=jax.ShapeDtypeStruct((T, D), F32),
        grid=(T // tm, N_EXPERTS),
        in_specs=[pl.BlockSpec((tm, D), lambda i, e: (i, 0)),
                  pl.BlockSpec((tm, LANES), lambda i, e: (i, 0)),
                  pl.BlockSpec((tm, D), lambda i, e: (i, 0)),
                  pl.BlockSpec((1, D, F), lambda i, e: (e0 + e, 0, 0)),
                  pl.BlockSpec((1, D, F), lambda i, e: (e0 + e, 0, 0)),
                  pl.BlockSpec((1, F, D), lambda i, e: (e0 + e, 0, 0)),
                  pl.BlockSpec((1, D), lambda i, e: (0, 0))],
        out_specs=pl.BlockSpec((tm, D), lambda i, e: (i, 0)),
        scratch_shapes=[pltpu.VMEM((tm, D), F32)],
        compiler_params=_cparams(("parallel", "arbitrary")),
        name="moe_dense",
    )(h, comb, x2, wg, wu, wd, nfin)


def _moe_dispatch_kernel(h_ref, comb_ref, o_ref, cnt_ref):
    nt = o_ref.shape[0]
    tm = h_ref.shape[0] // nt
    before = jnp.where(lax.broadcasted_iota(jnp.int32, (tm, tm), 0)
                       < lax.broadcasted_iota(jnp.int32, (tm, tm), 1), 1.0, 0.0).astype(BF16)
    slot = lax.broadcasted_iota(jnp.int32, (MOE_CAP, tm), 0).astype(F32)
    for u in range(nt):
        rows = slice(u * tm, (u + 1) * tm)
        comb = comb_ref[rows, :]
        hi = comb.astype(BF16)
        lo = (comb - hi.astype(F32)).astype(BF16)
        haug = jnp.concatenate([h_ref[rows, :], hi, lo], axis=1)
        a_t = comb.T[:N_EXPERTS] > 0.0
        a_f = jnp.where(a_t, 1.0, 0.0)
        rank_t = jnp.dot(a_f.astype(BF16), before, preferred_element_type=F32)
        blocks = [jnp.where((slot == rank_t[e:e + 1]) & a_t[e:e + 1], 1.0, 0.0).astype(BF16)
                  for e in range(N_EXPERTS)]
        res = jnp.dot(jnp.concatenate(blocks, axis=0), haug, preferred_element_type=F32)
        res = res.astype(o_ref.dtype)
        for e in range(N_EXPERTS):
            o_ref[u, e] = res[e * MOE_CAP:(e + 1) * MOE_CAP]
        cnt_ref[u] = jnp.broadcast_to(jnp.sum(a_f, axis=1, keepdims=True), cnt_ref.shape[1:])


def _moe_dispatch(h, comb):
    T, D = h.shape
    tm = min(TM_DISP, T)
    n = T // tm
    nt = math.gcd(TILES_PER_STEP, n)
    return pl.pallas_call(
        _moe_dispatch_kernel,
        out_shape=[jax.ShapeDtypeStruct((n, N_EXPERTS, MOE_CAP, D + 2 * LANES), BF16),
                   jax.ShapeDtypeStruct((n, N_EXPERTS, LANES), F32)],
        grid=(n // nt,),
        in_specs=[pl.BlockSpec((nt * tm, D), lambda i: (i, 0)),
                  pl.BlockSpec((nt * tm, LANES), lambda i: (i, 0))],
        out_specs=[pl.BlockSpec((nt, N_EXPERTS, MOE_CAP, D + 2 * LANES), lambda i: (i, 0, 0, 0)),
                   pl.BlockSpec((nt, N_EXPERTS, LANES), lambda i: (i, 0, 0))],
        compiler_params=_cparams(("parallel",)),
        name="moe_dispatch",
    )(h, comb)


def _moe_ffn_kernel(n16_ref, s_ref, wg_ref, wu_ref, wd_ref, o_ref,
                    wg_scr, wu_scr, wd_scr, lhs_scr, y_scr):
    e, c = pl.program_id(0), pl.program_id(1)

    @pl.when(c == 0)
    def _():
        wg_scr[...] = wg_ref[0].astype(BF16)
        wu_scr[...] = wu_ref[0].astype(BF16)
        wd_scr[...] = wd_ref[0].astype(BF16)

    g, _, cap, _ = s_ref.shape
    D = o_ref.shape[-1]
    @pl.when((e == 0) & (c == 0))
    def _():
        lhs_scr[...] = jnp.zeros(lhs_scr.shape, lhs_scr.dtype)
        y_scr[...] = jnp.zeros(y_scr.shape, y_scr.dtype)

    offs = []
    off = jnp.int32(0)
    for t in range(g):
        offs.append(off)
        lhs_scr[pl.ds(pl.multiple_of(off, BF16_ROWS), cap), :] = s_ref[t, 0]
        off = off + n16_ref[(c * g + t) * N_EXPERTS + e]
    total = off

    def run(nrows):
        rows = lhs_scr[:nrows]
        h = rows[:, :D]
        wparts = rows[:, D:].astype(F32)
        lane = lax.broadcasted_iota(jnp.int32, wparts.shape, 1)
        w = jnp.sum(jnp.where(lane % LANES == e, wparts, 0.0), axis=-1, keepdims=True)
        hid = (jax.nn.silu(jnp.dot(h, wg_scr[...], preferred_element_type=F32))
               * jnp.dot(h, wu_scr[...], preferred_element_type=F32))
        y = w * jnp.dot(hid.astype(BF16), wd_scr[...], preferred_element_type=F32)
        y_scr[:nrows] = y.astype(y_scr.dtype)

    classes = tuple(range(g * cap // 2, g * cap + 1, FFN_ROW_STEP))
    lower = 0
    for nrows in classes:
        pl.when((total > lower) & (total <= nrows))(functools.partial(run, nrows))
        lower = nrows

    for t in range(g):
        o_ref[t, 0] = y_scr[pl.ds(pl.multiple_of(offs[t], BF16_ROWS), cap), :]


def _moe_ffn(srt, n16, wg, wu, wd, e0):
    n, ne, cap, wdt = srt.shape
    D, F = wg.shape[1], wg.shape[2]
    g = math.gcd(G_FFN, n)
    return pl.pallas_call(
        _moe_ffn_kernel,
        out_shape=jax.ShapeDtypeStruct((n, ne, cap, D), BF16),
        grid_spec=pltpu.PrefetchScalarGridSpec(
            num_scalar_prefetch=1,
            grid=(ne, n // g),
            in_specs=[pl.BlockSpec((g, 1, cap, wdt), lambda e, c, n16: (c, e, 0, 0)),
                      pl.BlockSpec((1, D, F), lambda e, c, n16: (e0 + e, 0, 0)),
                      pl.BlockSpec((1, D, F), lambda e, c, n16: (e0 + e, 0, 0)),
                      pl.BlockSpec((1, F, D), lambda e, c, n16: (e0 + e, 0, 0))],
            out_specs=pl.BlockSpec((g, 1, cap, D), lambda e, c, n16: (c, e, 0, 0)),
            scratch_shapes=[pltpu.VMEM((D, F), BF16), pltpu.VMEM((D, F), BF16),
                            pltpu.VMEM((F, D), BF16), pltpu.VMEM((g * cap, wdt), BF16),
                            pltpu.VMEM((g * cap, D), BF16)]),
        compiler_params=_cparams(("arbitrary", "arbitrary")),
        name="moe_ffn",
    )(n16, srt, wg, wu, wd)


def _moe_combine_kernel(skip_ref, y_ref, comb_ref, x_ref, nfin_ref, o_ref, *, final_norm):
    nt = y_ref.shape[0]
    tm = x_ref.shape[0] // nt
    ncol = N_EXPERTS * MOE_CAP
    before = jnp.where(lax.broadcasted_iota(jnp.int32, (tm, tm), 1)
                       < lax.broadcasted_iota(jnp.int32, (tm, tm), 0), 1.0, 0.0).astype(BF16)
    spread = jnp.where(lax.broadcasted_iota(jnp.int32, (LANES, ncol), 1) // MOE_CAP
                       == lax.broadcasted_iota(jnp.int32, (LANES, ncol), 0), 1.0, 0.0).astype(BF16)
    slot = (lax.broadcasted_iota(jnp.int32, (tm, ncol), 1) % MOE_CAP).astype(F32)
    for u in range(nt):
        rows = slice(u * tm, (u + 1) * tm)
        a = comb_ref[rows, :] > 0.0
        rank = jnp.dot(before, jnp.where(a, 1.0, 0.0).astype(BF16), preferred_element_type=F32)
        key = jnp.where(a, rank, -1.0).astype(BF16)
        key_all = jnp.dot(key, spread, preferred_element_type=F32)
        pc = jnp.where(slot == key_all, 1.0, 0.0).astype(BF16)
        y = jnp.concatenate([y_ref[u, e] for e in range(N_EXPERTS)], axis=0)
        xn = x_ref[rows, :] + jnp.dot(pc, y, preferred_element_type=F32)
        if final_norm:
            ms = jnp.mean(xn * xn, axis=-1, keepdims=True)
            later = skip_ref[pl.program_id(0) * nt + u] == 1
            xn = jnp.where(later, xn, xn * lax.rsqrt(ms + EPS) * nfin_ref[...])
        o_ref[rows, :] = xn


def _moe_combine(skip, y, comb, x2, nfin, final_norm):
    T, D = x2.shape
    n, ne, cap, _ = y.shape
    tm = T // n
    nt = math.gcd(TILES_PER_STEP, n)
    kern = functools.partial(_moe_combine_kernel, final_norm=final_norm)
    return pl.pallas_call(
        kern,
        out_shape=jax.ShapeDtypeStruct((T, D), F32),
        grid_spec=pltpu.PrefetchScalarGridSpec(
            num_scalar_prefetch=1,
            grid=(n // nt,),
            in_specs=[pl.BlockSpec((nt, ne, cap, D), lambda i, sk: (i, 0, 0, 0)),
                      pl.BlockSpec((nt * tm, LANES), lambda i, sk: (i, 0)),
                      pl.BlockSpec((nt * tm, D), lambda i, sk: (i, 0)),
                      pl.BlockSpec((1, D), lambda i, sk: (0, 0))],
            out_specs=pl.BlockSpec((nt * tm, D), lambda i, sk: (i, 0))),
        compiler_params=_cparams(("parallel",)),
        name="moe_combine",
    )(skip, y, comb, x2, nfin)


def _moe_fix_kernel(tiles_ref, experts_ref, first_ref, last_ref, n_ref, h_ref, comb_ref, prev_ref,
                    wg_ref, wu_ref, wd_ref, nfin_ref, o_ref, *, final_norm):
    del tiles_ref
    s = pl.program_id(0)

    @pl.when(s < n_ref[0])
    def _():
        e = experts_ref[s]
        tm = h_ref.shape[0]
        comb = comb_ref[...]
        a = jnp.where(comb > 0.0, 1.0, 0.0)
        before = (lax.broadcasted_iota(jnp.int32, (tm, tm), 1)
                  < lax.broadcasted_iota(jnp.int32, (tm, tm), 0))
        rank = jnp.dot(jnp.where(before, 1.0, 0.0).astype(BF16), a.astype(BF16),
                       preferred_element_type=F32)
        lane = lax.broadcasted_iota(jnp.int32, comb.shape, 1)
        dropped = (lane == e) & (rank >= MOE_CAP)
        c = jnp.sum(jnp.where(dropped, comb, 0.0), axis=-1, keepdims=True)
        h = h_ref[...]
        hid = (jax.nn.silu(jnp.dot(h, wg_ref[0].astype(BF16), preferred_element_type=F32))
               * jnp.dot(h, wu_ref[0].astype(BF16), preferred_element_type=F32))
        add = c * jnp.dot(hid.astype(BF16), wd_ref[0].astype(BF16), preferred_element_type=F32)
        fresh = first_ref[s] == 1

        @pl.when(fresh)
        def _():
            o_ref[...] = prev_ref[...] + add

        @pl.when(jnp.logical_not(fresh))
        def _():
            o_ref[...] += add

        if final_norm:
            @pl.when(last_ref[s] == 1)
            def _():
                xn = o_ref[...]
                ms = jnp.mean(xn * xn, axis=-1, keepdims=True)
                o_ref[...] = xn * lax.rsqrt(ms + EPS) * nfin_ref[...]


def _moe_fix(tiles, experts, first, last, n, out, h, comb, wg, wu, wd, e0, nfin, final_norm):
    T, D = out.shape
    tm = min(TM_DISP, T)
    F = wg.shape[2]
    tile = lambda width: pl.BlockSpec((tm, width), lambda s, tl, ex, fi, la, n: (tl[s], 0))
    wspec = lambda shape: pl.BlockSpec(shape, lambda s, tl, ex, fi, la, n: (e0 + ex[s], 0, 0))
    kern = functools.partial(_moe_fix_kernel, final_norm=final_norm)
    return pl.pallas_call(
        kern,
        out_shape=jax.ShapeDtypeStruct((T, D), F32),
        grid_spec=pltpu.PrefetchScalarGridSpec(
            num_scalar_prefetch=5,
            grid=(MAX_OVF,),
            in_specs=[tile(D), tile(LANES), tile(D), wspec((1, D, F)), wspec((1, D, F)),
                      wspec((1, F, D)), pl.BlockSpec((1, D), lambda s, tl, ex, fi, la, n: (0, 0))],
            out_specs=tile(D)),
        input_output_aliases={7: 0},
        compiler_params=_cparams(("arbitrary",)),
        name="moe_fix",
    )(tiles, experts, first, last, n, h, comb, out, wg, wu, wd, nfin)


def _moe(h, comb, x2, wg, wu, wd, e0, nfin, final_norm):
    srt, cnt = _moe_dispatch(h, comb)
    over = (cnt[:, :, 0] > MOE_CAP).reshape(-1)
    n_ovf = jnp.sum(over.astype(jnp.int32))
    pairs = jnp.nonzero(over, size=MAX_OVF, fill_value=0)[0].astype(jnp.int32)
    pairs = jnp.where(jnp.arange(MAX_OVF) < n_ovf, pairs, pairs[jnp.clip(n_ovf - 1, 0, MAX_OVF - 1)])
    tiles, experts = pairs // N_EXPERTS, pairs % N_EXPERTS
    change = (tiles[1:] != tiles[:-1]).astype(jnp.int32)
    first = jnp.concatenate([jnp.ones((1,), jnp.int32), change])
    last = jnp.maximum(jnp.concatenate([change, jnp.ones((1,), jnp.int32)]),
                       (jnp.arange(MAX_OVF) == n_ovf - 1).astype(jnp.int32))
    skip = jnp.any(over.reshape(-1, N_EXPERTS), axis=1).astype(jnp.int32)

    used = jnp.minimum(cnt[:, :, 0], MOE_CAP).astype(jnp.int32).reshape(-1)
    n16 = (used + (BF16_ROWS - 1)) // BF16_ROWS * BF16_ROWS

    def routed():
        out = _moe_combine(skip, _moe_ffn(srt, n16, wg, wu, wd, e0), comb, x2, nfin, final_norm)
        return lax.cond(
            n_ovf > 0,
            lambda: _moe_fix(tiles, experts, first, last, n_ovf.reshape(1), out, h, comb,
                             wg, wu, wd, e0, nfin, final_norm),
            lambda: out)

    return lax.cond(n_ovf > MAX_OVF,
                    lambda: _moe_dense(h, comb, x2, wg, wu, wd, e0, nfin, final_norm), routed)


def kernel(x, rel_bias, norm_mix, w_in, diff_lambda, diff_subln, sgu_ln_g, sgu_ln_b, sgu_w, sgu_b,
           w_branch, w_out, norm_ffn, w_router_grp, b_router_grp, w_router_exp, b_router_exp,
           w_gate, w_up, w_down, norm_final):
    B, S, D = x.shape
    T = B * S
    depth = w_in.shape[0]
    a_out = HA * 2 * DA
    grp_w = HB * DB
    b_cols = 3 * NG_B * grp_w
    qkv_b0 = 3 * a_out
    zc0 = qkv_b0 + b_cols
    gate0 = zc0 + 2 * MIX_W
    qk_scale = DA ** -0.5

    bias_a, cfar = _attn_a_bias(rel_bias)
    bias_b = [_attn_b_bias(rel_bias, g) for g in range(NG_B)]

    col = jnp.arange(w_in.shape[2])
    is_q = (col < a_out) | ((col >= qkv_b0) & (col < qkv_b0 + NG_B * grp_w))
    col_scale = jnp.where(is_q, qk_scale * LOG2E, 1.0)

    def group_cols(w, g):
        return [w[:, qkv_b0 + (c * NG_B + g) * grp_w: qkv_b0 + (c * NG_B + g + 1) * grp_w]
                for c in range(3)]

    wg_all = w_gate.reshape((-1,) + w_gate.shape[2:])
    wu_all = w_up.reshape((-1,) + w_up.shape[2:])
    wd_all = w_down.reshape((-1,) + w_down.shape[2:])

    x2 = x.reshape(T, D)
    for i in range(depth):
        w = (w_in[i] * col_scale.astype(F32)).astype(BF16)
        nm = norm_mix[i][None, :]
        w_main = jnp.concatenate([w[:, :2 * a_out], w[:, zc0:]] + group_cols(w, 0), axis=1)
        x3 = x2.reshape(B, S, D)
        proj3, vt = _inproj(x3, nm, w_main, w[:, 2 * a_out:3 * a_out].T)
        proj2 = proj3.reshape(T, proj3.shape[2])

        lam_init = 0.8 - 0.6 * math.exp(-0.3 * i)
        lp = diff_lambda[i].astype(F32)
        lam = jnp.exp(jnp.sum(lp[0] * lp[1])) - jnp.exp(jnp.sum(lp[2] * lp[3])) + lam_init
        ya = _attn_a(proj3, vt, lam.reshape(1), cfar, bias_a, diff_subln[i][None, :], lam_init)

        strided = [g for g in range(NG_B) if DILATIONS[g] > 1]
        permuted = dict(zip(strided, _inproj_perm(
            x3, nm, [jnp.concatenate(group_cols(w, g), axis=1) for g in strided],
            [DILATIONS[g] for g in strided])))
        obs, lses = [], []
        for g in range(NG_B):
            if g in permuted:
                qkv4, cols = permuted[g], (0, 1, 2)
            else:
                qkv4, cols = proj3[:, None], (COL_QKV0, COL_QKV0 + 1, COL_QKV0 + 2)
            o, l = _attn_b(qkv4, bias_b[g], g, cols)
            obs.append(o)
            lses.append(l.reshape(T, LANES))

        b_exp = jnp.repeat(sgu_b[i].T, MIX_W // C_GROUPS, axis=1)
        yc = _sgu(proj2, sgu_ln_g[i][None, :], sgu_ln_b[i][None, :], sgu_w[i].astype(BF16), b_exp)

        wr = jnp.concatenate([w_router_exp[i].transpose(1, 0, 2).reshape(D, N_EXPERTS),
                              w_router_grp[i]], axis=1)
        wr = jnp.pad(wr, ((0, 0), (0, LANES - wr.shape[1]))).astype(BF16)
        br = jnp.concatenate([b_router_exp[i].reshape(N_EXPERTS), b_router_grp[i]])
        br = jnp.pad(br, (0, LANES - br.shape[0]))[None, :].astype(F32)

        x2, h, comb = _mix(x2, ya.reshape(T, a_out), obs, lses, yc, proj2,
                           w_branch[i].astype(BF16), w_out[i].astype(BF16), norm_ffn[i][None, :],
                           wr, br)
        x2 = _moe(h, comb, x2, wg_all, wu_all, wd_all, i * N_EXPERTS, norm_final[None, :],
                  i == depth - 1)
    return x2.reshape(B, S, D)
```

```python
import functools
import math

import jax
import jax.numpy as jnp
from jax import lax
from jax.experimental import pallas as pl
from jax.experimental.pallas import tpu as pltpu

F32 = jnp.float32
BF16 = jnp.bfloat16

EPS = 1e-6
NEG = -1e30
LOG2E = 1.4426950408889634
LN2 = 0.6931471805599453
LANES = 128
HALF_LANES = LANES // 2
VMEM_LIMIT = 48 * 1024 * 1024

HA = 4
DA = 64
MIX_W = 512
WINDOWS = (128, 512, 2048)
DILATIONS = (1, 4, 16)
NG_B = 3
HB = 8
DB = 64
HALF_WIN = 64
CHUNK = 128
C_GROUPS = 4
N_BRANCH = 3
N_BUCKETS = 32
MAX_DIST = 128
N_GROUPS = 4
E_PER_GROUP = 4
N_EXPERTS = N_GROUPS * E_PER_GROUP
N_SLABS = MIX_W // LANES

TM_PROJ = 1024
TN_PROJ = 3328
TM_PERM = 1024
PERM_BLK = 256
T_ATT = 512
QB_DIL = 128
KW_DIL = QB_DIL + 2 * HALF_WIN
ITEMS_DIL = 8
SUBS_DIL = 4
OUT_ROWS_DIL = 2048
TM_SGU = 2048
TM_MIX = 512
TM_MOE = 1024
TM_DISP = 256
MOE_CAP = HALF_LANES
G_FFN = 16
TILES_PER_STEP = 4
FFN_ROW_STEP = 64
BF16_ROWS = 16
MAX_OVF = 64

COL_ZU = 2
COL_GATE = 2
COL_QKV0 = 10


def _cparams(sem):
    return pltpu.CompilerParams(dimension_semantics=sem, vmem_limit_bytes=VMEM_LIMIT)


def _t5_bucket(rel):
    nb = N_BUCKETS // 2
    max_exact = nb // 2
    ret = (rel > 0).astype(jnp.int32) * nb
    n = jnp.abs(rel)
    nf = jnp.maximum(n, 1).astype(F32)
    large = max_exact + (jnp.log(nf / max_exact) / math.log(MAX_DIST / max_exact)
                         * (nb - max_exact)).astype(jnp.int32)
    large = jnp.minimum(large, nb - 1)
    return ret + jnp.where(n < max_exact, n, large)


def _bias_lookup(bucket, tab):
    out = jnp.zeros((tab.shape[1],) + bucket.shape, F32)
    expand = (slice(None),) + (None,) * bucket.ndim
    for b in range(N_BUCKETS):
        out = jnp.where(bucket[None] == b, tab[b][expand], out)
    return out


def _rms_bf16(x, g):
    ms = jnp.mean(x * x, axis=-1, keepdims=True)
    return (x * lax.rsqrt(ms + EPS) * g).astype(BF16)


def _inproj_kernel(x_ref, g_ref, w_ref, wt_ref, o_ref, vt_ref, h_scr):
    @pl.when(pl.program_id(2) == 0)
    def _():
        h = _rms_bf16(x_ref[0], g_ref[...])
        h_scr[...] = h
        res = lax.dot_general(wt_ref[...], h, (((1,), (1,)), ((), ())),
                              preferred_element_type=F32).astype(vt_ref.dtype)
        for hd in range(vt_ref.shape[1]):
            for n in range(vt_ref.shape[2]):
                vt_ref[0, hd, n] = res[hd * LANES:(hd + 1) * LANES, n * T_ATT:(n + 1) * T_ATT]

    o_ref[0] = jnp.dot(h_scr[...], w_ref[...], preferred_element_type=F32).astype(o_ref.dtype)


def _inproj(x3, g, w, wt):
    B, S, D = x3.shape
    N = w.shape[1]
    tm = min(TM_PROJ, S)
    nh, nb = wt.shape[0] // LANES, tm // T_ATT
    return pl.pallas_call(
        _inproj_kernel,
        out_shape=[jax.ShapeDtypeStruct((B, S, N), BF16),
                   jax.ShapeDtypeStruct((B, nh, S // T_ATT, LANES, T_ATT), BF16)],
        grid=(B, S // tm, N // TN_PROJ),
        in_specs=[pl.BlockSpec((1, tm, D), lambda b, i, j: (b, i, 0)),
                  pl.BlockSpec((1, D), lambda b, i, j: (0, 0)),
                  pl.BlockSpec((D, TN_PROJ), lambda b, i, j: (0, j)),
                  pl.BlockSpec(wt.shape, lambda b, i, j: (0, 0))],
        out_specs=[pl.BlockSpec((1, tm, TN_PROJ), lambda b, i, j: (b, i, j)),
                   pl.BlockSpec((1, nh, nb, LANES, T_ATT), lambda b, i, j: (b, 0, i, 0, 0))],
        scratch_shapes=[pltpu.VMEM((tm, D), BF16)],
        compiler_params=_cparams(("parallel", "parallel", "arbitrary")),
        name="inproj",
    )(x3, g, w, wt)


def _inproj_perm_kernel(x_ref, g_ref, *refs, dilations):
    ng = len(dilations)
    p_refs, w_refs, o_refs = refs[:ng], refs[ng:2 * ng], refs[2 * ng:]
    h = _rms_bf16(x_ref[0], g_ref[...])
    nblk = h.shape[0] // PERM_BLK
    for r, p_ref, w_ref, o_ref in zip(dilations, p_refs, w_refs, o_refs):
        hp = jnp.concatenate(
            [jnp.dot(p_ref[...], h[k * PERM_BLK:(k + 1) * PERM_BLK], preferred_element_type=F32)
             for k in range(nblk)], axis=0).astype(BF16)
        res = jnp.dot(hp, w_ref[...], preferred_element_type=F32).astype(o_ref.dtype)
        n = PERM_BLK // r
        for k in range(nblk):
            for s in range(r):
                o_ref[0, s, k * n:(k + 1) * n, :] = res[k * PERM_BLK + s * n:k * PERM_BLK + (s + 1) * n, :]


def _inproj_perm(x3, g, ws, dilations):
    B, S, D = x3.shape
    tm = min(TM_PERM, S)
    perms = []
    for r in dilations:
        n = PERM_BLK // r
        o = jnp.arange(PERM_BLK, dtype=jnp.int32)
        src = (o % n) * r + o // n
        perms.append((src[:, None] == jnp.arange(PERM_BLK, dtype=jnp.int32)[None, :]).astype(BF16))
    kern = functools.partial(_inproj_perm_kernel, dilations=tuple(dilations))
    return pl.pallas_call(
        kern,
        out_shape=[jax.ShapeDtypeStruct((B, r, S // r, w.shape[1]), BF16)
                   for r, w in zip(dilations, ws)],
        grid=(B, S // tm),
        in_specs=([pl.BlockSpec((1, tm, D), lambda b, i: (b, i, 0)),
                   pl.BlockSpec((1, D), lambda b, i: (0, 0))]
                  + [pl.BlockSpec((PERM_BLK, PERM_BLK), lambda b, i: (0, 0)) for _ in dilations]
                  + [pl.BlockSpec(w.shape, lambda b, i: (0, 0)) for w in ws]),
        out_specs=[pl.BlockSpec((1, r, tm // r, w.shape[1]), lambda b, i: (b, 0, i, 0))
                   for r, w in zip(dilations, ws)],
        compiler_params=_cparams(("parallel", "parallel")),
        name="inproj_perm",
    )(x3, g, *perms, *ws)


def _attn_a_kernel(lam_ref, cfar_ref, q_ref, k_ref, vt_ref, bias_ref, g_ref, o_ref,
                   st0_scr, st1_scr, m0_scr, m1_scr, acc_scr, l_scr, *, out_scale, nq, n_blocks):
    k = pl.program_id(0)
    t = T_ATT
    nk = k_ref.shape[1] // t
    n1 = jnp.minimum(k // 2, n_blocks - 1)
    n2 = jnp.maximum(k - 1, 0) // 2
    h1, qi1 = (n1 // nq) % HA, n1 % nq
    h2, qi2 = (n2 // nq) % HA, n2 % nq

    @pl.when(k == 0)
    def _():
        st1_scr[...] = jnp.zeros(st1_scr.shape, F32)
        m1_scr[...] = jnp.zeros(m1_scr.shape, F32)
        acc_scr[...] = jnp.zeros(acc_scr.shape, F32)
        l_scr[...] = jnp.ones(l_scr.shape, F32)

    low_half = lax.broadcasted_iota(jnp.int32, (1, LANES), 1) < HALF_LANES

    def both(cmap, st_w, m_w, st_r, m_r):
        q = q_ref[0]
        zero = jnp.zeros_like(q)
        qc = jnp.where(low_half, q, zero) if cmap == 0 else jnp.where(low_half, zero, q)
        m_prev = m_r[...]
        l = jnp.zeros((1, t), F32)
        acc = jnp.zeros((LANES, t), F32)
        m_new = None
        for j, d in enumerate(range(-1, nk - 1)):
            a1 = lax.rem(qi1 + (d + nk), nk)
            delta1 = a1 - qi1
            kb = k_ref[0, pl.ds(pl.multiple_of(a1 * t, t), t), :]
            st = lax.dot_general(kb, qc, (((1,), (1,)), ((), ())), preferred_element_type=F32)
            if d <= 1:
                st = st + bias_ref[0, jnp.clip(delta1, -2, 2) + 2]
                cm = jnp.max(st, axis=0, keepdims=True)
            else:
                cm = (jnp.max(st, axis=0, keepdims=True)
                      + cfar_ref[2 * h1 + (delta1 > 0).astype(jnp.int32)])
            st_w[j] = st
            m_new = cm if m_new is None else jnp.maximum(m_new, cm)

            a2 = lax.rem(qi2 + (d + nk), nk)
            if d <= 1:
                shifted = m_prev
            else:
                shifted = m_prev - cfar_ref[2 * h2 + (a2 > qi2).astype(jnp.int32)]
            p = jnp.exp2(st_r[j] - shifted)
            l = l + jnp.sum(p, axis=0, keepdims=True)
            acc = acc + jnp.dot(vt_ref[0, 0, a2], p.astype(BF16), preferred_element_type=F32)
        m_w[...] = m_new
        return l, acc

    @pl.when(k % 2 == 0)
    def _():
        l1, acc1 = both(0, st0_scr, m0_scr, st1_scr, m1_scr)
        ot = acc_scr[...] / l_scr[...] - lam_ref[0] * (acc1 / l1)
        o = ot.T
        ms = jnp.mean(o * o, axis=-1, keepdims=True)
        o_ref[0] = (o * lax.rsqrt(ms + EPS) * g_ref[...] * out_scale).astype(o_ref.dtype)

    @pl.when(k % 2 == 1)
    def _():
        l0, acc0 = both(1, st1_scr, m1_scr, st0_scr, m0_scr)
        acc_scr[...] = acc0
        l_scr[...] = l0


def _attn_a(proj3, vt, lam, cfar, bias5, subln_g, lam_init):
    B, S, _ = proj3.shape
    t = T_ATT
    nk = S // t
    n_blocks = B * HA * nk

    def scored(k):
        n = jnp.minimum(k // 2, n_blocks - 1)
        return n // (HA * nk), (n // nk) % HA, n % nk

    def lagged(k, lag):
        n = jnp.maximum(k - lag, 0) // 2
        return n // (HA * nk), (n // nk) % HA, n % nk

    def q_map(k):
        b, h, qi = scored(k)
        return b, qi, h

    def k_map(k):
        b, h, _ = scored(k)
        return b, 0, HA + h

    def vt_map(k):
        b, h, _ = lagged(k, 1)
        return b, h, 0, 0, 0

    def out_map(k):
        b, h, qi = lagged(k, 2)
        return b, qi, h

    kern = functools.partial(_attn_a_kernel, out_scale=1.0 - lam_init, nq=nk, n_blocks=n_blocks)
    return pl.pallas_call(
        kern,
        out_shape=jax.ShapeDtypeStruct((B, S, HA * 2 * DA), BF16),
        grid=(2 * n_blocks + 1,),
        in_specs=[
            pl.BlockSpec(memory_space=pltpu.SMEM),
            pl.BlockSpec(memory_space=pltpu.SMEM),
            pl.BlockSpec((1, t, LANES), q_map),
            pl.BlockSpec((1, S, LANES), k_map),
            pl.BlockSpec((1, 1, nk, LANES, t), vt_map),
            pl.BlockSpec((1, 5, t, t), lambda k: (scored(k)[1], 0, 0, 0)),
            pl.BlockSpec((1, LANES), lambda k: (0, 0)),
        ],
        out_specs=pl.BlockSpec((1, t, LANES), out_map),
        scratch_shapes=[pltpu.VMEM((nk, t, t), F32), pltpu.VMEM((nk, t, t), F32),
                        pltpu.VMEM((1, t), F32), pltpu.VMEM((1, t), F32),
                        pltpu.VMEM((LANES, t), F32), pltpu.VMEM((1, t), F32)],
        compiler_params=_cparams(("arbitrary",)),
        name="diff_attn",
    )(lam, cfar, proj3, proj3, vt, bias5, subln_g)


def _attn_a_bias(rel_bias):
    t = T_ATT
    tab = rel_bias[:, :HA].astype(F32) * LOG2E
    d = jnp.arange(-1, 2, dtype=jnp.int32)[:, None, None] * t
    rel = d + jnp.arange(t, dtype=jnp.int32)[None, :, None] - jnp.arange(t, dtype=jnp.int32)[None, None, :]
    near = _bias_lookup(_t5_bucket(rel), tab)
    far = tab[_t5_bucket(jnp.array([-(t + 1), t + 1], dtype=jnp.int32))].T
    fill = lambda side: jnp.broadcast_to(far[:, side, None, None, None], (HA, 1, t, t))
    tiles = jnp.concatenate([fill(0), near, fill(1)], axis=1)
    return tiles, far.reshape(2 * HA)


def _attn_b_kernel(q_ref, k_ref, v_ref, bias_ref, o_ref, lse_ref, *, sub_len, r, sp, qp):
    nblk = sub_len // QB_DIL
    low_half = lax.broadcasted_iota(jnp.int32, (1, LANES), 1) < HALF_LANES
    for si in range(sp):
        s = si if sp == r else pl.program_id(2) * sp + si
        for qb in range(qp):
            i = pl.program_id(1) * qp + qb
            start = jnp.clip(i * QB_DIL - HALF_WIN, 0, sub_len - KW_DIL)
            start = pl.multiple_of(start, HALF_WIN)
            variant = jnp.where(i == 0, 0, jnp.where(i == nblk - 1, 2, 1))
            q = q_ref[0, si, qb * QB_DIL:(qb + 1) * QB_DIL, :]
            kw = k_ref[0, s, pl.ds(start, KW_DIL), :]
            vw = v_ref[0, s, pl.ds(start, KW_DIL), :]
            rows = (slice(qb * QB_DIL, (qb + 1) * QB_DIL) if r == 1
                    else pl.ds(qb * QB_DIL * r + s, QB_DIL, stride=r))
            lane = lax.broadcasted_iota(jnp.int32, (QB_DIL, LANES), 1)
            lse_tile = jnp.zeros((QB_DIL, LANES), F32)
            for j in range(HB // 2):
                cols = slice(j * LANES, (j + 1) * LANES)
                qpair, kp, vp = q[:, cols], kw[:, cols], vw[:, cols]
                outs, lses = [], []
                for c in range(2):
                    qc = jnp.where(low_half if c == 0 else jnp.logical_not(low_half), qpair,
                                   jnp.zeros_like(qpair))
                    sc = lax.dot_general(qc, kp, (((1,), (1,)), ((), ())),
                                         preferred_element_type=F32)
                    sc = sc + bias_ref[2 * j + c, variant]
                    m = jnp.max(sc, axis=-1, keepdims=True)
                    p = jnp.exp2(sc - m)
                    l = jnp.sum(p, axis=-1, keepdims=True)
                    outs.append(jnp.dot(p.astype(BF16), vp, preferred_element_type=F32) / l)
                    lses.append((m + jnp.log2(l)) * LN2)
                o_ref[0, j, rows, :] = jnp.where(low_half, outs[0], outs[1])
                for c in range(2):
                    lse_tile = jnp.where(lane == 2 * j + c, lses[c], lse_tile)
            lse_ref[0, rows, :] = lse_tile


def _attn_b(qkv4, bias3, g, cols):
    B, r, L, _ = qkv4.shape
    S = r * L
    width = HB * DB
    nblk = L // QB_DIL
    sp = min(r, SUBS_DIL)
    qp = max(1, min(ITEMS_DIL // sp, OUT_ROWS_DIL // (QB_DIL * r)))
    qcol, kcol, vcol = cols
    kern = functools.partial(_attn_b_kernel, sub_len=L, r=r, sp=sp, qp=qp)
    slab = jax.ShapeDtypeStruct((B, N_SLABS, S, LANES), F32)
    slab_spec = pl.BlockSpec((1, N_SLABS, QB_DIL * r * qp, LANES), lambda b, i, s: (b, 0, i, 0))
    return pl.pallas_call(
        kern,
        out_shape=[slab, jax.ShapeDtypeStruct((B, S, LANES), F32)],
        grid=(B, nblk // qp, r // sp),
        in_specs=[
            pl.BlockSpec((1, sp, QB_DIL * qp, width), lambda b, i, s: (b, s, i, qcol)),
            pl.BlockSpec((1, r, L, width), lambda b, i, s: (b, 0, 0, kcol)),
            pl.BlockSpec((1, r, L, width), lambda b, i, s: (b, 0, 0, vcol)),
            pl.BlockSpec((HB, 3, QB_DIL, KW_DIL), lambda b, i, s: (0, 0, 0, 0)),
        ],
        out_specs=[slab_spec, pl.BlockSpec((1, QB_DIL * r * qp, LANES), lambda b, i, s: (b, i, 0))],
        compiler_params=_cparams(("parallel", "arbitrary", "arbitrary")),
        name=f"dilated_attn_{g}",
    )(qkv4, qkv4, qkv4, bias3)


def _attn_b_bias(rel_bias, g):
    r = DILATIONS[g]
    tab = rel_bias[:, HA + g * HB: HA + (g + 1) * HB].astype(F32) * LOG2E
    off = jnp.arange(3, dtype=jnp.int32)[:, None, None] * HALF_WIN
    rel = (jnp.arange(KW_DIL, dtype=jnp.int32)[None, None, :] - off
           - jnp.arange(QB_DIL, dtype=jnp.int32)[None, :, None])
    bias = _bias_lookup(_t5_bucket(rel * r), tab)
    return jnp.where((jnp.abs(rel) <= HALF_WIN)[None], bias, NEG)


def _sgu_kernel(zu_ref, zv_ref, lng_ref, lnb_ref, ws_ref, bs_ref, o_ref):
    u = jax.nn.gelu(zu_ref[...].astype(F32))
    v = jax.nn.gelu(zv_ref[...].astype(F32))
    mu = jnp.mean(v, axis=-1, keepdims=True)
    var = jnp.mean(jnp.square(v - mu), axis=-1, keepdims=True)
    v = ((v - mu) * lax.rsqrt(var + EPS) * lng_ref[...] + lnb_ref[...]).astype(BF16)
    gd = v.shape[1] // C_GROUPS
    for n in range(v.shape[0] // CHUNK):
        rows = slice(n * CHUNK, (n + 1) * CHUNK)
        for g in range(C_GROUPS):
            cols = slice(g * gd, (g + 1) * gd)
            mixed = jnp.dot(ws_ref[g], v[rows, cols], preferred_element_type=F32) + bs_ref[:, cols]
            o_ref[rows, cols] = (u[rows, cols] * mixed).astype(o_ref.dtype)


def _sgu(proj2, ln_g, ln_b, w_s, b_exp):
    T = proj2.shape[0]
    tm = min(TM_SGU, T)
    w = MIX_W
    return pl.pallas_call(
        _sgu_kernel,
        out_shape=jax.ShapeDtypeStruct((T, w), BF16),
        grid=(T // tm,),
        in_specs=[pl.BlockSpec((tm, w), lambda i: (i, COL_ZU)),
                  pl.BlockSpec((tm, w), lambda i: (i, COL_ZU + 1)),
                  pl.BlockSpec((1, w), lambda i: (0, 0)),
                  pl.BlockSpec((1, w), lambda i: (0, 0)),
                  pl.BlockSpec((C_GROUPS, CHUNK, CHUNK), lambda i: (0, 0, 0)),
                  pl.BlockSpec((CHUNK, w), lambda i: (0, 0))],
        out_specs=pl.BlockSpec((tm, w), lambda i: (i, 0)),
        compiler_params=_cparams(("parallel",)),
        name="sgu",
    )(proj2, proj2, ln_g, ln_b, w_s, b_exp)


def _route(logits):
    lane = lax.broadcasted_iota(jnp.int32, logits.shape, 1)
    big = jnp.int32(LANES)
    is_grp = (lane >= N_EXPERTS) & (lane < N_EXPERTS + N_GROUPS)
    gl = jnp.where(is_grp, logits, NEG)
    gmax = jnp.max(gl, axis=-1, keepdims=True)
    g_idx = jnp.min(jnp.where(is_grp & (gl == gmax), lane, big), axis=-1, keepdims=True) - N_EXPERTS
    g_w = 1.0 / jnp.sum(jnp.where(is_grp, jnp.exp(gl - gmax), 0.0), axis=-1, keepdims=True)
    in_grp = (lane >= g_idx * E_PER_GROUP) & (lane < (g_idx + 1) * E_PER_GROUP)
    sel = jnp.where(in_grp, logits, NEG)
    v1 = jnp.max(sel, axis=-1, keepdims=True)
    i1 = jnp.min(jnp.where(in_grp & (sel == v1), lane, big), axis=-1, keepdims=True)
    rest = in_grp & (lane != i1)
    sel2 = jnp.where(rest, logits, NEG)
    v2 = jnp.max(sel2, axis=-1, keepdims=True)
    i2 = jnp.min(jnp.where(rest & (sel2 == v2), lane, big), axis=-1, keepdims=True)
    e2 = jnp.exp(v2 - v1)
    w1 = g_w / (1.0 + e2)
    w2 = g_w * e2 / (1.0 + e2)
    return jnp.where(lane == i1, w1, jnp.where(lane == i2, w2, 0.0))


def _mix_kernel(x_ref, ya_ref, ob0_ref, ob1_ref, ob2_ref, ls0_ref, ls1_ref, ls2_ref, yc_ref,
                g0_ref, g1_ref, g2_ref, wb_ref, wo_ref, nf_ref, wr_ref, br_ref,
                xo_ref, h_ref, comb_ref, xn_scr):
    @pl.when(pl.program_id(0) == 0)
    def _():
        xn_scr[...] = jnp.zeros(xn_scr.shape, F32)

    h = _rms_bf16(xn_scr[...], nf_ref[...])
    h_ref[...] = h
    logits = jnp.dot(h, wr_ref[...], preferred_element_type=F32) + br_ref[...]
    comb_ref[...] = _route(logits)

    ls0, ls1, ls2 = ls0_ref[...], ls1_ref[...], ls2_ref[...]
    mx = jnp.maximum(jnp.maximum(ls0, ls1), ls2)
    es = [jnp.exp(ls0 - mx), jnp.exp(ls1 - mx), jnp.exp(ls2 - mx)]
    inv = 1.0 / (es[0] + es[1] + es[2])
    spread = jnp.where(lax.broadcasted_iota(jnp.int32, (2 * LANES, MIX_W), 1) // DB
                       == lax.broadcasted_iota(jnp.int32, (2 * LANES, MIX_W), 0) % LANES,
                       1.0, 0.0).astype(BF16)
    yb = None
    for e, ob_ref in zip(es, (ob0_ref, ob1_ref, ob2_ref)):
        w = e * inv
        hi = w.astype(BF16)
        lo = (w - hi.astype(F32)).astype(BF16)
        wide = jnp.dot(jnp.concatenate([hi, lo], axis=1), spread, preferred_element_type=F32)
        term = wide * jnp.concatenate([ob_ref[0, j] for j in range(N_SLABS)], axis=-1)
        yb = term if yb is None else yb + term
    yb = yb.astype(BF16)
    merged = jax.nn.sigmoid(g0_ref[...].astype(F32)) * jnp.dot(ya_ref[...], wb_ref[0],
                                                               preferred_element_type=F32)
    merged += jax.nn.sigmoid(g1_ref[...].astype(F32)) * jnp.dot(yb, wb_ref[1],
                                                                preferred_element_type=F32)
    merged += jax.nn.sigmoid(g2_ref[...].astype(F32)) * jnp.dot(yc_ref[...], wb_ref[2],
                                                                preferred_element_type=F32)
    xn = x_ref[...] + jnp.dot(merged.astype(BF16), wo_ref[...], preferred_element_type=F32)
    xo_ref[...] = xn
    xn_scr[...] = xn


def _mix(x2, ya, obs, lses, yc, proj2, wb, wo, nf, wr, br):
    T, D = x2.shape
    S = obs[0].shape[2]
    tm = min(TM_MIX, S)
    per_b = S // tm
    n = T // tm
    w = MIX_W
    cur = lambda i: jnp.minimum(i, n - 1)
    lag = lambda i: jnp.maximum(i - 1, 0)
    row = lambda width: pl.BlockSpec((tm, width), lambda i: (cur(i), 0))
    full = lambda a: pl.BlockSpec(a.shape, lambda i: (0,) * a.ndim)
    gate = lambda k: pl.BlockSpec((tm, D), lambda i: (cur(i), COL_GATE + k))
    slab = pl.BlockSpec((1, N_SLABS, tm, LANES),
                        lambda i: (cur(i) // per_b, 0, cur(i) % per_b, 0))
    late = lambda width: pl.BlockSpec((tm, width), lambda i: (lag(i), 0))
    return pl.pallas_call(
        _mix_kernel,
        out_shape=[jax.ShapeDtypeStruct((T, D), F32), jax.ShapeDtypeStruct((T, D), BF16),
                   jax.ShapeDtypeStruct((T, LANES), F32)],
        grid=(n + 1,),
        in_specs=[row(D), row(w), slab, slab, slab, row(LANES), row(LANES), row(LANES), row(w),
                  gate(0), gate(1), gate(2), full(wb), full(wo), full(nf), full(wr), full(br)],
        out_specs=[row(D), late(D), late(LANES)],
        scratch_shapes=[pltpu.VMEM((tm, D), F32)],
        compiler_params=_cparams(("arbitrary",)),
        name="mix",
    )(x2, ya, obs[0], obs[1], obs[2], lses[0], lses[1], lses[2], yc, proj2, proj2, proj2,
      wb, wo, nf, wr, br)


def _moe_kernel(h_ref, comb_ref, x_ref, wg_ref, wu_ref, wd_ref, nfin_ref, o_ref, acc_scr,
                *, final_norm):
    e = pl.program_id(1)

    @pl.when(e == 0)
    def _():
        acc_scr[...] = jnp.zeros(acc_scr.shape, F32)

    h = h_ref[...]
    lane = lax.broadcasted_iota(jnp.int32, comb_ref.shape, 1)
    c = jnp.sum(jnp.where(lane == e, comb_ref[...], 0.0), axis=-1, keepdims=True)
    hid = (jax.nn.silu(jnp.dot(h, wg_ref[0].astype(BF16), preferred_element_type=F32))
           * jnp.dot(h, wu_ref[0].astype(BF16), preferred_element_type=F32))
    acc_scr[...] += c * jnp.dot(hid.astype(BF16), wd_ref[0].astype(BF16),
                                preferred_element_type=F32)

    @pl.when(e == pl.num_programs(1) - 1)
    def _():
        xn = x_ref[...] + acc_scr[...]
        if final_norm:
            ms = jnp.mean(xn * xn, axis=-1, keepdims=True)
            xn = xn * lax.rsqrt(ms + EPS) * nfin_ref[...]
        o_ref[...] = xn


def _moe_dense(h, comb, x2, wg, wu, wd, e0, nfin, final_norm):
    T, D = x2.shape
    tm = min(TM_MOE, T)
    F = wg.shape[2]
    kern = functools.partial(_moe_kernel, final_norm=final_norm)
    return pl.pallas_call(
        kern,
        out_shape=jax.ShapeDtypeStruct((T, D), F32),
        grid=(T // tm, N_EXPERTS),
        in_specs=[pl.BlockSpec((tm, D), lambda i, e: (i, 0)),
                  pl.BlockSpec((tm, LANES), lambda i, e: (i, 0)),
                  pl.BlockSpec((tm, D), lambda i, e: (i, 0)),
                  pl.BlockSpec((1, D, F), lambda i, e: (e0 + e, 0, 0)),
                  pl.BlockSpec((1, D, F), lambda i, e: (e0 + e, 0, 0)),
                  pl.BlockSpec((1, F, D), lambda i, e: (e0 + e, 0, 0)),
                  pl.BlockSpec((1, D), lambda i, e: (0, 0))],
        out_specs=pl.BlockSpec((tm, D), lambda i, e: (i, 0)),
        scratch_shapes=[pltpu.VMEM((tm, D), F32)],
        compiler_params=_cparams(("parallel", "arbitrary")),
        name="moe_dense",
    )(h, comb, x2, wg, wu, wd, nfin)


def _moe_dispatch_kernel(h_ref, comb_ref, o_ref, cnt_ref):
    nt = o_ref.shape[0]
    tm = h_ref.shape[0] // nt
    before = jnp.where(lax.broadcasted_iota(jnp.int32, (tm, tm), 0)
                       < lax.broadcasted_iota(jnp.int32, (tm, tm), 1), 1.0, 0.0).astype(BF16)
    slot = lax.broadcasted_iota(jnp.int32, (MOE_CAP, tm), 0).astype(F32)
    for u in range(nt):
        rows = slice(u * tm, (u + 1) * tm)
        comb = comb_ref[rows, :]
        hi = comb.astype(BF16)
        lo = (comb - hi.astype(F32)).astype(BF16)
        haug = jnp.concatenate([h_ref[rows, :], hi, lo], axis=1)
        a_t = comb.T[:N_EXPERTS] > 0.0
        a_f = jnp.where(a_t, 1.0, 0.0)
        rank_t = jnp.dot(a_f.astype(BF16), before, preferred_element_type=F32)
        blocks = [jnp.where((slot == rank_t[e:e + 1]) & a_t[e:e + 1], 1.0, 0.0).astype(BF16)
                  for e in range(N_EXPERTS)]
        res = jnp.dot(jnp.concatenate(blocks, axis=0), haug, preferred_element_type=F32)
        res = res.astype(o_ref.dtype)
        for e in range(N_EXPERTS):
            o_ref[u, e] = res[e * MOE_CAP:(e + 1) * MOE_CAP]
        cnt_ref[u] = jnp.broadcast_to(jnp.sum(a_f, axis=1, keepdims=True), cnt_ref.shape[1:])


def _moe_dispatch(h, comb):
    T, D = h.shape
    tm = min(TM_DISP, T)
    n = T // tm
    nt = math.gcd(TILES_PER_STEP, n)
    return pl.pallas_call(
        _moe_dispatch_kernel,
        out_shape=[jax.ShapeDtypeStruct((n, N_EXPERTS, MOE_CAP, D + 2 * LANES), BF16),
                   jax.ShapeDtypeStruct((n, N_EXPERTS, LANES), F32)],
        grid=(n // nt,),
        in_specs=[pl.BlockSpec((nt * tm, D), lambda i: (i, 0)),
                  pl.BlockSpec((nt * tm, LANES), lambda i: (i, 0))],
        out_specs=[pl.BlockSpec((nt, N_EXPERTS, MOE_CAP, D + 2 * LANES), lambda i: (i, 0, 0, 0)),
                   pl.BlockSpec((nt, N_EXPERTS, LANES), lambda i: (i, 0, 0))],
        compiler_params=_cparams(("parallel",)),
        name="moe_dispatch",
    )(h, comb)


def _moe_ffn_kernel(n16_ref, s_ref, wg_ref, wu_ref, wd_ref, o_ref,
                    wg_scr, wu_scr, wd_scr, lhs_scr, y_scr):
    e, c = pl.program_id(0), pl.program_id(1)

    @pl.when(c == 0)
    def _():
        wg_scr[...] = wg_ref[0].astype(BF16)
        wu_scr[...] = wu_ref[0].astype(BF16)
        wd_scr[...] = wd_ref[0].astype(BF16)

    g, _, cap, _ = s_ref.shape
    D = o_ref.shape[-1]
    @pl.when((e == 0) & (c == 0))
    def _():
        lhs_scr[...] = jnp.zeros(lhs_scr.shape, lhs_scr.dtype)
        y_scr[...] = jnp.zeros(y_scr.shape, y_scr.dtype)

    offs = []
    off = jnp.int32(0)
    for t in range(g):
        offs.append(off)
        lhs_scr[pl.ds(pl.multiple_of(off, BF16_ROWS), cap), :] = s_ref[t, 0]
        off = off + n16_ref[(c * g + t) * N_EXPERTS + e]
    total = off

    def run(nrows):
        rows = lhs_scr[:nrows]
        h = rows[:, :D]
        wparts = rows[:, D:].astype(F32)
        lane = lax.broadcasted_iota(jnp.int32, wparts.shape, 1)
        w = jnp.sum(jnp.where(lane % LANES == e, wparts, 0.0), axis=-1, keepdims=True)
        hid = (jax.nn.silu(jnp.dot(h, wg_scr[...], preferred_element_type=F32))
               * jnp.dot(h, wu_scr[...], preferred_element_type=F32))
        y = w * jnp.dot(hid.astype(BF16), wd_scr[...], preferred_element_type=F32)
        y_scr[:nrows] = y.astype(y_scr.dtype)

    classes = tuple(range(g * cap // 2, g * cap + 1, FFN_ROW_STEP))
    lower = 0
    for nrows in classes:
        pl.when((total > lower) & (total <= nrows))(functools.partial(run, nrows))
        lower = nrows

    for t in range(g):
        o_ref[t, 0] = y_scr[pl.ds(pl.multiple_of(offs[t], BF16_ROWS), cap), :]


def _moe_ffn(srt, n16, wg, wu, wd, e0):
    n, ne, cap, wdt = srt.shape
    D, F = wg.shape[1], wg.shape[2]
    g = math.gcd(G_FFN, n)
    return pl.pallas_call(
        _moe_ffn_kernel,
        out_shape=jax.ShapeDtypeStruct((n, ne, cap, D), BF16),
        grid_spec=pltpu.PrefetchScalarGridSpec(
            num_scalar_prefetch=1,
            grid=(ne, n // g),
            in_specs=[pl.BlockSpec((g, 1, cap, wdt), lambda e, c, n16: (c, e, 0, 0)),
                      pl.BlockSpec((1, D, F), lambda e, c, n16: (e0 + e, 0, 0)),
                      pl.BlockSpec((1, D, F), lambda e, c, n16: (e0 + e, 0, 0)),
                      pl.BlockSpec((1, F, D), lambda e, c, n16: (e0 + e, 0, 0))],
            out_specs=pl.BlockSpec((g, 1, cap, D), lambda e, c, n16: (c, e, 0, 0)),
            scratch_shapes=[pltpu.VMEM((D, F), BF16), pltpu.VMEM((D, F), BF16),
                            pltpu.VMEM((F, D), BF16), pltpu.VMEM((g * cap, wdt), BF16),
                            pltpu.VMEM((g * cap, D), BF16)]),
        compiler_params=_cparams(("arbitrary", "arbitrary")),
        name="moe_ffn",
    )(n16, srt, wg, wu, wd)


def _moe_combine_kernel(skip_ref, y_ref, comb_ref, x_ref, nfin_ref, o_ref, *, final_norm):
    nt = y_ref.shape[0]
    tm = x_ref.shape[0] // nt
    ncol = N_EXPERTS * MOE_CAP
    before = jnp.where(lax.broadcasted_iota(jnp.int32, (tm, tm), 1)
                       < lax.broadcasted_iota(jnp.int32, (tm, tm), 0), 1.0, 0.0).astype(BF16)
    spread = jnp.where(lax.broadcasted_iota(jnp.int32, (LANES, ncol), 1) // MOE_CAP
                       == lax.broadcasted_iota(jnp.int32, (LANES, ncol), 0), 1.0, 0.0).astype(BF16)
    slot = (lax.broadcasted_iota(jnp.int32, (tm, ncol), 1) % MOE_CAP).astype(F32)
    for u in range(nt):
        rows = slice(u * tm, (u + 1) * tm)
        a = comb_ref[rows, :] > 0.0
        rank = jnp.dot(before, jnp.where(a, 1.0, 0.0).astype(BF16), preferred_element_type=F32)
        key = jnp.where(a, rank, -1.0).astype(BF16)
        key_all = jnp.dot(key, spread, preferred_element_type=F32)
        pc = jnp.where(slot == key_all, 1.0, 0.0).astype(BF16)
        y = jnp.concatenate([y_ref[u, e] for e in range(N_EXPERTS)], axis=0)
        xn = x_ref[rows, :] + jnp.dot(pc, y, preferred_element_type=F32)
        if final_norm:
            ms = jnp.mean(xn * xn, axis=-1, keepdims=True)
            later = skip_ref[pl.program_id(0) * nt + u] == 1
            xn = jnp.where(later, xn, xn * lax.rsqrt(ms + EPS) * nfin_ref[...])
        o_ref[rows, :] = xn


def _moe_combine(skip, y, comb, x2, nfin, final_norm):
    T, D = x2.shape
    n, ne, cap, _ = y.shape
    tm = T // n
    nt = math.gcd(TILES_PER_STEP, n)
    kern = functools.partial(_moe_combine_kernel, final_norm=final_norm)
    return pl.pallas_call(
        kern,
        out_shape=jax.ShapeDtypeStruct((T, D), F32),
        grid_spec=pltpu.PrefetchScalarGridSpec(
            num_scalar_prefetch=1,
            grid=(n // nt,),
            in_specs=[pl.BlockSpec((nt, ne, cap, D), lambda i, sk: (i, 0, 0, 0)),
                      pl.BlockSpec((nt * tm, LANES), lambda i, sk: (i, 0)),
                      pl.BlockSpec((nt * tm, D), lambda i, sk: (i, 0)),
                      pl.BlockSpec((1, D), lambda i, sk: (0, 0))],
            out_specs=pl.BlockSpec((nt * tm, D), lambda i, sk: (i, 0))),
        compiler_params=_cparams(("parallel",)),
        name="moe_combine",
    )(skip, y, comb, x2, nfin)


def _moe_fix_kernel(tiles_ref, experts_ref, first_ref, last_ref, n_ref, h_ref, comb_ref, prev_ref,
                    wg_ref, wu_ref, wd_ref, nfin_ref, o_ref, *, final_norm):
    del tiles_ref
    s = pl.program_id(0)

    @pl.when(s < n_ref[0])
    def _():
        e = experts_ref[s]
        tm = h_ref.shape[0]
        comb = comb_ref[...]
        a = jnp.where(comb > 0.0, 1.0, 0.0)
        before = (lax.broadcasted_iota(jnp.int32, (tm, tm), 1)
                  < lax.broadcasted_iota(jnp.int32, (tm, tm), 0))
        rank = jnp.dot(jnp.where(before, 1.0, 0.0).astype(BF16), a.astype(BF16),
                       preferred_element_type=F32)
        lane = lax.broadcasted_iota(jnp.int32, comb.shape, 1)
        dropped = (lane == e) & (rank >= MOE_CAP)
        c = jnp.sum(jnp.where(dropped, comb, 0.0), axis=-1, keepdims=True)
        h = h_ref[...]
        hid = (jax.nn.silu(jnp.dot(h, wg_ref[0].astype(BF16), preferred_element_type=F32))
               * jnp.dot(h, wu_ref[0].astype(BF16), preferred_element_type=F32))
        add = c * jnp.dot(hid.astype(BF16), wd_ref[0].astype(BF16), preferred_element_type=F32)
        fresh = first_ref[s] == 1

        @pl.when(fresh)
        def _():
            o_ref[...] = prev_ref[...] + add

        @pl.when(jnp.logical_not(fresh))
        def _():
            o_ref[...] += add

        if final_norm:
            @pl.when(last_ref[s] == 1)
            def _():
                xn = o_ref[...]
                ms = jnp.mean(xn * xn, axis=-1, keepdims=True)
                o_ref[...] = xn * lax.rsqrt(ms + EPS) * nfin_ref[...]


def _moe_fix(tiles, experts, first, last, n, out, h, comb, wg, wu, wd, e0, nfin, final_norm):
    T, D = out.shape
    tm = min(TM_DISP, T)
    F = wg.shape[2]
    tile = lambda width: pl.BlockSpec((tm, width), lambda s, tl, ex, fi, la, n: (tl[s], 0))
    wspec = lambda shape: pl.BlockSpec(shape, lambda s, tl, ex, fi, la, n: (e0 + ex[s], 0, 0))
    kern = functools.partial(_moe_fix_kernel, final_norm=final_norm)
    return pl.pallas_call(
        kern,
        out_shape=jax.ShapeDtypeStruct((T, D), F32),
        grid_spec=pltpu.PrefetchScalarGridSpec(
            num_scalar_prefetch=5,
            grid=(MAX_OVF,),
            in_specs=[tile(D), tile(LANES), tile(D), wspec((1, D, F)), wspec((1, D, F)),
                      wspec((1, F, D)), pl.BlockSpec((1, D), lambda s, tl, ex, fi, la, n: (0, 0))],
            out_specs=tile(D)),
        input_output_aliases={7: 0},
        compiler_params=_cparams(("arbitrary",)),
        name="moe_fix",
    )(tiles, experts, first, last, n, h, comb, out, wg, wu, wd, nfin)


def _moe(h, comb, x2, wg, wu, wd, e0, nfin, final_norm):
    srt, cnt = _moe_dispatch(h, comb)
    over = (cnt[:, :, 0] > MOE_CAP).reshape(-1)
    n_ovf = jnp.sum(over.astype(jnp.int32))
    pairs = jnp.nonzero(over, size=MAX_OVF, fill_value=0)[0].astype(jnp.int32)
    pairs = jnp.where(jnp.arange(MAX_OVF) < n_ovf, pairs, pairs[jnp.clip(n_ovf - 1, 0, MAX_OVF - 1)])
    tiles, experts = pairs // N_EXPERTS, pairs % N_EXPERTS
    change = (tiles[1:] != tiles[:-1]).astype(jnp.int32)
    first = jnp.concatenate([jnp.ones((1,), jnp.int32), change])
    last = jnp.maximum(jnp.concatenate([change, jnp.ones((1,), jnp.int32)]),
                       (jnp.arange(MAX_OVF) == n_ovf - 1).astype(jnp.int32))
    skip = jnp.any(over.reshape(-1, N_EXPERTS), axis=1).astype(jnp.int32)

    used = jnp.minimum(cnt[:, :, 0], MOE_CAP).astype(jnp.int32).reshape(-1)
    n16 = (used + (BF16_ROWS - 1)) // BF16_ROWS * BF16_ROWS

    def routed():
        out = _moe_combine(skip, _moe_ffn(srt, n16, wg, wu, wd, e0), comb, x2, nfin, final_norm)
        return lax.cond(
            n_ovf > 0,
            lambda: _moe_fix(tiles, experts, first, last, n_ovf.reshape(1), out, h, comb,
                             wg, wu, wd, e0, nfin, final_norm),
            lambda: out)

    return lax.cond(n_ovf > MAX_OVF,
                    lambda: _moe_dense(h, comb, x2, wg, wu, wd, e0, nfin, final_norm), routed)


def kernel(x, rel_bias, norm_mix, w_in, diff_lambda, diff_subln, sgu_ln_g, sgu_ln_b, sgu_w, sgu_b,
           w_branch, w_out, norm_ffn, w_router_grp, b_router_grp, w_router_exp, b_router_exp,
           w_gate, w_up, w_down, norm_final):
    B, S, D = x.shape
    T = B * S
    depth = w_in.shape[0]
    a_out = HA * 2 * DA
    grp_w = HB * DB
    b_cols = 3 * NG_B * grp_w
    qkv_b0 = 3 * a_out
    zc0 = qkv_b0 + b_cols
    gate0 = zc0 + 2 * MIX_W
    qk_scale = DA ** -0.5

    bias_a, cfar = _attn_a_bias(rel_bias)
    bias_b = [_attn_b_bias(rel_bias, g) for g in range(NG_B)]

    col = jnp.arange(w_in.shape[2])
    is_q = (col < a_out) | ((col >= qkv_b0) & (col < qkv_b0 + NG_B * grp_w))
    col_scale = jnp.where(is_q, qk_scale * LOG2E, 1.0)

    def group_cols(w, g):
        return [w[:, qkv_b0 + (c * NG_B + g) * grp_w: qkv_b0 + (c * NG_B + g + 1) * grp_w]
                for c in range(3)]

    wg_all = w_gate.reshape((-1,) + w_gate.shape[2:])
    wu_all = w_up.reshape((-1,) + w_up.shape[2:])
    wd_all = w_down.reshape((-1,) + w_down.shape[2:])

    x2 = x.reshape(T, D)
    for i in range(depth):
        w = (w_in[i] * col_scale.astype(F32)).astype(BF16)
        nm = norm_mix[i][None, :]
        w_main = jnp.concatenate([w[:, :2 * a_out], w[:, zc0:]] + group_cols(w, 0), axis=1)
        x3 = x2.reshape(B, S, D)
        proj3, vt = _inproj(x3, nm, w_main, w[:, 2 * a_out:3 * a_out].T)
        proj2 = proj3.reshape(T, proj3.shape[2])

        lam_init = 0.8 - 0.6 * math.exp(-0.3 * i)
        lp = diff_lambda[i].astype(F32)
        lam = jnp.exp(jnp.sum(lp[0] * lp[1])) - jnp.exp(jnp.sum(lp[2] * lp[3])) + lam_init
        ya = _attn_a(proj3, vt, lam.reshape(1), cfar, bias_a, diff_subln[i][None, :], lam_init)

        strided = [g for g in range(NG_B) if DILATIONS[g] > 1]
        permuted = dict(zip(strided, _inproj_perm(
            x3, nm, [jnp.concatenate(group_cols(w, g), axis=1) for g in strided],
            [DILATIONS[g] for g in strided])))
        obs, lses = [], []
        for g in range(NG_B):
            if g in permuted:
                qkv4, cols = permuted[g], (0, 1, 2)
            else:
                qkv4, cols = proj3[:, None], (COL_QKV0, COL_QKV0 + 1, COL_QKV0 + 2)
            o, l = _attn_b(qkv4, bias_b[g], g, cols)
            obs.append(o)
            lses.append(l.reshape(T, LANES))

        b_exp = jnp.repeat(sgu_b[i].T, MIX_W // C_GROUPS, axis=1)
        yc = _sgu(proj2, sgu_ln_g[i][None, :], sgu_ln_b[i][None, :], sgu_w[i].astype(BF16), b_exp)

        wr = jnp.concatenate([w_router_exp[i].transpose(1, 0, 2).reshape(D, N_EXPERTS),
                              w_router_grp[i]], axis=1)
        wr = jnp.pad(wr, ((0, 0), (0, LANES - wr.shape[1]))).astype(BF16)
        br = jnp.concatenate([b_router_exp[i].reshape(N_EXPERTS), b_router_grp[i]])
        br = jnp.pad(br, (0, LANES - br.shape[0]))[None, :].astype(F32)

        x2, h, comb = _mix(x2, ya.reshape(T, a_out), obs, lses, yc, proj2,
                           w_branch[i].astype(BF16), w_out[i].astype(BF16), norm_ffn[i][None, :],
                           wr, br)
        x2 = _moe(h, comb, x2, wg_all, wu_all, wd_all, i * N_EXPERTS, norm_final[None, :],
                  i == depth - 1)
    return x2.reshape(B, S, D)
```

```python
import functools
import math

import jax
import jax.numpy as jnp
from jax import lax
from jax.experimental import pallas as pl
from jax.experimental.pallas import tpu as pltpu

F32 = jnp.float32
BF16 = jnp.bfloat16

EPS = 1e-6
NEG = -1e30
LOG2E = 1.4426950408889634
LN2 = 0.6931471805599453
LANES = 128
HALF_LANES = LANES // 2
VMEM_LIMIT = 48 * 1024 * 1024

HA = 4
DA = 64
MIX_W = 512
WINDOWS = (128, 512, 2048)
DILATIONS = (1, 4, 16)
NG_B = 3
HB = 8
DB = 64
HALF_WIN = 64
CHUNK = 128
C_GROUPS = 4
N_BRANCH = 3
N_BUCKETS = 32
MAX_DIST = 128
N_GROUPS = 4
E_PER_GROUP = 4
N_EXPERTS = N_GROUPS * E_PER_GROUP
N_SLABS = MIX_W // LANES

TM_PROJ = 1024
TN_PROJ = 3328
TM_PERM = 1024
PERM_BLK = 256
T_ATT = 512
QB_DIL = 128
KW_DIL = QB_DIL + 2 * HALF_WIN
ITEMS_DIL = 8
SUBS_DIL = 4
OUT_ROWS_DIL = 2048
TM_SGU = 2048
TM_MIX = 512
TM_MOE = 1024
TM_DISP = 256
MOE_CAP = HALF_LANES
G_FFN = 16
TILES_PER_STEP = 4
FFN_ROW_STEP = 64
BF16_ROWS = 16
MAX_OVF = 64

COL_ZU = 2
COL_GATE = 2
COL_QKV0 = 10


def _cparams(sem):
    return pltpu.CompilerParams(dimension_semantics=sem, vmem_limit_bytes=VMEM_LIMIT)


def _t5_bucket(rel):
    nb = N_BUCKETS // 2
    max_exact = nb // 2
    ret = (rel > 0).astype(jnp.int32) * nb
    n = jnp.abs(rel)
    nf = jnp.maximum(n, 1).astype(F32)
    large = max_exact + (jnp.log(nf / max_exact) / math.log(MAX_DIST / max_exact)
                         * (nb - max_exact)).astype(jnp.int32)
    large = jnp.minimum(large, nb - 1)
    return ret + jnp.where(n < max_exact, n, large)


def _bias_lookup(bucket, tab):
    out = jnp.zeros((tab.shape[1],) + bucket.shape, F32)
    expand = (slice(None),) + (None,) * bucket.ndim
    for b in range(N_BUCKETS):
        out = jnp.where(bucket[None] == b, tab[b][expand], out)
    return out


def _rms_bf16(x, g):
    ms = jnp.mean(x * x, axis=-1, keepdims=True)
    return (x * lax.rsqrt(ms + EPS) * g).astype(BF16)


def _inproj_kernel(x_ref, g_ref, w_ref, wt_ref, o_ref, vt_ref, h_scr):
    @pl.when(pl.program_id(2) == 0)
    def _():
        h = _rms_bf16(x_ref[0], g_ref[...])
        h_scr[...] = h
        res = lax.dot_general(wt_ref[...], h, (((1,), (1,)), ((), ())),
                              preferred_element_type=F32).astype(vt_ref.dtype)
        for hd in range(vt_ref.shape[1]):
            for n in range(vt_ref.shape[2]):
                vt_ref[0, hd, n] = res[hd * LANES:(hd + 1) * LANES, n * T_ATT:(n + 1) * T_ATT]

    o_ref[0] = jnp.dot(h_scr[...], w_ref[...], preferred_element_type=F32).astype(o_ref.dtype)


def _inproj(x3, g, w, wt):
    B, S, D = x3.shape
    N = w.shape[1]
    tm = min(TM_PROJ, S)
    nh, nb = wt.shape[0] // LANES, tm // T_ATT
    return pl.pallas_call(
        _inproj_kernel,
        out_shape=[jax.ShapeDtypeStruct((B, S, N), BF16),
                   jax.ShapeDtypeStruct((B, nh, S // T_ATT, LANES, T_ATT), BF16)],
        grid=(B, S // tm, N // TN_PROJ),
        in_specs=[pl.BlockSpec((1, tm, D), lambda b, i, j: (b, i, 0)),
                  pl.BlockSpec((1, D), lambda b, i, j: (0, 0)),
                  pl.BlockSpec((D, TN_PROJ), lambda b, i, j: (0, j)),
                  pl.BlockSpec(wt.shape, lambda b, i, j: (0, 0))],
        out_specs=[pl.BlockSpec((1, tm, TN_PROJ), lambda b, i, j: (b, i, j)),
                   pl.BlockSpec((1, nh, nb, LANES, T_ATT), lambda b, i, j: (b, 0, i, 0, 0))],
        scratch_shapes=[pltpu.VMEM((tm, D), BF16)],
        compiler_params=_cparams(("parallel", "parallel", "arbitrary")),
        name="inproj",
    )(x3, g, w, wt)


def _inproj_perm_kernel(x_ref, g_ref, *refs, dilations):
    ng = len(dilations)
    p_refs, w_refs, o_refs = refs[:ng], refs[ng:2 * ng], refs[2 * ng:]
    h = _rms_bf16(x_ref[0], g_ref[...])
    nblk = h.shape[0] // PERM_BLK
    for r, p_ref, w_ref, o_ref in zip(dilations, p_refs, w_refs, o_refs):
        hp = jnp.concatenate(
            [jnp.dot(p_ref[...], h[k * PERM_BLK:(k + 1) * PERM_BLK], preferred_element_type=F32)
             for k in range(nblk)], axis=0).astype(BF16)
        res = jnp.dot(hp, w_ref[...], preferred_element_type=F32).astype(o_ref.dtype)
        n = PERM_BLK // r
        for k in range(nblk):
            for s in range(r):
                o_ref[0, s, k * n:(k + 1) * n, :] = res[k * PERM_BLK + s * n:k * PERM_BLK + (s + 1) * n, :]


def _inproj_perm(x3, g, ws, dilations):
    B, S, D = x3.shape
    tm = min(TM_PERM, S)
    perms = []
    for r in dilations:
        n = PERM_BLK // r
        o = jnp.arange(PERM_BLK, dtype=jnp.int32)
        src = (o % n) * r + o // n
        perms.append((src[:, None] == jnp.arange(PERM_BLK, dtype=jnp.int32)[None, :]).astype(BF16))
    kern = functools.partial(_inproj_perm_kernel, dilations=tuple(dilations))
    return pl.pallas_call(
        kern,
        out_shape=[jax.ShapeDtypeStruct((B, r, S // r, w.shape[1]), BF16)
                   for r, w in zip(dilations, ws)],
        grid=(B, S // tm),
        in_specs=([pl.BlockSpec((1, tm, D), lambda b, i: (b, i, 0)),
                   pl.BlockSpec((1, D), lambda b, i: (0, 0))]
                  + [pl.BlockSpec((PERM_BLK, PERM_BLK), lambda b, i: (0, 0)) for _ in dilations]
                  + [pl.BlockSpec(w.shape, lambda b, i: (0, 0)) for w in ws]),
        out_specs=[pl.BlockSpec((1, r, tm // r, w.shape[1]), lambda b, i: (b, 0, i, 0))
                   for r, w in zip(dilations, ws)],
        compiler_params=_cparams(("parallel", "parallel")),
        name="inproj_perm",
    )(x3, g, *perms, *ws)


def _attn_a_kernel(lam_ref, cfar_ref, q_ref, k_ref, vt_ref, bias_ref, g_ref, o_ref,
                   st0_scr, st1_scr, m0_scr, m1_scr, acc_scr, l_scr, *, out_scale, nq, n_blocks):
    k = pl.program_id(0)
    t = T_ATT
    nk = k_ref.shape[1] // t
    n1 = jnp.minimum(k // 2, n_blocks - 1)
    n2 = jnp.maximum(k - 1, 0) // 2
    h1, qi1 = (n1 // nq) % HA, n1 % nq
    h2, qi2 = (n2 // nq) % HA, n2 % nq

    @pl.when(k == 0)
    def _():
        st1_scr[...] = jnp.zeros(st1_scr.shape, F32)
        m1_scr[...] = jnp.zeros(m1_scr.shape, F32)
        acc_scr[...] = jnp.zeros(acc_scr.shape, F32)
        l_scr[...] = jnp.ones(l_scr.shape, F32)

    low_half = lax.broadcasted_iota(jnp.int32, (1, LANES), 1) < HALF_LANES

    def both(cmap, st_w, m_w, st_r, m_r):
        q = q_ref[0]
        zero = jnp.zeros_like(q)
        qc = jnp.where(low_half, q, zero) if cmap == 0 else jnp.where(low_half, zero, q)
        m_prev = m_r[...]
        l = jnp.zeros((1, t), F32)
        acc = jnp.zeros((LANES, t), F32)
        m_new = None
        for j, d in enumerate(range(-1, nk - 1)):
            a2 = lax.rem(qi2 + (d + nk), nk)
            if d <= 1:
                shifted = m_prev
            else:
                shifted = m_prev - cfar_ref[2 * h2 + (a2 > qi2).astype(jnp.int32)]
            p = jnp.exp2(st_r[j] - shifted)
            l = l + jnp.sum(p, axis=0, keepdims=True)
            acc = acc + jnp.dot(vt_ref[0, 0, a2], p.astype(BF16), preferred_element_type=F32)

            a1 = lax.rem(qi1 + (d + nk), nk)
            delta1 = a1 - qi1
            kb = k_ref[0, pl.ds(pl.multiple_of(a1 * t, t), t), :]
            st = lax.dot_general(kb, qc, (((1,), (1,)), ((), ())), preferred_element_type=F32)
            if d <= 1:
                st = st + bias_ref[0, jnp.clip(delta1, -2, 2) + 2]
                cm = jnp.max(st, axis=0, keepdims=True)
            else:
                cm = (jnp.max(st, axis=0, keepdims=True)
                      + cfar_ref[2 * h1 + (delta1 > 0).astype(jnp.int32)])
            st_w[j] = st
            m_new = cm if m_new is None else jnp.maximum(m_new, cm)
        m_w[...] = m_new
        return l, acc

    @pl.when(k % 2 == 0)
    def _():
        l1, acc1 = both(0, st0_scr, m0_scr, st1_scr, m1_scr)
        ot = acc_scr[...] / l_scr[...] - lam_ref[0] * (acc1 / l1)
        o = ot.T
        ms = jnp.mean(o * o, axis=-1, keepdims=True)
        o_ref[0] = (o * lax.rsqrt(ms + EPS) * g_ref[...] * out_scale).astype(o_ref.dtype)

    @pl.when(k % 2 == 1)
    def _():
        l0, acc0 = both(1, st1_scr, m1_scr, st0_scr, m0_scr)
        acc_scr[...] = acc0
        l_scr[...] = l0


def _attn_a(proj3, vt, lam, cfar, bias5, subln_g, lam_init):
    B, S, _ = proj3.shape
    t = T_ATT
    nk = S // t
    n_blocks = B * HA * nk

    def scored(k):
        n = jnp.minimum(k // 2, n_blocks - 1)
        return n // (HA * nk), (n // nk) % HA, n % nk

    def lagged(k, lag):
        n = jnp.maximum(k - lag, 0) // 2
        return n // (HA * nk), (n // nk) % HA, n % nk

    def q_map(k):
        b, h, qi = scored(k)
        return b, qi, h

    def k_map(k):
        b, h, _ = scored(k)
        return b, 0, HA + h

    def vt_map(k):
        b, h, _ = lagged(k, 1)
        return b, h, 0, 0, 0

    def out_map(k):
        b, h, qi = lagged(k, 2)
        return b, qi, h

    kern = functools.partial(_attn_a_kernel, out_scale=1.0 - lam_init, nq=nk, n_blocks=n_blocks)
    return pl.pallas_call(
        kern,
        out_shape=jax.ShapeDtypeStruct((B, S, HA * 2 * DA), BF16),
        grid=(2 * n_blocks + 1,),
        in_specs=[
            pl.BlockSpec(memory_space=pltpu.SMEM),
            pl.BlockSpec(memory_space=pltpu.SMEM),
            pl.BlockSpec((1, t, LANES), q_map),
            pl.BlockSpec((1, S, LANES), k_map),
            pl.BlockSpec((1, 1, nk, LANES, t), vt_map),
            pl.BlockSpec((1, 5, t, t), lambda k: (scored(k)[1], 0, 0, 0)),
            pl.BlockSpec((1, LANES), lambda k: (0, 0)),
        ],
        out_specs=pl.BlockSpec((1, t, LANES), out_map),
        scratch_shapes=[pltpu.VMEM((nk, t, t), F32), pltpu.VMEM((nk, t, t), F32),
                        pltpu.VMEM((1, t), F32), pltpu.VMEM((1, t), F32),
                        pltpu.VMEM((LANES, t), F32), pltpu.VMEM((1, t), F32)],
        compiler_params=_cparams(("arbitrary",)),
        name="diff_attn",
    )(lam, cfar, proj3, proj3, vt, bias5, subln_g)


def _attn_a_bias(rel_bias):
    t = T_ATT
    tab = rel_bias[:, :HA].astype(F32) * LOG2E
    d = jnp.arange(-1, 2, dtype=jnp.int32)[:, None, None] * t
    rel = d + jnp.arange(t, dtype=jnp.int32)[None, :, None] - jnp.arange(t, dtype=jnp.int32)[None, None, :]
    near = _bias_lookup(_t5_bucket(rel), tab)
    far = tab[_t5_bucket(jnp.array([-(t + 1), t + 1], dtype=jnp.int32))].T
    fill = lambda side: jnp.broadcast_to(far[:, side, None, None, None], (HA, 1, t, t))
    tiles = jnp.concatenate([fill(0), near, fill(1)], axis=1)
    return tiles, far.reshape(2 * HA)


def _attn_b_kernel(q_ref, k_ref, v_ref, bias_ref, o_ref, lse_ref, *, sub_len, r, sp, qp):
    nblk = sub_len // QB_DIL
    low_half = lax.broadcasted_iota(jnp.int32, (1, LANES), 1) < HALF_LANES
    for si in range(sp):
        s = si if sp == r else pl.program_id(2) * sp + si
        for qb in range(qp):
            i = pl.program_id(1) * qp + qb
            start = jnp.clip(i * QB_DIL - HALF_WIN, 0, sub_len - KW_DIL)
            start = pl.multiple_of(start, HALF_WIN)
            variant = jnp.where(i == 0, 0, jnp.where(i == nblk - 1, 2, 1))
            q = q_ref[0, si, qb * QB_DIL:(qb + 1) * QB_DIL, :]
            kw = k_ref[0, s, pl.ds(start, KW_DIL), :]
            vw = v_ref[0, s, pl.ds(start, KW_DIL), :]
            rows = (slice(qb * QB_DIL, (qb + 1) * QB_DIL) if r == 1
                    else pl.ds(qb * QB_DIL * r + s, QB_DIL, stride=r))
            lane = lax.broadcasted_iota(jnp.int32, (QB_DIL, LANES), 1)
            lse_tile = jnp.zeros((QB_DIL, LANES), F32)
            for j in range(HB // 2):
                cols = slice(j * LANES, (j + 1) * LANES)
                qpair, kp, vp = q[:, cols], kw[:, cols], vw[:, cols]
                outs, lses = [], []
                for c in range(2):
                    qc = jnp.where(low_half if c == 0 else jnp.logical_not(low_half), qpair,
                                   jnp.zeros_like(qpair))
                    sc = lax.dot_general(qc, kp, (((1,), (1,)), ((), ())),
                                         preferred_element_type=F32)
                    sc = sc + bias_ref[2 * j + c, variant]
                    m = jnp.max(sc, axis=-1, keepdims=True)
                    p = jnp.exp2(sc - m)
                    l = jnp.sum(p, axis=-1, keepdims=True)
                    outs.append(jnp.dot(p.astype(BF16), vp, preferred_element_type=F32) / l)
                    lses.append((m + jnp.log2(l)) * LN2)
                o_ref[0, j, rows, :] = jnp.where(low_half, outs[0], outs[1])
                for c in range(2):
                    lse_tile = jnp.where(lane == 2 * j + c, lses[c], lse_tile)
            lse_ref[0, rows, :] = lse_tile


def _attn_b(qkv4, bias3, g, cols):
    B, r, L, _ = qkv4.shape
    S = r * L
    width = HB * DB
    nblk = L // QB_DIL
    sp = min(r, SUBS_DIL)
    qp = max(1, min(ITEMS_DIL // sp, OUT_ROWS_DIL // (QB_DIL * r)))
    qcol, kcol, vcol = cols
    kern = functools.partial(_attn_b_kernel, sub_len=L, r=r, sp=sp, qp=qp)
    slab = jax.ShapeDtypeStruct((B, N_SLABS, S, LANES), F32)
    slab_spec = pl.BlockSpec((1, N_SLABS, QB_DIL * r * qp, LANES), lambda b, i, s: (b, 0, i, 0))
    return pl.pallas_call(
        kern,
        out_shape=[slab, jax.ShapeDtypeStruct((B, S, LANES), F32)],
        grid=(B, nblk // qp, r // sp),
        in_specs=[
            pl.BlockSpec((1, sp, QB_DIL * qp, width), lambda b, i, s: (b, s, i, qcol)),
            pl.BlockSpec((1, r, L, width), lambda b, i, s: (b, 0, 0, kcol)),
            pl.BlockSpec((1, r, L, width), lambda b, i, s: (b, 0, 0, vcol)),
            pl.BlockSpec((HB, 3, QB_DIL, KW_DIL), lambda b, i, s: (0, 0, 0, 0)),
        ],
        out_specs=[slab_spec, pl.BlockSpec((1, QB_DIL * r * qp, LANES), lambda b, i, s: (b, i, 0))],
        compiler_params=_cparams(("parallel", "arbitrary", "arbitrary")),
        name=f"dilated_attn_{g}",
    )(qkv4, qkv4, qkv4, bias3)


def _attn_b_bias(rel_bias, g):
    r = DILATIONS[g]
    tab = rel_bias[:, HA + g * HB: HA + (g + 1) * HB].astype(F32) * LOG2E
    off = jnp.arange(3, dtype=jnp.int32)[:, None, None] * HALF_WIN
    rel = (jnp.arange(KW_DIL, dtype=jnp.int32)[None, None, :] - off
           - jnp.arange(QB_DIL, dtype=jnp.int32)[None, :, None])
    bias = _bias_lookup(_t5_bucket(rel * r), tab)
    return jnp.where((jnp.abs(rel) <= HALF_WIN)[None], bias, NEG)


def _sgu_kernel(zu_ref, zv_ref, lng_ref, lnb_ref, ws_ref, bs_ref, o_ref):
    u = jax.nn.gelu(zu_ref[...].astype(F32))
    v = jax.nn.gelu(zv_ref[...].astype(F32))
    mu = jnp.mean(v, axis=-1, keepdims=True)
    var = jnp.mean(jnp.square(v - mu), axis=-1, keepdims=True)
    v = ((v - mu) * lax.rsqrt(var + EPS) * lng_ref[...] + lnb_ref[...]).astype(BF16)
    gd = v.shape[1] // C_GROUPS
    for n in range(v.shape[0] // CHUNK):
        rows = slice(n * CHUNK, (n + 1) * CHUNK)
        for g in range(C_GROUPS):
            cols = slice(g * gd, (g + 1) * gd)
            mixed = jnp.dot(ws_ref[g], v[rows, cols], preferred_element_type=F32) + bs_ref[:, cols]
            o_ref[rows, cols] = (u[rows, cols] * mixed).astype(o_ref.dtype)


def _sgu(proj2, ln_g, ln_b, w_s, b_exp):
    T = proj2.shape[0]
    tm = min(TM_SGU, T)
    w = MIX_W
    return pl.pallas_call(
        _sgu_kernel,
        out_shape=jax.ShapeDtypeStruct((T, w), BF16),
        grid=(T // tm,),
        in_specs=[pl.BlockSpec((tm, w), lambda i: (i, COL_ZU)),
                  pl.BlockSpec((tm, w), lambda i: (i, COL_ZU + 1)),
                  pl.BlockSpec((1, w), lambda i: (0, 0)),
                  pl.BlockSpec((1, w), lambda i: (0, 0)),
                  pl.BlockSpec((C_GROUPS, CHUNK, CHUNK), lambda i: (0, 0, 0)),
                  pl.BlockSpec((CHUNK, w), lambda i: (0, 0))],
        out_specs=pl.BlockSpec((tm, w), lambda i: (i, 0)),
        compiler_params=_cparams(("parallel",)),
        name="sgu",
    )(proj2, proj2, ln_g, ln_b, w_s, b_exp)


def _route(logits):
    lane = lax.broadcasted_iota(jnp.int32, logits.shape, 1)
    big = jnp.int32(LANES)
    is_grp = (lane >= N_EXPERTS) & (lane < N_EXPERTS + N_GROUPS)
    gl = jnp.where(is_grp, logits, NEG)
    gmax = jnp.max(gl, axis=-1, keepdims=True)
    g_idx = jnp.min(jnp.where(is_grp & (gl == gmax), lane, big), axis=-1, keepdims=True) - N_EXPERTS
    g_w = 1.0 / jnp.sum(jnp.where(is_grp, jnp.exp(gl - gmax), 0.0), axis=-1, keepdims=True)
    in_grp = (lane >= g_idx * E_PER_GROUP) & (lane < (g_idx + 1) * E_PER_GROUP)
    sel = jnp.where(in_grp, logits, NEG)
    v1 = jnp.max(sel, axis=-1, keepdims=True)
    i1 = jnp.min(jnp.where(in_grp & (sel == v1), lane, big), axis=-1, keepdims=True)
    rest = in_grp & (lane != i1)
    sel2 = jnp.where(rest, logits, NEG)
    v2 = jnp.max(sel2, axis=-1, keepdims=True)
    i2 = jnp.min(jnp.where(rest & (sel2 == v2), lane, big), axis=-1, keepdims=True)
    e2 = jnp.exp(v2 - v1)
    w1 = g_w / (1.0 + e2)
    w2 = g_w * e2 / (1.0 + e2)
    return jnp.where(lane == i1, w1, jnp.where(lane == i2, w2, 0.0))


def _mix_kernel(x_ref, ya_ref, ob0_ref, ob1_ref, ob2_ref, ls0_ref, ls1_ref, ls2_ref, yc_ref,
                g0_ref, g1_ref, g2_ref, wb_ref, wo_ref, nf_ref, wr_ref, br_ref,
                xo_ref, h_ref, comb_ref, xn_scr):
    @pl.when(pl.program_id(0) == 0)
    def _():
        xn_scr[...] = jnp.zeros(xn_scr.shape, F32)

    h = _rms_bf16(xn_scr[...], nf_ref[...])
    h_ref[...] = h
    logits = jnp.dot(h, wr_ref[...], preferred_element_type=F32) + br_ref[...]
    comb_ref[...] = _route(logits)

    ls0, ls1, ls2 = ls0_ref[...], ls1_ref[...], ls2_ref[...]
    mx = jnp.maximum(jnp.maximum(ls0, ls1), ls2)
    es = [jnp.exp(ls0 - mx), jnp.exp(ls1 - mx), jnp.exp(ls2 - mx)]
    inv = 1.0 / (es[0] + es[1] + es[2])
    spread = jnp.where(lax.broadcasted_iota(jnp.int32, (2 * LANES, MIX_W), 1) // DB
                       == lax.broadcasted_iota(jnp.int32, (2 * LANES, MIX_W), 0) % LANES,
                       1.0, 0.0).astype(BF16)
    yb = None
    for e, ob_ref in zip(es, (ob0_ref, ob1_ref, ob2_ref)):
        w = e * inv
        hi = w.astype(BF16)
        lo = (w - hi.astype(F32)).astype(BF16)
        wide = jnp.dot(jnp.concatenate([hi, lo], axis=1), spread, preferred_element_type=F32)
        term = wide * jnp.concatenate([ob_ref[0, j] for j in range(N_SLABS)], axis=-1)
        yb = term if yb is None else yb + term
    yb = yb.astype(BF16)
    merged = jax.nn.sigmoid(g0_ref[...].astype(F32)) * jnp.dot(ya_ref[...], wb_ref[0],
                                                               preferred_element_type=F32)
    merged += jax.nn.sigmoid(g1_ref[...].astype(F32)) * jnp.dot(yb, wb_ref[1],
                                                                preferred_element_type=F32)
    merged += jax.nn.sigmoid(g2_ref[...].astype(F32)) * jnp.dot(yc_ref[...], wb_ref[2],
                                                                preferred_element_type=F32)
    xn = x_ref[...] + jnp.dot(merged.astype(BF16), wo_ref[...], preferred_element_type=F32)
    xo_ref[...] = xn
    xn_scr[...] = xn


def _mix(x2, ya, obs, lses, yc, proj2, wb, wo, nf, wr, br):
    T, D = x2.shape
    S = obs[0].shape[2]
    tm = min(TM_MIX, S)
    per_b = S // tm
    n = T // tm
    w = MIX_W
    cur = lambda i: jnp.minimum(i, n - 1)
    lag = lambda i: jnp.maximum(i - 1, 0)
    row = lambda width: pl.BlockSpec((tm, width), lambda i: (cur(i), 0))
    full = lambda a: pl.BlockSpec(a.shape, lambda i: (0,) * a.ndim)
    gate = lambda k: pl.BlockSpec((tm, D), lambda i: (cur(i), COL_GATE + k))
    slab = pl.BlockSpec((1, N_SLABS, tm, LANES),
                        lambda i: (cur(i) // per_b, 0, cur(i) % per_b, 0))
    late = lambda width: pl.BlockSpec((tm, width), lambda i: (lag(i), 0))
    return pl.pallas_call(
        _mix_kernel,
        out_shape=[jax.ShapeDtypeStruct((T, D), F32), jax.ShapeDtypeStruct((T, D), BF16),
                   jax.ShapeDtypeStruct((T, LANES), F32)],
        grid=(n + 1,),
        in_specs=[row(D), row(w), slab, slab, slab, row(LANES), row(LANES), row(LANES), row(w),
                  gate(0), gate(1), gate(2), full(wb), full(wo), full(nf), full(wr), full(br)],
        out_specs=[row(D), late(D), late(LANES)],
        scratch_shapes=[pltpu.VMEM((tm, D), F32)],
        compiler_params=_cparams(("arbitrary",)),
        name="mix",
    )(x2, ya, obs[0], obs[1], obs[2], lses[0], lses[1], lses[2], yc, proj2, proj2, proj2,
      wb, wo, nf, wr, br)


def _moe_kernel(h_ref, comb_ref, x_ref, wg_ref, wu_ref, wd_ref, nfin_ref, o_ref, acc_scr,
                *, final_norm):
    e = pl.program_id(1)

    @pl.when(e == 0)
    def _():
        acc_scr[...] = jnp.zeros(acc_scr.shape, F32)

    h = h_ref[...]
    lane = lax.broadcasted_iota(jnp.int32, comb_ref.shape, 1)
    c = jnp.sum(jnp.where(lane == e, comb_ref[...], 0.0), axis=-1, keepdims=True)
    hid = (jax.nn.silu(jnp.dot(h, wg_ref[0].astype(BF16), preferred_element_type=F32))
           * jnp.dot(h, wu_ref[0].astype(BF16), preferred_element_type=F32))
    acc_scr[...] += c * jnp.dot(hid.astype(BF16), wd_ref[0].astype(BF16),
                                preferred_element_type=F32)

    @pl.when(e == pl.num_programs(1) - 1)
    def _():
        xn = x_ref[...] + acc_scr[...]
        if final_norm:
            ms = jnp.mean(xn * xn, axis=-1, keepdims=True)
            xn = xn * lax.rsqrt(ms + EPS) * nfin_ref[...]
        o_ref[...] = xn


def _moe_dense(h, comb, x2, wg, wu, wd, e0, nfin, final_norm):
    T, D = x2.shape
    tm = min(TM_MOE, T)
    F = wg.shape[2]
    kern = functools.partial(_moe_kernel, final_norm=final_norm)
    return pl.pallas_call(
        kern,
        out_shape=jax.ShapeDtypeStruct((T, D), F32),
        grid=(T // tm, N_EXPERTS),
        in_specs=[pl.BlockSpec((tm, D), lambda i, e: (i, 0)),
                  pl.BlockSpec((tm, LANES), lambda i, e: (i, 0)),
                  pl.BlockSpec((tm, D), lambda i, e: (i, 0)),
                  pl.BlockSpec((1, D, F), lambda i, e: (e0 + e, 0, 0)),
                  pl.BlockSpec((1, D, F), lambda i, e: (e0 + e, 0, 0)),
                  pl.BlockSpec((1, F, D), lambda i, e: (e0 + e, 0, 0)),
                  pl.BlockSpec((1, D), lambda i, e: (0, 0))],
        out_specs=pl.BlockSpec((tm, D), lambda i, e: (i, 0)),
        scratch_shapes=[pltpu.VMEM((tm, D), F32)],
        compiler_params=_cparams(("parallel", "arbitrary")),
        name="moe_dense",
    )(h, comb, x2, wg, wu, wd, nfin)


def _moe_dispatch_kernel(h_ref, comb_ref, o_ref, cnt_ref):
    nt = o_ref.shape[0]
    tm = h_ref.shape[0] // nt
    before = jnp.where(lax.broadcasted_iota(jnp.int32, (tm, tm), 0)
                       < lax.broadcasted_iota(jnp.int32, (tm, tm), 1), 1.0, 0.0).astype(BF16)
    slot = lax.broadcasted_iota(jnp.int32, (MOE_CAP, tm), 0).astype(F32)
    for u in range(nt):
        rows = slice(u * tm, (u + 1) * tm)
        comb = comb_ref[rows, :]
        hi = comb.astype(BF16)
        lo = (comb - hi.astype(F32)).astype(BF16)
        haug = jnp.concatenate([h_ref[rows, :], hi, lo], axis=1)
        a_t = comb.T[:N_EXPERTS] > 0.0
        a_f = jnp.where(a_t, 1.0, 0.0)
        rank_t = jnp.dot(a_f.astype(BF16), before, preferred_element_type=F32)
        blocks = [jnp.where((slot == rank_t[e:e + 1]) & a_t[e:e + 1], 1.0, 0.0).astype(BF16)
                  for e in range(N_EXPERTS)]
        res = jnp.dot(jnp.concatenate(blocks, axis=0), haug, preferred_element_type=F32)
        res = res.astype(o_ref.dtype)
        for e in range(N_EXPERTS):
            o_ref[u, e] = res[e * MOE_CAP:(e + 1) * MOE_CAP]
        cnt_ref[u] = jnp.broadcast_to(jnp.sum(a_f, axis=1, keepdims=True), cnt_ref.shape[1:])


def _moe_dispatch(h, comb):
    T, D = h.shape
    tm = min(TM_DISP, T)
    n = T // tm
    nt = math.gcd(TILES_PER_STEP, n)
    return pl.pallas_call(
        _moe_dispatch_kernel,
        out_shape=[jax.ShapeDtypeStruct((n, N_EXPERTS, MOE_CAP, D + 2 * LANES), BF16),
                   jax.ShapeDtypeStruct((n, N_EXPERTS, LANES), F32)],
        grid=(n // nt,),
        in_specs=[pl.BlockSpec((nt * tm, D), lambda i: (i, 0)),
                  pl.BlockSpec((nt * tm, LANES), lambda i: (i, 0))],
        out_specs=[pl.BlockSpec((nt, N_EXPERTS, MOE_CAP, D + 2 * LANES), lambda i: (i, 0, 0, 0)),
                   pl.BlockSpec((nt, N_EXPERTS, LANES), lambda i: (i, 0, 0))],
        compiler_params=_cparams(("parallel",)),
        name="moe_dispatch",
    )(h, comb)


def _moe_ffn_kernel(n16_ref, s_ref, wg_ref, wu_ref, wd_ref, o_ref,
                    wg_scr, wu_scr, wd_scr, lhs_scr, y_scr):
    e, c = pl.program_id(0), pl.program_id(1)

    @pl.when(c == 0)
    def _():
        wg_scr[...] = wg_ref[0].astype(BF16)
        wu_scr[...] = wu_ref[0].astype(BF16)
        wd_scr[...] = wd_ref[0].astype(BF16)

    g, _, cap, _ = s_ref.shape
    D = o_ref.shape[-1]
    @pl.when((e == 0) & (c == 0))
    def _():
        lhs_scr[...] = jnp.zeros(lhs_scr.shape, lhs_scr.dtype)
        y_scr[...] = jnp.zeros(y_scr.shape, y_scr.dtype)

    offs = []
    off = jnp.int32(0)
    for t in range(g):
        offs.append(off)
        lhs_scr[pl.ds(pl.multiple_of(off, BF16_ROWS), cap), :] = s_ref[t, 0]
        off = off + n16_ref[(c * g + t) * N_EXPERTS + e]
    total = off

    def run(nrows):
        rows = lhs_scr[:nrows]
        h = rows[:, :D]
        wparts = rows[:, D:].astype(F32)
        lane = lax.broadcasted_iota(jnp.int32, wparts.shape, 1)
        w = jnp.sum(jnp.where(lane % LANES == e, wparts, 0.0), axis=-1, keepdims=True)
        hid = (jax.nn.silu(jnp.dot(h, wg_scr[...], preferred_element_type=F32))
               * jnp.dot(h, wu_scr[...], preferred_element_type=F32))
        y = w * jnp.dot(hid.astype(BF16), wd_scr[...], preferred_element_type=F32)
        y_scr[:nrows] = y.astype(y_scr.dtype)

    classes = tuple(range(g * cap // 2, g * cap + 1, FFN_ROW_STEP))
    lower = 0
    for nrows in classes:
        pl.when((total > lower) & (total <= nrows))(functools.partial(run, nrows))
        lower = nrows

    for t in range(g):
        o_ref[t, 0] = y_scr[pl.ds(pl.multiple_of(offs[t], BF16_ROWS), cap), :]


def _moe_ffn(srt, n16, wg, wu, wd, e0):
    n, ne, cap, wdt = srt.shape
    D, F = wg.shape[1], wg.shape[2]
    g = math.gcd(G_FFN, n)
    return pl.pallas_call(
        _moe_ffn_kernel,
        out_shape=jax.ShapeDtypeStruct((n, ne, cap, D), BF16),
        grid_spec=pltpu.PrefetchScalarGridSpec(
            num_scalar_prefetch=1,
            grid=(ne, n // g),
            in_specs=[pl.BlockSpec((g, 1, cap, wdt), lambda e, c, n16: (c, e, 0, 0)),
                      pl.BlockSpec((1, D, F), lambda e, c, n16: (e0 + e, 0, 0)),
                      pl.BlockSpec((1, D, F), lambda e, c, n16: (e0 + e, 0, 0)),
                      pl.BlockSpec((1, F, D), lambda e, c, n16: (e0 + e, 0, 0))],
            out_specs=pl.BlockSpec((g, 1, cap, D), lambda e, c, n16: (c, e, 0, 0)),
            scratch_shapes=[pltpu.VMEM((D, F), BF16), pltpu.VMEM((D, F), BF16),
                            pltpu.VMEM((F, D), BF16), pltpu.VMEM((g * cap, wdt), BF16),
                            pltpu.VMEM((g * cap, D), BF16)]),
        compiler_params=_cparams(("arbitrary", "arbitrary")),
        name="moe_ffn",
    )(n16, srt, wg, wu, wd)


def _moe_combine_kernel(skip_ref, y_ref, comb_ref, x_ref, nfin_ref, o_ref, *, final_norm):
    nt = y_ref.shape[0]
    tm = x_ref.shape[0] // nt
    ncol = N_EXPERTS * MOE_CAP
    before = jnp.where(lax.broadcasted_iota(jnp.int32, (tm, tm), 1)
                       < lax.broadcasted_iota(jnp.int32, (tm, tm), 0), 1.0, 0.0).astype(BF16)
    spread = jnp.where(lax.broadcasted_iota(jnp.int32, (LANES, ncol), 1) // MOE_CAP
                       == lax.broadcasted_iota(jnp.int32, (LANES, ncol), 0), 1.0, 0.0).astype(BF16)
    slot = (lax.broadcasted_iota(jnp.int32, (tm, ncol), 1) % MOE_CAP).astype(F32)
    for u in range(nt):
        rows = slice(u * tm, (u + 1) * tm)
        a = comb_ref[rows, :] > 0.0
        rank = jnp.dot(before, jnp.where(a, 1.0, 0.0).astype(BF16), preferred_element_type=F32)
        key = jnp.where(a, rank, -1.0).astype(BF16)
        key_all = jnp.dot(key, spread, preferred_element_type=F32)
        pc = jnp.where(slot == key_all, 1.0, 0.0).astype(BF16)
        y = jnp.concatenate([y_ref[u, e] for e in range(N_EXPERTS)], axis=0)
        xn = x_ref[rows, :] + jnp.dot(pc, y, preferred_element_type=F32)
        if final_norm:
            ms = jnp.mean(xn * xn, axis=-1, keepdims=True)
            later = skip_ref[pl.program_id(0) * nt + u] == 1
            xn = jnp.where(later, xn, xn * lax.rsqrt(ms + EPS) * nfin_ref[...])
        o_ref[rows, :] = xn


def _moe_combine(skip, y, comb, x2, nfin, final_norm):
    T, D = x2.shape
    n, ne, cap, _ = y.shape
    tm = T // n
    nt = math.gcd(TILES_PER_STEP, n)
    kern = functools.partial(_moe_combine_kernel, final_norm=final_norm)
    return pl.pallas_call(
        kern,
        out_shape=jax.ShapeDtypeStruct((T, D), F32),
        grid_spec=pltpu.PrefetchScalarGridSpec(
            num_scalar_prefetch=1,
            grid=(n // nt,),
            in_specs=[pl.BlockSpec((nt, ne, cap, D), lambda i, sk: (i, 0, 0, 0)),
                      pl.BlockSpec((nt * tm, LANES), lambda i, sk: (i, 0)),
                      pl.BlockSpec((nt * tm, D), lambda i, sk: (i, 0)),
                      pl.BlockSpec((1, D), lambda i, sk: (0, 0))],
            out_specs=pl.BlockSpec((nt * tm, D), lambda i, sk: (i, 0))),
        compiler_params=_cparams(("parallel",)),
        name="moe_combine",
    )(skip, y, comb, x2, nfin)


def _moe_fix_kernel(tiles_ref, experts_ref, first_ref, last_ref, n_ref, h_ref, comb_ref, prev_ref,
                    wg_ref, wu_ref, wd_ref, nfin_ref, o_ref, *, final_norm):
    del tiles_ref
    s = pl.program_id(0)

    @pl.when(s < n_ref[0])
    def _():
        e = experts_ref[s]
        tm = h_ref.shape[0]
        comb = comb_ref[...]
        a = jnp.where(comb > 0.0, 1.0, 0.0)
        before = (lax.broadcasted_iota(jnp.int32, (tm, tm), 1)
                  < lax.broadcasted_iota(jnp.int32, (tm, tm), 0))
        rank = jnp.dot(jnp.where(before, 1.0, 0.0).astype(BF16), a.astype(BF16),
                       preferred_element_type=F32)
        lane = lax.broadcasted_iota(jnp.int32, comb.shape, 1)
        dropped = (lane == e) & (rank >= MOE_CAP)
        c = jnp.sum(jnp.where(dropped, comb, 0.0), axis=-1, keepdims=True)
        h = h_ref[...]
        hid = (jax.nn.silu(jnp.dot(h, wg_ref[0].astype(BF16), preferred_element_type=F32))
               * jnp.dot(h, wu_ref[0].astype(BF16), preferred_element_type=F32))
        add = c * jnp.dot(hid.astype(BF16), wd_ref[0].astype(BF16), preferred_element_type=F32)
        fresh = first_ref[s] == 1

        @pl.when(fresh)
        def _():
            o_ref[...] = prev_ref[...] + add

        @pl.when(jnp.logical_not(fresh))
        def _():
            o_ref[...] += add

        if final_norm:
            @pl.when(last_ref[s] == 1)
            def _():
                xn = o_ref[...]
                ms = jnp.mean(xn * xn, axis=-1, keepdims=True)
                o_ref[...] = xn * lax.rsqrt(ms + EPS) * nfin_ref[...]


def _moe_fix(tiles, experts, first, last, n, out, h, comb, wg, wu, wd, e0, nfin, final_norm):
    T, D = out.shape
    tm = min(TM_DISP, T)
    F = wg.shape[2]
    tile = lambda width: pl.BlockSpec((tm, width), lambda s, tl, ex, fi, la, n: (tl[s], 0))
    wspec = lambda shape: pl.BlockSpec(shape, lambda s, tl, ex, fi, la, n: (e0 + ex[s], 0, 0))
    kern = functools.partial(_moe_fix_kernel, final_norm=final_norm)
    return pl.pallas_call(
        kern,
        out_shape=jax.ShapeDtypeStruct((T, D), F32),
        grid_spec=pltpu.PrefetchScalarGridSpec(
            num_scalar_prefetch=5,
            grid=(MAX_OVF,),
            in_specs=[tile(D), tile(LANES), tile(D), wspec((1, D, F)), wspec((1, D, F)),
                      wspec((1, F, D)), pl.BlockSpec((1, D), lambda s, tl, ex, fi, la, n: (0, 0))],
            out_specs=tile(D)),
        input_output_aliases={7: 0},
        compiler_params=_cparams(("arbitrary",)),
        name="moe_fix",
    )(tiles, experts, first, last, n, h, comb, out, wg, wu, wd, nfin)


def _moe(h, comb, x2, wg, wu, wd, e0, nfin, final_norm):
    srt, cnt = _moe_dispatch(h, comb)
    over = (cnt[:, :, 0] > MOE_CAP).reshape(-1)
    n_ovf = jnp.sum(over.astype(jnp.int32))
    pairs = jnp.nonzero(over, size=MAX_OVF, fill_value=0)[0].astype(jnp.int32)
    pairs = jnp.where(jnp.arange(MAX_OVF) < n_ovf, pairs, pairs[jnp.clip(n_ovf - 1, 0, MAX_OVF - 1)])
    tiles, experts = pairs // N_EXPERTS, pairs % N_EXPERTS
    change = (tiles[1:] != tiles[:-1]).astype(jnp.int32)
    first = jnp.concatenate([jnp.ones((1,), jnp.int32), change])
    last = jnp.maximum(jnp.concatenate([change, jnp.ones((1,), jnp.int32)]),
                       (jnp.arange(MAX_OVF) == n_ovf - 1).astype(jnp.int32))
    skip = jnp.any(over.reshape(-1, N_EXPERTS), axis=1).astype(jnp.int32)

    used = jnp.minimum(cnt[:, :, 0], MOE_CAP).astype(jnp.int32).reshape(-1)
    n16 = (used + (BF16_ROWS - 1)) // BF16_ROWS * BF16_ROWS

    def routed():
        out = _moe_combine(skip, _moe_ffn(srt, n16, wg, wu, wd, e0), comb, x2, nfin, final_norm)
        return lax.cond(
            n_ovf > 0,
            lambda: _moe_fix(tiles, experts, first, last, n_ovf.reshape(1), out, h, comb,
                             wg, wu, wd, e0, nfin, final_norm),
            lambda: out)

    return lax.cond(n_ovf > MAX_OVF,
                    lambda: _moe_dense(h, comb, x2, wg, wu, wd, e0, nfin, final_norm), routed)


def kernel(x, rel_bias, norm_mix, w_in, diff_lambda, diff_subln, sgu_ln_g, sgu_ln_b, sgu_w, sgu_b,
           w_branch, w_out, norm_ffn, w_router_grp, b_router_grp, w_router_exp, b_router_exp,
           w_gate, w_up, w_down, norm_final):
    B, S, D = x.shape
    T = B * S
    depth = w_in.shape[0]
    a_out = HA * 2 * DA
    grp_w = HB * DB
    b_cols = 3 * NG_B * grp_w
    qkv_b0 = 3 * a_out
    zc0 = qkv_b0 + b_cols
    gate0 = zc0 + 2 * MIX_W
    qk_scale = DA ** -0.5

    bias_a, cfar = _attn_a_bias(rel_bias)
    bias_b = [_attn_b_bias(rel_bias, g) for g in range(NG_B)]

    col = jnp.arange(w_in.shape[2])
    is_q = (col < a_out) | ((col >= qkv_b0) & (col < qkv_b0 + NG_B * grp_w))
    col_scale = jnp.where(is_q, qk_scale * LOG2E, 1.0)

    def group_cols(w, g):
        return [w[:, qkv_b0 + (c * NG_B + g) * grp_w: qkv_b0 + (c * NG_B + g + 1) * grp_w]
                for c in range(3)]

    wg_all = w_gate.reshape((-1,) + w_gate.shape[2:])
    wu_all = w_up.reshape((-1,) + w_up.shape[2:])
    wd_all = w_down.reshape((-1,) + w_down.shape[2:])

    x2 = x.reshape(T, D)
    for i in range(depth):
        w = (w_in[i] * col_scale.astype(F32)).astype(BF16)
        nm = norm_mix[i][None, :]
        w_main = jnp.concatenate([w[:, :2 * a_out], w[:, zc0:]] + group_cols(w, 0), axis=1)
        x3 = x2.reshape(B, S, D)
        proj3, vt = _inproj(x3, nm, w_main, w[:, 2 * a_out:3 * a_out].T)
        proj2 = proj3.reshape(T, proj3.shape[2])

        lam_init = 0.8 - 0.6 * math.exp(-0.3 * i)
        lp = diff_lambda[i].astype(F32)
        lam = jnp.exp(jnp.sum(lp[0] * lp[1])) - jnp.exp(jnp.sum(lp[2] * lp[3])) + lam_init
        ya = _attn_a(proj3, vt, lam.reshape(1), cfar, bias_a, diff_subln[i][None, :], lam_init)

        strided = [g for g in range(NG_B) if DILATIONS[g] > 1]
        permuted = dict(zip(strided, _inproj_perm(
            x3, nm, [jnp.concatenate(group_cols(w, g), axis=1) for g in strided],
            [DILATIONS[g] for g in strided])))
        obs, lses = [], []
        for g in range(NG_B):
            if g in permuted:
                qkv4, cols = permuted[g], (0, 1, 2)
            else:
                qkv4, cols = proj3[:, None], (COL_QKV0, COL_QKV0 + 1, COL_QKV0 + 2)
            o, l = _attn_b(qkv4, bias_b[g], g, cols)
            obs.append(o)
            lses.append(l.reshape(T, LANES))

        b_exp = jnp.repeat(sgu_b[i].T, MIX_W // C_GROUPS, axis=1)
        yc = _sgu(proj2, sgu_ln_g[i][None, :], sgu_ln_b[i][None, :], sgu_w[i].astype(BF16), b_exp)

        wr = jnp.concatenate([w_router_exp[i].transpose(1, 0, 2).reshape(D, N_EXPERTS),
                              w_router_grp[i]], axis=1)
        wr = jnp.pad(wr, ((0, 0), (0, LANES - wr.shape[1]))).astype(BF16)
        br = jnp.concatenate([b_router_exp[i].reshape(N_EXPERTS), b_router_grp[i]])
        br = jnp.pad(br, (0, LANES - br.shape[0]))[None, :].astype(F32)

        x2, h, comb = _mix(x2, ya.reshape(T, a_out), obs, lses, yc, proj2,
                           w_branch[i].astype(BF16), w_out[i].astype(BF16), norm_ffn[i][None, :],
                           wr, br)
        x2 = _moe(h, comb, x2, wg_all, wu_all, wd_all, i * N_EXPERTS, norm_final[None, :],
                  i == depth - 1)
    return x2.reshape(B, S, D)
```

```python
import functools
import math

import jax
import jax.numpy as jnp
from jax import lax
from jax.experimental import pallas as pl
from jax.experimental.pallas import tpu as pltpu

F32 = jnp.float32
BF16 = jnp.bfloat16

EPS = 1e-6
NEG = -1e30
LOG2E = 1.4426950408889634
LN2 = 0.6931471805599453
LANES = 128
HALF_LANES = LANES // 2
VMEM_LIMIT = 48 * 1024 * 1024

HA = 4
DA = 64
MIX_W = 512
WINDOWS = (128, 512, 2048)
DILATIONS = (1, 4, 16)
NG_B = 3
HB = 8
DB = 64
HALF_WIN = 64
CHUNK = 128
C_GROUPS = 4
N_BRANCH = 3
N_BUCKETS = 32
MAX_DIST = 128
N_GROUPS = 4
E_PER_GROUP = 4
N_EXPERTS = N_GROUPS * E_PER_GROUP
N_SLABS = MIX_W // LANES

TM_PROJ = 1024
TN_PROJ = 3328
TM_PERM = 1024
PERM_BLK = 256
T_ATT = 512
QB_DIL = 128
KW_DIL = QB_DIL + 2 * HALF_WIN
ITEMS_DIL = 8
SUBS_DIL = 4
OUT_ROWS_DIL = 2048
TM_SGU = 2048
TM_MIX = 512
TM_MOE = 1024
TM_DISP = 256
MOE_CAP = HALF_LANES
G_FFN = 16
TILES_PER_STEP = 4
FFN_ROW_STEP = 64
BF16_ROWS = 16
MAX_OVF = 64

COL_ZU = 2
COL_GATE = 2
COL_QKV0 = 10


def _cparams(sem):
    return pltpu.CompilerParams(dimension_semantics=sem, vmem_limit_bytes=VMEM_LIMIT)


def _t5_bucket(rel):
    nb = N_BUCKETS // 2
    max_exact = nb // 2
    ret = (rel > 0).astype(jnp.int32) * nb
    n = jnp.abs(rel)
    nf = jnp.maximum(n, 1).astype(F32)
    large = max_exact + (jnp.log(nf / max_exact) / math.log(MAX_DIST / max_exact)
                         * (nb - max_exact)).astype(jnp.int32)
    large = jnp.minimum(large, nb - 1)
    return ret + jnp.where(n < max_exact, n, large)


def _bias_lookup(bucket, tab):
    out = jnp.zeros((tab.shape[1],) + bucket.shape, F32)
    expand = (slice(None),) + (None,) * bucket.ndim
    for b in range(N_BUCKETS):
        out = jnp.where(bucket[None] == b, tab[b][expand], out)
    return out


def _rms_bf16(x, g):
    ms = jnp.mean(x * x, axis=-1, keepdims=True)
    return (x * lax.rsqrt(ms + EPS) * g).astype(BF16)


def _inproj_kernel(x_ref, g_ref, w_ref, wt_ref, o_ref, vt_ref, h_scr):
    @pl.when(pl.program_id(2) == 0)
    def _():
        h = _rms_bf16(x_ref[0], g_ref[...])
        h_scr[...] = h
        res = lax.dot_general(wt_ref[...], h, (((1,), (1,)), ((), ())),
                              preferred_element_type=F32).astype(vt_ref.dtype)
        for hd in range(vt_ref.shape[1]):
            for n in range(vt_ref.shape[2]):
                vt_ref[0, hd, n] = res[hd * LANES:(hd + 1) * LANES, n * T_ATT:(n + 1) * T_ATT]

    o_ref[0] = jnp.dot(h_scr[...], w_ref[...], preferred_element_type=F32).astype(o_ref.dtype)


def _inproj(x3, g, w, wt):
    B, S, D = x3.shape
    N = w.shape[1]
    tm = min(TM_PROJ, S)
    nh, nb = wt.shape[0] // LANES, tm // T_ATT
    return pl.pallas_call(
        _inproj_kernel,
        out_shape=[jax.ShapeDtypeStruct((B, S, N), BF16),
                   jax.ShapeDtypeStruct((B, nh, S // T_ATT, LANES, T_ATT), BF16)],
        grid=(B, S // tm, N // TN_PROJ),
        in_specs=[pl.BlockSpec((1, tm, D), lambda b, i, j: (b, i, 0)),
                  pl.BlockSpec((1, D), lambda b, i, j: (0, 0)),
                  pl.BlockSpec((D, TN_PROJ), lambda b, i, j: (0, j)),
                  pl.BlockSpec(wt.shape, lambda b, i, j: (0, 0))],
        out_specs=[pl.BlockSpec((1, tm, TN_PROJ), lambda b, i, j: (b, i, j)),
                   pl.BlockSpec((1, nh, nb, LANES, T_ATT), lambda b, i, j: (b, 0, i, 0, 0))],
        scratch_shapes=[pltpu.VMEM((tm, D), BF16)],
        compiler_params=_cparams(("parallel", "parallel", "arbitrary")),
        name="inproj",
    )(x3, g, w, wt)


def _inproj_perm_kernel(x_ref, g_ref, *refs, dilations):
    ng = len(dilations)
    p_refs, w_refs, o_refs = refs[:ng], refs[ng:2 * ng], refs[2 * ng:]
    h = _rms_bf16(x_ref[0], g_ref[...])
    nblk = h.shape[0] // PERM_BLK
    for r, p_ref, w_ref, o_ref in zip(dilations, p_refs, w_refs, o_refs):
        hp = jnp.concatenate(
            [jnp.dot(p_ref[...], h[k * PERM_BLK:(k + 1) * PERM_BLK], preferred_element_type=F32)
             for k in range(nblk)], axis=0).astype(BF16)
        res = jnp.dot(hp, w_ref[...], preferred_element_type=F32).astype(o_ref.dtype)
        n = PERM_BLK // r
        for k in range(nblk):
            for s in range(r):
                o_ref[0, s, k * n:(k + 1) * n, :] = res[k * PERM_BLK + s * n:k * PERM_BLK + (s + 1) * n, :]


def _inproj_perm(x3, g, ws, dilations):
    B, S, D = x3.shape
    tm = min(TM_PERM, S)
    perms = []
    for r in dilations:
        n = PERM_BLK // r
        o = jnp.arange(PERM_BLK, dtype=jnp.int32)
        src = (o % n) * r + o // n
        perms.append((src[:, None] == jnp.arange(PERM_BLK, dtype=jnp.int32)[None, :]).astype(BF16))
    kern = functools.partial(_inproj_perm_kernel, dilations=tuple(dilations))
    return pl.pallas_call(
        kern,
        out_shape=[jax.ShapeDtypeStruct((B, r, S // r, w.shape[1]), BF16)
                   for r, w in zip(dilations, ws)],
        grid=(B, S // tm),
        in_specs=([pl.BlockSpec((1, tm, D), lambda b, i: (b, i, 0)),
                   pl.BlockSpec((1, D), lambda b, i: (0, 0))]
                  + [pl.BlockSpec((PERM_BLK, PERM_BLK), lambda b, i: (0, 0)) for _ in dilations]
                  + [pl.BlockSpec(w.shape, lambda b, i: (0, 0)) for w in ws]),
        out_specs=[pl.BlockSpec((1, r, tm // r, w.shape[1]), lambda b, i: (b, 0, i, 0))
                   for r, w in zip(dilations, ws)],
        compiler_params=_cparams(("parallel", "parallel")),
        name="inproj_perm",
    )(x3, g, *perms, *ws)


def _attn_a_kernel(lam_ref, cfar_ref, q_ref, k_ref, vt_ref, bias_ref, g_ref, o_ref,
                   st0_scr, st1_scr, m0_scr, m1_scr, acc_scr, l_scr, *, out_scale, nq, n_blocks):
    k = pl.program_id(0)
    t = T_ATT
    nk = k_ref.shape[1] // t
    n1 = jnp.minimum(k // 2, n_blocks - 1)
    n2 = jnp.maximum(k - 1, 0) // 2
    h1, qi1 = (n1 // nq) % HA, n1 % nq
    h2, qi2 = (n2 // nq) % HA, n2 % nq

    @pl.when(k == 0)
    def _():
        st1_scr[...] = jnp.zeros(st1_scr.shape, F32)
        m1_scr[...] = jnp.zeros(m1_scr.shape, F32)
        acc_scr[...] = jnp.zeros(acc_scr.shape, F32)
        l_scr[...] = jnp.ones(l_scr.shape, F32)

    low_half = lax.broadcasted_iota(jnp.int32, (1, LANES), 1) < HALF_LANES

    def both(cmap, st_w, m_w, st_r, m_r):
        q = q_ref[0]
        zero = jnp.zeros_like(q)
        qc = jnp.where(low_half, q, zero) if cmap == 0 else jnp.where(low_half, zero, q)
        m_prev = m_r[...]
        l = jnp.zeros((1, t), F32)
        acc = jnp.zeros((LANES, t), F32)
        m_new = None
        for j, d in enumerate(range(-1, nk - 1)):
            a2 = lax.rem(qi2 + (d + nk), nk)
            if d <= 1:
                shifted = m_prev
            else:
                shifted = m_prev - cfar_ref[2 * h2 + (a2 > qi2).astype(jnp.int32)]
            p = jnp.exp2(st_r[j] - shifted)
            l = l + jnp.sum(p, axis=0, keepdims=True)
            acc = acc + jnp.dot(vt_ref[0, 0, a2], p.astype(BF16), preferred_element_type=F32)

            a1 = lax.rem(qi1 + (d + nk), nk)
            delta1 = a1 - qi1
            kb = k_ref[0, pl.ds(pl.multiple_of(a1 * t, t), t), :]
            st = lax.dot_general(kb, qc, (((1,), (1,)), ((), ())), preferred_element_type=F32)
            if d <= 1:
                st = st + bias_ref[0, jnp.clip(delta1, -2, 2) + 2]
                cm = jnp.max(st, axis=0, keepdims=True)
            else:
                cm = (jnp.max(st, axis=0, keepdims=True)
                      + cfar_ref[2 * h1 + (delta1 > 0).astype(jnp.int32)])
            st_w[j] = st
            m_new = cm if m_new is None else jnp.maximum(m_new, cm)
        m_w[...] = m_new
        return l, acc

    @pl.when(k % 2 == 0)
    def _():
        l1, acc1 = both(0, st0_scr, m0_scr, st1_scr, m1_scr)
        ot = acc_scr[...] / l_scr[...] - lam_ref[0] * (acc1 / l1)
        o = ot.T
        ms = jnp.mean(o * o, axis=-1, keepdims=True)
        o_ref[0] = (o * lax.rsqrt(ms + EPS) * g_ref[...] * out_scale).astype(o_ref.dtype)

    @pl.when(k % 2 == 1)
    def _():
        l0, acc0 = both(1, st1_scr, m1_scr, st0_scr, m0_scr)
        acc_scr[...] = acc0
        l_scr[...] = l0


def _attn_a(proj3, vt, lam, cfar, bias5, subln_g, lam_init):
    B, S, _ = proj3.shape
    t = T_ATT
    nk = S // t
    n_blocks = B * HA * nk

    def scored(k):
        n = jnp.minimum(k // 2, n_blocks - 1)
        return n // (HA * nk), (n // nk) % HA, n % nk

    def lagged(k, lag):
        n = jnp.maximum(k - lag, 0) // 2
        return n // (HA * nk), (n // nk) % HA, n % nk

    def q_map(k):
        b, h, qi = scored(k)
        return b, qi, h

    def k_map(k):
        b, h, _ = scored(k)
        return b, 0, HA + h

    def vt_map(k):
        b, h, _ = lagged(k, 1)
        return b, h, 0, 0, 0

    def out_map(k):
        b, h, qi = lagged(k, 2)
        return b, qi, h

    kern = functools.partial(_attn_a_kernel, out_scale=1.0 - lam_init, nq=nk, n_blocks=n_blocks)
    return pl.pallas_call(
        kern,
        out_shape=jax.ShapeDtypeStruct((B, S, HA * 2 * DA), BF16),
        grid=(2 * n_blocks + 1,),
        in_specs=[
            pl.BlockSpec(memory_space=pltpu.SMEM),
            pl.BlockSpec(memory_space=pltpu.SMEM),
            pl.BlockSpec((1, t, LANES), q_map),
            pl.BlockSpec((1, S, LANES), k_map),
            pl.BlockSpec((1, 1, nk, LANES, t), vt_map),
            pl.BlockSpec((1, 5, t, t), lambda k: (scored(k)[1], 0, 0, 0)),
            pl.BlockSpec((1, LANES), lambda k: (0, 0)),
        ],
        out_specs=pl.BlockSpec((1, t, LANES), out_map),
        scratch_shapes=[pltpu.VMEM((nk, t, t), F32), pltpu.VMEM((nk, t, t), F32),
                        pltpu.VMEM((1, t), F32), pltpu.VMEM((1, t), F32),
                        pltpu.VMEM((LANES, t), F32), pltpu.VMEM((1, t), F32)],
        compiler_params=_cparams(("arbitrary",)),
        name="diff_attn",
    )(lam, cfar, proj3, proj3, vt, bias5, subln_g)


def _attn_a_bias(rel_bias):
    t = T_ATT
    tab = rel_bias[:, :HA].astype(F32) * LOG2E
    d = jnp.arange(-1, 2, dtype=jnp.int32)[:, None, None] * t
    rel = d + jnp.arange(t, dtype=jnp.int32)[None, :, None] - jnp.arange(t, dtype=jnp.int32)[None, None, :]
    near = _bias_lookup(_t5_bucket(rel), tab)
    far = tab[_t5_bucket(jnp.array([-(t + 1), t + 1], dtype=jnp.int32))].T
    fill = lambda side: jnp.broadcast_to(far[:, side, None, None, None], (HA, 1, t, t))
    tiles = jnp.concatenate([fill(0), near, fill(1)], axis=1)
    return tiles, far.reshape(2 * HA)


def _attn_b_kernel(q_ref, k_ref, v_ref, bias_ref, o_ref, lse_ref, *, sub_len, r, sp, qp):
    nblk = sub_len // QB_DIL
    low_half = lax.broadcasted_iota(jnp.int32, (1, LANES), 1) < HALF_LANES
    for si in range(sp):
        s = si if sp == r else pl.program_id(2) * sp + si
        for qb in range(qp):
            i = pl.program_id(1) * qp + qb
            start = jnp.clip(i * QB_DIL - HALF_WIN, 0, sub_len - KW_DIL)
            start = pl.multiple_of(start, HALF_WIN)
            variant = jnp.where(i == 0, 0, jnp.where(i == nblk - 1, 2, 1))
            q = q_ref[0, si, qb * QB_DIL:(qb + 1) * QB_DIL, :]
            kw = k_ref[0, s, pl.ds(start, KW_DIL), :]
            vw = v_ref[0, s, pl.ds(start, KW_DIL), :]
            rows = (slice(qb * QB_DIL, (qb + 1) * QB_DIL) if r == 1
                    else pl.ds(qb * QB_DIL * r + s, QB_DIL, stride=r))
            lane = lax.broadcasted_iota(jnp.int32, (QB_DIL, LANES), 1)
            lse_tile = jnp.zeros((QB_DIL, LANES), F32)
            scores = []
            for j in range(HB // 2):
                cols = slice(j * LANES, (j + 1) * LANES)
                qpair, kp = q[:, cols], kw[:, cols]
                for c in range(2):
                    qc = jnp.where(low_half if c == 0 else jnp.logical_not(low_half), qpair,
                                   jnp.zeros_like(qpair))
                    sc = lax.dot_general(qc, kp, (((1,), (1,)), ((), ())),
                                         preferred_element_type=F32)
                    scores.append(sc + bias_ref[2 * j + c, variant])
            for j in range(HB // 2):
                cols = slice(j * LANES, (j + 1) * LANES)
                vp = vw[:, cols]
                outs, lses = [], []
                for c in range(2):
                    sc = scores[2 * j + c]
                    m = jnp.max(sc, axis=-1, keepdims=True)
                    p = jnp.exp2(sc - m)
                    l = jnp.sum(p, axis=-1, keepdims=True)
                    outs.append(jnp.dot(p.astype(BF16), vp, preferred_element_type=F32) / l)
                    lses.append((m + jnp.log2(l)) * LN2)
                o_ref[0, j, rows, :] = jnp.where(low_half, outs[0], outs[1])
                for c in range(2):
                    lse_tile = jnp.where(lane == 2 * j + c, lses[c], lse_tile)
            lse_ref[0, rows, :] = lse_tile


def _attn_b(qkv4, bias3, g, cols):
    B, r, L, _ = qkv4.shape
    S = r * L
    width = HB * DB
    nblk = L // QB_DIL
    sp = min(r, SUBS_DIL)
    qp = max(1, min(ITEMS_DIL // sp, OUT_ROWS_DIL // (QB_DIL * r)))
    qcol, kcol, vcol = cols
    kern = functools.partial(_attn_b_kernel, sub_len=L, r=r, sp=sp, qp=qp)
    slab = jax.ShapeDtypeStruct((B, N_SLABS, S, LANES), F32)
    slab_spec = pl.BlockSpec((1, N_SLABS, QB_DIL * r * qp, LANES), lambda b, i, s: (b, 0, i, 0))
    return pl.pallas_call(
        kern,
        out_shape=[slab, jax.ShapeDtypeStruct((B, S, LANES), F32)],
        grid=(B, nblk // qp, r // sp),
        in_specs=[
            pl.BlockSpec((1, sp, QB_DIL * qp, width), lambda b, i, s: (b, s, i, qcol)),
            pl.BlockSpec((1, r, L, width), lambda b, i, s: (b, 0, 0, kcol)),
            pl.BlockSpec((1, r, L, width), lambda b, i, s: (b, 0, 0, vcol)),
            pl.BlockSpec((HB, 3, QB_DIL, KW_DIL), lambda b, i, s: (0, 0, 0, 0)),
        ],
        out_specs=[slab_spec, pl.BlockSpec((1, QB_DIL * r * qp, LANES), lambda b, i, s: (b, i, 0))],
        compiler_params=_cparams(("parallel", "arbitrary", "arbitrary")),
        name=f"dilated_attn_{g}",
    )(qkv4, qkv4, qkv4, bias3)


def _attn_b_bias(rel_bias, g):
    r = DILATIONS[g]
    tab = rel_bias[:, HA + g * HB: HA + (g + 1) * HB].astype(F32) * LOG2E
    off = jnp.arange(3, dtype=jnp.int32)[:, None, None] * HALF_WIN
    rel = (jnp.arange(KW_DIL, dtype=jnp.int32)[None, None, :] - off
           - jnp.arange(QB_DIL, dtype=jnp.int32)[None, :, None])
    bias = _bias_lookup(_t5_bucket(rel * r), tab)
    return jnp.where((jnp.abs(rel) <= HALF_WIN)[None], bias, NEG)


def _sgu_kernel(zu_ref, zv_ref, lng_ref, lnb_ref, ws_ref, bs_ref, o_ref):
    u = jax.nn.gelu(zu_ref[...].astype(F32))
    v = jax.nn.gelu(zv_ref[...].astype(F32))
    mu = jnp.mean(v, axis=-1, keepdims=True)
    var = jnp.mean(jnp.square(v - mu), axis=-1, keepdims=True)
    v = ((v - mu) * lax.rsqrt(var + EPS) * lng_ref[...] + lnb_ref[...]).astype(BF16)
    gd = v.shape[1] // C_GROUPS
    for n in range(v.shape[0] // CHUNK):
        rows = slice(n * CHUNK, (n + 1) * CHUNK)
        for g in range(C_GROUPS):
            cols = slice(g * gd, (g + 1) * gd)
            mixed = jnp.dot(ws_ref[g], v[rows, cols], preferred_element_type=F32) + bs_ref[:, cols]
            o_ref[rows, cols] = (u[rows, cols] * mixed).astype(o_ref.dtype)


def _sgu(proj2, ln_g, ln_b, w_s, b_exp):
    T = proj2.shape[0]
    tm = min(TM_SGU, T)
    w = MIX_W
    return pl.pallas_call(
        _sgu_kernel,
        out_shape=jax.ShapeDtypeStruct((T, w), BF16),
        grid=(T // tm,),
        in_specs=[pl.BlockSpec((tm, w), lambda i: (i, COL_ZU)),
                  pl.BlockSpec((tm, w), lambda i: (i, COL_ZU + 1)),
                  pl.BlockSpec((1, w), lambda i: (0, 0)),
                  pl.BlockSpec((1, w), lambda i: (0, 0)),
                  pl.BlockSpec((C_GROUPS, CHUNK, CHUNK), lambda i: (0, 0, 0)),
                  pl.BlockSpec((CHUNK, w), lambda i: (0, 0))],
        out_specs=pl.BlockSpec((tm, w), lambda i: (i, 0)),
        compiler_params=_cparams(("parallel",)),
        name="sgu",
    )(proj2, proj2, ln_g, ln_b, w_s, b_exp)


def _route(logits):
    lane = lax.broadcasted_iota(jnp.int32, logits.shape, 1)
    big = jnp.int32(LANES)
    is_grp = (lane >= N_EXPERTS) & (lane < N_EXPERTS + N_GROUPS)
    gl = jnp.where(is_grp, logits, NEG)
    gmax = jnp.max(gl, axis=-1, keepdims=True)
    g_idx = jnp.min(jnp.where(is_grp & (gl == gmax), lane, big), axis=-1, keepdims=True) - N_EXPERTS
    g_w = 1.0 / jnp.sum(jnp.where(is_grp, jnp.exp(gl - gmax), 0.0), axis=-1, keepdims=True)
    in_grp = (lane >= g_idx * E_PER_GROUP) & (lane < (g_idx + 1) * E_PER_GROUP)
    sel = jnp.where(in_grp, logits, NEG)
    v1 = jnp.max(sel, axis=-1, keepdims=True)
    i1 = jnp.min(jnp.where(in_grp & (sel == v1), lane, big), axis=-1, keepdims=True)
    rest = in_grp & (lane != i1)
    sel2 = jnp.where(rest, logits, NEG)
    v2 = jnp.max(sel2, axis=-1, keepdims=True)
    i2 = jnp.min(jnp.where(rest & (sel2 == v2), lane, big), axis=-1, keepdims=True)
    e2 = jnp.exp(v2 - v1)
    w1 = g_w / (1.0 + e2)
    w2 = g_w * e2 / (1.0 + e2)
    return jnp.where(lane == i1, w1, jnp.where(lane == i2, w2, 0.0))


def _mix_kernel(x_ref, ya_ref, ob0_ref, ob1_ref, ob2_ref, ls0_ref, ls1_ref, ls2_ref, yc_ref,
                g0_ref, g1_ref, g2_ref, wb_ref, wo_ref, nf_ref, wr_ref, br_ref,
                xo_ref, h_ref, comb_ref, xn_scr):
    @pl.when(pl.program_id(0) == 0)
    def _():
        xn_scr[...] = jnp.zeros(xn_scr.shape, F32)

    h = _rms_bf16(xn_scr[...], nf_ref[...])
    h_ref[...] = h
    logits = jnp.dot(h, wr_ref[...], preferred_element_type=F32) + br_ref[...]
    comb_ref[...] = _route(logits)

    ls0, ls1, ls2 = ls0_ref[...], ls1_ref[...], ls2_ref[...]
    mx = jnp.maximum(jnp.maximum(ls0, ls1), ls2)
    es = [jnp.exp(ls0 - mx), jnp.exp(ls1 - mx), jnp.exp(ls2 - mx)]
    inv = 1.0 / (es[0] + es[1] + es[2])
    spread = jnp.where(lax.broadcasted_iota(jnp.int32, (2 * LANES, MIX_W), 1) // DB
                       == lax.broadcasted_iota(jnp.int32, (2 * LANES, MIX_W), 0) % LANES,
                       1.0, 0.0).astype(BF16)
    yb = None
    for e, ob_ref in zip(es, (ob0_ref, ob1_ref, ob2_ref)):
        w = e * inv
        hi = w.astype(BF16)
        lo = (w - hi.astype(F32)).astype(BF16)
        wide = jnp.dot(jnp.concatenate([hi, lo], axis=1), spread, preferred_element_type=F32)
        term = wide * jnp.concatenate([ob_ref[0, j] for j in range(N_SLABS)], axis=-1)
        yb = term if yb is None else yb + term
    yb = yb.astype(BF16)
    merged = jax.nn.sigmoid(g0_ref[...].astype(F32)) * jnp.dot(ya_ref[...], wb_ref[0],
                                                               preferred_element_type=F32)
    merged += jax.nn.sigmoid(g1_ref[...].astype(F32)) * jnp.dot(yb, wb_ref[1],
                                                                preferred_element_type=F32)
    merged += jax.nn.sigmoid(g2_ref[...].astype(F32)) * jnp.dot(yc_ref[...], wb_ref[2],
                                                                preferred_element_type=F32)
    xn = x_ref[...] + jnp.dot(merged.astype(BF16), wo_ref[...], preferred_element_type=F32)
    xo_ref[...] = xn
    xn_scr[...] = xn


def _mix(x2, ya, obs, lses, yc, proj2, wb, wo, nf, wr, br):
    T, D = x2.shape
    S = obs[0].shape[2]
    tm = min(TM_MIX, S)
    per_b = S // tm
    n = T // tm
    w = MIX_W
    cur = lambda i: jnp.minimum(i, n - 1)
    lag = lambda i: jnp.maximum(i - 1, 0)
    row = lambda width: pl.BlockSpec((tm, width), lambda i: (cur(i), 0))
    full = lambda a: pl.BlockSpec(a.shape, lambda i: (0,) * a.ndim)
    gate = lambda k: pl.BlockSpec((tm, D), lambda i: (cur(i), COL_GATE + k))
    slab = pl.BlockSpec((1, N_SLABS, tm, LANES),
                        lambda i: (cur(i) // per_b, 0, cur(i) % per_b, 0))
    late = lambda width: pl.BlockSpec((tm, width), lambda i: (lag(i), 0))
    return pl.pallas_call(
        _mix_kernel,
        out_shape=[jax.ShapeDtypeStruct((T, D), F32), jax.ShapeDtypeStruct((T, D), BF16),
                   jax.ShapeDtypeStruct((T, LANES), F32)],
        grid=(n + 1,),
        in_specs=[row(D), row(w), slab, slab, slab, row(LANES), row(LANES), row(LANES), row(w),
                  gate(0), gate(1), gate(2), full(wb), full(wo), full(nf), full(wr), full(br)],
        out_specs=[row(D), late(D), late(LANES)],
        scratch_shapes=[pltpu.VMEM((tm, D), F32)],
        compiler_params=_cparams(("arbitrary",)),
        name="mix",
    )(x2, ya, obs[0], obs[1], obs[2], lses[0], lses[1], lses[2], yc, proj2, proj2, proj2,
      wb, wo, nf, wr, br)


def _moe_kernel(h_ref, comb_ref, x_ref, wg_ref, wu_ref, wd_ref, nfin_ref, o_ref, acc_scr,
                *, final_norm):
    e = pl.program_id(1)

    @pl.when(e == 0)
    def _():
        acc_scr[...] = jnp.zeros(acc_scr.shape, F32)

    h = h_ref[...]
    lane = lax.broadcasted_iota(jnp.int32, comb_ref.shape, 1)
    c = jnp.sum(jnp.where(lane == e, comb_ref[...], 0.0), axis=-1, keepdims=True)
    hid = (jax.nn.silu(jnp.dot(h, wg_ref[0].astype(BF16), preferred_element_type=F32))
           * jnp.dot(h, wu_ref[0].astype(BF16), preferred_element_type=F32))
    acc_scr[...] += c * jnp.dot(hid.astype(BF16), wd_ref[0].astype(BF16),
                                preferred_element_type=F32)

    @pl.when(e == pl.num_programs(1) - 1)
    def _():
        xn = x_ref[...] + acc_scr[...]
        if final_norm:
            ms = jnp.mean(xn * xn, axis=-1, keepdims=True)
            xn = xn * lax.rsqrt(ms + EPS) * nfin_ref[...]
        o_ref[...] = xn


def _moe_dense(h, comb, x2, wg, wu, wd, e0, nfin, final_norm):
    T, D = x2.shape
    tm = min(TM_MOE, T)
    F = wg.shape[2]
    kern = functools.partial(_moe_kernel, final_norm=final_norm)
    return pl.pallas_call(
        kern,
        out_shape=jax.ShapeDtypeStruct((T, D), F32),
        grid=(T // tm, N_EXPERTS),
        in_specs=[pl.BlockSpec((tm, D), lambda i, e: (i, 0)),
                  pl.BlockSpec((tm, LANES), lambda i, e: (i, 0)),
                  pl.BlockSpec((tm, D), lambda i, e: (i, 0)),
                  pl.BlockSpec((1, D, F), lambda i, e: (e0 + e, 0, 0)),
                  pl.BlockSpec((1, D, F), lambda i, e: (e0 + e, 0, 0)),
                  pl.BlockSpec((1, F, D), lambda i, e: (e0 + e, 0, 0)),
                  pl.BlockSpec((1, D), lambda i, e: (0, 0))],
        out_specs=pl.BlockSpec((tm, D), lambda i, e: (i, 0)),
        scratch_shapes=[pltpu.VMEM((tm, D), F32)],
        compiler_params=_cparams(("parallel", "arbitrary")),
        name="moe_dense",
    )(h, comb, x2, wg, wu, wd, nfin)


def _moe_dispatch_kernel(h_ref, comb_ref, o_ref, cnt_ref):
    nt = o_ref.shape[0]
    tm = h_ref.shape[0] // nt
    before = jnp.where(lax.broadcasted_iota(jnp.int32, (tm, tm), 0)
                       < lax.broadcasted_iota(jnp.int32, (tm, tm), 1), 1.0, 0.0).astype(BF16)
    slot = lax.broadcasted_iota(jnp.int32, (MOE_CAP, tm), 0).astype(F32)
    for u in range(nt):
        rows = slice(u * tm, (u + 1) * tm)
        comb = comb_ref[rows, :]
        hi = comb.astype(BF16)
        lo = (comb - hi.astype(F32)).astype(BF16)
        haug = jnp.concatenate([h_ref[rows, :], hi, lo], axis=1)
        a_t = comb.T[:N_EXPERTS] > 0.0
        a_f = jnp.where(a_t, 1.0, 0.0)
        rank_t = jnp.dot(a_f.astype(BF16), before, preferred_element_type=F32)
        blocks = [jnp.where((slot == rank_t[e:e + 1]) & a_t[e:e + 1], 1.0, 0.0).astype(BF16)
                  for e in range(N_EXPERTS)]
        res = jnp.dot(jnp.concatenate(blocks, axis=0), haug, preferred_element_type=F32)
        res = res.astype(o_ref.dtype)
        for e in range(N_EXPERTS):
            o_ref[u, e] = res[e * MOE_CAP:(e + 1) * MOE_CAP]
        cnt_ref[u] = jnp.broadcast_to(jnp.sum(a_f, axis=1, keepdims=True), cnt_ref.shape[1:])


def _moe_dispatch(h, comb):
    T, D = h.shape
    tm = min(TM_DISP, T)
    n = T // tm
    nt = math.gcd(TILES_PER_STEP, n)
    return pl.pallas_call(
        _moe_dispatch_kernel,
        out_shape=[jax.ShapeDtypeStruct((n, N_EXPERTS, MOE_CAP, D + 2 * LANES), BF16),
                   jax.ShapeDtypeStruct((n, N_EXPERTS, LANES), F32)],
        grid=(n // nt,),
        in_specs=[pl.BlockSpec((nt * tm, D), lambda i: (i, 0)),
                  pl.BlockSpec((nt * tm, LANES), lambda i: (i, 0))],
        out_specs=[pl.BlockSpec((nt, N_EXPERTS, MOE_CAP, D + 2 * LANES), lambda i: (i, 0, 0, 0)),
                   pl.BlockSpec((nt, N_EXPERTS, LANES), lambda i: (i, 0, 0))],
        compiler_params=_cparams(("parallel",)),
        name="moe_dispatch",
    )(h, comb)


def _moe_ffn_kernel(n16_ref, s_ref, wg_ref, wu_ref, wd_ref, o_ref,
                    wg_scr, wu_scr, wd_scr, lhs_scr, y_scr):
    e, c = pl.program_id(0), pl.program_id(1)

    @pl.when(c == 0)
    def _():
        wg_scr[...] = wg_ref[0].astype(BF16)
        wu_scr[...] = wu_ref[0].astype(BF16)
        wd_scr[...] = wd_ref[0].astype(BF16)

    g, _, cap, _ = s_ref.shape
    D = o_ref.shape[-1]
    @pl.when((e == 0) & (c == 0))
    def _():
        lhs_scr[...] = jnp.zeros(lhs_scr.shape, lhs_scr.dtype)
        y_scr[...] = jnp.zeros(y_scr.shape, y_scr.dtype)

    offs = []
    off = jnp.int32(0)
    for t in range(g):
        offs.append(off)
        lhs_scr[pl.ds(pl.multiple_of(off, BF16_ROWS), cap), :] = s_ref[t, 0]
        off = off + n16_ref[(c * g + t) * N_EXPERTS + e]
    total = off

    def run(nrows):
        rows = lhs_scr[:nrows]
        h = rows[:, :D]
        wparts = rows[:, D:].astype(F32)
        lane = lax.broadcasted_iota(jnp.int32, wparts.shape, 1)
        w = jnp.sum(jnp.where(lane % LANES == e, wparts, 0.0), axis=-1, keepdims=True)
        hid = (jax.nn.silu(jnp.dot(h, wg_scr[...], preferred_element_type=F32))
               * jnp.dot(h, wu_scr[...], preferred_element_type=F32))
        y = w * jnp.dot(hid.astype(BF16), wd_scr[...], preferred_element_type=F32)
        y_scr[:nrows] = y.astype(y_scr.dtype)

    classes = tuple(range(g * cap // 2, g * cap + 1, FFN_ROW_STEP))
    lower = 0
    for nrows in classes:
        pl.when((total > lower) & (total <= nrows))(functools.partial(run, nrows))
        lower = nrows

    for t in range(g):
        o_ref[t, 0] = y_scr[pl.ds(pl.multiple_of(offs[t], BF16_ROWS), cap), :]


def _moe_ffn(srt, n16, wg, wu, wd, e0):
    n, ne, cap, wdt = srt.shape
    D, F = wg.shape[1], wg.shape[2]
    g = math.gcd(G_FFN, n)
    return pl.pallas_call(
        _moe_ffn_kernel,
        out_shape=jax.ShapeDtypeStruct((n, ne, cap, D), BF16),
        grid_spec=pltpu.PrefetchScalarGridSpec(
            num_scalar_prefetch=1,
            grid=(ne, n // g),
            in_specs=[pl.BlockSpec((g, 1, cap, wdt), lambda e, c, n16: (c, e, 0, 0)),
                      pl.BlockSpec((1, D, F), lambda e, c, n16: (e0 + e, 0, 0)),
                      pl.BlockSpec((1, D, F), lambda e, c, n16: (e0 + e, 0, 0)),
                      pl.BlockSpec((1, F, D), lambda e, c, n16: (e0 + e, 0, 0))],
            out_specs=pl.BlockSpec((g, 1, cap, D), lambda e, c, n16: (c, e, 0, 0)),
            scratch_shapes=[pltpu.VMEM((D, F), BF16), pltpu.VMEM((D, F), BF16),
                            pltpu.VMEM((F, D), BF16), pltpu.VMEM((g * cap, wdt), BF16),
                            pltpu.VMEM((g * cap, D), BF16)]),
        compiler_params=_cparams(("arbitrary", "arbitrary")),
        name="moe_ffn",
    )(n16, srt, wg, wu, wd)


def _moe_combine_kernel(skip_ref, y_ref, comb_ref, x_ref, nfin_ref, o_ref, *, final_norm):
    nt = y_ref.shape[0]
    tm = x_ref.shape[0] // nt
    ncol = N_EXPERTS * MOE_CAP
    before = jnp.where(lax.broadcasted_iota(jnp.int32, (tm, tm), 1)
                       < lax.broadcasted_iota(jnp.int32, (tm, tm), 0), 1.0, 0.0).astype(BF16)
    spread = jnp.where(lax.broadcasted_iota(jnp.int32, (LANES, ncol), 1) // MOE_CAP
                       == lax.broadcasted_iota(jnp.int32, (LANES, ncol), 0), 1.0, 0.0).astype(BF16)
    slot = (lax.broadcasted_iota(jnp.int32, (tm, ncol), 1) % MOE_CAP).astype(F32)
    for u in range(nt):
        rows = slice(u * tm, (u + 1) * tm)
        a = comb_ref[rows, :] > 0.0
        rank = jnp.dot(before, jnp.where(a, 1.0, 0.0).astype(BF16), preferred_element_type=F32)
        key = jnp.where(a, rank, -1.0).astype(BF16)
        key_all = jnp.dot(key, spread, preferred_element_type=F32)
        pc = jnp.where(slot == key_all, 1.0, 0.0).astype(BF16)
        y = jnp.concatenate([y_ref[u, e] for e in range(N_EXPERTS)], axis=0)
        xn = x_ref[rows, :] + jnp.dot(pc, y, preferred_element_type=F32)
        if final_norm:
            ms = jnp.mean(xn * xn, axis=-1, keepdims=True)
            later = skip_ref[pl.program_id(0) * nt + u] == 1
            xn = jnp.where(later, xn, xn * lax.rsqrt(ms + EPS) * nfin_ref[...])
        o_ref[rows, :] = xn


def _moe_combine(skip, y, comb, x2, nfin, final_norm):
    T, D = x2.shape
    n, ne, cap, _ = y.shape
    tm = T // n
    nt = math.gcd(TILES_PER_STEP, n)
    kern = functools.partial(_moe_combine_kernel, final_norm=final_norm)
    return pl.pallas_call(
        kern,
        out_shape=jax.ShapeDtypeStruct((T, D), F32),
        grid_spec=pltpu.PrefetchScalarGridSpec(
            num_scalar_prefetch=1,
            grid=(n // nt,),
            in_specs=[pl.BlockSpec((nt, ne, cap, D), lambda i, sk: (i, 0, 0, 0)),
                      pl.BlockSpec((nt * tm, LANES), lambda i, sk: (i, 0)),
                      pl.BlockSpec((nt * tm, D), lambda i, sk: (i, 0)),
                      pl.BlockSpec((1, D), lambda i, sk: (0, 0))],
            out_specs=pl.BlockSpec((nt * tm, D), lambda i, sk: (i, 0))),
        compiler_params=_cparams(("parallel",)),
        name="moe_combine",
    )(skip, y, comb, x2, nfin)


def _moe_fix_kernel(tiles_ref, experts_ref, first_ref, last_ref, n_ref, h_ref, comb_ref, prev_ref,
                    wg_ref, wu_ref, wd_ref, nfin_ref, o_ref, *, final_norm):
    del tiles_ref
    s = pl.program_id(0)

    @pl.when(s < n_ref[0])
    def _():
        e = experts_ref[s]
        tm = h_ref.shape[0]
        comb = comb_ref[...]
        a = jnp.where(comb > 0.0, 1.0, 0.0)
        before = (lax.broadcasted_iota(jnp.int32, (tm, tm), 1)
                  < lax.broadcasted_iota(jnp.int32, (tm, tm), 0))
        rank = jnp.dot(jnp.where(before, 1.0, 0.0).astype(BF16), a.astype(BF16),
                       preferred_element_type=F32)
        lane = lax.broadcasted_iota(jnp.int32, comb.shape, 1)
        dropped = (lane == e) & (rank >= MOE_CAP)
        c = jnp.sum(jnp.where(dropped, comb, 0.0), axis=-1, keepdims=True)
        h = h_ref[...]
        hid = (jax.nn.silu(jnp.dot(h, wg_ref[0].astype(BF16), preferred_element_type=F32))
               * jnp.dot(h, wu_ref[0].astype(BF16), preferred_element_type=F32))
        add = c * jnp.dot(hid.astype(BF16), wd_ref[0].astype(BF16), preferred_element_type=F32)
        fresh = first_ref[s] == 1

        @pl.when(fresh)
        def _():
            o_ref[...] = prev_ref[...] + add

        @pl.when(jnp.logical_not(fresh))
        def _():
            o_ref[...] += add

        if final_norm:
            @pl.when(last_ref[s] == 1)
            def _():
                xn = o_ref[...]
                ms = jnp.mean(xn * xn, axis=-1, keepdims=True)
                o_ref[...] = xn * lax.rsqrt(ms + EPS) * nfin_ref[...]


def _moe_fix(tiles, experts, first, last, n, out, h, comb, wg, wu, wd, e0, nfin, final_norm):
    T, D = out.shape
    tm = min(TM_DISP, T)
    F = wg.shape[2]
    tile = lambda width: pl.BlockSpec((tm, width), lambda s, tl, ex, fi, la, n: (tl[s], 0))
    wspec = lambda shape: pl.BlockSpec(shape, lambda s, tl, ex, fi, la, n: (e0 + ex[s], 0, 0))
    kern = functools.partial(_moe_fix_kernel, final_norm=final_norm)
    return pl.pallas_call(
        kern,
        out_shape=jax.ShapeDtypeStruct((T, D), F32),
        grid_spec=pltpu.PrefetchScalarGridSpec(
            num_scalar_prefetch=5,
            grid=(MAX_OVF,),
            in_specs=[tile(D), tile(LANES), tile(D), wspec((1, D, F)), wspec((1, D, F)),
                      wspec((1, F, D)), pl.BlockSpec((1, D), lambda s, tl, ex, fi, la, n: (0, 0))],
            out_specs=tile(D)),
        input_output_aliases={7: 0},
        compiler_params=_cparams(("arbitrary",)),
        name="moe_fix",
    )(tiles, experts, first, last, n, h, comb, out, wg, wu, wd, nfin)


def _moe(h, comb, x2, wg, wu, wd, e0, nfin, final_norm):
    srt, cnt = _moe_dispatch(h, comb)
    over = (cnt[:, :, 0] > MOE_CAP).reshape(-1)
    n_ovf = jnp.sum(over.astype(jnp.int32))
    pairs = jnp.nonzero(over, size=MAX_OVF, fill_value=0)[0].astype(jnp.int32)
    pairs = jnp.where(jnp.arange(MAX_OVF) < n_ovf, pairs, pairs[jnp.clip(n_ovf - 1, 0, MAX_OVF - 1)])
    tiles, experts = pairs // N_EXPERTS, pairs % N_EXPERTS
    change = (tiles[1:] != tiles[:-1]).astype(jnp.int32)
    first = jnp.concatenate([jnp.ones((1,), jnp.int32), change])
    last = jnp.maximum(jnp.concatenate([change, jnp.ones((1,), jnp.int32)]),
                       (jnp.arange(MAX_OVF) == n_ovf - 1).astype(jnp.int32))
    skip = jnp.any(over.reshape(-1, N_EXPERTS), axis=1).astype(jnp.int32)

    used = jnp.minimum(cnt[:, :, 0], MOE_CAP).astype(jnp.int32).reshape(-1)
    n16 = (used + (BF16_ROWS - 1)) // BF16_ROWS * BF16_ROWS

    def routed():
        out = _moe_combine(skip, _moe_ffn(srt, n16, wg, wu, wd, e0), comb, x2, nfin, final_norm)
        return lax.cond(
            n_ovf > 0,
            lambda: _moe_fix(tiles, experts, first, last, n_ovf.reshape(1), out, h, comb,
                             wg, wu, wd, e0, nfin, final_norm),
            lambda: out)

    return lax.cond(n_ovf > MAX_OVF,
                    lambda: _moe_dense(h, comb, x2, wg, wu, wd, e0, nfin, final_norm), routed)


def kernel(x, rel_bias, norm_mix, w_in, diff_lambda, diff_subln, sgu_ln_g, sgu_ln_b, sgu_w, sgu_b,
           w_branch, w_out, norm_ffn, w_router_grp, b_router_grp, w_router_exp, b_router_exp,
           w_gate, w_up, w_down, norm_final):
    B, S, D = x.shape
    T = B * S
    depth = w_in.shape[0]
    a_out = HA * 2 * DA
    grp_w = HB * DB
    b_cols = 3 * NG_B * grp_w
    qkv_b0 = 3 * a_out
    zc0 = qkv_b0 + b_cols
    gate0 = zc0 + 2 * MIX_W
    qk_scale = DA ** -0.5

    bias_a, cfar = _attn_a_bias(rel_bias)
    bias_b = [_attn_b_bias(rel_bias, g) for g in range(NG_B)]

    col = jnp.arange(w_in.shape[2])
    is_q = (col < a_out) | ((col >= qkv_b0) & (col < qkv_b0 + NG_B * grp_w))
    col_scale = jnp.where(is_q, qk_scale * LOG2E, 1.0)

    def group_cols(w, g):
        return [w[:, qkv_b0 + (c * NG_B + g) * grp_w: qkv_b0 + (c * NG_B + g + 1) * grp_w]
                for c in range(3)]

    wg_all = w_gate.reshape((-1,) + w_gate.shape[2:])
    wu_all = w_up.reshape((-1,) + w_up.shape[2:])
    wd_all = w_down.reshape((-1,) + w_down.shape[2:])

    x2 = x.reshape(T, D)
    for i in range(depth):
        w = (w_in[i] * col_scale.astype(F32)).astype(BF16)
        nm = norm_mix[i][None, :]
        w_main = jnp.concatenate([w[:, :2 * a_out], w[:, zc0:]] + group_cols(w, 0), axis=1)
        x3 = x2.reshape(B, S, D)
        proj3, vt = _inproj(x3, nm, w_main, w[:, 2 * a_out:3 * a_out].T)
        proj2 = proj3.reshape(T, proj3.shape[2])

        lam_init = 0.8 - 0.6 * math.exp(-0.3 * i)
        lp = diff_lambda[i].astype(F32)
        lam = jnp.exp(jnp.sum(lp[0] * lp[1])) - jnp.exp(jnp.sum(lp[2] * lp[3])) + lam_init
        ya = _attn_a(proj3, vt, lam.reshape(1), cfar, bias_a, diff_subln[i][None, :], lam_init)

        strided = [g for g in range(NG_B) if DILATIONS[g] > 1]
        permuted = dict(zip(strided, _inproj_perm(
            x3, nm, [jnp.concatenate(group_cols(w, g), axis=1) for g in strided],
            [DILATIONS[g] for g in strided])))
        obs, lses = [], []
        for g in range(NG_B):
            if g in permuted:
                qkv4, cols = permuted[g], (0, 1, 2)
            else:
                qkv4, cols = proj3[:, None], (COL_QKV0, COL_QKV0 + 1, COL_QKV0 + 2)
            o, l = _attn_b(qkv4, bias_b[g], g, cols)
            obs.append(o)
            lses.append(l.reshape(T, LANES))

        b_exp = jnp.repeat(sgu_b[i].T, MIX_W // C_GROUPS, axis=1)
        yc = _sgu(proj2, sgu_ln_g[i][None, :], sgu_ln_b[i][None, :], sgu_w[i].astype(BF16), b_exp)

        wr = jnp.concatenate([w_router_exp[i].transpose(1, 0, 2).reshape(D, N_EXPERTS),
                              w_router_grp[i]], axis=1)
        wr = jnp.pad(wr, ((0, 0), (0, LANES - wr.shape[1]))).astype(BF16)
        br = jnp.concatenate([b_router_exp[i].reshape(N_EXPERTS), b_router_grp[i]])
        br = jnp.pad(br, (0, LANES - br.shape[0]))[None, :].astype(F32)

        x2, h, comb = _mix(x2, ya.reshape(T, a_out), obs, lses, yc, proj2,
                           w_branch[i].astype(BF16), w_out[i].astype(BF16), norm_ffn[i][None, :],
                           wr, br)
        x2 = _moe(h, comb, x2, wg_all, wu_all, wd_all, i * N_EXPERTS, norm_final[None, :],
                  i == depth - 1)
    return x2.reshape(B, S, D)
```

```python
import functools
import math

import jax
import jax.numpy as jnp
from jax import lax
from jax.experimental import pallas as pl
from jax.experimental.pallas import tpu as pltpu

F32 = jnp.float32
BF16 = jnp.bfloat16

EPS = 1e-6
NEG = -1e30
LOG2E = 1.4426950408889634
LN2 = 0.6931471805599453
LANES = 128
HALF_LANES = LANES // 2
VMEM_LIMIT = 48 * 1024 * 1024

HA = 4
DA = 64
MIX_W = 512
WINDOWS = (128, 512, 2048)
DILATIONS = (1, 4, 16)
NG_B = 3
HB = 8
DB = 64
HALF_WIN = 64
CHUNK = 128
C_GROUPS = 4
N_BRANCH = 3
N_BUCKETS = 32
MAX_DIST = 128
N_GROUPS = 4
E_PER_GROUP = 4
N_EXPERTS = N_GROUPS * E_PER_GROUP
N_SLABS = MIX_W // LANES

TM_PROJ = 1024
TN_PROJ = 3328
TM_PERM = 1024
PERM_BLK = 256
T_ATT = 512
QB_DIL = 128
KW_DIL = QB_DIL + 2 * HALF_WIN
ITEMS_DIL = 8
SUBS_DIL = 4
OUT_ROWS_DIL = 2048
TM_SGU = 2048
TM_MIX = 512
TM_MOE = 1024
TM_DISP = 256
MOE_CAP = HALF_LANES
G_FFN = 16
TILES_PER_STEP = 4
FFN_ROW_STEP = 64
BF16_ROWS = 16
MAX_OVF = 64

COL_ZU = 2
COL_GATE = 2
COL_QKV0 = 10


def _cparams(sem):
    return pltpu.CompilerParams(dimension_semantics=sem, vmem_limit_bytes=VMEM_LIMIT)


def _t5_bucket(rel):
    nb = N_BUCKETS // 2
    max_exact = nb // 2
    ret = (rel > 0).astype(jnp.int32) * nb
    n = jnp.abs(rel)
    nf = jnp.maximum(n, 1).astype(F32)
    large = max_exact + (jnp.log(nf / max_exact) / math.log(MAX_DIST / max_exact)
                         * (nb - max_exact)).astype(jnp.int32)
    large = jnp.minimum(large, nb - 1)
    return ret + jnp.where(n < max_exact, n, large)


def _bias_lookup(bucket, tab):
    out = jnp.zeros((tab.shape[1],) + bucket.shape, F32)
    expand = (slice(None),) + (None,) * bucket.ndim
    for b in range(N_BUCKETS):
        out = jnp.where(bucket[None] == b, tab[b][expand], out)
    return out


def _rms_bf16(x, g):
    ms = jnp.mean(x * x, axis=-1, keepdims=True)
    return (x * lax.rsqrt(ms + EPS) * g).astype(BF16)


def _inproj_kernel(x_ref, g_ref, w_ref, wt_ref, o_ref, vt_ref, h_scr):
    @pl.when(pl.program_id(2) == 0)
    def _():
        h = _rms_bf16(x_ref[0], g_ref[...])
        h_scr[...] = h
        res = lax.dot_general(wt_ref[...], h, (((1,), (1,)), ((), ())),
                              preferred_element_type=F32).astype(vt_ref.dtype)
        for hd in range(vt_ref.shape[1]):
            for n in range(vt_ref.shape[2]):
                vt_ref[0, hd, n] = res[hd * LANES:(hd + 1) * LANES, n * T_ATT:(n + 1) * T_ATT]

    o_ref[0] = jnp.dot(h_scr[...], w_ref[...], preferred_element_type=F32).astype(o_ref.dtype)


def _inproj(x3, g, w, wt):
    B, S, D = x3.shape
    N = w.shape[1]
    tm = min(TM_PROJ, S)
    nh, nb = wt.shape[0] // LANES, tm // T_ATT
    return pl.pallas_call(
        _inproj_kernel,
        out_shape=[jax.ShapeDtypeStruct((B, S, N), BF16),
                   jax.ShapeDtypeStruct((B, nh, S // T_ATT, LANES, T_ATT), BF16)],
        grid=(B, S // tm, N // TN_PROJ),
        in_specs=[pl.BlockSpec((1, tm, D), lambda b, i, j: (b, i, 0)),
                  pl.BlockSpec((1, D), lambda b, i, j: (0, 0)),
                  pl.BlockSpec((D, TN_PROJ), lambda b, i, j: (0, j)),
                  pl.BlockSpec(wt.shape, lambda b, i, j: (0, 0))],
        out_specs=[pl.BlockSpec((1, tm, TN_PROJ), lambda b, i, j: (b, i, j)),
                   pl.BlockSpec((1, nh, nb, LANES, T_ATT), lambda b, i, j: (b, 0, i, 0, 0))],
        scratch_shapes=[pltpu.VMEM((tm, D), BF16)],
        compiler_params=_cparams(("parallel", "parallel", "arbitrary")),
        name="inproj",
    )(x3, g, w, wt)


def _inproj_perm_kernel(x_ref, g_ref, *refs, dilations):
    ng = len(dilations)
    p_refs, w_refs, o_refs = refs[:ng], refs[ng:2 * ng], refs[2 * ng:]
    h = _rms_bf16(x_ref[0], g_ref[...])
    nblk = h.shape[0] // PERM_BLK
    for r, p_ref, w_ref, o_ref in zip(dilations, p_refs, w_refs, o_refs):
        hp = jnp.concatenate(
            [jnp.dot(p_ref[...], h[k * PERM_BLK:(k + 1) * PERM_BLK], preferred_element_type=F32)
             for k in range(nblk)], axis=0).astype(BF16)
        res = jnp.dot(hp, w_ref[...], preferred_element_type=F32).astype(o_ref.dtype)
        n = PERM_BLK // r
        for k in range(nblk):
            for s in range(r):
                o_ref[0, s, k * n:(k + 1) * n, :] = res[k * PERM_BLK + s * n:k * PERM_BLK + (s + 1) * n, :]


def _inproj_perm(x3, g, ws, dilations):
    B, S, D = x3.shape
    tm = min(TM_PERM, S)
    perms = []
    for r in dilations:
        n = PERM_BLK // r
        o = jnp.arange(PERM_BLK, dtype=jnp.int32)
        src = (o % n) * r + o // n
        perms.append((src[:, None] == jnp.arange(PERM_BLK, dtype=jnp.int32)[None, :]).astype(BF16))
    kern = functools.partial(_inproj_perm_kernel, dilations=tuple(dilations))
    return pl.pallas_call(
        kern,
        out_shape=[jax.ShapeDtypeStruct((B, r, S // r, w.shape[1]), BF16)
                   for r, w in zip(dilations, ws)],
        grid=(B, S // tm),
        in_specs=([pl.BlockSpec((1, tm, D), lambda b, i: (b, i, 0)),
                   pl.BlockSpec((1, D), lambda b, i: (0, 0))]
                  + [pl.BlockSpec((PERM_BLK, PERM_BLK), lambda b, i: (0, 0)) for _ in dilations]
                  + [pl.BlockSpec(w.shape, lambda b, i: (0, 0)) for w in ws]),
        out_specs=[pl.BlockSpec((1, r, tm // r, w.shape[1]), lambda b, i: (b, 0, i, 0))
                   for r, w in zip(dilations, ws)],
        compiler_params=_cparams(("parallel", "parallel")),
        name="inproj_perm",
    )(x3, g, *perms, *ws)


def _attn_a_kernel(lam_ref, cfar_ref, q_ref, k_ref, vt_ref, bias_ref, g_ref, o_ref,
                   st0_scr, st1_scr, m0_scr, m1_scr, acc_scr, l_scr, *, out_scale, nq, n_blocks):
    k = pl.program_id(0)
    t = T_ATT
    nk = k_ref.shape[1] // t
    n1 = jnp.minimum(k // 2, n_blocks - 1)
    n2 = jnp.maximum(k - 1, 0) // 2
    h1, qi1 = (n1 // nq) % HA, n1 % nq
    h2, qi2 = (n2 // nq) % HA, n2 % nq

    @pl.when(k == 0)
    def _():
        st1_scr[...] = jnp.zeros(st1_scr.shape, F32)
        m1_scr[...] = jnp.zeros(m1_scr.shape, F32)
        acc_scr[...] = jnp.zeros(acc_scr.shape, F32)
        l_scr[...] = jnp.ones(l_scr.shape, F32)

    low_half = lax.broadcasted_iota(jnp.int32, (1, LANES), 1) < HALF_LANES

    def both(cmap, st_w, m_w, st_r, m_r):
        q = q_ref[0]
        zero = jnp.zeros_like(q)
        qc = jnp.where(low_half, q, zero) if cmap == 0 else jnp.where(low_half, zero, q)
        m_prev = m_r[...]
        l = jnp.zeros((1, t), F32)
        acc = jnp.zeros((LANES, t), F32)
        m_new = None
        for j, d in enumerate(range(-1, nk - 1)):
            a2 = lax.rem(qi2 + (d + nk), nk)
            if d <= 1:
                shifted = m_prev
            else:
                shifted = m_prev - cfar_ref[2 * h2 + (a2 > qi2).astype(jnp.int32)]
            p = jnp.exp2(st_r[j] - shifted)
            l = l + jnp.sum(p, axis=0, keepdims=True)
            acc = acc + jnp.dot(vt_ref[0, 0, a2], p.astype(BF16), preferred_element_type=F32)

            a1 = lax.rem(qi1 + (d + nk), nk)
            delta1 = a1 - qi1
            kb = k_ref[0, pl.ds(pl.multiple_of(a1 * t, t), t), :]
            st = lax.dot_general(kb, qc, (((1,), (1,)), ((), ())), preferred_element_type=F32)
            if d <= 1:
                st = st + bias_ref[0, jnp.clip(delta1, -2, 2) + 2]
                cm = jnp.max(st, axis=0, keepdims=True)
            else:
                cm = (jnp.max(st, axis=0, keepdims=True)
                      + cfar_ref[2 * h1 + (delta1 > 0).astype(jnp.int32)])
            st_w[j] = st
            m_new = cm if m_new is None else jnp.maximum(m_new, cm)
        m_w[...] = m_new
        return l, acc

    @pl.when(k % 2 == 0)
    def _():
        l1, acc1 = both(0, st0_scr, m0_scr, st1_scr, m1_scr)
        ot = acc_scr[...] / l_scr[...] - lam_ref[0] * (acc1 / l1)
        o = ot.T
        ms = jnp.mean(o * o, axis=-1, keepdims=True)
        o_ref[0] = (o * lax.rsqrt(ms + EPS) * g_ref[...] * out_scale).astype(o_ref.dtype)

    @pl.when(k % 2 == 1)
    def _():
        l0, acc0 = both(1, st1_scr, m1_scr, st0_scr, m0_scr)
        acc_scr[...] = acc0
        l_scr[...] = l0


def _attn_a(proj3, vt, lam, cfar, bias5, subln_g, lam_init):
    B, S, _ = proj3.shape
    t = T_ATT
    nk = S // t
    n_blocks = B * HA * nk

    def scored(k):
        n = jnp.minimum(k // 2, n_blocks - 1)
        return n // (HA * nk), (n // nk) % HA, n % nk

    def lagged(k, lag):
        n = jnp.maximum(k - lag, 0) // 2
        return n // (HA * nk), (n // nk) % HA, n % nk

    def q_map(k):
        b, h, qi = scored(k)
        return b, qi, h

    def k_map(k):
        b, h, _ = scored(k)
        return b, 0, HA + h

    def vt_map(k):
        b, h, _ = lagged(k, 1)
        return b, h, 0, 0, 0

    def out_map(k):
        b, h, qi = lagged(k, 2)
        return b, qi, h

    kern = functools.partial(_attn_a_kernel, out_scale=1.0 - lam_init, nq=nk, n_blocks=n_blocks)
    return pl.pallas_call(
        kern,
        out_shape=jax.ShapeDtypeStruct((B, S, HA * 2 * DA), BF16),
        grid=(2 * n_blocks + 1,),
        in_specs=[
            pl.BlockSpec(memory_space=pltpu.SMEM),
            pl.BlockSpec(memory_space=pltpu.SMEM),
            pl.BlockSpec((1, t, LANES), q_map),
            pl.BlockSpec((1, S, LANES), k_map),
            pl.BlockSpec((1, 1, nk, LANES, t), vt_map),
            pl.BlockSpec((1, 5, t, t), lambda k: (scored(k)[1], 0, 0, 0)),
            pl.BlockSpec((1, LANES), lambda k: (0, 0)),
        ],
        out_specs=pl.BlockSpec((1, t, LANES), out_map),
        scratch_shapes=[pltpu.VMEM((nk, t, t), F32), pltpu.VMEM((nk, t, t), F32),
                        pltpu.VMEM((1, t), F32), pltpu.VMEM((1, t), F32),
                        pltpu.VMEM((LANES, t), F32), pltpu.VMEM((1, t), F32)],
        compiler_params=_cparams(("arbitrary",)),
        name="diff_attn",
    )(lam, cfar, proj3, proj3, vt, bias5, subln_g)


def _attn_a_bias(rel_bias):
    t = T_ATT
    tab = rel_bias[:, :HA].astype(F32) * LOG2E
    d = jnp.arange(-1, 2, dtype=jnp.int32)[:, None, None] * t
    rel = d + jnp.arange(t, dtype=jnp.int32)[None, :, None] - jnp.arange(t, dtype=jnp.int32)[None, None, :]
    near = _bias_lookup(_t5_bucket(rel), tab)
    far = tab[_t5_bucket(jnp.array([-(t + 1), t + 1], dtype=jnp.int32))].T
    fill = lambda side: jnp.broadcast_to(far[:, side, None, None, None], (HA, 1, t, t))
    tiles = jnp.concatenate([fill(0), near, fill(1)], axis=1)
    return tiles, far.reshape(2 * HA)


def _attn_b_kernel(q_ref, k_ref, v_ref, bias_ref, o_ref, lse_ref, *, sub_len, r, sp, qp):
    nblk = sub_len // QB_DIL
    low_half = lax.broadcasted_iota(jnp.int32, (1, LANES), 1) < HALF_LANES
    for si in range(sp):
        s = si if sp == r else pl.program_id(2) * sp + si
        for qb in range(qp):
            i = pl.program_id(1) * qp + qb
            start = jnp.clip(i * QB_DIL - HALF_WIN, 0, sub_len - KW_DIL)
            start = pl.multiple_of(start, HALF_WIN)
            variant = jnp.where(i == 0, 0, jnp.where(i == nblk - 1, 2, 1))
            q = q_ref[0, si, qb * QB_DIL:(qb + 1) * QB_DIL, :]
            kw = k_ref[0, s, pl.ds(start, KW_DIL), :]
            vw = v_ref[0, s, pl.ds(start, KW_DIL), :]
            rows = (slice(qb * QB_DIL, (qb + 1) * QB_DIL) if r == 1
                    else pl.ds(qb * QB_DIL * r + s, QB_DIL, stride=r))
            lane = lax.broadcasted_iota(jnp.int32, (QB_DIL, LANES), 1)
            lse_tile = jnp.zeros((QB_DIL, LANES), F32)
            scores = []
            for j in range(HB // 2):
                cols = slice(j * LANES, (j + 1) * LANES)
                qpair, kp = q[:, cols], kw[:, cols]
                for c in range(2):
                    qc = jnp.where(low_half if c == 0 else jnp.logical_not(low_half), qpair,
                                   jnp.zeros_like(qpair))
                    sc = lax.dot_general(qc, kp, (((1,), (1,)), ((), ())),
                                         preferred_element_type=F32)
                    scores.append(sc + bias_ref[2 * j + c, variant])
            for j in range(HB // 2):
                cols = slice(j * LANES, (j + 1) * LANES)
                vp = vw[:, cols]
                outs, lses = [], []
                for c in range(2):
                    sc = scores[2 * j + c]
                    m = jnp.max(sc, axis=-1, keepdims=True)
                    p = jnp.exp2(sc - m)
                    l = jnp.sum(p, axis=-1, keepdims=True)
                    outs.append(jnp.dot(p.astype(BF16), vp, preferred_element_type=F32) / l)
                    lses.append((m + jnp.log2(l)) * LN2)
                o_ref[0, j, rows, :] = jnp.where(low_half, outs[0], outs[1])
                for c in range(2):
                    lse_tile = jnp.where(lane == 2 * j + c, lses[c], lse_tile)
            lse_ref[0, rows, :] = lse_tile


def _attn_b(qkv4, bias3, g, cols):
    B, r, L, _ = qkv4.shape
    S = r * L
    width = HB * DB
    nblk = L // QB_DIL
    sp = min(r, SUBS_DIL)
    qp = max(1, min(ITEMS_DIL // sp, OUT_ROWS_DIL // (QB_DIL * r)))
    qcol, kcol, vcol = cols
    kern = functools.partial(_attn_b_kernel, sub_len=L, r=r, sp=sp, qp=qp)
    slab = jax.ShapeDtypeStruct((B, N_SLABS, S, LANES), F32)
    slab_spec = pl.BlockSpec((1, N_SLABS, QB_DIL * r * qp, LANES), lambda b, i, s: (b, 0, i, 0))
    return pl.pallas_call(
        kern,
        out_shape=[slab, jax.ShapeDtypeStruct((B, S, LANES), F32)],
        grid=(B, nblk // qp, r // sp),
        in_specs=[
            pl.BlockSpec((1, sp, QB_DIL * qp, width), lambda b, i, s: (b, s, i, qcol)),
            pl.BlockSpec((1, r, L, width), lambda b, i, s: (b, 0, 0, kcol)),
            pl.BlockSpec((1, r, L, width), lambda b, i, s: (b, 0, 0, vcol)),
            pl.BlockSpec((HB, 3, QB_DIL, KW_DIL), lambda b, i, s: (0, 0, 0, 0)),
        ],
        out_specs=[slab_spec, pl.BlockSpec((1, QB_DIL * r * qp, LANES), lambda b, i, s: (b, i, 0))],
        compiler_params=_cparams(("parallel", "arbitrary", "arbitrary")),
        name=f"dilated_attn_{g}",
    )(qkv4, qkv4, qkv4, bias3)


def _attn_b_bias(rel_bias, g):
    r = DILATIONS[g]
    tab = rel_bias[:, HA + g * HB: HA + (g + 1) * HB].astype(F32) * LOG2E
    off = jnp.arange(3, dtype=jnp.int32)[:, None, None] * HALF_WIN
    rel = (jnp.arange(KW_DIL, dtype=jnp.int32)[None, None, :] - off
           - jnp.arange(QB_DIL, dtype=jnp.int32)[None, :, None])
    bias = _bias_lookup(_t5_bucket(rel * r), tab)
    return jnp.where((jnp.abs(rel) <= HALF_WIN)[None], bias, NEG)


def _sgu_kernel(zu_ref, zv_ref, lng_ref, lnb_ref, ws_ref, bs_ref, o_ref):
    u = jax.nn.gelu(zu_ref[...].astype(F32))
    v = jax.nn.gelu(zv_ref[...].astype(F32))
    mu = jnp.mean(v, axis=-1, keepdims=True)
    var = jnp.mean(jnp.square(v - mu), axis=-1, keepdims=True)
    v = ((v - mu) * lax.rsqrt(var + EPS) * lng_ref[...] + lnb_ref[...]).astype(BF16)
    gd = v.shape[1] // C_GROUPS
    for n in range(v.shape[0] // CHUNK):
        rows = slice(n * CHUNK, (n + 1) * CHUNK)
        for g in range(C_GROUPS):
            cols = slice(g * gd, (g + 1) * gd)
            mixed = jnp.dot(ws_ref[g], v[rows, cols], preferred_element_type=F32) + bs_ref[:, cols]
            o_ref[rows, cols] = (u[rows, cols] * mixed).astype(o_ref.dtype)


def _sgu(proj2, ln_g, ln_b, w_s, b_exp):
    T = proj2.shape[0]
    tm = min(TM_SGU, T)
    w = MIX_W
    return pl.pallas_call(
        _sgu_kernel,
        out_shape=jax.ShapeDtypeStruct((T, w), BF16),
        grid=(T // tm,),
        in_specs=[pl.BlockSpec((tm, w), lambda i: (i, COL_ZU)),
                  pl.BlockSpec((tm, w), lambda i: (i, COL_ZU + 1)),
                  pl.BlockSpec((1, w), lambda i: (0, 0)),
                  pl.BlockSpec((1, w), lambda i: (0, 0)),
                  pl.BlockSpec((C_GROUPS, CHUNK, CHUNK), lambda i: (0, 0, 0)),
                  pl.BlockSpec((CHUNK, w), lambda i: (0, 0))],
        out_specs=pl.BlockSpec((tm, w), lambda i: (i, 0)),
        compiler_params=_cparams(("parallel",)),
        name="sgu",
    )(proj2, proj2, ln_g, ln_b, w_s, b_exp)


def _route(logits):
    lane = lax.broadcasted_iota(jnp.int32, logits.shape, 1)
    big = jnp.int32(LANES)
    is_grp = (lane >= N_EXPERTS) & (lane < N_EXPERTS + N_GROUPS)
    gl = jnp.where(is_grp, logits, NEG)
    gmax = jnp.max(gl, axis=-1, keepdims=True)
    g_idx = jnp.min(jnp.where(is_grp & (gl == gmax), lane, big), axis=-1, keepdims=True) - N_EXPERTS
    g_w = 1.0 / jnp.sum(jnp.where(is_grp, jnp.exp(gl - gmax), 0.0), axis=-1, keepdims=True)
    in_grp = (lane >= g_idx * E_PER_GROUP) & (lane < (g_idx + 1) * E_PER_GROUP)
    sel = jnp.where(in_grp, logits, NEG)
    v1 = jnp.max(sel, axis=-1, keepdims=True)
    i1 = jnp.min(jnp.where(in_grp & (sel == v1), lane, big), axis=-1, keepdims=True)
    rest = in_grp & (lane != i1)
    sel2 = jnp.where(rest, logits, NEG)
    v2 = jnp.max(sel2, axis=-1, keepdims=True)
    i2 = jnp.min(jnp.where(rest & (sel2 == v2), lane, big), axis=-1, keepdims=True)
    e2 = jnp.exp(v2 - v1)
    w1 = g_w / (1.0 + e2)
    w2 = g_w * e2 / (1.0 + e2)
    return jnp.where(lane == i1, w1, jnp.where(lane == i2, w2, 0.0))


def _mix_kernel(x_ref, ya_ref, ob0_ref, ob1_ref, ob2_ref, ls0_ref, ls1_ref, ls2_ref, yc_ref,
                g0_ref, g1_ref, g2_ref, wb_ref, wo_ref, nf_ref, wr_ref, br_ref,
                xo_ref, h_ref, comb_ref, xn_scr):
    @pl.when(pl.program_id(0) == 0)
    def _():
        xn_scr[...] = jnp.zeros(xn_scr.shape, F32)

    h = _rms_bf16(xn_scr[...], nf_ref[...])
    h_ref[...] = h
    logits = jnp.dot(h, wr_ref[...], preferred_element_type=F32) + br_ref[...]
    comb_ref[...] = _route(logits)

    ls0, ls1, ls2 = ls0_ref[...], ls1_ref[...], ls2_ref[...]
    mx = jnp.maximum(jnp.maximum(ls0, ls1), ls2)
    es = [jnp.exp(ls0 - mx), jnp.exp(ls1 - mx), jnp.exp(ls2 - mx)]
    inv = 1.0 / (es[0] + es[1] + es[2])
    spread = jnp.where(lax.broadcasted_iota(jnp.int32, (2 * LANES, MIX_W), 1) // DB
                       == lax.broadcasted_iota(jnp.int32, (2 * LANES, MIX_W), 0) % LANES,
                       1.0, 0.0).astype(BF16)
    yb = None
    for e, ob_ref in zip(es, (ob0_ref, ob1_ref, ob2_ref)):
        w = e * inv
        hi = w.astype(BF16)
        lo = (w - hi.astype(F32)).astype(BF16)
        wide = jnp.dot(jnp.concatenate([hi, lo], axis=1), spread, preferred_element_type=F32)
        term = wide * jnp.concatenate([ob_ref[0, j] for j in range(N_SLABS)], axis=-1)
        yb = term if yb is None else yb + term
    yb = yb.astype(BF16)
    merged = jax.nn.sigmoid(g0_ref[...].astype(F32)) * jnp.dot(ya_ref[...], wb_ref[0],
                                                               preferred_element_type=F32)
    merged += jax.nn.sigmoid(g1_ref[...].astype(F32)) * jnp.dot(yb, wb_ref[1],
                                                                preferred_element_type=F32)
    merged += jax.nn.sigmoid(g2_ref[...].astype(F32)) * jnp.dot(yc_ref[...], wb_ref[2],
                                                                preferred_element_type=F32)
    xn = x_ref[...] + jnp.dot(merged.astype(BF16), wo_ref[...], preferred_element_type=F32)
    xo_ref[...] = xn
    xn_scr[...] = xn


def _mix(x2, ya, obs, lses, yc, proj2, wb, wo, nf, wr, br):
    T, D = x2.shape
    S = obs[0].shape[2]
    tm = min(TM_MIX, S)
    per_b = S // tm
    n = T // tm
    w = MIX_W
    cur = lambda i: jnp.minimum(i, n - 1)
    lag = lambda i: jnp.maximum(i - 1, 0)
    row = lambda width: pl.BlockSpec((tm, width), lambda i: (cur(i), 0))
    full = lambda a: pl.BlockSpec(a.shape, lambda i: (0,) * a.ndim)
    gate = lambda k: pl.BlockSpec((tm, D), lambda i: (cur(i), COL_GATE + k))
    slab = pl.BlockSpec((1, N_SLABS, tm, LANES),
                        lambda i: (cur(i) // per_b, 0, cur(i) % per_b, 0))
    late = lambda width: pl.BlockSpec((tm, width), lambda i: (lag(i), 0))
    return pl.pallas_call(
        _mix_kernel,
        out_shape=[jax.ShapeDtypeStruct((T, D), F32), jax.ShapeDtypeStruct((T, D), BF16),
                   jax.ShapeDtypeStruct((T, LANES), F32)],
        grid=(n + 1,),
        in_specs=[row(D), row(w), slab, slab, slab, row(LANES), row(LANES), row(LANES), row(w),
                  gate(0), gate(1), gate(2), full(wb), full(wo), full(nf), full(wr), full(br)],
        out_specs=[row(D), late(D), late(LANES)],
        scratch_shapes=[pltpu.VMEM((tm, D), F32)],
        compiler_params=_cparams(("arbitrary",)),
        name="mix",
    )(x2, ya, obs[0], obs[1], obs[2], lses[0], lses[1], lses[2], yc, proj2, proj2, proj2,
      wb, wo, nf, wr, br)


def _moe_kernel(h_ref, comb_ref, x_ref, wg_ref, wu_ref, wd_ref, nfin_ref, o_ref, acc_scr,
                *, final_norm):
    e = pl.program_id(1)

    @pl.when(e == 0)
    def _():
        acc_scr[...] = jnp.zeros(acc_scr.shape, F32)

    h = h_ref[...]
    lane = lax.broadcasted_iota(jnp.int32, comb_ref.shape, 1)
    c = jnp.sum(jnp.where(lane == e, comb_ref[...], 0.0), axis=-1, keepdims=True)
    hid = (jax.nn.silu(jnp.dot(h, wg_ref[0].astype(BF16), preferred_element_type=F32))
           * jnp.dot(h, wu_ref[0].astype(BF16), preferred_element_type=F32))
    acc_scr[...] += c * jnp.dot(hid.astype(BF16), wd_ref[0].astype(BF16),
                                preferred_element_type=F32)

    @pl.when(e == pl.num_programs(1) - 1)
    def _():
        xn = x_ref[...] + acc_scr[...]
        if final_norm:
            ms = jnp.mean(xn * xn, axis=-1, keepdims=True)
            xn = xn * lax.rsqrt(ms + EPS) * nfin_ref[...]
        o_ref[...] = xn


def _moe_dense(h, comb, x2, wg, wu, wd, e0, nfin, final_norm):
    T, D = x2.shape
    tm = min(TM_MOE, T)
    F = wg.shape[2]
    kern = functools.partial(_moe_kernel, final_norm=final_norm)
    return pl.pallas_call(
        kern,
        out_shape=jax.ShapeDtypeStruct((T, D), F32),
        grid=(T // tm, N_EXPERTS),
        in_specs=[pl.BlockSpec((tm, D), lambda i, e: (i, 0)),
                  pl.BlockSpec((tm, LANES), lambda i, e: (i, 0)),
                  pl.BlockSpec((tm, D), lambda i, e: (i, 0)),
                  pl.BlockSpec((1, D, F), lambda i, e: (e0 + e, 0, 0)),
                  pl.BlockSpec((1, D, F), lambda i, e: (e0 + e, 0, 0)),
                  pl.BlockSpec((1, F, D), lambda i, e: (e0 + e, 0, 0)),
                  pl.BlockSpec((1, D), lambda i, e: (0, 0))],
        out_specs=pl.BlockSpec((tm, D), lambda i, e: (i, 0)),
        scratch_shapes=[pltpu.VMEM((tm, D), F32)],
        compiler_params=_cparams(("parallel", "arbitrary")),
        name="moe_dense",
    )(h, comb, x2, wg, wu, wd, nfin)


def _moe_dispatch_kernel(h_ref, comb_ref, o_ref, cnt_ref):
    nt = o_ref.shape[0]
    tm = h_ref.shape[0] // nt
    before = jnp.where(lax.broadcasted_iota(jnp.int32, (tm, tm), 0)
                       < lax.broadcasted_iota(jnp.int32, (tm, tm), 1), 1.0, 0.0).astype(BF16)
    slot = lax.broadcasted_iota(jnp.int32, (MOE_CAP, tm), 0).astype(F32)
    for u in range(nt):
        rows = slice(u * tm, (u + 1) * tm)
        comb = comb_ref[rows, :]
        hi = comb.astype(BF16).astype(F32)
        wcols = (hi + pltpu.roll(comb - hi, N_EXPERTS, axis=1)).astype(BF16)
        haug = jnp.concatenate([h_ref[rows, :], wcols], axis=1)
        a_t = comb.T[:N_EXPERTS] > 0.0
        a_f = jnp.where(a_t, 1.0, 0.0)
        rank_t = jnp.dot(a_f.astype(BF16), before, preferred_element_type=F32)
        blocks = [jnp.where((slot == rank_t[e:e + 1]) & a_t[e:e + 1], 1.0, 0.0).astype(BF16)
                  for e in range(N_EXPERTS)]
        res = jnp.dot(jnp.concatenate(blocks, axis=0), haug, preferred_element_type=F32)
        res = res.astype(o_ref.dtype)
        for e in range(N_EXPERTS):
            o_ref[u, e] = res[e * MOE_CAP:(e + 1) * MOE_CAP]
        cnt_ref[u] = jnp.broadcast_to(jnp.sum(a_f, axis=1, keepdims=True), cnt_ref.shape[1:])


def _moe_dispatch(h, comb):
    T, D = h.shape
    tm = min(TM_DISP, T)
    n = T // tm
    nt = math.gcd(TILES_PER_STEP, n)
    return pl.pallas_call(
        _moe_dispatch_kernel,
        out_shape=[jax.ShapeDtypeStruct((n, N_EXPERTS, MOE_CAP, D + LANES), BF16),
                   jax.ShapeDtypeStruct((n, N_EXPERTS, LANES), F32)],
        grid=(n // nt,),
        in_specs=[pl.BlockSpec((nt * tm, D), lambda i: (i, 0)),
                  pl.BlockSpec((nt * tm, LANES), lambda i: (i, 0))],
        out_specs=[pl.BlockSpec((nt, N_EXPERTS, MOE_CAP, D + LANES), lambda i: (i, 0, 0, 0)),
                   pl.BlockSpec((nt, N_EXPERTS, LANES), lambda i: (i, 0, 0))],
        compiler_params=_cparams(("parallel",)),
        name="moe_dispatch",
    )(h, comb)


def _moe_ffn_kernel(n16_ref, s_ref, wg_ref, wu_ref, wd_ref, o_ref,
                    wg_scr, wu_scr, wd_scr, lhs_scr, y_scr):
    e, c = pl.program_id(0), pl.program_id(1)

    @pl.when(c == 0)
    def _():
        wg_scr[...] = wg_ref[0].astype(BF16)
        wu_scr[...] = wu_ref[0].astype(BF16)
        wd_scr[...] = wd_ref[0].astype(BF16)

    g, _, cap, _ = s_ref.shape
    D = o_ref.shape[-1]
    @pl.when((e == 0) & (c == 0))
    def _():
        lhs_scr[...] = jnp.zeros(lhs_scr.shape, lhs_scr.dtype)
        y_scr[...] = jnp.zeros(y_scr.shape, y_scr.dtype)

    offs = []
    off = jnp.int32(0)
    for t in range(g):
        offs.append(off)
        lhs_scr[pl.ds(pl.multiple_of(off, BF16_ROWS), cap), :] = s_ref[t, 0]
        off = off + n16_ref[(c * g + t) * N_EXPERTS + e]
    total = off

    def run(nrows):
        rows = lhs_scr[:nrows]
        h = rows[:, :D]
        wparts = rows[:, D:].astype(F32)
        lane = lax.broadcasted_iota(jnp.int32, wparts.shape, 1)
        w = jnp.sum(jnp.where(lane % N_EXPERTS == e, wparts, 0.0), axis=-1, keepdims=True)
        hid = (jax.nn.silu(jnp.dot(h, wg_scr[...], preferred_element_type=F32))
               * jnp.dot(h, wu_scr[...], preferred_element_type=F32))
        y = w * jnp.dot(hid.astype(BF16), wd_scr[...], preferred_element_type=F32)
        y_scr[:nrows] = y.astype(y_scr.dtype)

    classes = tuple(range(g * cap // 2, g * cap + 1, FFN_ROW_STEP))
    lower = 0
    for nrows in classes:
        pl.when((total > lower) & (total <= nrows))(functools.partial(run, nrows))
        lower = nrows

    for t in range(g):
        o_ref[t, 0] = y_scr[pl.ds(pl.multiple_of(offs[t], BF16_ROWS), cap), :]


def _moe_ffn(srt, n16, wg, wu, wd, e0):
    n, ne, cap, wdt = srt.shape
    D, F = wg.shape[1], wg.shape[2]
    g = math.gcd(G_FFN, n)
    return pl.pallas_call(
        _moe_ffn_kernel,
        out_shape=jax.ShapeDtypeStruct((n, ne, cap, D), BF16),
        grid_spec=pltpu.PrefetchScalarGridSpec(
            num_scalar_prefetch=1,
            grid=(ne, n // g),
            in_specs=[pl.BlockSpec((g, 1, cap, wdt), lambda e, c, n16: (c, e, 0, 0)),
                      pl.BlockSpec((1, D, F), lambda e, c, n16: (e0 + e, 0, 0)),
                      pl.BlockSpec((1, D, F), lambda e, c, n16: (e0 + e, 0, 0)),
                      pl.BlockSpec((1, F, D), lambda e, c, n16: (e0 + e, 0, 0))],
            out_specs=pl.BlockSpec((g, 1, cap, D), lambda e, c, n16: (c, e, 0, 0)),
            scratch_shapes=[pltpu.VMEM((D, F), BF16), pltpu.VMEM((D, F), BF16),
                            pltpu.VMEM((F, D), BF16), pltpu.VMEM((g * cap, wdt), BF16),
                            pltpu.VMEM((g * cap, D), BF16)]),
        compiler_params=_cparams(("arbitrary", "arbitrary")),
        name="moe_ffn",
    )(n16, srt, wg, wu, wd)


def _moe_combine_kernel(skip_ref, y_ref, comb_ref, x_ref, nfin_ref, o_ref, *, final_norm):
    nt = y_ref.shape[0]
    tm = x_ref.shape[0] // nt
    ncol = N_EXPERTS * MOE_CAP
    before = jnp.where(lax.broadcasted_iota(jnp.int32, (tm, tm), 1)
                       < lax.broadcasted_iota(jnp.int32, (tm, tm), 0), 1.0, 0.0).astype(BF16)
    spread = jnp.where(lax.broadcasted_iota(jnp.int32, (LANES, ncol), 1) // MOE_CAP
                       == lax.broadcasted_iota(jnp.int32, (LANES, ncol), 0), 1.0, 0.0).astype(BF16)
    slot = (lax.broadcasted_iota(jnp.int32, (tm, ncol), 1) % MOE_CAP).astype(F32)
    for u in range(nt):
        rows = slice(u * tm, (u + 1) * tm)
        a = comb_ref[rows, :] > 0.0
        rank = jnp.dot(before, jnp.where(a, 1.0, 0.0).astype(BF16), preferred_element_type=F32)
        key = jnp.where(a, rank, -1.0).astype(BF16)
        key_all = jnp.dot(key, spread, preferred_element_type=F32)
        pc = jnp.where(slot == key_all, 1.0, 0.0).astype(BF16)
        y = jnp.concatenate([y_ref[u, e] for e in range(N_EXPERTS)], axis=0)
        xn = x_ref[rows, :] + jnp.dot(pc, y, preferred_element_type=F32)
        if final_norm:
            ms = jnp.mean(xn * xn, axis=-1, keepdims=True)
            later = skip_ref[pl.program_id(0) * nt + u] == 1
            xn = jnp.where(later, xn, xn * lax.rsqrt(ms + EPS) * nfin_ref[...])
        o_ref[rows, :] = xn


def _moe_combine(skip, y, comb, x2, nfin, final_norm):
    T, D = x2.shape
    n, ne, cap, _ = y.shape
    tm = T // n
    nt = math.gcd(TILES_PER_STEP, n)
    kern = functools.partial(_moe_combine_kernel, final_norm=final_norm)
    return pl.pallas_call(
        kern,
        out_shape=jax.ShapeDtypeStruct((T, D), F32),
        grid_spec=pltpu.PrefetchScalarGridSpec(
            num_scalar_prefetch=1,
            grid=(n // nt,),
            in_specs=[pl.BlockSpec((nt, ne, cap, D), lambda i, sk: (i, 0, 0, 0)),
                      pl.BlockSpec((nt * tm, LANES), lambda i, sk: (i, 0)),
                      pl.BlockSpec((nt * tm, D), lambda i, sk: (i, 0)),
                      pl.BlockSpec((1, D), lambda i, sk: (0, 0))],
            out_specs=pl.BlockSpec((nt * tm, D), lambda i, sk: (i, 0))),
        compiler_params=_cparams(("parallel",)),
        name="moe_combine",
    )(skip, y, comb, x2, nfin)


def _moe_fix_kernel(tiles_ref, experts_ref, first_ref, last_ref, n_ref, h_ref, comb_ref, prev_ref,
                    wg_ref, wu_ref, wd_ref, nfin_ref, o_ref, *, final_norm):
    del tiles_ref
    s = pl.program_id(0)

    @pl.when(s < n_ref[0])
    def _():
        e = experts_ref[s]
        tm = h_ref.shape[0]
        comb = comb_ref[...]
        a = jnp.where(comb > 0.0, 1.0, 0.0)
        before = (lax.broadcasted_iota(jnp.int32, (tm, tm), 1)
                  < lax.broadcasted_iota(jnp.int32, (tm, tm), 0))
        rank = jnp.dot(jnp.where(before, 1.0, 0.0).astype(BF16), a.astype(BF16),
                       preferred_element_type=F32)
        lane = lax.broadcasted_iota(jnp.int32, comb.shape, 1)
        dropped = (lane == e) & (rank >= MOE_CAP)
        c = jnp.sum(jnp.where(dropped, comb, 0.0), axis=-1, keepdims=True)
        h = h_ref[...]
        hid = (jax.nn.silu(jnp.dot(h, wg_ref[0].astype(BF16), preferred_element_type=F32))
               * jnp.dot(h, wu_ref[0].astype(BF16), preferred_element_type=F32))
        add = c * jnp.dot(hid.astype(BF16), wd_ref[0].astype(BF16), preferred_element_type=F32)
        fresh = first_ref[s] == 1

        @pl.when(fresh)
        def _():
            o_ref[...] = prev_ref[...] + add

        @pl.when(jnp.logical_not(fresh))
        def _():
            o_ref[...] += add

        if final_norm:
            @pl.when(last_ref[s] == 1)
            def _():
                xn = o_ref[...]
                ms = jnp.mean(xn * xn, axis=-1, keepdims=True)
                o_ref[...] = xn * lax.rsqrt(ms + EPS) * nfin_ref[...]


def _moe_fix(tiles, experts, first, last, n, out, h, comb, wg, wu, wd, e0, nfin, final_norm):
    T, D = out.shape
    tm = min(TM_DISP, T)
    F = wg.shape[2]
    tile = lambda width: pl.BlockSpec((tm, width), lambda s, tl, ex, fi, la, n: (tl[s], 0))
    wspec = lambda shape: pl.BlockSpec(shape, lambda s, tl, ex, fi, la, n: (e0 + ex[s], 0, 0))
    kern = functools.partial(_moe_fix_kernel, final_norm=final_norm)
    return pl.pallas_call(
        kern,
        out_shape=jax.ShapeDtypeStruct((T, D), F32),
        grid_spec=pltpu.PrefetchScalarGridSpec(
            num_scalar_prefetch=5,
            grid=(MAX_OVF,),
            in_specs=[tile(D), tile(LANES), tile(D), wspec((1, D, F)), wspec((1, D, F)),
                      wspec((1, F, D)), pl.BlockSpec((1, D), lambda s, tl, ex, fi, la, n: (0, 0))],
            out_specs=tile(D)),
        input_output_aliases={7: 0},
        compiler_params=_cparams(("arbitrary",)),
        name="moe_fix",
    )(tiles, experts, first, last, n, h, comb, out, wg, wu, wd, nfin)


def _moe(h, comb, x2, wg, wu, wd, e0, nfin, final_norm):
    srt, cnt = _moe_dispatch(h, comb)
    over = (cnt[:, :, 0] > MOE_CAP).reshape(-1)
    n_ovf = jnp.sum(over.astype(jnp.int32))
    pairs = jnp.nonzero(over, size=MAX_OVF, fill_value=0)[0].astype(jnp.int32)
    pairs = jnp.where(jnp.arange(MAX_OVF) < n_ovf, pairs, pairs[jnp.clip(n_ovf - 1, 0, MAX_OVF - 1)])
    tiles, experts = pairs // N_EXPERTS, pairs % N_EXPERTS
    change = (tiles[1:] != tiles[:-1]).astype(jnp.int32)
    first = jnp.concatenate([jnp.ones((1,), jnp.int32), change])
    last = jnp.maximum(jnp.concatenate([change, jnp.ones((1,), jnp.int32)]),
                       (jnp.arange(MAX_OVF) == n_ovf - 1).astype(jnp.int32))
    skip = jnp.any(over.reshape(-1, N_EXPERTS), axis=1).astype(jnp.int32)

    used = jnp.minimum(cnt[:, :, 0], MOE_CAP).astype(jnp.int32).reshape(-1)
    n16 = (used + (BF16_ROWS - 1)) // BF16_ROWS * BF16_ROWS

    def routed():
        out = _moe_combine(skip, _moe_ffn(srt, n16, wg, wu, wd, e0), comb, x2, nfin, final_norm)
        return lax.cond(
            n_ovf > 0,
            lambda: _moe_fix(tiles, experts, first, last, n_ovf.reshape(1), out, h, comb,
                             wg, wu, wd, e0, nfin, final_norm),
            lambda: out)

    return lax.cond(n_ovf > MAX_OVF,
                    lambda: _moe_dense(h, comb, x2, wg, wu, wd, e0, nfin, final_norm), routed)


def kernel(x, rel_bias, norm_mix, w_in, diff_lambda, diff_subln, sgu_ln_g, sgu_ln_b, sgu_w, sgu_b,
           w_branch, w_out, norm_ffn, w_router_grp, b_router_grp, w_router_exp, b_router_exp,
           w_gate, w_up, w_down, norm_final):
    B, S, D = x.shape
    T = B * S
    depth = w_in.shape[0]
    a_out = HA * 2 * DA
    grp_w = HB * DB
    b_cols = 3 * NG_B * grp_w
    qkv_b0 = 3 * a_out
    zc0 = qkv_b0 + b_cols
    gate0 = zc0 + 2 * MIX_W
    qk_scale = DA ** -0.5

    bias_a, cfar = _attn_a_bias(rel_bias)
    bias_b = [_attn_b_bias(rel_bias, g) for g in range(NG_B)]

    col = jnp.arange(w_in.shape[2])
    is_q = (col < a_out) | ((col >= qkv_b0) & (col < qkv_b0 + NG_B * grp_w))
    col_scale = jnp.where(is_q, qk_scale * LOG2E, 1.0)

    def group_cols(w, g):
        return [w[:, qkv_b0 + (c * NG_B + g) * grp_w: qkv_b0 + (c * NG_B + g + 1) * grp_w]
                for c in range(3)]

    wg_all = w_gate.reshape((-1,) + w_gate.shape[2:])
    wu_all = w_up.reshape((-1,) + w_up.shape[2:])
    wd_all = w_down.reshape((-1,) + w_down.shape[2:])

    x2 = x.reshape(T, D)
    for i in range(depth):
        w = (w_in[i] * col_scale.astype(F32)).astype(BF16)
        nm = norm_mix[i][None, :]
        w_main = jnp.concatenate([w[:, :2 * a_out], w[:, zc0:]] + group_cols(w, 0), axis=1)
        x3 = x2.reshape(B, S, D)
        proj3, vt = _inproj(x3, nm, w_main, w[:, 2 * a_out:3 * a_out].T)
        proj2 = proj3.reshape(T, proj3.shape[2])

        lam_init = 0.8 - 0.6 * math.exp(-0.3 * i)
        lp = diff_lambda[i].astype(F32)
        lam = jnp.exp(jnp.sum(lp[0] * lp[1])) - jnp.exp(jnp.sum(lp[2] * lp[3])) + lam_init
        ya = _attn_a(proj3, vt, lam.reshape(1), cfar, bias_a, diff_subln[i][None, :], lam_init)

        strided = [g for g in range(NG_B) if DILATIONS[g] > 1]
        permuted = dict(zip(strided, _inproj_perm(
            x3, nm, [jnp.concatenate(group_cols(w, g), axis=1) for g in strided],
            [DILATIONS[g] for g in strided])))
        obs, lses = [], []
        for g in range(NG_B):
            if g in permuted:
                qkv4, cols = permuted[g], (0, 1, 2)
            else:
                qkv4, cols = proj3[:, None], (COL_QKV0, COL_QKV0 + 1, COL_QKV0 + 2)
            o, l = _attn_b(qkv4, bias_b[g], g, cols)
            obs.append(o)
            lses.append(l.reshape(T, LANES))

        b_exp = jnp.repeat(sgu_b[i].T, MIX_W // C_GROUPS, axis=1)
        yc = _sgu(proj2, sgu_ln_g[i][None, :], sgu_ln_b[i][None, :], sgu_w[i].astype(BF16), b_exp)

        wr = jnp.concatenate([w_router_exp[i].transpose(1, 0, 2).reshape(D, N_EXPERTS),
                              w_router_grp[i]], axis=1)
        wr = jnp.pad(wr, ((0, 0), (0, LANES - wr.shape[1]))).astype(BF16)
        br = jnp.concatenate([b_router_exp[i].reshape(N_EXPERTS), b_router_grp[i]])
        br = jnp.pad(br, (0, LANES - br.shape[0]))[None, :].astype(F32)

        x2, h, comb = _mix(x2, ya.reshape(T, a_out), obs, lses, yc, proj2,
                           w_branch[i].astype(BF16), w_out[i].astype(BF16), norm_ffn[i][None, :],
                           wr, br)
        x2 = _moe(h, comb, x2, wg_all, wu_all, wd_all, i * N_EXPERTS, norm_final[None, :],
                  i == depth - 1)
    return x2.reshape(B, S, D)
```

```python
import functools
import math

import jax
import jax.numpy as jnp
from jax import lax
from jax.experimental import pallas as pl
from jax.experimental.pallas import tpu as pltpu

F32 = jnp.float32
BF16 = jnp.bfloat16

EPS = 1e-6
NEG = -1e30
LOG2E = 1.4426950408889634
LN2 = 0.6931471805599453
LANES = 128
HALF_LANES = LANES // 2
VMEM_LIMIT = 48 * 1024 * 1024

HA = 4
DA = 64
MIX_W = 512
WINDOWS = (128, 512, 2048)
DILATIONS = (1, 4, 16)
NG_B = 3
HB = 8
DB = 64
HALF_WIN = 64
CHUNK = 128
C_GROUPS = 4
N_BRANCH = 3
N_BUCKETS = 32
MAX_DIST = 128
N_GROUPS = 4
E_PER_GROUP = 4
N_EXPERTS = N_GROUPS * E_PER_GROUP
N_SLABS = MIX_W // LANES

TM_PROJ = 1024
TN_PROJ = 3328
TM_PERM = 1024
PERM_BLK = 256
T_ATT = 512
QB_DIL = 128
KW_DIL = QB_DIL + 2 * HALF_WIN
ITEMS_DIL = 8
SUBS_DIL = 4
OUT_ROWS_DIL = 2048
TM_SGU = 2048
TM_MIX = 512
TM_MOE = 1024
TM_DISP = 256
MOE_CAP = HALF_LANES
G_FFN = 16
TILES_PER_STEP = 4
FFN_ROW_STEP = 64
BF16_ROWS = 16
MAX_OVF = 64

COL_ZU = 2
COL_GATE = 2
COL_QKV0 = 10


def _cparams(sem):
    return pltpu.CompilerParams(dimension_semantics=sem, vmem_limit_bytes=VMEM_LIMIT)


def _t5_bucket(rel):
    nb = N_BUCKETS // 2
    max_exact = nb // 2
    ret = (rel > 0).astype(jnp.int32) * nb
    n = jnp.abs(rel)
    nf = jnp.maximum(n, 1).astype(F32)
    large = max_exact + (jnp.log(nf / max_exact) / math.log(MAX_DIST / max_exact)
                         * (nb - max_exact)).astype(jnp.int32)
    large = jnp.minimum(large, nb - 1)
    return ret + jnp.where(n < max_exact, n, large)


def _bias_lookup(bucket, tab):
    out = jnp.zeros((tab.shape[1],) + bucket.shape, F32)
    expand = (slice(None),) + (None,) * bucket.ndim
    for b in range(N_BUCKETS):
        out = jnp.where(bucket[None] == b, tab[b][expand], out)
    return out


def _rms_bf16(x, g):
    ms = jnp.mean(x * x, axis=-1, keepdims=True)
    return (x * lax.rsqrt(ms + EPS) * g).astype(BF16)


def _inproj_kernel(x_ref, g_ref, w_ref, wt_ref, o_ref, vt_ref, h_scr):
    @pl.when(pl.program_id(2) == 0)
    def _():
        h = _rms_bf16(x_ref[0], g_ref[...])
        h_scr[...] = h
        res = lax.dot_general(wt_ref[...], h, (((1,), (1,)), ((), ())),
                              preferred_element_type=F32).astype(vt_ref.dtype)
        for hd in range(vt_ref.shape[1]):
            for n in range(vt_ref.shape[2]):
                vt_ref[0, hd, n] = res[hd * LANES:(hd + 1) * LANES, n * T_ATT:(n + 1) * T_ATT]

    o_ref[0] = jnp.dot(h_scr[...], w_ref[...], preferred_element_type=F32).astype(o_ref.dtype)


def _inproj(x3, g, w, wt):
    B, S, D = x3.shape
    N = w.shape[1]
    tm = min(TM_PROJ, S)
    nh, nb = wt.shape[0] // LANES, tm // T_ATT
    return pl.pallas_call(
        _inproj_kernel,
        out_shape=[jax.ShapeDtypeStruct((B, S, N), BF16),
                   jax.ShapeDtypeStruct((B, nh, S // T_ATT, LANES, T_ATT), BF16)],
        grid=(B, S // tm, N // TN_PROJ),
        in_specs=[pl.BlockSpec((1, tm, D), lambda b, i, j: (b, i, 0)),
                  pl.BlockSpec((1, D), lambda b, i, j: (0, 0)),
                  pl.BlockSpec((D, TN_PROJ), lambda b, i, j: (0, j)),
                  pl.BlockSpec(wt.shape, lambda b, i, j: (0, 0), pipeline_mode=pl.Buffered(1))],
        out_specs=[pl.BlockSpec((1, tm, TN_PROJ), lambda b, i, j: (b, i, j)),
                   pl.BlockSpec((1, nh, nb, LANES, T_ATT), lambda b, i, j: (b, 0, i, 0, 0))],
        scratch_shapes=[pltpu.VMEM((tm, D), BF16)],
        compiler_params=_cparams(("parallel", "parallel", "arbitrary")),
        name="inproj",
    )(x3, g, w, wt)


def _inproj_perm_kernel(x_ref, g_ref, *refs, dilations):
    ng = len(dilations)
    p_refs, w_refs, o_refs = refs[:ng], refs[ng:2 * ng], refs[2 * ng:]
    h = _rms_bf16(x_ref[0], g_ref[...])
    nblk = h.shape[0] // PERM_BLK
    for r, p_ref, w_ref, o_ref in zip(dilations, p_refs, w_refs, o_refs):
        hp = jnp.concatenate(
            [jnp.dot(p_ref[...], h[k * PERM_BLK:(k + 1) * PERM_BLK], preferred_element_type=F32)
             for k in range(nblk)], axis=0).astype(BF16)
        res = jnp.dot(hp, w_ref[...], preferred_element_type=F32).astype(o_ref.dtype)
        n = PERM_BLK // r
        for k in range(nblk):
            for s in range(r):
                o_ref[0, s, k * n:(k + 1) * n, :] = res[k * PERM_BLK + s * n:k * PERM_BLK + (s + 1) * n, :]


def _inproj_perm(x3, g, ws, dilations):
    B, S, D = x3.shape
    tm = min(TM_PERM, S)
    perms = []
    for r in dilations:
        n = PERM_BLK // r
        o = jnp.arange(PERM_BLK, dtype=jnp.int32)
        src = (o % n) * r + o // n
        perms.append((src[:, None] == jnp.arange(PERM_BLK, dtype=jnp.int32)[None, :]).astype(BF16))
    kern = functools.partial(_inproj_perm_kernel, dilations=tuple(dilations))
    return pl.pallas_call(
        kern,
        out_shape=[jax.ShapeDtypeStruct((B, r, S // r, w.shape[1]), BF16)
                   for r, w in zip(dilations, ws)],
        grid=(B, S // tm),
        in_specs=([pl.BlockSpec((1, tm, D), lambda b, i: (b, i, 0)),
                   pl.BlockSpec((1, D), lambda b, i: (0, 0))]
                  + [pl.BlockSpec((PERM_BLK, PERM_BLK), lambda b, i: (0, 0),
                                  pipeline_mode=pl.Buffered(1)) for _ in dilations]
                  + [pl.BlockSpec(w.shape, lambda b, i: (0, 0), pipeline_mode=pl.Buffered(1))
                     for w in ws]),
        out_specs=[pl.BlockSpec((1, r, tm // r, w.shape[1]), lambda b, i: (b, 0, i, 0))
                   for r, w in zip(dilations, ws)],
        compiler_params=_cparams(("parallel", "parallel")),
        name="inproj_perm",
    )(x3, g, *perms, *ws)


def _attn_a_kernel(lam_ref, cfar_ref, q_ref, k_ref, vt_ref, bias_ref, g_ref, o_ref,
                   st0_scr, st1_scr, m0_scr, m1_scr, acc_scr, l_scr, *, out_scale, nq, n_blocks):
    k = pl.program_id(0)
    t = T_ATT
    nk = k_ref.shape[1] // t
    n1 = jnp.minimum(k // 2, n_blocks - 1)
    n2 = jnp.maximum(k - 1, 0) // 2
    h1, qi1 = (n1 // nq) % HA, n1 % nq
    h2, qi2 = (n2 // nq) % HA, n2 % nq

    @pl.when(k == 0)
    def _():
        st1_scr[...] = jnp.zeros(st1_scr.shape, F32)
        m1_scr[...] = jnp.zeros(m1_scr.shape, F32)
        acc_scr[...] = jnp.zeros(acc_scr.shape, F32)
        l_scr[...] = jnp.ones(l_scr.shape, F32)

    low_half = lax.broadcasted_iota(jnp.int32, (1, LANES), 1) < HALF_LANES

    def both(cmap, st_w, m_w, st_r, m_r):
        q = q_ref[0]
        zero = jnp.zeros_like(q)
        qc = jnp.where(low_half, q, zero) if cmap == 0 else jnp.where(low_half, zero, q)
        m_prev = m_r[...]
        l = jnp.zeros((1, t), F32)
        acc = jnp.zeros((LANES, t), F32)
        m_new = None
        for j, d in enumerate(range(-1, nk - 1)):
            a2 = lax.rem(qi2 + (d + nk), nk)
            if d <= 1:
                shifted = m_prev
            else:
                shifted = m_prev - cfar_ref[2 * h2 + (a2 > qi2).astype(jnp.int32)]
            p = jnp.exp2(st_r[j] - shifted)
            l = l + jnp.sum(p, axis=0, keepdims=True)
            acc = acc + jnp.dot(vt_ref[0, 0, a2], p.astype(BF16), preferred_element_type=F32)

            a1 = lax.rem(qi1 + (d + nk), nk)
            delta1 = a1 - qi1
            kb = k_ref[0, pl.ds(pl.multiple_of(a1 * t, t), t), :]
            st = lax.dot_general(kb, qc, (((1,), (1,)), ((), ())), preferred_element_type=F32)
            if d <= 1:
                st = st + bias_ref[0, jnp.clip(delta1, -2, 2) + 2]
                cm = jnp.max(st, axis=0, keepdims=True)
            else:
                cm = (jnp.max(st, axis=0, keepdims=True)
                      + cfar_ref[2 * h1 + (delta1 > 0).astype(jnp.int32)])
            st_w[j] = st
            m_new = cm if m_new is None else jnp.maximum(m_new, cm)
        m_w[...] = m_new
        return l, acc

    @pl.when(k % 2 == 0)
    def _():
        l1, acc1 = both(0, st0_scr, m0_scr, st1_scr, m1_scr)
        ot = acc_scr[...] / l_scr[...] - lam_ref[0] * (acc1 / l1)
        o = ot.T
        ms = jnp.mean(o * o, axis=-1, keepdims=True)
        o_ref[0] = (o * lax.rsqrt(ms + EPS) * g_ref[...] * out_scale).astype(o_ref.dtype)

    @pl.when(k % 2 == 1)
    def _():
        l0, acc0 = both(1, st1_scr, m1_scr, st0_scr, m0_scr)
        acc_scr[...] = acc0
        l_scr[...] = l0


def _attn_a(proj3, vt, lam, cfar, bias5, subln_g, lam_init):
    B, S, _ = proj3.shape
    t = T_ATT
    nk = S // t
    n_blocks = B * HA * nk

    def scored(k):
        n = jnp.minimum(k // 2, n_blocks - 1)
        return n // (HA * nk), (n // nk) % HA, n % nk

    def lagged(k, lag):
        n = jnp.maximum(k - lag, 0) // 2
        return n // (HA * nk), (n // nk) % HA, n % nk

    def q_map(k):
        b, h, qi = scored(k)
        return b, qi, h

    def k_map(k):
        b, h, _ = scored(k)
        return b, 0, HA + h

    def vt_map(k):
        b, h, _ = lagged(k, 1)
        return b, h, 0, 0, 0

    def out_map(k):
        b, h, qi = lagged(k, 2)
        return b, qi, h

    kern = functools.partial(_attn_a_kernel, out_scale=1.0 - lam_init, nq=nk, n_blocks=n_blocks)
    return pl.pallas_call(
        kern,
        out_shape=jax.ShapeDtypeStruct((B, S, HA * 2 * DA), BF16),
        grid=(2 * n_blocks + 1,),
        in_specs=[
            pl.BlockSpec(memory_space=pltpu.SMEM),
            pl.BlockSpec(memory_space=pltpu.SMEM),
            pl.BlockSpec((1, t, LANES), q_map),
            pl.BlockSpec((1, S, LANES), k_map),
            pl.BlockSpec((1, 1, nk, LANES, t), vt_map),
            pl.BlockSpec((1, 5, t, t), lambda k: (scored(k)[1], 0, 0, 0)),
            pl.BlockSpec((1, LANES), lambda k: (0, 0)),
        ],
        out_specs=pl.BlockSpec((1, t, LANES), out_map),
        scratch_shapes=[pltpu.VMEM((nk, t, t), F32), pltpu.VMEM((nk, t, t), F32),
                        pltpu.VMEM((1, t), F32), pltpu.VMEM((1, t), F32),
                        pltpu.VMEM((LANES, t), F32), pltpu.VMEM((1, t), F32)],
        compiler_params=_cparams(("arbitrary",)),
        name="diff_attn",
    )(lam, cfar, proj3, proj3, vt, bias5, subln_g)


def _attn_a_bias(rel_bias):
    t = T_ATT
    tab = rel_bias[:, :HA].astype(F32) * LOG2E
    d = jnp.arange(-1, 2, dtype=jnp.int32)[:, None, None] * t
    rel = d + jnp.arange(t, dtype=jnp.int32)[None, :, None] - jnp.arange(t, dtype=jnp.int32)[None, None, :]
    near = _bias_lookup(_t5_bucket(rel), tab)
    far = tab[_t5_bucket(jnp.array([-(t + 1), t + 1], dtype=jnp.int32))].T
    fill = lambda side: jnp.broadcast_to(far[:, side, None, None, None], (HA, 1, t, t))
    tiles = jnp.concatenate([fill(0), near, fill(1)], axis=1)
    return tiles, far.reshape(2 * HA)


def _attn_b_kernel(q_ref, k_ref, v_ref, bias_ref, o_ref, lse_ref, *, sub_len, r, sp, qp):
    nblk = sub_len // QB_DIL
    low_half = lax.broadcasted_iota(jnp.int32, (1, LANES), 1) < HALF_LANES
    for si in range(sp):
        s = si if sp == r else pl.program_id(2) * sp + si
        for qb in range(qp):
            i = pl.program_id(1) * qp + qb
            start = jnp.clip(i * QB_DIL - HALF_WIN, 0, sub_len - KW_DIL)
            start = pl.multiple_of(start, HALF_WIN)
            variant = jnp.where(i == 0, 0, jnp.where(i == nblk - 1, 2, 1))
            q = q_ref[0, si, qb * QB_DIL:(qb + 1) * QB_DIL, :]
            kw = k_ref[0, s, pl.ds(start, KW_DIL), :]
            vw = v_ref[0, s, pl.ds(start, KW_DIL), :]
            rows = (slice(qb * QB_DIL, (qb + 1) * QB_DIL) if r == 1
                    else pl.ds(qb * QB_DIL * r + s, QB_DIL, stride=r))
            lane = lax.broadcasted_iota(jnp.int32, (QB_DIL, LANES), 1)
            lse_tile = jnp.zeros((QB_DIL, LANES), F32)
            scores = []
            for j in range(HB // 2):
                cols = slice(j * LANES, (j + 1) * LANES)
                qpair, kp = q[:, cols], kw[:, cols]
                for c in range(2):
                    qc = jnp.where(low_half if c == 0 else jnp.logical_not(low_half), qpair,
                                   jnp.zeros_like(qpair))
                    sc = lax.dot_general(qc, kp, (((1,), (1,)), ((), ())),
                                         preferred_element_type=F32)
                    scores.append(sc + bias_ref[2 * j + c, variant])
            for j in range(HB // 2):
                cols = slice(j * LANES, (j + 1) * LANES)
                vp = vw[:, cols]
                outs, lses = [], []
                for c in range(2):
                    sc = scores[2 * j + c]
                    m = jnp.max(sc, axis=-1, keepdims=True)
                    p = jnp.exp2(sc - m)
                    l = jnp.sum(p, axis=-1, keepdims=True)
                    outs.append(jnp.dot(p.astype(BF16), vp, preferred_element_type=F32) / l)
                    lses.append((m + jnp.log2(l)) * LN2)
                o_ref[0, j, rows, :] = jnp.where(low_half, outs[0], outs[1])
                for c in range(2):
                    lse_tile = jnp.where(lane == 2 * j + c, lses[c], lse_tile)
            lse_ref[0, rows, :] = lse_tile


def _attn_b(qkv4, bias3, g, cols):
    B, r, L, _ = qkv4.shape
    S = r * L
    width = HB * DB
    nblk = L // QB_DIL
    sp = min(r, SUBS_DIL)
    qp = max(1, min(ITEMS_DIL // sp, OUT_ROWS_DIL // (QB_DIL * r)))
    qcol, kcol, vcol = cols
    kern = functools.partial(_attn_b_kernel, sub_len=L, r=r, sp=sp, qp=qp)
    slab = jax.ShapeDtypeStruct((B, N_SLABS, S, LANES), F32)
    slab_spec = pl.BlockSpec((1, N_SLABS, QB_DIL * r * qp, LANES), lambda b, i, s: (b, 0, i, 0))
    return pl.pallas_call(
        kern,
        out_shape=[slab, jax.ShapeDtypeStruct((B, S, LANES), F32)],
        grid=(B, nblk // qp, r // sp),
        in_specs=[
            pl.BlockSpec((1, sp, QB_DIL * qp, width), lambda b, i, s: (b, s, i, qcol)),
            pl.BlockSpec((1, r, L, width), lambda b, i, s: (b, 0, 0, kcol)),
            pl.BlockSpec((1, r, L, width), lambda b, i, s: (b, 0, 0, vcol)),
            pl.BlockSpec((HB, 3, QB_DIL, KW_DIL), lambda b, i, s: (0, 0, 0, 0),
                         pipeline_mode=pl.Buffered(1)),
        ],
        out_specs=[slab_spec, pl.BlockSpec((1, QB_DIL * r * qp, LANES), lambda b, i, s: (b, i, 0))],
        compiler_params=_cparams(("parallel", "arbitrary", "arbitrary")),
        name=f"dilated_attn_{g}",
    )(qkv4, qkv4, qkv4, bias3)


def _attn_b_bias(rel_bias, g):
    r = DILATIONS[g]
    tab = rel_bias[:, HA + g * HB: HA + (g + 1) * HB].astype(F32) * LOG2E
    off = jnp.arange(3, dtype=jnp.int32)[:, None, None] * HALF_WIN
    rel = (jnp.arange(KW_DIL, dtype=jnp.int32)[None, None, :] - off
           - jnp.arange(QB_DIL, dtype=jnp.int32)[None, :, None])
    bias = _bias_lookup(_t5_bucket(rel * r), tab)
    return jnp.where((jnp.abs(rel) <= HALF_WIN)[None], bias, NEG)


def _sgu_kernel(zu_ref, zv_ref, lng_ref, lnb_ref, ws_ref, bs_ref, o_ref):
    u = jax.nn.gelu(zu_ref[...].astype(F32))
    v = jax.nn.gelu(zv_ref[...].astype(F32))
    mu = jnp.mean(v, axis=-1, keepdims=True)
    var = jnp.mean(jnp.square(v - mu), axis=-1, keepdims=True)
    v = ((v - mu) * lax.rsqrt(var + EPS) * lng_ref[...] + lnb_ref[...]).astype(BF16)
    gd = v.shape[1] // C_GROUPS
    for n in range(v.shape[0] // CHUNK):
        rows = slice(n * CHUNK, (n + 1) * CHUNK)
        for g in range(C_GROUPS):
            cols = slice(g * gd, (g + 1) * gd)
            mixed = jnp.dot(ws_ref[g], v[rows, cols], preferred_element_type=F32) + bs_ref[:, cols]
            o_ref[rows, cols] = (u[rows, cols] * mixed).astype(o_ref.dtype)


def _sgu(proj2, ln_g, ln_b, w_s, b_exp):
    T = proj2.shape[0]
    tm = min(TM_SGU, T)
    w = MIX_W
    return pl.pallas_call(
        _sgu_kernel,
        out_shape=jax.ShapeDtypeStruct((T, w), BF16),
        grid=(T // tm,),
        in_specs=[pl.BlockSpec((tm, w), lambda i: (i, COL_ZU)),
                  pl.BlockSpec((tm, w), lambda i: (i, COL_ZU + 1)),
                  pl.BlockSpec((1, w), lambda i: (0, 0)),
                  pl.BlockSpec((1, w), lambda i: (0, 0)),
                  pl.BlockSpec((C_GROUPS, CHUNK, CHUNK), lambda i: (0, 0, 0)),
                  pl.BlockSpec((CHUNK, w), lambda i: (0, 0))],
        out_specs=pl.BlockSpec((tm, w), lambda i: (i, 0)),
        compiler_params=_cparams(("parallel",)),
        name="sgu",
    )(proj2, proj2, ln_g, ln_b, w_s, b_exp)


def _route(logits):
    lane = lax.broadcasted_iota(jnp.int32, logits.shape, 1)
    big = jnp.int32(LANES)
    is_grp = (lane >= N_EXPERTS) & (lane < N_EXPERTS + N_GROUPS)
    gl = jnp.where(is_grp, logits, NEG)
    gmax = jnp.max(gl, axis=-1, keepdims=True)
    g_idx = jnp.min(jnp.where(is_grp & (gl == gmax), lane, big), axis=-1, keepdims=True) - N_EXPERTS
    g_w = 1.0 / jnp.sum(jnp.where(is_grp, jnp.exp(gl - gmax), 0.0), axis=-1, keepdims=True)
    in_grp = (lane >= g_idx * E_PER_GROUP) & (lane < (g_idx + 1) * E_PER_GROUP)
    sel = jnp.where(in_grp, logits, NEG)
    v1 = jnp.max(sel, axis=-1, keepdims=True)
    i1 = jnp.min(jnp.where(in_grp & (sel == v1), lane, big), axis=-1, keepdims=True)
    rest = in_grp & (lane != i1)
    sel2 = jnp.where(rest, logits, NEG)
    v2 = jnp.max(sel2, axis=-1, keepdims=True)
    i2 = jnp.min(jnp.where(rest & (sel2 == v2), lane, big), axis=-1, keepdims=True)
    e2 = jnp.exp(v2 - v1)
    w1 = g_w / (1.0 + e2)
    w2 = g_w * e2 / (1.0 + e2)
    return jnp.where(lane == i1, w1, jnp.where(lane == i2, w2, 0.0))


def _mix_kernel(x_ref, ya_ref, ob0_ref, ob1_ref, ob2_ref, ls0_ref, ls1_ref, ls2_ref, yc_ref,
                g0_ref, g1_ref, g2_ref, wb_ref, wo_ref, nf_ref, wr_ref, br_ref,
                xo_ref, h_ref, comb_ref, xn_scr):
    @pl.when(pl.program_id(0) == 0)
    def _():
        xn_scr[...] = jnp.zeros(xn_scr.shape, F32)

    h = _rms_bf16(xn_scr[...], nf_ref[...])
    h_ref[...] = h
    logits = jnp.dot(h, wr_ref[...], preferred_element_type=F32) + br_ref[...]
    comb_ref[...] = _route(logits)

    ls0, ls1, ls2 = ls0_ref[...], ls1_ref[...], ls2_ref[...]
    mx = jnp.maximum(jnp.maximum(ls0, ls1), ls2)
    es = [jnp.exp(ls0 - mx), jnp.exp(ls1 - mx), jnp.exp(ls2 - mx)]
    inv = 1.0 / (es[0] + es[1] + es[2])
    spread = jnp.where(lax.broadcasted_iota(jnp.int32, (2 * LANES, MIX_W), 1) // DB
                       == lax.broadcasted_iota(jnp.int32, (2 * LANES, MIX_W), 0) % LANES,
                       1.0, 0.0).astype(BF16)
    yb = None
    for e, ob_ref in zip(es, (ob0_ref, ob1_ref, ob2_ref)):
        w = e * inv
        hi = w.astype(BF16)
        lo = (w - hi.astype(F32)).astype(BF16)
        wide = jnp.dot(jnp.concatenate([hi, lo], axis=1), spread, preferred_element_type=F32)
        term = wide * jnp.concatenate([ob_ref[0, j] for j in range(N_SLABS)], axis=-1)
        yb = term if yb is None else yb + term
    yb = yb.astype(BF16)
    merged = jax.nn.sigmoid(g0_ref[...].astype(F32)) * jnp.dot(ya_ref[...], wb_ref[0],
                                                               preferred_element_type=F32)
    merged += jax.nn.sigmoid(g1_ref[...].astype(F32)) * jnp.dot(yb, wb_ref[1],
                                                                preferred_element_type=F32)
    merged += jax.nn.sigmoid(g2_ref[...].astype(F32)) * jnp.dot(yc_ref[...], wb_ref[2],
                                                                preferred_element_type=F32)
    xn = x_ref[...] + jnp.dot(merged.astype(BF16), wo_ref[...], preferred_element_type=F32)
    xo_ref[...] = xn
    xn_scr[...] = xn


def _mix(x2, ya, obs, lses, yc, proj2, wb, wo, nf, wr, br):
    T, D = x2.shape
    S = obs[0].shape[2]
    tm = min(TM_MIX, S)
    per_b = S // tm
    n = T // tm
    w = MIX_W
    cur = lambda i: jnp.minimum(i, n - 1)
    lag = lambda i: jnp.maximum(i - 1, 0)
    row = lambda width: pl.BlockSpec((tm, width), lambda i: (cur(i), 0))
    full = lambda a: pl.BlockSpec(a.shape, lambda i: (0,) * a.ndim, pipeline_mode=pl.Buffered(1))
    gate = lambda k: pl.BlockSpec((tm, D), lambda i: (cur(i), COL_GATE + k))
    slab = pl.BlockSpec((1, N_SLABS, tm, LANES),
                        lambda i: (cur(i) // per_b, 0, cur(i) % per_b, 0))
    late = lambda width: pl.BlockSpec((tm, width), lambda i: (lag(i), 0))
    return pl.pallas_call(
        _mix_kernel,
        out_shape=[jax.ShapeDtypeStruct((T, D), F32), jax.ShapeDtypeStruct((T, D), BF16),
                   jax.ShapeDtypeStruct((T, LANES), F32)],
        grid=(n + 1,),
        in_specs=[row(D), row(w), slab, slab, slab, row(LANES), row(LANES), row(LANES), row(w),
                  gate(0), gate(1), gate(2), full(wb), full(wo), full(nf), full(wr), full(br)],
        out_specs=[row(D), late(D), late(LANES)],
        scratch_shapes=[pltpu.VMEM((tm, D), F32)],
        compiler_params=_cparams(("arbitrary",)),
        name="mix",
    )(x2, ya, obs[0], obs[1], obs[2], lses[0], lses[1], lses[2], yc, proj2, proj2, proj2,
      wb, wo, nf, wr, br)


def _moe_kernel(h_ref, comb_ref, x_ref, wg_ref, wu_ref, wd_ref, nfin_ref, o_ref, acc_scr,
                *, final_norm):
    e = pl.program_id(1)

    @pl.when(e == 0)
    def _():
        acc_scr[...] = jnp.zeros(acc_scr.shape, F32)

    h = h_ref[...]
    lane = lax.broadcasted_iota(jnp.int32, comb_ref.shape, 1)
    c = jnp.sum(jnp.where(lane == e, comb_ref[...], 0.0), axis=-1, keepdims=True)
    hid = (jax.nn.silu(jnp.dot(h, wg_ref[0].astype(BF16), preferred_element_type=F32))
           * jnp.dot(h, wu_ref[0].astype(BF16), preferred_element_type=F32))
    acc_scr[...] += c * jnp.dot(hid.astype(BF16), wd_ref[0].astype(BF16),
                                preferred_element_type=F32)

    @pl.when(e == pl.num_programs(1) - 1)
    def _():
        xn = x_ref[...] + acc_scr[...]
        if final_norm:
            ms = jnp.mean(xn * xn, axis=-1, keepdims=True)
            xn = xn * lax.rsqrt(ms + EPS) * nfin_ref[...]
        o_ref[...] = xn


def _moe_dense(h, comb, x2, wg, wu, wd, e0, nfin, final_norm):
    T, D = x2.shape
    tm = min(TM_MOE, T)
    F = wg.shape[2]
    kern = functools.partial(_moe_kernel, final_norm=final_norm)
    return pl.pallas_call(
        kern,
        out_shape=jax.ShapeDtypeStruct((T, D), F32),
        grid=(T // tm, N_EXPERTS),
        in_specs=[pl.BlockSpec((tm, D), lambda i, e: (i, 0)),
                  pl.BlockSpec((tm, LANES), lambda i, e: (i, 0)),
                  pl.BlockSpec((tm, D), lambda i, e: (i, 0)),
                  pl.BlockSpec((1, D, F), lambda i, e: (e0 + e, 0, 0)),
                  pl.BlockSpec((1, D, F), lambda i, e: (e0 + e, 0, 0)),
                  pl.BlockSpec((1, F, D), lambda i, e: (e0 + e, 0, 0)),
                  pl.BlockSpec((1, D), lambda i, e: (0, 0))],
        out_specs=pl.BlockSpec((tm, D), lambda i, e: (i, 0)),
        scratch_shapes=[pltpu.VMEM((tm, D), F32)],
        compiler_params=_cparams(("parallel", "arbitrary")),
        name="moe_dense",
    )(h, comb, x2, wg, wu, wd, nfin)


def _moe_dispatch_kernel(h_ref, comb_ref, o_ref, cnt_ref):
    nt = o_ref.shape[0]
    tm = h_ref.shape[0] // nt
    before = jnp.where(lax.broadcasted_iota(jnp.int32, (tm, tm), 0)
                       < lax.broadcasted_iota(jnp.int32, (tm, tm), 1), 1.0, 0.0).astype(BF16)
    slot = lax.broadcasted_iota(jnp.int32, (MOE_CAP, tm), 0).astype(F32)
    for u in range(nt):
        rows = slice(u * tm, (u + 1) * tm)
        comb = comb_ref[rows, :]
        hi = comb.astype(BF16).astype(F32)
        wcols = (hi + pltpu.roll(comb - hi, N_EXPERTS, axis=1)).astype(BF16)
        haug = jnp.concatenate([h_ref[rows, :], wcols], axis=1)
        a_t = comb.T[:N_EXPERTS] > 0.0
        a_f = jnp.where(a_t, 1.0, 0.0)
        rank_t = jnp.dot(a_f.astype(BF16), before, preferred_element_type=F32)
        blocks = [jnp.where((slot == rank_t[e:e + 1]) & a_t[e:e + 1], 1.0, 0.0).astype(BF16)
                  for e in range(N_EXPERTS)]
        res = jnp.dot(jnp.concatenate(blocks, axis=0), haug, preferred_element_type=F32)
        res = res.astype(o_ref.dtype)
        for e in range(N_EXPERTS):
            o_ref[u, e] = res[e * MOE_CAP:(e + 1) * MOE_CAP]
        cnt_ref[u] = jnp.broadcast_to(jnp.sum(a_f, axis=1, keepdims=True), cnt_ref.shape[1:])


def _moe_dispatch(h, comb):
    T, D = h.shape
    tm = min(TM_DISP, T)
    n = T // tm
    nt = math.gcd(TILES_PER_STEP, n)
    return pl.pallas_call(
        _moe_dispatch_kernel,
        out_shape=[jax.ShapeDtypeStruct((n, N_EXPERTS, MOE_CAP, D + LANES), BF16),
                   jax.ShapeDtypeStruct((n, N_EXPERTS, LANES), F32)],
        grid=(n // nt,),
        in_specs=[pl.BlockSpec((nt * tm, D), lambda i: (i, 0)),
                  pl.BlockSpec((nt * tm, LANES), lambda i: (i, 0))],
        out_specs=[pl.BlockSpec((nt, N_EXPERTS, MOE_CAP, D + LANES), lambda i: (i, 0, 0, 0)),
                   pl.BlockSpec((nt, N_EXPERTS, LANES), lambda i: (i, 0, 0))],
        compiler_params=_cparams(("parallel",)),
        name="moe_dispatch",
    )(h, comb)


def _moe_ffn_kernel(n16_ref, s_ref, wg_ref, wu_ref, wd_ref, o_ref,
                    wg_scr, wu_scr, wd_scr, lhs_scr, y_scr):
    e, c = pl.program_id(0), pl.program_id(1)

    @pl.when(c == 0)
    def _():
        wg_scr[...] = wg_ref[0].astype(BF16)
        wu_scr[...] = wu_ref[0].astype(BF16)
        wd_scr[...] = wd_ref[0].astype(BF16)

    g, _, cap, _ = s_ref.shape
    D = o_ref.shape[-1]
    @pl.when((e == 0) & (c == 0))
    def _():
        lhs_scr[...] = jnp.zeros(lhs_scr.shape, lhs_scr.dtype)
        y_scr[...] = jnp.zeros(y_scr.shape, y_scr.dtype)

    offs = []
    off = jnp.int32(0)
    for t in range(g):
        offs.append(off)
        lhs_scr[pl.ds(pl.multiple_of(off, BF16_ROWS), cap), :] = s_ref[t, 0]
        off = off + n16_ref[(c * g + t) * N_EXPERTS + e]
    total = off

    def run(nrows):
        rows = lhs_scr[:nrows]
        h = rows[:, :D]
        wparts = rows[:, D:].astype(F32)
        lane = lax.broadcasted_iota(jnp.int32, wparts.shape, 1)
        w = jnp.sum(jnp.where(lane % N_EXPERTS == e, wparts, 0.0), axis=-1, keepdims=True)
        hid = (jax.nn.silu(jnp.dot(h, wg_scr[...], preferred_element_type=F32))
               * jnp.dot(h, wu_scr[...], preferred_element_type=F32))
        y = w * jnp.dot(hid.astype(BF16), wd_scr[...], preferred_element_type=F32)
        y_scr[:nrows] = y.astype(y_scr.dtype)

    classes = tuple(range(g * cap // 2, g * cap + 1, FFN_ROW_STEP))
    lower = 0
    for nrows in classes:
        pl.when((total > lower) & (total <= nrows))(functools.partial(run, nrows))
        lower = nrows

    for t in range(g):
        o_ref[t, 0] = y_scr[pl.ds(pl.multiple_of(offs[t], BF16_ROWS), cap), :]


def _moe_ffn(srt, n16, wg, wu, wd, e0):
    n, ne, cap, wdt = srt.shape
    D, F = wg.shape[1], wg.shape[2]
    g = math.gcd(G_FFN, n)
    return pl.pallas_call(
        _moe_ffn_kernel,
        out_shape=jax.ShapeDtypeStruct((n, ne, cap, D), BF16),
        grid_spec=pltpu.PrefetchScalarGridSpec(
            num_scalar_prefetch=1,
            grid=(ne, n // g),
            in_specs=[pl.BlockSpec((g, 1, cap, wdt), lambda e, c, n16: (c, e, 0, 0)),
                      pl.BlockSpec((1, D, F), lambda e, c, n16: (e0 + e, 0, 0)),
                      pl.BlockSpec((1, D, F), lambda e, c, n16: (e0 + e, 0, 0)),
                      pl.BlockSpec((1, F, D), lambda e, c, n16: (e0 + e, 0, 0))],
            out_specs=pl.BlockSpec((g, 1, cap, D), lambda e, c, n16: (c, e, 0, 0)),
            scratch_shapes=[pltpu.VMEM((D, F), BF16), pltpu.VMEM((D, F), BF16),
                            pltpu.VMEM((F, D), BF16), pltpu.VMEM((g * cap, wdt), BF16),
                            pltpu.VMEM((g * cap, D), BF16)]),
        compiler_params=_cparams(("arbitrary", "arbitrary")),
        name="moe_ffn",
    )(n16, srt, wg, wu, wd)


def _moe_combine_kernel(skip_ref, y_ref, comb_ref, x_ref, nfin_ref, o_ref, *, final_norm):
    nt = y_ref.shape[0]
    tm = x_ref.shape[0] // nt
    ncol = N_EXPERTS * MOE_CAP
    before = jnp.where(lax.broadcasted_iota(jnp.int32, (tm, tm), 1)
                       < lax.broadcasted_iota(jnp.int32, (tm, tm), 0), 1.0, 0.0).astype(BF16)
    spread = jnp.where(lax.broadcasted_iota(jnp.int32, (LANES, ncol), 1) // MOE_CAP
                       == lax.broadcasted_iota(jnp.int32, (LANES, ncol), 0), 1.0, 0.0).astype(BF16)
    slot = (lax.broadcasted_iota(jnp.int32, (tm, ncol), 1) % MOE_CAP).astype(F32)
    for u in range(nt):
        rows = slice(u * tm, (u + 1) * tm)
        a = comb_ref[rows, :] > 0.0
        rank = jnp.dot(before, jnp.where(a, 1.0, 0.0).astype(BF16), preferred_element_type=F32)
        key = jnp.where(a, rank, -1.0).astype(BF16)
        key_all = jnp.dot(key, spread, preferred_element_type=F32)
        pc = jnp.where(slot == key_all, 1.0, 0.0).astype(BF16)
        y = jnp.concatenate([y_ref[u, e] for e in range(N_EXPERTS)], axis=0)
        xn = x_ref[rows, :] + jnp.dot(pc, y, preferred_element_type=F32)
        if final_norm:
            ms = jnp.mean(xn * xn, axis=-1, keepdims=True)
            later = skip_ref[pl.program_id(0) * nt + u] == 1
            xn = jnp.where(later, xn, xn * lax.rsqrt(ms + EPS) * nfin_ref[...])
        o_ref[rows, :] = xn


def _moe_combine(skip, y, comb, x2, nfin, final_norm):
    T, D = x2.shape
    n, ne, cap, _ = y.shape
    tm = T // n
    nt = math.gcd(TILES_PER_STEP, n)
    kern = functools.partial(_moe_combine_kernel, final_norm=final_norm)
    return pl.pallas_call(
        kern,
        out_shape=jax.ShapeDtypeStruct((T, D), F32),
        grid_spec=pltpu.PrefetchScalarGridSpec(
            num_scalar_prefetch=1,
            grid=(n // nt,),
            in_specs=[pl.BlockSpec((nt, ne, cap, D), lambda i, sk: (i, 0, 0, 0)),
                      pl.BlockSpec((nt * tm, LANES), lambda i, sk: (i, 0)),
                      pl.BlockSpec((nt * tm, D), lambda i, sk: (i, 0)),
                      pl.BlockSpec((1, D), lambda i, sk: (0, 0))],
            out_specs=pl.BlockSpec((nt * tm, D), lambda i, sk: (i, 0))),
        compiler_params=_cparams(("parallel",)),
        name="moe_combine",
    )(skip, y, comb, x2, nfin)


def _moe_fix_kernel(tiles_ref, experts_ref, first_ref, last_ref, n_ref, h_ref, comb_ref, prev_ref,
                    wg_ref, wu_ref, wd_ref, nfin_ref, o_ref, *, final_norm):
    del tiles_ref
    s = pl.program_id(0)

    @pl.when(s < n_ref[0])
    def _():
        e = experts_ref[s]
        tm = h_ref.shape[0]
        comb = comb_ref[...]
        a = jnp.where(comb > 0.0, 1.0, 0.0)
        before = (lax.broadcasted_iota(jnp.int32, (tm, tm), 1)
                  < lax.broadcasted_iota(jnp.int32, (tm, tm), 0))
        rank = jnp.dot(jnp.where(before, 1.0, 0.0).astype(BF16), a.astype(BF16),
                       preferred_element_type=F32)
        lane = lax.broadcasted_iota(jnp.int32, comb.shape, 1)
        dropped = (lane == e) & (rank >= MOE_CAP)
        c = jnp.sum(jnp.where(dropped, comb, 0.0), axis=-1, keepdims=True)
        h = h_ref[...]
        hid = (jax.nn.silu(jnp.dot(h, wg_ref[0].astype(BF16), preferred_element_type=F32))
               * jnp.dot(h, wu_ref[0].astype(BF16), preferred_element_type=F32))
        add = c * jnp.dot(hid.astype(BF16), wd_ref[0].astype(BF16), preferred_element_type=F32)
        fresh = first_ref[s] == 1

        @pl.when(fresh)
        def _():
            o_ref[...] = prev_ref[...] + add

        @pl.when(jnp.logical_not(fresh))
        def _():
            o_ref[...] += add

        if final_norm:
            @pl.when(last_ref[s] == 1)
            def _():
                xn = o_ref[...]
                ms = jnp.mean(xn * xn, axis=-1, keepdims=True)
                o_ref[...] = xn * lax.rsqrt(ms + EPS) * nfin_ref[...]


def _moe_fix(tiles, experts, first, last, n, out, h, comb, wg, wu, wd, e0, nfin, final_norm):
    T, D = out.shape
    tm = min(TM_DISP, T)
    F = wg.shape[2]
    tile = lambda width: pl.BlockSpec((tm, width), lambda s, tl, ex, fi, la, n: (tl[s], 0))
    wspec = lambda shape: pl.BlockSpec(shape, lambda s, tl, ex, fi, la, n: (e0 + ex[s], 0, 0))
    kern = functools.partial(_moe_fix_kernel, final_norm=final_norm)
    return pl.pallas_call(
        kern,
        out_shape=jax.ShapeDtypeStruct((T, D), F32),
        grid_spec=pltpu.PrefetchScalarGridSpec(
            num_scalar_prefetch=5,
            grid=(MAX_OVF,),
            in_specs=[tile(D), tile(LANES), tile(D), wspec((1, D, F)), wspec((1, D, F)),
                      wspec((1, F, D)), pl.BlockSpec((1, D), lambda s, tl, ex, fi, la, n: (0, 0))],
            out_specs=tile(D)),
        input_output_aliases={7: 0},
        compiler_params=_cparams(("arbitrary",)),
        name="moe_fix",
    )(tiles, experts, first, last, n, h, comb, out, wg, wu, wd, nfin)


def _moe(h, comb, x2, wg, wu, wd, e0, nfin, final_norm):
    srt, cnt = _moe_dispatch(h, comb)
    over = (cnt[:, :, 0] > MOE_CAP).reshape(-1)
    n_ovf = jnp.sum(over.astype(jnp.int32))
    pairs = jnp.nonzero(over, size=MAX_OVF, fill_value=0)[0].astype(jnp.int32)
    pairs = jnp.where(jnp.arange(MAX_OVF) < n_ovf, pairs, pairs[jnp.clip(n_ovf - 1, 0, MAX_OVF - 1)])
    tiles, experts = pairs // N_EXPERTS, pairs % N_EXPERTS
    change = (tiles[1:] != tiles[:-1]).astype(jnp.int32)
    first = jnp.concatenate([jnp.ones((1,), jnp.int32), change])
    last = jnp.maximum(jnp.concatenate([change, jnp.ones((1,), jnp.int32)]),
                       (jnp.arange(MAX_OVF) == n_ovf - 1).astype(jnp.int32))
    skip = jnp.any(over.reshape(-1, N_EXPERTS), axis=1).astype(jnp.int32)

    used = jnp.minimum(cnt[:, :, 0], MOE_CAP).astype(jnp.int32).reshape(-1)
    n16 = (used + (BF16_ROWS - 1)) // BF16_ROWS * BF16_ROWS

    def routed():
        out = _moe_combine(skip, _moe_ffn(srt, n16, wg, wu, wd, e0), comb, x2, nfin, final_norm)
        return lax.cond(
            n_ovf > 0,
            lambda: _moe_fix(tiles, experts, first, last, n_ovf.reshape(1), out, h, comb,
                             wg, wu, wd, e0, nfin, final_norm),
            lambda: out)

    return lax.cond(n_ovf > MAX_OVF,
                    lambda: _moe_dense(h, comb, x2, wg, wu, wd, e0, nfin, final_norm), routed)


def kernel(x, rel_bias, norm_mix, w_in, diff_lambda, diff_subln, sgu_ln_g, sgu_ln_b, sgu_w, sgu_b,
           w_branch, w_out, norm_ffn, w_router_grp, b_router_grp, w_router_exp, b_router_exp,
           w_gate, w_up, w_down, norm_final):
    B, S, D = x.shape
    T = B * S
    depth = w_in.shape[0]
    a_out = HA * 2 * DA
    grp_w = HB * DB
    b_cols = 3 * NG_B * grp_w
    qkv_b0 = 3 * a_out
    zc0 = qkv_b0 + b_cols
    gate0 = zc0 + 2 * MIX_W
    qk_scale = DA ** -0.5

    bias_a, cfar = _attn_a_bias(rel_bias)
    bias_b = [_attn_b_bias(rel_bias, g) for g in range(NG_B)]

    col = jnp.arange(w_in.shape[2])
    is_q = (col < a_out) | ((col >= qkv_b0) & (col < qkv_b0 + NG_B * grp_w))
    col_scale = jnp.where(is_q, qk_scale * LOG2E, 1.0)

    def group_cols(w, g):
        return [w[:, qkv_b0 + (c * NG_B + g) * grp_w: qkv_b0 + (c * NG_B + g + 1) * grp_w]
                for c in range(3)]

    wg_all = w_gate.reshape((-1,) + w_gate.shape[2:])
    wu_all = w_up.reshape((-1,) + w_up.shape[2:])
    wd_all = w_down.reshape((-1,) + w_down.shape[2:])

    x2 = x.reshape(T, D)
    for i in range(depth):
        w = (w_in[i] * col_scale.astype(F32)).astype(BF16)
        nm = norm_mix[i][None, :]
        w_main = jnp.concatenate([w[:, :2 * a_out], w[:, zc0:]] + group_cols(w, 0), axis=1)
        x3 = x2.reshape(B, S, D)
        proj3, vt = _inproj(x3, nm, w_main, w[:, 2 * a_out:3 * a_out].T)
        proj2 = proj3.reshape(T, proj3.shape[2])

        lam_init = 0.8 - 0.6 * math.exp(-0.3 * i)
        lp = diff_lambda[i].astype(F32)
        lam = jnp.exp(jnp.sum(lp[0] * lp[1])) - jnp.exp(jnp.sum(lp[2] * lp[3])) + lam_init
        ya = _attn_a(proj3, vt, lam.reshape(1), cfar, bias_a, diff_subln[i][None, :], lam_init)

        strided = [g for g in range(NG_B) if DILATIONS[g] > 1]
        permuted = dict(zip(strided, _inproj_perm(
            x3, nm, [jnp.concatenate(group_cols(w, g), axis=1) for g in strided],
            [DILATIONS[g] for g in strided])))
        obs, lses = [], []
        for g in range(NG_B):
            if g in permuted:
                qkv4, cols = permuted[g], (0, 1, 2)
            else:
                qkv4, cols = proj3[:, None], (COL_QKV0, COL_QKV0 + 1, COL_QKV0 + 2)
            o, l = _attn_b(qkv4, bias_b[g], g, cols)
            obs.append(o)
            lses.append(l.reshape(T, LANES))

        b_exp = jnp.repeat(sgu_b[i].T, MIX_W // C_GROUPS, axis=1)
        yc = _sgu(proj2, sgu_ln_g[i][None, :], sgu_ln_b[i][None, :], sgu_w[i].astype(BF16), b_exp)

        wr = jnp.concatenate([w_router_exp[i].transpose(1, 0, 2).reshape(D, N_EXPERTS),
                              w_router_grp[i]], axis=1)
        wr = jnp.pad(wr, ((0, 0), (0, LANES - wr.shape[1]))).astype(BF16)
        br = jnp.concatenate([b_router_exp[i].reshape(N_EXPERTS), b_router_grp[i]])
        br = jnp.pad(br, (0, LANES - br.shape[0]))[None, :].astype(F32)

        x2, h, comb = _mix(x2, ya.reshape(T, a_out), obs, lses, yc, proj2,
                           w_branch[i].astype(BF16), w_out[i].astype(BF16), norm_ffn[i][None, :],
                           wr, br)
        x2 = _moe(h, comb, x2, wg_all, wu_all, wd_all, i * N_EXPERTS, norm_final[None, :],
                  i == depth - 1)
    return x2.reshape(B, S, D)
```

```python
import functools
import math

import jax
import jax.numpy as jnp
from jax import lax
from jax.experimental import pallas as pl
from jax.experimental.pallas import tpu as pltpu

F32 = jnp.float32
BF16 = jnp.bfloat16

EPS = 1e-6
NEG = -1e30
LOG2E = 1.4426950408889634
LN2 = 0.6931471805599453
LANES = 128
HALF_LANES = LANES // 2
VMEM_LIMIT = 48 * 1024 * 1024

HA = 4
DA = 64
MIX_W = 512
WINDOWS = (128, 512, 2048)
DILATIONS = (1, 4, 16)
NG_B = 3
HB = 8
DB = 64
HALF_WIN = 64
CHUNK = 128
C_GROUPS = 4
N_BRANCH = 3
N_BUCKETS = 32
MAX_DIST = 128
N_GROUPS = 4
E_PER_GROUP = 4
N_EXPERTS = N_GROUPS * E_PER_GROUP
N_SLABS = MIX_W // LANES

TM_PROJ = 1024
TN_PROJ = 3328
TM_PERM = 1024
PERM_BLK = 256
T_ATT = 512
QB_DIL = 128
KW_DIL = QB_DIL + 2 * HALF_WIN
ITEMS_DIL = 8
SUBS_DIL = 4
OUT_ROWS_DIL = 2048
TM_SGU = 2048
TM_MIX = 512
TM_MOE = 1024
TM_DISP = 256
MOE_CAP = HALF_LANES
G_FFN = 16
TILES_PER_STEP = 4
FFN_ROW_STEP = 64
BF16_ROWS = 16
MAX_OVF = 64

COL_ZU = 2
COL_GATE = 2
COL_QKV0 = 10


def _cparams(sem):
    return pltpu.CompilerParams(dimension_semantics=sem, vmem_limit_bytes=VMEM_LIMIT)


def _t5_bucket(rel):
    nb = N_BUCKETS // 2
    max_exact = nb // 2
    ret = (rel > 0).astype(jnp.int32) * nb
    n = jnp.abs(rel)
    nf = jnp.maximum(n, 1).astype(F32)
    large = max_exact + (jnp.log(nf / max_exact) / math.log(MAX_DIST / max_exact)
                         * (nb - max_exact)).astype(jnp.int32)
    large = jnp.minimum(large, nb - 1)
    return ret + jnp.where(n < max_exact, n, large)


def _bias_lookup(bucket, tab):
    out = jnp.zeros((tab.shape[1],) + bucket.shape, F32)
    expand = (slice(None),) + (None,) * bucket.ndim
    for b in range(N_BUCKETS):
        out = jnp.where(bucket[None] == b, tab[b][expand], out)
    return out


def _rms_bf16(x, g):
    ms = jnp.mean(x * x, axis=-1, keepdims=True)
    return (x * lax.rsqrt(ms + EPS) * g).astype(BF16)


def _inproj_kernel(x_ref, g_ref, w_ref, wt_ref, o_ref, vt_ref, h_scr):
    @pl.when(pl.program_id(2) == 0)
    def _():
        h = _rms_bf16(x_ref[0], g_ref[...])
        h_scr[...] = h
        res = lax.dot_general(wt_ref[...], h, (((1,), (1,)), ((), ())),
                              preferred_element_type=F32).astype(vt_ref.dtype)
        for hd in range(vt_ref.shape[1]):
            for n in range(vt_ref.shape[2]):
                vt_ref[0, hd, n] = res[hd * LANES:(hd + 1) * LANES, n * T_ATT:(n + 1) * T_ATT]

    o_ref[0] = jnp.dot(h_scr[...], w_ref[...], preferred_element_type=F32).astype(o_ref.dtype)


def _inproj(x3, g, w, wt):
    B, S, D = x3.shape
    N = w.shape[1]
    tm = min(TM_PROJ, S)
    nh, nb = wt.shape[0] // LANES, tm // T_ATT
    return pl.pallas_call(
        _inproj_kernel,
        out_shape=[jax.ShapeDtypeStruct((B, S, N), BF16),
                   jax.ShapeDtypeStruct((B, nh, S // T_ATT, LANES, T_ATT), BF16)],
        grid=(B, S // tm, N // TN_PROJ),
        in_specs=[pl.BlockSpec((1, tm, D), lambda b, i, j: (b, i, 0)),
                  pl.BlockSpec((1, D), lambda b, i, j: (0, 0)),
                  pl.BlockSpec((D, TN_PROJ), lambda b, i, j: (0, j)),
                  pl.BlockSpec(wt.shape, lambda b, i, j: (0, 0))],
        out_specs=[pl.BlockSpec((1, tm, TN_PROJ), lambda b, i, j: (b, i, j)),
                   pl.BlockSpec((1, nh, nb, LANES, T_ATT), lambda b, i, j: (b, 0, i, 0, 0))],
        scratch_shapes=[pltpu.VMEM((tm, D), BF16)],
        compiler_params=_cparams(("parallel", "parallel", "arbitrary")),
        name="inproj",
    )(x3, g, w, wt)


def _inproj_perm_kernel(x_ref, g_ref, *refs, dilations):
    ng = len(dilations)
    p_refs, w_refs, o_refs = refs[:ng], refs[ng:2 * ng], refs[2 * ng:]
    h = _rms_bf16(x_ref[0], g_ref[...])
    nblk = h.shape[0] // PERM_BLK
    for r, p_ref, w_ref, o_ref in zip(dilations, p_refs, w_refs, o_refs):
        hp = jnp.concatenate(
            [jnp.dot(p_ref[...], h[k * PERM_BLK:(k + 1) * PERM_BLK], preferred_element_type=F32)
             for k in range(nblk)], axis=0).astype(BF16)
        res = jnp.dot(hp, w_ref[...], preferred_element_type=F32).astype(o_ref.dtype)
        n = PERM_BLK // r
        for k in range(nblk):
            for s in range(r):
                o_ref[0, s, k * n:(k + 1) * n, :] = res[k * PERM_BLK + s * n:k * PERM_BLK + (s + 1) * n, :]


def _inproj_perm(x3, g, ws, dilations):
    B, S, D = x3.shape
    tm = min(TM_PERM, S)
    perms = []
    for r in dilations:
        n = PERM_BLK // r
        o = jnp.arange(PERM_BLK, dtype=jnp.int32)
        src = (o % n) * r + o // n
        perms.append((src[:, None] == jnp.arange(PERM_BLK, dtype=jnp.int32)[None, :]).astype(BF16))
    kern = functools.partial(_inproj_perm_kernel, dilations=tuple(dilations))
    return pl.pallas_call(
        kern,
        out_shape=[jax.ShapeDtypeStruct((B, r, S // r, w.shape[1]), BF16)
                   for r, w in zip(dilations, ws)],
        grid=(B, S // tm),
        in_specs=([pl.BlockSpec((1, tm, D), lambda b, i: (b, i, 0)),
                   pl.BlockSpec((1, D), lambda b, i: (0, 0))]
                  + [pl.BlockSpec((PERM_BLK, PERM_BLK), lambda b, i: (0, 0)) for _ in dilations]
                  + [pl.BlockSpec(w.shape, lambda b, i: (0, 0)) for w in ws]),
        out_specs=[pl.BlockSpec((1, r, tm // r, w.shape[1]), lambda b, i: (b, 0, i, 0))
                   for r, w in zip(dilations, ws)],
        compiler_params=_cparams(("parallel", "parallel")),
        name="inproj_perm",
    )(x3, g, *perms, *ws)


def _attn_a_kernel(lam_ref, cfar_ref, q_ref, k_ref, vt_ref, bias_ref, g_ref, o_ref,
                   st0_scr, st1_scr, m0_scr, m1_scr, acc_scr, l_scr, *, out_scale, nq, n_blocks):
    k = pl.program_id(0)
    t = T_ATT
    nk = k_ref.shape[1] // t
    n1 = jnp.minimum(k // 2, n_blocks - 1)
    n2 = jnp.maximum(k - 1, 0) // 2
    h1, qi1 = (n1 // nq) % HA, n1 % nq
    h2, qi2 = (n2 // nq) % HA, n2 % nq

    @pl.when(k == 0)
    def _():
        st1_scr[...] = jnp.zeros(st1_scr.shape, F32)
        m1_scr[...] = jnp.zeros(m1_scr.shape, F32)
        acc_scr[...] = jnp.zeros(acc_scr.shape, F32)
        l_scr[...] = jnp.ones(l_scr.shape, F32)

    low_half = lax.broadcasted_iota(jnp.int32, (1, LANES), 1) < HALF_LANES

    def both(cmap, st_w, m_w, st_r, m_r):
        q = q_ref[0]
        zero = jnp.zeros_like(q)
        qc = jnp.where(low_half, q, zero) if cmap == 0 else jnp.where(low_half, zero, q)
        m_prev = m_r[...]
        l = jnp.zeros((1, t), F32)
        acc = jnp.zeros((LANES, t), F32)
        m_new = None
        for j, d in enumerate(range(-1, nk - 1)):
            a2 = lax.rem(qi2 + (d + nk), nk)
            if d <= 1:
                shifted = m_prev
            else:
                shifted = m_prev - cfar_ref[2 * h2 + (a2 > qi2).astype(jnp.int32)]
            p = jnp.exp2(st_r[j] - shifted)
            l = l + jnp.sum(p, axis=0, keepdims=True)
            acc = acc + jnp.dot(vt_ref[0, 0, a2], p.astype(BF16), preferred_element_type=F32)

            a1 = lax.rem(qi1 + (d + nk), nk)
            delta1 = a1 - qi1
            kb = k_ref[0, pl.ds(pl.multiple_of(a1 * t, t), t), :]
            st = lax.dot_general(kb, qc, (((1,), (1,)), ((), ())), preferred_element_type=F32)
            if d <= 1:
                st = st + bias_ref[0, jnp.clip(delta1, -2, 2) + 2]
                cm = jnp.max(st, axis=0, keepdims=True)
            else:
                cm = (jnp.max(st, axis=0, keepdims=True)
                      + cfar_ref[2 * h1 + (delta1 > 0).astype(jnp.int32)])
            st_w[j] = st
            m_new = cm if m_new is None else jnp.maximum(m_new, cm)
        m_w[...] = m_new
        return l, acc

    @pl.when(k % 2 == 0)
    def _():
        l1, acc1 = both(0, st0_scr, m0_scr, st1_scr, m1_scr)
        ot = acc_scr[...] / l_scr[...] - lam_ref[0] * (acc1 / l1)
        o = ot.T
        ms = jnp.mean(o * o, axis=-1, keepdims=True)
        o_ref[0] = (o * lax.rsqrt(ms + EPS) * g_ref[...] * out_scale).astype(o_ref.dtype)

    @pl.when(k % 2 == 1)
    def _():
        l0, acc0 = both(1, st1_scr, m1_scr, st0_scr, m0_scr)
        acc_scr[...] = acc0
        l_scr[...] = l0


def _attn_a(proj3, vt, lam, cfar, bias5, subln_g, lam_init):
    B, S, _ = proj3.shape
    t = T_ATT
    nk = S // t
    n_blocks = B * HA * nk

    def scored(k):
        n = jnp.minimum(k // 2, n_blocks - 1)
        return n // (HA * nk), (n // nk) % HA, n % nk

    def lagged(k, lag):
        n = jnp.maximum(k - lag, 0) // 2
        return n // (HA * nk), (n // nk) % HA, n % nk

    def q_map(k):
        b, h, qi = scored(k)
        return b, qi, h

    def k_map(k):
        b, h, _ = scored(k)
        return b, 0, HA + h

    def vt_map(k):
        b, h, _ = lagged(k, 1)
        return b, h, 0, 0, 0

    def out_map(k):
        b, h, qi = lagged(k, 2)
        return b, qi, h

    kern = functools.partial(_attn_a_kernel, out_scale=1.0 - lam_init, nq=nk, n_blocks=n_blocks)
    return pl.pallas_call(
        kern,
        out_shape=jax.ShapeDtypeStruct((B, S, HA * 2 * DA), BF16),
        grid=(2 * n_blocks + 1,),
        in_specs=[
            pl.BlockSpec(memory_space=pltpu.SMEM),
            pl.BlockSpec(memory_space=pltpu.SMEM),
            pl.BlockSpec((1, t, LANES), q_map),
            pl.BlockSpec((1, S, LANES), k_map),
            pl.BlockSpec((1, 1, nk, LANES, t), vt_map),
            pl.BlockSpec((1, 5, t, t), lambda k: (scored(k)[1], 0, 0, 0)),
            pl.BlockSpec((1, LANES), lambda k: (0, 0)),
        ],
        out_specs=pl.BlockSpec((1, t, LANES), out_map),
        scratch_shapes=[pltpu.VMEM((nk, t, t), F32), pltpu.VMEM((nk, t, t), F32),
                        pltpu.VMEM((1, t), F32), pltpu.VMEM((1, t), F32),
                        pltpu.VMEM((LANES, t), F32), pltpu.VMEM((1, t), F32)],
        compiler_params=_cparams(("arbitrary",)),
        name="diff_attn",
    )(lam, cfar, proj3, proj3, vt, bias5, subln_g)


def _attn_a_bias(rel_bias):
    t = T_ATT
    tab = rel_bias[:, :HA].astype(F32) * LOG2E
    d = jnp.arange(-1, 2, dtype=jnp.int32)[:, None, None] * t
    rel = d + jnp.arange(t, dtype=jnp.int32)[None, :, None] - jnp.arange(t, dtype=jnp.int32)[None, None, :]
    near = _bias_lookup(_t5_bucket(rel), tab)
    far = tab[_t5_bucket(jnp.array([-(t + 1), t + 1], dtype=jnp.int32))].T
    fill = lambda side: jnp.broadcast_to(far[:, side, None, None, None], (HA, 1, t, t))
    tiles = jnp.concatenate([fill(0), near, fill(1)], axis=1)
    return tiles, far.reshape(2 * HA)


def _attn_b_kernel(q_ref, k_ref, v_ref, bias_ref, o_ref, lse_ref, *, sub_len, r, sp, qp):
    nblk = sub_len // QB_DIL
    low_half = lax.broadcasted_iota(jnp.int32, (1, LANES), 1) < HALF_LANES
    for si in range(sp):
        s = si if sp == r else pl.program_id(2) * sp + si
        for qb in range(qp):
            i = pl.program_id(1) * qp + qb
            start = jnp.clip(i * QB_DIL - HALF_WIN, 0, sub_len - KW_DIL)
            start = pl.multiple_of(start, HALF_WIN)
            variant = jnp.where(i == 0, 0, jnp.where(i == nblk - 1, 2, 1))
            q = q_ref[0, si, qb * QB_DIL:(qb + 1) * QB_DIL, :]
            kw = k_ref[0, s, pl.ds(start, KW_DIL), :]
            vw = v_ref[0, s, pl.ds(start, KW_DIL), :]
            rows = (slice(qb * QB_DIL, (qb + 1) * QB_DIL) if r == 1
                    else pl.ds(qb * QB_DIL * r + s, QB_DIL, stride=r))
            lane = lax.broadcasted_iota(jnp.int32, (QB_DIL, LANES), 1)
            lse_tile = jnp.zeros((QB_DIL, LANES), F32)
            scores = []
            for j in range(HB // 2):
                cols = slice(j * LANES, (j + 1) * LANES)
                qpair, kp = q[:, cols], kw[:, cols]
                for c in range(2):
                    qc = jnp.where(low_half if c == 0 else jnp.logical_not(low_half), qpair,
                                   jnp.zeros_like(qpair))
                    sc = lax.dot_general(qc, kp, (((1,), (1,)), ((), ())),
                                         preferred_element_type=F32)
                    scores.append(sc + bias_ref[2 * j + c, variant])
            for j in range(HB // 2):
                cols = slice(j * LANES, (j + 1) * LANES)
                vp = vw[:, cols]
                outs, lses = [], []
                for c in range(2):
                    sc = scores[2 * j + c]
                    m = jnp.max(sc, axis=-1, keepdims=True)
                    p = jnp.exp2(sc - m)
                    l = jnp.sum(p, axis=-1, keepdims=True)
                    outs.append(jnp.dot(p.astype(BF16), vp, preferred_element_type=F32) / l)
                    lses.append((m + jnp.log2(l)) * LN2)
                o_ref[0, j, rows, :] = jnp.where(low_half, outs[0], outs[1])
                for c in range(2):
                    lse_tile = jnp.where(lane == 2 * j + c, lses[c], lse_tile)
            lse_ref[0, rows, :] = lse_tile


def _attn_b(qkv4, bias3, g, cols):
    B, r, L, _ = qkv4.shape
    S = r * L
    width = HB * DB
    nblk = L // QB_DIL
    sp = min(r, SUBS_DIL)
    qp = max(1, min(ITEMS_DIL // sp, OUT_ROWS_DIL // (QB_DIL * r)))
    qcol, kcol, vcol = cols
    kern = functools.partial(_attn_b_kernel, sub_len=L, r=r, sp=sp, qp=qp)
    slab = jax.ShapeDtypeStruct((B, N_SLABS, S, LANES), F32)
    slab_spec = pl.BlockSpec((1, N_SLABS, QB_DIL * r * qp, LANES), lambda b, i, s: (b, 0, i, 0))
    return pl.pallas_call(
        kern,
        out_shape=[slab, jax.ShapeDtypeStruct((B, S, LANES), F32)],
        grid=(B, nblk // qp, r // sp),
        in_specs=[
            pl.BlockSpec((1, sp, QB_DIL * qp, width), lambda b, i, s: (b, s, i, qcol)),
            pl.BlockSpec((1, r, L, width), lambda b, i, s: (b, 0, 0, kcol)),
            pl.BlockSpec((1, r, L, width), lambda b, i, s: (b, 0, 0, vcol)),
            pl.BlockSpec((HB, 3, QB_DIL, KW_DIL), lambda b, i, s: (0, 0, 0, 0)),
        ],
        out_specs=[slab_spec, pl.BlockSpec((1, QB_DIL * r * qp, LANES), lambda b, i, s: (b, i, 0))],
        compiler_params=_cparams(("parallel", "arbitrary", "arbitrary")),
        name=f"dilated_attn_{g}",
    )(qkv4, qkv4, qkv4, bias3)


def _attn_b_bias(rel_bias, g):
    r = DILATIONS[g]
    tab = rel_bias[:, HA + g * HB: HA + (g + 1) * HB].astype(F32) * LOG2E
    off = jnp.arange(3, dtype=jnp.int32)[:, None, None] * HALF_WIN
    rel = (jnp.arange(KW_DIL, dtype=jnp.int32)[None, None, :] - off
           - jnp.arange(QB_DIL, dtype=jnp.int32)[None, :, None])
    bias = _bias_lookup(_t5_bucket(rel * r), tab)
    return jnp.where((jnp.abs(rel) <= HALF_WIN)[None], bias, NEG)


def _sgu_kernel(zu_ref, zv_ref, lng_ref, lnb_ref, ws_ref, bs_ref, o_ref):
    u = jax.nn.gelu(zu_ref[...].astype(F32))
    v = jax.nn.gelu(zv_ref[...].astype(F32))
    mu = jnp.mean(v, axis=-1, keepdims=True)
    var = jnp.mean(jnp.square(v - mu), axis=-1, keepdims=True)
    v = ((v - mu) * lax.rsqrt(var + EPS) * lng_ref[...] + lnb_ref[...]).astype(BF16)
    gd = v.shape[1] // C_GROUPS
    for n in range(v.shape[0] // CHUNK):
        rows = slice(n * CHUNK, (n + 1) * CHUNK)
        for g in range(C_GROUPS):
            cols = slice(g * gd, (g + 1) * gd)
            mixed = jnp.dot(ws_ref[g], v[rows, cols], preferred_element_type=F32) + bs_ref[:, cols]
            o_ref[rows, cols] = (u[rows, cols] * mixed).astype(o_ref.dtype)


def _sgu(proj2, ln_g, ln_b, w_s, b_exp):
    T = proj2.shape[0]
    tm = min(TM_SGU, T)
    w = MIX_W
    return pl.pallas_call(
        _sgu_kernel,
        out_shape=jax.ShapeDtypeStruct((T, w), BF16),
        grid=(T // tm,),
        in_specs=[pl.BlockSpec((tm, w), lambda i: (i, COL_ZU)),
                  pl.BlockSpec((tm, w), lambda i: (i, COL_ZU + 1)),
                  pl.BlockSpec((1, w), lambda i: (0, 0)),
                  pl.BlockSpec((1, w), lambda i: (0, 0)),
                  pl.BlockSpec((C_GROUPS, CHUNK, CHUNK), lambda i: (0, 0, 0)),
                  pl.BlockSpec((CHUNK, w), lambda i: (0, 0))],
        out_specs=pl.BlockSpec((tm, w), lambda i: (i, 0)),
        compiler_params=_cparams(("parallel",)),
        name="sgu",
    )(proj2, proj2, ln_g, ln_b, w_s, b_exp)


def _route(logits):
    lane = lax.broadcasted_iota(jnp.int32, logits.shape, 1)
    big = jnp.int32(LANES)
    is_grp = (lane >= N_EXPERTS) & (lane < N_EXPERTS + N_GROUPS)
    gl = jnp.where(is_grp, logits, NEG)
    gmax = jnp.max(gl, axis=-1, keepdims=True)
    g_idx = jnp.min(jnp.where(is_grp & (gl == gmax), lane, big), axis=-1, keepdims=True) - N_EXPERTS
    g_w = 1.0 / jnp.sum(jnp.where(is_grp, jnp.exp(gl - gmax), 0.0), axis=-1, keepdims=True)
    in_grp = (lane >= g_idx * E_PER_GROUP) & (lane < (g_idx + 1) * E_PER_GROUP)
    sel = jnp.where(in_grp, logits, NEG)
    v1 = jnp.max(sel, axis=-1, keepdims=True)
    i1 = jnp.min(jnp.where(in_grp & (sel == v1), lane, big), axis=-1, keepdims=True)
    rest = in_grp & (lane != i1)
    sel2 = jnp.where(rest, logits, NEG)
    v2 = jnp.max(sel2, axis=-1, keepdims=True)
    i2 = jnp.min(jnp.where(rest & (sel2 == v2), lane, big), axis=-1, keepdims=True)
    e2 = jnp.exp(v2 - v1)
    w1 = g_w / (1.0 + e2)
    w2 = g_w * e2 / (1.0 + e2)
    return jnp.where(lane == i1, w1, jnp.where(lane == i2, w2, 0.0))


def _mix_kernel(x_ref, ya_ref, ob0_ref, ob1_ref, ob2_ref, ls0_ref, ls1_ref, ls2_ref, yc_ref,
                g0_ref, g1_ref, g2_ref, wb_ref, wo_ref, nf_ref, wr_ref, br_ref,
                xo_ref, h_ref, comb_ref, xn_scr):
    @pl.when(pl.program_id(0) == 0)
    def _():
        xn_scr[...] = jnp.zeros(xn_scr.shape, F32)

    h = _rms_bf16(xn_scr[...], nf_ref[...])
    h_ref[...] = h
    logits = jnp.dot(h, wr_ref[...], preferred_element_type=F32) + br_ref[...]
    comb_ref[...] = _route(logits)

    ls0, ls1, ls2 = ls0_ref[...], ls1_ref[...], ls2_ref[...]
    mx = jnp.maximum(jnp.maximum(ls0, ls1), ls2)
    es = [jnp.exp(ls0 - mx), jnp.exp(ls1 - mx), jnp.exp(ls2 - mx)]
    inv = 1.0 / (es[0] + es[1] + es[2])
    spread = jnp.where(lax.broadcasted_iota(jnp.int32, (2 * LANES, MIX_W), 1) // DB
                       == lax.broadcasted_iota(jnp.int32, (2 * LANES, MIX_W), 0) % LANES,
                       1.0, 0.0).astype(BF16)
    yb = None
    for e, ob_ref in zip(es, (ob0_ref, ob1_ref, ob2_ref)):
        w = e * inv
        hi = w.astype(BF16)
        lo = (w - hi.astype(F32)).astype(BF16)
        wide = jnp.dot(jnp.concatenate([hi, lo], axis=1), spread, preferred_element_type=F32)
        term = wide * jnp.concatenate([ob_ref[0, j] for j in range(N_SLABS)], axis=-1)
        yb = term if yb is None else yb + term
    yb = yb.astype(BF16)
    merged = jax.nn.sigmoid(g0_ref[...].astype(F32)) * jnp.dot(ya_ref[...], wb_ref[0],
                                                               preferred_element_type=F32)
    merged += jax.nn.sigmoid(g1_ref[...].astype(F32)) * jnp.dot(yb, wb_ref[1],
                                                                preferred_element_type=F32)
    merged += jax.nn.sigmoid(g2_ref[...].astype(F32)) * jnp.dot(yc_ref[...], wb_ref[2],
                                                                preferred_element_type=F32)
    xn = x_ref[...] + jnp.dot(merged.astype(BF16), wo_ref[...], preferred_element_type=F32)
    xo_ref[...] = xn
    xn_scr[...] = xn


def _mix(x2, ya, obs, lses, yc, proj2, wb, wo, nf, wr, br):
    T, D = x2.shape
    S = obs[0].shape[2]
    tm = min(TM_MIX, S)
    per_b = S // tm
    n = T // tm
    w = MIX_W
    cur = lambda i: jnp.minimum(i, n - 1)
    lag = lambda i: jnp.maximum(i - 1, 0)
    row = lambda width: pl.BlockSpec((tm, width), lambda i: (cur(i), 0))
    full = lambda a: pl.BlockSpec(a.shape, lambda i: (0,) * a.ndim)
    gate = lambda k: pl.BlockSpec((tm, D), lambda i: (cur(i), COL_GATE + k))
    slab = pl.BlockSpec((1, N_SLABS, tm, LANES),
                        lambda i: (cur(i) // per_b, 0, cur(i) % per_b, 0))
    late = lambda width: pl.BlockSpec((tm, width), lambda i: (lag(i), 0))
    return pl.pallas_call(
        _mix_kernel,
        out_shape=[jax.ShapeDtypeStruct((T, D), F32), jax.ShapeDtypeStruct((T, D), BF16),
                   jax.ShapeDtypeStruct((T, LANES), F32)],
        grid=(n + 1,),
        in_specs=[row(D), row(w), slab, slab, slab, row(LANES), row(LANES), row(LANES), row(w),
                  gate(0), gate(1), gate(2), full(wb), full(wo), full(nf), full(wr), full(br)],
        out_specs=[row(D), late(D), late(LANES)],
        scratch_shapes=[pltpu.VMEM((tm, D), F32)],
        compiler_params=_cparams(("arbitrary",)),
        name="mix",
    )(x2, ya, obs[0], obs[1], obs[2], lses[0], lses[1], lses[2], yc, proj2, proj2, proj2,
      wb, wo, nf, wr, br)


def _moe_kernel(h_ref, comb_ref, x_ref, wg_ref, wu_ref, wd_ref, nfin_ref, o_ref, acc_scr,
                *, final_norm):
    e = pl.program_id(1)

    @pl.when(e == 0)
    def _():
        acc_scr[...] = jnp.zeros(acc_scr.shape, F32)

    h = h_ref[...]
    lane = lax.broadcasted_iota(jnp.int32, comb_ref.shape, 1)
    c = jnp.sum(jnp.where(lane == e, comb_ref[...], 0.0), axis=-1, keepdims=True)
    hid = (jax.nn.silu(jnp.dot(h, wg_ref[0].astype(BF16), preferred_element_type=F32))
           * jnp.dot(h, wu_ref[0].astype(BF16), preferred_element_type=F32))
    acc_scr[...] += c * jnp.dot(hid.astype(BF16), wd_ref[0].astype(BF16),
                                preferred_element_type=F32)

    @pl.when(e == pl.num_programs(1) - 1)
    def _():
        xn = x_ref[...] + acc_scr[...]
        if final_norm:
            ms = jnp.mean(xn * xn, axis=-1, keepdims=True)
            xn = xn * lax.rsqrt(ms + EPS) * nfin_ref[...]
        o_ref[...] = xn


def _moe_dense(h, comb, x2, wg, wu, wd, e0, nfin, final_norm):
    T, D = x2.shape
    tm = min(TM_MOE, T)
    F = wg.shape[2]
    kern = functools.partial(_moe_kernel, final_norm=final_norm)
    return pl.pallas_call(
        kern,
        out_shape=jax.ShapeDtypeStruct((T, D), F32),
        grid=(T // tm, N_EXPERTS),
        in_specs=[pl.BlockSpec((tm, D), lambda i, e: (i, 0)),
                  pl.BlockSpec((tm, LANES), lambda i, e: (i, 0)),
                  pl.BlockSpec((tm, D), lambda i, e: (i, 0)),
                  pl.BlockSpec((1, D, F), lambda i, e: (e0 + e, 0, 0)),
                  pl.BlockSpec((1, D, F), lambda i, e: (e0 + e, 0, 0)),
                  pl.BlockSpec((1, F, D), lambda i, e: (e0 + e, 0, 0)),
                  pl.BlockSpec((1, D), lambda i, e: (0, 0))],
        out_specs=pl.BlockSpec((tm, D), lambda i, e: (i, 0)),
        scratch_shapes=[pltpu.VMEM((tm, D), F32)],
        compiler_params=_cparams(("parallel", "arbitrary")),
        name="moe_dense",
    )(h, comb, x2, wg, wu, wd, nfin)


def _moe_dispatch_kernel(h_ref, comb_ref, o_ref, cnt_ref):
    nt = o_ref.shape[0]
    tm = h_ref.shape[0] // nt
    before = jnp.where(lax.broadcasted_iota(jnp.int32, (tm, tm), 0)
                       < lax.broadcasted_iota(jnp.int32, (tm, tm), 1), 1.0, 0.0).astype(BF16)
    slot = lax.broadcasted_iota(jnp.int32, (MOE_CAP, tm), 0).astype(F32)
    staged = []
    for u in range(nt):
        rows = slice(u * tm, (u + 1) * tm)
        comb = comb_ref[rows, :]
        hi = comb.astype(BF16).astype(F32)
        wcols = (hi + pltpu.roll(comb - hi, N_EXPERTS, axis=1)).astype(BF16)
        haug = jnp.concatenate([h_ref[rows, :], wcols], axis=1)
        a_t = comb.T[:N_EXPERTS] > 0.0
        a_f = jnp.where(a_t, 1.0, 0.0)
        rank_t = jnp.dot(a_f.astype(BF16), before, preferred_element_type=F32)
        blocks = [jnp.where((slot == rank_t[e:e + 1]) & a_t[e:e + 1], 1.0, 0.0).astype(BF16)
                  for e in range(N_EXPERTS)]
        staged.append((jnp.concatenate(blocks, axis=0), haug, a_f))
    for u, (onehot, haug, a_f) in enumerate(staged):
        res = jnp.dot(onehot, haug, preferred_element_type=F32).astype(o_ref.dtype)
        for e in range(N_EXPERTS):
            o_ref[u, e] = res[e * MOE_CAP:(e + 1) * MOE_CAP]
        cnt_ref[u] = jnp.broadcast_to(jnp.sum(a_f, axis=1, keepdims=True), cnt_ref.shape[1:])


def _moe_dispatch(h, comb):
    T, D = h.shape
    tm = min(TM_DISP, T)
    n = T // tm
    nt = math.gcd(TILES_PER_STEP, n)
    return pl.pallas_call(
        _moe_dispatch_kernel,
        out_shape=[jax.ShapeDtypeStruct((n, N_EXPERTS, MOE_CAP, D + LANES), BF16),
                   jax.ShapeDtypeStruct((n, N_EXPERTS, LANES), F32)],
        grid=(n // nt,),
        in_specs=[pl.BlockSpec((nt * tm, D), lambda i: (i, 0)),
                  pl.BlockSpec((nt * tm, LANES), lambda i: (i, 0))],
        out_specs=[pl.BlockSpec((nt, N_EXPERTS, MOE_CAP, D + LANES), lambda i: (i, 0, 0, 0)),
                   pl.BlockSpec((nt, N_EXPERTS, LANES), lambda i: (i, 0, 0))],
        compiler_params=_cparams(("parallel",)),
        name="moe_dispatch",
    )(h, comb)


def _moe_ffn_kernel(n16_ref, s_ref, wg_ref, wu_ref, wd_ref, o_ref,
                    wg_scr, wu_scr, wd_scr, lhs_scr, y_scr):
    e, c = pl.program_id(0), pl.program_id(1)

    @pl.when(c == 0)
    def _():
        wg_scr[...] = wg_ref[0].astype(BF16)
        wu_scr[...] = wu_ref[0].astype(BF16)
        wd_scr[...] = wd_ref[0].astype(BF16)

    g, _, cap, _ = s_ref.shape
    D = o_ref.shape[-1]
    @pl.when((e == 0) & (c == 0))
    def _():
        lhs_scr[...] = jnp.zeros(lhs_scr.shape, lhs_scr.dtype)
        y_scr[...] = jnp.zeros(y_scr.shape, y_scr.dtype)

    offs = []
    off = jnp.int32(0)
    for t in range(g):
        offs.append(off)
        lhs_scr[pl.ds(pl.multiple_of(off, BF16_ROWS), cap), :] = s_ref[t, 0]
        off = off + n16_ref[(c * g + t) * N_EXPERTS + e]
    total = off

    def run(nrows):
        rows = lhs_scr[:nrows]
        h = rows[:, :D]
        wparts = rows[:, D:].astype(F32)
        lane = lax.broadcasted_iota(jnp.int32, wparts.shape, 1)
        w = jnp.sum(jnp.where(lane % N_EXPERTS == e, wparts, 0.0), axis=-1, keepdims=True)
        hid = (jax.nn.silu(jnp.dot(h, wg_scr[...], preferred_element_type=F32))
               * jnp.dot(h, wu_scr[...], preferred_element_type=F32))
        y = w * jnp.dot(hid.astype(BF16), wd_scr[...], preferred_element_type=F32)
        y_scr[:nrows] = y.astype(y_scr.dtype)

    classes = tuple(range(g * cap // 2, g * cap + 1, FFN_ROW_STEP))
    lower = 0
    for nrows in classes:
        pl.when((total > lower) & (total <= nrows))(functools.partial(run, nrows))
        lower = nrows

    for t in range(g):
        o_ref[t, 0] = y_scr[pl.ds(pl.multiple_of(offs[t], BF16_ROWS), cap), :]


def _moe_ffn(srt, n16, wg, wu, wd, e0):
    n, ne, cap, wdt = srt.shape
    D, F = wg.shape[1], wg.shape[2]
    g = math.gcd(G_FFN, n)
    return pl.pallas_call(
        _moe_ffn_kernel,
        out_shape=jax.ShapeDtypeStruct((n, ne, cap, D), BF16),
        grid_spec=pltpu.PrefetchScalarGridSpec(
            num_scalar_prefetch=1,
            grid=(ne, n // g),
            in_specs=[pl.BlockSpec((g, 1, cap, wdt), lambda e, c, n16: (c, e, 0, 0)),
                      pl.BlockSpec((1, D, F), lambda e, c, n16: (e0 + e, 0, 0)),
                      pl.BlockSpec((1, D, F), lambda e, c, n16: (e0 + e, 0, 0)),
                      pl.BlockSpec((1, F, D), lambda e, c, n16: (e0 + e, 0, 0))],
            out_specs=pl.BlockSpec((g, 1, cap, D), lambda e, c, n16: (c, e, 0, 0)),
            scratch_shapes=[pltpu.VMEM((D, F), BF16), pltpu.VMEM((D, F), BF16),
                            pltpu.VMEM((F, D), BF16), pltpu.VMEM((g * cap, wdt), BF16),
                            pltpu.VMEM((g * cap, D), BF16)]),
        compiler_params=_cparams(("arbitrary", "arbitrary")),
        name="moe_ffn",
    )(n16, srt, wg, wu, wd)


def _moe_combine_kernel(skip_ref, y_ref, comb_ref, x_ref, nfin_ref, o_ref, *, final_norm):
    nt = y_ref.shape[0]
    tm = x_ref.shape[0] // nt
    ncol = N_EXPERTS * MOE_CAP
    before = jnp.where(lax.broadcasted_iota(jnp.int32, (tm, tm), 1)
                       < lax.broadcasted_iota(jnp.int32, (tm, tm), 0), 1.0, 0.0).astype(BF16)
    spread = jnp.where(lax.broadcasted_iota(jnp.int32, (LANES, ncol), 1) // MOE_CAP
                       == lax.broadcasted_iota(jnp.int32, (LANES, ncol), 0), 1.0, 0.0).astype(BF16)
    slot = (lax.broadcasted_iota(jnp.int32, (tm, ncol), 1) % MOE_CAP).astype(F32)
    for u in range(nt):
        rows = slice(u * tm, (u + 1) * tm)
        a = comb_ref[rows, :] > 0.0
        rank = jnp.dot(before, jnp.where(a, 1.0, 0.0).astype(BF16), preferred_element_type=F32)
        key = jnp.where(a, rank, -1.0).astype(BF16)
        key_all = jnp.dot(key, spread, preferred_element_type=F32)
        pc = jnp.where(slot == key_all, 1.0, 0.0).astype(BF16)
        y = jnp.concatenate([y_ref[u, e] for e in range(N_EXPERTS)], axis=0)
        xn = x_ref[rows, :] + jnp.dot(pc, y, preferred_element_type=F32)
        if final_norm:
            ms = jnp.mean(xn * xn, axis=-1, keepdims=True)
            later = skip_ref[pl.program_id(0) * nt + u] == 1
            xn = jnp.where(later, xn, xn * lax.rsqrt(ms + EPS) * nfin_ref[...])
        o_ref[rows, :] = xn


def _moe_combine(skip, y, comb, x2, nfin, final_norm):
    T, D = x2.shape
    n, ne, cap, _ = y.shape
    tm = T // n
    nt = math.gcd(TILES_PER_STEP, n)
    kern = functools.partial(_moe_combine_kernel, final_norm=final_norm)
    return pl.pallas_call(
        kern,
        out_shape=jax.ShapeDtypeStruct((T, D), F32),
        grid_spec=pltpu.PrefetchScalarGridSpec(
            num_scalar_prefetch=1,
            grid=(n // nt,),
            in_specs=[pl.BlockSpec((nt, ne, cap, D), lambda i, sk: (i, 0, 0, 0)),
                      pl.BlockSpec((nt * tm, LANES), lambda i, sk: (i, 0)),
                      pl.BlockSpec((nt * tm, D), lambda i, sk: (i, 0)),
                      pl.BlockSpec((1, D), lambda i, sk: (0, 0))],
            out_specs=pl.BlockSpec((nt * tm, D), lambda i, sk: (i, 0))),
        compiler_params=_cparams(("parallel",)),
        name="moe_combine",
    )(skip, y, comb, x2, nfin)


def _moe_fix_kernel(tiles_ref, experts_ref, first_ref, last_ref, n_ref, h_ref, comb_ref, prev_ref,
                    wg_ref, wu_ref, wd_ref, nfin_ref, o_ref, *, final_norm):
    del tiles_ref
    s = pl.program_id(0)

    @pl.when(s < n_ref[0])
    def _():
        e = experts_ref[s]
        tm = h_ref.shape[0]
        comb = comb_ref[...]
        a = jnp.where(comb > 0.0, 1.0, 0.0)
        before = (lax.broadcasted_iota(jnp.int32, (tm, tm), 1)
                  < lax.broadcasted_iota(jnp.int32, (tm, tm), 0))
        rank = jnp.dot(jnp.where(before, 1.0, 0.0).astype(BF16), a.astype(BF16),
                       preferred_element_type=F32)
        lane = lax.broadcasted_iota(jnp.int32, comb.shape, 1)
        dropped = (lane == e) & (rank >= MOE_CAP)
        c = jnp.sum(jnp.where(dropped, comb, 0.0), axis=-1, keepdims=True)
        h = h_ref[...]
        hid = (jax.nn.silu(jnp.dot(h, wg_ref[0].astype(BF16), preferred_element_type=F32))
               * jnp.dot(h, wu_ref[0].astype(BF16), preferred_element_type=F32))
        add = c * jnp.dot(hid.astype(BF16), wd_ref[0].astype(BF16), preferred_element_type=F32)
        fresh = first_ref[s] == 1

        @pl.when(fresh)
        def _():
            o_ref[...] = prev_ref[...] + add

        @pl.when(jnp.logical_not(fresh))
        def _():
            o_ref[...] += add

        if final_norm:
            @pl.when(last_ref[s] == 1)
            def _():
                xn = o_ref[...]
                ms = jnp.mean(xn * xn, axis=-1, keepdims=True)
                o_ref[...] = xn * lax.rsqrt(ms + EPS) * nfin_ref[...]


def _moe_fix(tiles, experts, first, last, n, out, h, comb, wg, wu, wd, e0, nfin, final_norm):
    T, D = out.shape
    tm = min(TM_DISP, T)
    F = wg.shape[2]
    tile = lambda width: pl.BlockSpec((tm, width), lambda s, tl, ex, fi, la, n: (tl[s], 0))
    wspec = lambda shape: pl.BlockSpec(shape, lambda s, tl, ex, fi, la, n: (e0 + ex[s], 0, 0))
    kern = functools.partial(_moe_fix_kernel, final_norm=final_norm)
    return pl.pallas_call(
        kern,
        out_shape=jax.ShapeDtypeStruct((T, D), F32),
        grid_spec=pltpu.PrefetchScalarGridSpec(
            num_scalar_prefetch=5,
            grid=(MAX_OVF,),
            in_specs=[tile(D), tile(LANES), tile(D), wspec((1, D, F)), wspec((1, D, F)),
                      wspec((1, F, D)), pl.BlockSpec((1, D), lambda s, tl, ex, fi, la, n: (0, 0))],
            out_specs=tile(D)),
        input_output_aliases={7: 0},
        compiler_params=_cparams(("arbitrary",)),
        name="moe_fix",
    )(tiles, experts, first, last, n, h, comb, out, wg, wu, wd, nfin)


def _moe(h, comb, x2, wg, wu, wd, e0, nfin, final_norm):
    srt, cnt = _moe_dispatch(h, comb)
    over = (cnt[:, :, 0] > MOE_CAP).reshape(-1)
    n_ovf = jnp.sum(over.astype(jnp.int32))
    pairs = jnp.nonzero(over, size=MAX_OVF, fill_value=0)[0].astype(jnp.int32)
    pairs = jnp.where(jnp.arange(MAX_OVF) < n_ovf, pairs, pairs[jnp.clip(n_ovf - 1, 0, MAX_OVF - 1)])
    tiles, experts = pairs // N_EXPERTS, pairs % N_EXPERTS
    change = (tiles[1:] != tiles[:-1]).astype(jnp.int32)
    first = jnp.concatenate([jnp.ones((1,), jnp.int32), change])
    last = jnp.maximum(jnp.concatenate([change, jnp.ones((1,), jnp.int32)]),
                       (jnp.arange(MAX_OVF) == n_ovf - 1).astype(jnp.int32))
    skip = jnp.any(over.reshape(-1, N_EXPERTS), axis=1).astype(jnp.int32)

    used = jnp.minimum(cnt[:, :, 0], MOE_CAP).astype(jnp.int32).reshape(-1)
    n16 = (used + (BF16_ROWS - 1)) // BF16_ROWS * BF16_ROWS

    def routed():
        out = _moe_combine(skip, _moe_ffn(srt, n16, wg, wu, wd, e0), comb, x2, nfin, final_norm)
        return lax.cond(
            n_ovf > 0,
            lambda: _moe_fix(tiles, experts, first, last, n_ovf.reshape(1), out, h, comb,
                             wg, wu, wd, e0, nfin, final_norm),
            lambda: out)

    return lax.cond(n_ovf > MAX_OVF,
                    lambda: _moe_dense(h, comb, x2, wg, wu, wd, e0, nfin, final_norm), routed)


def kernel(x, rel_bias, norm_mix, w_in, diff_lambda, diff_subln, sgu_ln_g, sgu_ln_b, sgu_w, sgu_b,
           w_branch, w_out, norm_ffn, w_router_grp, b_router_grp, w_router_exp, b_router_exp,
           w_gate, w_up, w_down, norm_final):
    B, S, D = x.shape
    T = B * S
    depth = w_in.shape[0]
    a_out = HA * 2 * DA
    grp_w = HB * DB
    b_cols = 3 * NG_B * grp_w
    qkv_b0 = 3 * a_out
    zc0 = qkv_b0 + b_cols
    gate0 = zc0 + 2 * MIX_W
    qk_scale = DA ** -0.5

    bias_a, cfar = _attn_a_bias(rel_bias)
    bias_b = [_attn_b_bias(rel_bias, g) for g in range(NG_B)]

    col = jnp.arange(w_in.shape[2])
    is_q = (col < a_out) | ((col >= qkv_b0) & (col < qkv_b0 + NG_B * grp_w))
    col_scale = jnp.where(is_q, qk_scale * LOG2E, 1.0)

    def group_cols(w, g):
        return [w[:, qkv_b0 + (c * NG_B + g) * grp_w: qkv_b0 + (c * NG_B + g + 1) * grp_w]
                for c in range(3)]

    wg_all = w_gate.reshape((-1,) + w_gate.shape[2:])
    wu_all = w_up.reshape((-1,) + w_up.shape[2:])
    wd_all = w_down.reshape((-1,) + w_down.shape[2:])

    x2 = x.reshape(T, D)
    for i in range(depth):
        w = (w_in[i] * col_scale.astype(F32)).astype(BF16)
        nm = norm_mix[i][None, :]
        w_main = jnp.concatenate([w[:, :2 * a_out], w[:, zc0:]] + group_cols(w, 0), axis=1)
        x3 = x2.reshape(B, S, D)
        proj3, vt = _inproj(x3, nm, w_main, w[:, 2 * a_out:3 * a_out].T)
        proj2 = proj3.reshape(T, proj3.shape[2])

        lam_init = 0.8 - 0.6 * math.exp(-0.3 * i)
        lp = diff_lambda[i].astype(F32)
        lam = jnp.exp(jnp.sum(lp[0] * lp[1])) - jnp.exp(jnp.sum(lp[2] * lp[3])) + lam_init
        ya = _attn_a(proj3, vt, lam.reshape(1), cfar, bias_a, diff_subln[i][None, :], lam_init)

        strided = [g for g in range(NG_B) if DILATIONS[g] > 1]
        permuted = dict(zip(strided, _inproj_perm(
            x3, nm, [jnp.concatenate(group_cols(w, g), axis=1) for g in strided],
            [DILATIONS[g] for g in strided])))
        obs, lses = [], []
        for g in range(NG_B):
            if g in permuted:
                qkv4, cols = permuted[g], (0, 1, 2)
            else:
                qkv4, cols = proj3[:, None], (COL_QKV0, COL_QKV0 + 1, COL_QKV0 + 2)
            o, l = _attn_b(qkv4, bias_b[g], g, cols)
            obs.append(o)
            lses.append(l.reshape(T, LANES))

        b_exp = jnp.repeat(sgu_b[i].T, MIX_W // C_GROUPS, axis=1)
        yc = _sgu(proj2, sgu_ln_g[i][None, :], sgu_ln_b[i][None, :], sgu_w[i].astype(BF16), b_exp)

        wr = jnp.concatenate([w_router_exp[i].transpose(1, 0, 2).reshape(D, N_EXPERTS),
                              w_router_grp[i]], axis=1)
        wr = jnp.pad(wr, ((0, 0), (0, LANES - wr.shape[1]))).astype(BF16)
        br = jnp.concatenate([b_router_exp[i].reshape(N_EXPERTS), b_router_grp[i]])
        br = jnp.pad(br, (0, LANES - br.shape[0]))[None, :].astype(F32)

        x2, h, comb = _mix(x2, ya.reshape(T, a_out), obs, lses, yc, proj2,
                           w_branch[i].astype(BF16), w_out[i].astype(BF16), norm_ffn[i][None, :],
                           wr, br)
        x2 = _moe(h, comb, x2, wg_all, wu_all, wd_all, i * N_EXPERTS, norm_final[None, :],
                  i == depth - 1)
    return x2.reshape(B, S, D)
```
